```python
import math
import jax, jax.numpy as jnp
from jax import lax
import numpy as np

D_MODEL = 1024
BATCH = 8
SEQ = 4096
DEPTH = 2

F32 = jnp.float32
EPS = 1e-6
NEG_INF = -1e30
ROPE_THETA = 10000.0
BLOCK = 128
HEAD_DIM = 64

SWA_HEADS = 8
SWA_KV_HEADS = 2
SWA_WINDOW = 128
S5_CHANNELS = 512
S5_GROUP = 16
S5_GROUPS = S5_CHANNELS // S5_GROUP
S5_STATE = 64
S5_DT_MIN = 1e-3
S5_DT_MAX = 1e-1
RET_HEADS = 4
RET_QK_DIM = 64
RET_V_DIM = 128
RET_CHUNK = 128
MLA_HEADS = 8
MLA_Q_RANK = 256
MLA_KV_RANK = 128
MLA_NOPE_DIM = 64
MLA_ROPE_DIM = 32
MLA_V_DIM = 64
N_BRANCH = 4
BRANCH_WIDTH = 512
D_FF = 4 * D_MODEL

SWA_Q_W = SWA_HEADS * HEAD_DIM
SWA_KV_W = SWA_KV_HEADS * HEAD_DIM
RET_QK_W = RET_HEADS * RET_QK_DIM
RET_V_W = RET_HEADS * RET_V_DIM
IN_SPLITS = (SWA_Q_W, SWA_KV_W, SWA_KV_W,
             S5_CHANNELS,
             RET_QK_W, RET_QK_W, RET_V_W, RET_V_W,
             MLA_Q_RANK, MLA_KV_RANK, MLA_ROPE_DIM,
             N_BRANCH * D_MODEL)
D_IN = sum(IN_SPLITS)

kernel_name = "hybrid_gated_swa_s5_retnet_mla_block"


def rms_norm(x, g):
    xf = x.astype(F32)
    y = xf * lax.rsqrt(jnp.mean(xf * xf, axis=-1, keepdims=True) + EPS)
    return (y * g.astype(F32)).astype(x.dtype)


def rope_tables(seq, dim):
    inv = 1.0 / (ROPE_THETA ** (jnp.arange(0, dim, 2, dtype=F32) / dim))
    ang = jnp.arange(seq, dtype=F32)[:, None] * inv[None, :]
    return jnp.cos(ang), jnp.sin(ang)


def apply_rope(x, cos, sin):
    xf = x.astype(F32)
    x1, x2 = jnp.split(xf, 2, axis=-1)
    c = cos[None, :, None, :]
    s = sin[None, :, None, :]
    return jnp.concatenate([x1 * c - x2 * s, x2 * c + x1 * s], axis=-1).astype(x.dtype)


def swa_attention(q, k, v, sinks):
    b, s_len, h, d = q.shape
    nb = s_len // BLOCK
    grp = h // SWA_KV_HEADS
    qb = q.reshape(b, nb, BLOCK, SWA_KV_HEADS, grp, d)

    def with_prev(t):
        tb = t.reshape(b, nb, BLOCK, SWA_KV_HEADS, d)
        prev = jnp.pad(tb[:, :-1], ((0, 0), (1, 0), (0, 0), (0, 0), (0, 0)))
        return jnp.concatenate([prev, tb], axis=2)

    kb, vb = with_prev(k), with_prev(v)
    sc = jnp.einsum('bnqkgd,bnckd->bnkgqc', qb, kb, preferred_element_type=F32) * (d ** -0.5)
    qi = jnp.arange(BLOCK)[:, None] + BLOCK
    kj = jnp.arange(2 * BLOCK)[None, :]
    diff = qi - kj
    band = (diff >= 0) & (diff < SWA_WINDOW)
    has_prev = (jnp.arange(nb) > 0)[:, None, None] | (kj >= BLOCK)[None]
    mask = (band[None] & has_prev)[None, :, None, None]
    sc = jnp.where(mask, sc, NEG_INF)
    sink = sinks.astype(F32).reshape(1, 1, SWA_KV_HEADS, grp, 1, 1)
    m = jnp.maximum(jnp.max(sc, axis=-1, keepdims=True), sink)
    p = jnp.exp(sc - m)
    denom = jnp.sum(p, axis=-1, keepdims=True) + jnp.exp(sink - m)
    o = jnp.einsum('bnkgqc,bnckd->bnqkgd', (p / denom).astype(v.dtype), vb)
    return o.reshape(b, s_len, h * d)


def s5_ssm(u, lam_re, lam_im, log_dt, b_re, b_im, c_re, c_im, d_skip, w_glu):
    b, s_len, _ = u.shape
    uf = u.astype(F32).reshape(b, s_len, S5_GROUPS, S5_GROUP)
    dt = jnp.exp(log_dt.astype(F32))[:, None]
    lr, li = lam_re.astype(F32), lam_im.astype(F32)
    mag = jnp.exp(lr * dt)
    ab_re, ab_im = mag * jnp.cos(li * dt), mag * jnp.sin(li * dt)
    den = lr * lr + li * li
    nr, ni = ab_re - 1.0, ab_im
    f_re = (nr * lr + ni * li) / den
    f_im = (ni * lr - nr * li) / den
    br, bi = b_re.astype(F32), b_im.astype(F32)
    bb_re = f_re[..., None] * br - f_im[..., None] * bi
    bb_im = f_re[..., None] * bi + f_im[..., None] * br
    bu_re = jnp.einsum('bsgc,gpc->bsgp', uf, bb_re)
    bu_im = jnp.einsum('bsgc,gpc->bsgp', uf, bb_im)
    a_re = jnp.broadcast_to(ab_re, bu_re.shape)
    a_im = jnp.broadcast_to(ab_im, bu_im.shape)

    def combine(left, right):
        a1r, a1i, b1r, b1i = left
        a2r, a2i, b2r, b2i = right
        return (a1r * a2r - a1i * a2i, a1r * a2i + a1i * a2r,
                a2r * b1r - a2i * b1i + b2r, a2r * b1i + a2i * b1r + b2i)

    _, _, x_re, x_im = lax.associative_scan(combine, (a_re, a_im, bu_re, bu_im), axis=1)
    y = (jnp.einsum('bsgp,gcp->bsgc', x_re, c_re.astype(F32))
         - jnp.einsum('bsgp,gcp->bsgc', x_im, c_im.astype(F32)))
    y = y.reshape(b, s_len, S5_CHANNELS) + d_skip.astype(F32) * u.astype(F32)
    z = jax.nn.gelu(y).astype(u.dtype)
    za, zb = jnp.split(z @ w_glu, 2, axis=-1)
    return za * jax.nn.sigmoid(zb)


def retention(q, k, v, g, cos, sin):
    b, s_len, h, dk = q.shape
    dv = v.shape[-1]
    c = RET_CHUNK
    nc = s_len // c
    qf = apply_rope(q, cos, sin).astype(F32)
    kf = apply_rope(k, cos, sin).astype(F32) * (dk ** -0.5)
    qc = qf.reshape(b, nc, c, h, dk)
    kc = kf.reshape(b, nc, c, h, dk)
    vc = v.astype(F32).reshape(b, nc, c, h, dv)
    log_gamma = jnp.log1p(-jnp.exp2(-5.0 - jnp.arange(h, dtype=F32)))
    idx = jnp.arange(c, dtype=F32)
    diff = idx[:, None] - idx[None, :]
    decay = jnp.where(diff >= 0, jnp.exp(log_gamma[:, None, None] * jnp.maximum(diff, 0.0)), 0.0)
    inner_s = jnp.einsum('bnqhd,bnkhd->bnhqk', qc, kc) * decay
    inner = jnp.einsum('bnhqk,bnkhe->bnqhe', inner_s, vc)
    k_w = jnp.exp(log_gamma[None, :] * (c - 1 - idx)[:, None])
    upd = jnp.einsum('bnkhd,bnkhe->bnhde', kc * k_w[None, None, :, :, None], vc)
    chunk_decay = jnp.exp(log_gamma * c)[None, :, None, None]

    def step(state, u_n):
        return state * chunk_decay + u_n, state

    _, prev = lax.scan(step, jnp.zeros((b, h, dk, dv), F32), jnp.moveaxis(upd, 1, 0))
    prev = jnp.moveaxis(prev, 0, 1)
    q_w = jnp.exp(log_gamma[None, :] * (idx + 1.0)[:, None])
    cross = jnp.einsum('bnqhd,bnhde->bnqhe', qc * q_w[None, None, :, :, None], prev)
    y = (inner + cross).reshape(b, s_len, h, dv)
    mu = jnp.mean(y, axis=-1, keepdims=True)
    var = jnp.mean(jnp.square(y - mu), axis=-1, keepdims=True)
    y = ((y - mu) * lax.rsqrt(var + EPS)).reshape(b, s_len, h * dv)
    return (jax.nn.silu(g.astype(F32)) * y).astype(v.dtype)


def mla_attention(c_q, c_kv, k_rope, g_q, g_kv, w_uq, w_ukv, cos, sin):
    b, s_len, _ = c_q.shape
    h = MLA_HEADS
    q = (rms_norm(c_q, g_q) @ w_uq).reshape(b, s_len, h, MLA_NOPE_DIM + MLA_ROPE_DIM)
    q_nope = q[..., :MLA_NOPE_DIM]
    q_rope = apply_rope(q[..., MLA_NOPE_DIM:], cos, sin)
    kv = (rms_norm(c_kv, g_kv) @ w_ukv).reshape(b, s_len, h, MLA_NOPE_DIM + MLA_V_DIM)
    k_nope, v = kv[..., :MLA_NOPE_DIM], kv[..., MLA_NOPE_DIM:]
    k_r = apply_rope(k_rope[:, :, None, :], cos, sin)[:, :, 0]
    scale = (MLA_NOPE_DIM + MLA_ROPE_DIM) ** -0.5
    nb = s_len // BLOCK
    qn_b = jnp.moveaxis(q_nope.reshape(b, nb, BLOCK, h, MLA_NOPE_DIM), 1, 0)
    qr_b = jnp.moveaxis(q_rope.reshape(b, nb, BLOCK, h, MLA_ROPE_DIM), 1, 0)
    kpos = jnp.arange(s_len)

    def one_block(args):
        qn, qr, i = args
        sc = (jnp.einsum('bqhd,bkhd->bhqk', qn, k_nope, preferred_element_type=F32)
              + jnp.einsum('bqhd,bkd->bhqk', qr, k_r, preferred_element_type=F32)) * scale
        qpos = i * BLOCK + jnp.arange(BLOCK)
        sc = jnp.where(kpos[None, :] <= qpos[:, None], sc, NEG_INF)
        p = jax.nn.softmax(sc, axis=-1).astype(v.dtype)
        return jnp.einsum('bhqk,bkhe->bqhe', p, v)

    o = lax.map(one_block, (qn_b, qr_b, jnp.arange(nb)))
    return jnp.moveaxis(o, 0, 1).reshape(b, s_len, h * MLA_V_DIM)


def _fwd_setup_inputs(seed: int = 0) -> dict:
    key = jax.random.key(seed)
    ks = jax.random.split(key, 32)
    L = DEPTH

    def nrm(k, shape, scale):
        return jax.random.normal(k, shape, F32) * scale

    def gain(k, shape):
        return 1.0 + 0.05 * jax.random.normal(k, shape, F32)

    n_idx = jnp.arange(S5_STATE, dtype=F32)
    return {
        "x": jax.random.normal(ks[0], (BATCH, SEQ, D_MODEL), F32),
        "g_pre_mix": gain(ks[1], (L, D_MODEL)),
        "g_post_mix": gain(ks[2], (L, D_MODEL)),
        "g_pre_mlp": gain(ks[3], (L, D_MODEL)),
        "g_post_mlp": gain(ks[4], (L, D_MODEL)),
        "w_in": nrm(ks[5], (L, D_MODEL, D_IN), D_MODEL ** -0.5),
        "swa_sinks": nrm(ks[6], (L, SWA_HEADS), 0.5),
        "s5_lam_re": -0.5 + 0.01 * jax.random.normal(ks[7], (L, S5_GROUPS, S5_STATE), F32),
        "s5_lam_im": math.pi * n_idx + 0.01 * jax.random.normal(ks[8], (L, S5_GROUPS, S5_STATE), F32),
        "s5_log_dt": jax.random.uniform(ks[9], (L, S5_GROUPS), F32, math.log(S5_DT_MIN), math.log(S5_DT_MAX)),
        "s5_b_re": nrm(ks[10], (L, S5_GROUPS, S5_STATE, S5_GROUP), (2 * S5_GROUP) ** -0.5),
        "s5_b_im": nrm(ks[11], (L, S5_GROUPS, S5_STATE, S5_GROUP), (2 * S5_GROUP) ** -0.5),
        "s5_c_re": nrm(ks[12], (L, S5_GROUPS, S5_GROUP, S5_STATE), (2 * S5_STATE) ** -0.5),
        "s5_c_im": nrm(ks[13], (L, S5_GROUPS, S5_GROUP, S5_STATE), (2 * S5_STATE) ** -0.5),
        "s5_d": nrm(ks[14], (L, S5_CHANNELS), 1.0),
        "s5_w_glu": nrm(ks[15], (L, S5_CHANNELS, 2 * S5_CHANNELS), S5_CHANNELS ** -0.5),
        "mla_g_q": gain(ks[16], (L, MLA_Q_RANK)),
        "mla_g_kv": gain(ks[17], (L, MLA_KV_RANK)),
        "mla_w_uq": nrm(ks[18], (L, MLA_Q_RANK, MLA_HEADS * (MLA_NOPE_DIM + MLA_ROPE_DIM)), MLA_Q_RANK ** -0.5),
        "mla_w_ukv": nrm(ks[19], (L, MLA_KV_RANK, MLA_HEADS * (MLA_NOPE_DIM + MLA_V_DIM)), MLA_KV_RANK ** -0.5),
        "w_branch": nrm(ks[20], (L, N_BRANCH, BRANCH_WIDTH, D_MODEL), BRANCH_WIDTH ** -0.5),
        "w_out": nrm(ks[21], (L, D_MODEL, D_MODEL), D_MODEL ** -0.5),
        "w_ff1": nrm(ks[22], (L, D_MODEL, D_FF), D_MODEL ** -0.5),
        "w_ff2": nrm(ks[23], (L, D_FF, D_MODEL), D_FF ** -0.5),
    }


def _fwd_reference(x, g_pre_mix, g_post_mix, g_pre_mlp, g_post_mlp, w_in, swa_sinks,
              s5_lam_re, s5_lam_im, s5_log_dt, s5_b_re, s5_b_im, s5_c_re, s5_c_im,
              s5_d, s5_w_glu, mla_g_q, mla_g_kv, mla_w_uq, mla_w_ukv,
              w_branch, w_out, w_ff1, w_ff2):
    b, s_len, _ = x.shape
    cos_h, sin_h = rope_tables(s_len, HEAD_DIM)
    cos_r, sin_r = rope_tables(s_len, MLA_ROPE_DIM)
    split_at = tuple(int(i) for i in np.cumsum(IN_SPLITS)[:-1])
    for l in range(DEPTH):
        h = rms_norm(x, g_pre_mix[l])
        (sq, sk, sv, su, rq, rk, rv, rg, cq, ckv, kr, gate_logits) = jnp.split(h @ w_in[l], split_at, axis=-1)
        y_a = swa_attention(
            apply_rope(sq.reshape(b, s_len, SWA_HEADS, HEAD_DIM), cos_h, sin_h),
            apply_rope(sk.reshape(b, s_len, SWA_KV_HEADS, HEAD_DIM), cos_h, sin_h),
            sv.reshape(b, s_len, SWA_KV_HEADS, HEAD_DIM), swa_sinks[l])
        y_b = s5_ssm(su, s5_lam_re[l], s5_lam_im[l], s5_log_dt[l], s5_b_re[l], s5_b_im[l],
                     s5_c_re[l], s5_c_im[l], s5_d[l], s5_w_glu[l])
        y_c = retention(rq.reshape(b, s_len, RET_HEADS, RET_QK_DIM),
                        rk.reshape(b, s_len, RET_HEADS, RET_QK_DIM),
                        rv.reshape(b, s_len, RET_HEADS, RET_V_DIM), rg, cos_h, sin_h)
        y_d = mla_attention(cq, ckv, kr, mla_g_q[l], mla_g_kv[l], mla_w_uq[l], mla_w_ukv[l], cos_r, sin_r)
        gates = jax.nn.sigmoid(gate_logits.astype(F32)).astype(x.dtype).reshape(b, s_len, N_BRANCH, D_MODEL)
        merged = (gates[:, :, 0] * (y_a @ w_branch[l, 0])
                  + gates[:, :, 1] * (y_b @ w_branch[l, 1])
                  + gates[:, :, 2] * (y_c @ w_branch[l, 2])
                  + gates[:, :, 3] * (y_d @ w_branch[l, 3]))
        x = x + rms_norm(merged @ w_out[l], g_post_mix[l])
        h = rms_norm(x, g_pre_mlp[l])
        f = jnp.square(jax.nn.relu(h @ w_ff1[l])) @ w_ff2[l]
        x = x + rms_norm(f, g_post_mlp[l])
    return x


import jax as _jax
import jax.numpy as _jnp

TWIN_FORMAT = 'train_step'
FWD_PARAMS = ['x', 'g_pre_mix', 'g_post_mix', 'g_pre_mlp', 'g_post_mlp', 'w_in', 'swa_sinks', 's5_lam_re', 's5_lam_im', 's5_log_dt', 's5_b_re', 's5_b_im', 's5_c_re', 's5_c_im', 's5_d', 's5_w_glu', 'mla_g_q', 'mla_g_kv', 'mla_w_uq', 'mla_w_ukv', 'w_branch', 'w_out', 'w_ff1', 'w_ff2']
TWIN_WEIGHTS = ['g_pre_mix', 'g_post_mix', 'g_pre_mlp', 'g_post_mlp', 'w_in', 'swa_sinks', 's5_lam_re', 's5_lam_im', 's5_log_dt', 's5_b_re', 's5_b_im', 's5_c_re', 's5_c_im', 's5_d', 's5_w_glu', 'mla_g_q', 'mla_g_kv', 'mla_w_uq', 'mla_w_ukv', 'w_branch', 'w_out', 'w_ff1', 'w_ff2']
TWIN_DIFF_INPUT = 'x'
TWIN_INPUTS = ['x', 'g_pre_mix', 'g_post_mix', 'g_pre_mlp', 'g_post_mlp', 'w_in', 'swa_sinks', 's5_lam_re', 's5_lam_im', 's5_log_dt', 's5_b_re', 's5_b_im', 's5_c_re', 's5_c_im', 's5_d', 's5_w_glu', 'mla_g_q', 'mla_g_kv', 'mla_w_uq', 'mla_w_ukv', 'w_branch', 'w_out', 'w_ff1', 'w_ff2', 'loss_target', 'm_g_pre_mix', 'm_g_post_mix', 'm_g_pre_mlp', 'm_g_post_mlp', 'm_w_in', 'm_swa_sinks', 'm_s5_lam_re', 'm_s5_lam_im', 'm_s5_log_dt', 'm_s5_b_re', 'm_s5_b_im', 'm_s5_c_re', 'm_s5_c_im', 'm_s5_d', 'm_s5_w_glu', 'm_mla_g_q', 'm_mla_g_kv', 'm_mla_w_uq', 'm_mla_w_ukv', 'm_w_branch', 'm_w_out', 'm_w_ff1', 'm_w_ff2', 'v_g_pre_mix', 'v_g_post_mix', 'v_g_pre_mlp', 'v_g_post_mlp', 'v_w_in', 'v_swa_sinks', 'v_s5_lam_re', 'v_s5_lam_im', 'v_s5_log_dt', 'v_s5_b_re', 'v_s5_b_im', 'v_s5_c_re', 'v_s5_c_im', 'v_s5_d', 'v_s5_w_glu', 'v_mla_g_q', 'v_mla_g_kv', 'v_mla_w_uq', 'v_mla_w_ukv', 'v_w_branch', 'v_w_out', 'v_w_ff1', 'v_w_ff2']
TWIN_OUTPUTS = ['loss', 'grad_x', 'grad_g_pre_mix', 'grad_g_post_mix', 'grad_g_pre_mlp', 'grad_g_post_mlp', 'grad_w_in', 'grad_swa_sinks', 'grad_s5_lam_re', 'grad_s5_lam_im', 'grad_s5_log_dt', 'grad_s5_b_re', 'grad_s5_b_im', 'grad_s5_c_re', 'grad_s5_c_im', 'grad_s5_d', 'grad_s5_w_glu', 'grad_mla_g_q', 'grad_mla_g_kv', 'grad_mla_w_uq', 'grad_mla_w_ukv', 'grad_w_branch', 'grad_w_out', 'grad_w_ff1', 'grad_w_ff2', 'delta_g_pre_mix', 'delta_g_post_mix', 'delta_g_pre_mlp', 'delta_g_post_mlp', 'delta_w_in', 'delta_swa_sinks', 'delta_s5_lam_re', 'delta_s5_lam_im', 'delta_s5_log_dt', 'delta_s5_b_re', 'delta_s5_b_im', 'delta_s5_c_re', 'delta_s5_c_im', 'delta_s5_d', 'delta_s5_w_glu', 'delta_mla_g_q', 'delta_mla_g_kv', 'delta_mla_w_uq', 'delta_mla_w_ukv', 'delta_w_branch', 'delta_w_out', 'delta_w_ff1', 'delta_w_ff2', 'new_m_g_pre_mix', 'new_m_g_post_mix', 'new_m_g_pre_mlp', 'new_m_g_post_mlp', 'new_m_w_in', 'new_m_swa_sinks', 'new_m_s5_lam_re', 'new_m_s5_lam_im', 'new_m_s5_log_dt', 'new_m_s5_b_re', 'new_m_s5_b_im', 'new_m_s5_c_re', 'new_m_s5_c_im', 'new_m_s5_d', 'new_m_s5_w_glu', 'new_m_mla_g_q', 'new_m_mla_g_kv', 'new_m_mla_w_uq', 'new_m_mla_w_ukv', 'new_m_w_branch', 'new_m_w_out', 'new_m_w_ff1', 'new_m_w_ff2', 'new_v_g_pre_mix', 'new_v_g_post_mix', 'new_v_g_pre_mlp', 'new_v_g_post_mlp', 'new_v_w_in', 'new_v_swa_sinks', 'new_v_s5_lam_re', 'new_v_s5_lam_im', 'new_v_s5_log_dt', 'new_v_s5_b_re', 'new_v_s5_b_im', 'new_v_s5_c_re', 'new_v_s5_c_im', 'new_v_s5_d', 'new_v_s5_w_glu', 'new_v_mla_g_q', 'new_v_mla_g_kv', 'new_v_mla_w_uq', 'new_v_mla_w_ukv', 'new_v_w_branch', 'new_v_w_out', 'new_v_w_ff1', 'new_v_w_ff2']
TWIN_LEAF_KINDS = {'loss': 'loss', 'grad_x': 'grad_x', 'grad_g_pre_mix': 'grad_w', 'grad_g_post_mix': 'grad_w', 'grad_g_pre_mlp': 'grad_w', 'grad_g_post_mlp': 'grad_w', 'grad_w_in': 'grad_w', 'grad_swa_sinks': 'grad_w', 'grad_s5_lam_re': 'grad_w', 'grad_s5_lam_im': 'grad_w', 'grad_s5_log_dt': 'grad_w', 'grad_s5_b_re': 'grad_w', 'grad_s5_b_im': 'grad_w', 'grad_s5_c_re': 'grad_w', 'grad_s5_c_im': 'grad_w', 'grad_s5_d': 'grad_w', 'grad_s5_w_glu': 'grad_w', 'grad_mla_g_q': 'grad_w', 'grad_mla_g_kv': 'grad_w', 'grad_mla_w_uq': 'grad_w', 'grad_mla_w_ukv': 'grad_w', 'grad_w_branch': 'grad_w', 'grad_w_out': 'grad_w', 'grad_w_ff1': 'grad_w', 'grad_w_ff2': 'grad_w', 'delta_g_pre_mix': 'delta_w', 'delta_g_post_mix': 'delta_w', 'delta_g_pre_mlp': 'delta_w', 'delta_g_post_mlp': 'delta_w', 'delta_w_in': 'delta_w', 'delta_swa_sinks': 'delta_w', 'delta_s5_lam_re': 'delta_w', 'delta_s5_lam_im': 'delta_w', 'delta_s5_log_dt': 'delta_w', 'delta_s5_b_re': 'delta_w', 'delta_s5_b_im': 'delta_w', 'delta_s5_c_re': 'delta_w', 'delta_s5_c_im': 'delta_w', 'delta_s5_d': 'delta_w', 'delta_s5_w_glu': 'delta_w', 'delta_mla_g_q': 'delta_w', 'delta_mla_g_kv': 'delta_w', 'delta_mla_w_uq': 'delta_w', 'delta_mla_w_ukv': 'delta_w', 'delta_w_branch': 'delta_w', 'delta_w_out': 'delta_w', 'delta_w_ff1': 'delta_w', 'delta_w_ff2': 'delta_w', 'new_m_g_pre_mix': 'new_m', 'new_m_g_post_mix': 'new_m', 'new_m_g_pre_mlp': 'new_m', 'new_m_g_post_mlp': 'new_m', 'new_m_w_in': 'new_m', 'new_m_swa_sinks': 'new_m', 'new_m_s5_lam_re': 'new_m', 'new_m_s5_lam_im': 'new_m', 'new_m_s5_log_dt': 'new_m', 'new_m_s5_b_re': 'new_m', 'new_m_s5_b_im': 'new_m', 'new_m_s5_c_re': 'new_m', 'new_m_s5_c_im': 'new_m', 'new_m_s5_d': 'new_m', 'new_m_s5_w_glu': 'new_m', 'new_m_mla_g_q': 'new_m', 'new_m_mla_g_kv': 'new_m', 'new_m_mla_w_uq': 'new_m', 'new_m_mla_w_ukv': 'new_m', 'new_m_w_branch': 'new_m', 'new_m_w_out': 'new_m', 'new_m_w_ff1': 'new_m', 'new_m_w_ff2': 'new_m', 'new_v_g_pre_mix': 'new_v', 'new_v_g_post_mix': 'new_v', 'new_v_g_pre_mlp': 'new_v', 'new_v_g_post_mlp': 'new_v', 'new_v_w_in': 'new_v', 'new_v_swa_sinks': 'new_v', 'new_v_s5_lam_re': 'new_v', 'new_v_s5_lam_im': 'new_v', 'new_v_s5_log_dt': 'new_v', 'new_v_s5_b_re': 'new_v', 'new_v_s5_b_im': 'new_v', 'new_v_s5_c_re': 'new_v', 'new_v_s5_c_im': 'new_v', 'new_v_s5_d': 'new_v', 'new_v_s5_w_glu': 'new_v', 'new_v_mla_g_q': 'new_v', 'new_v_mla_g_kv': 'new_v', 'new_v_mla_w_uq': 'new_v', 'new_v_mla_w_ukv': 'new_v', 'new_v_w_branch': 'new_v', 'new_v_w_out': 'new_v', 'new_v_w_ff1': 'new_v', 'new_v_w_ff2': 'new_v'}


def _forward(args):
    return _fwd_reference(*[args[k] for k in FWD_PARAMS])


def _output_shape():
    out = _jax.eval_shape(lambda: _forward(_fwd_setup_inputs(0)))
    return out.shape, out.dtype

N_MICROBATCH = 1
ADAM_LR = 0.001
ADAM_B1 = 0.9
ADAM_B2 = 0.999
ADAM_EPS = 1e-08
ADAM_WD = 0.01
ADAM_STEP = 10
PER_EXAMPLE_BATCH_AXIS = {'x': 0, 'loss_target': 0}
SHARED_INPUTS = []
_WEIGHT_DTYPES = {'g_pre_mix': _jnp.float32, 'g_post_mix': _jnp.float32, 'g_pre_mlp': _jnp.float32, 'g_post_mlp': _jnp.float32, 'w_in': _jnp.float32, 'swa_sinks': _jnp.float32, 's5_lam_re': _jnp.float32, 's5_lam_im': _jnp.float32, 's5_log_dt': _jnp.float32, 's5_b_re': _jnp.float32, 's5_b_im': _jnp.float32, 's5_c_re': _jnp.float32, 's5_c_im': _jnp.float32, 's5_d': _jnp.float32, 's5_w_glu': _jnp.float32, 'mla_g_q': _jnp.float32, 'mla_g_kv': _jnp.float32, 'mla_w_uq': _jnp.float32, 'mla_w_ukv': _jnp.float32, 'w_branch': _jnp.float32, 'w_out': _jnp.float32, 'w_ff1': _jnp.float32, 'w_ff2': _jnp.float32}
MOMENT_SCALE = {'g_pre_mix': 5.710986e+00, 'g_post_mix': 3.374652e+01, 'g_pre_mlp': 3.758625e+00, 'g_post_mlp': 3.512658e+01, 'w_in': 2.069925e+00, 'swa_sinks': 3.416538e-01, 's5_lam_re': 1.115812e-01, 's5_lam_im': 1.119051e-01, 's5_log_dt': 2.516643e+01, 's5_b_re': 1.179702e-01, 's5_b_im': 1.103228e-01, 's5_c_re': 2.163256e-01, 's5_c_im': 1.933786e-01, 's5_d': 9.939638e+00, 's5_w_glu': 6.515784e+00, 'mla_g_q': 3.612234e-01, 'mla_g_kv': 1.228289e+01, 'mla_w_uq': 2.356179e-01, 'mla_w_ukv': 4.131165e+00, 'w_branch': 4.359580e+00, 'w_out': 8.998958e+00, 'w_ff1': 1.846363e+00, 'w_ff2': 1.166612e+01}


def _to_microbatches(a, axis):
    t = _jnp.moveaxis(a, axis, 0)
    t = t.reshape((N_MICROBATCH, t.shape[0] // N_MICROBATCH) + t.shape[1:])
    return _jnp.moveaxis(t, 1, axis + 1)


def setup_inputs(seed: int = 0) -> dict:
    inp = _fwd_setup_inputs(seed)
    key = _jax.random.fold_in(_jax.random.key(seed), 7919)
    shape, _ = _output_shape()
    out = dict(inp)
    out["loss_target"] = _jax.random.normal(_jax.random.fold_in(key, 0), shape, _jnp.float32)
    for i, name in enumerate(TWIN_WEIGHTS):
        w = inp[name].astype(_jnp.float32)
        if MOMENT_SCALE is None:
            s = _jnp.sqrt(_jnp.mean(_jnp.square(w)) + 1e-30)
        else:
            s = MOMENT_SCALE[name]
        km, kv = _jax.random.split(_jax.random.fold_in(key, i + 1))
        out[name] = w
        out["m_" + name] = s * _jax.random.normal(km, w.shape, _jnp.float32)
        out["v_" + name] = (s * s) * _jax.random.uniform(kv, w.shape, _jnp.float32, 0.5, 1.5)
    if N_MICROBATCH > 1:
        for name, axis in PER_EXAMPLE_BATCH_AXIS.items():
            out[name] = _to_microbatches(out[name], axis)
    return {'x': out['x'], 'g_pre_mix': out['g_pre_mix'], 'g_post_mix': out['g_post_mix'], 'g_pre_mlp': out['g_pre_mlp'], 'g_post_mlp': out['g_post_mlp'], 'w_in': out['w_in'], 'swa_sinks': out['swa_sinks'], 's5_lam_re': out['s5_lam_re'], 's5_lam_im': out['s5_lam_im'], 's5_log_dt': out['s5_log_dt'], 's5_b_re': out['s5_b_re'], 's5_b_im': out['s5_b_im'], 's5_c_re': out['s5_c_re'], 's5_c_im': out['s5_c_im'], 's5_d': out['s5_d'], 's5_w_glu': out['s5_w_glu'], 'mla_g_q': out['mla_g_q'], 'mla_g_kv': out['mla_g_kv'], 'mla_w_uq': out['mla_w_uq'], 'mla_w_ukv': out['mla_w_ukv'], 'w_branch': out['w_branch'], 'w_out': out['w_out'], 'w_ff1': out['w_ff1'], 'w_ff2': out['w_ff2'], 'loss_target': out['loss_target'], 'm_g_pre_mix': out['m_g_pre_mix'], 'm_g_post_mix': out['m_g_post_mix'], 'm_g_pre_mlp': out['m_g_pre_mlp'], 'm_g_post_mlp': out['m_g_post_mlp'], 'm_w_in': out['m_w_in'], 'm_swa_sinks': out['m_swa_sinks'], 'm_s5_lam_re': out['m_s5_lam_re'], 'm_s5_lam_im': out['m_s5_lam_im'], 'm_s5_log_dt': out['m_s5_log_dt'], 'm_s5_b_re': out['m_s5_b_re'], 'm_s5_b_im': out['m_s5_b_im'], 'm_s5_c_re': out['m_s5_c_re'], 'm_s5_c_im': out['m_s5_c_im'], 'm_s5_d': out['m_s5_d'], 'm_s5_w_glu': out['m_s5_w_glu'], 'm_mla_g_q': out['m_mla_g_q'], 'm_mla_g_kv': out['m_mla_g_kv'], 'm_mla_w_uq': out['m_mla_w_uq'], 'm_mla_w_ukv': out['m_mla_w_ukv'], 'm_w_branch': out['m_w_branch'], 'm_w_out': out['m_w_out'], 'm_w_ff1': out['m_w_ff1'], 'm_w_ff2': out['m_w_ff2'], 'v_g_pre_mix': out['v_g_pre_mix'], 'v_g_post_mix': out['v_g_post_mix'], 'v_g_pre_mlp': out['v_g_pre_mlp'], 'v_g_post_mlp': out['v_g_post_mlp'], 'v_w_in': out['v_w_in'], 'v_swa_sinks': out['v_swa_sinks'], 'v_s5_lam_re': out['v_s5_lam_re'], 'v_s5_lam_im': out['v_s5_lam_im'], 'v_s5_log_dt': out['v_s5_log_dt'], 'v_s5_b_re': out['v_s5_b_re'], 'v_s5_b_im': out['v_s5_b_im'], 'v_s5_c_re': out['v_s5_c_re'], 'v_s5_c_im': out['v_s5_c_im'], 'v_s5_d': out['v_s5_d'], 'v_s5_w_glu': out['v_s5_w_glu'], 'v_mla_g_q': out['v_mla_g_q'], 'v_mla_g_kv': out['v_mla_g_kv'], 'v_mla_w_uq': out['v_mla_w_uq'], 'v_mla_w_ukv': out['v_mla_w_ukv'], 'v_w_branch': out['v_w_branch'], 'v_w_out': out['v_w_out'], 'v_w_ff1': out['v_w_ff1'], 'v_w_ff2': out['v_w_ff2']}


def _loss(weights, diff, rest, loss_target):
    with _jax.named_scope("forward"):
        args = {**rest, TWIN_DIFF_INPUT: diff, **{k: w.astype(_WEIGHT_DTYPES[k]) for k, w in weights.items()}}
        y = _forward(args)
    with _jax.named_scope("loss_head"):
        err = _jnp.square(y.astype(_jnp.float32) - loss_target)
        return 0.5 * _jnp.sum(_jnp.mean(err, axis=-1)) if err.ndim else 0.5 * err


def _adamw(w, g, m, v):
    m = ADAM_B1 * m + (1.0 - ADAM_B1) * g
    v = ADAM_B2 * v + (1.0 - ADAM_B2) * _jnp.square(g)
    m_hat = m / (1.0 - ADAM_B1 ** ADAM_STEP)
    v_hat = v / (1.0 - ADAM_B2 ** ADAM_STEP)
    delta = -ADAM_LR * (m_hat / (_jnp.sqrt(v_hat) + ADAM_EPS) + ADAM_WD * w)
    return delta, m, v


def reference(x, g_pre_mix, g_post_mix, g_pre_mlp, g_post_mlp, w_in, swa_sinks, s5_lam_re, s5_lam_im, s5_log_dt, s5_b_re, s5_b_im, s5_c_re, s5_c_im, s5_d, s5_w_glu, mla_g_q, mla_g_kv, mla_w_uq, mla_w_ukv, w_branch, w_out, w_ff1, w_ff2, loss_target, m_g_pre_mix, m_g_post_mix, m_g_pre_mlp, m_g_post_mlp, m_w_in, m_swa_sinks, m_s5_lam_re, m_s5_lam_im, m_s5_log_dt, m_s5_b_re, m_s5_b_im, m_s5_c_re, m_s5_c_im, m_s5_d, m_s5_w_glu, m_mla_g_q, m_mla_g_kv, m_mla_w_uq, m_mla_w_ukv, m_w_branch, m_w_out, m_w_ff1, m_w_ff2, v_g_pre_mix, v_g_post_mix, v_g_pre_mlp, v_g_post_mlp, v_w_in, v_swa_sinks, v_s5_lam_re, v_s5_lam_im, v_s5_log_dt, v_s5_b_re, v_s5_b_im, v_s5_c_re, v_s5_c_im, v_s5_d, v_s5_w_glu, v_mla_g_q, v_mla_g_kv, v_mla_w_uq, v_mla_w_ukv, v_w_branch, v_w_out, v_w_ff1, v_w_ff2):
    given = dict(x=x, g_pre_mix=g_pre_mix, g_post_mix=g_post_mix, g_pre_mlp=g_pre_mlp, g_post_mlp=g_post_mlp, w_in=w_in, swa_sinks=swa_sinks, s5_lam_re=s5_lam_re, s5_lam_im=s5_lam_im, s5_log_dt=s5_log_dt, s5_b_re=s5_b_re, s5_b_im=s5_b_im, s5_c_re=s5_c_re, s5_c_im=s5_c_im, s5_d=s5_d, s5_w_glu=s5_w_glu, mla_g_q=mla_g_q, mla_g_kv=mla_g_kv, mla_w_uq=mla_w_uq, mla_w_ukv=mla_w_ukv, w_branch=w_branch, w_out=w_out, w_ff1=w_ff1, w_ff2=w_ff2, loss_target=loss_target, m_g_pre_mix=m_g_pre_mix, m_g_post_mix=m_g_post_mix, m_g_pre_mlp=m_g_pre_mlp, m_g_post_mlp=m_g_post_mlp, m_w_in=m_w_in, m_swa_sinks=m_swa_sinks, m_s5_lam_re=m_s5_lam_re, m_s5_lam_im=m_s5_lam_im, m_s5_log_dt=m_s5_log_dt, m_s5_b_re=m_s5_b_re, m_s5_b_im=m_s5_b_im, m_s5_c_re=m_s5_c_re, m_s5_c_im=m_s5_c_im, m_s5_d=m_s5_d, m_s5_w_glu=m_s5_w_glu, m_mla_g_q=m_mla_g_q, m_mla_g_kv=m_mla_g_kv, m_mla_w_uq=m_mla_w_uq, m_mla_w_ukv=m_mla_w_ukv, m_w_branch=m_w_branch, m_w_out=m_w_out, m_w_ff1=m_w_ff1, m_w_ff2=m_w_ff2, v_g_pre_mix=v_g_pre_mix, v_g_post_mix=v_g_post_mix, v_g_pre_mlp=v_g_pre_mlp, v_g_post_mlp=v_g_post_mlp, v_w_in=v_w_in, v_swa_sinks=v_swa_sinks, v_s5_lam_re=v_s5_lam_re, v_s5_lam_im=v_s5_lam_im, v_s5_log_dt=v_s5_log_dt, v_s5_b_re=v_s5_b_re, v_s5_b_im=v_s5_b_im, v_s5_c_re=v_s5_c_re, v_s5_c_im=v_s5_c_im, v_s5_d=v_s5_d, v_s5_w_glu=v_s5_w_glu, v_mla_g_q=v_mla_g_q, v_mla_g_kv=v_mla_g_kv, v_mla_w_uq=v_mla_w_uq, v_mla_w_ukv=v_mla_w_ukv, v_w_branch=v_w_branch, v_w_out=v_w_out, v_w_ff1=v_w_ff1, v_w_ff2=v_w_ff2)
    weights = {n: given[n] for n in TWIN_WEIGHTS}
    shared = {n: given[n] for n in SHARED_INPUTS}
    per_example = {n: given[n] for n in ['x']}
    grad_fn = _jax.value_and_grad(_loss, argnums=(0, 1))

    def one_microbatch(ex, loss_target):
        ex = dict(ex)
        diff = ex.pop(TWIN_DIFF_INPUT)
        return grad_fn(weights, diff, {**shared, **ex}, loss_target)

    if N_MICROBATCH == 1:
        loss, (grad_w, grad_x) = one_microbatch(per_example, given["loss_target"])
    else:
        def body(carry, xs):
            loss_sum, grad_sum = carry
            l_k, (gw_k, gx_k) = one_microbatch(xs[0], xs[1])
            with _jax.named_scope("update"):
                return (loss_sum + l_k, _jax.tree.map(_jnp.add, grad_sum, gw_k)), gx_k

        init = (_jnp.zeros((), _jnp.float32), _jax.tree.map(_jnp.zeros_like, weights))
        (loss, grad_w), grad_x = _jax.lax.scan(body, init, (per_example, given["loss_target"]))
    with _jax.named_scope("update"):
        delta_w, new_m, new_v = {}, {}, {}
        for n in TWIN_WEIGHTS:
            delta_w[n], new_m[n], new_v[n] = _adamw(weights[n], grad_w[n], given["m_" + n], given["v_" + n])
    return (loss, grad_x, *[grad_w[n] for n in TWIN_WEIGHTS], *[delta_w[n] for n in TWIN_WEIGHTS],
            *[new_m[n] for n in TWIN_WEIGHTS], *[new_v[n] for n in TWIN_WEIGHTS])
```

```python
import functools
import math

import numpy as np
import jax
import jax.numpy as jnp
from jax import lax
from jax.experimental import pallas as pl
from jax.experimental.pallas import tpu as pltpu

F32, BF16 = jnp.float32, jnp.bfloat16
EPS = 1e-6
NEG_INF = -1e30
ROPE_THETA = 10000.0
D_MODEL = 1024
N_DEV = 8
BLOCK = 128
SWA_HEADS, SWA_KV = 8, 2
RET_HEADS, RET_DK, RET_DV = 4, 64, 128
MLA_HEADS = 8
S5_GROUPS, S5_STATE, S5_GROUP = 32, 64, 16
S5_CH = S5_GROUPS * S5_STATE
SCAN_LANES = 256
SCAN_SEG = 8
ADAM_LR, ADAM_B1, ADAM_B2, ADAM_EPS, ADAM_WD, ADAM_STEP = 0.001, 0.9, 0.999, 1e-08, 0.01, 10
VMEM_MB = 56

PROJ = dict(sq=(0, 512), su=(512, 512), rv=(1024, 512), rg=(1536, 512), rq=(2048, 256), rk=(2304, 256),
            cq=(2560, 256), sk=(2816, 128), sv=(2944, 128), ckv=(3072, 128), kr=(3200, 128), gates=(3328, 4096))
PROJ_W = 7424
REF_SPLIT = dict(sq=(0, 512), sk=(512, 128), sv=(640, 128), su=(768, 512), rq=(1280, 256), rk=(1536, 256),
                 rv=(1792, 512), rg=(2304, 512), cq=(2816, 256), ckv=(3072, 128), kr=(3200, 32), gates=(3232, 4096))
PROJ_ORDER = ("sq", "su", "rv", "rg", "rq", "rk", "cq", "sk", "sv", "ckv", "kr", "gates")
REF_ORDER = ("sq", "sk", "sv", "su", "rq", "rk", "rv", "rg", "cq", "ckv", "kr", "gates")

NN = (((1,), (0,)), ((), ()))
NT = (((1,), (1,)), ((), ()))
TN = (((0,), (0,)), ((), ()))


def _params(sem):
    return pltpu.CompilerParams(dimension_semantics=sem, vmem_limit_bytes=VMEM_MB * 2 ** 20)


def _raw_dot(a, b, dims):
    return lax.dot_general(a.astype(BF16), b.astype(BF16), dims, preferred_element_type=F32)


@jax.custom_vjp
def _dnn(a, b):
    return _raw_dot(a, b, NN)


@jax.custom_vjp
def _dnt(a, b):
    return _raw_dot(a, b, NT)


@jax.custom_vjp
def _dtn(a, b):
    return _raw_dot(a, b, TN)


_dnn.defvjp(lambda a, b: (_dnn(a, b), (a, b)),
            lambda r, g: (_dnt(g, r[1]).astype(r[0].dtype), _dtn(r[0], g).astype(r[1].dtype)))
_dnt.defvjp(lambda a, b: (_dnt(a, b), (a, b)),
            lambda r, g: (_dnn(g, r[1]).astype(r[0].dtype), _dtn(g, r[0]).astype(r[1].dtype)))
_dtn.defvjp(lambda a, b: (_dtn(a, b), (a, b)),
            lambda r, g: (_dnt(r[1], g).astype(r[0].dtype), _dnn(r[0], g).astype(r[1].dtype)))


class _Cols:
    def __init__(self, arr, width, index):
        self.arr, self.width, self.index = arr, width, index


def _seg(proj, name):
    off, width = PROJ[name]
    return _Cols(proj, width, off // width)


def _row_spec(shape, tile):
    lead = len(shape) - 2
    return pl.BlockSpec(tuple(shape[:-2]) + (tile, shape[-1]), lambda i, lead=lead: (0,) * lead + (i, 0))


def _const_spec(shape):
    nd = len(shape)
    return pl.BlockSpec(tuple(shape), lambda *_, nd=nd: (0,) * nd)


def _rowwise(name, fn, rows, consts, out_rows, out_accs=(), tile=256):
    arrs = [r.arr if isinstance(r, _Cols) else r for r in rows]
    t_len = arrs[0].shape[-2]
    tile = min(tile, t_len)
    n = t_len // tile
    in_specs = []
    for r in rows:
        if isinstance(r, _Cols):
            in_specs.append(pl.BlockSpec((tile, r.width), lambda i, c=r.index: (i, c)))
        else:
            in_specs.append(_row_spec(r.shape, tile))
    in_specs += [_const_spec(c.shape) for c in consts]
    out_specs = [_row_spec(s, tile) for s, _ in out_rows] + [_const_spec(s) for s, _ in out_accs]
    out_shape = [jax.ShapeDtypeStruct(s, d) for s, d in tuple(out_rows) + tuple(out_accs)]
    n_in, n_or = len(rows) + len(consts), len(out_rows)

    def body(*refs):
        res = fn(*[r[...] for r in refs[:n_in]])
        row_out, acc_out = res if out_accs else (res, ())
        for ref, val in zip(refs[n_in:n_in + n_or], row_out):
            if isinstance(val, (list, tuple)):
                for h, v_h in enumerate(val):
                    ref[h] = v_h.astype(ref.dtype)
            else:
                ref[...] = val.astype(ref.dtype)
        first = pl.program_id(0) == 0
        for ref, val in zip(refs[n_in + n_or:], acc_out):
            @pl.when(first)
            def _(ref=ref, val=val):
                ref[...] = val

            @pl.when(jnp.logical_not(first))
            def _(ref=ref, val=val):
                ref[...] += val

    return pl.pallas_call(body, name=name, grid=(n,), in_specs=in_specs, out_specs=out_specs, out_shape=out_shape,
                          compiler_params=_params(("arbitrary",)))(*arrs, *consts)


def _mm(name, a, b, mode, out_dtypes=(F32,), tm=512, tn=512, epi=None, epi_in=()):
    if mode == "nn":
        (m, k), n = a.shape, b.shape[1]
    elif mode == "nt":
        (m, k), n = a.shape, b.shape[0]
    else:
        (k, m), n = a.shape, b.shape[1]
    tm, tn = min(tm, m), min(tn, n)
    assert m % tm == 0 and n % tn == 0, (name, m, n, tm, tn)
    a_spec = pl.BlockSpec((k, tm), lambda i, j: (0, i)) if mode == "tn" else pl.BlockSpec((tm, k), lambda i, j: (i, 0))
    b_spec = pl.BlockSpec((tn, k), lambda i, j: (j, 0)) if mode == "nt" else pl.BlockSpec((k, tn), lambda i, j: (0, j))
    dims = dict(nn=NN, nt=NT, tn=TN)[mode]
    o_spec = pl.BlockSpec((tm, tn), lambda i, j: (i, j))
    n_e = len(epi_in)

    def body(a_ref, b_ref, *refs):
        acc = _raw_dot(a_ref[...], b_ref[...], dims)
        outs = epi(acc, *[r[...] for r in refs[:n_e]]) if epi is not None else (acc,)
        for ref, val in zip(refs[n_e:], outs):
            ref[...] = val.astype(ref.dtype)

    res = pl.pallas_call(body, name=name, grid=(m // tm, n // tn),
                         in_specs=[a_spec, b_spec] + [o_spec] * n_e, out_specs=[o_spec] * len(out_dtypes),
                         out_shape=[jax.ShapeDtypeStruct((m, n), d) for d in out_dtypes],
                         compiler_params=_params(("parallel", "parallel")))(a, b, *epi_in)
    return res[0] if len(out_dtypes) == 1 else res


def _rms(x, g):
    return x * lax.rsqrt(jnp.mean(x * x, axis=-1, keepdims=True) + EPS) * g


def _rms_bwd(x, g, dy):
    _, vjp = jax.vjp(_rms, x, g)
    return vjp(dy)


def _rope_tables(t_len, half):
    dim = 2 * half
    inv = 1.0 / (ROPE_THETA ** (jnp.arange(0, dim, 2, dtype=F32) / dim))
    ang = jnp.arange(t_len, dtype=F32)[:, None] * inv[None, :]
    c, s = jnp.cos(ang), jnp.sin(ang)
    reps = 128 // dim
    return jnp.tile(jnp.concatenate([c, c], 1), (1, reps)), jnp.tile(jnp.concatenate([-s, s], 1), (1, reps))


def _rope(x, cos, sin, half):
    w = x.shape[-1]
    reps = w // 128
    if reps > 1:
        cos, sin = jnp.tile(cos, (1, reps)), jnp.tile(sin, (1, reps))
    lane = lax.broadcasted_iota(jnp.int32, x.shape, 1)
    partner = jnp.where((lane % (2 * half)) < half, pltpu.roll(x, w - half, 1), pltpu.roll(x, half, 1))
    return x * cos + partner * sin


def _swa_mask(i):
    qi = lax.broadcasted_iota(jnp.int32, (BLOCK, 2 * BLOCK), 0) + BLOCK
    kj = lax.broadcasted_iota(jnp.int32, (BLOCK, 2 * BLOCK), 1)
    diff = qi - kj
    return (diff >= 0) & (diff < BLOCK) & ((i > 0) | (kj >= BLOCK))


def _swa_head(q, kw, vw, sink, mask):
    sc = _dnt(q, kw) * (64 ** -0.5)
    sc = jnp.where(mask, sc, NEG_INF)
    m = lax.stop_gradient(jnp.maximum(jnp.max(sc, axis=-1, keepdims=True), sink))
    p = jnp.exp(sc - m)
    denom = jnp.sum(p, axis=-1, keepdims=True) + jnp.exp(sink - m)
    return _dnn(p / denom, vw)


def _swa_prep(proj, cos, sin):
    def fn(sq, sk, sv, c, s):
        return _rope(sq, c, s, 32), _rope(sk, c, s, 32), sv
    t_len = proj.shape[0]
    return _rowwise("swa_prep", fn, [_seg(proj, "sq"), _seg(proj, "sk"), _seg(proj, "sv"), cos, sin], [],
                    [((t_len, 512), BF16), ((t_len, 128), BF16), ((t_len, 128), BF16)])


def _swa_fwd(q, k, v, sinks):
    t_len = q.shape[0]
    nb = t_len // BLOCK

    def body(q_ref, kp_ref, kc_ref, vp_ref, vc_ref, s_ref, o_ref):
        mask = _swa_mask(pl.program_id(0))
        for j in range(SWA_KV):
            cols = slice(64 * j, 64 * j + 64)
            kw = jnp.concatenate([kp_ref[:, cols], kc_ref[:, cols]], axis=0)
            vw = jnp.concatenate([vp_ref[:, cols], vc_ref[:, cols]], axis=0)
            for g in range(SWA_HEADS // SWA_KV):
                h = j * 4 + g
                o = _swa_head(q_ref[:, 64 * h:64 * h + 64], kw, vw, s_ref[h:h + 1, 0:1], mask)
                o_ref[:, 64 * h:64 * h + 64] = o.astype(o_ref.dtype)

    prev = lambda i: (jnp.maximum(i - 1, 0), 0)
    cur = lambda i: (i, 0)
    return pl.pallas_call(
        body, name="swa_fwd", grid=(nb,),
        in_specs=[pl.BlockSpec((BLOCK, 512), cur), pl.BlockSpec((BLOCK, 128), prev), pl.BlockSpec((BLOCK, 128), cur),
                  pl.BlockSpec((BLOCK, 128), prev), pl.BlockSpec((BLOCK, 128), cur), _const_spec((8, 128))],
        out_specs=pl.BlockSpec((BLOCK, 512), cur), out_shape=jax.ShapeDtypeStruct((t_len, 512), BF16),
        compiler_params=_params(("parallel",)))(q, k, k, v, v, sinks)


def _swa_bwd(q, k, v, sinks, dy):
    t_len = q.shape[0]
    nb = t_len // BLOCK

    def body(q_ref, kp_ref, kc_ref, vp_ref, vc_ref, s_ref, dy_ref, dq_ref, dk_ref, dv_ref, ds_ref, k_acc, v_acc):
        i = pl.program_id(0)

        @pl.when(i == 0)
        def _():
            k_acc[...] = jnp.zeros_like(k_acc)
            v_acc[...] = jnp.zeros_like(v_acc)
            ds_ref[...] = jnp.zeros_like(ds_ref)

        @pl.when(i < nb)
        def _():
            mask = _swa_mask(i)
            for j in range(SWA_KV):
                cols = slice(64 * j, 64 * j + 64)
                kw = jnp.concatenate([kp_ref[:, cols], kc_ref[:, cols]], axis=0)
                vw = jnp.concatenate([vp_ref[:, cols], vc_ref[:, cols]], axis=0)
                dkw = jnp.zeros((2 * BLOCK, 64), F32)
                dvw = jnp.zeros((2 * BLOCK, 64), F32)
                for g in range(SWA_HEADS // SWA_KV):
                    h = j * 4 + g
                    hc = slice(64 * h, 64 * h + 64)
                    _, vjp = jax.vjp(lambda a, b, c, d: _swa_head(a, b, c, d, mask), q_ref[:, hc], kw, vw,
                                     s_ref[h:h + 1, 0:1])
                    dq, dk_h, dv_h, dsink = vjp(dy_ref[:, hc].astype(F32))
                    dq_ref[:, hc] = dq.astype(F32)
                    dkw += dk_h.astype(F32)
                    dvw += dv_h.astype(F32)
                    ds_ref[h:h + 1, :] += jnp.broadcast_to(dsink, (1, 128))
                dk_ref[:, cols] = k_acc[:, cols] + dkw[:BLOCK]
                dv_ref[:, cols] = v_acc[:, cols] + dvw[:BLOCK]
                k_acc[:, cols] = dkw[BLOCK:]
                v_acc[:, cols] = dvw[BLOCK:]

        @pl.when(i == nb)
        def _():
            dk_ref[...] = k_acc[...]
            dv_ref[...] = v_acc[...]

    cur = lambda i: (jnp.minimum(i, nb - 1), 0)
    prev = lambda i: (jnp.maximum(jnp.minimum(i, nb - 1) - 1, 0), 0)
    done = lambda i: (jnp.maximum(i - 1, 0), 0)
    return pl.pallas_call(
        body, name="swa_bwd", grid=(nb + 1,),
        in_specs=[pl.BlockSpec((BLOCK, 512), cur), pl.BlockSpec((BLOCK, 128), prev), pl.BlockSpec((BLOCK, 128), cur),
                  pl.BlockSpec((BLOCK, 128), prev), pl.BlockSpec((BLOCK, 128), cur), _const_spec((8, 128)),
                  pl.BlockSpec((BLOCK, 512), cur)],
        out_specs=[pl.BlockSpec((BLOCK, 512), cur), pl.BlockSpec((BLOCK, 128), done), pl.BlockSpec((BLOCK, 128), done),
                   _const_spec((8, 128))],
        out_shape=[jax.ShapeDtypeStruct((t_len, 512), F32), jax.ShapeDtypeStruct((t_len, 128), F32),
                   jax.ShapeDtypeStruct((t_len, 128), F32), jax.ShapeDtypeStruct((8, 128), F32)],
        scratch_shapes=[pltpu.VMEM((BLOCK, 128), F32), pltpu.VMEM((BLOCK, 128), F32)],
        compiler_params=_params(("arbitrary",)))(q, k, k, v, v, sinks, dy)


def _swa_unprep(dq, dk, dv, cos, sin):
    def fn(dq_t, dk_t, dv_t, c, s):
        return _rope(dq_t, c, -s, 32), _rope(dk_t, c, -s, 32), dv_t
    t_len = dq.shape[0]
    return _rowwise("swa_unprep", fn, [dq, dk, dv, cos, sin], [],
                    [((t_len, 512), BF16), ((t_len, 128), BF16), ((t_len, 128), BF16)])


def _s5_discretize(lam_re, lam_im, log_dt, b_re, b_im):
    dt = jnp.exp(log_dt)
    mag = jnp.exp(lam_re * dt)
    ab_re, ab_im = mag * jnp.cos(lam_im * dt), mag * jnp.sin(lam_im * dt)
    den = lam_re * lam_re + lam_im * lam_im
    nr, ni = ab_re - 1.0, ab_im
    f_re = (nr * lam_re + ni * lam_im) / den
    f_im = (ni * lam_re - nr * lam_im) / den
    return ab_re, ab_im, f_re * b_re - f_im * b_im, f_re * b_im + f_im * b_re


def _s5_param_fwd(lam_re, lam_im, log_dt, b_re, b_im):
    def body(*refs):
        outs = _s5_discretize(*[r[...] for r in refs[:5]])
        for ref, val in zip(refs[5:], outs):
            ref[...] = val
    n = S5_CH
    return pl.pallas_call(body, name="s5_param_fwd",
                          out_shape=[jax.ShapeDtypeStruct((n, 1), F32)] * 2 + [jax.ShapeDtypeStruct((n, 16), F32)] * 2,
                          )(lam_re, lam_im, log_dt, b_re, b_im)


def _s5_param_bwd(lam_re, lam_im, log_dt, b_re, b_im, group_sum, d_ab_re, d_ab_im, d_bb_re, d_bb_im):
    def body(*refs):
        ins = [r[...] for r in refs[:10]]
        _, vjp = jax.vjp(_s5_discretize, *ins[:5])
        d_lr, d_li, d_ldt, d_br, d_bi = vjp(tuple(ins[6:10]))
        o_lr, o_li, o_ldt, o_br, o_bi = refs[10:]
        o_lr[...], o_li[...], o_br[...], o_bi[...] = d_lr, d_li, d_br, d_bi
        o_ldt[...] = lax.dot_general(ins[5], jnp.broadcast_to(d_ldt, (S5_CH, 128)), NN,
                                     precision=lax.Precision.HIGHEST, preferred_element_type=F32)
    n = S5_CH
    return pl.pallas_call(body, name="s5_param_bwd",
                          out_shape=[jax.ShapeDtypeStruct((n, 1), F32), jax.ShapeDtypeStruct((n, 1), F32),
                                     jax.ShapeDtypeStruct((S5_GROUPS, 128), F32),
                                     jax.ShapeDtypeStruct((n, 16), F32), jax.ShapeDtypeStruct((n, 16), F32)],
                          )(lam_re, lam_im, log_dt, b_re, b_im, group_sum, d_ab_re, d_ab_im, d_bb_re, d_bb_im)


def _s5_expand(name, u, w):
    t_len = u.shape[0]
    tile = min(512, t_len)

    def body(u_ref, w_ref, o_ref):
        o_ref[...] = _raw_dot(u_ref[...], w_ref[...], NN)

    return pl.pallas_call(
        body, name=name, grid=(t_len // tile, 8),
        in_specs=[pl.BlockSpec((tile, 128), lambda i, bb: (i, bb % 4)), pl.BlockSpec((None, 128, 512), lambda i, bb: (bb, 0, 0))],
        out_specs=pl.BlockSpec((None, tile, 512), lambda i, bb: (bb // 4, i, bb % 4)),
        out_shape=jax.ShapeDtypeStruct((2, t_len, S5_CH), F32), compiler_params=_params(("parallel", "parallel")))(u, w)


def _s5_contract(name, x, w):
    t_len = x.shape[1]
    tile = min(512, t_len)

    def body(xr_ref, xi_ref, wr_ref, wi_ref, o_ref):
        o_ref[...] = _raw_dot(xr_ref[...], wr_ref[...], NN) + _raw_dot(xi_ref[...], wi_ref[...], NN)

    return pl.pallas_call(
        body, name=name, grid=(t_len // tile, 4),
        in_specs=[pl.BlockSpec((None, tile, 512), lambda i, b: (0, i, b)), pl.BlockSpec((None, tile, 512), lambda i, b: (1, i, b)),
                  pl.BlockSpec((None, 512, 128), lambda i, b: (b, 0, 0)), pl.BlockSpec((None, 512, 128), lambda i, b: (4 + b, 0, 0))],
        out_specs=pl.BlockSpec((tile, 128), lambda i, b: (i, b)),
        out_shape=jax.ShapeDtypeStruct((t_len, 512), F32), compiler_params=_params(("parallel", "parallel")))(x, x, w, w)


def _s5_band_grad(name, narrow, wide, narrow_first):
    t_len = narrow.shape[0]

    def body(n_ref, w_ref, o_ref):
        if narrow_first:
            o_ref[...] = _raw_dot(n_ref[...], w_ref[...], TN)
        else:
            o_ref[...] = _raw_dot(w_ref[...], n_ref[...], TN)

    shape = (8, 128, 512) if narrow_first else (8, 512, 128)
    return pl.pallas_call(
        body, name=name, grid=(8,),
        in_specs=[pl.BlockSpec((t_len, 128), lambda bb: (0, bb % 4)), pl.BlockSpec((None, t_len, 512), lambda bb: (bb // 4, 0, bb % 4))],
        out_specs=pl.BlockSpec((None,) + shape[1:], lambda bb: (bb, 0, 0)),
        out_shape=jax.ShapeDtypeStruct(shape, F32), compiler_params=_params(("parallel",)))(narrow, wide)


def _s5_scan(name, bu, a_re, a_im, states=None, carry_in=None):
    t_len = bu.shape[1]
    n = t_len // SCAN_SEG
    log_n = int(round(math.log2(n)))
    assert 2 ** log_n == n
    reverse = states is not None
    lanes = SCAN_LANES // 2 if reverse else SCAN_LANES
    nblk = S5_CH // lanes

    def body(*refs):
        if reverse:
            b_ref, ar_ref, ai_ref, x_ref, cin_ref, o_ref, dar_ref, dai_ref, c_scr = refs
            xr_ref, xi_ref, cr_ref, ci_ref = x_ref.at[0], x_ref.at[1], cin_ref.at[0], cin_ref.at[1]
        else:
            b_ref, ar_ref, ai_ref, o_ref, co_ref, c_scr = refs
            cor_ref, coi_ref = co_ref.at[0], co_ref.at[1]
        br_ref, bi_ref, or_ref, oi_ref = b_ref.at[0], b_ref.at[1], o_ref.at[0], o_ref.at[1]
        ar = ar_ref[...]
        ai = -ai_ref[...] if reverse else ai_ref[...]

        def tile_at(i):
            return pl.multiple_of((n - 1 - i if reverse else i) * SCAN_SEG, SCAN_SEG)

        def advance(s, row):
            sr, si = s
            return (ar * sr - ai * si + br_ref[pl.ds(row, SCAN_SEG), :], ar * si + ai * sr + bi_ref[pl.ds(row, SCAN_SEG), :])

        zero = jnp.zeros((SCAN_SEG, lanes), F32)
        fin_r, fin_i = lax.fori_loop(0, n, lambda i, s: advance(s, tile_at(i)), (zero, zero), unroll=8)
        pr, pi = ar[0:1], ai[0:1]
        for _ in range(log_n):
            pr, pi = pr * pr - pi * pi, 2.0 * pr * pi
        er = ei = jnp.zeros((1, lanes), F32)
        for step in range(SCAN_SEG):
            j = SCAN_SEG - 1 - step if reverse else step
            c_scr[0, j:j + 1, :] = er
            c_scr[1, j:j + 1, :] = ei
            er, ei = pr * er - pi * ei + fin_r[j:j + 1], pr * ei + pi * er + fin_i[j:j + 1]
        entry = (c_scr[0], c_scr[1])

        if not reverse:
            cor_ref[...], coi_ref[...] = entry

            def step2(i, s):
                row = tile_at(i)
                s = advance(s, row)
                or_ref[pl.ds(row, SCAN_SEG), :] = s[0]
                oi_ref[pl.ds(row, SCAN_SEG), :] = s[1]
                return s
            lax.fori_loop(0, n, step2, entry, unroll=8)
        else:
            def emit(s, row, xr_prev, xi_prev, acc):
                s = advance(s, row)
                or_ref[pl.ds(row, SCAN_SEG), :] = s[0]
                oi_ref[pl.ds(row, SCAN_SEG), :] = s[1]
                return s, (acc[0] + s[0] * xr_prev + s[1] * xi_prev, acc[1] + s[1] * xr_prev - s[0] * xi_prev)

            def step2(i, carry):
                s, acc = carry
                row = tile_at(i)
                prev = pl.multiple_of(row - SCAN_SEG, SCAN_SEG)
                return emit(s, row, xr_ref[pl.ds(prev, SCAN_SEG), :], xi_ref[pl.ds(prev, SCAN_SEG), :], acc)
            s, acc = lax.fori_loop(0, n - 1, step2, (entry, (zero, zero)), unroll=8)
            s, acc = emit(s, 0, cr_ref[...], ci_ref[...], acc)
            dar_ref[...] = jnp.sum(acc[0], axis=0, keepdims=True)
            dai_ref[...] = jnp.sum(acc[1], axis=0, keepdims=True)

    planes = pl.BlockSpec((2, t_len, lanes), lambda b: (0, 0, b))
    tile8 = pl.BlockSpec((SCAN_SEG, lanes), lambda b: (0, b))
    entry_spec = pl.BlockSpec((2, SCAN_SEG, lanes), lambda b: (0, 0, b))
    row1 = pl.BlockSpec((1, lanes), lambda b: (0, b))
    seq = jax.ShapeDtypeStruct((2, t_len, S5_CH), F32)
    if reverse:
        in_specs = [planes, tile8, tile8, planes, entry_spec]
        args = (bu, a_re, a_im, states, carry_in)
        out_specs = [planes, row1, row1]
        out_shape = [seq, jax.ShapeDtypeStruct((1, S5_CH), F32), jax.ShapeDtypeStruct((1, S5_CH), F32)]
    else:
        in_specs = [planes, tile8, tile8]
        args = (bu, a_re, a_im)
        out_specs = [planes, entry_spec]
        out_shape = [seq, jax.ShapeDtypeStruct((2, SCAN_SEG, S5_CH), F32)]
    return pl.pallas_call(body, name=name, grid=(nblk,), in_specs=in_specs, out_specs=out_specs, out_shape=out_shape,
                          scratch_shapes=[pltpu.VMEM((2, SCAN_SEG, lanes), F32)],
                          compiler_params=_params(("parallel",)))(*args)


def _ret_consts():
    h = np.arange(RET_HEADS, dtype=np.float32)
    log_gamma = np.log1p(-np.exp2(-5.0 - h)).astype(np.float32)
    idx = np.arange(BLOCK, dtype=np.float32)
    diff = idx[:, None] - idx[None, :]
    decay = np.where(diff >= 0, np.exp(log_gamma[:, None, None] * np.maximum(diff, 0.0)), 0.0).astype(np.float32)
    k_w = np.exp(log_gamma[:, None] * (BLOCK - 1 - idx)[None, :]).astype(np.float32)[:, :, None]
    q_w = np.exp(log_gamma[:, None] * (idx + 1.0)[None, :]).astype(np.float32)[:, :, None]
    chunk_decay = [float(v) for v in np.exp(log_gamma * BLOCK).astype(np.float32)]
    return jnp.asarray(decay), jnp.asarray(q_w), jnp.asarray(k_w), chunk_decay


def _ret_head(q, k, v, g, prev, decay, q_w, k_w, chunk_decay):
    inner_s = _dnt(q, k) * decay
    y = _dnn(inner_s, v) + _dnn(q * q_w, prev)
    mu = jnp.mean(y, axis=-1, keepdims=True)
    var = jnp.mean(jnp.square(y - mu), axis=-1, keepdims=True)
    yn = (y - mu) * lax.rsqrt(var + EPS)
    out = g * jax.nn.sigmoid(g) * yn
    return out, prev * chunk_decay + _dtn(k * k_w, v)


def _ret_prep(proj, cos, sin):
    def fn(rq, rk, c, s):
        return _rope(rq, c, s, 32), _rope(rk, c, s, 32) * (RET_DK ** -0.5)
    t_len = proj.shape[0]
    return _rowwise("ret_prep", fn, [_seg(proj, "rq"), _seg(proj, "rk"), cos, sin], [],
                    [((t_len, 256), F32), ((t_len, 256), F32)])


def _ret_unprep(dq, dk, cos, sin):
    def fn(dq_t, dk_t, c, s):
        return _rope(dq_t, c, -s, 32), _rope(dk_t * (RET_DK ** -0.5), c, -s, 32)
    t_len = dq.shape[0]
    return _rowwise("ret_unprep", fn, [dq, dk, cos, sin], [], [((t_len, 256), BF16), ((t_len, 256), BF16)])


def _ret_fwd(q, k, proj):
    t_len = q.shape[0]
    nc = t_len // BLOCK
    decay, q_w, k_w, chunk_decay = _ret_consts()

    def body(q_ref, k_ref, v_ref, g_ref, dec_ref, qw_ref, kw_ref, o_ref, prev_ref, state):
        @pl.when(pl.program_id(0) == 0)
        def _():
            state[...] = jnp.zeros_like(state)
        for h in range(RET_HEADS):
            qk, vv = slice(64 * h, 64 * h + 64), slice(128 * h, 128 * h + 128)
            prev = state[h]
            prev_ref[h] = prev
            out, nxt = _ret_head(q_ref[:, qk], k_ref[:, qk], v_ref[:, vv], g_ref[:, vv], prev,
                                 dec_ref[h], qw_ref[h], kw_ref[h], chunk_decay[h])
            o_ref[:, vv] = out.astype(o_ref.dtype)
            state[h] = nxt

    chunk = lambda w, c=0: pl.BlockSpec((BLOCK, w), lambda i, c=c: (i, c))
    return pl.pallas_call(
        body, name="ret_fwd", grid=(nc,),
        in_specs=[chunk(256), chunk(256), chunk(512, PROJ["rv"][0] // 512), chunk(512, PROJ["rg"][0] // 512),
                  _const_spec(decay.shape), _const_spec(q_w.shape), _const_spec(k_w.shape)],
        out_specs=[chunk(512), pl.BlockSpec((None, RET_HEADS, RET_DK, RET_DV), lambda i: (i, 0, 0, 0))],
        out_shape=[jax.ShapeDtypeStruct((t_len, 512), BF16), jax.ShapeDtypeStruct((nc, RET_HEADS, RET_DK, RET_DV), F32)],
        scratch_shapes=[pltpu.VMEM((RET_HEADS, RET_DK, RET_DV), F32)],
        compiler_params=_params(("arbitrary",)))(q, k, proj, proj, decay, q_w, k_w)


def _ret_bwd(q, k, proj, prevs, dy):
    t_len = q.shape[0]
    nc = t_len // BLOCK
    decay, q_w, k_w, chunk_decay = _ret_consts()

    def body(q_ref, k_ref, v_ref, g_ref, prev_ref, dy_ref, dec_ref, qw_ref, kw_ref, dq_ref, dk_ref, dv_ref, dg_ref, dstate):
        @pl.when(pl.program_id(0) == 0)
        def _():
            dstate[...] = jnp.zeros_like(dstate)
        for h in range(RET_HEADS):
            qk, vv = slice(64 * h, 64 * h + 64), slice(128 * h, 128 * h + 128)
            head = functools.partial(_ret_head, decay=dec_ref[h], q_w=qw_ref[h], k_w=kw_ref[h], chunk_decay=chunk_decay[h])
            _, vjp = jax.vjp(head, q_ref[:, qk], k_ref[:, qk], v_ref[:, vv], g_ref[:, vv], prev_ref[h])
            dq, dk, dv, dg, dprev = vjp((dy_ref[:, vv].astype(F32), dstate[h]))
            dq_ref[:, qk], dk_ref[:, qk] = dq, dk
            dv_ref[:, vv], dg_ref[:, vv] = dv.astype(dv_ref.dtype), dg.astype(dg_ref.dtype)
            dstate[h] = dprev

    chunk = lambda w, c=0: pl.BlockSpec((BLOCK, w), lambda i, c=c: (nc - 1 - i, c))
    return pl.pallas_call(
        body, name="ret_bwd", grid=(nc,),
        in_specs=[chunk(256), chunk(256), chunk(512, PROJ["rv"][0] // 512), chunk(512, PROJ["rg"][0] // 512),
                  pl.BlockSpec((None, RET_HEADS, RET_DK, RET_DV), lambda i: (nc - 1 - i, 0, 0, 0)), chunk(512),
                  _const_spec(decay.shape), _const_spec(q_w.shape), _const_spec(k_w.shape)],
        out_specs=[chunk(256), chunk(256), chunk(512), chunk(512)],
        out_shape=[jax.ShapeDtypeStruct((t_len, 256), F32), jax.ShapeDtypeStruct((t_len, 256), F32),
                   jax.ShapeDtypeStruct((t_len, 512), BF16), jax.ShapeDtypeStruct((t_len, 512), BF16)],
        scratch_shapes=[pltpu.VMEM((RET_HEADS, RET_DK, RET_DV), F32)],
        compiler_params=_params(("arbitrary",)))(q, k, proj, proj, prevs, dy, decay, q_w, k_w)


MLA_SCALE = 96 ** -0.5


def _mla_prep(qf, kvf, proj, cos, sin):
    def fn(q, kv, kr, c, s):
        q_rope = _rope(q[:, 512:768], c, s, 16)
        k_rope = _rope(kr, c, s, 16)[:, :32]
        zero = jnp.zeros_like(k_rope)
        qs = [jnp.concatenate([q[:, 64 * h:64 * h + 64], q_rope[:, 32 * h:32 * h + 32], zero], axis=1) for h in range(MLA_HEADS)]
        ks = [jnp.concatenate([kv[:, 64 * h:64 * h + 64], k_rope, zero], axis=1) for h in range(MLA_HEADS)]
        vs = [kv[:, 512 + 64 * h:576 + 64 * h] for h in range(MLA_HEADS)]
        return qs, ks, vs
    t_len = qf.shape[0]
    return _rowwise("mla_prep", fn, [qf, kvf, _seg(proj, "kr"), cos, sin], [],
                    [((8, t_len, 128), BF16), ((8, t_len, 128), BF16), ((8, t_len, 64), BF16)])


def _mla_unprep(dq, dk, dv, cos, sin):
    def fn(dq_t, dk_t, dv_t, c, s):
        q_rope = _rope(jnp.concatenate([dq_t[h][:, 64:96] for h in range(MLA_HEADS)], axis=1), c, -s, 16)
        d_qf = jnp.concatenate([dq_t[h][:, :64] for h in range(MLA_HEADS)] + [q_rope], axis=1)
        d_kvf = jnp.concatenate([dk_t[h][:, :64] for h in range(MLA_HEADS)] + [dv_t[h] for h in range(MLA_HEADS)], axis=1)
        k_rope = dk_t[0][:, 64:96]
        for h in range(1, MLA_HEADS):
            k_rope = k_rope + dk_t[h][:, 64:96]
        d_kr = _rope(jnp.concatenate([k_rope, jnp.zeros((k_rope.shape[0], 96), F32)], axis=1), c, -s, 16)
        return d_qf, d_kvf, d_kr
    t_len = dq.shape[1]
    return _rowwise("mla_unprep", fn, [dq, dk, dv, cos, sin], [],
                    [((t_len, 768), BF16), ((t_len, 1024), BF16), ((t_len, 128), BF16)])


def _causal(i, j, tq, tk):
    row = i * tq + lax.broadcasted_iota(jnp.int32, (tq, tk), 0)
    col = j * tk + lax.broadcasted_iota(jnp.int32, (tq, tk), 1)
    return col <= row


def _mla_fwd(q, k, v):
    t_len = q.shape[1]
    tq = min(512, t_len)
    nq = t_len // tq

    def body(q_ref, k_ref, v_ref, o_ref, lse_ref, m_s, l_s, acc_s):
        i, j = pl.program_id(1), pl.program_id(2)

        @pl.when(j == 0)
        def _():
            m_s[...] = jnp.full_like(m_s, NEG_INF)
            l_s[...] = jnp.zeros_like(l_s)
            acc_s[...] = jnp.zeros_like(acc_s)

        @pl.when(j <= i)
        def _():
            mask = _causal(i, j, tq, tq)
            for e in range(2):
                s = jnp.where(mask, _raw_dot(q_ref[e], k_ref[e], NT) * MLA_SCALE, NEG_INF)
                m_old = m_s[e]
                m_new = jnp.maximum(m_old, jnp.max(s, axis=-1, keepdims=True))
                alpha = jnp.exp(m_old - m_new)
                p = jnp.exp(s - m_new)
                l_s[e] = alpha * l_s[e] + jnp.sum(p, axis=-1, keepdims=True)
                acc_s[e] = alpha * acc_s[e] + _raw_dot(p, v_ref[e], NN)
                m_s[e] = m_new

        @pl.when(j == i)
        def _():
            o_ref[...] = jnp.concatenate([acc_s[e] / l_s[e] for e in range(2)], axis=1)
            for e in range(2):
                lse_ref[e] = m_s[e] + jnp.log(l_s[e])

    kv_map = lambda hp, i, j: (hp, jnp.minimum(j, i), 0)
    return pl.pallas_call(
        body, name="mla_fwd", grid=(MLA_HEADS // 2, nq, nq),
        in_specs=[pl.BlockSpec((2, tq, 128), lambda hp, i, j: (hp, i, 0)), pl.BlockSpec((2, tq, 128), kv_map),
                  pl.BlockSpec((2, tq, 64), kv_map)],
        out_specs=[pl.BlockSpec((tq, 128), lambda hp, i, j: (i, hp)), pl.BlockSpec((2, tq, 1), lambda hp, i, j: (hp, i, 0))],
        out_shape=[jax.ShapeDtypeStruct((t_len, 512), F32), jax.ShapeDtypeStruct((MLA_HEADS, t_len, 1), F32)],
        scratch_shapes=[pltpu.VMEM((2, tq, 1), F32), pltpu.VMEM((2, tq, 1), F32), pltpu.VMEM((2, tq, 64), F32)],
        compiler_params=_params(("parallel", "parallel", "arbitrary")))(q, k, v)


def _mla_bwd_prep(o, dy):
    def fn(o_t, dy_t):
        dos, deltas = [], []
        for h in range(MLA_HEADS):
            d_h = dy_t[:, 64 * h:64 * h + 64].astype(F32)
            dos.append(d_h)
            deltas.append(jnp.sum(d_h * o_t[:, 64 * h:64 * h + 64], axis=-1, keepdims=True))
        return dos, deltas
    t_len = o.shape[0]
    return _rowwise("mla_bwd_prep", fn, [o, dy], [], [((8, t_len, 64), BF16), ((8, t_len, 1), F32)])


def _mla_bwd_tile(q, k, v, do, lse, delta, mask):
    s = _raw_dot(q, k, NT) * MLA_SCALE
    p = jnp.where(mask, jnp.exp(s - lse), 0.0)
    dp = _raw_dot(do, v, NT)
    return p, p * (dp - delta) * MLA_SCALE


def _mla_bwd_dq(q, k, v, do, lse, delta):
    t_len = q.shape[1]
    tq = min(512, t_len)
    nq = t_len // tq

    def body(q_ref, k_ref, v_ref, do_ref, lse_ref, dl_ref, dq_ref, acc):
        i, j = pl.program_id(1), pl.program_id(2)

        @pl.when(j == 0)
        def _():
            acc[...] = jnp.zeros_like(acc)

        @pl.when(j <= i)
        def _():
            _, ds = _mla_bwd_tile(q_ref[...], k_ref[...], v_ref[...], do_ref[...], lse_ref[...], dl_ref[...], _causal(i, j, tq, tq))
            acc[...] += _raw_dot(ds, k_ref[...], NN)

        @pl.when(j == i)
        def _():
            dq_ref[...] = acc[...]

    q_map = lambda h, i, j: (h, i, 0)
    kv_map = lambda h, i, j: (h, jnp.minimum(j, i), 0)
    blk = lambda w, m: pl.BlockSpec((None, tq, w), m)
    return pl.pallas_call(
        body, name="mla_bwd_dq", grid=(MLA_HEADS, nq, nq),
        in_specs=[blk(128, q_map), blk(128, kv_map), blk(64, kv_map), blk(64, q_map), blk(1, q_map), blk(1, q_map)],
        out_specs=blk(128, q_map), out_shape=jax.ShapeDtypeStruct((MLA_HEADS, t_len, 128), F32),
        scratch_shapes=[pltpu.VMEM((tq, 128), F32)],
        compiler_params=_params(("parallel", "parallel", "arbitrary")))(q, k, v, do, lse, delta)


def _mla_bwd_dkv(q, k, v, do, lse, delta):
    t_len = q.shape[1]
    tq = min(512, t_len)
    nq = t_len // tq

    def body(q_ref, k_ref, v_ref, do_ref, lse_ref, dl_ref, dk_ref, dv_ref, k_acc, v_acc):
        j, i = pl.program_id(1), pl.program_id(2)

        @pl.when(i == 0)
        def _():
            k_acc[...] = jnp.zeros_like(k_acc)
            v_acc[...] = jnp.zeros_like(v_acc)

        @pl.when(i >= j)
        def _():
            p, ds = _mla_bwd_tile(q_ref[...], k_ref[...], v_ref[...], do_ref[...], lse_ref[...], dl_ref[...], _causal(i, j, tq, tq))
            v_acc[...] += _raw_dot(p, do_ref[...], TN)
            k_acc[...] += _raw_dot(ds, q_ref[...], TN)

        @pl.when(i == nq - 1)
        def _():
            dk_ref[...] = k_acc[...]
            dv_ref[...] = v_acc[...]

    q_map = lambda h, j, i: (h, jnp.maximum(i, j), 0)
    kv_map = lambda h, j, i: (h, j, 0)
    blk = lambda w, m: pl.BlockSpec((None, tq, w), m)
    return pl.pallas_call(
        body, name="mla_bwd_dkv", grid=(MLA_HEADS, nq, nq),
        in_specs=[blk(128, q_map), blk(128, kv_map), blk(64, kv_map), blk(64, q_map), blk(1, q_map), blk(1, q_map)],
        out_specs=[blk(128, kv_map), blk(64, kv_map)],
        out_shape=[jax.ShapeDtypeStruct((MLA_HEADS, t_len, 128), F32), jax.ShapeDtypeStruct((MLA_HEADS, t_len, 64), F32)],
        scratch_shapes=[pltpu.VMEM((tq, 128), F32), pltpu.VMEM((tq, 64), F32)],
        compiler_params=_params(("parallel", "parallel", "arbitrary")))(q, k, v, do, lse, delta)


MERGE_TN = 256
GATE_BLOCK0 = PROJ["gates"][0] // MERGE_TN


def _merge_specs(tm):
    y_spec = pl.BlockSpec((tm, 512), lambda i, j: (i, 0))
    w_spec = pl.BlockSpec((512, MERGE_TN), lambda i, j: (0, j))
    gate = lambda b: pl.BlockSpec((tm, MERGE_TN), lambda i, j, b=b: (i, GATE_BLOCK0 + 4 * b + j))
    return [y_spec] * 4 + [w_spec] * 4 + [gate(b) for b in range(4)]


def _merge_fwd(ys, wbs, proj):
    t_len = proj.shape[0]
    tm = min(512, t_len)

    def body(*refs):
        y, w, gl, o_ref = refs[0:4], refs[4:8], refs[8:12], refs[12]
        acc = jnp.zeros((tm, MERGE_TN), F32)
        for b in range(4):
            acc += jax.nn.sigmoid(gl[b][...]) * _raw_dot(y[b][...], w[b][...], NN)
        o_ref[...] = acc.astype(o_ref.dtype)

    return pl.pallas_call(body, name="merge_fwd", grid=(t_len // tm, D_MODEL // MERGE_TN), in_specs=_merge_specs(tm),
                          out_specs=pl.BlockSpec((tm, MERGE_TN), lambda i, j: (i, j)),
                          out_shape=jax.ShapeDtypeStruct((t_len, D_MODEL), BF16),
                          compiler_params=_params(("parallel", "parallel")))(*ys, *wbs, *([proj] * 4))


def _merge_bwd(ys, wbs, proj, dmerged):
    t_len = proj.shape[0]
    tm = min(512, t_len)

    def body(*refs):
        y, w, gl, dm_ref, dgl, dt = refs[0:4], refs[4:8], refs[8:12], refs[12], refs[13:17], refs[17:21]
        dm = dm_ref[...]
        for b in range(4):
            gate = jax.nn.sigmoid(gl[b][...])
            t_b = _raw_dot(y[b][...], w[b][...], NN)
            dgl[b][...] = (dm * t_b * gate * (1.0 - gate)).astype(BF16)
            dt[b][...] = (dm * gate).astype(BF16)

    o_spec = pl.BlockSpec((tm, MERGE_TN), lambda i, j: (i, j))
    return pl.pallas_call(body, name="merge_bwd", grid=(t_len // tm, D_MODEL // MERGE_TN), in_specs=_merge_specs(tm) + [o_spec],
                          out_specs=[o_spec] * 8, out_shape=[jax.ShapeDtypeStruct((t_len, D_MODEL), BF16)] * 8,
                          compiler_params=_params(("parallel", "parallel")))(*ys, *wbs, *([proj] * 4), dmerged)


def _adamw(name, w, g, m, v, slabs):
    shape = w.shape
    size = int(np.prod(shape))
    if shape[-1] >= 96:
        view = (size // shape[-1], shape[-1])
    elif size % 128 == 0:
        view = (size // 128, 128)
    else:
        view = (1, size)
    rows = view[0]
    tile = rows
    for cand in (512, 256, 128, 64):
        if rows > cand and rows % cand == 0 and cand * view[1] * 4 <= 2 ** 20:
            tile = cand
            break
    if rows * view[1] * 4 <= 2 ** 20:
        tile = rows
    c1, c2 = 1.0 - ADAM_B1 ** ADAM_STEP, 1.0 - ADAM_B2 ** ADAM_STEP

    def body(w_ref, g_ref, m_ref, v_ref, og_ref, od_ref, om_ref, ov_ref):
        if slabs:
            grad = g_ref[0]
            for d in range(1, N_DEV):
                grad = grad + g_ref[d]
        else:
            grad = g_ref[...]
        m_new = ADAM_B1 * m_ref[...] + (1.0 - ADAM_B1) * grad
        v_new = ADAM_B2 * v_ref[...] + (1.0 - ADAM_B2) * jnp.square(grad)
        og_ref[...] = grad
        od_ref[...] = -ADAM_LR * ((m_new / c1) / (jnp.sqrt(v_new / c2) + ADAM_EPS) + ADAM_WD * w_ref[...])
        om_ref[...] = m_new
        ov_ref[...] = v_new

    spec = pl.BlockSpec((tile, view[1]), lambda i: (i, 0))
    g_spec = pl.BlockSpec((N_DEV, tile, view[1]), lambda i: (0, i, 0)) if slabs else spec
    g_view = g.reshape((N_DEV,) + view) if slabs else g.reshape(view)
    outs = pl.pallas_call(body, name=name, grid=(rows // tile,), in_specs=[spec, g_spec, spec, spec], out_specs=[spec] * 4,
                          out_shape=[jax.ShapeDtypeStruct(view, F32)] * 4,
                          compiler_params=_params(("parallel",)))(w.reshape(view), g_view, m.reshape(view), v.reshape(view))
    return [o.reshape(shape) for o in outs]


def _sum_slabs(name, x):
    def body(x_ref, o_ref):
        acc = x_ref[0]
        for d in range(1, N_DEV):
            acc = acc + x_ref[d]
        o_ref[...] = acc
    return pl.pallas_call(body, name=name, out_shape=jax.ShapeDtypeStruct(x.shape[1:], x.dtype))(x)


def _exchange(name, items):
    n = len(items)
    out_shape = []
    for arr, per_peer in items:
        blk = arr.shape[1:] if per_peer else arr.shape
        out_shape.append(jax.ShapeDtypeStruct((N_DEV,) + tuple(blk), arr.dtype))

    def body(*refs):
        srcs, dsts = refs[:n], refs[n:2 * n]
        send_sems, recv_sems, local_sems = refs[2 * n:]
        x, y, c = lax.axis_index("x"), lax.axis_index("y"), lax.axis_index("c")
        me = 4 * x + 2 * y + c
        started = []
        for t, (_, per_peer) in enumerate(items):
            own = pltpu.make_async_copy(srcs[t].at[me] if per_peer else srcs[t], dsts[t].at[me], local_sems.at[t])
            own.start()
            started.append(own)
        remote = []
        for k in range(1, N_DEV):
            px, py, pc = x ^ ((k >> 2) & 1), y ^ ((k >> 1) & 1), c ^ (k & 1)
            peer = 4 * px + 2 * py + pc
            for t, (_, per_peer) in enumerate(items):
                sem = t * (N_DEV - 1) + k - 1
                src = srcs[t].at[peer] if per_peer else srcs[t]
                out = pltpu.make_async_remote_copy(src_ref=src, dst_ref=dsts[t].at[me], send_sem=send_sems.at[sem],
                                                   recv_sem=recv_sems.at[sem], device_id=(px, py, pc),
                                                   device_id_type=pl.DeviceIdType.MESH)
                out.start()
                arrival = pltpu.make_async_remote_copy(src_ref=src, dst_ref=dsts[t].at[peer], send_sem=send_sems.at[sem],
                                                       recv_sem=recv_sems.at[sem], device_id=(px, py, pc),
                                                       device_id_type=pl.DeviceIdType.MESH)
                remote.append((out, arrival))
        for out, arrival in remote:
            out.wait_send()
            arrival.wait_recv()
        for own in started:
            own.wait()

    any_spec = pl.BlockSpec(memory_space=pl.ANY)
    n_sem = n * (N_DEV - 1)
    return pl.pallas_call(
        body, name=name, in_specs=[any_spec] * n, out_specs=[any_spec] * n, out_shape=out_shape,
        scratch_shapes=[pltpu.SemaphoreType.DMA((n_sem,)), pltpu.SemaphoreType.DMA((n_sem,)), pltpu.SemaphoreType.DMA((n,))],
        compiler_params=pltpu.CompilerParams(has_side_effects=True))(*[a for a, _ in items])


def _perm(a):
    t_len, w = a.shape
    return a.reshape(SCAN_SEG, t_len // SCAN_SEG, w).transpose(1, 0, 2).reshape(t_len, w)


def _unperm(a):
    t_len, w = a.shape
    return a.reshape(t_len // SCAN_SEG, SCAN_SEG, w).transpose(1, 0, 2).reshape(t_len, w)


def _band_diag(p):
    a, b = p.shape[1:]
    eye = jnp.eye(8, dtype=p.dtype)
    return (p.reshape(4, 8, a, 1, b) * eye[None, :, None, :, None]).reshape(4, 8 * a, 8 * b)


def _band_diag_extract(m, a, b):
    eye = jnp.eye(8, dtype=m.dtype)
    return (m.reshape(4, 8, a, 8, b) * eye[None, :, None, :, None]).sum(axis=3).reshape(32, a, b)


def _proj_cols(w_ref_layout):
    parts = []
    for name in PROJ_ORDER:
        off, width = REF_SPLIT[name]
        part = w_ref_layout[:, off:off + width]
        if width < PROJ[name][1]:
            part = jnp.pad(part, ((0, 0), (0, PROJ[name][1] - width)))
        parts.append(part)
    return jnp.concatenate(parts, axis=1)


def _unproj_cols(w_proj_layout):
    parts = []
    for name in REF_ORDER:
        parts.append(w_proj_layout[:, PROJ[name][0]:PROJ[name][0] + REF_SPLIT[name][1]])
    return jnp.concatenate(parts, axis=1)


def _uq_cols(w):
    w3 = w.reshape(w.shape[0], MLA_HEADS, 96)
    return jnp.concatenate([w3[:, :, :64].reshape(-1, 512), w3[:, :, 64:].reshape(-1, 256)], axis=1)


def _un_uq_cols(w):
    return jnp.concatenate([w[:, :512].reshape(-1, MLA_HEADS, 64), w[:, 512:].reshape(-1, MLA_HEADS, 32)], axis=2).reshape(-1, 768)


def _ukv_cols(w):
    w3 = w.reshape(w.shape[0], MLA_HEADS, 128)
    return jnp.concatenate([w3[:, :, :64].reshape(-1, 512), w3[:, :, 64:].reshape(-1, 512)], axis=1)


def _un_ukv_cols(w):
    return jnp.concatenate([w[:, :512].reshape(-1, MLA_HEADS, 64), w[:, 512:].reshape(-1, MLA_HEADS, 64)], axis=2).reshape(-1, 1024)


def _cols_gathered(g):
    return g.transpose(1, 0, 2).reshape(g.shape[1], -1)


def _cols_slabs(w):
    r, c = w.shape
    return w.reshape(r, N_DEV, c // N_DEV).transpose(1, 0, 2)


def _row(v):
    return v.reshape(1, -1)


def _s5_weights(p):
    flat = dict(lam_re=p["s5_lam_re"].reshape(S5_CH, 1), lam_im=p["s5_lam_im"].reshape(S5_CH, 1),
                log_dt=jnp.repeat(p["s5_log_dt"], S5_STATE).reshape(S5_CH, 1),
                b_re=p["s5_b_re"].reshape(S5_CH, S5_GROUP), b_im=p["s5_b_im"].reshape(S5_CH, S5_GROUP))
    ab_re, ab_im, bb_re, bb_im = _s5_param_fwd(flat["lam_re"], flat["lam_im"], flat["log_dt"], flat["b_re"], flat["b_im"])
    w_bu = jnp.concatenate([_band_diag(bb.reshape(32, 64, 16).transpose(0, 2, 1)) for bb in (bb_re, bb_im)], axis=0)
    w_y = jnp.concatenate([_band_diag(p["s5_c_re"].transpose(0, 2, 1)), -_band_diag(p["s5_c_im"].transpose(0, 2, 1))], axis=0)
    return dict(flat=flat,
                a_re=jnp.broadcast_to(ab_re.reshape(1, S5_CH), (SCAN_SEG, S5_CH)),
                a_im=jnp.broadcast_to(ab_im.reshape(1, S5_CH), (SCAN_SEG, S5_CH)),
                w_bu=w_bu.astype(BF16), w_bu_t=w_bu.swapaxes(1, 2).astype(BF16),
                w_y=w_y.astype(BF16), w_y_t=w_y.swapaxes(1, 2).astype(BF16))


def _layer_fwd(x, w, p, tabs):
    t_len = x.shape[0]
    cos64, sin64, cos32, sin32 = tabs
    s = {}
    (s["h1"],) = _rowwise("pre_mix", lambda xt, g: (_rms(xt, g),), [x], [_row(p["g_pre_mix"])], [((t_len, D_MODEL), BF16)])
    proj = s["proj"] = _mm("proj", s["h1"], w["w_in"], "nn", tn=256)
    s["qa"], s["ka"], s["va"] = _swa_prep(proj, cos64, sin64)
    s["sinks"] = jnp.broadcast_to(p["swa_sinks"][:, None], (8, 128))
    y_a = _swa_fwd(s["qa"], s["ka"], s["va"], s["sinks"])
    s5 = s["s5"] = _s5_weights(p)
    su = proj[:, PROJ["su"][0]:PROJ["su"][0] + 512]
    s["u_p"] = _perm(su).astype(BF16)
    bu = _s5_expand("s5_bu", s["u_p"], s5["w_bu"])
    s["xs"], s["cin"] = _s5_scan("s5_scan_fwd", bu, s5["a_re"], s5["a_im"])
    s["y_s"] = _unperm(_s5_contract("s5_y", s["xs"], s5["w_y"]))
    (s["z"],) = _rowwise("s5_gelu", lambda y, u, d: (jax.nn.gelu(y + d * u),), [s["y_s"], _seg(proj, "su")],
                         [_row(p["s5_d"])], [((t_len, 512), BF16)])
    s["zz"] = _mm("s5_glu_mm", s["z"], w["w_glu"], "nn")
    (y_b,) = _rowwise("s5_glu", lambda t: (t[:, :512] * jax.nn.sigmoid(t[:, 512:]),), [s["zz"]], [], [((t_len, 512), BF16)])
    s["qc"], s["kc"] = _ret_prep(proj, cos64, sin64)
    y_c, s["prevs"] = _ret_fwd(s["qc"], s["kc"], proj)
    s["cqn"], s["ckvn"] = _rowwise("mla_norm", lambda cq, ckv, gq, gkv: (_rms(cq, gq), _rms(ckv, gkv)),
                                   [_seg(proj, "cq"), _seg(proj, "ckv")], [_row(p["mla_g_q"]), _row(p["mla_g_kv"])],
                                   [((t_len, 256), BF16), ((t_len, 128), BF16)])
    qf = _mm("mla_uq", s["cqn"], w["w_uq"], "nn", tn=256)
    kvf = _mm("mla_ukv", s["ckvn"], w["w_ukv"], "nn")
    s["Q"], s["K"], s["V"] = _mla_prep(qf, kvf, proj, cos32, sin32)
    y_d, s["lse"] = _mla_fwd(s["Q"], s["K"], s["V"])
    s["ys"] = [y_a, y_b, y_c, y_d]
    s["merged"] = _merge_fwd(s["ys"], w["wb"], proj)
    s["o"] = _mm("out_mm", s["merged"], w["w_out"], "nn")

    def post_mix(xt, ot, g1, g2):
        x1 = xt + _rms(ot, g1)
        return x1, _rms(x1, g2)
    s["x1"], s["h2"] = _rowwise("post_mix", post_mix, [x, s["o"]], [_row(p["g_post_mix"]), _row(p["g_pre_mlp"])],
                                [((t_len, D_MODEL), F32), ((t_len, D_MODEL), BF16)])
    s["a"], s["r"] = _mm("ff1", s["h2"], w["w_ff1"], "nn", out_dtypes=(F32, BF16),
                         epi=lambda acc: (acc, jnp.square(jnp.maximum(acc, 0.0))))
    s["f"] = _mm("ff2", s["r"], w["w_ff2"], "nn")
    (x2,) = _rowwise("post_mlp", lambda x1, f, g: (x1 + _rms(f, g),), [s["x1"], s["f"]], [_row(p["g_post_mlp"])],
                     [((t_len, D_MODEL), F32)])
    return x2, s


def _layer_bwd(x, dx2, w, p, s, tabs):
    t_len = x.shape[0]
    cos64, sin64, cos32, sin32 = tabs
    proj = s["proj"]
    big, small = {}, {}

    def post_mlp_bwd(f, d, g):
        df, dg = _rms_bwd(f, g, d)
        return (df,), (dg,)
    df, small["g_post_mlp"] = _rowwise("post_mlp_bwd", post_mlp_bwd, [s["f"], dx2], [_row(p["g_post_mlp"])],
                                       [((t_len, D_MODEL), BF16)], [((1, D_MODEL), F32)])
    da = _mm("ff2_dx", df, w["w_ff2"], "nt", out_dtypes=(BF16,), epi=lambda acc, a: (acc * 2.0 * jnp.maximum(a, 0.0),),
             epi_in=(s["a"],))
    big["w_ff2"] = _mm("ff2_dw", s["r"], df, "tn")
    dh2 = _mm("ff1_dx", da, w["w_ff1"], "nt")
    big["w_ff1"] = _mm("ff1_dw", s["h2"], da, "tn")

    def post_mix_bwd(x1, o, dh, d2, g1, g2):
        dx1_n, dg2 = _rms_bwd(x1, g2, dh)
        dx1 = d2 + dx1_n
        do, dg1 = _rms_bwd(o, g1, dx1)
        return (dx1, do), (dg1, dg2)
    dx1, do, small["g_post_mix"], small["g_pre_mlp"] = _rowwise(
        "post_mix_bwd", post_mix_bwd, [s["x1"], s["o"], dh2, dx2], [_row(p["g_post_mix"]), _row(p["g_pre_mlp"])],
        [((t_len, D_MODEL), F32), ((t_len, D_MODEL), BF16)], [((1, D_MODEL), F32), ((1, D_MODEL), F32)])
    dmerged = _mm("out_dx", do, w["w_out"], "nt")
    big["w_out"] = _mm("out_dw", s["merged"], do, "tn")

    res = _merge_bwd(s["ys"], w["wb"], proj, dmerged)
    dgl, dts = res[:4], res[4:]
    dys = [_mm("branch_dx", dts[b], w["wb"][b], "nt") for b in range(4)]
    big["wb"] = [_mm("branch_dw", s["ys"][b], dts[b], "tn") for b in range(4)]
    seg = {}

    dqa, dka, dva, dsink = _swa_bwd(s["qa"], s["ka"], s["va"], s["sinks"], dys[0])
    seg["sq"], seg["sk"], seg["sv"] = _swa_unprep(dqa, dka, dva, cos64, sin64)
    small["swa_sinks"] = dsink[:, 0]

    def glu_bwd(t, d):
        za, sg = t[:, :512], jax.nn.sigmoid(t[:, 512:])
        return (jnp.concatenate([d * sg, d * za * sg * (1.0 - sg)], axis=1),)
    (dzz,) = _rowwise("s5_glu_bwd", glu_bwd, [s["zz"], dys[1]], [], [((t_len, 1024), BF16)])
    dz = _mm("s5_glu_dx", dzz, w["w_glu"], "nt")
    big["w_glu"] = _mm("s5_glu_dw", s["z"], dzz, "tn")

    def gelu_bwd(y, u, d, skip):
        _, vjp = jax.vjp(jax.nn.gelu, y + skip * u)
        (dy2,) = vjp(d)
        return (dy2, dy2 * skip), (jnp.sum(dy2 * u, axis=0, keepdims=True),)
    dy_s, du_skip, d_skip = _rowwise("s5_gelu_bwd", gelu_bwd, [s["y_s"], _seg(proj, "su"), dz], [_row(p["s5_d"])],
                                     [((t_len, 512), F32), ((t_len, 512), F32)], [((1, 512), F32)])
    small["s5_d"] = d_skip[0]
    s5 = s["s5"]
    dy_p = _perm(dy_s).astype(BF16)
    dlam_in = _s5_expand("s5_dx_in", dy_p, s5["w_y_t"])
    lam, da_re, da_im = _s5_scan("s5_scan_bwd", dlam_in, s5["a_re"], s5["a_im"], states=s["xs"], carry_in=s["cin"])
    du_s5 = _unperm(_s5_contract("s5_du", lam, s5["w_bu_t"]))
    d_w_bu = _s5_band_grad("s5_dbu", s["u_p"], lam, True)
    d_w_y = _s5_band_grad("s5_dc", dy_p, s["xs"], False)
    (seg["su"],) = _rowwise("s5_du_sum", lambda a, b: (a + b,), [du_s5, du_skip], [], [((t_len, 512), BF16)])
    d_bb_re = _band_diag_extract(d_w_bu[:4], 16, 64).transpose(0, 2, 1).reshape(S5_CH, S5_GROUP)
    d_bb_im = _band_diag_extract(d_w_bu[4:], 16, 64).transpose(0, 2, 1).reshape(S5_CH, S5_GROUP)
    small["s5_c_re"] = _band_diag_extract(d_w_y[:4], 64, 16).transpose(0, 2, 1)
    small["s5_c_im"] = -_band_diag_extract(d_w_y[4:], 64, 16).transpose(0, 2, 1)
    fl = s5["flat"]
    group_sum = jnp.repeat(jnp.eye(S5_GROUPS, dtype=F32), S5_STATE, axis=1)
    d_lr, d_li, d_ldt, d_br, d_bi = _s5_param_bwd(fl["lam_re"], fl["lam_im"], fl["log_dt"], fl["b_re"], fl["b_im"], group_sum,
                                                  da_re.reshape(S5_CH, 1), da_im.reshape(S5_CH, 1), d_bb_re, d_bb_im)
    small["s5_lam_re"], small["s5_lam_im"] = d_lr.reshape(32, 64), d_li.reshape(32, 64)
    small["s5_log_dt"] = d_ldt[:, 0]
    small["s5_b_re"], small["s5_b_im"] = d_br.reshape(32, 64, 16), d_bi.reshape(32, 64, 16)

    dqc, dkc, seg["rv"], seg["rg"] = _ret_bwd(s["qc"], s["kc"], proj, s["prevs"], dys[2])
    seg["rq"], seg["rk"] = _ret_unprep(dqc, dkc, cos64, sin64)

    do_h, delta = _mla_bwd_prep(s["ys"][3], dys[3])
    d_q = _mla_bwd_dq(s["Q"], s["K"], s["V"], do_h, s["lse"], delta)
    d_k, d_v = _mla_bwd_dkv(s["Q"], s["K"], s["V"], do_h, s["lse"], delta)
    dqf, dkvf, seg["kr"] = _mla_unprep(d_q, d_k, d_v, cos32, sin32)
    dcqn = _mm("mla_uq_dx", dqf, w["w_uq"], "nt")
    big["w_uq"] = _mm("mla_uq_dw", s["cqn"], dqf, "tn", tn=256)
    dckvn = _mm("mla_ukv_dx", dkvf, w["w_ukv"], "nt")
    big["w_ukv"] = _mm("mla_ukv_dw", s["ckvn"], dkvf, "tn")

    def mla_norm_bwd(cq, ckv, d1, d2, gq, gkv):
        dcq, dgq = _rms_bwd(cq, gq, d1)
        dckv, dgkv = _rms_bwd(ckv, gkv, d2)
        return (dcq, dckv), (dgq, dgkv)
    seg["cq"], seg["ckv"], dgq, dgkv = _rowwise(
        "mla_norm_bwd", mla_norm_bwd, [_seg(proj, "cq"), _seg(proj, "ckv"), dcqn, dckvn],
        [_row(p["mla_g_q"]), _row(p["mla_g_kv"])], [((t_len, 256), BF16), ((t_len, 128), BF16)],
        [((1, 256), F32), ((1, 128), F32)])
    small["mla_g_q"], small["mla_g_kv"] = dgq[0], dgkv[0]

    dproj = jnp.concatenate([seg[name] for name in PROJ_ORDER[:-1]] + list(dgl), axis=1)
    dh1 = _mm("proj_dx", dproj, w["w_in"], "nt")
    big["w_in"] = _mm("proj_dw", s["h1"], dproj, "tn", tn=256)

    def pre_mix_bwd(xt, dh, d1, g):
        dxn, dg = _rms_bwd(xt, g, dh)
        return (d1 + dxn,), (dg,)
    dx, small["g_pre_mix"] = _rowwise("pre_mix_bwd", pre_mix_bwd, [x, dh1, dx1], [_row(p["g_pre_mix"])],
                                      [((t_len, D_MODEL), F32)], [((1, D_MODEL), F32)])
    for name in ("g_pre_mix", "g_post_mix", "g_pre_mlp", "g_post_mlp"):
        small[name] = small[name][0]
    return dx, big, small


BIG = ("w_in", "s5_w_glu", "mla_w_uq", "mla_w_ukv", "w_branch", "w_out", "w_ff1", "w_ff2")
SMALL = ("g_pre_mix", "g_post_mix", "g_pre_mlp", "g_post_mlp", "swa_sinks", "s5_lam_re", "s5_lam_im", "s5_log_dt",
         "s5_b_re", "s5_b_im", "s5_c_re", "s5_c_im", "s5_d", "mla_g_q", "mla_g_kv")
WEIGHTS = ("g_pre_mix", "g_post_mix", "g_pre_mlp", "g_post_mlp", "w_in", "swa_sinks", "s5_lam_re", "s5_lam_im", "s5_log_dt",
           "s5_b_re", "s5_b_im", "s5_c_re", "s5_c_im", "s5_d", "s5_w_glu", "mla_g_q", "mla_g_kv", "mla_w_uq", "mla_w_ukv",
           "w_branch", "w_out", "w_ff1", "w_ff2")


def _layer_weights(gathered, l):
    g = {name: gathered[name][:, l] for name in BIG}
    wb = g["w_branch"].transpose(1, 2, 0, 3).reshape(4, 512, D_MODEL)
    return dict(w_in=_proj_cols(_cols_gathered(g["w_in"])), w_glu=_cols_gathered(g["s5_w_glu"]),
                w_uq=_uq_cols(_cols_gathered(g["mla_w_uq"])), w_ukv=_ukv_cols(_cols_gathered(g["mla_w_ukv"])),
                wb=[wb[b] for b in range(4)], w_out=g["w_out"].reshape(D_MODEL, D_MODEL),
                w_ff1=_cols_gathered(g["w_ff1"]), w_ff2=g["w_ff2"].reshape(4 * D_MODEL, D_MODEL))


def _grad_slabs(big):
    wb = jnp.stack(big["wb"])
    return dict(w_in=_cols_slabs(_unproj_cols(big["w_in"])), s5_w_glu=_cols_slabs(big["w_glu"]),
                mla_w_uq=_cols_slabs(_un_uq_cols(big["w_uq"])), mla_w_ukv=_cols_slabs(_un_ukv_cols(big["w_ukv"])),
                w_branch=wb.reshape(4, 512, N_DEV, D_MODEL // N_DEV).transpose(2, 0, 1, 3),
                w_out=big["w_out"].reshape(N_DEV, D_MODEL // N_DEV, D_MODEL),
                w_ff1=_cols_slabs(big["w_ff1"]), w_ff2=big["w_ff2"].reshape(N_DEV, 4 * D_MODEL // N_DEV, D_MODEL))


def kernel(x, g_pre_mix, g_post_mix, g_pre_mlp, g_post_mlp, w_in, swa_sinks, s5_lam_re, s5_lam_im, s5_log_dt, s5_b_re, s5_b_im, s5_c_re, s5_c_im, s5_d, s5_w_glu, mla_g_q, mla_g_kv, mla_w_uq, mla_w_ukv, w_branch, w_out, w_ff1, w_ff2, loss_target, m_g_pre_mix, m_g_post_mix, m_g_pre_mlp, m_g_post_mlp, m_w_in, m_swa_sinks, m_s5_lam_re, m_s5_lam_im, m_s5_log_dt, m_s5_b_re, m_s5_b_im, m_s5_c_re, m_s5_c_im, m_s5_d, m_s5_w_glu, m_mla_g_q, m_mla_g_kv, m_mla_w_uq, m_mla_w_ukv, m_w_branch, m_w_out, m_w_ff1, m_w_ff2, v_g_pre_mix, v_g_post_mix, v_g_pre_mlp, v_g_post_mlp, v_w_in, v_swa_sinks, v_s5_lam_re, v_s5_lam_im, v_s5_log_dt, v_s5_b_re, v_s5_b_im, v_s5_c_re, v_s5_c_im, v_s5_d, v_s5_w_glu, v_mla_g_q, v_mla_g_kv, v_mla_w_uq, v_mla_w_ukv, v_w_branch, v_w_out, v_w_ff1, v_w_ff2):
    weights = dict(g_pre_mix=g_pre_mix, g_post_mix=g_post_mix, g_pre_mlp=g_pre_mlp, g_post_mlp=g_post_mlp, w_in=w_in,
                   swa_sinks=swa_sinks, s5_lam_re=s5_lam_re, s5_lam_im=s5_lam_im, s5_log_dt=s5_log_dt, s5_b_re=s5_b_re,
                   s5_b_im=s5_b_im, s5_c_re=s5_c_re, s5_c_im=s5_c_im, s5_d=s5_d, s5_w_glu=s5_w_glu, mla_g_q=mla_g_q,
                   mla_g_kv=mla_g_kv, mla_w_uq=mla_w_uq, mla_w_ukv=mla_w_ukv, w_branch=w_branch, w_out=w_out, w_ff1=w_ff1,
                   w_ff2=w_ff2)
    m_in = dict(g_pre_mix=m_g_pre_mix, g_post_mix=m_g_post_mix, g_pre_mlp=m_g_pre_mlp, g_post_mlp=m_g_post_mlp, w_in=m_w_in,
                swa_sinks=m_swa_sinks, s5_lam_re=m_s5_lam_re, s5_lam_im=m_s5_lam_im, s5_log_dt=m_s5_log_dt,
                s5_b_re=m_s5_b_re, s5_b_im=m_s5_b_im, s5_c_re=m_s5_c_re, s5_c_im=m_s5_c_im, s5_d=m_s5_d,
                s5_w_glu=m_s5_w_glu, mla_g_q=m_mla_g_q, mla_g_kv=m_mla_g_kv, mla_w_uq=m_mla_w_uq, mla_w_ukv=m_mla_w_ukv,
                w_branch=m_w_branch, w_out=m_w_out, w_ff1=m_w_ff1, w_ff2=m_w_ff2)
    v_in = dict(g_pre_mix=v_g_pre_mix, g_post_mix=v_g_post_mix, g_pre_mlp=v_g_pre_mlp, g_post_mlp=v_g_post_mlp, w_in=v_w_in,
                swa_sinks=v_swa_sinks, s5_lam_re=v_s5_lam_re, s5_lam_im=v_s5_lam_im, s5_log_dt=v_s5_log_dt,
                s5_b_re=v_s5_b_re, s5_b_im=v_s5_b_im, s5_c_re=v_s5_c_re, s5_c_im=v_s5_c_im, s5_d=v_s5_d,
                s5_w_glu=v_s5_w_glu, mla_g_q=v_mla_g_q, mla_g_kv=v_mla_g_kv, mla_w_uq=v_mla_w_uq, mla_w_ukv=v_mla_w_ukv,
                w_branch=v_w_branch, w_out=v_w_out, w_ff1=v_w_ff1, w_ff2=v_w_ff2)
    depth = g_pre_mix.shape[0]
    t_len = x.shape[1]
    tabs = _rope_tables(t_len, 32) + _rope_tables(t_len, 16)

    recv = _exchange("gather_weights", [(weights[name].astype(BF16), False) for name in BIG])
    gathered = dict(zip(BIG, recv))

    xs, saved, layer_w, layer_p = [x[0]], [], [], []
    for l in range(depth):
        layer_w.append(_layer_weights(gathered, l))
        layer_p.append({name: weights[name][l] for name in SMALL})
        x_next, s = _layer_fwd(xs[-1], layer_w[l], layer_p[l], tabs)
        xs.append(x_next)
        saved.append(s)

    def loss_fn(y, tgt):
        err = y - tgt
        part = 0.5 * jnp.sum(jnp.mean(err * err, axis=-1, keepdims=True), axis=0, keepdims=True)
        return (err * (1.0 / D_MODEL),), (jnp.broadcast_to(part, (8, 128)),)
    dx, loss_part = _rowwise("loss", loss_fn, [xs[-1], loss_target[0]], [], [((t_len, D_MODEL), F32)], [((8, 128), F32)])
    loss = lax.psum(loss_part[0, 0], ("x", "y", "c"))

    big_grads, small_grads = [None] * depth, [None] * depth
    for l in reversed(range(depth)):
        dx, big, small = _layer_bwd(xs[l], dx, layer_w[l], layer_p[l], saved[l], tabs)
        big_grads[l], small_grads[l] = _grad_slabs(big), small

    small_vec = jnp.concatenate([jnp.stack([small_grads[l][name] for l in range(depth)]).reshape(-1) for name in SMALL])
    n_small = small_vec.shape[0]
    rows = -(-n_small // 1024) * 8
    small_mat = jnp.pad(small_vec, (0, rows * 128 - n_small)).reshape(rows, 128)
    items = [(jnp.stack([big_grads[l][name] for l in range(depth)], axis=1), True) for name in BIG] + [(small_mat, False)]
    recv = _exchange("exchange_grads", items)
    small_sum = _sum_slabs("sum_small_grads", recv[-1]).reshape(-1)

    grads, deltas, new_m, new_v = {}, {}, {}, {}
    for name, slabs in zip(BIG, recv[:-1]):
        grads[name], deltas[name], new_m[name], new_v[name] = _adamw("adamw_" + name, weights[name], slabs, m_in[name],
                                                                     v_in[name], True)
    off = 0
    for name in SMALL:
        size = int(np.prod(weights[name].shape))
        g = small_sum[off:off + size].reshape(weights[name].shape)
        off += size
        grads[name], deltas[name], new_m[name], new_v[name] = _adamw("adamw_" + name, weights[name], g, m_in[name],
                                                                     v_in[name], False)
    return (loss, dx[None], *[grads[n] for n in WEIGHTS], *[deltas[n] for n in WEIGHTS], *[new_m[n] for n in WEIGHTS],
            *[new_v[n] for n in WEIGHTS])
```

```python
import functools
import math

import numpy as np
import jax
import jax.numpy as jnp
from jax import lax
from jax.experimental import pallas as pl
from jax.experimental.pallas import tpu as pltpu

F32, BF16 = jnp.float32, jnp.bfloat16
EPS = 1e-6
NEG_INF = -1e30
ROPE_THETA = 10000.0
D_MODEL = 1024
N_DEV = 8
BLOCK = 128
SWA_HEADS, SWA_KV = 8, 2
RET_HEADS, RET_DK, RET_DV = 4, 64, 128
MLA_HEADS = 8
S5_GROUPS, S5_STATE, S5_GROUP = 32, 64, 16
S5_CH = S5_GROUPS * S5_STATE
SCAN_LANES = 256
SCAN_SEG = 8
ADAM_LR, ADAM_B1, ADAM_B2, ADAM_EPS, ADAM_WD, ADAM_STEP = 0.001, 0.9, 0.999, 1e-08, 0.01, 10
VMEM_MB = 56

PROJ = dict(sq=(0, 512), su=(512, 512), rv=(1024, 512), rg=(1536, 512), rq=(2048, 256), rk=(2304, 256),
            cq=(2560, 256), sk=(2816, 128), sv=(2944, 128), ckv=(3072, 128), kr=(3200, 128), gates=(3328, 4096))
PROJ_W = 7424
REF_SPLIT = dict(sq=(0, 512), sk=(512, 128), sv=(640, 128), su=(768, 512), rq=(1280, 256), rk=(1536, 256),
                 rv=(1792, 512), rg=(2304, 512), cq=(2816, 256), ckv=(3072, 128), kr=(3200, 32), gates=(3232, 4096))
PROJ_ORDER = ("sq", "su", "rv", "rg", "rq", "rk", "cq", "sk", "sv", "ckv", "kr", "gates")
REF_ORDER = ("sq", "sk", "sv", "su", "rq", "rk", "rv", "rg", "cq", "ckv", "kr", "gates")

NN = (((1,), (0,)), ((), ()))
NT = (((1,), (1,)), ((), ()))
TN = (((0,), (0,)), ((), ()))


def _params(sem):
    return pltpu.CompilerParams(dimension_semantics=sem, vmem_limit_bytes=VMEM_MB * 2 ** 20)


def _raw_dot(a, b, dims):
    return lax.dot_general(a.astype(BF16), b.astype(BF16), dims, preferred_element_type=F32)


@jax.custom_vjp
def _dnn(a, b):
    return _raw_dot(a, b, NN)


@jax.custom_vjp
def _dnt(a, b):
    return _raw_dot(a, b, NT)


@jax.custom_vjp
def _dtn(a, b):
    return _raw_dot(a, b, TN)


_dnn.defvjp(lambda a, b: (_dnn(a, b), (a, b)),
            lambda r, g: (_dnt(g, r[1]).astype(r[0].dtype), _dtn(r[0], g).astype(r[1].dtype)))
_dnt.defvjp(lambda a, b: (_dnt(a, b), (a, b)),
            lambda r, g: (_dnn(g, r[1]).astype(r[0].dtype), _dtn(g, r[0]).astype(r[1].dtype)))
_dtn.defvjp(lambda a, b: (_dtn(a, b), (a, b)),
            lambda r, g: (_dnt(r[1], g).astype(r[0].dtype), _dnn(r[0], g).astype(r[1].dtype)))


class _Cols:
    def __init__(self, arr, width, index):
        self.arr, self.width, self.index = arr, width, index


def _seg(proj, name):
    off, width = PROJ[name]
    return _Cols(proj, width, off // width)


def _row_spec(shape, tile):
    lead = len(shape) - 2
    return pl.BlockSpec(tuple(shape[:-2]) + (tile, shape[-1]), lambda i, lead=lead: (0,) * lead + (i, 0))


def _const_spec(shape):
    nd = len(shape)
    return pl.BlockSpec(tuple(shape), lambda *_, nd=nd: (0,) * nd)


def _rowwise(name, fn, rows, consts, out_rows, out_accs=(), tile=256):
    arrs = [r.arr if isinstance(r, _Cols) else r for r in rows]
    t_len = arrs[0].shape[-2]
    tile = min(tile, t_len)
    n = t_len // tile
    in_specs = []
    for r in rows:
        if isinstance(r, _Cols):
            in_specs.append(pl.BlockSpec((tile, r.width), lambda i, c=r.index: (i, c)))
        else:
            in_specs.append(_row_spec(r.shape, tile))
    in_specs += [_const_spec(c.shape) for c in consts]
    out_specs = [_row_spec(s, tile) for s, _ in out_rows] + [_const_spec(s) for s, _ in out_accs]
    out_shape = [jax.ShapeDtypeStruct(s, d) for s, d in tuple(out_rows) + tuple(out_accs)]
    n_in, n_or = len(rows) + len(consts), len(out_rows)

    def body(*refs):
        res = fn(*[r[...] for r in refs[:n_in]])
        row_out, acc_out = res if out_accs else (res, ())
        for ref, val in zip(refs[n_in:n_in + n_or], row_out):
            if isinstance(val, (list, tuple)):
                for h, v_h in enumerate(val):
                    ref[h] = v_h.astype(ref.dtype)
            else:
                ref[...] = val.astype(ref.dtype)
        first = pl.program_id(0) == 0
        for ref, val in zip(refs[n_in + n_or:], acc_out):
            @pl.when(first)
            def _(ref=ref, val=val):
                ref[...] = val

            @pl.when(jnp.logical_not(first))
            def _(ref=ref, val=val):
                ref[...] += val

    return pl.pallas_call(body, name=name, grid=(n,), in_specs=in_specs, out_specs=out_specs, out_shape=out_shape,
                          compiler_params=_params(("arbitrary",)))(*arrs, *consts)


def _mm(name, a, b, mode, out_dtypes=(F32,), tm=512, tn=512, epi=None, epi_in=()):
    if mode == "nn":
        (m, k), n = a.shape, b.shape[1]
    elif mode == "nt":
        (m, k), n = a.shape, b.shape[0]
    else:
        (k, m), n = a.shape, b.shape[1]
    tm, tn = min(tm, m), min(tn, n)
    assert m % tm == 0 and n % tn == 0, (name, m, n, tm, tn)
    a_spec = pl.BlockSpec((k, tm), lambda i, j: (0, i)) if mode == "tn" else pl.BlockSpec((tm, k), lambda i, j: (i, 0))
    b_spec = pl.BlockSpec((tn, k), lambda i, j: (j, 0)) if mode == "nt" else pl.BlockSpec((k, tn), lambda i, j: (0, j))
    dims = dict(nn=NN, nt=NT, tn=TN)[mode]
    o_spec = pl.BlockSpec((tm, tn), lambda i, j: (i, j))
    n_e = len(epi_in)

    def body(a_ref, b_ref, *refs):
        acc = _raw_dot(a_ref[...], b_ref[...], dims)
        outs = epi(acc, *[r[...] for r in refs[:n_e]]) if epi is not None else (acc,)
        for ref, val in zip(refs[n_e:], outs):
            ref[...] = val.astype(ref.dtype)

    res = pl.pallas_call(body, name=name, grid=(m // tm, n // tn),
                         in_specs=[a_spec, b_spec] + [o_spec] * n_e, out_specs=[o_spec] * len(out_dtypes),
                         out_shape=[jax.ShapeDtypeStruct((m, n), d) for d in out_dtypes],
                         compiler_params=_params(("parallel", "parallel")))(a, b, *epi_in)
    return res[0] if len(out_dtypes) == 1 else res


def _rms(x, g):
    return x * lax.rsqrt(jnp.mean(x * x, axis=-1, keepdims=True) + EPS) * g


def _rms_bwd(x, g, dy):
    _, vjp = jax.vjp(_rms, x, g)
    return vjp(dy)


def _rope_tables(t_len, half):
    dim = 2 * half
    inv = 1.0 / (ROPE_THETA ** (jnp.arange(0, dim, 2, dtype=F32) / dim))
    ang = jnp.arange(t_len, dtype=F32)[:, None] * inv[None, :]
    c, s = jnp.cos(ang), jnp.sin(ang)
    reps = 128 // dim
    return jnp.tile(jnp.concatenate([c, c], 1), (1, reps)), jnp.tile(jnp.concatenate([-s, s], 1), (1, reps))


def _rope(x, cos, sin, half):
    w = x.shape[-1]
    reps = w // 128
    if reps > 1:
        cos, sin = jnp.tile(cos, (1, reps)), jnp.tile(sin, (1, reps))
    lane = lax.broadcasted_iota(jnp.int32, x.shape, 1)
    partner = jnp.where((lane % (2 * half)) < half, pltpu.roll(x, w - half, 1), pltpu.roll(x, half, 1))
    return x * cos + partner * sin


def _swa_mask(i):
    qi = lax.broadcasted_iota(jnp.int32, (BLOCK, 2 * BLOCK), 0) + BLOCK
    kj = lax.broadcasted_iota(jnp.int32, (BLOCK, 2 * BLOCK), 1)
    diff = qi - kj
    return (diff >= 0) & (diff < BLOCK) & ((i > 0) | (kj >= BLOCK))


def _swa_head(q, kw, vw, sink, mask):
    sc = _dnt(q, kw) * (64 ** -0.5)
    sc = jnp.where(mask, sc, NEG_INF)
    m = lax.stop_gradient(jnp.maximum(jnp.max(sc, axis=-1, keepdims=True), sink))
    p = jnp.exp(sc - m)
    denom = jnp.sum(p, axis=-1, keepdims=True) + jnp.exp(sink - m)
    return _dnn(p / denom, vw)


def _swa_prep(proj, cos, sin):
    def fn(sq, sk, sv, c, s):
        return _rope(sq, c, s, 32), _rope(sk, c, s, 32), sv
    t_len = proj.shape[0]
    return _rowwise("swa_prep", fn, [_seg(proj, "sq"), _seg(proj, "sk"), _seg(proj, "sv"), cos, sin], [],
                    [((t_len, 512), BF16), ((t_len, 128), BF16), ((t_len, 128), BF16)])


def _swa_fwd(q, k, v, sinks):
    t_len = q.shape[0]
    nb = t_len // BLOCK

    def body(q_ref, kp_ref, kc_ref, vp_ref, vc_ref, s_ref, o_ref):
        mask = _swa_mask(pl.program_id(0))
        for j in range(SWA_KV):
            cols = slice(64 * j, 64 * j + 64)
            kw = jnp.concatenate([kp_ref[:, cols], kc_ref[:, cols]], axis=0)
            vw = jnp.concatenate([vp_ref[:, cols], vc_ref[:, cols]], axis=0)
            for g in range(SWA_HEADS // SWA_KV):
                h = j * 4 + g
                o = _swa_head(q_ref[:, 64 * h:64 * h + 64], kw, vw, s_ref[h:h + 1, 0:1], mask)
                o_ref[:, 64 * h:64 * h + 64] = o.astype(o_ref.dtype)

    prev = lambda i: (jnp.maximum(i - 1, 0), 0)
    cur = lambda i: (i, 0)
    return pl.pallas_call(
        body, name="swa_fwd", grid=(nb,),
        in_specs=[pl.BlockSpec((BLOCK, 512), cur), pl.BlockSpec((BLOCK, 128), prev), pl.BlockSpec((BLOCK, 128), cur),
                  pl.BlockSpec((BLOCK, 128), prev), pl.BlockSpec((BLOCK, 128), cur), _const_spec((8, 128))],
        out_specs=pl.BlockSpec((BLOCK, 512), cur), out_shape=jax.ShapeDtypeStruct((t_len, 512), BF16),
        compiler_params=_params(("parallel",)))(q, k, k, v, v, sinks)


def _swa_bwd(q, k, v, sinks, dy):
    t_len = q.shape[0]
    nb = t_len // BLOCK

    def body(q_ref, kp_ref, kc_ref, vp_ref, vc_ref, s_ref, dy_ref, dq_ref, dk_ref, dv_ref, ds_ref, k_acc, v_acc):
        i = pl.program_id(0)

        @pl.when(i == 0)
        def _():
            k_acc[...] = jnp.zeros_like(k_acc)
            v_acc[...] = jnp.zeros_like(v_acc)
            ds_ref[...] = jnp.zeros_like(ds_ref)

        @pl.when(i < nb)
        def _():
            mask = _swa_mask(i)
            for j in range(SWA_KV):
                cols = slice(64 * j, 64 * j + 64)
                kw = jnp.concatenate([kp_ref[:, cols], kc_ref[:, cols]], axis=0)
                vw = jnp.concatenate([vp_ref[:, cols], vc_ref[:, cols]], axis=0)
                dkw = jnp.zeros((2 * BLOCK, 64), F32)
                dvw = jnp.zeros((2 * BLOCK, 64), F32)
                for g in range(SWA_HEADS // SWA_KV):
                    h = j * 4 + g
                    hc = slice(64 * h, 64 * h + 64)
                    _, vjp = jax.vjp(lambda a, b, c, d: _swa_head(a, b, c, d, mask), q_ref[:, hc], kw, vw,
                                     s_ref[h:h + 1, 0:1])
                    dq, dk_h, dv_h, dsink = vjp(dy_ref[:, hc].astype(F32))
                    dq_ref[:, hc] = dq.astype(F32)
                    dkw += dk_h.astype(F32)
                    dvw += dv_h.astype(F32)
                    ds_ref[h:h + 1, :] += jnp.broadcast_to(dsink, (1, 128))
                dk_ref[:, cols] = k_acc[:, cols] + dkw[:BLOCK]
                dv_ref[:, cols] = v_acc[:, cols] + dvw[:BLOCK]
                k_acc[:, cols] = dkw[BLOCK:]
                v_acc[:, cols] = dvw[BLOCK:]

        @pl.when(i == nb)
        def _():
            dk_ref[...] = k_acc[...]
            dv_ref[...] = v_acc[...]

    cur = lambda i: (jnp.minimum(i, nb - 1), 0)
    prev = lambda i: (jnp.maximum(jnp.minimum(i, nb - 1) - 1, 0), 0)
    done = lambda i: (jnp.maximum(i - 1, 0), 0)
    return pl.pallas_call(
        body, name="swa_bwd", grid=(nb + 1,),
        in_specs=[pl.BlockSpec((BLOCK, 512), cur), pl.BlockSpec((BLOCK, 128), prev), pl.BlockSpec((BLOCK, 128), cur),
                  pl.BlockSpec((BLOCK, 128), prev), pl.BlockSpec((BLOCK, 128), cur), _const_spec((8, 128)),
                  pl.BlockSpec((BLOCK, 512), cur)],
        out_specs=[pl.BlockSpec((BLOCK, 512), cur), pl.BlockSpec((BLOCK, 128), done), pl.BlockSpec((BLOCK, 128), done),
                   _const_spec((8, 128))],
        out_shape=[jax.ShapeDtypeStruct((t_len, 512), F32), jax.ShapeDtypeStruct((t_len, 128), F32),
                   jax.ShapeDtypeStruct((t_len, 128), F32), jax.ShapeDtypeStruct((8, 128), F32)],
        scratch_shapes=[pltpu.VMEM((BLOCK, 128), F32), pltpu.VMEM((BLOCK, 128), F32)],
        compiler_params=_params(("arbitrary",)))(q, k, k, v, v, sinks, dy)


def _swa_unprep(dq, dk, dv, cos, sin):
    def fn(dq_t, dk_t, dv_t, c, s):
        return _rope(dq_t, c, -s, 32), _rope(dk_t, c, -s, 32), dv_t
    t_len = dq.shape[0]
    return _rowwise("swa_unprep", fn, [dq, dk, dv, cos, sin], [],
                    [((t_len, 512), BF16), ((t_len, 128), BF16), ((t_len, 128), BF16)])


def _s5_discretize(lam_re, lam_im, log_dt, b_re, b_im):
    dt = jnp.exp(log_dt)
    mag = jnp.exp(lam_re * dt)
    ab_re, ab_im = mag * jnp.cos(lam_im * dt), mag * jnp.sin(lam_im * dt)
    den = lam_re * lam_re + lam_im * lam_im
    nr, ni = ab_re - 1.0, ab_im
    f_re = (nr * lam_re + ni * lam_im) / den
    f_im = (ni * lam_re - nr * lam_im) / den
    return ab_re, ab_im, f_re * b_re - f_im * b_im, f_re * b_im + f_im * b_re


def _s5_param_fwd(lam_re, lam_im, log_dt, b_re, b_im):
    def body(*refs):
        outs = _s5_discretize(*[r[...] for r in refs[:5]])
        for ref, val in zip(refs[5:], outs):
            ref[...] = val
    n = S5_CH
    return pl.pallas_call(body, name="s5_param_fwd",
                          out_shape=[jax.ShapeDtypeStruct((n, 1), F32)] * 2 + [jax.ShapeDtypeStruct((n, 16), F32)] * 2,
                          )(lam_re, lam_im, log_dt, b_re, b_im)


def _s5_param_bwd(lam_re, lam_im, log_dt, b_re, b_im, group_sum, d_ab_re, d_ab_im, d_bb_re, d_bb_im):
    def body(*refs):
        ins = [r[...] for r in refs[:10]]
        _, vjp = jax.vjp(_s5_discretize, *ins[:5])
        d_lr, d_li, d_ldt, d_br, d_bi = vjp(tuple(ins[6:10]))
        o_lr, o_li, o_ldt, o_br, o_bi = refs[10:]
        o_lr[...], o_li[...], o_br[...], o_bi[...] = d_lr, d_li, d_br, d_bi
        o_ldt[...] = lax.dot_general(ins[5], jnp.broadcast_to(d_ldt, (S5_CH, 128)), NN,
                                     precision=lax.Precision.HIGHEST, preferred_element_type=F32)
    n = S5_CH
    return pl.pallas_call(body, name="s5_param_bwd",
                          out_shape=[jax.ShapeDtypeStruct((n, 1), F32), jax.ShapeDtypeStruct((n, 1), F32),
                                     jax.ShapeDtypeStruct((S5_GROUPS, 128), F32),
                                     jax.ShapeDtypeStruct((n, 16), F32), jax.ShapeDtypeStruct((n, 16), F32)],
                          )(lam_re, lam_im, log_dt, b_re, b_im, group_sum, d_ab_re, d_ab_im, d_bb_re, d_bb_im)


def _s5_expand(name, u, w):
    t_len = u.shape[0]
    tile = min(512, t_len)

    def body(u_ref, w_ref, o_ref):
        o_ref[...] = _raw_dot(u_ref[...], w_ref[...], NN)

    return pl.pallas_call(
        body, name=name, grid=(t_len // tile, 8),
        in_specs=[pl.BlockSpec((tile, 128), lambda i, bb: (i, bb % 4)), pl.BlockSpec((None, 128, 512), lambda i, bb: (bb, 0, 0))],
        out_specs=pl.BlockSpec((None, tile, 512), lambda i, bb: (bb // 4, i, bb % 4)),
        out_shape=jax.ShapeDtypeStruct((2, t_len, S5_CH), F32), compiler_params=_params(("parallel", "parallel")))(u, w)


def _s5_contract(name, x, w):
    t_len = x.shape[1]
    tile = min(512, t_len)

    def body(xr_ref, xi_ref, wr_ref, wi_ref, o_ref):
        o_ref[...] = _raw_dot(xr_ref[...], wr_ref[...], NN) + _raw_dot(xi_ref[...], wi_ref[...], NN)

    return pl.pallas_call(
        body, name=name, grid=(t_len // tile, 4),
        in_specs=[pl.BlockSpec((None, tile, 512), lambda i, b: (0, i, b)), pl.BlockSpec((None, tile, 512), lambda i, b: (1, i, b)),
                  pl.BlockSpec((None, 512, 128), lambda i, b: (b, 0, 0)), pl.BlockSpec((None, 512, 128), lambda i, b: (4 + b, 0, 0))],
        out_specs=pl.BlockSpec((tile, 128), lambda i, b: (i, b)),
        out_shape=jax.ShapeDtypeStruct((t_len, 512), F32), compiler_params=_params(("parallel", "parallel")))(x, x, w, w)


def _s5_band_grad(name, narrow, wide, narrow_first):
    t_len = narrow.shape[0]

    def body(n_ref, w_ref, o_ref):
        if narrow_first:
            o_ref[...] = _raw_dot(n_ref[...], w_ref[...], TN)
        else:
            o_ref[...] = _raw_dot(w_ref[...], n_ref[...], TN)

    shape = (8, 128, 512) if narrow_first else (8, 512, 128)
    return pl.pallas_call(
        body, name=name, grid=(8,),
        in_specs=[pl.BlockSpec((t_len, 128), lambda bb: (0, bb % 4)), pl.BlockSpec((None, t_len, 512), lambda bb: (bb // 4, 0, bb % 4))],
        out_specs=pl.BlockSpec((None,) + shape[1:], lambda bb: (bb, 0, 0)),
        out_shape=jax.ShapeDtypeStruct(shape, F32), compiler_params=_params(("parallel",)))(narrow, wide)


def _s5_scan(name, bu, a_re, a_im, states=None, carry_in=None):
    t_len = bu.shape[1]
    n = t_len // SCAN_SEG
    log_n = int(round(math.log2(n)))
    assert 2 ** log_n == n
    reverse = states is not None
    lanes = SCAN_LANES // 2 if reverse else SCAN_LANES
    nblk = S5_CH // lanes

    def body(*refs):
        if reverse:
            b_ref, ar_ref, ai_ref, x_ref, cin_ref, o_ref, dar_ref, dai_ref, c_scr = refs
            xr_ref, xi_ref, cr_ref, ci_ref = x_ref.at[0], x_ref.at[1], cin_ref.at[0], cin_ref.at[1]
        else:
            b_ref, ar_ref, ai_ref, o_ref, co_ref, c_scr = refs
            cor_ref, coi_ref = co_ref.at[0], co_ref.at[1]
        br_ref, bi_ref, or_ref, oi_ref = b_ref.at[0], b_ref.at[1], o_ref.at[0], o_ref.at[1]
        ar = ar_ref[...]
        ai = -ai_ref[...] if reverse else ai_ref[...]

        def tile_at(i):
            return pl.multiple_of((n - 1 - i if reverse else i) * SCAN_SEG, SCAN_SEG)

        def advance(s, row):
            sr, si = s
            return (ar * sr - ai * si + br_ref[pl.ds(row, SCAN_SEG), :], ar * si + ai * sr + bi_ref[pl.ds(row, SCAN_SEG), :])

        zero = jnp.zeros((SCAN_SEG, lanes), F32)
        fin_r, fin_i = lax.fori_loop(0, n, lambda i, s: advance(s, tile_at(i)), (zero, zero), unroll=8)
        pr, pi = ar[0:1], ai[0:1]
        for _ in range(log_n):
            pr, pi = pr * pr - pi * pi, 2.0 * pr * pi
        er = ei = jnp.zeros((1, lanes), F32)
        for step in range(SCAN_SEG):
            j = SCAN_SEG - 1 - step if reverse else step
            c_scr[0, j:j + 1, :] = er
            c_scr[1, j:j + 1, :] = ei
            er, ei = pr * er - pi * ei + fin_r[j:j + 1], pr * ei + pi * er + fin_i[j:j + 1]
        entry = (c_scr[0], c_scr[1])

        if not reverse:
            cor_ref[...], coi_ref[...] = entry

            def step2(i, s):
                row = tile_at(i)
                s = advance(s, row)
                or_ref[pl.ds(row, SCAN_SEG), :] = s[0]
                oi_ref[pl.ds(row, SCAN_SEG), :] = s[1]
                return s
            lax.fori_loop(0, n, step2, entry, unroll=8)
        else:
            def emit(s, row, xr_prev, xi_prev, acc):
                s = advance(s, row)
                or_ref[pl.ds(row, SCAN_SEG), :] = s[0]
                oi_ref[pl.ds(row, SCAN_SEG), :] = s[1]
                return s, (acc[0] + s[0] * xr_prev + s[1] * xi_prev, acc[1] + s[1] * xr_prev - s[0] * xi_prev)

            def step2(i, carry):
                s, acc = carry
                row = tile_at(i)
                prev = pl.multiple_of(row - SCAN_SEG, SCAN_SEG)
                return emit(s, row, xr_ref[pl.ds(prev, SCAN_SEG), :], xi_ref[pl.ds(prev, SCAN_SEG), :], acc)
            s, acc = lax.fori_loop(0, n - 1, step2, (entry, (zero, zero)), unroll=8)
            s, acc = emit(s, 0, cr_ref[...], ci_ref[...], acc)
            dar_ref[...] = jnp.sum(acc[0], axis=0, keepdims=True)
            dai_ref[...] = jnp.sum(acc[1], axis=0, keepdims=True)

    planes = pl.BlockSpec((2, t_len, lanes), lambda b: (0, 0, b))
    tile8 = pl.BlockSpec((SCAN_SEG, lanes), lambda b: (0, b))
    entry_spec = pl.BlockSpec((2, SCAN_SEG, lanes), lambda b: (0, 0, b))
    row1 = pl.BlockSpec((1, lanes), lambda b: (0, b))
    seq = jax.ShapeDtypeStruct((2, t_len, S5_CH), F32)
    if reverse:
        in_specs = [planes, tile8, tile8, planes, entry_spec]
        args = (bu, a_re, a_im, states, carry_in)
        out_specs = [planes, row1, row1]
        out_shape = [seq, jax.ShapeDtypeStruct((1, S5_CH), F32), jax.ShapeDtypeStruct((1, S5_CH), F32)]
    else:
        in_specs = [planes, tile8, tile8]
        args = (bu, a_re, a_im)
        out_specs = [planes, entry_spec]
        out_shape = [seq, jax.ShapeDtypeStruct((2, SCAN_SEG, S5_CH), F32)]
    return pl.pallas_call(body, name=name, grid=(nblk,), in_specs=in_specs, out_specs=out_specs, out_shape=out_shape,
                          scratch_shapes=[pltpu.VMEM((2, SCAN_SEG, lanes), F32)],
                          compiler_params=_params(("parallel",)))(*args)


def _ret_consts():
    h = np.arange(RET_HEADS, dtype=np.float32)
    log_gamma = np.log1p(-np.exp2(-5.0 - h)).astype(np.float32)
    idx = np.arange(BLOCK, dtype=np.float32)
    diff = idx[:, None] - idx[None, :]
    decay = np.where(diff >= 0, np.exp(log_gamma[:, None, None] * np.maximum(diff, 0.0)), 0.0).astype(np.float32)
    k_w = np.exp(log_gamma[:, None] * (BLOCK - 1 - idx)[None, :]).astype(np.float32)[:, :, None]
    q_w = np.exp(log_gamma[:, None] * (idx + 1.0)[None, :]).astype(np.float32)[:, :, None]
    chunk_decay = [float(v) for v in np.exp(log_gamma * BLOCK).astype(np.float32)]
    return jnp.asarray(decay), jnp.asarray(q_w), jnp.asarray(k_w), chunk_decay


def _ret_head(q, k, v, g, prev, decay, q_w, k_w, chunk_decay):
    inner_s = _dnt(q, k) * decay
    y = _dnn(inner_s, v) + _dnn(q * q_w, prev)
    mu = jnp.mean(y, axis=-1, keepdims=True)
    var = jnp.mean(jnp.square(y - mu), axis=-1, keepdims=True)
    yn = (y - mu) * lax.rsqrt(var + EPS)
    out = g * jax.nn.sigmoid(g) * yn
    return out, prev * chunk_decay + _dtn(k * k_w, v)


def _ret_prep(proj, cos, sin):
    def fn(rq, rk, c, s):
        return _rope(rq, c, s, 32), _rope(rk, c, s, 32) * (RET_DK ** -0.5)
    t_len = proj.shape[0]
    return _rowwise("ret_prep", fn, [_seg(proj, "rq"), _seg(proj, "rk"), cos, sin], [],
                    [((t_len, 256), F32), ((t_len, 256), F32)])


def _ret_unprep(dq, dk, cos, sin):
    def fn(dq_t, dk_t, c, s):
        return _rope(dq_t, c, -s, 32), _rope(dk_t * (RET_DK ** -0.5), c, -s, 32)
    t_len = dq.shape[0]
    return _rowwise("ret_unprep", fn, [dq, dk, cos, sin], [], [((t_len, 256), BF16), ((t_len, 256), BF16)])


def _ret_fwd(q, k, proj):
    t_len = q.shape[0]
    nc = t_len // BLOCK
    decay, q_w, k_w, chunk_decay = _ret_consts()

    def body(q_ref, k_ref, v_ref, g_ref, dec_ref, qw_ref, kw_ref, o_ref, prev_ref, state):
        @pl.when(pl.program_id(0) == 0)
        def _():
            state[...] = jnp.zeros_like(state)
        for h in range(RET_HEADS):
            qk, vv = slice(64 * h, 64 * h + 64), slice(128 * h, 128 * h + 128)
            prev = state[h]
            prev_ref[h] = prev
            out, nxt = _ret_head(q_ref[:, qk], k_ref[:, qk], v_ref[:, vv], g_ref[:, vv], prev,
                                 dec_ref[h], qw_ref[h], kw_ref[h], chunk_decay[h])
            o_ref[:, vv] = out.astype(o_ref.dtype)
            state[h] = nxt

    chunk = lambda w, c=0: pl.BlockSpec((BLOCK, w), lambda i, c=c: (i, c))
    return pl.pallas_call(
        body, name="ret_fwd", grid=(nc,),
        in_specs=[chunk(256), chunk(256), chunk(512, PROJ["rv"][0] // 512), chunk(512, PROJ["rg"][0] // 512),
                  _const_spec(decay.shape), _const_spec(q_w.shape), _const_spec(k_w.shape)],
        out_specs=[chunk(512), pl.BlockSpec((None, RET_HEADS, RET_DK, RET_DV), lambda i: (i, 0, 0, 0))],
        out_shape=[jax.ShapeDtypeStruct((t_len, 512), BF16), jax.ShapeDtypeStruct((nc, RET_HEADS, RET_DK, RET_DV), F32)],
        scratch_shapes=[pltpu.VMEM((RET_HEADS, RET_DK, RET_DV), F32)],
        compiler_params=_params(("arbitrary",)))(q, k, proj, proj, decay, q_w, k_w)


def _ret_bwd(q, k, proj, prevs, dy):
    t_len = q.shape[0]
    nc = t_len // BLOCK
    decay, q_w, k_w, chunk_decay = _ret_consts()

    def body(q_ref, k_ref, v_ref, g_ref, prev_ref, dy_ref, dec_ref, qw_ref, kw_ref, dq_ref, dk_ref, dv_ref, dg_ref, dstate):
        @pl.when(pl.program_id(0) == 0)
        def _():
            dstate[...] = jnp.zeros_like(dstate)
        for h in range(RET_HEADS):
            qk, vv = slice(64 * h, 64 * h + 64), slice(128 * h, 128 * h + 128)
            head = functools.partial(_ret_head, decay=dec_ref[h], q_w=qw_ref[h], k_w=kw_ref[h], chunk_decay=chunk_decay[h])
            _, vjp = jax.vjp(head, q_ref[:, qk], k_ref[:, qk], v_ref[:, vv], g_ref[:, vv], prev_ref[h])
            dq, dk, dv, dg, dprev = vjp((dy_ref[:, vv].astype(F32), dstate[h]))
            dq_ref[:, qk], dk_ref[:, qk] = dq, dk
            dv_ref[:, vv], dg_ref[:, vv] = dv.astype(dv_ref.dtype), dg.astype(dg_ref.dtype)
            dstate[h] = dprev

    chunk = lambda w, c=0: pl.BlockSpec((BLOCK, w), lambda i, c=c: (nc - 1 - i, c))
    return pl.pallas_call(
        body, name="ret_bwd", grid=(nc,),
        in_specs=[chunk(256), chunk(256), chunk(512, PROJ["rv"][0] // 512), chunk(512, PROJ["rg"][0] // 512),
                  pl.BlockSpec((None, RET_HEADS, RET_DK, RET_DV), lambda i: (nc - 1 - i, 0, 0, 0)), chunk(512),
                  _const_spec(decay.shape), _const_spec(q_w.shape), _const_spec(k_w.shape)],
        out_specs=[chunk(256), chunk(256), chunk(512), chunk(512)],
        out_shape=[jax.ShapeDtypeStruct((t_len, 256), F32), jax.ShapeDtypeStruct((t_len, 256), F32),
                   jax.ShapeDtypeStruct((t_len, 512), BF16), jax.ShapeDtypeStruct((t_len, 512), BF16)],
        scratch_shapes=[pltpu.VMEM((RET_HEADS, RET_DK, RET_DV), F32)],
        compiler_params=_params(("arbitrary",)))(q, k, proj, proj, prevs, dy, decay, q_w, k_w)


MLA_SCALE = 96 ** -0.5


def _mla_prep(qf, kvf, proj, cos, sin):
    def fn(q, kv, kr, c, s):
        q_rope = _rope(q[:, 512:768], c, s, 16)
        k_rope = _rope(kr, c, s, 16)[:, :32]
        zero = jnp.zeros_like(k_rope)
        qs = [jnp.concatenate([q[:, 64 * h:64 * h + 64], q_rope[:, 32 * h:32 * h + 32], zero], axis=1) for h in range(MLA_HEADS)]
        ks = [jnp.concatenate([kv[:, 64 * h:64 * h + 64], k_rope, zero], axis=1) for h in range(MLA_HEADS)]
        vs = [kv[:, 512 + 64 * h:576 + 64 * h] for h in range(MLA_HEADS)]
        return qs, ks, vs
    t_len = qf.shape[0]
    return _rowwise("mla_prep", fn, [qf, kvf, _seg(proj, "kr"), cos, sin], [],
                    [((8, t_len, 128), BF16), ((8, t_len, 128), BF16), ((8, t_len, 64), BF16)])


def _mla_unprep(dq, dk, dv, cos, sin):
    def fn(dq_t, dk_t, dv_t, c, s):
        q_rope = _rope(jnp.concatenate([dq_t[h][:, 64:96] for h in range(MLA_HEADS)], axis=1), c, -s, 16)
        d_qf = jnp.concatenate([dq_t[h][:, :64] for h in range(MLA_HEADS)] + [q_rope], axis=1)
        d_kvf = jnp.concatenate([dk_t[h][:, :64] for h in range(MLA_HEADS)] + [dv_t[h] for h in range(MLA_HEADS)], axis=1)
        k_rope = dk_t[0][:, 64:96]
        for h in range(1, MLA_HEADS):
            k_rope = k_rope + dk_t[h][:, 64:96]
        d_kr = _rope(jnp.concatenate([k_rope, jnp.zeros((k_rope.shape[0], 96), F32)], axis=1), c, -s, 16)
        return d_qf, d_kvf, d_kr
    t_len = dq.shape[1]
    return _rowwise("mla_unprep", fn, [dq, dk, dv, cos, sin], [],
                    [((t_len, 768), BF16), ((t_len, 1024), BF16), ((t_len, 128), BF16)])


MLA_EXP2 = MLA_SCALE * math.log2(math.e)


def _diag_mask(tq):
    return lax.broadcasted_iota(jnp.int32, (tq, tq), 1) <= lax.broadcasted_iota(jnp.int32, (tq, tq), 0)


def _tri_steps(nq, key_major):
    if key_major:
        pairs = [(i, j) for j in range(nq) for i in range(j, nq)]
    else:
        pairs = [(i, j) for i in range(nq) for j in range(i + 1)]
    return jnp.asarray([p[0] for p in pairs], jnp.int32), jnp.asarray([p[1] for p in pairs], jnp.int32)


def _mla_fwd(q, k, v):
    t_len = q.shape[1]
    tq = min(512, t_len)
    nq = t_len // tq
    qi, kj = _tri_steps(nq, False)

    def body(i_ref, j_ref, q_ref, k_ref, v_ref, o_ref, lse_ref, m_s, l_s, acc_s):
        step = pl.program_id(1)
        i, j = i_ref[step], j_ref[step]

        @pl.when(j == 0)
        def _():
            m_s[...] = jnp.full_like(m_s, NEG_INF)
            l_s[...] = jnp.zeros_like(l_s)
            acc_s[...] = jnp.zeros_like(acc_s)

        def update(mask):
            for e in range(2):
                s = _raw_dot(q_ref[e], k_ref[e], NT)
                if mask is not None:
                    s = jnp.where(mask, s, NEG_INF)
                m_old = m_s[e]
                m_new = jnp.maximum(m_old, jnp.max(s, axis=-1, keepdims=True))
                alpha = jnp.exp2((m_old - m_new) * MLA_EXP2)
                p = jnp.exp2((s - m_new) * MLA_EXP2)
                l_s[e] = alpha * l_s[e] + jnp.sum(p, axis=-1, keepdims=True)
                acc_s[e] = alpha * acc_s[e] + _raw_dot(p, v_ref[e], NN)
                m_s[e] = m_new

        @pl.when(j < i)
        def _():
            update(None)

        @pl.when(j == i)
        def _():
            update(_diag_mask(tq))
            o_ref[...] = jnp.concatenate([acc_s[e] / l_s[e] for e in range(2)], axis=1)
            for e in range(2):
                lse_ref[e] = m_s[e] * MLA_EXP2 + jnp.log2(l_s[e])

    grid_spec = pltpu.PrefetchScalarGridSpec(
        num_scalar_prefetch=2, grid=(MLA_HEADS // 2, int(qi.shape[0])),
        in_specs=[pl.BlockSpec((2, tq, 128), lambda hp, s, qi, kj: (hp, qi[s], 0)),
                  pl.BlockSpec((2, tq, 128), lambda hp, s, qi, kj: (hp, kj[s], 0)),
                  pl.BlockSpec((2, tq, 64), lambda hp, s, qi, kj: (hp, kj[s], 0))],
        out_specs=[pl.BlockSpec((tq, 128), lambda hp, s, qi, kj: (qi[s], hp)),
                   pl.BlockSpec((2, tq, 1), lambda hp, s, qi, kj: (hp, qi[s], 0))],
        scratch_shapes=[pltpu.VMEM((2, tq, 1), F32), pltpu.VMEM((2, tq, 1), F32), pltpu.VMEM((2, tq, 64), F32)])
    return pl.pallas_call(
        body, name="mla_fwd", grid_spec=grid_spec,
        out_shape=[jax.ShapeDtypeStruct((t_len, 512), F32), jax.ShapeDtypeStruct((MLA_HEADS, t_len, 1), F32)],
        compiler_params=_params(("parallel", "arbitrary")))(qi, kj, q, k, v)


def _mla_bwd_prep(o, dy):
    def fn(o_t, dy_t):
        dos, deltas = [], []
        for h in range(MLA_HEADS):
            d_h = dy_t[:, 64 * h:64 * h + 64].astype(F32)
            dos.append(d_h)
            deltas.append(jnp.sum(d_h * o_t[:, 64 * h:64 * h + 64], axis=-1, keepdims=True))
        return dos, deltas
    t_len = o.shape[0]
    return _rowwise("mla_bwd_prep", fn, [o, dy], [], [((8, t_len, 64), BF16), ((8, t_len, 1), F32)])


def _mla_bwd(q, k, v, do, lse2, delta):
    t_len = q.shape[1]
    tq = min(512, t_len)
    nq = t_len // tq
    qi, kj = _tri_steps(nq, True)

    def body(i_ref, j_ref, q_ref, k_ref, v_ref, do_ref, lse_ref, dl_ref, dq_ref, dk_ref, dv_ref, k_acc, v_acc):
        step = pl.program_id(1)
        i, j = i_ref[step], j_ref[step]

        @pl.when(step == 0)
        def _():
            dq_ref[...] = jnp.zeros_like(dq_ref)

        @pl.when(i == j)
        def _():
            k_acc[...] = jnp.zeros_like(k_acc)
            v_acc[...] = jnp.zeros_like(v_acc)

        def tile(mask):
            p = jnp.exp2(_raw_dot(q_ref[...], k_ref[...], NT) * MLA_EXP2 - lse_ref[...])
            if mask is not None:
                p = jnp.where(mask, p, 0.0)
            ds = p * (_raw_dot(do_ref[...], v_ref[...], NT) - dl_ref[...]) * MLA_SCALE
            v_acc[...] += _raw_dot(p, do_ref[...], TN)
            k_acc[...] += _raw_dot(ds, q_ref[...], TN)
            rows = pl.ds(pl.multiple_of(i * tq, tq), tq)
            dq_ref[rows, :] += _raw_dot(ds, k_ref[...], NN)

        @pl.when(i == j)
        def _():
            tile(_diag_mask(tq))

        @pl.when(i > j)
        def _():
            tile(None)

        @pl.when(i == nq - 1)
        def _():
            dk_ref[...] = k_acc[...]
            dv_ref[...] = v_acc[...]

    q_blk = lambda w: pl.BlockSpec((None, tq, w), lambda h, s, qi, kj: (h, qi[s], 0))
    k_blk = lambda w: pl.BlockSpec((None, tq, w), lambda h, s, qi, kj: (h, kj[s], 0))
    grid_spec = pltpu.PrefetchScalarGridSpec(
        num_scalar_prefetch=2, grid=(MLA_HEADS, int(qi.shape[0])),
        in_specs=[q_blk(128), k_blk(128), k_blk(64), q_blk(64), q_blk(1), q_blk(1)],
        out_specs=[pl.BlockSpec((None, t_len, 128), lambda h, s, qi, kj: (h, 0, 0)), k_blk(128), k_blk(64)],
        scratch_shapes=[pltpu.VMEM((tq, 128), F32), pltpu.VMEM((tq, 64), F32)])
    return pl.pallas_call(
        body, name="mla_bwd", grid_spec=grid_spec,
        out_shape=[jax.ShapeDtypeStruct((MLA_HEADS, t_len, 128), F32), jax.ShapeDtypeStruct((MLA_HEADS, t_len, 128), F32),
                   jax.ShapeDtypeStruct((MLA_HEADS, t_len, 64), F32)],
        compiler_params=_params(("parallel", "arbitrary")))(qi, kj, q, k, v, do, lse2, delta)


MERGE_TN = 256
GATE_BLOCK0 = PROJ["gates"][0] // MERGE_TN


def _merge_specs(tm):
    y_spec = pl.BlockSpec((tm, 512), lambda i, j: (i, 0))
    w_spec = pl.BlockSpec((512, MERGE_TN), lambda i, j: (0, j))
    gate = lambda b: pl.BlockSpec((tm, MERGE_TN), lambda i, j, b=b: (i, GATE_BLOCK0 + 4 * b + j))
    return [y_spec] * 4 + [w_spec] * 4 + [gate(b) for b in range(4)]


def _merge_fwd(ys, wbs, proj):
    t_len = proj.shape[0]
    tm = min(512, t_len)

    def body(*refs):
        y, w, gl, o_ref = refs[0:4], refs[4:8], refs[8:12], refs[12]
        acc = jnp.zeros((tm, MERGE_TN), F32)
        for b in range(4):
            acc += jax.nn.sigmoid(gl[b][...]) * _raw_dot(y[b][...], w[b][...], NN)
        o_ref[...] = acc.astype(o_ref.dtype)

    return pl.pallas_call(body, name="merge_fwd", grid=(t_len // tm, D_MODEL // MERGE_TN), in_specs=_merge_specs(tm),
                          out_specs=pl.BlockSpec((tm, MERGE_TN), lambda i, j: (i, j)),
                          out_shape=jax.ShapeDtypeStruct((t_len, D_MODEL), BF16),
                          compiler_params=_params(("parallel", "parallel")))(*ys, *wbs, *([proj] * 4))


def _merge_bwd(ys, wbs, proj, dmerged):
    t_len = proj.shape[0]
    tm = min(512, t_len)

    def body(*refs):
        y, w, gl, dm_ref, dgl, dt = refs[0:4], refs[4:8], refs[8:12], refs[12], refs[13:17], refs[17:21]
        dm = dm_ref[...]
        for b in range(4):
            gate = jax.nn.sigmoid(gl[b][...])
            t_b = _raw_dot(y[b][...], w[b][...], NN)
            dgl[b][...] = (dm * t_b * gate * (1.0 - gate)).astype(BF16)
            dt[b][...] = (dm * gate).astype(BF16)

    o_spec = pl.BlockSpec((tm, MERGE_TN), lambda i, j: (i, j))
    return pl.pallas_call(body, name="merge_bwd", grid=(t_len // tm, D_MODEL // MERGE_TN), in_specs=_merge_specs(tm) + [o_spec],
                          out_specs=[o_spec] * 8, out_shape=[jax.ShapeDtypeStruct((t_len, D_MODEL), BF16)] * 8,
                          compiler_params=_params(("parallel", "parallel")))(*ys, *wbs, *([proj] * 4), dmerged)


def _adamw(name, w, g, m, v, slabs):
    shape = w.shape
    size = int(np.prod(shape))
    if shape[-1] >= 96:
        view = (size // shape[-1], shape[-1])
    elif size % 128 == 0:
        view = (size // 128, 128)
    else:
        view = (1, size)
    rows = view[0]
    tile = rows
    for cand in (512, 256, 128, 64):
        if rows > cand and rows % cand == 0 and cand * view[1] * 4 <= 2 ** 20:
            tile = cand
            break
    if rows * view[1] * 4 <= 2 ** 20:
        tile = rows
    c1, c2 = 1.0 - ADAM_B1 ** ADAM_STEP, 1.0 - ADAM_B2 ** ADAM_STEP

    def body(w_ref, g_ref, m_ref, v_ref, og_ref, od_ref, om_ref, ov_ref):
        if slabs:
            grad = g_ref[0].astype(F32)
            for d in range(1, slabs):
                grad = grad + g_ref[d].astype(F32)
        else:
            grad = g_ref[...]
        m_new = ADAM_B1 * m_ref[...] + (1.0 - ADAM_B1) * grad
        v_new = ADAM_B2 * v_ref[...] + (1.0 - ADAM_B2) * jnp.square(grad)
        og_ref[...] = grad
        od_ref[...] = -ADAM_LR * ((m_new / c1) / (jnp.sqrt(v_new / c2) + ADAM_EPS) + ADAM_WD * w_ref[...])
        om_ref[...] = m_new
        ov_ref[...] = v_new

    spec = pl.BlockSpec((tile, view[1]), lambda i: (i, 0))
    g_spec = pl.BlockSpec((slabs, tile, view[1]), lambda i: (0, i, 0)) if slabs else spec
    g_view = g.reshape((slabs,) + view) if slabs else g.reshape(view)
    outs = pl.pallas_call(body, name=name, grid=(rows // tile,), in_specs=[spec, g_spec, spec, spec], out_specs=[spec] * 4,
                          out_shape=[jax.ShapeDtypeStruct(view, F32)] * 4,
                          compiler_params=_params(("parallel",)))(w.reshape(view), g_view, m.reshape(view), v.reshape(view))
    return [o.reshape(shape) for o in outs]


def _sum_slabs(name, x):
    def body(x_ref, o_ref):
        acc = x_ref[0]
        for d in range(1, N_DEV):
            acc = acc + x_ref[d]
        o_ref[...] = acc
    return pl.pallas_call(body, name=name, out_shape=jax.ShapeDtypeStruct(x.shape[1:], x.dtype))(x)


MESH_ID = pl.DeviceIdType.MESH
ANY_SPEC = pl.BlockSpec(memory_space=pl.ANY)


def _remote(src, dst, send_sems, recv_sems, k, to):
    return pltpu.make_async_remote_copy(src_ref=src, dst_ref=dst, send_sem=send_sems.at[k], recv_sem=recv_sems.at[k],
                                        device_id=to, device_id_type=MESH_ID)


def _gather_two_level(name, blocks):
    n = len(blocks)

    def body(*refs):
        srcs, dsts = refs[:n], refs[n:2 * n]
        send_sems, recv_sems, local_sems = refs[2 * n:]
        x, y, c = lax.axis_index("x"), lax.axis_index("y"), lax.axis_index("c")
        sibling = (x, y, 1 - c)
        chips = [(1 - x, y), (x, 1 - y), (1 - x, 1 - y)]
        index = lambda px, py, pc: 4 * px + 2 * py + pc
        me = index(x, y, c)

        def copy(t, k, block, to, src=None):
            slot = dsts[t].at[block]
            return _remote(slot if src is None else src, slot, send_sems, recv_sems, 7 * t + k, to)

        own = [pltpu.make_async_copy(srcs[t], dsts[t].at[me], local_sems.at[t]) for t in range(n)]
        first = []
        for t in range(n):
            own[t].start()
            first.append(copy(t, 0, me, sibling, src=srcs[t]))
            first += [copy(t, 1 + j, me, (*chip, c), src=srcs[t]) for j, chip in enumerate(chips)]
        for cp in first:
            cp.start()
        passed = []
        for j, chip in enumerate(chips):
            for t in range(n):
                block = index(*chip, c)
                copy(t, 1 + j, block, sibling).wait_recv()
                passed.append(copy(t, 4 + j, block, sibling))
                passed[-1].start()
        for t in range(n):
            copy(t, 0, index(x, y, 1 - c), sibling).wait_recv()
            for j, chip in enumerate(chips):
                copy(t, 4 + j, index(*chip, 1 - c), sibling).wait_recv()
        for cp in first + passed:
            cp.wait_send()
        for cp in own:
            cp.wait()

    return pl.pallas_call(
        body, name=name, in_specs=[ANY_SPEC] * n, out_specs=[ANY_SPEC] * n,
        out_shape=[jax.ShapeDtypeStruct((N_DEV,) + b.shape, b.dtype) for b in blocks],
        scratch_shapes=[pltpu.SemaphoreType.DMA((7 * n,)), pltpu.SemaphoreType.DMA((7 * n,)), pltpu.SemaphoreType.DMA((n,))],
        compiler_params=pltpu.CompilerParams(has_side_effects=True))(*blocks)


def _pair_exchange(name, blocks):
    n = len(blocks)

    def body(*refs):
        srcs, dsts = refs[:n], refs[n:2 * n]
        send_sems, recv_sems = refs[2 * n:]
        sibling = (lax.axis_index("x"), lax.axis_index("y"), 1 - lax.axis_index("c"))
        copies = [_remote(srcs[t], dsts[t], send_sems, recv_sems, t, sibling) for t in range(n)]
        for cp in copies:
            cp.start()
        for cp in copies:
            cp.wait()

    return pl.pallas_call(
        body, name=name, in_specs=[ANY_SPEC] * n, out_specs=[ANY_SPEC] * n,
        out_shape=[jax.ShapeDtypeStruct(b.shape, b.dtype) for b in blocks],
        scratch_shapes=[pltpu.SemaphoreType.DMA((n,)), pltpu.SemaphoreType.DMA((n,))],
        compiler_params=pltpu.CompilerParams(has_side_effects=True))(*blocks)


def _chip_exchange(name, items):
    n = len(items)
    out_shape = [jax.ShapeDtypeStruct((4,) + tuple(a.shape[1:] if per_chip else a.shape), a.dtype) for a, per_chip in items]

    def body(*refs):
        srcs, dsts = refs[:n], refs[n:2 * n]
        send_sems, recv_sems, local_sems = refs[2 * n:]
        x, y, c = lax.axis_index("x"), lax.axis_index("y"), lax.axis_index("c")
        mine = 2 * x + y
        own, remote = [], []
        for t, (_, per_chip) in enumerate(items):
            own.append(pltpu.make_async_copy(srcs[t].at[mine] if per_chip else srcs[t], dsts[t].at[mine], local_sems.at[t]))
            own[-1].start()
        for k in range(1, 4):
            px, py = x ^ (k >> 1), y ^ (k & 1)
            peer = 2 * px + py
            for t, (_, per_chip) in enumerate(items):
                src = srcs[t].at[peer] if per_chip else srcs[t]
                out = _remote(src, dsts[t].at[mine], send_sems, recv_sems, 3 * t + k - 1, (px, py, c))
                out.start()
                remote.append((out, _remote(src, dsts[t].at[peer], send_sems, recv_sems, 3 * t + k - 1, (px, py, c))))
        for out, arrival in remote:
            out.wait_send()
            arrival.wait_recv()
        for cp in own:
            cp.wait()

    return pl.pallas_call(
        body, name=name, in_specs=[ANY_SPEC] * n, out_specs=[ANY_SPEC] * n, out_shape=out_shape,
        scratch_shapes=[pltpu.SemaphoreType.DMA((3 * n,)), pltpu.SemaphoreType.DMA((3 * n,)), pltpu.SemaphoreType.DMA((n,))],
        compiler_params=pltpu.CompilerParams(has_side_effects=True))(*[a for a, _ in items])


def _perm(a):
    t_len, w = a.shape
    return a.reshape(SCAN_SEG, t_len // SCAN_SEG, w).transpose(1, 0, 2).reshape(t_len, w)


def _unperm(a):
    t_len, w = a.shape
    return a.reshape(t_len // SCAN_SEG, SCAN_SEG, w).transpose(1, 0, 2).reshape(t_len, w)


def _band_diag(p):
    a, b = p.shape[1:]
    eye = jnp.eye(8, dtype=p.dtype)
    return (p.reshape(4, 8, a, 1, b) * eye[None, :, None, :, None]).reshape(4, 8 * a, 8 * b)


def _band_diag_extract(m, a, b):
    eye = jnp.eye(8, dtype=m.dtype)
    return (m.reshape(4, 8, a, 8, b) * eye[None, :, None, :, None]).sum(axis=3).reshape(32, a, b)


def _proj_cols(w_ref_layout):
    parts = []
    for name in PROJ_ORDER:
        off, width = REF_SPLIT[name]
        part = w_ref_layout[:, off:off + width]
        if width < PROJ[name][1]:
            part = jnp.pad(part, ((0, 0), (0, PROJ[name][1] - width)))
        parts.append(part)
    return jnp.concatenate(parts, axis=1)


def _unproj_cols(w_proj_layout):
    parts = []
    for name in REF_ORDER:
        parts.append(w_proj_layout[:, PROJ[name][0]:PROJ[name][0] + REF_SPLIT[name][1]])
    return jnp.concatenate(parts, axis=1)


def _uq_cols(w):
    w3 = w.reshape(w.shape[0], MLA_HEADS, 96)
    return jnp.concatenate([w3[:, :, :64].reshape(-1, 512), w3[:, :, 64:].reshape(-1, 256)], axis=1)


def _un_uq_cols(w):
    return jnp.concatenate([w[:, :512].reshape(-1, MLA_HEADS, 64), w[:, 512:].reshape(-1, MLA_HEADS, 32)], axis=2).reshape(-1, 768)


def _ukv_cols(w):
    w3 = w.reshape(w.shape[0], MLA_HEADS, 128)
    return jnp.concatenate([w3[:, :, :64].reshape(-1, 512), w3[:, :, 64:].reshape(-1, 512)], axis=1)


def _un_ukv_cols(w):
    return jnp.concatenate([w[:, :512].reshape(-1, MLA_HEADS, 64), w[:, 512:].reshape(-1, MLA_HEADS, 64)], axis=2).reshape(-1, 1024)


def _cols_gathered(g):
    return g.transpose(1, 0, 2).reshape(g.shape[1], -1)


def _cols_slabs(w):
    r, c = w.shape
    return w.reshape(r, N_DEV, c // N_DEV).transpose(1, 0, 2)


def _row(v):
    return v.reshape(1, -1)


def _s5_weights(p):
    flat = dict(lam_re=p["s5_lam_re"].reshape(S5_CH, 1), lam_im=p["s5_lam_im"].reshape(S5_CH, 1),
                log_dt=jnp.repeat(p["s5_log_dt"], S5_STATE).reshape(S5_CH, 1),
                b_re=p["s5_b_re"].reshape(S5_CH, S5_GROUP), b_im=p["s5_b_im"].reshape(S5_CH, S5_GROUP))
    ab_re, ab_im, bb_re, bb_im = _s5_param_fwd(flat["lam_re"], flat["lam_im"], flat["log_dt"], flat["b_re"], flat["b_im"])
    w_bu = jnp.concatenate([_band_diag(bb.reshape(32, 64, 16).transpose(0, 2, 1)) for bb in (bb_re, bb_im)], axis=0)
    w_y = jnp.concatenate([_band_diag(p["s5_c_re"].transpose(0, 2, 1)), -_band_diag(p["s5_c_im"].transpose(0, 2, 1))], axis=0)
    return dict(flat=flat,
                a_re=jnp.broadcast_to(ab_re.reshape(1, S5_CH), (SCAN_SEG, S5_CH)),
                a_im=jnp.broadcast_to(ab_im.reshape(1, S5_CH), (SCAN_SEG, S5_CH)),
                w_bu=w_bu.astype(BF16), w_bu_t=w_bu.swapaxes(1, 2).astype(BF16),
                w_y=w_y.astype(BF16), w_y_t=w_y.swapaxes(1, 2).astype(BF16))


def _layer_fwd(x, w, p, tabs):
    t_len = x.shape[0]
    cos64, sin64, cos32, sin32 = tabs
    s = {}
    (s["h1"],) = _rowwise("pre_mix", lambda xt, g: (_rms(xt, g),), [x], [_row(p["g_pre_mix"])], [((t_len, D_MODEL), BF16)])
    proj = s["proj"] = _mm("proj", s["h1"], w["w_in"], "nn", tn=256)
    s["qa"], s["ka"], s["va"] = _swa_prep(proj, cos64, sin64)
    s["sinks"] = jnp.broadcast_to(p["swa_sinks"][:, None], (8, 128))
    y_a = _swa_fwd(s["qa"], s["ka"], s["va"], s["sinks"])
    s5 = s["s5"] = _s5_weights(p)
    su = proj[:, PROJ["su"][0]:PROJ["su"][0] + 512]
    s["u_p"] = _perm(su).astype(BF16)
    bu = _s5_expand("s5_bu", s["u_p"], s5["w_bu"])
    s["xs"], s["cin"] = _s5_scan("s5_scan_fwd", bu, s5["a_re"], s5["a_im"])
    s["y_s"] = _unperm(_s5_contract("s5_y", s["xs"], s5["w_y"]))
    (s["z"],) = _rowwise("s5_gelu", lambda y, u, d: (jax.nn.gelu(y + d * u),), [s["y_s"], _seg(proj, "su")],
                         [_row(p["s5_d"])], [((t_len, 512), BF16)])
    s["zz"] = _mm("s5_glu_mm", s["z"], w["w_glu"], "nn")
    (y_b,) = _rowwise("s5_glu", lambda t: (t[:, :512] * jax.nn.sigmoid(t[:, 512:]),), [s["zz"]], [], [((t_len, 512), BF16)])
    s["qc"], s["kc"] = _ret_prep(proj, cos64, sin64)
    y_c, s["prevs"] = _ret_fwd(s["qc"], s["kc"], proj)
    s["cqn"], s["ckvn"] = _rowwise("mla_norm", lambda cq, ckv, gq, gkv: (_rms(cq, gq), _rms(ckv, gkv)),
                                   [_seg(proj, "cq"), _seg(proj, "ckv")], [_row(p["mla_g_q"]), _row(p["mla_g_kv"])],
                                   [((t_len, 256), BF16), ((t_len, 128), BF16)])
    qf = _mm("mla_uq", s["cqn"], w["w_uq"], "nn", tn=256)
    kvf = _mm("mla_ukv", s["ckvn"], w["w_ukv"], "nn")
    s["Q"], s["K"], s["V"] = _mla_prep(qf, kvf, proj, cos32, sin32)
    y_d, s["lse"] = _mla_fwd(s["Q"], s["K"], s["V"])
    s["ys"] = [y_a, y_b, y_c, y_d]
    s["merged"] = _merge_fwd(s["ys"], w["wb"], proj)
    s["o"] = _mm("out_mm", s["merged"], w["w_out"], "nn")

    def post_mix(xt, ot, g1, g2):
        x1 = xt + _rms(ot, g1)
        return x1, _rms(x1, g2)
    s["x1"], s["h2"] = _rowwise("post_mix", post_mix, [x, s["o"]], [_row(p["g_post_mix"]), _row(p["g_pre_mlp"])],
                                [((t_len, D_MODEL), F32), ((t_len, D_MODEL), BF16)])
    s["a"], s["r"] = _mm("ff1", s["h2"], w["w_ff1"], "nn", out_dtypes=(F32, BF16),
                         epi=lambda acc: (acc, jnp.square(jnp.maximum(acc, 0.0))))
    s["f"] = _mm("ff2", s["r"], w["w_ff2"], "nn")
    (x2,) = _rowwise("post_mlp", lambda x1, f, g: (x1 + _rms(f, g),), [s["x1"], s["f"]], [_row(p["g_post_mlp"])],
                     [((t_len, D_MODEL), F32)])
    return x2, s


def _layer_bwd(x, dx2, w, p, s, tabs):
    t_len = x.shape[0]
    cos64, sin64, cos32, sin32 = tabs
    proj = s["proj"]
    big, small = {}, {}

    def post_mlp_bwd(f, d, g):
        df, dg = _rms_bwd(f, g, d)
        return (df,), (dg,)
    df, small["g_post_mlp"] = _rowwise("post_mlp_bwd", post_mlp_bwd, [s["f"], dx2], [_row(p["g_post_mlp"])],
                                       [((t_len, D_MODEL), BF16)], [((1, D_MODEL), F32)])
    da = _mm("ff2_dx", df, w["w_ff2"], "nt", out_dtypes=(BF16,), epi=lambda acc, a: (acc * 2.0 * jnp.maximum(a, 0.0),),
             epi_in=(s["a"],))
    big["w_ff2"] = _mm("ff2_dw", s["r"], df, "tn")
    dh2 = _mm("ff1_dx", da, w["w_ff1"], "nt")
    big["w_ff1"] = _mm("ff1_dw", s["h2"], da, "tn")

    def post_mix_bwd(x1, o, dh, d2, g1, g2):
        dx1_n, dg2 = _rms_bwd(x1, g2, dh)
        dx1 = d2 + dx1_n
        do, dg1 = _rms_bwd(o, g1, dx1)
        return (dx1, do), (dg1, dg2)
    dx1, do, small["g_post_mix"], small["g_pre_mlp"] = _rowwise(
        "post_mix_bwd", post_mix_bwd, [s["x1"], s["o"], dh2, dx2], [_row(p["g_post_mix"]), _row(p["g_pre_mlp"])],
        [((t_len, D_MODEL), F32), ((t_len, D_MODEL), BF16)], [((1, D_MODEL), F32), ((1, D_MODEL), F32)])
    dmerged = _mm("out_dx", do, w["w_out"], "nt")
    big["w_out"] = _mm("out_dw", s["merged"], do, "tn")

    res = _merge_bwd(s["ys"], w["wb"], proj, dmerged)
    dgl, dts = res[:4], res[4:]
    dys = [_mm("branch_dx", dts[b], w["wb"][b], "nt") for b in range(4)]
    big["wb"] = [_mm("branch_dw", s["ys"][b], dts[b], "tn") for b in range(4)]
    seg = {}

    dqa, dka, dva, dsink = _swa_bwd(s["qa"], s["ka"], s["va"], s["sinks"], dys[0])
    seg["sq"], seg["sk"], seg["sv"] = _swa_unprep(dqa, dka, dva, cos64, sin64)
    small["swa_sinks"] = dsink[:, 0]

    def glu_bwd(t, d):
        za, sg = t[:, :512], jax.nn.sigmoid(t[:, 512:])
        return (jnp.concatenate([d * sg, d * za * sg * (1.0 - sg)], axis=1),)
    (dzz,) = _rowwise("s5_glu_bwd", glu_bwd, [s["zz"], dys[1]], [], [((t_len, 1024), BF16)])
    dz = _mm("s5_glu_dx", dzz, w["w_glu"], "nt")
    big["w_glu"] = _mm("s5_glu_dw", s["z"], dzz, "tn")

    def gelu_bwd(y, u, d, skip):
        _, vjp = jax.vjp(jax.nn.gelu, y + skip * u)
        (dy2,) = vjp(d)
        return (dy2, dy2 * skip), (jnp.sum(dy2 * u, axis=0, keepdims=True),)
    dy_s, du_skip, d_skip = _rowwise("s5_gelu_bwd", gelu_bwd, [s["y_s"], _seg(proj, "su"), dz], [_row(p["s5_d"])],
                                     [((t_len, 512), F32), ((t_len, 512), F32)], [((1, 512), F32)])
    small["s5_d"] = d_skip[0]
    s5 = s["s5"]
    dy_p = _perm(dy_s).astype(BF16)
    dlam_in = _s5_expand("s5_dx_in", dy_p, s5["w_y_t"])
    lam, da_re, da_im = _s5_scan("s5_scan_bwd", dlam_in, s5["a_re"], s5["a_im"], states=s["xs"], carry_in=s["cin"])
    du_s5 = _unperm(_s5_contract("s5_du", lam, s5["w_bu_t"]))
    d_w_bu = _s5_band_grad("s5_dbu", s["u_p"], lam, True)
    d_w_y = _s5_band_grad("s5_dc", dy_p, s["xs"], False)
    (seg["su"],) = _rowwise("s5_du_sum", lambda a, b: (a + b,), [du_s5, du_skip], [], [((t_len, 512), BF16)])
    d_bb_re = _band_diag_extract(d_w_bu[:4], 16, 64).transpose(0, 2, 1).reshape(S5_CH, S5_GROUP)
    d_bb_im = _band_diag_extract(d_w_bu[4:], 16, 64).transpose(0, 2, 1).reshape(S5_CH, S5_GROUP)
    small["s5_c_re"] = _band_diag_extract(d_w_y[:4], 64, 16).transpose(0, 2, 1)
    small["s5_c_im"] = -_band_diag_extract(d_w_y[4:], 64, 16).transpose(0, 2, 1)
    fl = s5["flat"]
    group_sum = jnp.repeat(jnp.eye(S5_GROUPS, dtype=F32), S5_STATE, axis=1)
    d_lr, d_li, d_ldt, d_br, d_bi = _s5_param_bwd(fl["lam_re"], fl["lam_im"], fl["log_dt"], fl["b_re"], fl["b_im"], group_sum,
                                                  da_re.reshape(S5_CH, 1), da_im.reshape(S5_CH, 1), d_bb_re, d_bb_im)
    small["s5_lam_re"], small["s5_lam_im"] = d_lr.reshape(32, 64), d_li.reshape(32, 64)
    small["s5_log_dt"] = d_ldt[:, 0]
    small["s5_b_re"], small["s5_b_im"] = d_br.reshape(32, 64, 16), d_bi.reshape(32, 64, 16)

    dqc, dkc, seg["rv"], seg["rg"] = _ret_bwd(s["qc"], s["kc"], proj, s["prevs"], dys[2])
    seg["rq"], seg["rk"] = _ret_unprep(dqc, dkc, cos64, sin64)

    do_h, delta = _mla_bwd_prep(s["ys"][3], dys[3])
    d_q, d_k, d_v = _mla_bwd(s["Q"], s["K"], s["V"], do_h, s["lse"], delta)
    dqf, dkvf, seg["kr"] = _mla_unprep(d_q, d_k, d_v, cos32, sin32)
    dcqn = _mm("mla_uq_dx", dqf, w["w_uq"], "nt")
    big["w_uq"] = _mm("mla_uq_dw", s["cqn"], dqf, "tn", tn=256)
    dckvn = _mm("mla_ukv_dx", dkvf, w["w_ukv"], "nt")
    big["w_ukv"] = _mm("mla_ukv_dw", s["ckvn"], dkvf, "tn")

    def mla_norm_bwd(cq, ckv, d1, d2, gq, gkv):
        dcq, dgq = _rms_bwd(cq, gq, d1)
        dckv, dgkv = _rms_bwd(ckv, gkv, d2)
        return (dcq, dckv), (dgq, dgkv)
    seg["cq"], seg["ckv"], dgq, dgkv = _rowwise(
        "mla_norm_bwd", mla_norm_bwd, [_seg(proj, "cq"), _seg(proj, "ckv"), dcqn, dckvn],
        [_row(p["mla_g_q"]), _row(p["mla_g_kv"])], [((t_len, 256), BF16), ((t_len, 128), BF16)],
        [((1, 256), F32), ((1, 128), F32)])
    small["mla_g_q"], small["mla_g_kv"] = dgq[0], dgkv[0]

    dproj = jnp.concatenate([seg[name] for name in PROJ_ORDER[:-1]] + list(dgl), axis=1)
    dh1 = _mm("proj_dx", dproj, w["w_in"], "nt")
    big["w_in"] = _mm("proj_dw", s["h1"], dproj, "tn", tn=256)

    def pre_mix_bwd(xt, dh, d1, g):
        dxn, dg = _rms_bwd(xt, g, dh)
        return (d1 + dxn,), (dg,)
    dx, small["g_pre_mix"] = _rowwise("pre_mix_bwd", pre_mix_bwd, [x, dh1, dx1], [_row(p["g_pre_mix"])],
                                      [((t_len, D_MODEL), F32)], [((1, D_MODEL), F32)])
    for name in ("g_pre_mix", "g_post_mix", "g_pre_mlp", "g_post_mlp"):
        small[name] = small[name][0]
    return dx, big, small


BIG = ("w_in", "s5_w_glu", "mla_w_uq", "mla_w_ukv", "w_branch", "w_out", "w_ff1", "w_ff2")
SMALL = ("g_pre_mix", "g_post_mix", "g_pre_mlp", "g_post_mlp", "swa_sinks", "s5_lam_re", "s5_lam_im", "s5_log_dt",
         "s5_b_re", "s5_b_im", "s5_c_re", "s5_c_im", "s5_d", "mla_g_q", "mla_g_kv")
WEIGHTS = ("g_pre_mix", "g_post_mix", "g_pre_mlp", "g_post_mlp", "w_in", "swa_sinks", "s5_lam_re", "s5_lam_im", "s5_log_dt",
           "s5_b_re", "s5_b_im", "s5_c_re", "s5_c_im", "s5_d", "s5_w_glu", "mla_g_q", "mla_g_kv", "mla_w_uq", "mla_w_ukv",
           "w_branch", "w_out", "w_ff1", "w_ff2")


def _layer_weights(gathered, l):
    g = {name: gathered[name][:, l] for name in BIG}
    wb = g["w_branch"].transpose(1, 2, 0, 3).reshape(4, 512, D_MODEL)
    return dict(w_in=_proj_cols(_cols_gathered(g["w_in"])), w_glu=_cols_gathered(g["s5_w_glu"]),
                w_uq=_uq_cols(_cols_gathered(g["mla_w_uq"])), w_ukv=_ukv_cols(_cols_gathered(g["mla_w_ukv"])),
                wb=[wb[b] for b in range(4)], w_out=g["w_out"].reshape(D_MODEL, D_MODEL),
                w_ff1=_cols_gathered(g["w_ff1"]), w_ff2=g["w_ff2"].reshape(4 * D_MODEL, D_MODEL))


def _grad_slabs(big):
    wb = jnp.stack(big["wb"])
    return dict(w_in=_cols_slabs(_unproj_cols(big["w_in"])), s5_w_glu=_cols_slabs(big["w_glu"]),
                mla_w_uq=_cols_slabs(_un_uq_cols(big["w_uq"])), mla_w_ukv=_cols_slabs(_un_ukv_cols(big["w_ukv"])),
                w_branch=wb.reshape(4, 512, N_DEV, D_MODEL // N_DEV).transpose(2, 0, 1, 3),
                w_out=big["w_out"].reshape(N_DEV, D_MODEL // N_DEV, D_MODEL),
                w_ff1=_cols_slabs(big["w_ff1"]), w_ff2=big["w_ff2"].reshape(N_DEV, 4 * D_MODEL // N_DEV, D_MODEL))


def kernel(x, g_pre_mix, g_post_mix, g_pre_mlp, g_post_mlp, w_in, swa_sinks, s5_lam_re, s5_lam_im, s5_log_dt, s5_b_re, s5_b_im, s5_c_re, s5_c_im, s5_d, s5_w_glu, mla_g_q, mla_g_kv, mla_w_uq, mla_w_ukv, w_branch, w_out, w_ff1, w_ff2, loss_target, m_g_pre_mix, m_g_post_mix, m_g_pre_mlp, m_g_post_mlp, m_w_in, m_swa_sinks, m_s5_lam_re, m_s5_lam_im, m_s5_log_dt, m_s5_b_re, m_s5_b_im, m_s5_c_re, m_s5_c_im, m_s5_d, m_s5_w_glu, m_mla_g_q, m_mla_g_kv, m_mla_w_uq, m_mla_w_ukv, m_w_branch, m_w_out, m_w_ff1, m_w_ff2, v_g_pre_mix, v_g_post_mix, v_g_pre_mlp, v_g_post_mlp, v_w_in, v_swa_sinks, v_s5_lam_re, v_s5_lam_im, v_s5_log_dt, v_s5_b_re, v_s5_b_im, v_s5_c_re, v_s5_c_im, v_s5_d, v_s5_w_glu, v_mla_g_q, v_mla_g_kv, v_mla_w_uq, v_mla_w_ukv, v_w_branch, v_w_out, v_w_ff1, v_w_ff2):
    weights = dict(g_pre_mix=g_pre_mix, g_post_mix=g_post_mix, g_pre_mlp=g_pre_mlp, g_post_mlp=g_post_mlp, w_in=w_in,
                   swa_sinks=swa_sinks, s5_lam_re=s5_lam_re, s5_lam_im=s5_lam_im, s5_log_dt=s5_log_dt, s5_b_re=s5_b_re,
                   s5_b_im=s5_b_im, s5_c_re=s5_c_re, s5_c_im=s5_c_im, s5_d=s5_d, s5_w_glu=s5_w_glu, mla_g_q=mla_g_q,
                   mla_g_kv=mla_g_kv, mla_w_uq=mla_w_uq, mla_w_ukv=mla_w_ukv, w_branch=w_branch, w_out=w_out, w_ff1=w_ff1,
                   w_ff2=w_ff2)
    m_in = dict(g_pre_mix=m_g_pre_mix, g_post_mix=m_g_post_mix, g_pre_mlp=m_g_pre_mlp, g_post_mlp=m_g_post_mlp, w_in=m_w_in,
                swa_sinks=m_swa_sinks, s5_lam_re=m_s5_lam_re, s5_lam_im=m_s5_lam_im, s5_log_dt=m_s5_log_dt,
                s5_b_re=m_s5_b_re, s5_b_im=m_s5_b_im, s5_c_re=m_s5_c_re, s5_c_im=m_s5_c_im, s5_d=m_s5_d,
                s5_w_glu=m_s5_w_glu, mla_g_q=m_mla_g_q, mla_g_kv=m_mla_g_kv, mla_w_uq=m_mla_w_uq, mla_w_ukv=m_mla_w_ukv,
                w_branch=m_w_branch, w_out=m_w_out, w_ff1=m_w_ff1, w_ff2=m_w_ff2)
    v_in = dict(g_pre_mix=v_g_pre_mix, g_post_mix=v_g_post_mix, g_pre_mlp=v_g_pre_mlp, g_post_mlp=v_g_post_mlp, w_in=v_w_in,
                swa_sinks=v_swa_sinks, s5_lam_re=v_s5_lam_re, s5_lam_im=v_s5_lam_im, s5_log_dt=v_s5_log_dt,
                s5_b_re=v_s5_b_re, s5_b_im=v_s5_b_im, s5_c_re=v_s5_c_re, s5_c_im=v_s5_c_im, s5_d=v_s5_d,
                s5_w_glu=v_s5_w_glu, mla_g_q=v_mla_g_q, mla_g_kv=v_mla_g_kv, mla_w_uq=v_mla_w_uq, mla_w_ukv=v_mla_w_ukv,
                w_branch=v_w_branch, w_out=v_w_out, w_ff1=v_w_ff1, w_ff2=v_w_ff2)
    depth = g_pre_mix.shape[0]
    t_len = x.shape[1]
    tabs = _rope_tables(t_len, 32) + _rope_tables(t_len, 16)

    recv = _gather_two_level("gather_weights", [weights[name].astype(BF16) for name in BIG])
    gathered = dict(zip(BIG, recv))

    xs, saved, layer_w, layer_p = [x[0]], [], [], []
    for l in range(depth):
        layer_w.append(_layer_weights(gathered, l))
        layer_p.append({name: weights[name][l] for name in SMALL})
        x_next, s = _layer_fwd(xs[-1], layer_w[l], layer_p[l], tabs)
        xs.append(x_next)
        saved.append(s)

    def loss_fn(y, tgt):
        err = y - tgt
        part = 0.5 * jnp.sum(jnp.mean(err * err, axis=-1, keepdims=True), axis=0, keepdims=True)
        return (err * (1.0 / D_MODEL),), (jnp.broadcast_to(part, (8, 128)),)
    dx, loss_part = _rowwise("loss", loss_fn, [xs[-1], loss_target[0]], [], [((t_len, D_MODEL), F32)], [((8, 128), F32)])
    loss = lax.psum(loss_part[0, 0], ("x", "y", "c"))

    big_grads, small_grads = [None] * depth, [None] * depth
    for l in reversed(range(depth)):
        dx, big, small = _layer_bwd(xs[l], dx, layer_w[l], layer_p[l], saved[l], tabs)
        big_grads[l], small_grads[l] = _grad_slabs(big), small

    small_vec = jnp.concatenate([jnp.stack([small_grads[l][name] for l in range(depth)]).reshape(-1) for name in SMALL])
    n_small = small_vec.shape[0]
    rows = -(-n_small // 1024) * 8
    small_mat = jnp.pad(small_vec, (0, rows * 128 - n_small)).reshape(rows, 128)
    core = lax.axis_index("c")
    own, send = [], []
    for name in BIG:
        slabs = jnp.stack([big_grads[l][name] for l in range(depth)], axis=1)
        slabs = slabs.reshape((4, 2) + slabs.shape[1:])
        own.append(lax.dynamic_index_in_dim(slabs, core, axis=1, keepdims=False))
        send.append(lax.dynamic_index_in_dim(slabs, 1 - core, axis=1, keepdims=False).astype(BF16))
    got = _pair_exchange("grads_to_sibling", send + [small_mat])
    chip_sums = []
    for name, mine, theirs in zip(BIG, own, got[:-1]):
        flat = (-1, mine.shape[-1])
        (part,) = _rowwise("grads_pair_sum", lambda a, b: (a + b.astype(F32),), [mine.reshape(flat), theirs.reshape(flat)], [],
                           [((int(np.prod(mine.shape[:-1])), mine.shape[-1]), BF16)])
        chip_sums.append(part.reshape(mine.shape))
    small_pair = jnp.where(core == 0, jnp.stack([small_mat, got[-1]]), jnp.stack([got[-1], small_mat]))
    recv = _chip_exchange("grads_to_chips", [(p, True) for p in chip_sums] + [(small_pair, False)])
    small_sum = _sum_slabs("sum_small_grads", recv[-1].reshape((N_DEV,) + small_mat.shape)).reshape(-1)

    grads, deltas, new_m, new_v = {}, {}, {}, {}
    for name, slabs in zip(BIG, recv[:-1]):
        grads[name], deltas[name], new_m[name], new_v[name] = _adamw("adamw_" + name, weights[name], slabs, m_in[name],
                                                                     v_in[name], 4)
    off = 0
    for name in SMALL:
        size = int(np.prod(weights[name].shape))
        g = small_sum[off:off + size].reshape(weights[name].shape)
        off += size
        grads[name], deltas[name], new_m[name], new_v[name] = _adamw("adamw_" + name, weights[name], g, m_in[name],
                                                                     v_in[name], False)
    return (loss, dx[None], *[grads[n] for n in WEIGHTS], *[deltas[n] for n in WEIGHTS], *[new_m[n] for n in WEIGHTS],
            *[new_v[n] for n in WEIGHTS])
```

```python
import functools
import math

import numpy as np
import jax
import jax.numpy as jnp
from jax import lax
from jax.experimental import pallas as pl
from jax.experimental.pallas import tpu as pltpu

F32, BF16 = jnp.float32, jnp.bfloat16
EPS = 1e-6
NEG_INF = -1e30
ROPE_THETA = 10000.0
D_MODEL = 1024
N_DEV = 8
BLOCK = 128
SWA_HEADS, SWA_KV = 8, 2
RET_HEADS, RET_DK, RET_DV = 4, 64, 128
MLA_HEADS = 8
S5_GROUPS, S5_STATE, S5_GROUP = 32, 64, 16
S5_CH = S5_GROUPS * S5_STATE
SCAN_LANES = 256
SCAN_SEG = 8
ADAM_LR, ADAM_B1, ADAM_B2, ADAM_EPS, ADAM_WD, ADAM_STEP = 0.001, 0.9, 0.999, 1e-08, 0.01, 10
VMEM_MB = 56

PROJ = dict(sq=(0, 512), su=(512, 512), rv=(1024, 512), rg=(1536, 512), rq=(2048, 256), rk=(2304, 256),
            cq=(2560, 256), sk=(2816, 128), sv=(2944, 128), ckv=(3072, 128), kr=(3200, 128), gates=(3328, 4096))
PROJ_W = 7424
REF_SPLIT = dict(sq=(0, 512), sk=(512, 128), sv=(640, 128), su=(768, 512), rq=(1280, 256), rk=(1536, 256),
                 rv=(1792, 512), rg=(2304, 512), cq=(2816, 256), ckv=(3072, 128), kr=(3200, 32), gates=(3232, 4096))
PROJ_ORDER = ("sq", "su", "rv", "rg", "rq", "rk", "cq", "sk", "sv", "ckv", "kr", "gates")
REF_ORDER = ("sq", "sk", "sv", "su", "rq", "rk", "rv", "rg", "cq", "ckv", "kr", "gates")

NN = (((1,), (0,)), ((), ()))
NT = (((1,), (1,)), ((), ()))
TN = (((0,), (0,)), ((), ()))


def _params(sem):
    return pltpu.CompilerParams(dimension_semantics=sem, vmem_limit_bytes=VMEM_MB * 2 ** 20)


def _raw_dot(a, b, dims):
    return lax.dot_general(a.astype(BF16), b.astype(BF16), dims, preferred_element_type=F32)


@jax.custom_vjp
def _dnn(a, b):
    return _raw_dot(a, b, NN)


@jax.custom_vjp
def _dnt(a, b):
    return _raw_dot(a, b, NT)


@jax.custom_vjp
def _dtn(a, b):
    return _raw_dot(a, b, TN)


_dnn.defvjp(lambda a, b: (_dnn(a, b), (a, b)),
            lambda r, g: (_dnt(g, r[1]).astype(r[0].dtype), _dtn(r[0], g).astype(r[1].dtype)))
_dnt.defvjp(lambda a, b: (_dnt(a, b), (a, b)),
            lambda r, g: (_dnn(g, r[1]).astype(r[0].dtype), _dtn(g, r[0]).astype(r[1].dtype)))
_dtn.defvjp(lambda a, b: (_dtn(a, b), (a, b)),
            lambda r, g: (_dnt(r[1], g).astype(r[0].dtype), _dnn(r[0], g).astype(r[1].dtype)))


class _Cols:
    def __init__(self, arr, width, index):
        self.arr, self.width, self.index = arr, width, index


def _seg(proj, name):
    off, width = PROJ[name]
    return _Cols(proj, width, off // width)


def _row_spec(shape, tile):
    lead = len(shape) - 2
    return pl.BlockSpec(tuple(shape[:-2]) + (tile, shape[-1]), lambda i, lead=lead: (0,) * lead + (i, 0))


def _const_spec(shape):
    nd = len(shape)
    return pl.BlockSpec(tuple(shape), lambda *_, nd=nd: (0,) * nd)


def _rowwise(name, fn, rows, consts, out_rows, out_accs=(), tile=256):
    arrs = [r.arr if isinstance(r, _Cols) else r for r in rows]
    t_len = arrs[0].shape[-2]
    tile = min(tile, t_len)
    n = t_len // tile
    in_specs = []
    for r in rows:
        if isinstance(r, _Cols):
            in_specs.append(pl.BlockSpec((tile, r.width), lambda i, c=r.index: (i, c)))
        else:
            in_specs.append(_row_spec(r.shape, tile))
    in_specs += [_const_spec(c.shape) for c in consts]
    out_specs = [_row_spec(s, tile) for s, _ in out_rows] + [_const_spec(s) for s, _ in out_accs]
    out_shape = [jax.ShapeDtypeStruct(s, d) for s, d in tuple(out_rows) + tuple(out_accs)]
    n_in, n_or = len(rows) + len(consts), len(out_rows)

    def body(*refs):
        res = fn(*[r[...] for r in refs[:n_in]])
        row_out, acc_out = res if out_accs else (res, ())
        for ref, val in zip(refs[n_in:n_in + n_or], row_out):
            if isinstance(val, (list, tuple)):
                for h, v_h in enumerate(val):
                    ref[h] = v_h.astype(ref.dtype)
            else:
                ref[...] = val.astype(ref.dtype)
        first = pl.program_id(0) == 0
        for ref, val in zip(refs[n_in + n_or:], acc_out):
            @pl.when(first)
            def _(ref=ref, val=val):
                ref[...] = val

            @pl.when(jnp.logical_not(first))
            def _(ref=ref, val=val):
                ref[...] += val

    return pl.pallas_call(body, name=name, grid=(n,), in_specs=in_specs, out_specs=out_specs, out_shape=out_shape,
                          compiler_params=_params(("arbitrary",)))(*arrs, *consts)


MM_VMEM_BUDGET = 44 * 2 ** 20


def _pick_tn(n, io_col_bytes, tmp_col_bytes, fixed_bytes):
    best = None
    for cand in range(128, n + 1, 128):
        if n % cand == 0 and 2 * (fixed_bytes + cand * io_col_bytes) + cand * tmp_col_bytes <= MM_VMEM_BUDGET:
            best = cand
    return best if best is not None else min(n, 128)


def _mm(name, a, b, mode, out_dtypes=(F32,), tm=512, tn=None, epi=None, epi_in=()):
    if mode == "nn":
        (m, k), n = a.shape, b.shape[1]
    elif mode == "nt":
        (m, k), n = a.shape, b.shape[0]
    else:
        (k, m), n = a.shape, b.shape[1]
    tm = min(tm, m)
    if tn is None:
        io_col = k * b.dtype.itemsize + tm * (sum(jnp.dtype(d).itemsize for d in out_dtypes) + sum(e.dtype.itemsize for e in epi_in))
        tmp_col = tm * 4 * (2 if epi is not None else 1)
        tn = _pick_tn(n, io_col, tmp_col, tm * k * a.dtype.itemsize)
    tn = min(tn, n)
    assert m % tm == 0 and n % tn == 0, (name, m, n, tm, tn)
    a_spec = pl.BlockSpec((k, tm), lambda i, j: (0, i)) if mode == "tn" else pl.BlockSpec((tm, k), lambda i, j: (i, 0))
    b_spec = pl.BlockSpec((tn, k), lambda i, j: (j, 0)) if mode == "nt" else pl.BlockSpec((k, tn), lambda i, j: (0, j))
    dims = dict(nn=NN, nt=NT, tn=TN)[mode]
    o_spec = pl.BlockSpec((tm, tn), lambda i, j: (i, j))
    n_e = len(epi_in)

    def body(a_ref, b_ref, *refs):
        acc = _raw_dot(a_ref[...], b_ref[...], dims)
        outs = epi(acc, *[r[...] for r in refs[:n_e]]) if epi is not None else (acc,)
        for ref, val in zip(refs[n_e:], outs):
            ref[...] = val.astype(ref.dtype)

    res = pl.pallas_call(body, name=name, grid=(m // tm, n // tn),
                         in_specs=[a_spec, b_spec] + [o_spec] * n_e, out_specs=[o_spec] * len(out_dtypes),
                         out_shape=[jax.ShapeDtypeStruct((m, n), d) for d in out_dtypes],
                         compiler_params=_params(("parallel", "parallel")))(a, b, *epi_in)
    return res[0] if len(out_dtypes) == 1 else res


def _rms(x, g):
    return x * lax.rsqrt(jnp.mean(x * x, axis=-1, keepdims=True) + EPS) * g


def _rms_bwd(x, g, dy):
    _, vjp = jax.vjp(_rms, x, g)
    return vjp(dy)


def _rope_tables(t_len, half):
    dim = 2 * half
    inv = 1.0 / (ROPE_THETA ** (jnp.arange(0, dim, 2, dtype=F32) / dim))
    ang = jnp.arange(t_len, dtype=F32)[:, None] * inv[None, :]
    c, s = jnp.cos(ang), jnp.sin(ang)
    reps = 128 // dim
    return jnp.tile(jnp.concatenate([c, c], 1), (1, reps)), jnp.tile(jnp.concatenate([-s, s], 1), (1, reps))


def _rope(x, cos, sin, half):
    w = x.shape[-1]
    reps = w // 128
    if reps > 1:
        cos, sin = jnp.tile(cos, (1, reps)), jnp.tile(sin, (1, reps))
    lane = lax.broadcasted_iota(jnp.int32, x.shape, 1)
    partner = jnp.where((lane % (2 * half)) < half, pltpu.roll(x, w - half, 1), pltpu.roll(x, half, 1))
    return x * cos + partner * sin


SWA_GROUP = SWA_HEADS // SWA_KV


def _swa_mask(i):
    rows = SWA_GROUP * BLOCK
    qi = lax.broadcasted_iota(jnp.int32, (rows, 2 * BLOCK), 0) % BLOCK + BLOCK
    kj = lax.broadcasted_iota(jnp.int32, (rows, 2 * BLOCK), 1)
    diff = qi - kj
    return (diff >= 0) & (diff < BLOCK) & ((i > 0) | (kj >= BLOCK))


def _swa_stack(ref, j, dtype=None):
    parts = [ref[:, 64 * (SWA_GROUP * j + g):64 * (SWA_GROUP * j + g) + 64] for g in range(SWA_GROUP)]
    out = jnp.concatenate(parts, axis=0)
    return out if dtype is None else out.astype(dtype)


def _swa_sinks(s_ref, j):
    return jnp.concatenate([jnp.broadcast_to(s_ref[SWA_GROUP * j + g:SWA_GROUP * j + g + 1, 0:1], (BLOCK, 1))
                            for g in range(SWA_GROUP)], axis=0)


def _swa_head(q, kw, vw, sink, mask):
    sc = _dnt(q, kw) * (64 ** -0.5)
    sc = jnp.where(mask, sc, NEG_INF)
    m = lax.stop_gradient(jnp.maximum(jnp.max(sc, axis=-1, keepdims=True), sink))
    p = jnp.exp(sc - m)
    denom = jnp.sum(p, axis=-1, keepdims=True) + jnp.exp(sink - m)
    return _dnn(p / denom, vw)


def _swa_prep(proj, cos, sin):
    def fn(sq, sk, sv, c, s):
        return _rope(sq, c, s, 32), _rope(sk, c, s, 32), sv
    t_len = proj.shape[0]
    return _rowwise("swa_prep", fn, [_seg(proj, "sq"), _seg(proj, "sk"), _seg(proj, "sv"), cos, sin], [],
                    [((t_len, 512), BF16), ((t_len, 128), BF16), ((t_len, 128), BF16)])


def _swa_fwd(q, k, v, sinks):
    t_len = q.shape[0]
    nb = t_len // BLOCK

    def body(q_ref, kp_ref, kc_ref, vp_ref, vc_ref, s_ref, o_ref):
        mask = _swa_mask(pl.program_id(0))
        for j in range(SWA_KV):
            cols = slice(64 * j, 64 * j + 64)
            kw = jnp.concatenate([kp_ref[:, cols], kc_ref[:, cols]], axis=0)
            vw = jnp.concatenate([vp_ref[:, cols], vc_ref[:, cols]], axis=0)
            o = _swa_head(_swa_stack(q_ref, j), kw, vw, _swa_sinks(s_ref, j), mask)
            for g in range(SWA_GROUP):
                h = SWA_GROUP * j + g
                o_ref[:, 64 * h:64 * h + 64] = o[BLOCK * g:BLOCK * g + BLOCK].astype(o_ref.dtype)

    prev = lambda i: (jnp.maximum(i - 1, 0), 0)
    cur = lambda i: (i, 0)
    return pl.pallas_call(
        body, name="swa_fwd", grid=(nb,),
        in_specs=[pl.BlockSpec((BLOCK, 512), cur), pl.BlockSpec((BLOCK, 128), prev), pl.BlockSpec((BLOCK, 128), cur),
                  pl.BlockSpec((BLOCK, 128), prev), pl.BlockSpec((BLOCK, 128), cur), _const_spec((8, 128))],
        out_specs=pl.BlockSpec((BLOCK, 512), cur), out_shape=jax.ShapeDtypeStruct((t_len, 512), BF16),
        compiler_params=_params(("parallel",)))(q, k, k, v, v, sinks)


def _swa_bwd(q, k, v, sinks, dy):
    t_len = q.shape[0]
    nb = t_len // BLOCK

    def body(q_ref, kp_ref, kc_ref, vp_ref, vc_ref, s_ref, dy_ref, dq_ref, dk_ref, dv_ref, ds_ref, k_acc, v_acc):
        i = pl.program_id(0)

        @pl.when(i == 0)
        def _():
            k_acc[...] = jnp.zeros_like(k_acc)
            v_acc[...] = jnp.zeros_like(v_acc)
            ds_ref[...] = jnp.zeros_like(ds_ref)

        @pl.when(i < nb)
        def _():
            mask = _swa_mask(i)
            for j in range(SWA_KV):
                cols = slice(64 * j, 64 * j + 64)
                kw = jnp.concatenate([kp_ref[:, cols], kc_ref[:, cols]], axis=0)
                vw = jnp.concatenate([vp_ref[:, cols], vc_ref[:, cols]], axis=0)
                _, vjp = jax.vjp(lambda a, b, c, d: _swa_head(a, b, c, d, mask), _swa_stack(q_ref, j), kw, vw,
                                 _swa_sinks(s_ref, j))
                dq, dkw, dvw, dsink = vjp(_swa_stack(dy_ref, j, F32))
                dkw, dvw = dkw.astype(F32), dvw.astype(F32)
                for g in range(SWA_GROUP):
                    h = SWA_GROUP * j + g
                    rows = slice(BLOCK * g, BLOCK * g + BLOCK)
                    dq_ref[:, 64 * h:64 * h + 64] = dq[rows].astype(F32)
                    ds_ref[h:h + 1, :] += jnp.broadcast_to(jnp.sum(dsink[rows], axis=0, keepdims=True), (1, 128))
                dk_ref[:, cols] = k_acc[:, cols] + dkw[:BLOCK]
                dv_ref[:, cols] = v_acc[:, cols] + dvw[:BLOCK]
                k_acc[:, cols] = dkw[BLOCK:]
                v_acc[:, cols] = dvw[BLOCK:]

        @pl.when(i == nb)
        def _():
            dk_ref[...] = k_acc[...]
            dv_ref[...] = v_acc[...]

    cur = lambda i: (jnp.minimum(i, nb - 1), 0)
    prev = lambda i: (jnp.maximum(jnp.minimum(i, nb - 1) - 1, 0), 0)
    done = lambda i: (jnp.maximum(i - 1, 0), 0)
    return pl.pallas_call(
        body, name="swa_bwd", grid=(nb + 1,),
        in_specs=[pl.BlockSpec((BLOCK, 512), cur), pl.BlockSpec((BLOCK, 128), prev), pl.BlockSpec((BLOCK, 128), cur),
                  pl.BlockSpec((BLOCK, 128), prev), pl.BlockSpec((BLOCK, 128), cur), _const_spec((8, 128)),
                  pl.BlockSpec((BLOCK, 512), cur)],
        out_specs=[pl.BlockSpec((BLOCK, 512), cur), pl.BlockSpec((BLOCK, 128), done), pl.BlockSpec((BLOCK, 128), done),
                   _const_spec((8, 128))],
        out_shape=[jax.ShapeDtypeStruct((t_len, 512), F32), jax.ShapeDtypeStruct((t_len, 128), F32),
                   jax.ShapeDtypeStruct((t_len, 128), F32), jax.ShapeDtypeStruct((8, 128), F32)],
        scratch_shapes=[pltpu.VMEM((BLOCK, 128), F32), pltpu.VMEM((BLOCK, 128), F32)],
        compiler_params=_params(("arbitrary",)))(q, k, k, v, v, sinks, dy)


def _swa_unprep(dq, dk, dv, cos, sin):
    def fn(dq_t, dk_t, dv_t, c, s):
        return _rope(dq_t, c, -s, 32), _rope(dk_t, c, -s, 32), dv_t
    t_len = dq.shape[0]
    return _rowwise("swa_unprep", fn, [dq, dk, dv, cos, sin], [],
                    [((t_len, 512), BF16), ((t_len, 128), BF16), ((t_len, 128), BF16)])


def _s5_discretize(lam_re, lam_im, log_dt, b_re, b_im):
    dt = jnp.exp(log_dt)
    mag = jnp.exp(lam_re * dt)
    ab_re, ab_im = mag * jnp.cos(lam_im * dt), mag * jnp.sin(lam_im * dt)
    den = lam_re * lam_re + lam_im * lam_im
    nr, ni = ab_re - 1.0, ab_im
    f_re = (nr * lam_re + ni * lam_im) / den
    f_im = (ni * lam_re - nr * lam_im) / den
    return ab_re, ab_im, f_re * b_re - f_im * b_im, f_re * b_im + f_im * b_re


def _s5_param_fwd(lam_re, lam_im, log_dt, b_re, b_im):
    def body(*refs):
        outs = _s5_discretize(*[r[...] for r in refs[:5]])
        for ref, val in zip(refs[5:], outs):
            ref[...] = val
    n = S5_CH
    return pl.pallas_call(body, name="s5_param_fwd",
                          out_shape=[jax.ShapeDtypeStruct((n, 1), F32)] * 2 + [jax.ShapeDtypeStruct((n, 16), F32)] * 2,
                          )(lam_re, lam_im, log_dt, b_re, b_im)


def _s5_param_bwd(lam_re, lam_im, log_dt, b_re, b_im, group_sum, d_ab_re, d_ab_im, d_bb_re, d_bb_im):
    def body(*refs):
        ins = [r[...] for r in refs[:10]]
        _, vjp = jax.vjp(_s5_discretize, *ins[:5])
        d_lr, d_li, d_ldt, d_br, d_bi = vjp(tuple(ins[6:10]))
        o_lr, o_li, o_ldt, o_br, o_bi = refs[10:]
        o_lr[...], o_li[...], o_br[...], o_bi[...] = d_lr, d_li, d_br, d_bi
        o_ldt[...] = lax.dot_general(ins[5], jnp.broadcast_to(d_ldt, (S5_CH, 128)), NN,
                                     precision=lax.Precision.HIGHEST, preferred_element_type=F32)
    n = S5_CH
    return pl.pallas_call(body, name="s5_param_bwd",
                          out_shape=[jax.ShapeDtypeStruct((n, 1), F32), jax.ShapeDtypeStruct((n, 1), F32),
                                     jax.ShapeDtypeStruct((S5_GROUPS, 128), F32),
                                     jax.ShapeDtypeStruct((n, 16), F32), jax.ShapeDtypeStruct((n, 16), F32)],
                          )(lam_re, lam_im, log_dt, b_re, b_im, group_sum, d_ab_re, d_ab_im, d_bb_re, d_bb_im)


def _s5_expand(name, u, w):
    t_len = u.shape[0]
    tile = min(512, t_len)

    def body(u_ref, w_ref, o_ref):
        o_ref[...] = _raw_dot(u_ref[...], w_ref[...], NN)

    return pl.pallas_call(
        body, name=name, grid=(t_len // tile, 8),
        in_specs=[pl.BlockSpec((tile, 128), lambda i, bb: (i, bb % 4)), pl.BlockSpec((None, 128, 512), lambda i, bb: (bb, 0, 0))],
        out_specs=pl.BlockSpec((None, tile, 512), lambda i, bb: (bb // 4, i, bb % 4)),
        out_shape=jax.ShapeDtypeStruct((2, t_len, S5_CH), F32), compiler_params=_params(("parallel", "parallel")))(u, w)


def _s5_contract(name, x, w):
    t_len = x.shape[1]
    tile = min(512, t_len)

    def body(xr_ref, xi_ref, wr_ref, wi_ref, o_ref):
        o_ref[...] = _raw_dot(xr_ref[...], wr_ref[...], NN) + _raw_dot(xi_ref[...], wi_ref[...], NN)

    return pl.pallas_call(
        body, name=name, grid=(t_len // tile, 4),
        in_specs=[pl.BlockSpec((None, tile, 512), lambda i, b: (0, i, b)), pl.BlockSpec((None, tile, 512), lambda i, b: (1, i, b)),
                  pl.BlockSpec((None, 512, 128), lambda i, b: (b, 0, 0)), pl.BlockSpec((None, 512, 128), lambda i, b: (4 + b, 0, 0))],
        out_specs=pl.BlockSpec((tile, 128), lambda i, b: (i, b)),
        out_shape=jax.ShapeDtypeStruct((t_len, 512), F32), compiler_params=_params(("parallel", "parallel")))(x, x, w, w)


def _s5_band_grad(name, narrow, wide, narrow_first):
    t_len = narrow.shape[0]

    def body(n_ref, w_ref, o_ref):
        if narrow_first:
            o_ref[...] = _raw_dot(n_ref[...], w_ref[...], TN)
        else:
            o_ref[...] = _raw_dot(w_ref[...], n_ref[...], TN)

    shape = (8, 128, 512) if narrow_first else (8, 512, 128)
    return pl.pallas_call(
        body, name=name, grid=(8,),
        in_specs=[pl.BlockSpec((t_len, 128), lambda bb: (0, bb % 4)), pl.BlockSpec((None, t_len, 512), lambda bb: (bb // 4, 0, bb % 4))],
        out_specs=pl.BlockSpec((None,) + shape[1:], lambda bb: (bb, 0, 0)),
        out_shape=jax.ShapeDtypeStruct(shape, F32), compiler_params=_params(("parallel",)))(narrow, wide)


def _s5_scan(name, bu, a_re, a_im, states=None, carry_in=None):
    t_len = bu.shape[1]
    n = t_len // SCAN_SEG
    log_n = int(round(math.log2(n)))
    assert 2 ** log_n == n
    reverse = states is not None
    lanes = SCAN_LANES // 2 if reverse else SCAN_LANES
    nblk = S5_CH // lanes

    def body(*refs):
        if reverse:
            b_ref, ar_ref, ai_ref, x_ref, cin_ref, o_ref, dar_ref, dai_ref, c_scr = refs
            xr_ref, xi_ref, cr_ref, ci_ref = x_ref.at[0], x_ref.at[1], cin_ref.at[0], cin_ref.at[1]
        else:
            b_ref, ar_ref, ai_ref, o_ref, co_ref, c_scr = refs
            cor_ref, coi_ref = co_ref.at[0], co_ref.at[1]
        br_ref, bi_ref, or_ref, oi_ref = b_ref.at[0], b_ref.at[1], o_ref.at[0], o_ref.at[1]
        ar = ar_ref[...]
        ai = -ai_ref[...] if reverse else ai_ref[...]

        def tile_at(i):
            return pl.multiple_of((n - 1 - i if reverse else i) * SCAN_SEG, SCAN_SEG)

        def advance(s, row):
            sr, si = s
            return (ar * sr - ai * si + br_ref[pl.ds(row, SCAN_SEG), :], ar * si + ai * sr + bi_ref[pl.ds(row, SCAN_SEG), :])

        zero = jnp.zeros((SCAN_SEG, lanes), F32)
        fin_r, fin_i = lax.fori_loop(0, n, lambda i, s: advance(s, tile_at(i)), (zero, zero), unroll=8)
        pr, pi = ar[0:1], ai[0:1]
        for _ in range(log_n):
            pr, pi = pr * pr - pi * pi, 2.0 * pr * pi
        er = ei = jnp.zeros((1, lanes), F32)
        for step in range(SCAN_SEG):
            j = SCAN_SEG - 1 - step if reverse else step
            c_scr[0, j:j + 1, :] = er
            c_scr[1, j:j + 1, :] = ei
            er, ei = pr * er - pi * ei + fin_r[j:j + 1], pr * ei + pi * er + fin_i[j:j + 1]
        entry = (c_scr[0], c_scr[1])

        if not reverse:
            cor_ref[...], coi_ref[...] = entry

            def step2(i, s):
                row = tile_at(i)
                s = advance(s, row)
                or_ref[pl.ds(row, SCAN_SEG), :] = s[0]
                oi_ref[pl.ds(row, SCAN_SEG), :] = s[1]
                return s
            lax.fori_loop(0, n, step2, entry, unroll=8)
        else:
            def emit(s, row, xr_prev, xi_prev, acc):
                s = advance(s, row)
                or_ref[pl.ds(row, SCAN_SEG), :] = s[0]
                oi_ref[pl.ds(row, SCAN_SEG), :] = s[1]
                return s, (acc[0] + s[0] * xr_prev + s[1] * xi_prev, acc[1] + s[1] * xr_prev - s[0] * xi_prev)

            def step2(i, carry):
                s, acc = carry
                row = tile_at(i)
                prev = pl.multiple_of(row - SCAN_SEG, SCAN_SEG)
                return emit(s, row, xr_ref[pl.ds(prev, SCAN_SEG), :], xi_ref[pl.ds(prev, SCAN_SEG), :], acc)
            s, acc = lax.fori_loop(0, n - 1, step2, (entry, (zero, zero)), unroll=8)
            s, acc = emit(s, 0, cr_ref[...], ci_ref[...], acc)
            dar_ref[...] = jnp.sum(acc[0], axis=0, keepdims=True)
            dai_ref[...] = jnp.sum(acc[1], axis=0, keepdims=True)

    planes = pl.BlockSpec((2, t_len, lanes), lambda b: (0, 0, b))
    tile8 = pl.BlockSpec((SCAN_SEG, lanes), lambda b: (0, b))
    entry_spec = pl.BlockSpec((2, SCAN_SEG, lanes), lambda b: (0, 0, b))
    row1 = pl.BlockSpec((1, lanes), lambda b: (0, b))
    seq = jax.ShapeDtypeStruct((2, t_len, S5_CH), F32)
    if reverse:
        in_specs = [planes, tile8, tile8, planes, entry_spec]
        args = (bu, a_re, a_im, states, carry_in)
        out_specs = [planes, row1, row1]
        out_shape = [seq, jax.ShapeDtypeStruct((1, S5_CH), F32), jax.ShapeDtypeStruct((1, S5_CH), F32)]
    else:
        in_specs = [planes, tile8, tile8]
        args = (bu, a_re, a_im)
        out_specs = [planes, entry_spec]
        out_shape = [seq, jax.ShapeDtypeStruct((2, SCAN_SEG, S5_CH), F32)]
    return pl.pallas_call(body, name=name, grid=(nblk,), in_specs=in_specs, out_specs=out_specs, out_shape=out_shape,
                          scratch_shapes=[pltpu.VMEM((2, SCAN_SEG, lanes), F32)],
                          compiler_params=_params(("parallel",)))(*args)


def _ret_consts():
    h = np.arange(RET_HEADS, dtype=np.float32)
    log_gamma = np.log1p(-np.exp2(-5.0 - h)).astype(np.float32)
    idx = np.arange(BLOCK, dtype=np.float32)
    diff = idx[:, None] - idx[None, :]
    decay = np.where(diff >= 0, np.exp(log_gamma[:, None, None] * np.maximum(diff, 0.0)), 0.0).astype(np.float32)
    k_w = np.exp(log_gamma[:, None] * (BLOCK - 1 - idx)[None, :]).astype(np.float32)[:, :, None]
    q_w = np.exp(log_gamma[:, None] * (idx + 1.0)[None, :]).astype(np.float32)[:, :, None]
    chunk_decay = [float(v) for v in np.exp(log_gamma * BLOCK).astype(np.float32)]
    return jnp.asarray(decay), jnp.asarray(q_w), jnp.asarray(k_w), chunk_decay


def _ret_head(q, k, v, g, prev, decay, q_w, k_w, chunk_decay):
    inner_s = _dnt(q, k) * decay
    y = _dnn(inner_s, v) + _dnn(q * q_w, prev)
    mu = jnp.mean(y, axis=-1, keepdims=True)
    var = jnp.mean(jnp.square(y - mu), axis=-1, keepdims=True)
    yn = (y - mu) * lax.rsqrt(var + EPS)
    out = g * jax.nn.sigmoid(g) * yn
    return out, prev * chunk_decay + _dtn(k * k_w, v)


def _ret_prep(proj, cos, sin):
    def fn(rq, rk, c, s):
        return _rope(rq, c, s, 32), _rope(rk, c, s, 32) * (RET_DK ** -0.5)
    t_len = proj.shape[0]
    return _rowwise("ret_prep", fn, [_seg(proj, "rq"), _seg(proj, "rk"), cos, sin], [],
                    [((t_len, 256), F32), ((t_len, 256), F32)])


def _ret_unprep(dq, dk, cos, sin):
    def fn(dq_t, dk_t, c, s):
        return _rope(dq_t, c, -s, 32), _rope(dk_t * (RET_DK ** -0.5), c, -s, 32)
    t_len = dq.shape[0]
    return _rowwise("ret_unprep", fn, [dq, dk, cos, sin], [], [((t_len, 256), BF16), ((t_len, 256), BF16)])


def _ret_fwd(q, k, proj):
    t_len = q.shape[0]
    nc = t_len // BLOCK
    decay, q_w, k_w, chunk_decay = _ret_consts()

    def body(q_ref, k_ref, v_ref, g_ref, dec_ref, qw_ref, kw_ref, o_ref, prev_ref, state):
        @pl.when(pl.program_id(0) == 0)
        def _():
            state[...] = jnp.zeros_like(state)
        for h in range(RET_HEADS):
            qk, vv = slice(64 * h, 64 * h + 64), slice(128 * h, 128 * h + 128)
            prev = state[h]
            prev_ref[h] = prev
            out, nxt = _ret_head(q_ref[:, qk], k_ref[:, qk], v_ref[:, vv], g_ref[:, vv], prev,
                                 dec_ref[h], qw_ref[h], kw_ref[h], chunk_decay[h])
            o_ref[:, vv] = out.astype(o_ref.dtype)
            state[h] = nxt

    chunk = lambda w, c=0: pl.BlockSpec((BLOCK, w), lambda i, c=c: (i, c))
    return pl.pallas_call(
        body, name="ret_fwd", grid=(nc,),
        in_specs=[chunk(256), chunk(256), chunk(512, PROJ["rv"][0] // 512), chunk(512, PROJ["rg"][0] // 512),
                  _const_spec(decay.shape), _const_spec(q_w.shape), _const_spec(k_w.shape)],
        out_specs=[chunk(512), pl.BlockSpec((None, RET_HEADS, RET_DK, RET_DV), lambda i: (i, 0, 0, 0))],
        out_shape=[jax.ShapeDtypeStruct((t_len, 512), BF16), jax.ShapeDtypeStruct((nc, RET_HEADS, RET_DK, RET_DV), F32)],
        scratch_shapes=[pltpu.VMEM((RET_HEADS, RET_DK, RET_DV), F32)],
        compiler_params=_params(("arbitrary",)))(q, k, proj, proj, decay, q_w, k_w)


def _ret_bwd(q, k, proj, prevs, dy):
    t_len = q.shape[0]
    nc = t_len // BLOCK
    decay, q_w, k_w, chunk_decay = _ret_consts()

    def body(q_ref, k_ref, v_ref, g_ref, prev_ref, dy_ref, dec_ref, qw_ref, kw_ref, dq_ref, dk_ref, dv_ref, dg_ref, dstate):
        @pl.when(pl.program_id(0) == 0)
        def _():
            dstate[...] = jnp.zeros_like(dstate)
        for h in range(RET_HEADS):
            qk, vv = slice(64 * h, 64 * h + 64), slice(128 * h, 128 * h + 128)
            head = functools.partial(_ret_head, decay=dec_ref[h], q_w=qw_ref[h], k_w=kw_ref[h], chunk_decay=chunk_decay[h])
            _, vjp = jax.vjp(head, q_ref[:, qk], k_ref[:, qk], v_ref[:, vv], g_ref[:, vv], prev_ref[h])
            dq, dk, dv, dg, dprev = vjp((dy_ref[:, vv].astype(F32), dstate[h]))
            dq_ref[:, qk], dk_ref[:, qk] = dq, dk
            dv_ref[:, vv], dg_ref[:, vv] = dv.astype(dv_ref.dtype), dg.astype(dg_ref.dtype)
            dstate[h] = dprev

    chunk = lambda w, c=0: pl.BlockSpec((BLOCK, w), lambda i, c=c: (nc - 1 - i, c))
    return pl.pallas_call(
        body, name="ret_bwd", grid=(nc,),
        in_specs=[chunk(256), chunk(256), chunk(512, PROJ["rv"][0] // 512), chunk(512, PROJ["rg"][0] // 512),
                  pl.BlockSpec((None, RET_HEADS, RET_DK, RET_DV), lambda i: (nc - 1 - i, 0, 0, 0)), chunk(512),
                  _const_spec(decay.shape), _const_spec(q_w.shape), _const_spec(k_w.shape)],
        out_specs=[chunk(256), chunk(256), chunk(512), chunk(512)],
        out_shape=[jax.ShapeDtypeStruct((t_len, 256), F32), jax.ShapeDtypeStruct((t_len, 256), F32),
                   jax.ShapeDtypeStruct((t_len, 512), BF16), jax.ShapeDtypeStruct((t_len, 512), BF16)],
        scratch_shapes=[pltpu.VMEM((RET_HEADS, RET_DK, RET_DV), F32)],
        compiler_params=_params(("arbitrary",)))(q, k, proj, proj, prevs, dy, decay, q_w, k_w)


MLA_SCALE = 96 ** -0.5


def _mla_prep(qf, kvf, proj, cos, sin):
    def fn(q, kv, kr, c, s):
        q_rope = _rope(q[:, 512:768], c, s, 16)
        k_rope = _rope(kr, c, s, 16)[:, :32]
        zero = jnp.zeros_like(k_rope)
        qs = [jnp.concatenate([q[:, 64 * h:64 * h + 64], q_rope[:, 32 * h:32 * h + 32], zero], axis=1) for h in range(MLA_HEADS)]
        ks = [jnp.concatenate([kv[:, 64 * h:64 * h + 64], k_rope, zero], axis=1) for h in range(MLA_HEADS)]
        vs = [kv[:, 512 + 64 * h:576 + 64 * h] for h in range(MLA_HEADS)]
        return qs, ks, vs
    t_len = qf.shape[0]
    return _rowwise("mla_prep", fn, [qf, kvf, _seg(proj, "kr"), cos, sin], [],
                    [((8, t_len, 128), BF16), ((8, t_len, 128), BF16), ((8, t_len, 64), BF16)])


def _mla_unprep(dq, dk, dv, cos, sin):
    def fn(dq_t, dk_t, dv_t, c, s):
        q_rope = _rope(jnp.concatenate([dq_t[h][:, 64:96] for h in range(MLA_HEADS)], axis=1), c, -s, 16)
        d_qf = jnp.concatenate([dq_t[h][:, :64] for h in range(MLA_HEADS)] + [q_rope], axis=1)
        d_kvf = jnp.concatenate([dk_t[h][:, :64] for h in range(MLA_HEADS)] + [dv_t[h] for h in range(MLA_HEADS)], axis=1)
        k_rope = dk_t[0][:, 64:96]
        for h in range(1, MLA_HEADS):
            k_rope = k_rope + dk_t[h][:, 64:96]
        d_kr = _rope(jnp.concatenate([k_rope, jnp.zeros((k_rope.shape[0], 96), F32)], axis=1), c, -s, 16)
        return d_qf, d_kvf, d_kr
    t_len = dq.shape[1]
    return _rowwise("mla_unprep", fn, [dq, dk, dv, cos, sin], [],
                    [((t_len, 768), BF16), ((t_len, 1024), BF16), ((t_len, 128), BF16)])


MLA_EXP2 = MLA_SCALE * math.log2(math.e)


def _diag_mask(tq):
    return lax.broadcasted_iota(jnp.int32, (tq, tq), 1) <= lax.broadcasted_iota(jnp.int32, (tq, tq), 0)


def _tri_steps(nq, key_major):
    if key_major:
        pairs = [(i, j) for j in range(nq) for i in range(j, nq)]
    else:
        pairs = [(i, j) for i in range(nq) for j in range(i + 1)]
    return jnp.asarray([p[0] for p in pairs], jnp.int32), jnp.asarray([p[1] for p in pairs], jnp.int32)


def _mla_fwd(q, k, v):
    t_len = q.shape[1]
    tq = min(512, t_len)
    nq = t_len // tq
    qi, kj = _tri_steps(nq, False)

    def body(i_ref, j_ref, q_ref, k_ref, v_ref, o_ref, lse_ref, m_s, l_s, acc_s):
        step = pl.program_id(1)
        i, j = i_ref[step], j_ref[step]

        @pl.when(j == 0)
        def _():
            m_s[...] = jnp.full_like(m_s, NEG_INF)
            l_s[...] = jnp.zeros_like(l_s)
            acc_s[...] = jnp.zeros_like(acc_s)

        def update(mask):
            for e in range(2):
                s = _raw_dot(q_ref[e], k_ref[e], NT)
                if mask is not None:
                    s = jnp.where(mask, s, NEG_INF)
                m_old = m_s[e]
                m_new = jnp.maximum(m_old, jnp.max(s, axis=-1, keepdims=True))
                alpha = jnp.exp2((m_old - m_new) * MLA_EXP2)
                p = jnp.exp2((s - m_new) * MLA_EXP2)
                l_s[e] = alpha * l_s[e] + jnp.sum(p, axis=-1, keepdims=True)
                acc_s[e] = alpha * acc_s[e] + _raw_dot(p, v_ref[e], NN)
                m_s[e] = m_new

        @pl.when(j < i)
        def _():
            update(None)

        @pl.when(j == i)
        def _():
            update(_diag_mask(tq))
            o_ref[...] = jnp.concatenate([acc_s[e] / l_s[e] for e in range(2)], axis=1)
            for e in range(2):
                lse_ref[e] = m_s[e] * MLA_EXP2 + jnp.log2(l_s[e])

    grid_spec = pltpu.PrefetchScalarGridSpec(
        num_scalar_prefetch=2, grid=(MLA_HEADS // 2, int(qi.shape[0])),
        in_specs=[pl.BlockSpec((2, tq, 128), lambda hp, s, qi, kj: (hp, qi[s], 0)),
                  pl.BlockSpec((2, tq, 128), lambda hp, s, qi, kj: (hp, kj[s], 0)),
                  pl.BlockSpec((2, tq, 64), lambda hp, s, qi, kj: (hp, kj[s], 0))],
        out_specs=[pl.BlockSpec((tq, 128), lambda hp, s, qi, kj: (qi[s], hp)),
                   pl.BlockSpec((2, tq, 1), lambda hp, s, qi, kj: (hp, qi[s], 0))],
        scratch_shapes=[pltpu.VMEM((2, tq, 1), F32), pltpu.VMEM((2, tq, 1), F32), pltpu.VMEM((2, tq, 64), F32)])
    return pl.pallas_call(
        body, name="mla_fwd", grid_spec=grid_spec,
        out_shape=[jax.ShapeDtypeStruct((t_len, 512), F32), jax.ShapeDtypeStruct((MLA_HEADS, t_len, 1), F32)],
        compiler_params=_params(("parallel", "arbitrary")))(qi, kj, q, k, v)


def _mla_bwd_prep(o, dy):
    def fn(o_t, dy_t):
        dos, deltas = [], []
        for h in range(MLA_HEADS):
            d_h = dy_t[:, 64 * h:64 * h + 64].astype(F32)
            dos.append(d_h)
            deltas.append(jnp.sum(d_h * o_t[:, 64 * h:64 * h + 64], axis=-1, keepdims=True))
        return dos, deltas
    t_len = o.shape[0]
    return _rowwise("mla_bwd_prep", fn, [o, dy], [], [((8, t_len, 64), BF16), ((8, t_len, 1), F32)])


def _mla_bwd(q, k, v, do, lse2, delta):
    t_len = q.shape[1]
    tq = min(512, t_len)
    nq = t_len // tq
    qi, kj = _tri_steps(nq, True)

    def body(i_ref, j_ref, q_ref, k_ref, v_ref, do_ref, lse_ref, dl_ref, dq_ref, dk_ref, dv_ref, k_acc, v_acc):
        step = pl.program_id(1)
        i, j = i_ref[step], j_ref[step]

        @pl.when(step == 0)
        def _():
            dq_ref[...] = jnp.zeros_like(dq_ref)

        @pl.when(i == j)
        def _():
            k_acc[...] = jnp.zeros_like(k_acc)
            v_acc[...] = jnp.zeros_like(v_acc)

        def tile(mask):
            p = jnp.exp2(_raw_dot(q_ref[...], k_ref[...], NT) * MLA_EXP2 - lse_ref[...])
            if mask is not None:
                p = jnp.where(mask, p, 0.0)
            ds = p * (_raw_dot(do_ref[...], v_ref[...], NT) - dl_ref[...]) * MLA_SCALE
            v_acc[...] += _raw_dot(p, do_ref[...], TN)
            k_acc[...] += _raw_dot(ds, q_ref[...], TN)
            rows = pl.ds(pl.multiple_of(i * tq, tq), tq)
            dq_ref[rows, :] += _raw_dot(ds, k_ref[...], NN)

        @pl.when(i == j)
        def _():
            tile(_diag_mask(tq))

        @pl.when(i > j)
        def _():
            tile(None)

        @pl.when(i == nq - 1)
        def _():
            dk_ref[...] = k_acc[...]
            dv_ref[...] = v_acc[...]

    q_blk = lambda w: pl.BlockSpec((None, tq, w), lambda h, s, qi, kj: (h, qi[s], 0))
    k_blk = lambda w: pl.BlockSpec((None, tq, w), lambda h, s, qi, kj: (h, kj[s], 0))
    grid_spec = pltpu.PrefetchScalarGridSpec(
        num_scalar_prefetch=2, grid=(MLA_HEADS, int(qi.shape[0])),
        in_specs=[q_blk(128), k_blk(128), k_blk(64), q_blk(64), q_blk(1), q_blk(1)],
        out_specs=[pl.BlockSpec((None, t_len, 128), lambda h, s, qi, kj: (h, 0, 0)), k_blk(128), k_blk(64)],
        scratch_shapes=[pltpu.VMEM((tq, 128), F32), pltpu.VMEM((tq, 64), F32)])
    return pl.pallas_call(
        body, name="mla_bwd", grid_spec=grid_spec,
        out_shape=[jax.ShapeDtypeStruct((MLA_HEADS, t_len, 128), F32), jax.ShapeDtypeStruct((MLA_HEADS, t_len, 128), F32),
                   jax.ShapeDtypeStruct((MLA_HEADS, t_len, 64), F32)],
        compiler_params=_params(("parallel", "arbitrary")))(qi, kj, q, k, v, do, lse2, delta)


MERGE_TN = 256
GATE_BLOCK0 = PROJ["gates"][0] // MERGE_TN


def _merge_specs(tm):
    y_spec = pl.BlockSpec((tm, 512), lambda i, j: (i, 0))
    w_spec = pl.BlockSpec((512, MERGE_TN), lambda i, j: (0, j))
    gate = lambda b: pl.BlockSpec((tm, MERGE_TN), lambda i, j, b=b: (i, GATE_BLOCK0 + 4 * b + j))
    return [y_spec] * 4 + [w_spec] * 4 + [gate(b) for b in range(4)]


def _merge_fwd(ys, wbs, proj):
    t_len = proj.shape[0]
    tm = min(512, t_len)

    def body(*refs):
        y, w, gl, o_ref = refs[0:4], refs[4:8], refs[8:12], refs[12]
        acc = jnp.zeros((tm, MERGE_TN), F32)
        for b in range(4):
            acc += jax.nn.sigmoid(gl[b][...]) * _raw_dot(y[b][...], w[b][...], NN)
        o_ref[...] = acc.astype(o_ref.dtype)

    return pl.pallas_call(body, name="merge_fwd", grid=(t_len // tm, D_MODEL // MERGE_TN), in_specs=_merge_specs(tm),
                          out_specs=pl.BlockSpec((tm, MERGE_TN), lambda i, j: (i, j)),
                          out_shape=jax.ShapeDtypeStruct((t_len, D_MODEL), BF16),
                          compiler_params=_params(("parallel", "parallel")))(*ys, *wbs, *([proj] * 4))


def _merge_bwd(ys, wbs, proj, dmerged):
    t_len = proj.shape[0]
    tm = min(512, t_len)

    def body(*refs):
        y, w, gl, dm_ref, dgl, dt = refs[0:4], refs[4:8], refs[8:12], refs[12], refs[13:17], refs[17:21]
        dm = dm_ref[...]
        for b in range(4):
            gate = jax.nn.sigmoid(gl[b][...])
            t_b = _raw_dot(y[b][...], w[b][...], NN)
            dgl[b][...] = (dm * t_b * gate * (1.0 - gate)).astype(BF16)
            dt[b][...] = (dm * gate).astype(BF16)

    o_spec = pl.BlockSpec((tm, MERGE_TN), lambda i, j: (i, j))
    return pl.pallas_call(body, name="merge_bwd", grid=(t_len // tm, D_MODEL // MERGE_TN), in_specs=_merge_specs(tm) + [o_spec],
                          out_specs=[o_spec] * 8, out_shape=[jax.ShapeDtypeStruct((t_len, D_MODEL), BF16)] * 8,
                          compiler_params=_params(("parallel", "parallel")))(*ys, *wbs, *([proj] * 4), dmerged)


def _adamw(name, w, g, m, v, slabs):
    shape = w.shape
    size = int(np.prod(shape))
    if shape[-1] >= 96:
        view = (size // shape[-1], shape[-1])
    elif size % 128 == 0:
        view = (size // 128, 128)
    else:
        view = (1, size)
    rows = view[0]
    tile = rows
    for cand in (512, 256, 128, 64):
        if rows > cand and rows % cand == 0 and cand * view[1] * 4 <= 2 ** 20:
            tile = cand
            break
    if rows * view[1] * 4 <= 2 ** 20:
        tile = rows
    c1, c2 = 1.0 - ADAM_B1 ** ADAM_STEP, 1.0 - ADAM_B2 ** ADAM_STEP

    def body(w_ref, g_ref, m_ref, v_ref, og_ref, od_ref, om_ref, ov_ref):
        if slabs:
            grad = g_ref[0].astype(F32)
            for d in range(1, slabs):
                grad = grad + g_ref[d].astype(F32)
        else:
            grad = g_ref[...]
        m_new = ADAM_B1 * m_ref[...] + (1.0 - ADAM_B1) * grad
        v_new = ADAM_B2 * v_ref[...] + (1.0 - ADAM_B2) * jnp.square(grad)
        og_ref[...] = grad
        od_ref[...] = -ADAM_LR * ((m_new / c1) / (jnp.sqrt(v_new / c2) + ADAM_EPS) + ADAM_WD * w_ref[...])
        om_ref[...] = m_new
        ov_ref[...] = v_new

    spec = pl.BlockSpec((tile, view[1]), lambda i: (i, 0))
    g_spec = pl.BlockSpec((slabs, tile, view[1]), lambda i: (0, i, 0)) if slabs else spec
    g_view = g.reshape((slabs,) + view) if slabs else g.reshape(view)
    outs = pl.pallas_call(body, name=name, grid=(rows // tile,), in_specs=[spec, g_spec, spec, spec], out_specs=[spec] * 4,
                          out_shape=[jax.ShapeDtypeStruct(view, F32)] * 4,
                          compiler_params=_params(("parallel",)))(w.reshape(view), g_view, m.reshape(view), v.reshape(view))
    return [o.reshape(shape) for o in outs]


def _sum_slabs(name, x):
    def body(x_ref, o_ref):
        acc = x_ref[0]
        for d in range(1, N_DEV):
            acc = acc + x_ref[d]
        o_ref[...] = acc
    return pl.pallas_call(body, name=name, out_shape=jax.ShapeDtypeStruct(x.shape[1:], x.dtype))(x)


MESH_ID = pl.DeviceIdType.MESH
ANY_SPEC = pl.BlockSpec(memory_space=pl.ANY)


def _remote(src, dst, send_sems, recv_sems, k, to):
    return pltpu.make_async_remote_copy(src_ref=src, dst_ref=dst, send_sem=send_sems.at[k], recv_sem=recv_sems.at[k],
                                        device_id=to, device_id_type=MESH_ID)


def _gather_two_level(name, blocks):
    n = len(blocks)

    def body(*refs):
        srcs, dsts = refs[:n], refs[n:2 * n]
        send_sems, recv_sems, local_sems = refs[2 * n:]
        x, y, c = lax.axis_index("x"), lax.axis_index("y"), lax.axis_index("c")
        sibling = (x, y, 1 - c)
        chips = [(1 - x, y), (x, 1 - y), (1 - x, 1 - y)]
        index = lambda px, py, pc: 4 * px + 2 * py + pc
        me = index(x, y, c)

        def copy(t, k, block, to, src=None):
            slot = dsts[t].at[block]
            return _remote(slot if src is None else src, slot, send_sems, recv_sems, 7 * t + k, to)

        own = [pltpu.make_async_copy(srcs[t], dsts[t].at[me], local_sems.at[t]) for t in range(n)]
        first = []
        for t in range(n):
            own[t].start()
            first.append(copy(t, 0, me, sibling, src=srcs[t]))
            first += [copy(t, 1 + j, me, (*chip, c), src=srcs[t]) for j, chip in enumerate(chips)]
        for cp in first:
            cp.start()
        passed = []
        for j, chip in enumerate(chips):
            for t in range(n):
                block = index(*chip, c)
                copy(t, 1 + j, block, sibling).wait_recv()
                passed.append(copy(t, 4 + j, block, sibling))
                passed[-1].start()
        for t in range(n):
            copy(t, 0, index(x, y, 1 - c), sibling).wait_recv()
            for j, chip in enumerate(chips):
                copy(t, 4 + j, index(*chip, 1 - c), sibling).wait_recv()
        for cp in first + passed:
            cp.wait_send()
        for cp in own:
            cp.wait()

    return pl.pallas_call(
        body, name=name, in_specs=[ANY_SPEC] * n, out_specs=[ANY_SPEC] * n,
        out_shape=[jax.ShapeDtypeStruct((N_DEV,) + b.shape, b.dtype) for b in blocks],
        scratch_shapes=[pltpu.SemaphoreType.DMA((7 * n,)), pltpu.SemaphoreType.DMA((7 * n,)), pltpu.SemaphoreType.DMA((n,))],
        compiler_params=pltpu.CompilerParams(has_side_effects=True))(*blocks)


def _pair_exchange(name, blocks):
    n = len(blocks)

    def body(*refs):
        srcs, dsts = refs[:n], refs[n:2 * n]
        send_sems, recv_sems = refs[2 * n:]
        sibling = (lax.axis_index("x"), lax.axis_index("y"), 1 - lax.axis_index("c"))
        copies = [_remote(srcs[t], dsts[t], send_sems, recv_sems, t, sibling) for t in range(n)]
        for cp in copies:
            cp.start()
        for cp in copies:
            cp.wait()

    return pl.pallas_call(
        body, name=name, in_specs=[ANY_SPEC] * n, out_specs=[ANY_SPEC] * n,
        out_shape=[jax.ShapeDtypeStruct(b.shape, b.dtype) for b in blocks],
        scratch_shapes=[pltpu.SemaphoreType.DMA((n,)), pltpu.SemaphoreType.DMA((n,))],
        compiler_params=pltpu.CompilerParams(has_side_effects=True))(*blocks)


def _chip_exchange(name, items):
    n = len(items)
    out_shape = [jax.ShapeDtypeStruct((4,) + tuple(a.shape[1:] if per_chip else a.shape), a.dtype) for a, per_chip in items]

    def body(*refs):
        srcs, dsts = refs[:n], refs[n:2 * n]
        send_sems, recv_sems, local_sems = refs[2 * n:]
        x, y, c = lax.axis_index("x"), lax.axis_index("y"), lax.axis_index("c")
        mine = 2 * x + y
        own, remote = [], []
        for t, (_, per_chip) in enumerate(items):
            own.append(pltpu.make_async_copy(srcs[t].at[mine] if per_chip else srcs[t], dsts[t].at[mine], local_sems.at[t]))
            own[-1].start()
        for k in range(1, 4):
            px, py = x ^ (k >> 1), y ^ (k & 1)
            peer = 2 * px + py
            for t, (_, per_chip) in enumerate(items):
                src = srcs[t].at[peer] if per_chip else srcs[t]
                out = _remote(src, dsts[t].at[mine], send_sems, recv_sems, 3 * t + k - 1, (px, py, c))
                out.start()
                remote.append((out, _remote(src, dsts[t].at[peer], send_sems, recv_sems, 3 * t + k - 1, (px, py, c))))
        for out, arrival in remote:
            out.wait_send()
            arrival.wait_recv()
        for cp in own:
            cp.wait()

    return pl.pallas_call(
        body, name=name, in_specs=[ANY_SPEC] * n, out_specs=[ANY_SPEC] * n, out_shape=out_shape,
        scratch_shapes=[pltpu.SemaphoreType.DMA((3 * n,)), pltpu.SemaphoreType.DMA((3 * n,)), pltpu.SemaphoreType.DMA((n,))],
        compiler_params=pltpu.CompilerParams(has_side_effects=True))(*[a for a, _ in items])


def _perm(a):
    t_len, w = a.shape
    return a.reshape(SCAN_SEG, t_len // SCAN_SEG, w).transpose(1, 0, 2).reshape(t_len, w)


def _unperm(a):
    t_len, w = a.shape
    return a.reshape(t_len // SCAN_SEG, SCAN_SEG, w).transpose(1, 0, 2).reshape(t_len, w)


def _band_diag(p):
    a, b = p.shape[1:]
    eye = jnp.eye(8, dtype=p.dtype)
    return (p.reshape(4, 8, a, 1, b) * eye[None, :, None, :, None]).reshape(4, 8 * a, 8 * b)


def _band_diag_extract(m, a, b):
    eye = jnp.eye(8, dtype=m.dtype)
    return (m.reshape(4, 8, a, 8, b) * eye[None, :, None, :, None]).sum(axis=3).reshape(32, a, b)


def _proj_cols(w_ref_layout):
    parts = []
    for name in PROJ_ORDER:
        off, width = REF_SPLIT[name]
        part = w_ref_layout[:, off:off + width]
        if width < PROJ[name][1]:
            part = jnp.pad(part, ((0, 0), (0, PROJ[name][1] - width)))
        parts.append(part)
    return jnp.concatenate(parts, axis=1)


def _unproj_cols(w_proj_layout):
    parts = []
    for name in REF_ORDER:
        parts.append(w_proj_layout[:, PROJ[name][0]:PROJ[name][0] + REF_SPLIT[name][1]])
    return jnp.concatenate(parts, axis=1)


def _uq_cols(w):
    w3 = w.reshape(w.shape[0], MLA_HEADS, 96)
    return jnp.concatenate([w3[:, :, :64].reshape(-1, 512), w3[:, :, 64:].reshape(-1, 256)], axis=1)


def _un_uq_cols(w):
    return jnp.concatenate([w[:, :512].reshape(-1, MLA_HEADS, 64), w[:, 512:].reshape(-1, MLA_HEADS, 32)], axis=2).reshape(-1, 768)


def _ukv_cols(w):
    w3 = w.reshape(w.shape[0], MLA_HEADS, 128)
    return jnp.concatenate([w3[:, :, :64].reshape(-1, 512), w3[:, :, 64:].reshape(-1, 512)], axis=1)


def _un_ukv_cols(w):
    return jnp.concatenate([w[:, :512].reshape(-1, MLA_HEADS, 64), w[:, 512:].reshape(-1, MLA_HEADS, 64)], axis=2).reshape(-1, 1024)


def _cols_gathered(g):
    return g.transpose(1, 0, 2).reshape(g.shape[1], -1)


def _cols_slabs(w):
    r, c = w.shape
    return w.reshape(r, N_DEV, c // N_DEV).transpose(1, 0, 2)


def _row(v):
    return v.reshape(1, -1)


def _s5_weights(p):
    flat = dict(lam_re=p["s5_lam_re"].reshape(S5_CH, 1), lam_im=p["s5_lam_im"].reshape(S5_CH, 1),
                log_dt=jnp.repeat(p["s5_log_dt"], S5_STATE).reshape(S5_CH, 1),
                b_re=p["s5_b_re"].reshape(S5_CH, S5_GROUP), b_im=p["s5_b_im"].reshape(S5_CH, S5_GROUP))
    ab_re, ab_im, bb_re, bb_im = _s5_param_fwd(flat["lam_re"], flat["lam_im"], flat["log_dt"], flat["b_re"], flat["b_im"])
    w_bu = jnp.concatenate([_band_diag(bb.reshape(32, 64, 16).transpose(0, 2, 1)) for bb in (bb_re, bb_im)], axis=0)
    w_y = jnp.concatenate([_band_diag(p["s5_c_re"].transpose(0, 2, 1)), -_band_diag(p["s5_c_im"].transpose(0, 2, 1))], axis=0)
    return dict(flat=flat,
                a_re=jnp.broadcast_to(ab_re.reshape(1, S5_CH), (SCAN_SEG, S5_CH)),
                a_im=jnp.broadcast_to(ab_im.reshape(1, S5_CH), (SCAN_SEG, S5_CH)),
                w_bu=w_bu.astype(BF16), w_bu_t=w_bu.swapaxes(1, 2).astype(BF16),
                w_y=w_y.astype(BF16), w_y_t=w_y.swapaxes(1, 2).astype(BF16))


def _layer_fwd(x, w, p, tabs):
    t_len = x.shape[0]
    cos64, sin64, cos32, sin32 = tabs
    s = {}
    (s["h1"],) = _rowwise("pre_mix", lambda xt, g: (_rms(xt, g),), [x], [_row(p["g_pre_mix"])], [((t_len, D_MODEL), BF16)])
    proj = s["proj"] = _mm("proj", s["h1"], w["w_in"], "nn")
    s["qa"], s["ka"], s["va"] = _swa_prep(proj, cos64, sin64)
    s["sinks"] = jnp.broadcast_to(p["swa_sinks"][:, None], (8, 128))
    y_a = _swa_fwd(s["qa"], s["ka"], s["va"], s["sinks"])
    s5 = s["s5"] = _s5_weights(p)
    su = proj[:, PROJ["su"][0]:PROJ["su"][0] + 512]
    s["u_p"] = _perm(su).astype(BF16)
    bu = _s5_expand("s5_bu", s["u_p"], s5["w_bu"])
    s["xs"], s["cin"] = _s5_scan("s5_scan_fwd", bu, s5["a_re"], s5["a_im"])
    s["y_s"] = _unperm(_s5_contract("s5_y", s["xs"], s5["w_y"]))
    (s["z"],) = _rowwise("s5_gelu", lambda y, u, d: (jax.nn.gelu(y + d * u),), [s["y_s"], _seg(proj, "su")],
                         [_row(p["s5_d"])], [((t_len, 512), BF16)])
    s["zz"] = _mm("s5_glu_mm", s["z"], w["w_glu"], "nn")
    (y_b,) = _rowwise("s5_glu", lambda t: (t[:, :512] * jax.nn.sigmoid(t[:, 512:]),), [s["zz"]], [], [((t_len, 512), BF16)])
    s["qc"], s["kc"] = _ret_prep(proj, cos64, sin64)
    y_c, s["prevs"] = _ret_fwd(s["qc"], s["kc"], proj)
    s["cqn"], s["ckvn"] = _rowwise("mla_norm", lambda cq, ckv, gq, gkv: (_rms(cq, gq), _rms(ckv, gkv)),
                                   [_seg(proj, "cq"), _seg(proj, "ckv")], [_row(p["mla_g_q"]), _row(p["mla_g_kv"])],
                                   [((t_len, 256), BF16), ((t_len, 128), BF16)])
    qf = _mm("mla_uq", s["cqn"], w["w_uq"], "nn")
    kvf = _mm("mla_ukv", s["ckvn"], w["w_ukv"], "nn")
    s["Q"], s["K"], s["V"] = _mla_prep(qf, kvf, proj, cos32, sin32)
    y_d, s["lse"] = _mla_fwd(s["Q"], s["K"], s["V"])
    s["ys"] = [y_a, y_b, y_c, y_d]
    s["merged"] = _merge_fwd(s["ys"], w["wb"], proj)
    s["o"] = _mm("out_mm", s["merged"], w["w_out"], "nn")

    def post_mix(xt, ot, g1, g2):
        x1 = xt + _rms(ot, g1)
        return x1, _rms(x1, g2)
    s["x1"], s["h2"] = _rowwise("post_mix", post_mix, [x, s["o"]], [_row(p["g_post_mix"]), _row(p["g_pre_mlp"])],
                                [((t_len, D_MODEL), F32), ((t_len, D_MODEL), BF16)])
    s["a"], s["r"] = _mm("ff1", s["h2"], w["w_ff1"], "nn", out_dtypes=(F32, BF16),
                         epi=lambda acc: (acc, jnp.square(jnp.maximum(acc, 0.0))))
    s["f"] = _mm("ff2", s["r"], w["w_ff2"], "nn")
    (x2,) = _rowwise("post_mlp", lambda x1, f, g: (x1 + _rms(f, g),), [s["x1"], s["f"]], [_row(p["g_post_mlp"])],
                     [((t_len, D_MODEL), F32)])
    return x2, s


def _layer_bwd(x, dx2, w, p, s, tabs):
    t_len = x.shape[0]
    cos64, sin64, cos32, sin32 = tabs
    proj = s["proj"]
    big, small = {}, {}

    def post_mlp_bwd(f, d, g):
        df, dg = _rms_bwd(f, g, d)
        return (df,), (dg,)
    df, small["g_post_mlp"] = _rowwise("post_mlp_bwd", post_mlp_bwd, [s["f"], dx2], [_row(p["g_post_mlp"])],
                                       [((t_len, D_MODEL), BF16)], [((1, D_MODEL), F32)])
    da = _mm("ff2_dx", df, w["w_ff2"], "nt", out_dtypes=(BF16,), epi=lambda acc, a: (acc * 2.0 * jnp.maximum(a, 0.0),),
             epi_in=(s["a"],))
    big["w_ff2"] = _mm("ff2_dw", s["r"], df, "tn")
    dh2 = _mm("ff1_dx", da, w["w_ff1"], "nt")
    big["w_ff1"] = _mm("ff1_dw", s["h2"], da, "tn")

    def post_mix_bwd(x1, o, dh, d2, g1, g2):
        dx1_n, dg2 = _rms_bwd(x1, g2, dh)
        dx1 = d2 + dx1_n
        do, dg1 = _rms_bwd(o, g1, dx1)
        return (dx1, do), (dg1, dg2)
    dx1, do, small["g_post_mix"], small["g_pre_mlp"] = _rowwise(
        "post_mix_bwd", post_mix_bwd, [s["x1"], s["o"], dh2, dx2], [_row(p["g_post_mix"]), _row(p["g_pre_mlp"])],
        [((t_len, D_MODEL), F32), ((t_len, D_MODEL), BF16)], [((1, D_MODEL), F32), ((1, D_MODEL), F32)])
    dmerged = _mm("out_dx", do, w["w_out"], "nt")
    big["w_out"] = _mm("out_dw", s["merged"], do, "tn")

    res = _merge_bwd(s["ys"], w["wb"], proj, dmerged)
    dgl, dts = res[:4], res[4:]
    dys = [_mm("branch_dx", dts[b], w["wb"][b], "nt") for b in range(4)]
    big["wb"] = [_mm("branch_dw", s["ys"][b], dts[b], "tn") for b in range(4)]
    seg = {}

    dqa, dka, dva, dsink = _swa_bwd(s["qa"], s["ka"], s["va"], s["sinks"], dys[0])
    seg["sq"], seg["sk"], seg["sv"] = _swa_unprep(dqa, dka, dva, cos64, sin64)
    small["swa_sinks"] = dsink[:, 0]

    def glu_bwd(t, d):
        za, sg = t[:, :512], jax.nn.sigmoid(t[:, 512:])
        return (jnp.concatenate([d * sg, d * za * sg * (1.0 - sg)], axis=1),)
    (dzz,) = _rowwise("s5_glu_bwd", glu_bwd, [s["zz"], dys[1]], [], [((t_len, 1024), BF16)])
    dz = _mm("s5_glu_dx", dzz, w["w_glu"], "nt")
    big["w_glu"] = _mm("s5_glu_dw", s["z"], dzz, "tn")

    def gelu_bwd(y, u, d, skip):
        _, vjp = jax.vjp(jax.nn.gelu, y + skip * u)
        (dy2,) = vjp(d)
        return (dy2, dy2 * skip), (jnp.sum(dy2 * u, axis=0, keepdims=True),)
    dy_s, du_skip, d_skip = _rowwise("s5_gelu_bwd", gelu_bwd, [s["y_s"], _seg(proj, "su"), dz], [_row(p["s5_d"])],
                                     [((t_len, 512), F32), ((t_len, 512), F32)], [((1, 512), F32)])
    small["s5_d"] = d_skip[0]
    s5 = s["s5"]
    dy_p = _perm(dy_s).astype(BF16)
    dlam_in = _s5_expand("s5_dx_in", dy_p, s5["w_y_t"])
    lam, da_re, da_im = _s5_scan("s5_scan_bwd", dlam_in, s5["a_re"], s5["a_im"], states=s["xs"], carry_in=s["cin"])
    du_s5 = _unperm(_s5_contract("s5_du", lam, s5["w_bu_t"]))
    d_w_bu = _s5_band_grad("s5_dbu", s["u_p"], lam, True)
    d_w_y = _s5_band_grad("s5_dc", dy_p, s["xs"], False)
    (seg["su"],) = _rowwise("s5_du_sum", lambda a, b: (a + b,), [du_s5, du_skip], [], [((t_len, 512), BF16)])
    d_bb_re = _band_diag_extract(d_w_bu[:4], 16, 64).transpose(0, 2, 1).reshape(S5_CH, S5_GROUP)
    d_bb_im = _band_diag_extract(d_w_bu[4:], 16, 64).transpose(0, 2, 1).reshape(S5_CH, S5_GROUP)
    small["s5_c_re"] = _band_diag_extract(d_w_y[:4], 64, 16).transpose(0, 2, 1)
    small["s5_c_im"] = -_band_diag_extract(d_w_y[4:], 64, 16).transpose(0, 2, 1)
    fl = s5["flat"]
    group_sum = jnp.repeat(jnp.eye(S5_GROUPS, dtype=F32), S5_STATE, axis=1)
    d_lr, d_li, d_ldt, d_br, d_bi = _s5_param_bwd(fl["lam_re"], fl["lam_im"], fl["log_dt"], fl["b_re"], fl["b_im"], group_sum,
                                                  da_re.reshape(S5_CH, 1), da_im.reshape(S5_CH, 1), d_bb_re, d_bb_im)
    small["s5_lam_re"], small["s5_lam_im"] = d_lr.reshape(32, 64), d_li.reshape(32, 64)
    small["s5_log_dt"] = d_ldt[:, 0]
    small["s5_b_re"], small["s5_b_im"] = d_br.reshape(32, 64, 16), d_bi.reshape(32, 64, 16)

    dqc, dkc, seg["rv"], seg["rg"] = _ret_bwd(s["qc"], s["kc"], proj, s["prevs"], dys[2])
    seg["rq"], seg["rk"] = _ret_unprep(dqc, dkc, cos64, sin64)

    do_h, delta = _mla_bwd_prep(s["ys"][3], dys[3])
    d_q, d_k, d_v = _mla_bwd(s["Q"], s["K"], s["V"], do_h, s["lse"], delta)
    dqf, dkvf, seg["kr"] = _mla_unprep(d_q, d_k, d_v, cos32, sin32)
    dcqn = _mm("mla_uq_dx", dqf, w["w_uq"], "nt")
    big["w_uq"] = _mm("mla_uq_dw", s["cqn"], dqf, "tn")
    dckvn = _mm("mla_ukv_dx", dkvf, w["w_ukv"], "nt")
    big["w_ukv"] = _mm("mla_ukv_dw", s["ckvn"], dkvf, "tn")

    def mla_norm_bwd(cq, ckv, d1, d2, gq, gkv):
        dcq, dgq = _rms_bwd(cq, gq, d1)
        dckv, dgkv = _rms_bwd(ckv, gkv, d2)
        return (dcq, dckv), (dgq, dgkv)
    seg["cq"], seg["ckv"], dgq, dgkv = _rowwise(
        "mla_norm_bwd", mla_norm_bwd, [_seg(proj, "cq"), _seg(proj, "ckv"), dcqn, dckvn],
        [_row(p["mla_g_q"]), _row(p["mla_g_kv"])], [((t_len, 256), BF16), ((t_len, 128), BF16)],
        [((1, 256), F32), ((1, 128), F32)])
    small["mla_g_q"], small["mla_g_kv"] = dgq[0], dgkv[0]

    dproj = jnp.concatenate([seg[name] for name in PROJ_ORDER[:-1]] + list(dgl), axis=1)
    dh1 = _mm("proj_dx", dproj, w["w_in"], "nt")
    big["w_in"] = _mm("proj_dw", s["h1"], dproj, "tn")

    def pre_mix_bwd(xt, dh, d1, g):
        dxn, dg = _rms_bwd(xt, g, dh)
        return (d1 + dxn,), (dg,)
    dx, small["g_pre_mix"] = _rowwise("pre_mix_bwd", pre_mix_bwd, [x, dh1, dx1], [_row(p["g_pre_mix"])],
                                      [((t_len, D_MODEL), F32)], [((1, D_MODEL), F32)])
    for name in ("g_pre_mix", "g_post_mix", "g_pre_mlp", "g_post_mlp"):
        small[name] = small[name][0]
    return dx, big, small


BIG = ("w_in", "s5_w_glu", "mla_w_uq", "mla_w_ukv", "w_branch", "w_out", "w_ff1", "w_ff2")
SMALL = ("g_pre_mix", "g_post_mix", "g_pre_mlp", "g_post_mlp", "swa_sinks", "s5_lam_re", "s5_lam_im", "s5_log_dt",
         "s5_b_re", "s5_b_im", "s5_c_re", "s5_c_im", "s5_d", "mla_g_q", "mla_g_kv")
WEIGHTS = ("g_pre_mix", "g_post_mix", "g_pre_mlp", "g_post_mlp", "w_in", "swa_sinks", "s5_lam_re", "s5_lam_im", "s5_log_dt",
           "s5_b_re", "s5_b_im", "s5_c_re", "s5_c_im", "s5_d", "s5_w_glu", "mla_g_q", "mla_g_kv", "mla_w_uq", "mla_w_ukv",
           "w_branch", "w_out", "w_ff1", "w_ff2")


def _layer_weights(gathered, l):
    g = {name: gathered[name][:, l] for name in BIG}
    wb = g["w_branch"].transpose(1, 2, 0, 3).reshape(4, 512, D_MODEL)
    return dict(w_in=_proj_cols(_cols_gathered(g["w_in"])), w_glu=_cols_gathered(g["s5_w_glu"]),
                w_uq=_uq_cols(_cols_gathered(g["mla_w_uq"])), w_ukv=_ukv_cols(_cols_gathered(g["mla_w_ukv"])),
                wb=[wb[b] for b in range(4)], w_out=g["w_out"].reshape(D_MODEL, D_MODEL),
                w_ff1=_cols_gathered(g["w_ff1"]), w_ff2=g["w_ff2"].reshape(4 * D_MODEL, D_MODEL))


def _grad_slabs(big):
    wb = jnp.stack(big["wb"])
    return dict(w_in=_cols_slabs(_unproj_cols(big["w_in"])), s5_w_glu=_cols_slabs(big["w_glu"]),
                mla_w_uq=_cols_slabs(_un_uq_cols(big["w_uq"])), mla_w_ukv=_cols_slabs(_un_ukv_cols(big["w_ukv"])),
                w_branch=wb.reshape(4, 512, N_DEV, D_MODEL // N_DEV).transpose(2, 0, 1, 3),
                w_out=big["w_out"].reshape(N_DEV, D_MODEL // N_DEV, D_MODEL),
                w_ff1=_cols_slabs(big["w_ff1"]), w_ff2=big["w_ff2"].reshape(N_DEV, 4 * D_MODEL // N_DEV, D_MODEL))


def kernel(x, g_pre_mix, g_post_mix, g_pre_mlp, g_post_mlp, w_in, swa_sinks, s5_lam_re, s5_lam_im, s5_log_dt, s5_b_re, s5_b_im, s5_c_re, s5_c_im, s5_d, s5_w_glu, mla_g_q, mla_g_kv, mla_w_uq, mla_w_ukv, w_branch, w_out, w_ff1, w_ff2, loss_target, m_g_pre_mix, m_g_post_mix, m_g_pre_mlp, m_g_post_mlp, m_w_in, m_swa_sinks, m_s5_lam_re, m_s5_lam_im, m_s5_log_dt, m_s5_b_re, m_s5_b_im, m_s5_c_re, m_s5_c_im, m_s5_d, m_s5_w_glu, m_mla_g_q, m_mla_g_kv, m_mla_w_uq, m_mla_w_ukv, m_w_branch, m_w_out, m_w_ff1, m_w_ff2, v_g_pre_mix, v_g_post_mix, v_g_pre_mlp, v_g_post_mlp, v_w_in, v_swa_sinks, v_s5_lam_re, v_s5_lam_im, v_s5_log_dt, v_s5_b_re, v_s5_b_im, v_s5_c_re, v_s5_c_im, v_s5_d, v_s5_w_glu, v_mla_g_q, v_mla_g_kv, v_mla_w_uq, v_mla_w_ukv, v_w_branch, v_w_out, v_w_ff1, v_w_ff2):
    weights = dict(g_pre_mix=g_pre_mix, g_post_mix=g_post_mix, g_pre_mlp=g_pre_mlp, g_post_mlp=g_post_mlp, w_in=w_in,
                   swa_sinks=swa_sinks, s5_lam_re=s5_lam_re, s5_lam_im=s5_lam_im, s5_log_dt=s5_log_dt, s5_b_re=s5_b_re,
                   s5_b_im=s5_b_im, s5_c_re=s5_c_re, s5_c_im=s5_c_im, s5_d=s5_d, s5_w_glu=s5_w_glu, mla_g_q=mla_g_q,
                   mla_g_kv=mla_g_kv, mla_w_uq=mla_w_uq, mla_w_ukv=mla_w_ukv, w_branch=w_branch, w_out=w_out, w_ff1=w_ff1,
                   w_ff2=w_ff2)
    m_in = dict(g_pre_mix=m_g_pre_mix, g_post_mix=m_g_post_mix, g_pre_mlp=m_g_pre_mlp, g_post_mlp=m_g_post_mlp, w_in=m_w_in,
                swa_sinks=m_swa_sinks, s5_lam_re=m_s5_lam_re, s5_lam_im=m_s5_lam_im, s5_log_dt=m_s5_log_dt,
                s5_b_re=m_s5_b_re, s5_b_im=m_s5_b_im, s5_c_re=m_s5_c_re, s5_c_im=m_s5_c_im, s5_d=m_s5_d,
                s5_w_glu=m_s5_w_glu, mla_g_q=m_mla_g_q, mla_g_kv=m_mla_g_kv, mla_w_uq=m_mla_w_uq, mla_w_ukv=m_mla_w_ukv,
                w_branch=m_w_branch, w_out=m_w_out, w_ff1=m_w_ff1, w_ff2=m_w_ff2)
    v_in = dict(g_pre_mix=v_g_pre_mix, g_post_mix=v_g_post_mix, g_pre_mlp=v_g_pre_mlp, g_post_mlp=v_g_post_mlp, w_in=v_w_in,
                swa_sinks=v_swa_sinks, s5_lam_re=v_s5_lam_re, s5_lam_im=v_s5_lam_im, s5_log_dt=v_s5_log_dt,
                s5_b_re=v_s5_b_re, s5_b_im=v_s5_b_im, s5_c_re=v_s5_c_re, s5_c_im=v_s5_c_im, s5_d=v_s5_d,
                s5_w_glu=v_s5_w_glu, mla_g_q=v_mla_g_q, mla_g_kv=v_mla_g_kv, mla_w_uq=v_mla_w_uq, mla_w_ukv=v_mla_w_ukv,
                w_branch=v_w_branch, w_out=v_w_out, w_ff1=v_w_ff1, w_ff2=v_w_ff2)
    depth = g_pre_mix.shape[0]
    t_len = x.shape[1]
    tabs = _rope_tables(t_len, 32) + _rope_tables(t_len, 16)

    recv = _gather_two_level("gather_weights", [weights[name].astype(BF16) for name in BIG])
    gathered = dict(zip(BIG, recv))

    xs, saved, layer_w, layer_p = [x[0]], [], [], []
    for l in range(depth):
        layer_w.append(_layer_weights(gathered, l))
        layer_p.append({name: weights[name][l] for name in SMALL})
        x_next, s = _layer_fwd(xs[-1], layer_w[l], layer_p[l], tabs)
        xs.append(x_next)
        saved.append(s)

    def loss_fn(y, tgt):
        err = y - tgt
        part = 0.5 * jnp.sum(jnp.mean(err * err, axis=-1, keepdims=True), axis=0, keepdims=True)
        return (err * (1.0 / D_MODEL),), (jnp.broadcast_to(part, (8, 128)),)
    dx, loss_part = _rowwise("loss", loss_fn, [xs[-1], loss_target[0]], [], [((t_len, D_MODEL), F32)], [((8, 128), F32)])
    loss = lax.psum(loss_part[0, 0], ("x", "y", "c"))

    big_grads, small_grads = [None] * depth, [None] * depth
    for l in reversed(range(depth)):
        dx, big, small = _layer_bwd(xs[l], dx, layer_w[l], layer_p[l], saved[l], tabs)
        big_grads[l], small_grads[l] = _grad_slabs(big), small

    small_vec = jnp.concatenate([jnp.stack([small_grads[l][name] for l in range(depth)]).reshape(-1) for name in SMALL])
    n_small = small_vec.shape[0]
    rows = -(-n_small // 1024) * 8
    small_mat = jnp.pad(small_vec, (0, rows * 128 - n_small)).reshape(rows, 128)
    core = lax.axis_index("c")
    own, send = [], []
    for name in BIG:
        slabs = jnp.stack([big_grads[l][name] for l in range(depth)], axis=1)
        slabs = slabs.reshape((4, 2) + slabs.shape[1:])
        own.append(lax.dynamic_index_in_dim(slabs, core, axis=1, keepdims=False))
        send.append(lax.dynamic_index_in_dim(slabs, 1 - core, axis=1, keepdims=False).astype(BF16))
    got = _pair_exchange("grads_to_sibling", send + [small_mat])
    chip_sums = []
    for name, mine, theirs in zip(BIG, own, got[:-1]):
        flat = (-1, mine.shape[-1])
        (part,) = _rowwise("grads_pair_sum", lambda a, b: (a + b.astype(F32),), [mine.reshape(flat), theirs.reshape(flat)], [],
                           [((int(np.prod(mine.shape[:-1])), mine.shape[-1]), BF16)])
        chip_sums.append(part.reshape(mine.shape))
    small_pair = jnp.where(core == 0, jnp.stack([small_mat, got[-1]]), jnp.stack([got[-1], small_mat]))
    recv = _chip_exchange("grads_to_chips", [(p, True) for p in chip_sums] + [(small_pair, False)])
    small_sum = _sum_slabs("sum_small_grads", recv[-1].reshape((N_DEV,) + small_mat.shape)).reshape(-1)

    grads, deltas, new_m, new_v = {}, {}, {}, {}
    for name, slabs in zip(BIG, recv[:-1]):
        grads[name], deltas[name], new_m[name], new_v[name] = _adamw("adamw_" + name, weights[name], slabs, m_in[name],
                                                                     v_in[name], 4)
    off = 0
    for name in SMALL:
        size = int(np.prod(weights[name].shape))
        g = small_sum[off:off + size].reshape(weights[name].shape)
        off += size
        grads[name], deltas[name], new_m[name], new_v[name] = _adamw("adamw_" + name, weights[name], g, m_in[name],
                                                                     v_in[name], False)
    return (loss, dx[None], *[grads[n] for n in WEIGHTS], *[deltas[n] for n in WEIGHTS], *[new_m[n] for n in WEIGHTS],
            *[new_v[n] for n in WEIGHTS])
```

```python
import functools
import math

import numpy as np
import jax
import jax.numpy as jnp
from jax import lax
from jax.experimental import pallas as pl
from jax.experimental.pallas import tpu as pltpu

F32, BF16 = jnp.float32, jnp.bfloat16
EPS = 1e-6
NEG_INF = -1e30
ROPE_THETA = 10000.0
D_MODEL = 1024
N_DEV = 8
BLOCK = 128
SWA_HEADS, SWA_KV = 8, 2
RET_HEADS, RET_DK, RET_DV = 4, 64, 128
MLA_HEADS = 8
S5_GROUPS, S5_STATE, S5_GROUP = 32, 64, 16
S5_CH = S5_GROUPS * S5_STATE
SCAN_LANES = 256
SCAN_SEG = 8
ADAM_LR, ADAM_B1, ADAM_B2, ADAM_EPS, ADAM_WD, ADAM_STEP = 0.001, 0.9, 0.999, 1e-08, 0.01, 10
VMEM_MB = 56

PROJ = dict(sq=(0, 512), su=(512, 512), rv=(1024, 512), rg=(1536, 512), rq=(2048, 256), rk=(2304, 256),
            cq=(2560, 256), sk=(2816, 128), sv=(2944, 128), ckv=(3072, 128), kr=(3200, 128), gates=(3328, 4096))
PROJ_W = 7424
REF_SPLIT = dict(sq=(0, 512), sk=(512, 128), sv=(640, 128), su=(768, 512), rq=(1280, 256), rk=(1536, 256),
                 rv=(1792, 512), rg=(2304, 512), cq=(2816, 256), ckv=(3072, 128), kr=(3200, 32), gates=(3232, 4096))
PROJ_ORDER = ("sq", "su", "rv", "rg", "rq", "rk", "cq", "sk", "sv", "ckv", "kr", "gates")
REF_ORDER = ("sq", "sk", "sv", "su", "rq", "rk", "rv", "rg", "cq", "ckv", "kr", "gates")

NN = (((1,), (0,)), ((), ()))
NT = (((1,), (1,)), ((), ()))
TN = (((0,), (0,)), ((), ()))


def _params(sem):
    return pltpu.CompilerParams(dimension_semantics=sem, vmem_limit_bytes=VMEM_MB * 2 ** 20)


def _raw_dot(a, b, dims):
    return lax.dot_general(a.astype(BF16), b.astype(BF16), dims, preferred_element_type=F32)


@jax.custom_vjp
def _dnn(a, b):
    return _raw_dot(a, b, NN)


@jax.custom_vjp
def _dnt(a, b):
    return _raw_dot(a, b, NT)


@jax.custom_vjp
def _dtn(a, b):
    return _raw_dot(a, b, TN)


_dnn.defvjp(lambda a, b: (_dnn(a, b), (a, b)),
            lambda r, g: (_dnt(g, r[1]).astype(r[0].dtype), _dtn(r[0], g).astype(r[1].dtype)))
_dnt.defvjp(lambda a, b: (_dnt(a, b), (a, b)),
            lambda r, g: (_dnn(g, r[1]).astype(r[0].dtype), _dtn(g, r[0]).astype(r[1].dtype)))
_dtn.defvjp(lambda a, b: (_dtn(a, b), (a, b)),
            lambda r, g: (_dnt(r[1], g).astype(r[0].dtype), _dnn(r[0], g).astype(r[1].dtype)))


class _Cols:
    def __init__(self, arr, width, index):
        self.arr, self.width, self.index = arr, width, index


def _seg(proj, name):
    off, width = PROJ[name]
    return _Cols(proj, width, off // width)


def _row_spec(shape, tile):
    lead = len(shape) - 2
    return pl.BlockSpec(tuple(shape[:-2]) + (tile, shape[-1]), lambda i, lead=lead: (0,) * lead + (i, 0))


def _const_spec(shape):
    nd = len(shape)
    return pl.BlockSpec(tuple(shape), lambda *_, nd=nd: (0,) * nd)


def _rowwise(name, fn, rows, consts, out_rows, out_accs=(), tile=256):
    arrs = [r.arr if isinstance(r, _Cols) else r for r in rows]
    t_len = arrs[0].shape[-2]
    tile = min(tile, t_len)
    n = t_len // tile
    in_specs = []
    for r in rows:
        if isinstance(r, _Cols):
            in_specs.append(pl.BlockSpec((tile, r.width), lambda i, c=r.index: (i, c)))
        else:
            in_specs.append(_row_spec(r.shape, tile))
    in_specs += [_const_spec(c.shape) for c in consts]
    out_specs = [_row_spec(s, tile) for s, _ in out_rows] + [_const_spec(s) for s, _ in out_accs]
    out_shape = [jax.ShapeDtypeStruct(s, d) for s, d in tuple(out_rows) + tuple(out_accs)]
    n_in, n_or = len(rows) + len(consts), len(out_rows)

    def body(*refs):
        res = fn(*[r[...] for r in refs[:n_in]])
        row_out, acc_out = res if out_accs else (res, ())
        for ref, val in zip(refs[n_in:n_in + n_or], row_out):
            if isinstance(val, (list, tuple)):
                for h, v_h in enumerate(val):
                    ref[h] = v_h.astype(ref.dtype)
            else:
                ref[...] = val.astype(ref.dtype)
        first = pl.program_id(0) == 0
        for ref, val in zip(refs[n_in + n_or:], acc_out):
            @pl.when(first)
            def _(ref=ref, val=val):
                ref[...] = val

            @pl.when(jnp.logical_not(first))
            def _(ref=ref, val=val):
                ref[...] += val

    return pl.pallas_call(body, name=name, grid=(n,), in_specs=in_specs, out_specs=out_specs, out_shape=out_shape,
                          compiler_params=_params(("arbitrary",)))(*arrs, *consts)


MM_VMEM_BUDGET = 44 * 2 ** 20


def _pick_tn(n, io_col_bytes, tmp_col_bytes, fixed_bytes):
    best = None
    for cand in range(128, n + 1, 128):
        if n % cand == 0 and 2 * (fixed_bytes + cand * io_col_bytes) + cand * tmp_col_bytes <= MM_VMEM_BUDGET:
            best = cand
    return best if best is not None else min(n, 128)


def _mm(name, a, b, mode, out_dtypes=(F32,), tm=512, tn=None, epi=None, epi_in=()):
    if mode == "nn":
        (m, k), n = a.shape, b.shape[1]
    elif mode == "nt":
        (m, k), n = a.shape, b.shape[0]
    else:
        (k, m), n = a.shape, b.shape[1]
    tm = min(tm, m)
    if tn is None:
        io_col = k * b.dtype.itemsize + tm * (sum(jnp.dtype(d).itemsize for d in out_dtypes) + sum(e.dtype.itemsize for e in epi_in))
        tmp_col = tm * 4 * (2 if epi is not None else 1)
        tn = _pick_tn(n, io_col, tmp_col, tm * k * a.dtype.itemsize)
    tn = min(tn, n)
    assert m % tm == 0 and n % tn == 0, (name, m, n, tm, tn)
    a_spec = pl.BlockSpec((k, tm), lambda i, j: (0, i)) if mode == "tn" else pl.BlockSpec((tm, k), lambda i, j: (i, 0))
    b_spec = pl.BlockSpec((tn, k), lambda i, j: (j, 0)) if mode == "nt" else pl.BlockSpec((k, tn), lambda i, j: (0, j))
    dims = dict(nn=NN, nt=NT, tn=TN)[mode]
    o_spec = pl.BlockSpec((tm, tn), lambda i, j: (i, j))
    n_e = len(epi_in)

    def body(a_ref, b_ref, *refs):
        acc = _raw_dot(a_ref[...], b_ref[...], dims)
        outs = epi(acc, *[r[...] for r in refs[:n_e]]) if epi is not None else (acc,)
        for ref, val in zip(refs[n_e:], outs):
            ref[...] = val.astype(ref.dtype)

    res = pl.pallas_call(body, name=name, grid=(m // tm, n // tn),
                         in_specs=[a_spec, b_spec] + [o_spec] * n_e, out_specs=[o_spec] * len(out_dtypes),
                         out_shape=[jax.ShapeDtypeStruct((m, n), d) for d in out_dtypes],
                         compiler_params=_params(("parallel", "parallel")))(a, b, *epi_in)
    return res[0] if len(out_dtypes) == 1 else res


def _rms(x, g):
    return x * lax.rsqrt(jnp.mean(x * x, axis=-1, keepdims=True) + EPS) * g


def _rms_bwd(x, g, dy):
    _, vjp = jax.vjp(_rms, x, g)
    return vjp(dy)


def _rope_tables(t_len, half):
    dim = 2 * half
    inv = 1.0 / (ROPE_THETA ** (jnp.arange(0, dim, 2, dtype=F32) / dim))
    ang = jnp.arange(t_len, dtype=F32)[:, None] * inv[None, :]
    c, s = jnp.cos(ang), jnp.sin(ang)
    reps = 128 // dim
    return jnp.tile(jnp.concatenate([c, c], 1), (1, reps)), jnp.tile(jnp.concatenate([-s, s], 1), (1, reps))


def _rope(x, cos, sin, half):
    w = x.shape[-1]
    reps = w // 128
    if reps > 1:
        cos, sin = jnp.tile(cos, (1, reps)), jnp.tile(sin, (1, reps))
    lane = lax.broadcasted_iota(jnp.int32, x.shape, 1)
    partner = jnp.where((lane % (2 * half)) < half, pltpu.roll(x, w - half, 1), pltpu.roll(x, half, 1))
    return x * cos + partner * sin


SWA_GROUP = SWA_HEADS // SWA_KV


def _swa_mask(i):
    rows = SWA_GROUP * BLOCK
    qi = lax.broadcasted_iota(jnp.int32, (rows, 2 * BLOCK), 0) % BLOCK + BLOCK
    kj = lax.broadcasted_iota(jnp.int32, (rows, 2 * BLOCK), 1)
    diff = qi - kj
    return (diff >= 0) & (diff < BLOCK) & ((i > 0) | (kj >= BLOCK))


def _swa_stack(ref, j, dtype=None):
    parts = [ref[:, 64 * (SWA_GROUP * j + g):64 * (SWA_GROUP * j + g) + 64] for g in range(SWA_GROUP)]
    out = jnp.concatenate(parts, axis=0)
    return out if dtype is None else out.astype(dtype)


def _swa_sinks(s_ref, j):
    return jnp.concatenate([jnp.broadcast_to(s_ref[SWA_GROUP * j + g:SWA_GROUP * j + g + 1, 0:1], (BLOCK, 1))
                            for g in range(SWA_GROUP)], axis=0)


def _swa_head(q, kw, vw, sink, mask):
    sc = _dnt(q, kw) * (64 ** -0.5)
    sc = jnp.where(mask, sc, NEG_INF)
    m = lax.stop_gradient(jnp.maximum(jnp.max(sc, axis=-1, keepdims=True), sink))
    p = jnp.exp(sc - m)
    denom = jnp.sum(p, axis=-1, keepdims=True) + jnp.exp(sink - m)
    return _dnn(p / denom, vw)


def _swa_prep(proj, cos, sin):
    def fn(sq, sk, sv, c, s):
        return _rope(sq, c, s, 32), _rope(sk, c, s, 32), sv
    t_len = proj.shape[0]
    return _rowwise("swa_prep", fn, [_seg(proj, "sq"), _seg(proj, "sk"), _seg(proj, "sv"), cos, sin], [],
                    [((t_len, 512), BF16), ((t_len, 128), BF16), ((t_len, 128), BF16)])


def _swa_fwd(q, k, v, sinks):
    t_len = q.shape[0]
    nb = t_len // BLOCK

    def body(q_ref, kp_ref, kc_ref, vp_ref, vc_ref, s_ref, o_ref):
        mask = _swa_mask(pl.program_id(0))
        for j in range(SWA_KV):
            cols = slice(64 * j, 64 * j + 64)
            kw = jnp.concatenate([kp_ref[:, cols], kc_ref[:, cols]], axis=0)
            vw = jnp.concatenate([vp_ref[:, cols], vc_ref[:, cols]], axis=0)
            o = _swa_head(_swa_stack(q_ref, j), kw, vw, _swa_sinks(s_ref, j), mask)
            for g in range(SWA_GROUP):
                h = SWA_GROUP * j + g
                o_ref[:, 64 * h:64 * h + 64] = o[BLOCK * g:BLOCK * g + BLOCK].astype(o_ref.dtype)

    prev = lambda i: (jnp.maximum(i - 1, 0), 0)
    cur = lambda i: (i, 0)
    return pl.pallas_call(
        body, name="swa_fwd", grid=(nb,),
        in_specs=[pl.BlockSpec((BLOCK, 512), cur), pl.BlockSpec((BLOCK, 128), prev), pl.BlockSpec((BLOCK, 128), cur),
                  pl.BlockSpec((BLOCK, 128), prev), pl.BlockSpec((BLOCK, 128), cur), _const_spec((8, 128))],
        out_specs=pl.BlockSpec((BLOCK, 512), cur), out_shape=jax.ShapeDtypeStruct((t_len, 512), BF16),
        compiler_params=_params(("parallel",)))(q, k, k, v, v, sinks)


def _swa_bwd(q, k, v, sinks, dy):
    t_len = q.shape[0]
    nb = t_len // BLOCK

    def body(q_ref, kp_ref, kc_ref, vp_ref, vc_ref, s_ref, dy_ref, dq_ref, dk_ref, dv_ref, ds_ref, k_acc, v_acc):
        i = pl.program_id(0)

        @pl.when(i == 0)
        def _():
            k_acc[...] = jnp.zeros_like(k_acc)
            v_acc[...] = jnp.zeros_like(v_acc)
            ds_ref[...] = jnp.zeros_like(ds_ref)

        @pl.when(i < nb)
        def _():
            mask = _swa_mask(i)
            for j in range(SWA_KV):
                cols = slice(64 * j, 64 * j + 64)
                kw = jnp.concatenate([kp_ref[:, cols], kc_ref[:, cols]], axis=0)
                vw = jnp.concatenate([vp_ref[:, cols], vc_ref[:, cols]], axis=0)
                _, vjp = jax.vjp(lambda a, b, c, d: _swa_head(a, b, c, d, mask), _swa_stack(q_ref, j), kw, vw,
                                 _swa_sinks(s_ref, j))
                dq, dkw, dvw, dsink = vjp(_swa_stack(dy_ref, j, F32))
                dkw, dvw = dkw.astype(F32), dvw.astype(F32)
                for g in range(SWA_GROUP):
                    h = SWA_GROUP * j + g
                    rows = slice(BLOCK * g, BLOCK * g + BLOCK)
                    dq_ref[:, 64 * h:64 * h + 64] = dq[rows].astype(F32)
                    ds_ref[h:h + 1, :] += jnp.broadcast_to(jnp.sum(dsink[rows], axis=0, keepdims=True), (1, 128))
                dk_ref[:, cols] = k_acc[:, cols] + dkw[:BLOCK]
                dv_ref[:, cols] = v_acc[:, cols] + dvw[:BLOCK]
                k_acc[:, cols] = dkw[BLOCK:]
                v_acc[:, cols] = dvw[BLOCK:]

        @pl.when(i == nb)
        def _():
            dk_ref[...] = k_acc[...]
            dv_ref[...] = v_acc[...]

    cur = lambda i: (jnp.minimum(i, nb - 1), 0)
    prev = lambda i: (jnp.maximum(jnp.minimum(i, nb - 1) - 1, 0), 0)
    done = lambda i: (jnp.maximum(i - 1, 0), 0)
    return pl.pallas_call(
        body, name="swa_bwd", grid=(nb + 1,),
        in_specs=[pl.BlockSpec((BLOCK, 512), cur), pl.BlockSpec((BLOCK, 128), prev), pl.BlockSpec((BLOCK, 128), cur),
                  pl.BlockSpec((BLOCK, 128), prev), pl.BlockSpec((BLOCK, 128), cur), _const_spec((8, 128)),
                  pl.BlockSpec((BLOCK, 512), cur)],
        out_specs=[pl.BlockSpec((BLOCK, 512), cur), pl.BlockSpec((BLOCK, 128), done), pl.BlockSpec((BLOCK, 128), done),
                   _const_spec((8, 128))],
        out_shape=[jax.ShapeDtypeStruct((t_len, 512), F32), jax.ShapeDtypeStruct((t_len, 128), F32),
                   jax.ShapeDtypeStruct((t_len, 128), F32), jax.ShapeDtypeStruct((8, 128), F32)],
        scratch_shapes=[pltpu.VMEM((BLOCK, 128), F32), pltpu.VMEM((BLOCK, 128), F32)],
        compiler_params=_params(("arbitrary",)))(q, k, k, v, v, sinks, dy)


def _swa_unprep(dq, dk, dv, cos, sin):
    def fn(dq_t, dk_t, dv_t, c, s):
        return _rope(dq_t, c, -s, 32), _rope(dk_t, c, -s, 32), dv_t
    t_len = dq.shape[0]
    return _rowwise("swa_unprep", fn, [dq, dk, dv, cos, sin], [],
                    [((t_len, 512), BF16), ((t_len, 128), BF16), ((t_len, 128), BF16)])


def _s5_discretize(lam_re, lam_im, log_dt, b_re, b_im):
    dt = jnp.exp(log_dt)
    mag = jnp.exp(lam_re * dt)
    ab_re, ab_im = mag * jnp.cos(lam_im * dt), mag * jnp.sin(lam_im * dt)
    den = lam_re * lam_re + lam_im * lam_im
    nr, ni = ab_re - 1.0, ab_im
    f_re = (nr * lam_re + ni * lam_im) / den
    f_im = (ni * lam_re - nr * lam_im) / den
    return ab_re, ab_im, f_re * b_re - f_im * b_im, f_re * b_im + f_im * b_re


def _s5_param_fwd(lam_re, lam_im, log_dt, b_re, b_im):
    def body(*refs):
        outs = _s5_discretize(*[r[...] for r in refs[:5]])
        for ref, val in zip(refs[5:], outs):
            ref[...] = val
    n = S5_CH
    return pl.pallas_call(body, name="s5_param_fwd",
                          out_shape=[jax.ShapeDtypeStruct((n, 1), F32)] * 2 + [jax.ShapeDtypeStruct((n, 16), F32)] * 2,
                          )(lam_re, lam_im, log_dt, b_re, b_im)


def _s5_param_bwd(lam_re, lam_im, log_dt, b_re, b_im, group_sum, d_ab_re, d_ab_im, d_bb_re, d_bb_im):
    def body(*refs):
        ins = [r[...] for r in refs[:10]]
        _, vjp = jax.vjp(_s5_discretize, *ins[:5])
        d_lr, d_li, d_ldt, d_br, d_bi = vjp(tuple(ins[6:10]))
        o_lr, o_li, o_ldt, o_br, o_bi = refs[10:]
        o_lr[...], o_li[...], o_br[...], o_bi[...] = d_lr, d_li, d_br, d_bi
        o_ldt[...] = lax.dot_general(ins[5], jnp.broadcast_to(d_ldt, (S5_CH, 128)), NN,
                                     precision=lax.Precision.HIGHEST, preferred_element_type=F32)
    n = S5_CH
    return pl.pallas_call(body, name="s5_param_bwd",
                          out_shape=[jax.ShapeDtypeStruct((n, 1), F32), jax.ShapeDtypeStruct((n, 1), F32),
                                     jax.ShapeDtypeStruct((S5_GROUPS, 128), F32),
                                     jax.ShapeDtypeStruct((n, 16), F32), jax.ShapeDtypeStruct((n, 16), F32)],
                          )(lam_re, lam_im, log_dt, b_re, b_im, group_sum, d_ab_re, d_ab_im, d_bb_re, d_bb_im)


def _s5_expand(name, u, w):
    t_len = u.shape[0]
    tile = min(512, t_len)

    def body(u_ref, w_ref, o_ref):
        o_ref[...] = _raw_dot(u_ref[...], w_ref[...], NN)

    return pl.pallas_call(
        body, name=name, grid=(t_len // tile, 8),
        in_specs=[pl.BlockSpec((tile, 128), lambda i, bb: (i, bb % 4)), pl.BlockSpec((None, 128, 512), lambda i, bb: (bb, 0, 0))],
        out_specs=pl.BlockSpec((None, tile, 512), lambda i, bb: (bb // 4, i, bb % 4)),
        out_shape=jax.ShapeDtypeStruct((2, t_len, S5_CH), F32), compiler_params=_params(("parallel", "parallel")))(u, w)


def _s5_contract(name, x, w):
    t_len = x.shape[1]
    tile = min(512, t_len)

    def body(xr_ref, xi_ref, wr_ref, wi_ref, o_ref):
        o_ref[...] = _raw_dot(xr_ref[...], wr_ref[...], NN) + _raw_dot(xi_ref[...], wi_ref[...], NN)

    return pl.pallas_call(
        body, name=name, grid=(t_len // tile, 4),
        in_specs=[pl.BlockSpec((None, tile, 512), lambda i, b: (0, i, b)), pl.BlockSpec((None, tile, 512), lambda i, b: (1, i, b)),
                  pl.BlockSpec((None, 512, 128), lambda i, b: (b, 0, 0)), pl.BlockSpec((None, 512, 128), lambda i, b: (4 + b, 0, 0))],
        out_specs=pl.BlockSpec((tile, 128), lambda i, b: (i, b)),
        out_shape=jax.ShapeDtypeStruct((t_len, 512), F32), compiler_params=_params(("parallel", "parallel")))(x, x, w, w)


def _s5_band_grad(name, narrow, wide, narrow_first):
    t_len = narrow.shape[0]

    def body(n_ref, w_ref, o_ref):
        if narrow_first:
            o_ref[...] = _raw_dot(n_ref[...], w_ref[...], TN)
        else:
            o_ref[...] = _raw_dot(w_ref[...], n_ref[...], TN)

    shape = (8, 128, 512) if narrow_first else (8, 512, 128)
    return pl.pallas_call(
        body, name=name, grid=(8,),
        in_specs=[pl.BlockSpec((t_len, 128), lambda bb: (0, bb % 4)), pl.BlockSpec((None, t_len, 512), lambda bb: (bb // 4, 0, bb % 4))],
        out_specs=pl.BlockSpec((None,) + shape[1:], lambda bb: (bb, 0, 0)),
        out_shape=jax.ShapeDtypeStruct(shape, F32), compiler_params=_params(("parallel",)))(narrow, wide)


def _s5_scan(name, bu, a_re, a_im, states=None, carry_in=None):
    t_len = bu.shape[1]
    n = t_len // SCAN_SEG
    log_n = int(round(math.log2(n)))
    assert 2 ** log_n == n
    reverse = states is not None
    lanes = SCAN_LANES // 2 if reverse else SCAN_LANES
    nblk = S5_CH // lanes

    def body(*refs):
        if reverse:
            b_ref, ar_ref, ai_ref, x_ref, cin_ref, o_ref, dar_ref, dai_ref, c_scr = refs
            xr_ref, xi_ref, cr_ref, ci_ref = x_ref.at[0], x_ref.at[1], cin_ref.at[0], cin_ref.at[1]
        else:
            b_ref, ar_ref, ai_ref, o_ref, co_ref, c_scr = refs
            cor_ref, coi_ref = co_ref.at[0], co_ref.at[1]
        br_ref, bi_ref, or_ref, oi_ref = b_ref.at[0], b_ref.at[1], o_ref.at[0], o_ref.at[1]
        ar = ar_ref[...]
        ai = -ai_ref[...] if reverse else ai_ref[...]

        def tile_at(i):
            return pl.multiple_of((n - 1 - i if reverse else i) * SCAN_SEG, SCAN_SEG)

        def advance(s, row):
            sr, si = s
            return (ar * sr - ai * si + br_ref[pl.ds(row, SCAN_SEG), :], ar * si + ai * sr + bi_ref[pl.ds(row, SCAN_SEG), :])

        zero = jnp.zeros((SCAN_SEG, lanes), F32)
        fin_r, fin_i = lax.fori_loop(0, n, lambda i, s: advance(s, tile_at(i)), (zero, zero), unroll=8)
        pr, pi = ar[0:1], ai[0:1]
        for _ in range(log_n):
            pr, pi = pr * pr - pi * pi, 2.0 * pr * pi
        er = ei = jnp.zeros((1, lanes), F32)
        for step in range(SCAN_SEG):
            j = SCAN_SEG - 1 - step if reverse else step
            c_scr[0, j:j + 1, :] = er
            c_scr[1, j:j + 1, :] = ei
            er, ei = pr * er - pi * ei + fin_r[j:j + 1], pr * ei + pi * er + fin_i[j:j + 1]
        entry = (c_scr[0], c_scr[1])

        if not reverse:
            cor_ref[...], coi_ref[...] = entry

            def step2(i, s):
                row = tile_at(i)
                s = advance(s, row)
                or_ref[pl.ds(row, SCAN_SEG), :] = s[0]
                oi_ref[pl.ds(row, SCAN_SEG), :] = s[1]
                return s
            lax.fori_loop(0, n, step2, entry, unroll=8)
        else:
            def emit(s, row, xr_prev, xi_prev, acc):
                s = advance(s, row)
                or_ref[pl.ds(row, SCAN_SEG), :] = s[0]
                oi_ref[pl.ds(row, SCAN_SEG), :] = s[1]
                return s, (acc[0] + s[0] * xr_prev + s[1] * xi_prev, acc[1] + s[1] * xr_prev - s[0] * xi_prev)

            def step2(i, carry):
                s, acc = carry
                row = tile_at(i)
                prev = pl.multiple_of(row - SCAN_SEG, SCAN_SEG)
                return emit(s, row, xr_ref[pl.ds(prev, SCAN_SEG), :], xi_ref[pl.ds(prev, SCAN_SEG), :], acc)
            s, acc = lax.fori_loop(0, n - 1, step2, (entry, (zero, zero)), unroll=8)
            s, acc = emit(s, 0, cr_ref[...], ci_ref[...], acc)
            dar_ref[...] = jnp.sum(acc[0], axis=0, keepdims=True)
            dai_ref[...] = jnp.sum(acc[1], axis=0, keepdims=True)

    planes = pl.BlockSpec((2, t_len, lanes), lambda b: (0, 0, b))
    tile8 = pl.BlockSpec((SCAN_SEG, lanes), lambda b: (0, b))
    entry_spec = pl.BlockSpec((2, SCAN_SEG, lanes), lambda b: (0, 0, b))
    row1 = pl.BlockSpec((1, lanes), lambda b: (0, b))
    seq = jax.ShapeDtypeStruct((2, t_len, S5_CH), F32)
    if reverse:
        in_specs = [planes, tile8, tile8, planes, entry_spec]
        args = (bu, a_re, a_im, states, carry_in)
        out_specs = [planes, row1, row1]
        out_shape = [seq, jax.ShapeDtypeStruct((1, S5_CH), F32), jax.ShapeDtypeStruct((1, S5_CH), F32)]
    else:
        in_specs = [planes, tile8, tile8]
        args = (bu, a_re, a_im)
        out_specs = [planes, entry_spec]
        out_shape = [seq, jax.ShapeDtypeStruct((2, SCAN_SEG, S5_CH), F32)]
    return pl.pallas_call(body, name=name, grid=(nblk,), in_specs=in_specs, out_specs=out_specs, out_shape=out_shape,
                          scratch_shapes=[pltpu.VMEM((2, SCAN_SEG, lanes), F32)],
                          compiler_params=_params(("parallel",)))(*args)


def _ret_consts():
    h = np.arange(RET_HEADS, dtype=np.float32)
    log_gamma = np.log1p(-np.exp2(-5.0 - h)).astype(np.float32)
    idx = np.arange(BLOCK, dtype=np.float32)
    diff = idx[:, None] - idx[None, :]
    decay = np.where(diff >= 0, np.exp(log_gamma[:, None, None] * np.maximum(diff, 0.0)), 0.0).astype(np.float32)
    k_w = np.exp(log_gamma[:, None] * (BLOCK - 1 - idx)[None, :]).astype(np.float32)[:, :, None]
    q_w = np.exp(log_gamma[:, None] * (idx + 1.0)[None, :]).astype(np.float32)[:, :, None]
    chunk_decay = [float(v) for v in np.exp(log_gamma * BLOCK).astype(np.float32)]
    return jnp.asarray(decay), jnp.asarray(q_w), jnp.asarray(k_w), chunk_decay


def _ret_head(q, k, v, g, prev, decay, q_w, k_w, chunk_decay):
    inner_s = _dnt(q, k) * decay
    y = _dnn(inner_s, v) + _dnn(q * q_w, prev)
    mu = jnp.mean(y, axis=-1, keepdims=True)
    var = jnp.mean(jnp.square(y - mu), axis=-1, keepdims=True)
    yn = (y - mu) * lax.rsqrt(var + EPS)
    out = g * jax.nn.sigmoid(g) * yn
    return out, prev * chunk_decay + _dtn(k * k_w, v)


def _ret_prep(proj, cos, sin):
    def fn(rq, rk, c, s):
        return _rope(rq, c, s, 32), _rope(rk, c, s, 32) * (RET_DK ** -0.5)
    t_len = proj.shape[0]
    return _rowwise("ret_prep", fn, [_seg(proj, "rq"), _seg(proj, "rk"), cos, sin], [],
                    [((t_len, 256), F32), ((t_len, 256), F32)])


def _ret_unprep(dq, dk, cos, sin):
    def fn(dq_t, dk_t, c, s):
        return _rope(dq_t, c, -s, 32), _rope(dk_t * (RET_DK ** -0.5), c, -s, 32)
    t_len = dq.shape[0]
    return _rowwise("ret_unprep", fn, [dq, dk, cos, sin], [], [((t_len, 256), BF16), ((t_len, 256), BF16)])


def _ret_fwd(q, k, proj):
    t_len = q.shape[0]
    nc = t_len // BLOCK
    decay, q_w, k_w, chunk_decay = _ret_consts()

    def body(q_ref, k_ref, v_ref, g_ref, dec_ref, qw_ref, kw_ref, o_ref, prev_ref, state):
        @pl.when(pl.program_id(0) == 0)
        def _():
            state[...] = jnp.zeros_like(state)
        for h in range(RET_HEADS):
            qk, vv = slice(64 * h, 64 * h + 64), slice(128 * h, 128 * h + 128)
            prev = state[h]
            prev_ref[h] = prev
            out, nxt = _ret_head(q_ref[:, qk], k_ref[:, qk], v_ref[:, vv], g_ref[:, vv], prev,
                                 dec_ref[h], qw_ref[h], kw_ref[h], chunk_decay[h])
            o_ref[:, vv] = out.astype(o_ref.dtype)
            state[h] = nxt

    chunk = lambda w, c=0: pl.BlockSpec((BLOCK, w), lambda i, c=c: (i, c))
    return pl.pallas_call(
        body, name="ret_fwd", grid=(nc,),
        in_specs=[chunk(256), chunk(256), chunk(512, PROJ["rv"][0] // 512), chunk(512, PROJ["rg"][0] // 512),
                  _const_spec(decay.shape), _const_spec(q_w.shape), _const_spec(k_w.shape)],
        out_specs=[chunk(512), pl.BlockSpec((None, RET_HEADS, RET_DK, RET_DV), lambda i: (i, 0, 0, 0))],
        out_shape=[jax.ShapeDtypeStruct((t_len, 512), BF16), jax.ShapeDtypeStruct((nc, RET_HEADS, RET_DK, RET_DV), F32)],
        scratch_shapes=[pltpu.VMEM((RET_HEADS, RET_DK, RET_DV), F32)],
        compiler_params=_params(("arbitrary",)))(q, k, proj, proj, decay, q_w, k_w)


def _ret_bwd(q, k, proj, prevs, dy):
    t_len = q.shape[0]
    nc = t_len // BLOCK
    decay, q_w, k_w, chunk_decay = _ret_consts()

    def body(q_ref, k_ref, v_ref, g_ref, prev_ref, dy_ref, dec_ref, qw_ref, kw_ref, dq_ref, dk_ref, dv_ref, dg_ref, dstate):
        @pl.when(pl.program_id(0) == 0)
        def _():
            dstate[...] = jnp.zeros_like(dstate)
        for h in range(RET_HEADS):
            qk, vv = slice(64 * h, 64 * h + 64), slice(128 * h, 128 * h + 128)
            head = functools.partial(_ret_head, decay=dec_ref[h], q_w=qw_ref[h], k_w=kw_ref[h], chunk_decay=chunk_decay[h])
            _, vjp = jax.vjp(head, q_ref[:, qk], k_ref[:, qk], v_ref[:, vv], g_ref[:, vv], prev_ref[h])
            dq, dk, dv, dg, dprev = vjp((dy_ref[:, vv].astype(F32), dstate[h]))
            dq_ref[:, qk], dk_ref[:, qk] = dq, dk
            dv_ref[:, vv], dg_ref[:, vv] = dv.astype(dv_ref.dtype), dg.astype(dg_ref.dtype)
            dstate[h] = dprev

    chunk = lambda w, c=0: pl.BlockSpec((BLOCK, w), lambda i, c=c: (nc - 1 - i, c))
    return pl.pallas_call(
        body, name="ret_bwd", grid=(nc,),
        in_specs=[chunk(256), chunk(256), chunk(512, PROJ["rv"][0] // 512), chunk(512, PROJ["rg"][0] // 512),
                  pl.BlockSpec((None, RET_HEADS, RET_DK, RET_DV), lambda i: (nc - 1 - i, 0, 0, 0)), chunk(512),
                  _const_spec(decay.shape), _const_spec(q_w.shape), _const_spec(k_w.shape)],
        out_specs=[chunk(256), chunk(256), chunk(512), chunk(512)],
        out_shape=[jax.ShapeDtypeStruct((t_len, 256), F32), jax.ShapeDtypeStruct((t_len, 256), F32),
                   jax.ShapeDtypeStruct((t_len, 512), BF16), jax.ShapeDtypeStruct((t_len, 512), BF16)],
        scratch_shapes=[pltpu.VMEM((RET_HEADS, RET_DK, RET_DV), F32)],
        compiler_params=_params(("arbitrary",)))(q, k, proj, proj, prevs, dy, decay, q_w, k_w)


MLA_SCALE = 96 ** -0.5


def _mla_prep(qf, kvf, proj, cos, sin):
    def fn(q, kv, kr, c, s):
        q_rope = _rope(q[:, 512:768], c, s, 16)
        k_rope = _rope(kr, c, s, 16)[:, :32]
        zero = jnp.zeros_like(k_rope)
        qs = [jnp.concatenate([q[:, 64 * h:64 * h + 64], q_rope[:, 32 * h:32 * h + 32], zero], axis=1) for h in range(MLA_HEADS)]
        ks = [jnp.concatenate([kv[:, 64 * h:64 * h + 64], k_rope, zero], axis=1) for h in range(MLA_HEADS)]
        ones = (lax.broadcasted_iota(jnp.int32, (kv.shape[0], 64), 1) == 0).astype(F32)
        vs = [jnp.concatenate([kv[:, 512 + 64 * h:576 + 64 * h], ones], axis=1) for h in range(MLA_HEADS)]
        return qs, ks, vs
    t_len = qf.shape[0]
    return _rowwise("mla_prep", fn, [qf, kvf, _seg(proj, "kr"), cos, sin], [],
                    [((8, t_len, 128), BF16), ((8, t_len, 128), BF16), ((8, t_len, 128), BF16)])


def _mla_unprep(dq, dk, dv, cos, sin):
    def fn(dq_t, dk_t, dv_t, c, s):
        q_rope = _rope(jnp.concatenate([dq_t[h][:, 64:96] for h in range(MLA_HEADS)], axis=1), c, -s, 16)
        d_qf = jnp.concatenate([dq_t[h][:, :64] for h in range(MLA_HEADS)] + [q_rope], axis=1)
        d_kvf = jnp.concatenate([dk_t[h][:, :64] for h in range(MLA_HEADS)] + [dv_t[h][:, :64] for h in range(MLA_HEADS)], axis=1)
        k_rope = dk_t[0][:, 64:96]
        for h in range(1, MLA_HEADS):
            k_rope = k_rope + dk_t[h][:, 64:96]
        d_kr = _rope(jnp.concatenate([k_rope, jnp.zeros((k_rope.shape[0], 96), F32)], axis=1), c, -s, 16)
        return d_qf, d_kvf, d_kr
    t_len = dq.shape[1]
    return _rowwise("mla_unprep", fn, [dq, dk, dv, cos, sin], [],
                    [((t_len, 768), BF16), ((t_len, 1024), BF16), ((t_len, 128), BF16)])


MLA_EXP2 = MLA_SCALE * math.log2(math.e)
MLA_TILE = 1024


def _diag_mask(tq):
    return lax.broadcasted_iota(jnp.int32, (tq, tq), 1) <= lax.broadcasted_iota(jnp.int32, (tq, tq), 0)


def _tri_steps(nq, key_major):
    if key_major:
        pairs = [(i, j) for j in range(nq) for i in range(j, nq)]
    else:
        pairs = [(i, j) for i in range(nq) for j in range(i + 1)]
    return jnp.asarray([p[0] for p in pairs], jnp.int32), jnp.asarray([p[1] for p in pairs], jnp.int32)


def _mla_fwd(q, k, v):
    t_len = q.shape[1]
    tq = min(MLA_TILE, t_len)
    nq = t_len // tq
    qi, kj = _tri_steps(nq, False)

    def body(i_ref, j_ref, q_ref, k_ref, v_ref, o_ref, lse_ref, m_s, acc_s):
        step = pl.program_id(1)
        i, j = i_ref[step], j_ref[step]

        @pl.when(j == 0)
        def _():
            m_s[...] = jnp.full_like(m_s, NEG_INF)
            acc_s[...] = jnp.zeros_like(acc_s)

        def update(mask):
            for e in range(2):
                s = _raw_dot(q_ref[e], k_ref[e], NT)
                if mask is not None:
                    s = jnp.where(mask, s, NEG_INF)
                m_old = m_s[e]
                m_new = jnp.maximum(m_old, jnp.max(s, axis=-1, keepdims=True))
                p = jnp.exp2((s - m_new) * MLA_EXP2)
                acc_s[e] = jnp.exp2((m_old - m_new) * MLA_EXP2) * acc_s[e] + _raw_dot(p, v_ref[e], NN)
                m_s[e] = m_new

        @pl.when(j < i)
        def _():
            update(None)

        @pl.when(j == i)
        def _():
            update(_diag_mask(tq))
            outs = []
            for e in range(2):
                acc = acc_s[e]
                denom = acc[:, 64:65]
                outs.append(acc[:, :64] / denom)
                lse_ref[e] = m_s[e] * MLA_EXP2 + jnp.log2(denom)
            o_ref[...] = jnp.concatenate(outs, axis=1)

    grid_spec = pltpu.PrefetchScalarGridSpec(
        num_scalar_prefetch=2, grid=(MLA_HEADS // 2, int(qi.shape[0])),
        in_specs=[pl.BlockSpec((2, tq, 128), lambda hp, s, qi, kj: (hp, qi[s], 0)),
                  pl.BlockSpec((2, tq, 128), lambda hp, s, qi, kj: (hp, kj[s], 0)),
                  pl.BlockSpec((2, tq, 128), lambda hp, s, qi, kj: (hp, kj[s], 0))],
        out_specs=[pl.BlockSpec((tq, 128), lambda hp, s, qi, kj: (qi[s], hp)),
                   pl.BlockSpec((2, tq, 1), lambda hp, s, qi, kj: (hp, qi[s], 0))],
        scratch_shapes=[pltpu.VMEM((2, tq, 1), F32), pltpu.VMEM((2, tq, 128), F32)])
    return pl.pallas_call(
        body, name="mla_fwd", grid_spec=grid_spec,
        out_shape=[jax.ShapeDtypeStruct((t_len, 512), F32), jax.ShapeDtypeStruct((MLA_HEADS, t_len, 1), F32)],
        compiler_params=_params(("parallel", "arbitrary")))(qi, kj, q, k, v)


def _mla_bwd_prep(o, dy):
    def fn(o_t, dy_t):
        dos, deltas = [], []
        for h in range(MLA_HEADS):
            d_h = dy_t[:, 64 * h:64 * h + 64].astype(F32)
            dos.append(jnp.concatenate([d_h, jnp.zeros_like(d_h)], axis=1))
            deltas.append(jnp.sum(d_h * o_t[:, 64 * h:64 * h + 64], axis=-1, keepdims=True))
        return dos, deltas
    t_len = o.shape[0]
    return _rowwise("mla_bwd_prep", fn, [o, dy], [], [((8, t_len, 128), BF16), ((8, t_len, 1), F32)])


def _mla_bwd(q, k, v, do, lse2, delta):
    t_len = q.shape[1]
    tq = min(MLA_TILE, t_len)
    nq = t_len // tq
    qi, kj = _tri_steps(nq, True)

    def body(i_ref, j_ref, q_ref, k_ref, v_ref, do_ref, lse_ref, dl_ref, dq_ref, dk_ref, dv_ref, k_acc, v_acc):
        step = pl.program_id(1)
        i, j = i_ref[step], j_ref[step]

        @pl.when(step == 0)
        def _():
            dq_ref[...] = jnp.zeros_like(dq_ref)

        @pl.when(i == j)
        def _():
            k_acc[...] = jnp.zeros_like(k_acc)
            v_acc[...] = jnp.zeros_like(v_acc)

        def tile(mask):
            p = jnp.exp2(_raw_dot(q_ref[...], k_ref[...], NT) * MLA_EXP2 - lse_ref[...])
            if mask is not None:
                p = jnp.where(mask, p, 0.0)
            ds = p * (_raw_dot(do_ref[...], v_ref[...], NT) - dl_ref[...]) * MLA_SCALE
            v_acc[...] += _raw_dot(p, do_ref[...], TN)
            k_acc[...] += _raw_dot(ds, q_ref[...], TN)
            rows = pl.ds(pl.multiple_of(i * tq, tq), tq)
            dq_ref[rows, :] += _raw_dot(ds, k_ref[...], NN)

        @pl.when(i == j)
        def _():
            tile(_diag_mask(tq))

        @pl.when(i > j)
        def _():
            tile(None)

        @pl.when(i == nq - 1)
        def _():
            dk_ref[...] = k_acc[...]
            dv_ref[...] = v_acc[...]

    q_blk = lambda w: pl.BlockSpec((None, tq, w), lambda h, s, qi, kj: (h, qi[s], 0))
    k_blk = lambda w: pl.BlockSpec((None, tq, w), lambda h, s, qi, kj: (h, kj[s], 0))
    grid_spec = pltpu.PrefetchScalarGridSpec(
        num_scalar_prefetch=2, grid=(MLA_HEADS, int(qi.shape[0])),
        in_specs=[q_blk(128), k_blk(128), k_blk(128), q_blk(128), q_blk(1), q_blk(1)],
        out_specs=[pl.BlockSpec((None, t_len, 128), lambda h, s, qi, kj: (h, 0, 0)), k_blk(128), k_blk(128)],
        scratch_shapes=[pltpu.VMEM((tq, 128), F32), pltpu.VMEM((tq, 128), F32)])
    return pl.pallas_call(
        body, name="mla_bwd", grid_spec=grid_spec,
        out_shape=[jax.ShapeDtypeStruct((MLA_HEADS, t_len, 128), F32), jax.ShapeDtypeStruct((MLA_HEADS, t_len, 128), F32),
                   jax.ShapeDtypeStruct((MLA_HEADS, t_len, 128), F32)],
        compiler_params=_params(("parallel", "arbitrary")))(qi, kj, q, k, v, do, lse2, delta)


MERGE_TN = 256
GATE_BLOCK0 = PROJ["gates"][0] // MERGE_TN


def _merge_specs(tm):
    y_spec = pl.BlockSpec((tm, 512), lambda i, j: (i, 0))
    w_spec = pl.BlockSpec((512, MERGE_TN), lambda i, j: (0, j))
    gate = lambda b: pl.BlockSpec((tm, MERGE_TN), lambda i, j, b=b: (i, GATE_BLOCK0 + 4 * b + j))
    return [y_spec] * 4 + [w_spec] * 4 + [gate(b) for b in range(4)]


def _merge_fwd(ys, wbs, proj):
    t_len = proj.shape[0]
    tm = min(512, t_len)

    def body(*refs):
        y, w, gl, o_ref = refs[0:4], refs[4:8], refs[8:12], refs[12]
        acc = jnp.zeros((tm, MERGE_TN), F32)
        for b in range(4):
            acc += jax.nn.sigmoid(gl[b][...]) * _raw_dot(y[b][...], w[b][...], NN)
        o_ref[...] = acc.astype(o_ref.dtype)

    return pl.pallas_call(body, name="merge_fwd", grid=(t_len // tm, D_MODEL // MERGE_TN), in_specs=_merge_specs(tm),
                          out_specs=pl.BlockSpec((tm, MERGE_TN), lambda i, j: (i, j)),
                          out_shape=jax.ShapeDtypeStruct((t_len, D_MODEL), BF16),
                          compiler_params=_params(("parallel", "parallel")))(*ys, *wbs, *([proj] * 4))


def _merge_bwd(ys, wbs, proj, dmerged):
    t_len = proj.shape[0]
    tm = min(512, t_len)

    def body(*refs):
        y, w, gl, dm_ref, dgl, dt = refs[0:4], refs[4:8], refs[8:12], refs[12], refs[13:17], refs[17:21]
        dm = dm_ref[...]
        for b in range(4):
            gate = jax.nn.sigmoid(gl[b][...])
            t_b = _raw_dot(y[b][...], w[b][...], NN)
            dgl[b][...] = (dm * t_b * gate * (1.0 - gate)).astype(BF16)
            dt[b][...] = (dm * gate).astype(BF16)

    o_spec = pl.BlockSpec((tm, MERGE_TN), lambda i, j: (i, j))
    return pl.pallas_call(body, name="merge_bwd", grid=(t_len // tm, D_MODEL // MERGE_TN), in_specs=_merge_specs(tm) + [o_spec],
                          out_specs=[o_spec] * 8, out_shape=[jax.ShapeDtypeStruct((t_len, D_MODEL), BF16)] * 8,
                          compiler_params=_params(("parallel", "parallel")))(*ys, *wbs, *([proj] * 4), dmerged)


def _adamw(name, w, g, m, v, slabs):
    shape = w.shape
    size = int(np.prod(shape))
    if shape[-1] >= 96:
        view = (size // shape[-1], shape[-1])
    elif size % 128 == 0:
        view = (size // 128, 128)
    else:
        view = (1, size)
    rows = view[0]
    tile = rows
    for cand in (512, 256, 128, 64):
        if rows > cand and rows % cand == 0 and cand * view[1] * 4 <= 2 ** 20:
            tile = cand
            break
    if rows * view[1] * 4 <= 2 ** 20:
        tile = rows
    c1, c2 = 1.0 - ADAM_B1 ** ADAM_STEP, 1.0 - ADAM_B2 ** ADAM_STEP

    def body(w_ref, g_ref, m_ref, v_ref, og_ref, od_ref, om_ref, ov_ref):
        if slabs:
            grad = g_ref[0].astype(F32)
            for d in range(1, slabs):
                grad = grad + g_ref[d].astype(F32)
        else:
            grad = g_ref[...]
        m_new = ADAM_B1 * m_ref[...] + (1.0 - ADAM_B1) * grad
        v_new = ADAM_B2 * v_ref[...] + (1.0 - ADAM_B2) * jnp.square(grad)
        og_ref[...] = grad
        od_ref[...] = -ADAM_LR * ((m_new / c1) / (jnp.sqrt(v_new / c2) + ADAM_EPS) + ADAM_WD * w_ref[...])
        om_ref[...] = m_new
        ov_ref[...] = v_new

    spec = pl.BlockSpec((tile, view[1]), lambda i: (i, 0))
    g_spec = pl.BlockSpec((slabs, tile, view[1]), lambda i: (0, i, 0)) if slabs else spec
    g_view = g.reshape((slabs,) + view) if slabs else g.reshape(view)
    outs = pl.pallas_call(body, name=name, grid=(rows // tile,), in_specs=[spec, g_spec, spec, spec], out_specs=[spec] * 4,
                          out_shape=[jax.ShapeDtypeStruct(view, F32)] * 4,
                          compiler_params=_params(("parallel",)))(w.reshape(view), g_view, m.reshape(view), v.reshape(view))
    return [o.reshape(shape) for o in outs]


def _sum_slabs(name, x):
    def body(x_ref, o_ref):
        acc = x_ref[0]
        for d in range(1, N_DEV):
            acc = acc + x_ref[d]
        o_ref[...] = acc
    return pl.pallas_call(body, name=name, out_shape=jax.ShapeDtypeStruct(x.shape[1:], x.dtype))(x)


MESH_ID = pl.DeviceIdType.MESH
ANY_SPEC = pl.BlockSpec(memory_space=pl.ANY)


def _remote(src, dst, send_sems, recv_sems, k, to):
    return pltpu.make_async_remote_copy(src_ref=src, dst_ref=dst, send_sem=send_sems.at[k], recv_sem=recv_sems.at[k],
                                        device_id=to, device_id_type=MESH_ID)


def _gather_two_level(name, blocks):
    n = len(blocks)

    def body(*refs):
        srcs, dsts = refs[:n], refs[n:2 * n]
        send_sems, recv_sems, local_sems = refs[2 * n:]
        x, y, c = lax.axis_index("x"), lax.axis_index("y"), lax.axis_index("c")
        sibling = (x, y, 1 - c)
        chips = [(1 - x, y), (x, 1 - y), (1 - x, 1 - y)]
        index = lambda px, py, pc: 4 * px + 2 * py + pc
        me = index(x, y, c)

        def copy(t, k, block, to, src=None):
            slot = dsts[t].at[block]
            return _remote(slot if src is None else src, slot, send_sems, recv_sems, 7 * t + k, to)

        own = [pltpu.make_async_copy(srcs[t], dsts[t].at[me], local_sems.at[t]) for t in range(n)]
        first = []
        for t in range(n):
            own[t].start()
            first.append(copy(t, 0, me, sibling, src=srcs[t]))
            first += [copy(t, 1 + j, me, (*chip, c), src=srcs[t]) for j, chip in enumerate(chips)]
        for cp in first:
            cp.start()
        passed = []
        for j, chip in enumerate(chips):
            for t in range(n):
                block = index(*chip, c)
                copy(t, 1 + j, block, sibling).wait_recv()
                passed.append(copy(t, 4 + j, block, sibling))
                passed[-1].start()
        for t in range(n):
            copy(t, 0, index(x, y, 1 - c), sibling).wait_recv()
            for j, chip in enumerate(chips):
                copy(t, 4 + j, index(*chip, 1 - c), sibling).wait_recv()
        for cp in first + passed:
            cp.wait_send()
        for cp in own:
            cp.wait()

    return pl.pallas_call(
        body, name=name, in_specs=[ANY_SPEC] * n, out_specs=[ANY_SPEC] * n,
        out_shape=[jax.ShapeDtypeStruct((N_DEV,) + b.shape, b.dtype) for b in blocks],
        scratch_shapes=[pltpu.SemaphoreType.DMA((7 * n,)), pltpu.SemaphoreType.DMA((7 * n,)), pltpu.SemaphoreType.DMA((n,))],
        compiler_params=pltpu.CompilerParams(has_side_effects=True))(*blocks)


def _pair_exchange(name, blocks):
    n = len(blocks)

    def body(*refs):
        srcs, dsts = refs[:n], refs[n:2 * n]
        send_sems, recv_sems = refs[2 * n:]
        sibling = (lax.axis_index("x"), lax.axis_index("y"), 1 - lax.axis_index("c"))
        copies = [_remote(srcs[t], dsts[t], send_sems, recv_sems, t, sibling) for t in range(n)]
        for cp in copies:
            cp.start()
        for cp in copies:
            cp.wait()

    return pl.pallas_call(
        body, name=name, in_specs=[ANY_SPEC] * n, out_specs=[ANY_SPEC] * n,
        out_shape=[jax.ShapeDtypeStruct(b.shape, b.dtype) for b in blocks],
        scratch_shapes=[pltpu.SemaphoreType.DMA((n,)), pltpu.SemaphoreType.DMA((n,))],
        compiler_params=pltpu.CompilerParams(has_side_effects=True))(*blocks)


def _chip_exchange(name, items):
    n = len(items)
    out_shape = [jax.ShapeDtypeStruct((4,) + tuple(a.shape[1:] if per_chip else a.shape), a.dtype) for a, per_chip in items]

    def body(*refs):
        srcs, dsts = refs[:n], refs[n:2 * n]
        send_sems, recv_sems, local_sems = refs[2 * n:]
        x, y, c = lax.axis_index("x"), lax.axis_index("y"), lax.axis_index("c")
        mine = 2 * x + y
        own, remote = [], []
        for t, (_, per_chip) in enumerate(items):
            own.append(pltpu.make_async_copy(srcs[t].at[mine] if per_chip else srcs[t], dsts[t].at[mine], local_sems.at[t]))
            own[-1].start()
        for k in range(1, 4):
            px, py = x ^ (k >> 1), y ^ (k & 1)
            peer = 2 * px + py
            for t, (_, per_chip) in enumerate(items):
                src = srcs[t].at[peer] if per_chip else srcs[t]
                out = _remote(src, dsts[t].at[mine], send_sems, recv_sems, 3 * t + k - 1, (px, py, c))
                out.start()
                remote.append((out, _remote(src, dsts[t].at[peer], send_sems, recv_sems, 3 * t + k - 1, (px, py, c))))
        for out, arrival in remote:
            out.wait_send()
            arrival.wait_recv()
        for cp in own:
            cp.wait()

    return pl.pallas_call(
        body, name=name, in_specs=[ANY_SPEC] * n, out_specs=[ANY_SPEC] * n, out_shape=out_shape,
        scratch_shapes=[pltpu.SemaphoreType.DMA((3 * n,)), pltpu.SemaphoreType.DMA((3 * n,)), pltpu.SemaphoreType.DMA((n,))],
        compiler_params=pltpu.CompilerParams(has_side_effects=True))(*[a for a, _ in items])


def _perm(a):
    t_len, w = a.shape
    return a.reshape(SCAN_SEG, t_len // SCAN_SEG, w).transpose(1, 0, 2).reshape(t_len, w)


def _unperm(a):
    t_len, w = a.shape
    return a.reshape(t_len // SCAN_SEG, SCAN_SEG, w).transpose(1, 0, 2).reshape(t_len, w)


def _band_diag(p):
    a, b = p.shape[1:]
    eye = jnp.eye(8, dtype=p.dtype)
    return (p.reshape(4, 8, a, 1, b) * eye[None, :, None, :, None]).reshape(4, 8 * a, 8 * b)


def _band_diag_extract(m, a, b):
    eye = jnp.eye(8, dtype=m.dtype)
    return (m.reshape(4, 8, a, 8, b) * eye[None, :, None, :, None]).sum(axis=3).reshape(32, a, b)


def _proj_cols(w_ref_layout):
    parts = []
    for name in PROJ_ORDER:
        off, width = REF_SPLIT[name]
        part = w_ref_layout[:, off:off + width]
        if width < PROJ[name][1]:
            part = jnp.pad(part, ((0, 0), (0, PROJ[name][1] - width)))
        parts.append(part)
    return jnp.concatenate(parts, axis=1)


def _unproj_cols(w_proj_layout):
    parts = []
    for name in REF_ORDER:
        parts.append(w_proj_layout[:, PROJ[name][0]:PROJ[name][0] + REF_SPLIT[name][1]])
    return jnp.concatenate(parts, axis=1)


def _uq_cols(w):
    w3 = w.reshape(w.shape[0], MLA_HEADS, 96)
    return jnp.concatenate([w3[:, :, :64].reshape(-1, 512), w3[:, :, 64:].reshape(-1, 256)], axis=1)


def _un_uq_cols(w):
    return jnp.concatenate([w[:, :512].reshape(-1, MLA_HEADS, 64), w[:, 512:].reshape(-1, MLA_HEADS, 32)], axis=2).reshape(-1, 768)


def _ukv_cols(w):
    w3 = w.reshape(w.shape[0], MLA_HEADS, 128)
    return jnp.concatenate([w3[:, :, :64].reshape(-1, 512), w3[:, :, 64:].reshape(-1, 512)], axis=1)


def _un_ukv_cols(w):
    return jnp.concatenate([w[:, :512].reshape(-1, MLA_HEADS, 64), w[:, 512:].reshape(-1, MLA_HEADS, 64)], axis=2).reshape(-1, 1024)


def _cols_gathered(g):
    return g.transpose(1, 0, 2).reshape(g.shape[1], -1)


def _cols_slabs(w):
    r, c = w.shape
    return w.reshape(r, N_DEV, c // N_DEV).transpose(1, 0, 2)


def _row(v):
    return v.reshape(1, -1)


def _s5_weights(p):
    flat = dict(lam_re=p["s5_lam_re"].reshape(S5_CH, 1), lam_im=p["s5_lam_im"].reshape(S5_CH, 1),
                log_dt=jnp.repeat(p["s5_log_dt"], S5_STATE).reshape(S5_CH, 1),
                b_re=p["s5_b_re"].reshape(S5_CH, S5_GROUP), b_im=p["s5_b_im"].reshape(S5_CH, S5_GROUP))
    ab_re, ab_im, bb_re, bb_im = _s5_param_fwd(flat["lam_re"], flat["lam_im"], flat["log_dt"], flat["b_re"], flat["b_im"])
    w_bu = jnp.concatenate([_band_diag(bb.reshape(32, 64, 16).transpose(0, 2, 1)) for bb in (bb_re, bb_im)], axis=0)
    w_y = jnp.concatenate([_band_diag(p["s5_c_re"].transpose(0, 2, 1)), -_band_diag(p["s5_c_im"].transpose(0, 2, 1))], axis=0)
    return dict(flat=flat,
                a_re=jnp.broadcast_to(ab_re.reshape(1, S5_CH), (SCAN_SEG, S5_CH)),
                a_im=jnp.broadcast_to(ab_im.reshape(1, S5_CH), (SCAN_SEG, S5_CH)),
                w_bu=w_bu.astype(BF16), w_bu_t=w_bu.swapaxes(1, 2).astype(BF16),
                w_y=w_y.astype(BF16), w_y_t=w_y.swapaxes(1, 2).astype(BF16))


def _layer_fwd(x, w, p, tabs):
    t_len = x.shape[0]
    cos64, sin64, cos32, sin32 = tabs
    s = {}
    (s["h1"],) = _rowwise("pre_mix", lambda xt, g: (_rms(xt, g),), [x], [_row(p["g_pre_mix"])], [((t_len, D_MODEL), BF16)])
    proj = s["proj"] = _mm("proj", s["h1"], w["w_in"], "nn")
    s["qa"], s["ka"], s["va"] = _swa_prep(proj, cos64, sin64)
    s["sinks"] = jnp.broadcast_to(p["swa_sinks"][:, None], (8, 128))
    y_a = _swa_fwd(s["qa"], s["ka"], s["va"], s["sinks"])
    s5 = s["s5"] = _s5_weights(p)
    su = proj[:, PROJ["su"][0]:PROJ["su"][0] + 512]
    s["u_p"] = _perm(su).astype(BF16)
    bu = _s5_expand("s5_bu", s["u_p"], s5["w_bu"])
    s["xs"], s["cin"] = _s5_scan("s5_scan_fwd", bu, s5["a_re"], s5["a_im"])
    s["y_s"] = _unperm(_s5_contract("s5_y", s["xs"], s5["w_y"]))
    (s["z"],) = _rowwise("s5_gelu", lambda y, u, d: (jax.nn.gelu(y + d * u),), [s["y_s"], _seg(proj, "su")],
                         [_row(p["s5_d"])], [((t_len, 512), BF16)])
    s["zz"] = _mm("s5_glu_mm", s["z"], w["w_glu"], "nn")
    (y_b,) = _rowwise("s5_glu", lambda t: (t[:, :512] * jax.nn.sigmoid(t[:, 512:]),), [s["zz"]], [], [((t_len, 512), BF16)])
    s["qc"], s["kc"] = _ret_prep(proj, cos64, sin64)
    y_c, s["prevs"] = _ret_fwd(s["qc"], s["kc"], proj)
    s["cqn"], s["ckvn"] = _rowwise("mla_norm", lambda cq, ckv, gq, gkv: (_rms(cq, gq), _rms(ckv, gkv)),
                                   [_seg(proj, "cq"), _seg(proj, "ckv")], [_row(p["mla_g_q"]), _row(p["mla_g_kv"])],
                                   [((t_len, 256), BF16), ((t_len, 128), BF16)])
    qf = _mm("mla_uq", s["cqn"], w["w_uq"], "nn")
    kvf = _mm("mla_ukv", s["ckvn"], w["w_ukv"], "nn")
    s["Q"], s["K"], s["V"] = _mla_prep(qf, kvf, proj, cos32, sin32)
    y_d, s["lse"] = _mla_fwd(s["Q"], s["K"], s["V"])
    s["ys"] = [y_a, y_b, y_c, y_d]
    s["merged"] = _merge_fwd(s["ys"], w["wb"], proj)
    s["o"] = _mm("out_mm", s["merged"], w["w_out"], "nn")

    def post_mix(xt, ot, g1, g2):
        x1 = xt + _rms(ot, g1)
        return x1, _rms(x1, g2)
    s["x1"], s["h2"] = _rowwise("post_mix", post_mix, [x, s["o"]], [_row(p["g_post_mix"]), _row(p["g_pre_mlp"])],
                                [((t_len, D_MODEL), F32), ((t_len, D_MODEL), BF16)])
    s["a"], s["r"] = _mm("ff1", s["h2"], w["w_ff1"], "nn", out_dtypes=(F32, BF16),
                         epi=lambda acc: (acc, jnp.square(jnp.maximum(acc, 0.0))))
    s["f"] = _mm("ff2", s["r"], w["w_ff2"], "nn")
    (x2,) = _rowwise("post_mlp", lambda x1, f, g: (x1 + _rms(f, g),), [s["x1"], s["f"]], [_row(p["g_post_mlp"])],
                     [((t_len, D_MODEL), F32)])
    return x2, s


def _layer_bwd(x, dx2, w, p, s, tabs):
    t_len = x.shape[0]
    cos64, sin64, cos32, sin32 = tabs
    proj = s["proj"]
    big, small = {}, {}

    def post_mlp_bwd(f, d, g):
        df, dg = _rms_bwd(f, g, d)
        return (df,), (dg,)
    df, small["g_post_mlp"] = _rowwise("post_mlp_bwd", post_mlp_bwd, [s["f"], dx2], [_row(p["g_post_mlp"])],
                                       [((t_len, D_MODEL), BF16)], [((1, D_MODEL), F32)])
    da = _mm("ff2_dx", df, w["w_ff2"], "nt", out_dtypes=(BF16,), epi=lambda acc, a: (acc * 2.0 * jnp.maximum(a, 0.0),),
             epi_in=(s["a"],))
    big["w_ff2"] = _mm("ff2_dw", s["r"], df, "tn")
    dh2 = _mm("ff1_dx", da, w["w_ff1"], "nt")
    big["w_ff1"] = _mm("ff1_dw", s["h2"], da, "tn")

    def post_mix_bwd(x1, o, dh, d2, g1, g2):
        dx1_n, dg2 = _rms_bwd(x1, g2, dh)
        dx1 = d2 + dx1_n
        do, dg1 = _rms_bwd(o, g1, dx1)
        return (dx1, do), (dg1, dg2)
    dx1, do, small["g_post_mix"], small["g_pre_mlp"] = _rowwise(
        "post_mix_bwd", post_mix_bwd, [s["x1"], s["o"], dh2, dx2], [_row(p["g_post_mix"]), _row(p["g_pre_mlp"])],
        [((t_len, D_MODEL), F32), ((t_len, D_MODEL), BF16)], [((1, D_MODEL), F32), ((1, D_MODEL), F32)])
    dmerged = _mm("out_dx", do, w["w_out"], "nt")
    big["w_out"] = _mm("out_dw", s["merged"], do, "tn")

    res = _merge_bwd(s["ys"], w["wb"], proj, dmerged)
    dgl, dts = res[:4], res[4:]
    dys = [_mm("branch_dx", dts[b], w["wb"][b], "nt") for b in range(4)]
    big["wb"] = [_mm("branch_dw", s["ys"][b], dts[b], "tn") for b in range(4)]
    seg = {}

    dqa, dka, dva, dsink = _swa_bwd(s["qa"], s["ka"], s["va"], s["sinks"], dys[0])
    seg["sq"], seg["sk"], seg["sv"] = _swa_unprep(dqa, dka, dva, cos64, sin64)
    small["swa_sinks"] = dsink[:, 0]

    def glu_bwd(t, d):
        za, sg = t[:, :512], jax.nn.sigmoid(t[:, 512:])
        return (jnp.concatenate([d * sg, d * za * sg * (1.0 - sg)], axis=1),)
    (dzz,) = _rowwise("s5_glu_bwd", glu_bwd, [s["zz"], dys[1]], [], [((t_len, 1024), BF16)])
    dz = _mm("s5_glu_dx", dzz, w["w_glu"], "nt")
    big["w_glu"] = _mm("s5_glu_dw", s["z"], dzz, "tn")

    def gelu_bwd(y, u, d, skip):
        _, vjp = jax.vjp(jax.nn.gelu, y + skip * u)
        (dy2,) = vjp(d)
        return (dy2, dy2 * skip), (jnp.sum(dy2 * u, axis=0, keepdims=True),)
    dy_s, du_skip, d_skip = _rowwise("s5_gelu_bwd", gelu_bwd, [s["y_s"], _seg(proj, "su"), dz], [_row(p["s5_d"])],
                                     [((t_len, 512), F32), ((t_len, 512), F32)], [((1, 512), F32)])
    small["s5_d"] = d_skip[0]
    s5 = s["s5"]
    dy_p = _perm(dy_s).astype(BF16)
    dlam_in = _s5_expand("s5_dx_in", dy_p, s5["w_y_t"])
    lam, da_re, da_im = _s5_scan("s5_scan_bwd", dlam_in, s5["a_re"], s5["a_im"], states=s["xs"], carry_in=s["cin"])
    du_s5 = _unperm(_s5_contract("s5_du", lam, s5["w_bu_t"]))
    d_w_bu = _s5_band_grad("s5_dbu", s["u_p"], lam, True)
    d_w_y = _s5_band_grad("s5_dc", dy_p, s["xs"], False)
    (seg["su"],) = _rowwise("s5_du_sum", lambda a, b: (a + b,), [du_s5, du_skip], [], [((t_len, 512), BF16)])
    d_bb_re = _band_diag_extract(d_w_bu[:4], 16, 64).transpose(0, 2, 1).reshape(S5_CH, S5_GROUP)
    d_bb_im = _band_diag_extract(d_w_bu[4:], 16, 64).transpose(0, 2, 1).reshape(S5_CH, S5_GROUP)
    small["s5_c_re"] = _band_diag_extract(d_w_y[:4], 64, 16).transpose(0, 2, 1)
    small["s5_c_im"] = -_band_diag_extract(d_w_y[4:], 64, 16).transpose(0, 2, 1)
    fl = s5["flat"]
    group_sum = jnp.repeat(jnp.eye(S5_GROUPS, dtype=F32), S5_STATE, axis=1)
    d_lr, d_li, d_ldt, d_br, d_bi = _s5_param_bwd(fl["lam_re"], fl["lam_im"], fl["log_dt"], fl["b_re"], fl["b_im"], group_sum,
                                                  da_re.reshape(S5_CH, 1), da_im.reshape(S5_CH, 1), d_bb_re, d_bb_im)
    small["s5_lam_re"], small["s5_lam_im"] = d_lr.reshape(32, 64), d_li.reshape(32, 64)
    small["s5_log_dt"] = d_ldt[:, 0]
    small["s5_b_re"], small["s5_b_im"] = d_br.reshape(32, 64, 16), d_bi.reshape(32, 64, 16)

    dqc, dkc, seg["rv"], seg["rg"] = _ret_bwd(s["qc"], s["kc"], proj, s["prevs"], dys[2])
    seg["rq"], seg["rk"] = _ret_unprep(dqc, dkc, cos64, sin64)

    do_h, delta = _mla_bwd_prep(s["ys"][3], dys[3])
    d_q, d_k, d_v = _mla_bwd(s["Q"], s["K"], s["V"], do_h, s["lse"], delta)
    dqf, dkvf, seg["kr"] = _mla_unprep(d_q, d_k, d_v, cos32, sin32)
    dcqn = _mm("mla_uq_dx", dqf, w["w_uq"], "nt")
    big["w_uq"] = _mm("mla_uq_dw", s["cqn"], dqf, "tn")
    dckvn = _mm("mla_ukv_dx", dkvf, w["w_ukv"], "nt")
    big["w_ukv"] = _mm("mla_ukv_dw", s["ckvn"], dkvf, "tn")

    def mla_norm_bwd(cq, ckv, d1, d2, gq, gkv):
        dcq, dgq = _rms_bwd(cq, gq, d1)
        dckv, dgkv = _rms_bwd(ckv, gkv, d2)
        return (dcq, dckv), (dgq, dgkv)
    seg["cq"], seg["ckv"], dgq, dgkv = _rowwise(
        "mla_norm_bwd", mla_norm_bwd, [_seg(proj, "cq"), _seg(proj, "ckv"), dcqn, dckvn],
        [_row(p["mla_g_q"]), _row(p["mla_g_kv"])], [((t_len, 256), BF16), ((t_len, 128), BF16)],
        [((1, 256), F32), ((1, 128), F32)])
    small["mla_g_q"], small["mla_g_kv"] = dgq[0], dgkv[0]

    dproj = jnp.concatenate([seg[name] for name in PROJ_ORDER[:-1]] + list(dgl), axis=1)
    dh1 = _mm("proj_dx", dproj, w["w_in"], "nt")
    big["w_in"] = _mm("proj_dw", s["h1"], dproj, "tn")

    def pre_mix_bwd(xt, dh, d1, g):
        dxn, dg = _rms_bwd(xt, g, dh)
        return (d1 + dxn,), (dg,)
    dx, small["g_pre_mix"] = _rowwise("pre_mix_bwd", pre_mix_bwd, [x, dh1, dx1], [_row(p["g_pre_mix"])],
                                      [((t_len, D_MODEL), F32)], [((1, D_MODEL), F32)])
    for name in ("g_pre_mix", "g_post_mix", "g_pre_mlp", "g_post_mlp"):
        small[name] = small[name][0]
    return dx, big, small


BIG = ("w_in", "s5_w_glu", "mla_w_uq", "mla_w_ukv", "w_branch", "w_out", "w_ff1", "w_ff2")
SMALL = ("g_pre_mix", "g_post_mix", "g_pre_mlp", "g_post_mlp", "swa_sinks", "s5_lam_re", "s5_lam_im", "s5_log_dt",
         "s5_b_re", "s5_b_im", "s5_c_re", "s5_c_im", "s5_d", "mla_g_q", "mla_g_kv")
WEIGHTS = ("g_pre_mix", "g_post_mix", "g_pre_mlp", "g_post_mlp", "w_in", "swa_sinks", "s5_lam_re", "s5_lam_im", "s5_log_dt",
           "s5_b_re", "s5_b_im", "s5_c_re", "s5_c_im", "s5_d", "s5_w_glu", "mla_g_q", "mla_g_kv", "mla_w_uq", "mla_w_ukv",
           "w_branch", "w_out", "w_ff1", "w_ff2")


def _layer_weights(gathered, l):
    g = {name: gathered[name][:, l] for name in BIG}
    wb = g["w_branch"].transpose(1, 2, 0, 3).reshape(4, 512, D_MODEL)
    return dict(w_in=_proj_cols(_cols_gathered(g["w_in"])), w_glu=_cols_gathered(g["s5_w_glu"]),
                w_uq=_uq_cols(_cols_gathered(g["mla_w_uq"])), w_ukv=_ukv_cols(_cols_gathered(g["mla_w_ukv"])),
                wb=[wb[b] for b in range(4)], w_out=g["w_out"].reshape(D_MODEL, D_MODEL),
                w_ff1=_cols_gathered(g["w_ff1"]), w_ff2=g["w_ff2"].reshape(4 * D_MODEL, D_MODEL))


def _grad_slabs(big):
    wb = jnp.stack(big["wb"])
    return dict(w_in=_cols_slabs(_unproj_cols(big["w_in"])), s5_w_glu=_cols_slabs(big["w_glu"]),
                mla_w_uq=_cols_slabs(_un_uq_cols(big["w_uq"])), mla_w_ukv=_cols_slabs(_un_ukv_cols(big["w_ukv"])),
                w_branch=wb.reshape(4, 512, N_DEV, D_MODEL // N_DEV).transpose(2, 0, 1, 3),
                w_out=big["w_out"].reshape(N_DEV, D_MODEL // N_DEV, D_MODEL),
                w_ff1=_cols_slabs(big["w_ff1"]), w_ff2=big["w_ff2"].reshape(N_DEV, 4 * D_MODEL // N_DEV, D_MODEL))


def kernel(x, g_pre_mix, g_post_mix, g_pre_mlp, g_post_mlp, w_in, swa_sinks, s5_lam_re, s5_lam_im, s5_log_dt, s5_b_re, s5_b_im, s5_c_re, s5_c_im, s5_d, s5_w_glu, mla_g_q, mla_g_kv, mla_w_uq, mla_w_ukv, w_branch, w_out, w_ff1, w_ff2, loss_target, m_g_pre_mix, m_g_post_mix, m_g_pre_mlp, m_g_post_mlp, m_w_in, m_swa_sinks, m_s5_lam_re, m_s5_lam_im, m_s5_log_dt, m_s5_b_re, m_s5_b_im, m_s5_c_re, m_s5_c_im, m_s5_d, m_s5_w_glu, m_mla_g_q, m_mla_g_kv, m_mla_w_uq, m_mla_w_ukv, m_w_branch, m_w_out, m_w_ff1, m_w_ff2, v_g_pre_mix, v_g_post_mix, v_g_pre_mlp, v_g_post_mlp, v_w_in, v_swa_sinks, v_s5_lam_re, v_s5_lam_im, v_s5_log_dt, v_s5_b_re, v_s5_b_im, v_s5_c_re, v_s5_c_im, v_s5_d, v_s5_w_glu, v_mla_g_q, v_mla_g_kv, v_mla_w_uq, v_mla_w_ukv, v_w_branch, v_w_out, v_w_ff1, v_w_ff2):
    weights = dict(g_pre_mix=g_pre_mix, g_post_mix=g_post_mix, g_pre_mlp=g_pre_mlp, g_post_mlp=g_post_mlp, w_in=w_in,
                   swa_sinks=swa_sinks, s5_lam_re=s5_lam_re, s5_lam_im=s5_lam_im, s5_log_dt=s5_log_dt, s5_b_re=s5_b_re,
                   s5_b_im=s5_b_im, s5_c_re=s5_c_re, s5_c_im=s5_c_im, s5_d=s5_d, s5_w_glu=s5_w_glu, mla_g_q=mla_g_q,
                   mla_g_kv=mla_g_kv, mla_w_uq=mla_w_uq, mla_w_ukv=mla_w_ukv, w_branch=w_branch, w_out=w_out, w_ff1=w_ff1,
                   w_ff2=w_ff2)
    m_in = dict(g_pre_mix=m_g_pre_mix, g_post_mix=m_g_post_mix, g_pre_mlp=m_g_pre_mlp, g_post_mlp=m_g_post_mlp, w_in=m_w_in,
                swa_sinks=m_swa_sinks, s5_lam_re=m_s5_lam_re, s5_lam_im=m_s5_lam_im, s5_log_dt=m_s5_log_dt,
                s5_b_re=m_s5_b_re, s5_b_im=m_s5_b_im, s5_c_re=m_s5_c_re, s5_c_im=m_s5_c_im, s5_d=m_s5_d,
                s5_w_glu=m_s5_w_glu, mla_g_q=m_mla_g_q, mla_g_kv=m_mla_g_kv, mla_w_uq=m_mla_w_uq, mla_w_ukv=m_mla_w_ukv,
                w_branch=m_w_branch, w_out=m_w_out, w_ff1=m_w_ff1, w_ff2=m_w_ff2)
    v_in = dict(g_pre_mix=v_g_pre_mix, g_post_mix=v_g_post_mix, g_pre_mlp=v_g_pre_mlp, g_post_mlp=v_g_post_mlp, w_in=v_w_in,
                swa_sinks=v_swa_sinks, s5_lam_re=v_s5_lam_re, s5_lam_im=v_s5_lam_im, s5_log_dt=v_s5_log_dt,
                s5_b_re=v_s5_b_re, s5_b_im=v_s5_b_im, s5_c_re=v_s5_c_re, s5_c_im=v_s5_c_im, s5_d=v_s5_d,
                s5_w_glu=v_s5_w_glu, mla_g_q=v_mla_g_q, mla_g_kv=v_mla_g_kv, mla_w_uq=v_mla_w_uq, mla_w_ukv=v_mla_w_ukv,
                w_branch=v_w_branch, w_out=v_w_out, w_ff1=v_w_ff1, w_ff2=v_w_ff2)
    depth = g_pre_mix.shape[0]
    t_len = x.shape[1]
    tabs = _rope_tables(t_len, 32) + _rope_tables(t_len, 16)

    recv = _gather_two_level("gather_weights", [weights[name].astype(BF16) for name in BIG])
    gathered = dict(zip(BIG, recv))

    xs, saved, layer_w, layer_p = [x[0]], [], [], []
    for l in range(depth):
        layer_w.append(_layer_weights(gathered, l))
        layer_p.append({name: weights[name][l] for name in SMALL})
        x_next, s = _layer_fwd(xs[-1], layer_w[l], layer_p[l], tabs)
        xs.append(x_next)
        saved.append(s)

    def loss_fn(y, tgt):
        err = y - tgt
        part = 0.5 * jnp.sum(jnp.mean(err * err, axis=-1, keepdims=True), axis=0, keepdims=True)
        return (err * (1.0 / D_MODEL),), (jnp.broadcast_to(part, (8, 128)),)
    dx, loss_part = _rowwise("loss", loss_fn, [xs[-1], loss_target[0]], [], [((t_len, D_MODEL), F32)], [((8, 128), F32)])
    loss = lax.psum(loss_part[0, 0], ("x", "y", "c"))

    big_grads, small_grads = [None] * depth, [None] * depth
    for l in reversed(range(depth)):
        dx, big, small = _layer_bwd(xs[l], dx, layer_w[l], layer_p[l], saved[l], tabs)
        big_grads[l], small_grads[l] = _grad_slabs(big), small

    small_vec = jnp.concatenate([jnp.stack([small_grads[l][name] for l in range(depth)]).reshape(-1) for name in SMALL])
    n_small = small_vec.shape[0]
    rows = -(-n_small // 1024) * 8
    small_mat = jnp.pad(small_vec, (0, rows * 128 - n_small)).reshape(rows, 128)
    core = lax.axis_index("c")
    own, send = [], []
    for name in BIG:
        slabs = jnp.stack([big_grads[l][name] for l in range(depth)], axis=1)
        slabs = slabs.reshape((4, 2) + slabs.shape[1:])
        own.append(lax.dynamic_index_in_dim(slabs, core, axis=1, keepdims=False))
        send.append(lax.dynamic_index_in_dim(slabs, 1 - core, axis=1, keepdims=False).astype(BF16))
    got = _pair_exchange("grads_to_sibling", send + [small_mat])
    chip_sums = []
    for name, mine, theirs in zip(BIG, own, got[:-1]):
        flat = (-1, mine.shape[-1])
        (part,) = _rowwise("grads_pair_sum", lambda a, b: (a + b.astype(F32),), [mine.reshape(flat), theirs.reshape(flat)], [],
                           [((int(np.prod(mine.shape[:-1])), mine.shape[-1]), BF16)])
        chip_sums.append(part.reshape(mine.shape))
    small_pair = jnp.where(core == 0, jnp.stack([small_mat, got[-1]]), jnp.stack([got[-1], small_mat]))
    recv = _chip_exchange("grads_to_chips", [(p, True) for p in chip_sums] + [(small_pair, False)])
    small_sum = _sum_slabs("sum_small_grads", recv[-1].reshape((N_DEV,) + small_mat.shape)).reshape(-1)

    grads, deltas, new_m, new_v = {}, {}, {}, {}
    for name, slabs in zip(BIG, recv[:-1]):
        grads[name], deltas[name], new_m[name], new_v[name] = _adamw("adamw_" + name, weights[name], slabs, m_in[name],
                                                                     v_in[name], 4)
    off = 0
    for name in SMALL:
        size = int(np.prod(weights[name].shape))
        g = small_sum[off:off + size].reshape(weights[name].shape)
        off += size
        grads[name], deltas[name], new_m[name], new_v[name] = _adamw("adamw_" + name, weights[name], g, m_in[name],
                                                                     v_in[name], False)
    return (loss, dx[None], *[grads[n] for n in WEIGHTS], *[deltas[n] for n in WEIGHTS], *[new_m[n] for n in WEIGHTS],
            *[new_v[n] for n in WEIGHTS])
```

```python
import functools
import math

import numpy as np
import jax
import jax.numpy as jnp
from jax import lax
from jax.experimental import pallas as pl
from jax.experimental.pallas import tpu as pltpu

F32, BF16 = jnp.float32, jnp.bfloat16
EPS = 1e-6
NEG_INF = -1e30
ROPE_THETA = 10000.0
D_MODEL = 1024
N_DEV = 8
BLOCK = 128
SWA_HEADS, SWA_KV = 8, 2
RET_HEADS, RET_DK, RET_DV = 4, 64, 128
MLA_HEADS = 8
S5_GROUPS, S5_STATE, S5_GROUP = 32, 64, 16
S5_CH = S5_GROUPS * S5_STATE
SCAN_LANES = 256
SCAN_SEG = 8
ADAM_LR, ADAM_B1, ADAM_B2, ADAM_EPS, ADAM_WD, ADAM_STEP = 0.001, 0.9, 0.999, 1e-08, 0.01, 10
VMEM_MB = 56

PROJ = dict(sq=(0, 512), su=(512, 512), rv=(1024, 512), rg=(1536, 512), rq=(2048, 256), rk=(2304, 256),
            cq=(2560, 256), sk=(2816, 128), sv=(2944, 128), ckv=(3072, 128), kr=(3200, 128), gates=(3328, 4096))
PROJ_W = 7424
REF_SPLIT = dict(sq=(0, 512), sk=(512, 128), sv=(640, 128), su=(768, 512), rq=(1280, 256), rk=(1536, 256),
                 rv=(1792, 512), rg=(2304, 512), cq=(2816, 256), ckv=(3072, 128), kr=(3200, 32), gates=(3232, 4096))
PROJ_ORDER = ("sq", "su", "rv", "rg", "rq", "rk", "cq", "sk", "sv", "ckv", "kr", "gates")
REF_ORDER = ("sq", "sk", "sv", "su", "rq", "rk", "rv", "rg", "cq", "ckv", "kr", "gates")

NN = (((1,), (0,)), ((), ()))
NT = (((1,), (1,)), ((), ()))
TN = (((0,), (0,)), ((), ()))


def _params(sem):
    return pltpu.CompilerParams(dimension_semantics=sem, vmem_limit_bytes=VMEM_MB * 2 ** 20)


def _raw_dot(a, b, dims):
    return lax.dot_general(a.astype(BF16), b.astype(BF16), dims, preferred_element_type=F32)


@jax.custom_vjp
def _dnn(a, b):
    return _raw_dot(a, b, NN)


@jax.custom_vjp
def _dnt(a, b):
    return _raw_dot(a, b, NT)


@jax.custom_vjp
def _dtn(a, b):
    return _raw_dot(a, b, TN)


_dnn.defvjp(lambda a, b: (_dnn(a, b), (a, b)),
            lambda r, g: (_dnt(g, r[1]).astype(r[0].dtype), _dtn(r[0], g).astype(r[1].dtype)))
_dnt.defvjp(lambda a, b: (_dnt(a, b), (a, b)),
            lambda r, g: (_dnn(g, r[1]).astype(r[0].dtype), _dtn(g, r[0]).astype(r[1].dtype)))
_dtn.defvjp(lambda a, b: (_dtn(a, b), (a, b)),
            lambda r, g: (_dnt(r[1], g).astype(r[0].dtype), _dnn(r[0], g).astype(r[1].dtype)))


class _Cols:
    def __init__(self, arr, width, index):
        self.arr, self.width, self.index = arr, width, index


def _seg(proj, name):
    off, width = PROJ[name]
    return _Cols(proj, width, off // width)


def _row_spec(shape, tile):
    lead = len(shape) - 2
    return pl.BlockSpec(tuple(shape[:-2]) + (tile, shape[-1]), lambda i, lead=lead: (0,) * lead + (i, 0))


def _const_spec(shape):
    nd = len(shape)
    return pl.BlockSpec(tuple(shape), lambda *_, nd=nd: (0,) * nd)


def _rowwise(name, fn, rows, consts, out_rows, out_accs=(), tile=256):
    arrs = [r.arr if isinstance(r, _Cols) else r for r in rows]
    t_len = arrs[0].shape[-2]
    tile = min(tile, t_len)
    n = t_len // tile
    in_specs = []
    for r in rows:
        if isinstance(r, _Cols):
            in_specs.append(pl.BlockSpec((tile, r.width), lambda i, c=r.index: (i, c)))
        else:
            in_specs.append(_row_spec(r.shape, tile))
    in_specs += [_const_spec(c.shape) for c in consts]
    out_specs = [_row_spec(s, tile) for s, _ in out_rows] + [_const_spec(s) for s, _ in out_accs]
    out_shape = [jax.ShapeDtypeStruct(s, d) for s, d in tuple(out_rows) + tuple(out_accs)]
    n_in, n_or = len(rows) + len(consts), len(out_rows)

    def body(*refs):
        res = fn(*[r[...] for r in refs[:n_in]])
        row_out, acc_out = res if out_accs else (res, ())
        for ref, val in zip(refs[n_in:n_in + n_or], row_out):
            if isinstance(val, (list, tuple)):
                for h, v_h in enumerate(val):
                    ref[h] = v_h.astype(ref.dtype)
            else:
                ref[...] = val.astype(ref.dtype)
        first = pl.program_id(0) == 0
        for ref, val in zip(refs[n_in + n_or:], acc_out):
            @pl.when(first)
            def _(ref=ref, val=val):
                ref[...] = val

            @pl.when(jnp.logical_not(first))
            def _(ref=ref, val=val):
                ref[...] += val

    return pl.pallas_call(body, name=name, grid=(n,), in_specs=in_specs, out_specs=out_specs, out_shape=out_shape,
                          compiler_params=_params(("arbitrary",)))(*arrs, *consts)


MM_VMEM_BUDGET = 44 * 2 ** 20


def _pick_tn(n, io_col_bytes, tmp_col_bytes, fixed_bytes):
    best = None
    for cand in range(128, n + 1, 128):
        if n % cand == 0 and 2 * (fixed_bytes + cand * io_col_bytes) + cand * tmp_col_bytes <= MM_VMEM_BUDGET:
            best = cand
    return best if best is not None else min(n, 128)


def _mm(name, a, b, mode, out_dtypes=(F32,), tm=512, tn=None, epi=None, epi_in=()):
    if mode == "nn":
        (m, k), n = a.shape, b.shape[1]
    elif mode == "nt":
        (m, k), n = a.shape, b.shape[0]
    else:
        (k, m), n = a.shape, b.shape[1]
    tm = min(tm, m)
    if tn is None:
        io_col = k * b.dtype.itemsize + tm * (sum(jnp.dtype(d).itemsize for d in out_dtypes) + sum(e.dtype.itemsize for e in epi_in))
        tmp_col = tm * 4 * (2 if epi is not None else 1)
        tn = _pick_tn(n, io_col, tmp_col, tm * k * a.dtype.itemsize)
    tn = min(tn, n)
    assert m % tm == 0 and n % tn == 0, (name, m, n, tm, tn)
    a_spec = pl.BlockSpec((k, tm), lambda i, j: (0, i)) if mode == "tn" else pl.BlockSpec((tm, k), lambda i, j: (i, 0))
    b_spec = pl.BlockSpec((tn, k), lambda i, j: (j, 0)) if mode == "nt" else pl.BlockSpec((k, tn), lambda i, j: (0, j))
    dims = dict(nn=NN, nt=NT, tn=TN)[mode]
    o_spec = pl.BlockSpec((tm, tn), lambda i, j: (i, j))
    n_e = len(epi_in)

    def body(a_ref, b_ref, *refs):
        acc = _raw_dot(a_ref[...], b_ref[...], dims)
        outs = epi(acc, *[r[...] for r in refs[:n_e]]) if epi is not None else (acc,)
        for ref, val in zip(refs[n_e:], outs):
            ref[...] = val.astype(ref.dtype)

    res = pl.pallas_call(body, name=name, grid=(m // tm, n // tn),
                         in_specs=[a_spec, b_spec] + [o_spec] * n_e, out_specs=[o_spec] * len(out_dtypes),
                         out_shape=[jax.ShapeDtypeStruct((m, n), d) for d in out_dtypes],
                         compiler_params=_params(("parallel", "parallel")))(a, b, *epi_in)
    return res[0] if len(out_dtypes) == 1 else res


def _rms(x, g):
    return x * lax.rsqrt(jnp.mean(x * x, axis=-1, keepdims=True) + EPS) * g


def _rms_bwd(x, g, dy):
    _, vjp = jax.vjp(_rms, x, g)
    return vjp(dy)


def _rope_tables(t_len, half):
    dim = 2 * half
    inv = 1.0 / (ROPE_THETA ** (jnp.arange(0, dim, 2, dtype=F32) / dim))
    ang = jnp.arange(t_len, dtype=F32)[:, None] * inv[None, :]
    c, s = jnp.cos(ang), jnp.sin(ang)
    reps = 128 // dim
    return jnp.tile(jnp.concatenate([c, c], 1), (1, reps)), jnp.tile(jnp.concatenate([-s, s], 1), (1, reps))


def _rope(x, cos, sin, half):
    w = x.shape[-1]
    reps = w // 128
    if reps > 1:
        cos, sin = jnp.tile(cos, (1, reps)), jnp.tile(sin, (1, reps))
    lane = lax.broadcasted_iota(jnp.int32, x.shape, 1)
    partner = jnp.where((lane % (2 * half)) < half, pltpu.roll(x, w - half, 1), pltpu.roll(x, half, 1))
    return x * cos + partner * sin


SWA_GROUP = SWA_HEADS // SWA_KV
SWA_QB = 4
SWA_TB_FWD, SWA_TB_BWD = 4, 2


def _swa_mask(tb, first_block):
    rows, keys = SWA_GROUP * tb * BLOCK, (tb + 1) * BLOCK
    qi = lax.broadcasted_iota(jnp.int32, (rows, keys), 0) % (tb * BLOCK) + BLOCK
    kj = lax.broadcasted_iota(jnp.int32, (rows, keys), 1)
    diff = qi - kj
    band = (diff >= 0) & (diff < BLOCK)
    return band if first_block is False else band & (jnp.logical_not(first_block) | (kj >= BLOCK))


def _swa_stack(ref, j, b, tb, dtype=None):
    rows = slice(BLOCK * b, BLOCK * (b + tb))
    parts = [ref[rows, 64 * (SWA_GROUP * j + g):64 * (SWA_GROUP * j + g) + 64] for g in range(SWA_GROUP)]
    out = jnp.concatenate(parts, axis=0)
    return out if dtype is None else out.astype(dtype)


def _swa_sinks(s_ref, j, tb):
    return jnp.concatenate([jnp.broadcast_to(s_ref[SWA_GROUP * j + g:SWA_GROUP * j + g + 1, 0:1], (tb * BLOCK, 1))
                            for g in range(SWA_GROUP)], axis=0)


def _swa_head(q, kw, vw, sink, mask):
    sc = _dnt(q, kw) * (64 ** -0.5)
    sc = jnp.where(mask, sc, NEG_INF)
    m = lax.stop_gradient(jnp.maximum(jnp.max(sc, axis=-1, keepdims=True), sink))
    p = jnp.exp(sc - m)
    denom = jnp.sum(p, axis=-1, keepdims=True) + jnp.exp(sink - m)
    return _dnn(p / denom, vw)


def _swa_prep(proj, cos, sin):
    def fn(sq, sk, sv, c, s):
        return _rope(sq, c, s, 32), _rope(sk, c, s, 32), sv
    t_len = proj.shape[0]
    return _rowwise("swa_prep", fn, [_seg(proj, "sq"), _seg(proj, "sk"), _seg(proj, "sv"), cos, sin], [],
                    [((t_len, 512), BF16), ((t_len, 128), BF16), ((t_len, 128), BF16)])


def _swa_specs(t_len, tb_max=1):
    qb = min(SWA_QB, t_len // BLOCK)
    tb = min(tb_max, qb)
    rows = qb * BLOCK
    cur = lambda w: pl.BlockSpec((rows, w), lambda i: (i, 0))
    prev = pl.BlockSpec((BLOCK, 128), lambda i: (jnp.maximum(qb * i - 1, 0), 0))
    return qb, tb, t_len // rows, cur, prev


def _swa_fwd(q, k, v, sinks):
    t_len = q.shape[0]
    qb, tb, steps, cur, prev = _swa_specs(t_len, SWA_TB_FWD)

    def body(q_ref, kp_ref, kc_ref, vp_ref, vc_ref, s_ref, o_ref):
        first = pl.program_id(0) == 0
        k_all = jnp.concatenate([kp_ref[...], kc_ref[...]], axis=0)
        v_all = jnp.concatenate([vp_ref[...], vc_ref[...]], axis=0)
        for j in range(SWA_KV):
            cols = slice(64 * j, 64 * j + 64)
            sink = _swa_sinks(s_ref, j, tb)
            for b in range(0, qb, tb):
                window = slice(BLOCK * b, BLOCK * (b + tb + 1))
                mask = _swa_mask(tb, first if b == 0 else False)
                o = _swa_head(_swa_stack(q_ref, j, b, tb), k_all[window, cols], v_all[window, cols], sink, mask)
                for g in range(SWA_GROUP):
                    h = SWA_GROUP * j + g
                    o_ref[BLOCK * b:BLOCK * (b + tb), 64 * h:64 * h + 64] = o[tb * BLOCK * g:tb * BLOCK * (g + 1)].astype(o_ref.dtype)

    return pl.pallas_call(
        body, name="swa_fwd", grid=(steps,),
        in_specs=[cur(512), prev, cur(128), prev, cur(128), _const_spec((8, 128))],
        out_specs=cur(512), out_shape=jax.ShapeDtypeStruct((t_len, 512), BF16),
        compiler_params=_params(("parallel",)))(q, k, k, v, v, sinks)


def _swa_bwd(q, k, v, sinks, dy):
    t_len = q.shape[0]
    qb, tb, steps, cur, prev = _swa_specs(t_len, SWA_TB_BWD)

    def body(q_ref, kp_ref, kc_ref, vp_ref, vc_ref, s_ref, dy_ref, dq_ref, dk_ref, dv_ref, dkp_ref, dvp_ref, ds_ref):
        first = pl.program_id(0) == 0

        @pl.when(first)
        def _():
            ds_ref[...] = jnp.zeros_like(ds_ref)

        k_all = jnp.concatenate([kp_ref[...], kc_ref[...]], axis=0)
        v_all = jnp.concatenate([vp_ref[...], vc_ref[...]], axis=0)
        for j in range(SWA_KV):
            cols = slice(64 * j, 64 * j + 64)
            sink = _swa_sinks(s_ref, j, tb)
            dk_parts = [jnp.zeros((BLOCK, 64), F32) for _ in range(qb + 1)]
            dv_parts = [jnp.zeros((BLOCK, 64), F32) for _ in range(qb + 1)]
            dsink = jnp.zeros((SWA_GROUP * tb * BLOCK, 1), F32)
            for b in range(0, qb, tb):
                window = slice(BLOCK * b, BLOCK * (b + tb + 1))
                mask = _swa_mask(tb, first if b == 0 else False)
                _, vjp = jax.vjp(lambda a, b_, c, d, mask=mask: _swa_head(a, b_, c, d, mask), _swa_stack(q_ref, j, b, tb),
                                 k_all[window, cols], v_all[window, cols], sink)
                dq, dkw, dvw, dsink_b = vjp(_swa_stack(dy_ref, j, b, tb, F32))
                dkw, dvw = dkw.astype(F32), dvw.astype(F32)
                for r in range(tb + 1):
                    dk_parts[b + r] = dk_parts[b + r] + dkw[BLOCK * r:BLOCK * (r + 1)]
                    dv_parts[b + r] = dv_parts[b + r] + dvw[BLOCK * r:BLOCK * (r + 1)]
                dsink = dsink + dsink_b
                for g in range(SWA_GROUP):
                    h = SWA_GROUP * j + g
                    dq_ref[BLOCK * b:BLOCK * (b + tb), 64 * h:64 * h + 64] = dq[tb * BLOCK * g:tb * BLOCK * (g + 1)].astype(F32)
            dkp_ref[:, cols], dvp_ref[:, cols] = dk_parts[0], dv_parts[0]
            for b in range(qb):
                dk_ref[BLOCK * b:BLOCK * b + BLOCK, cols] = dk_parts[b + 1]
                dv_ref[BLOCK * b:BLOCK * b + BLOCK, cols] = dv_parts[b + 1]
            for g in range(SWA_GROUP):
                h = SWA_GROUP * j + g
                ds_ref[h:h + 1, :] += jnp.broadcast_to(jnp.sum(dsink[tb * BLOCK * g:tb * BLOCK * (g + 1)], axis=0, keepdims=True), (1, 128))

    part = pl.BlockSpec((BLOCK, 128), lambda i: (i, 0))
    return pl.pallas_call(
        body, name="swa_bwd", grid=(steps,),
        in_specs=[cur(512), prev, cur(128), prev, cur(128), _const_spec((8, 128)), cur(512)],
        out_specs=[cur(512), cur(128), cur(128), part, part, _const_spec((8, 128))],
        out_shape=[jax.ShapeDtypeStruct((t_len, 512), F32), jax.ShapeDtypeStruct((t_len, 128), F32),
                   jax.ShapeDtypeStruct((t_len, 128), F32), jax.ShapeDtypeStruct((steps * BLOCK, 128), F32),
                   jax.ShapeDtypeStruct((steps * BLOCK, 128), F32), jax.ShapeDtypeStruct((8, 128), F32)],
        compiler_params=_params(("arbitrary",)))(q, k, k, v, v, sinks, dy)


def _swa_unprep(dq, dk, dv, dk_prev, dv_prev, cos, sin):
    t_len = dq.shape[0]
    qb, _, steps, cur, _ = _swa_specs(t_len)
    tail = (qb - 1) * BLOCK

    def body(dq_ref, dk_ref, dv_ref, dkn_ref, dvn_ref, c_ref, s_ref, oq_ref, ok_ref, ov_ref):
        more = (pl.program_id(0) < steps - 1).astype(F32)
        cos_t, sin_t = c_ref[...], -s_ref[...]
        oq_ref[...] = _rope(dq_ref[...], cos_t, sin_t, 32).astype(BF16)
        dk_t = jnp.concatenate([dk_ref[:tail], dk_ref[tail:] + more * dkn_ref[...]], axis=0) if tail else dk_ref[...] + more * dkn_ref[...]
        dv_t = jnp.concatenate([dv_ref[:tail], dv_ref[tail:] + more * dvn_ref[...]], axis=0) if tail else dv_ref[...] + more * dvn_ref[...]
        ok_ref[...] = _rope(dk_t, cos_t, sin_t, 32).astype(BF16)
        ov_ref[...] = dv_t.astype(BF16)

    nxt = pl.BlockSpec((BLOCK, 128), lambda i: (jnp.minimum(i + 1, steps - 1), 0))
    return pl.pallas_call(
        body, name="swa_unprep", grid=(steps,),
        in_specs=[cur(512), cur(128), cur(128), nxt, nxt, cur(128), cur(128)],
        out_specs=[cur(512), cur(128), cur(128)],
        out_shape=[jax.ShapeDtypeStruct((t_len, 512), BF16), jax.ShapeDtypeStruct((t_len, 128), BF16),
                   jax.ShapeDtypeStruct((t_len, 128), BF16)],
        compiler_params=_params(("parallel",)))(dq, dk, dv, dk_prev, dv_prev, cos, sin)


def _s5_discretize(lam_re, lam_im, log_dt, b_re, b_im):
    dt = jnp.exp(log_dt)
    mag = jnp.exp(lam_re * dt)
    ab_re, ab_im = mag * jnp.cos(lam_im * dt), mag * jnp.sin(lam_im * dt)
    den = lam_re * lam_re + lam_im * lam_im
    nr, ni = ab_re - 1.0, ab_im
    f_re = (nr * lam_re + ni * lam_im) / den
    f_im = (ni * lam_re - nr * lam_im) / den
    return ab_re, ab_im, f_re * b_re - f_im * b_im, f_re * b_im + f_im * b_re


def _s5_param_fwd(lam_re, lam_im, log_dt, b_re, b_im):
    def body(*refs):
        outs = _s5_discretize(*[r[...] for r in refs[:5]])
        for ref, val in zip(refs[5:], outs):
            ref[...] = val
    n = S5_CH
    return pl.pallas_call(body, name="s5_param_fwd",
                          out_shape=[jax.ShapeDtypeStruct((n, 1), F32)] * 2 + [jax.ShapeDtypeStruct((n, 16), F32)] * 2,
                          )(lam_re, lam_im, log_dt, b_re, b_im)


def _s5_param_bwd(lam_re, lam_im, log_dt, b_re, b_im, group_sum, d_ab_re, d_ab_im, d_bb_re, d_bb_im):
    def body(*refs):
        ins = [r[...] for r in refs[:10]]
        _, vjp = jax.vjp(_s5_discretize, *ins[:5])
        d_lr, d_li, d_ldt, d_br, d_bi = vjp(tuple(ins[6:10]))
        o_lr, o_li, o_ldt, o_br, o_bi = refs[10:]
        o_lr[...], o_li[...], o_br[...], o_bi[...] = d_lr, d_li, d_br, d_bi
        o_ldt[...] = lax.dot_general(ins[5], jnp.broadcast_to(d_ldt, (S5_CH, 128)), NN,
                                     precision=lax.Precision.HIGHEST, preferred_element_type=F32)
    n = S5_CH
    return pl.pallas_call(body, name="s5_param_bwd",
                          out_shape=[jax.ShapeDtypeStruct((n, 1), F32), jax.ShapeDtypeStruct((n, 1), F32),
                                     jax.ShapeDtypeStruct((S5_GROUPS, 128), F32),
                                     jax.ShapeDtypeStruct((n, 16), F32), jax.ShapeDtypeStruct((n, 16), F32)],
                          )(lam_re, lam_im, log_dt, b_re, b_im, group_sum, d_ab_re, d_ab_im, d_bb_re, d_bb_im)


def _s5_expand(name, u, w):
    t_len = u.shape[0]
    tile = min(512, t_len)

    def body(u_ref, w_ref, o_ref):
        o_ref[...] = _raw_dot(u_ref[...], w_ref[...], NN)

    return pl.pallas_call(
        body, name=name, grid=(t_len // tile, 8),
        in_specs=[pl.BlockSpec((tile, 128), lambda i, bb: (i, bb % 4)), pl.BlockSpec((None, 128, 512), lambda i, bb: (bb, 0, 0))],
        out_specs=pl.BlockSpec((None, tile, 512), lambda i, bb: (bb // 4, i, bb % 4)),
        out_shape=jax.ShapeDtypeStruct((2, t_len, S5_CH), F32), compiler_params=_params(("parallel", "parallel")))(u, w)


def _s5_contract(name, x, w):
    t_len = x.shape[1]
    tile = min(512, t_len)

    def body(xr_ref, xi_ref, wr_ref, wi_ref, o_ref):
        o_ref[...] = _raw_dot(xr_ref[...], wr_ref[...], NN) + _raw_dot(xi_ref[...], wi_ref[...], NN)

    return pl.pallas_call(
        body, name=name, grid=(t_len // tile, 4),
        in_specs=[pl.BlockSpec((None, tile, 512), lambda i, b: (0, i, b)), pl.BlockSpec((None, tile, 512), lambda i, b: (1, i, b)),
                  pl.BlockSpec((None, 512, 128), lambda i, b: (b, 0, 0)), pl.BlockSpec((None, 512, 128), lambda i, b: (4 + b, 0, 0))],
        out_specs=pl.BlockSpec((tile, 128), lambda i, b: (i, b)),
        out_shape=jax.ShapeDtypeStruct((t_len, 512), F32), compiler_params=_params(("parallel", "parallel")))(x, x, w, w)


def _s5_band_grad(name, narrow, wide, narrow_first):
    t_len = narrow.shape[0]

    def body(n_ref, w_ref, o_ref):
        if narrow_first:
            o_ref[...] = _raw_dot(n_ref[...], w_ref[...], TN)
        else:
            o_ref[...] = _raw_dot(w_ref[...], n_ref[...], TN)

    shape = (8, 128, 512) if narrow_first else (8, 512, 128)
    return pl.pallas_call(
        body, name=name, grid=(8,),
        in_specs=[pl.BlockSpec((t_len, 128), lambda bb: (0, bb % 4)), pl.BlockSpec((None, t_len, 512), lambda bb: (bb // 4, 0, bb % 4))],
        out_specs=pl.BlockSpec((None,) + shape[1:], lambda bb: (bb, 0, 0)),
        out_shape=jax.ShapeDtypeStruct(shape, F32), compiler_params=_params(("parallel",)))(narrow, wide)


def _s5_scan(name, bu, a_re, a_im, states=None, carry_in=None):
    t_len = bu.shape[1]
    n = t_len // SCAN_SEG
    log_n = int(round(math.log2(n)))
    assert 2 ** log_n == n
    reverse = states is not None
    lanes = SCAN_LANES // 2 if reverse else SCAN_LANES
    nblk = S5_CH // lanes

    def body(*refs):
        if reverse:
            b_ref, ar_ref, ai_ref, x_ref, cin_ref, o_ref, dar_ref, dai_ref, c_scr = refs
            xr_ref, xi_ref, cr_ref, ci_ref = x_ref.at[0], x_ref.at[1], cin_ref.at[0], cin_ref.at[1]
        else:
            b_ref, ar_ref, ai_ref, o_ref, co_ref, c_scr = refs
            cor_ref, coi_ref = co_ref.at[0], co_ref.at[1]
        br_ref, bi_ref, or_ref, oi_ref = b_ref.at[0], b_ref.at[1], o_ref.at[0], o_ref.at[1]
        ar = ar_ref[...]
        ai = -ai_ref[...] if reverse else ai_ref[...]

        def tile_at(i):
            return pl.multiple_of((n - 1 - i if reverse else i) * SCAN_SEG, SCAN_SEG)

        def advance(s, row):
            sr, si = s
            return (ar * sr - ai * si + br_ref[pl.ds(row, SCAN_SEG), :], ar * si + ai * sr + bi_ref[pl.ds(row, SCAN_SEG), :])

        zero = jnp.zeros((SCAN_SEG, lanes), F32)
        fin_r, fin_i = lax.fori_loop(0, n, lambda i, s: advance(s, tile_at(i)), (zero, zero), unroll=8)
        pr, pi = ar[0:1], ai[0:1]
        for _ in range(log_n):
            pr, pi = pr * pr - pi * pi, 2.0 * pr * pi
        er = ei = jnp.zeros((1, lanes), F32)
        for step in range(SCAN_SEG):
            j = SCAN_SEG - 1 - step if reverse else step
            c_scr[0, j:j + 1, :] = er
            c_scr[1, j:j + 1, :] = ei
            er, ei = pr * er - pi * ei + fin_r[j:j + 1], pr * ei + pi * er + fin_i[j:j + 1]
        entry = (c_scr[0], c_scr[1])

        if not reverse:
            cor_ref[...], coi_ref[...] = entry

            def step2(i, s):
                row = tile_at(i)
                s = advance(s, row)
                or_ref[pl.ds(row, SCAN_SEG), :] = s[0]
                oi_ref[pl.ds(row, SCAN_SEG), :] = s[1]
                return s
            lax.fori_loop(0, n, step2, entry, unroll=8)
        else:
            def emit(s, row, xr_prev, xi_prev, acc):
                s = advance(s, row)
                or_ref[pl.ds(row, SCAN_SEG), :] = s[0]
                oi_ref[pl.ds(row, SCAN_SEG), :] = s[1]
                return s, (acc[0] + s[0] * xr_prev + s[1] * xi_prev, acc[1] + s[1] * xr_prev - s[0] * xi_prev)

            def step2(i, carry):
                s, acc = carry
                row = tile_at(i)
                prev = pl.multiple_of(row - SCAN_SEG, SCAN_SEG)
                return emit(s, row, xr_ref[pl.ds(prev, SCAN_SEG), :], xi_ref[pl.ds(prev, SCAN_SEG), :], acc)
            s, acc = lax.fori_loop(0, n - 1, step2, (entry, (zero, zero)), unroll=8)
            s, acc = emit(s, 0, cr_ref[...], ci_ref[...], acc)
            dar_ref[...] = jnp.sum(acc[0], axis=0, keepdims=True)
            dai_ref[...] = jnp.sum(acc[1], axis=0, keepdims=True)

    planes = pl.BlockSpec((2, t_len, lanes), lambda b: (0, 0, b))
    tile8 = pl.BlockSpec((SCAN_SEG, lanes), lambda b: (0, b))
    entry_spec = pl.BlockSpec((2, SCAN_SEG, lanes), lambda b: (0, 0, b))
    row1 = pl.BlockSpec((1, lanes), lambda b: (0, b))
    seq = jax.ShapeDtypeStruct((2, t_len, S5_CH), F32)
    if reverse:
        in_specs = [planes, tile8, tile8, planes, entry_spec]
        args = (bu, a_re, a_im, states, carry_in)
        out_specs = [planes, row1, row1]
        out_shape = [seq, jax.ShapeDtypeStruct((1, S5_CH), F32), jax.ShapeDtypeStruct((1, S5_CH), F32)]
    else:
        in_specs = [planes, tile8, tile8]
        args = (bu, a_re, a_im)
        out_specs = [planes, entry_spec]
        out_shape = [seq, jax.ShapeDtypeStruct((2, SCAN_SEG, S5_CH), F32)]
    return pl.pallas_call(body, name=name, grid=(nblk,), in_specs=in_specs, out_specs=out_specs, out_shape=out_shape,
                          scratch_shapes=[pltpu.VMEM((2, SCAN_SEG, lanes), F32)],
                          compiler_params=_params(("parallel",)))(*args)


def _ret_consts():
    h = np.arange(RET_HEADS, dtype=np.float32)
    log_gamma = np.log1p(-np.exp2(-5.0 - h)).astype(np.float32)
    idx = np.arange(BLOCK, dtype=np.float32)
    diff = idx[:, None] - idx[None, :]
    decay = np.where(diff >= 0, np.exp(log_gamma[:, None, None] * np.maximum(diff, 0.0)), 0.0).astype(np.float32)
    k_w = np.exp(log_gamma[:, None] * (BLOCK - 1 - idx)[None, :]).astype(np.float32)[:, :, None]
    q_w = np.exp(log_gamma[:, None] * (idx + 1.0)[None, :]).astype(np.float32)[:, :, None]
    chunk_decay = [float(v) for v in np.exp(log_gamma * BLOCK).astype(np.float32)]
    return jnp.asarray(decay), jnp.asarray(q_w), jnp.asarray(k_w), chunk_decay


def _ret_head(q, k, v, g, prev, decay, q_w, k_w, chunk_decay):
    inner_s = _dnt(q, k) * decay
    y = _dnn(inner_s, v) + _dnn(q * q_w, prev)
    mu = jnp.mean(y, axis=-1, keepdims=True)
    var = jnp.mean(jnp.square(y - mu), axis=-1, keepdims=True)
    yn = (y - mu) * lax.rsqrt(var + EPS)
    out = g * jax.nn.sigmoid(g) * yn
    return out, prev * chunk_decay + _dtn(k * k_w, v)


def _ret_prep(proj, cos, sin):
    def fn(rq, rk, c, s):
        return _rope(rq, c, s, 32), _rope(rk, c, s, 32) * (RET_DK ** -0.5)
    t_len = proj.shape[0]
    return _rowwise("ret_prep", fn, [_seg(proj, "rq"), _seg(proj, "rk"), cos, sin], [],
                    [((t_len, 256), F32), ((t_len, 256), F32)])


def _ret_unprep(dq, dk, cos, sin):
    def fn(dq_t, dk_t, c, s):
        return _rope(dq_t, c, -s, 32), _rope(dk_t * (RET_DK ** -0.5), c, -s, 32)
    t_len = dq.shape[0]
    return _rowwise("ret_unprep", fn, [dq, dk, cos, sin], [], [((t_len, 256), BF16), ((t_len, 256), BF16)])


def _ret_fwd(q, k, proj):
    t_len = q.shape[0]
    nc = t_len // BLOCK
    decay, q_w, k_w, chunk_decay = _ret_consts()

    def body(q_ref, k_ref, v_ref, g_ref, dec_ref, qw_ref, kw_ref, o_ref, prev_ref, state):
        @pl.when(pl.program_id(0) == 0)
        def _():
            state[...] = jnp.zeros_like(state)
        for h in range(RET_HEADS):
            qk, vv = slice(64 * h, 64 * h + 64), slice(128 * h, 128 * h + 128)
            prev = state[h]
            prev_ref[h] = prev
            out, nxt = _ret_head(q_ref[:, qk], k_ref[:, qk], v_ref[:, vv], g_ref[:, vv], prev,
                                 dec_ref[h], qw_ref[h], kw_ref[h], chunk_decay[h])
            o_ref[:, vv] = out.astype(o_ref.dtype)
            state[h] = nxt

    chunk = lambda w, c=0: pl.BlockSpec((BLOCK, w), lambda i, c=c: (i, c))
    return pl.pallas_call(
        body, name="ret_fwd", grid=(nc,),
        in_specs=[chunk(256), chunk(256), chunk(512, PROJ["rv"][0] // 512), chunk(512, PROJ["rg"][0] // 512),
                  _const_spec(decay.shape), _const_spec(q_w.shape), _const_spec(k_w.shape)],
        out_specs=[chunk(512), pl.BlockSpec((None, RET_HEADS, RET_DK, RET_DV), lambda i: (i, 0, 0, 0))],
        out_shape=[jax.ShapeDtypeStruct((t_len, 512), BF16), jax.ShapeDtypeStruct((nc, RET_HEADS, RET_DK, RET_DV), F32)],
        scratch_shapes=[pltpu.VMEM((RET_HEADS, RET_DK, RET_DV), F32)],
        compiler_params=_params(("arbitrary",)))(q, k, proj, proj, decay, q_w, k_w)


def _ret_bwd(q, k, proj, prevs, dy):
    t_len = q.shape[0]
    nc = t_len // BLOCK
    decay, q_w, k_w, chunk_decay = _ret_consts()

    def body(q_ref, k_ref, v_ref, g_ref, prev_ref, dy_ref, dec_ref, qw_ref, kw_ref, dq_ref, dk_ref, dv_ref, dg_ref, dstate):
        @pl.when(pl.program_id(0) == 0)
        def _():
            dstate[...] = jnp.zeros_like(dstate)
        for h in range(RET_HEADS):
            qk, vv = slice(64 * h, 64 * h + 64), slice(128 * h, 128 * h + 128)
            head = functools.partial(_ret_head, decay=dec_ref[h], q_w=qw_ref[h], k_w=kw_ref[h], chunk_decay=chunk_decay[h])
            _, vjp = jax.vjp(head, q_ref[:, qk], k_ref[:, qk], v_ref[:, vv], g_ref[:, vv], prev_ref[h])
            dq, dk, dv, dg, dprev = vjp((dy_ref[:, vv].astype(F32), dstate[h]))
            dq_ref[:, qk], dk_ref[:, qk] = dq, dk
            dv_ref[:, vv], dg_ref[:, vv] = dv.astype(dv_ref.dtype), dg.astype(dg_ref.dtype)
            dstate[h] = dprev

    chunk = lambda w, c=0: pl.BlockSpec((BLOCK, w), lambda i, c=c: (nc - 1 - i, c))
    return pl.pallas_call(
        body, name="ret_bwd", grid=(nc,),
        in_specs=[chunk(256), chunk(256), chunk(512, PROJ["rv"][0] // 512), chunk(512, PROJ["rg"][0] // 512),
                  pl.BlockSpec((None, RET_HEADS, RET_DK, RET_DV), lambda i: (nc - 1 - i, 0, 0, 0)), chunk(512),
                  _const_spec(decay.shape), _const_spec(q_w.shape), _const_spec(k_w.shape)],
        out_specs=[chunk(256), chunk(256), chunk(512), chunk(512)],
        out_shape=[jax.ShapeDtypeStruct((t_len, 256), F32), jax.ShapeDtypeStruct((t_len, 256), F32),
                   jax.ShapeDtypeStruct((t_len, 512), BF16), jax.ShapeDtypeStruct((t_len, 512), BF16)],
        scratch_shapes=[pltpu.VMEM((RET_HEADS, RET_DK, RET_DV), F32)],
        compiler_params=_params(("arbitrary",)))(q, k, proj, proj, prevs, dy, decay, q_w, k_w)


MLA_SCALE = 96 ** -0.5


def _mla_prep(qf, kvf, proj, cos, sin):
    def fn(q, kv, kr, c, s):
        q_rope = _rope(q[:, 512:768], c, s, 16)
        k_rope = _rope(kr, c, s, 16)[:, :32]
        zero = jnp.zeros_like(k_rope)
        qs = [jnp.concatenate([q[:, 64 * h:64 * h + 64], q_rope[:, 32 * h:32 * h + 32], zero], axis=1) for h in range(MLA_HEADS)]
        ks = [jnp.concatenate([kv[:, 64 * h:64 * h + 64], k_rope, zero], axis=1) for h in range(MLA_HEADS)]
        ones = (lax.broadcasted_iota(jnp.int32, (kv.shape[0], 64), 1) == 0).astype(F32)
        vs = [jnp.concatenate([kv[:, 512 + 64 * h:576 + 64 * h], ones], axis=1) for h in range(MLA_HEADS)]
        return qs, ks, vs
    t_len = qf.shape[0]
    return _rowwise("mla_prep", fn, [qf, kvf, _seg(proj, "kr"), cos, sin], [],
                    [((8, t_len, 128), BF16), ((8, t_len, 128), BF16), ((8, t_len, 128), BF16)])


def _mla_unprep(dq, dk, dv, cos, sin):
    def fn(dq_t, dk_t, dv_t, c, s):
        q_rope = _rope(jnp.concatenate([dq_t[h][:, 64:96] for h in range(MLA_HEADS)], axis=1), c, -s, 16)
        d_qf = jnp.concatenate([dq_t[h][:, :64] for h in range(MLA_HEADS)] + [q_rope], axis=1)
        d_kvf = jnp.concatenate([dk_t[h][:, :64] for h in range(MLA_HEADS)] + [dv_t[h][:, :64] for h in range(MLA_HEADS)], axis=1)
        k_rope = dk_t[0][:, 64:96]
        for h in range(1, MLA_HEADS):
            k_rope = k_rope + dk_t[h][:, 64:96]
        d_kr = _rope(jnp.concatenate([k_rope, jnp.zeros((k_rope.shape[0], 96), F32)], axis=1), c, -s, 16)
        return d_qf, d_kvf, d_kr
    t_len = dq.shape[1]
    return _rowwise("mla_unprep", fn, [dq, dk, dv, cos, sin], [],
                    [((t_len, 768), BF16), ((t_len, 1024), BF16), ((t_len, 128), BF16)])


MLA_EXP2 = MLA_SCALE * math.log2(math.e)
MLA_TILE = 1024


def _diag_mask(tq):
    return lax.broadcasted_iota(jnp.int32, (tq, tq), 1) <= lax.broadcasted_iota(jnp.int32, (tq, tq), 0)


def _tri_steps(nq, key_major):
    if key_major:
        pairs = [(i, j) for j in range(nq) for i in range(j, nq)]
    else:
        pairs = [(i, j) for i in range(nq) for j in range(i + 1)]
    return jnp.asarray([p[0] for p in pairs], jnp.int32), jnp.asarray([p[1] for p in pairs], jnp.int32)


def _mla_fwd(q, k, v):
    t_len = q.shape[1]
    tq = min(MLA_TILE, t_len)
    nq = t_len // tq
    qi, kj = _tri_steps(nq, False)

    def body(i_ref, j_ref, q_ref, k_ref, v_ref, o_ref, lse_ref, m_s, acc_s):
        step = pl.program_id(1)
        i, j = i_ref[step], j_ref[step]

        @pl.when(j == 0)
        def _():
            m_s[...] = jnp.full_like(m_s, NEG_INF)
            acc_s[...] = jnp.zeros_like(acc_s)

        def update(mask):
            for e in range(2):
                s = _raw_dot(q_ref[e], k_ref[e], NT)
                if mask is not None:
                    s = jnp.where(mask, s, NEG_INF)
                m_old = m_s[e]
                m_new = jnp.maximum(m_old, jnp.max(s, axis=-1, keepdims=True))
                p = jnp.exp2((s - m_new) * MLA_EXP2)
                acc_s[e] = jnp.exp2((m_old - m_new) * MLA_EXP2) * acc_s[e] + _raw_dot(p, v_ref[e], NN)
                m_s[e] = m_new

        @pl.when(j < i)
        def _():
            update(None)

        @pl.when(j == i)
        def _():
            update(_diag_mask(tq))
            outs = []
            for e in range(2):
                acc = acc_s[e]
                denom = acc[:, 64:65]
                outs.append(acc[:, :64] / denom)
                lse_ref[e] = m_s[e] * MLA_EXP2 + jnp.log2(denom)
            o_ref[...] = jnp.concatenate(outs, axis=1)

    grid_spec = pltpu.PrefetchScalarGridSpec(
        num_scalar_prefetch=2, grid=(MLA_HEADS // 2, int(qi.shape[0])),
        in_specs=[pl.BlockSpec((2, tq, 128), lambda hp, s, qi, kj: (hp, qi[s], 0)),
                  pl.BlockSpec((2, tq, 128), lambda hp, s, qi, kj: (hp, kj[s], 0)),
                  pl.BlockSpec((2, tq, 128), lambda hp, s, qi, kj: (hp, kj[s], 0))],
        out_specs=[pl.BlockSpec((tq, 128), lambda hp, s, qi, kj: (qi[s], hp)),
                   pl.BlockSpec((2, tq, 1), lambda hp, s, qi, kj: (hp, qi[s], 0))],
        scratch_shapes=[pltpu.VMEM((2, tq, 1), F32), pltpu.VMEM((2, tq, 128), F32)])
    return pl.pallas_call(
        body, name="mla_fwd", grid_spec=grid_spec,
        out_shape=[jax.ShapeDtypeStruct((t_len, 512), F32), jax.ShapeDtypeStruct((MLA_HEADS, t_len, 1), F32)],
        compiler_params=_params(("parallel", "arbitrary")))(qi, kj, q, k, v)


def _mla_bwd_prep(o, dy):
    def fn(o_t, dy_t):
        dos, deltas = [], []
        for h in range(MLA_HEADS):
            d_h = dy_t[:, 64 * h:64 * h + 64].astype(F32)
            dos.append(jnp.concatenate([d_h, jnp.zeros_like(d_h)], axis=1))
            deltas.append(jnp.sum(d_h * o_t[:, 64 * h:64 * h + 64], axis=-1, keepdims=True))
        return dos, deltas
    t_len = o.shape[0]
    return _rowwise("mla_bwd_prep", fn, [o, dy], [], [((8, t_len, 128), BF16), ((8, t_len, 1), F32)])


def _mla_bwd(q, k, v, do, lse2, delta):
    t_len = q.shape[1]
    tq = min(MLA_TILE, t_len)
    nq = t_len // tq
    qi, kj = _tri_steps(nq, True)

    def body(i_ref, j_ref, q_ref, k_ref, v_ref, do_ref, lse_ref, dl_ref, dq_ref, dk_ref, dv_ref, k_acc, v_acc):
        step = pl.program_id(1)
        i, j = i_ref[step], j_ref[step]

        @pl.when(step == 0)
        def _():
            dq_ref[...] = jnp.zeros_like(dq_ref)

        @pl.when(i == j)
        def _():
            k_acc[...] = jnp.zeros_like(k_acc)
            v_acc[...] = jnp.zeros_like(v_acc)

        def tile(mask):
            p = jnp.exp2(_raw_dot(q_ref[...], k_ref[...], NT) * MLA_EXP2 - lse_ref[...])
            if mask is not None:
                p = jnp.where(mask, p, 0.0)
            ds = p * (_raw_dot(do_ref[...], v_ref[...], NT) - dl_ref[...]) * MLA_SCALE
            v_acc[...] += _raw_dot(p, do_ref[...], TN)
            k_acc[...] += _raw_dot(ds, q_ref[...], TN)
            rows = pl.ds(pl.multiple_of(i * tq, tq), tq)
            dq_ref[rows, :] += _raw_dot(ds, k_ref[...], NN)

        @pl.when(i == j)
        def _():
            tile(_diag_mask(tq))

        @pl.when(i > j)
        def _():
            tile(None)

        @pl.when(i == nq - 1)
        def _():
            dk_ref[...] = k_acc[...]
            dv_ref[...] = v_acc[...]

    q_blk = lambda w: pl.BlockSpec((None, tq, w), lambda h, s, qi, kj: (h, qi[s], 0))
    k_blk = lambda w: pl.BlockSpec((None, tq, w), lambda h, s, qi, kj: (h, kj[s], 0))
    grid_spec = pltpu.PrefetchScalarGridSpec(
        num_scalar_prefetch=2, grid=(MLA_HEADS, int(qi.shape[0])),
        in_specs=[q_blk(128), k_blk(128), k_blk(128), q_blk(128), q_blk(1), q_blk(1)],
        out_specs=[pl.BlockSpec((None, t_len, 128), lambda h, s, qi, kj: (h, 0, 0)), k_blk(128), k_blk(128)],
        scratch_shapes=[pltpu.VMEM((tq, 128), F32), pltpu.VMEM((tq, 128), F32)])
    return pl.pallas_call(
        body, name="mla_bwd", grid_spec=grid_spec,
        out_shape=[jax.ShapeDtypeStruct((MLA_HEADS, t_len, 128), F32), jax.ShapeDtypeStruct((MLA_HEADS, t_len, 128), F32),
                   jax.ShapeDtypeStruct((MLA_HEADS, t_len, 128), F32)],
        compiler_params=_params(("parallel", "arbitrary")))(qi, kj, q, k, v, do, lse2, delta)


MERGE_TN = 256
GATE_BLOCK0 = PROJ["gates"][0] // MERGE_TN


def _merge_specs(tm):
    y_spec = pl.BlockSpec((tm, 512), lambda i, j: (i, 0))
    w_spec = pl.BlockSpec((512, MERGE_TN), lambda i, j: (0, j))
    gate = lambda b: pl.BlockSpec((tm, MERGE_TN), lambda i, j, b=b: (i, GATE_BLOCK0 + 4 * b + j))
    return [y_spec] * 4 + [w_spec] * 4 + [gate(b) for b in range(4)]


def _merge_fwd(ys, wbs, proj):
    t_len = proj.shape[0]
    tm = min(512, t_len)

    def body(*refs):
        y, w, gl, o_ref = refs[0:4], refs[4:8], refs[8:12], refs[12]
        acc = jnp.zeros((tm, MERGE_TN), F32)
        for b in range(4):
            acc += jax.nn.sigmoid(gl[b][...]) * _raw_dot(y[b][...], w[b][...], NN)
        o_ref[...] = acc.astype(o_ref.dtype)

    return pl.pallas_call(body, name="merge_fwd", grid=(t_len // tm, D_MODEL // MERGE_TN), in_specs=_merge_specs(tm),
                          out_specs=pl.BlockSpec((tm, MERGE_TN), lambda i, j: (i, j)),
                          out_shape=jax.ShapeDtypeStruct((t_len, D_MODEL), BF16),
                          compiler_params=_params(("parallel", "parallel")))(*ys, *wbs, *([proj] * 4))


def _merge_bwd(ys, wbs, proj, dmerged):
    t_len = proj.shape[0]
    tm = min(512, t_len)

    def body(*refs):
        y, w, gl, dm_ref, dgl, dt = refs[0:4], refs[4:8], refs[8:12], refs[12], refs[13:17], refs[17:21]
        dm = dm_ref[...]
        for b in range(4):
            gate = jax.nn.sigmoid(gl[b][...])
            t_b = _raw_dot(y[b][...], w[b][...], NN)
            dgl[b][...] = (dm * t_b * gate * (1.0 - gate)).astype(BF16)
            dt[b][...] = (dm * gate).astype(BF16)

    o_spec = pl.BlockSpec((tm, MERGE_TN), lambda i, j: (i, j))
    return pl.pallas_call(body, name="merge_bwd", grid=(t_len // tm, D_MODEL // MERGE_TN), in_specs=_merge_specs(tm) + [o_spec],
                          out_specs=[o_spec] * 8, out_shape=[jax.ShapeDtypeStruct((t_len, D_MODEL), BF16)] * 8,
                          compiler_params=_params(("parallel", "parallel")))(*ys, *wbs, *([proj] * 4), dmerged)


def _adamw(name, w, g, m, v, slabs):
    shape = w.shape
    size = int(np.prod(shape))
    if shape[-1] >= 96:
        view = (size // shape[-1], shape[-1])
    elif size % 128 == 0:
        view = (size // 128, 128)
    else:
        view = (1, size)
    rows = view[0]
    tile = rows
    for cand in (512, 256, 128, 64):
        if rows > cand and rows % cand == 0 and cand * view[1] * 4 <= 2 ** 20:
            tile = cand
            break
    if rows * view[1] * 4 <= 2 ** 20:
        tile = rows
    c1, c2 = 1.0 - ADAM_B1 ** ADAM_STEP, 1.0 - ADAM_B2 ** ADAM_STEP

    def body(w_ref, g_ref, m_ref, v_ref, og_ref, od_ref, om_ref, ov_ref):
        if slabs:
            grad = g_ref[0].astype(F32)
            for d in range(1, slabs):
                grad = grad + g_ref[d].astype(F32)
        else:
            grad = g_ref[...]
        m_new = ADAM_B1 * m_ref[...] + (1.0 - ADAM_B1) * grad
        v_new = ADAM_B2 * v_ref[...] + (1.0 - ADAM_B2) * jnp.square(grad)
        og_ref[...] = grad
        od_ref[...] = -ADAM_LR * ((m_new / c1) / (jnp.sqrt(v_new / c2) + ADAM_EPS) + ADAM_WD * w_ref[...])
        om_ref[...] = m_new
        ov_ref[...] = v_new

    spec = pl.BlockSpec((tile, view[1]), lambda i: (i, 0))
    g_spec = pl.BlockSpec((slabs, tile, view[1]), lambda i: (0, i, 0)) if slabs else spec
    g_view = g.reshape((slabs,) + view) if slabs else g.reshape(view)
    outs = pl.pallas_call(body, name=name, grid=(rows // tile,), in_specs=[spec, g_spec, spec, spec], out_specs=[spec] * 4,
                          out_shape=[jax.ShapeDtypeStruct(view, F32)] * 4,
                          compiler_params=_params(("parallel",)))(w.reshape(view), g_view, m.reshape(view), v.reshape(view))
    return [o.reshape(shape) for o in outs]


def _sum_slabs(name, x):
    def body(x_ref, o_ref):
        acc = x_ref[0]
        for d in range(1, N_DEV):
            acc = acc + x_ref[d]
        o_ref[...] = acc
    return pl.pallas_call(body, name=name, out_shape=jax.ShapeDtypeStruct(x.shape[1:], x.dtype))(x)


MESH_ID = pl.DeviceIdType.MESH
ANY_SPEC = pl.BlockSpec(memory_space=pl.ANY)


def _remote(src, dst, send_sems, recv_sems, k, to):
    return pltpu.make_async_remote_copy(src_ref=src, dst_ref=dst, send_sem=send_sems.at[k], recv_sem=recv_sems.at[k],
                                        device_id=to, device_id_type=MESH_ID)


def _gather_two_level(name, blocks):
    n = len(blocks)

    def body(*refs):
        srcs, dsts = refs[:n], refs[n:2 * n]
        send_sems, recv_sems, local_sems = refs[2 * n:]
        x, y, c = lax.axis_index("x"), lax.axis_index("y"), lax.axis_index("c")
        sibling = (x, y, 1 - c)
        chips = [(1 - x, y), (x, 1 - y), (1 - x, 1 - y)]
        index = lambda px, py, pc: 4 * px + 2 * py + pc
        me = index(x, y, c)

        def copy(t, k, block, to, src=None):
            slot = dsts[t].at[block]
            return _remote(slot if src is None else src, slot, send_sems, recv_sems, 7 * t + k, to)

        own = [pltpu.make_async_copy(srcs[t], dsts[t].at[me], local_sems.at[t]) for t in range(n)]
        first = []
        for t in range(n):
            own[t].start()
            first.append(copy(t, 0, me, sibling, src=srcs[t]))
            first += [copy(t, 1 + j, me, (*chip, c), src=srcs[t]) for j, chip in enumerate(chips)]
        for cp in first:
            cp.start()
        passed = []
        for j, chip in enumerate(chips):
            for t in range(n):
                block = index(*chip, c)
                copy(t, 1 + j, block, sibling).wait_recv()
                passed.append(copy(t, 4 + j, block, sibling))
                passed[-1].start()
        for t in range(n):
            copy(t, 0, index(x, y, 1 - c), sibling).wait_recv()
            for j, chip in enumerate(chips):
                copy(t, 4 + j, index(*chip, 1 - c), sibling).wait_recv()
        for cp in first + passed:
            cp.wait_send()
        for cp in own:
            cp.wait()

    return pl.pallas_call(
        body, name=name, in_specs=[ANY_SPEC] * n, out_specs=[ANY_SPEC] * n,
        out_shape=[jax.ShapeDtypeStruct((N_DEV,) + b.shape, b.dtype) for b in blocks],
        scratch_shapes=[pltpu.SemaphoreType.DMA((7 * n,)), pltpu.SemaphoreType.DMA((7 * n,)), pltpu.SemaphoreType.DMA((n,))],
        compiler_params=pltpu.CompilerParams(has_side_effects=True))(*blocks)


def _pair_exchange(name, blocks):
    n = len(blocks)

    def body(*refs):
        srcs, dsts = refs[:n], refs[n:2 * n]
        send_sems, recv_sems = refs[2 * n:]
        sibling = (lax.axis_index("x"), lax.axis_index("y"), 1 - lax.axis_index("c"))
        copies = [_remote(srcs[t], dsts[t], send_sems, recv_sems, t, sibling) for t in range(n)]
        for cp in copies:
            cp.start()
        for cp in copies:
            cp.wait()

    return pl.pallas_call(
        body, name=name, in_specs=[ANY_SPEC] * n, out_specs=[ANY_SPEC] * n,
        out_shape=[jax.ShapeDtypeStruct(b.shape, b.dtype) for b in blocks],
        scratch_shapes=[pltpu.SemaphoreType.DMA((n,)), pltpu.SemaphoreType.DMA((n,))],
        compiler_params=pltpu.CompilerParams(has_side_effects=True))(*blocks)


def _chip_exchange(name, items):
    n = len(items)
    out_shape = [jax.ShapeDtypeStruct((4,) + tuple(a.shape[1:] if per_chip else a.shape), a.dtype) for a, per_chip in items]

    def body(*refs):
        srcs, dsts = refs[:n], refs[n:2 * n]
        send_sems, recv_sems, local_sems = refs[2 * n:]
        x, y, c = lax.axis_index("x"), lax.axis_index("y"), lax.axis_index("c")
        mine = 2 * x + y
        own, remote = [], []
        for t, (_, per_chip) in enumerate(items):
            own.append(pltpu.make_async_copy(srcs[t].at[mine] if per_chip else srcs[t], dsts[t].at[mine], local_sems.at[t]))
            own[-1].start()
        for k in range(1, 4):
            px, py = x ^ (k >> 1), y ^ (k & 1)
            peer = 2 * px + py
            for t, (_, per_chip) in enumerate(items):
                src = srcs[t].at[peer] if per_chip else srcs[t]
                out = _remote(src, dsts[t].at[mine], send_sems, recv_sems, 3 * t + k - 1, (px, py, c))
                out.start()
                remote.append((out, _remote(src, dsts[t].at[peer], send_sems, recv_sems, 3 * t + k - 1, (px, py, c))))
        for out, arrival in remote:
            out.wait_send()
            arrival.wait_recv()
        for cp in own:
            cp.wait()

    return pl.pallas_call(
        body, name=name, in_specs=[ANY_SPEC] * n, out_specs=[ANY_SPEC] * n, out_shape=out_shape,
        scratch_shapes=[pltpu.SemaphoreType.DMA((3 * n,)), pltpu.SemaphoreType.DMA((3 * n,)), pltpu.SemaphoreType.DMA((n,))],
        compiler_params=pltpu.CompilerParams(has_side_effects=True))(*[a for a, _ in items])


def _perm(a):
    t_len, w = a.shape
    return a.reshape(SCAN_SEG, t_len // SCAN_SEG, w).transpose(1, 0, 2).reshape(t_len, w)


def _unperm(a):
    t_len, w = a.shape
    return a.reshape(t_len // SCAN_SEG, SCAN_SEG, w).transpose(1, 0, 2).reshape(t_len, w)


def _band_diag(p):
    a, b = p.shape[1:]
    eye = jnp.eye(8, dtype=p.dtype)
    return (p.reshape(4, 8, a, 1, b) * eye[None, :, None, :, None]).reshape(4, 8 * a, 8 * b)


def _band_diag_extract(m, a, b):
    eye = jnp.eye(8, dtype=m.dtype)
    return (m.reshape(4, 8, a, 8, b) * eye[None, :, None, :, None]).sum(axis=3).reshape(32, a, b)


def _proj_cols(w_ref_layout):
    parts = []
    for name in PROJ_ORDER:
        off, width = REF_SPLIT[name]
        part = w_ref_layout[:, off:off + width]
        if width < PROJ[name][1]:
            part = jnp.pad(part, ((0, 0), (0, PROJ[name][1] - width)))
        parts.append(part)
    return jnp.concatenate(parts, axis=1)


def _unproj_cols(w_proj_layout):
    parts = []
    for name in REF_ORDER:
        parts.append(w_proj_layout[:, PROJ[name][0]:PROJ[name][0] + REF_SPLIT[name][1]])
    return jnp.concatenate(parts, axis=1)


def _uq_cols(w):
    w3 = w.reshape(w.shape[0], MLA_HEADS, 96)
    return jnp.concatenate([w3[:, :, :64].reshape(-1, 512), w3[:, :, 64:].reshape(-1, 256)], axis=1)


def _un_uq_cols(w):
    return jnp.concatenate([w[:, :512].reshape(-1, MLA_HEADS, 64), w[:, 512:].reshape(-1, MLA_HEADS, 32)], axis=2).reshape(-1, 768)


def _ukv_cols(w):
    w3 = w.reshape(w.shape[0], MLA_HEADS, 128)
    return jnp.concatenate([w3[:, :, :64].reshape(-1, 512), w3[:, :, 64:].reshape(-1, 512)], axis=1)


def _un_ukv_cols(w):
    return jnp.concatenate([w[:, :512].reshape(-1, MLA_HEADS, 64), w[:, 512:].reshape(-1, MLA_HEADS, 64)], axis=2).reshape(-1, 1024)


def _cols_gathered(g):
    return g.transpose(1, 0, 2).reshape(g.shape[1], -1)


def _cols_slabs(w):
    r, c = w.shape
    return w.reshape(r, N_DEV, c // N_DEV).transpose(1, 0, 2)


def _row(v):
    return v.reshape(1, -1)


def _s5_weights(p):
    flat = dict(lam_re=p["s5_lam_re"].reshape(S5_CH, 1), lam_im=p["s5_lam_im"].reshape(S5_CH, 1),
                log_dt=jnp.repeat(p["s5_log_dt"], S5_STATE).reshape(S5_CH, 1),
                b_re=p["s5_b_re"].reshape(S5_CH, S5_GROUP), b_im=p["s5_b_im"].reshape(S5_CH, S5_GROUP))
    ab_re, ab_im, bb_re, bb_im = _s5_param_fwd(flat["lam_re"], flat["lam_im"], flat["log_dt"], flat["b_re"], flat["b_im"])
    w_bu = jnp.concatenate([_band_diag(bb.reshape(32, 64, 16).transpose(0, 2, 1)) for bb in (bb_re, bb_im)], axis=0)
    w_y = jnp.concatenate([_band_diag(p["s5_c_re"].transpose(0, 2, 1)), -_band_diag(p["s5_c_im"].transpose(0, 2, 1))], axis=0)
    return dict(flat=flat,
                a_re=jnp.broadcast_to(ab_re.reshape(1, S5_CH), (SCAN_SEG, S5_CH)),
                a_im=jnp.broadcast_to(ab_im.reshape(1, S5_CH), (SCAN_SEG, S5_CH)),
                w_bu=w_bu.astype(BF16), w_bu_t=w_bu.swapaxes(1, 2).astype(BF16),
                w_y=w_y.astype(BF16), w_y_t=w_y.swapaxes(1, 2).astype(BF16))


def _layer_fwd(x, w, p, tabs):
    t_len = x.shape[0]
    cos64, sin64, cos32, sin32 = tabs
    s = {}
    (s["h1"],) = _rowwise("pre_mix", lambda xt, g: (_rms(xt, g),), [x], [_row(p["g_pre_mix"])], [((t_len, D_MODEL), BF16)])
    proj = s["proj"] = _mm("proj", s["h1"], w["w_in"], "nn")
    s["qa"], s["ka"], s["va"] = _swa_prep(proj, cos64, sin64)
    s["sinks"] = jnp.broadcast_to(p["swa_sinks"][:, None], (8, 128))
    y_a = _swa_fwd(s["qa"], s["ka"], s["va"], s["sinks"])
    s5 = s["s5"] = _s5_weights(p)
    su = proj[:, PROJ["su"][0]:PROJ["su"][0] + 512]
    s["u_p"] = _perm(su).astype(BF16)
    bu = _s5_expand("s5_bu", s["u_p"], s5["w_bu"])
    s["xs"], s["cin"] = _s5_scan("s5_scan_fwd", bu, s5["a_re"], s5["a_im"])
    s["y_s"] = _unperm(_s5_contract("s5_y", s["xs"], s5["w_y"]))
    (s["z"],) = _rowwise("s5_gelu", lambda y, u, d: (jax.nn.gelu(y + d * u),), [s["y_s"], _seg(proj, "su")],
                         [_row(p["s5_d"])], [((t_len, 512), BF16)])
    s["zz"] = _mm("s5_glu_mm", s["z"], w["w_glu"], "nn")
    (y_b,) = _rowwise("s5_glu", lambda t: (t[:, :512] * jax.nn.sigmoid(t[:, 512:]),), [s["zz"]], [], [((t_len, 512), BF16)])
    s["qc"], s["kc"] = _ret_prep(proj, cos64, sin64)
    y_c, s["prevs"] = _ret_fwd(s["qc"], s["kc"], proj)
    s["cqn"], s["ckvn"] = _rowwise("mla_norm", lambda cq, ckv, gq, gkv: (_rms(cq, gq), _rms(ckv, gkv)),
                                   [_seg(proj, "cq"), _seg(proj, "ckv")], [_row(p["mla_g_q"]), _row(p["mla_g_kv"])],
                                   [((t_len, 256), BF16), ((t_len, 128), BF16)])
    qf = _mm("mla_uq", s["cqn"], w["w_uq"], "nn")
    kvf = _mm("mla_ukv", s["ckvn"], w["w_ukv"], "nn")
    s["Q"], s["K"], s["V"] = _mla_prep(qf, kvf, proj, cos32, sin32)
    y_d, s["lse"] = _mla_fwd(s["Q"], s["K"], s["V"])
    s["ys"] = [y_a, y_b, y_c, y_d]
    s["merged"] = _merge_fwd(s["ys"], w["wb"], proj)
    s["o"] = _mm("out_mm", s["merged"], w["w_out"], "nn")

    def post_mix(xt, ot, g1, g2):
        x1 = xt + _rms(ot, g1)
        return x1, _rms(x1, g2)
    s["x1"], s["h2"] = _rowwise("post_mix", post_mix, [x, s["o"]], [_row(p["g_post_mix"]), _row(p["g_pre_mlp"])],
                                [((t_len, D_MODEL), F32), ((t_len, D_MODEL), BF16)])
    s["a"], s["r"] = _mm("ff1", s["h2"], w["w_ff1"], "nn", out_dtypes=(F32, BF16),
                         epi=lambda acc: (acc, jnp.square(jnp.maximum(acc, 0.0))))
    s["f"] = _mm("ff2", s["r"], w["w_ff2"], "nn")
    (x2,) = _rowwise("post_mlp", lambda x1, f, g: (x1 + _rms(f, g),), [s["x1"], s["f"]], [_row(p["g_post_mlp"])],
                     [((t_len, D_MODEL), F32)])
    return x2, s


def _layer_bwd(x, dx2, w, p, s, tabs):
    t_len = x.shape[0]
    cos64, sin64, cos32, sin32 = tabs
    proj = s["proj"]
    big, small = {}, {}

    def post_mlp_bwd(f, d, g):
        df, dg = _rms_bwd(f, g, d)
        return (df,), (dg,)
    df, small["g_post_mlp"] = _rowwise("post_mlp_bwd", post_mlp_bwd, [s["f"], dx2], [_row(p["g_post_mlp"])],
                                       [((t_len, D_MODEL), BF16)], [((1, D_MODEL), F32)])
    da = _mm("ff2_dx", df, w["w_ff2"], "nt", out_dtypes=(BF16,), epi=lambda acc, a: (acc * 2.0 * jnp.maximum(a, 0.0),),
             epi_in=(s["a"],))
    big["w_ff2"] = _mm("ff2_dw", s["r"], df, "tn")
    dh2 = _mm("ff1_dx", da, w["w_ff1"], "nt")
    big["w_ff1"] = _mm("ff1_dw", s["h2"], da, "tn")

    def post_mix_bwd(x1, o, dh, d2, g1, g2):
        dx1_n, dg2 = _rms_bwd(x1, g2, dh)
        dx1 = d2 + dx1_n
        do, dg1 = _rms_bwd(o, g1, dx1)
        return (dx1, do), (dg1, dg2)
    dx1, do, small["g_post_mix"], small["g_pre_mlp"] = _rowwise(
        "post_mix_bwd", post_mix_bwd, [s["x1"], s["o"], dh2, dx2], [_row(p["g_post_mix"]), _row(p["g_pre_mlp"])],
        [((t_len, D_MODEL), F32), ((t_len, D_MODEL), BF16)], [((1, D_MODEL), F32), ((1, D_MODEL), F32)])
    dmerged = _mm("out_dx", do, w["w_out"], "nt")
    big["w_out"] = _mm("out_dw", s["merged"], do, "tn")

    res = _merge_bwd(s["ys"], w["wb"], proj, dmerged)
    dgl, dts = res[:4], res[4:]
    dys = [_mm("branch_dx", dts[b], w["wb"][b], "nt") for b in range(4)]
    big["wb"] = [_mm("branch_dw", s["ys"][b], dts[b], "tn") for b in range(4)]
    seg = {}

    dqa, dka, dva, dka_prev, dva_prev, dsink = _swa_bwd(s["qa"], s["ka"], s["va"], s["sinks"], dys[0])
    seg["sq"], seg["sk"], seg["sv"] = _swa_unprep(dqa, dka, dva, dka_prev, dva_prev, cos64, sin64)
    small["swa_sinks"] = dsink[:, 0]

    def glu_bwd(t, d):
        za, sg = t[:, :512], jax.nn.sigmoid(t[:, 512:])
        return (jnp.concatenate([d * sg, d * za * sg * (1.0 - sg)], axis=1),)
    (dzz,) = _rowwise("s5_glu_bwd", glu_bwd, [s["zz"], dys[1]], [], [((t_len, 1024), BF16)])
    dz = _mm("s5_glu_dx", dzz, w["w_glu"], "nt")
    big["w_glu"] = _mm("s5_glu_dw", s["z"], dzz, "tn")

    def gelu_bwd(y, u, d, skip):
        _, vjp = jax.vjp(jax.nn.gelu, y + skip * u)
        (dy2,) = vjp(d)
        return (dy2, dy2 * skip), (jnp.sum(dy2 * u, axis=0, keepdims=True),)
    dy_s, du_skip, d_skip = _rowwise("s5_gelu_bwd", gelu_bwd, [s["y_s"], _seg(proj, "su"), dz], [_row(p["s5_d"])],
                                     [((t_len, 512), F32), ((t_len, 512), F32)], [((1, 512), F32)])
    small["s5_d"] = d_skip[0]
    s5 = s["s5"]
    dy_p = _perm(dy_s).astype(BF16)
    dlam_in = _s5_expand("s5_dx_in", dy_p, s5["w_y_t"])
    lam, da_re, da_im = _s5_scan("s5_scan_bwd", dlam_in, s5["a_re"], s5["a_im"], states=s["xs"], carry_in=s["cin"])
    du_s5 = _unperm(_s5_contract("s5_du", lam, s5["w_bu_t"]))
    d_w_bu = _s5_band_grad("s5_dbu", s["u_p"], lam, True)
    d_w_y = _s5_band_grad("s5_dc", dy_p, s["xs"], False)
    (seg["su"],) = _rowwise("s5_du_sum", lambda a, b: (a + b,), [du_s5, du_skip], [], [((t_len, 512), BF16)])
    d_bb_re = _band_diag_extract(d_w_bu[:4], 16, 64).transpose(0, 2, 1).reshape(S5_CH, S5_GROUP)
    d_bb_im = _band_diag_extract(d_w_bu[4:], 16, 64).transpose(0, 2, 1).reshape(S5_CH, S5_GROUP)
    small["s5_c_re"] = _band_diag_extract(d_w_y[:4], 64, 16).transpose(0, 2, 1)
    small["s5_c_im"] = -_band_diag_extract(d_w_y[4:], 64, 16).transpose(0, 2, 1)
    fl = s5["flat"]
    group_sum = jnp.repeat(jnp.eye(S5_GROUPS, dtype=F32), S5_STATE, axis=1)
    d_lr, d_li, d_ldt, d_br, d_bi = _s5_param_bwd(fl["lam_re"], fl["lam_im"], fl["log_dt"], fl["b_re"], fl["b_im"], group_sum,
                                                  da_re.reshape(S5_CH, 1), da_im.reshape(S5_CH, 1), d_bb_re, d_bb_im)
    small["s5_lam_re"], small["s5_lam_im"] = d_lr.reshape(32, 64), d_li.reshape(32, 64)
    small["s5_log_dt"] = d_ldt[:, 0]
    small["s5_b_re"], small["s5_b_im"] = d_br.reshape(32, 64, 16), d_bi.reshape(32, 64, 16)

    dqc, dkc, seg["rv"], seg["rg"] = _ret_bwd(s["qc"], s["kc"], proj, s["prevs"], dys[2])
    seg["rq"], seg["rk"] = _ret_unprep(dqc, dkc, cos64, sin64)

    do_h, delta = _mla_bwd_prep(s["ys"][3], dys[3])
    d_q, d_k, d_v = _mla_bwd(s["Q"], s["K"], s["V"], do_h, s["lse"], delta)
    dqf, dkvf, seg["kr"] = _mla_unprep(d_q, d_k, d_v, cos32, sin32)
    dcqn = _mm("mla_uq_dx", dqf, w["w_uq"], "nt")
    big["w_uq"] = _mm("mla_uq_dw", s["cqn"], dqf, "tn")
    dckvn = _mm("mla_ukv_dx", dkvf, w["w_ukv"], "nt")
    big["w_ukv"] = _mm("mla_ukv_dw", s["ckvn"], dkvf, "tn")

    def mla_norm_bwd(cq, ckv, d1, d2, gq, gkv):
        dcq, dgq = _rms_bwd(cq, gq, d1)
        dckv, dgkv = _rms_bwd(ckv, gkv, d2)
        return (dcq, dckv), (dgq, dgkv)
    seg["cq"], seg["ckv"], dgq, dgkv = _rowwise(
        "mla_norm_bwd", mla_norm_bwd, [_seg(proj, "cq"), _seg(proj, "ckv"), dcqn, dckvn],
        [_row(p["mla_g_q"]), _row(p["mla_g_kv"])], [((t_len, 256), BF16), ((t_len, 128), BF16)],
        [((1, 256), F32), ((1, 128), F32)])
    small["mla_g_q"], small["mla_g_kv"] = dgq[0], dgkv[0]

    dproj = jnp.concatenate([seg[name] for name in PROJ_ORDER[:-1]] + list(dgl), axis=1)
    dh1 = _mm("proj_dx", dproj, w["w_in"], "nt")
    big["w_in"] = _mm("proj_dw", s["h1"], dproj, "tn")

    def pre_mix_bwd(xt, dh, d1, g):
        dxn, dg = _rms_bwd(xt, g, dh)
        return (d1 + dxn,), (dg,)
    dx, small["g_pre_mix"] = _rowwise("pre_mix_bwd", pre_mix_bwd, [x, dh1, dx1], [_row(p["g_pre_mix"])],
                                      [((t_len, D_MODEL), F32)], [((1, D_MODEL), F32)])
    for name in ("g_pre_mix", "g_post_mix", "g_pre_mlp", "g_post_mlp"):
        small[name] = small[name][0]
    return dx, big, small


BIG = ("w_in", "s5_w_glu", "mla_w_uq", "mla_w_ukv", "w_branch", "w_out", "w_ff1", "w_ff2")
SMALL = ("g_pre_mix", "g_post_mix", "g_pre_mlp", "g_post_mlp", "swa_sinks", "s5_lam_re", "s5_lam_im", "s5_log_dt",
         "s5_b_re", "s5_b_im", "s5_c_re", "s5_c_im", "s5_d", "mla_g_q", "mla_g_kv")
WEIGHTS = ("g_pre_mix", "g_post_mix", "g_pre_mlp", "g_post_mlp", "w_in", "swa_sinks", "s5_lam_re", "s5_lam_im", "s5_log_dt",
           "s5_b_re", "s5_b_im", "s5_c_re", "s5_c_im", "s5_d", "s5_w_glu", "mla_g_q", "mla_g_kv", "mla_w_uq", "mla_w_ukv",
           "w_branch", "w_out", "w_ff1", "w_ff2")


def _layer_weights(gathered, l):
    g = {name: gathered[name][:, l] for name in BIG}
    wb = g["w_branch"].transpose(1, 2, 0, 3).reshape(4, 512, D_MODEL)
    return dict(w_in=_proj_cols(_cols_gathered(g["w_in"])), w_glu=_cols_gathered(g["s5_w_glu"]),
                w_uq=_uq_cols(_cols_gathered(g["mla_w_uq"])), w_ukv=_ukv_cols(_cols_gathered(g["mla_w_ukv"])),
                wb=[wb[b] for b in range(4)], w_out=g["w_out"].reshape(D_MODEL, D_MODEL),
                w_ff1=_cols_gathered(g["w_ff1"]), w_ff2=g["w_ff2"].reshape(4 * D_MODEL, D_MODEL))


def _grad_slabs(big):
    wb = jnp.stack(big["wb"])
    return dict(w_in=_cols_slabs(_unproj_cols(big["w_in"])), s5_w_glu=_cols_slabs(big["w_glu"]),
                mla_w_uq=_cols_slabs(_un_uq_cols(big["w_uq"])), mla_w_ukv=_cols_slabs(_un_ukv_cols(big["w_ukv"])),
                w_branch=wb.reshape(4, 512, N_DEV, D_MODEL // N_DEV).transpose(2, 0, 1, 3),
                w_out=big["w_out"].reshape(N_DEV, D_MODEL // N_DEV, D_MODEL),
                w_ff1=_cols_slabs(big["w_ff1"]), w_ff2=big["w_ff2"].reshape(N_DEV, 4 * D_MODEL // N_DEV, D_MODEL))


def kernel(x, g_pre_mix, g_post_mix, g_pre_mlp, g_post_mlp, w_in, swa_sinks, s5_lam_re, s5_lam_im, s5_log_dt, s5_b_re, s5_b_im, s5_c_re, s5_c_im, s5_d, s5_w_glu, mla_g_q, mla_g_kv, mla_w_uq, mla_w_ukv, w_branch, w_out, w_ff1, w_ff2, loss_target, m_g_pre_mix, m_g_post_mix, m_g_pre_mlp, m_g_post_mlp, m_w_in, m_swa_sinks, m_s5_lam_re, m_s5_lam_im, m_s5_log_dt, m_s5_b_re, m_s5_b_im, m_s5_c_re, m_s5_c_im, m_s5_d, m_s5_w_glu, m_mla_g_q, m_mla_g_kv, m_mla_w_uq, m_mla_w_ukv, m_w_branch, m_w_out, m_w_ff1, m_w_ff2, v_g_pre_mix, v_g_post_mix, v_g_pre_mlp, v_g_post_mlp, v_w_in, v_swa_sinks, v_s5_lam_re, v_s5_lam_im, v_s5_log_dt, v_s5_b_re, v_s5_b_im, v_s5_c_re, v_s5_c_im, v_s5_d, v_s5_w_glu, v_mla_g_q, v_mla_g_kv, v_mla_w_uq, v_mla_w_ukv, v_w_branch, v_w_out, v_w_ff1, v_w_ff2):
    weights = dict(g_pre_mix=g_pre_mix, g_post_mix=g_post_mix, g_pre_mlp=g_pre_mlp, g_post_mlp=g_post_mlp, w_in=w_in,
                   swa_sinks=swa_sinks, s5_lam_re=s5_lam_re, s5_lam_im=s5_lam_im, s5_log_dt=s5_log_dt, s5_b_re=s5_b_re,
                   s5_b_im=s5_b_im, s5_c_re=s5_c_re, s5_c_im=s5_c_im, s5_d=s5_d, s5_w_glu=s5_w_glu, mla_g_q=mla_g_q,
                   mla_g_kv=mla_g_kv, mla_w_uq=mla_w_uq, mla_w_ukv=mla_w_ukv, w_branch=w_branch, w_out=w_out, w_ff1=w_ff1,
                   w_ff2=w_ff2)
    m_in = dict(g_pre_mix=m_g_pre_mix, g_post_mix=m_g_post_mix, g_pre_mlp=m_g_pre_mlp, g_post_mlp=m_g_post_mlp, w_in=m_w_in,
                swa_sinks=m_swa_sinks, s5_lam_re=m_s5_lam_re, s5_lam_im=m_s5_lam_im, s5_log_dt=m_s5_log_dt,
                s5_b_re=m_s5_b_re, s5_b_im=m_s5_b_im, s5_c_re=m_s5_c_re, s5_c_im=m_s5_c_im, s5_d=m_s5_d,
                s5_w_glu=m_s5_w_glu, mla_g_q=m_mla_g_q, mla_g_kv=m_mla_g_kv, mla_w_uq=m_mla_w_uq, mla_w_ukv=m_mla_w_ukv,
                w_branch=m_w_branch, w_out=m_w_out, w_ff1=m_w_ff1, w_ff2=m_w_ff2)
    v_in = dict(g_pre_mix=v_g_pre_mix, g_post_mix=v_g_post_mix, g_pre_mlp=v_g_pre_mlp, g_post_mlp=v_g_post_mlp, w_in=v_w_in,
                swa_sinks=v_swa_sinks, s5_lam_re=v_s5_lam_re, s5_lam_im=v_s5_lam_im, s5_log_dt=v_s5_log_dt,
                s5_b_re=v_s5_b_re, s5_b_im=v_s5_b_im, s5_c_re=v_s5_c_re, s5_c_im=v_s5_c_im, s5_d=v_s5_d,
                s5_w_glu=v_s5_w_glu, mla_g_q=v_mla_g_q, mla_g_kv=v_mla_g_kv, mla_w_uq=v_mla_w_uq, mla_w_ukv=v_mla_w_ukv,
                w_branch=v_w_branch, w_out=v_w_out, w_ff1=v_w_ff1, w_ff2=v_w_ff2)
    depth = g_pre_mix.shape[0]
    t_len = x.shape[1]
    tabs = _rope_tables(t_len, 32) + _rope_tables(t_len, 16)

    recv = _gather_two_level("gather_weights", [weights[name].astype(BF16) for name in BIG])
    gathered = dict(zip(BIG, recv))

    xs, saved, layer_w, layer_p = [x[0]], [], [], []
    for l in range(depth):
        layer_w.append(_layer_weights(gathered, l))
        layer_p.append({name: weights[name][l] for name in SMALL})
        x_next, s = _layer_fwd(xs[-1], layer_w[l], layer_p[l], tabs)
        xs.append(x_next)
        saved.append(s)

    def loss_fn(y, tgt):
        err = y - tgt
        part = 0.5 * jnp.sum(jnp.mean(err * err, axis=-1, keepdims=True), axis=0, keepdims=True)
        return (err * (1.0 / D_MODEL),), (jnp.broadcast_to(part, (8, 128)),)
    dx, loss_part = _rowwise("loss", loss_fn, [xs[-1], loss_target[0]], [], [((t_len, D_MODEL), F32)], [((8, 128), F32)])
    loss = lax.psum(loss_part[0, 0], ("x", "y", "c"))

    big_grads, small_grads = [None] * depth, [None] * depth
    for l in reversed(range(depth)):
        dx, big, small = _layer_bwd(xs[l], dx, layer_w[l], layer_p[l], saved[l], tabs)
        big_grads[l], small_grads[l] = _grad_slabs(big), small

    small_vec = jnp.concatenate([jnp.stack([small_grads[l][name] for l in range(depth)]).reshape(-1) for name in SMALL])
    n_small = small_vec.shape[0]
    rows = -(-n_small // 1024) * 8
    small_mat = jnp.pad(small_vec, (0, rows * 128 - n_small)).reshape(rows, 128)
    core = lax.axis_index("c")
    own, send = [], []
    for name in BIG:
        slabs = jnp.stack([big_grads[l][name] for l in range(depth)], axis=1)
        slabs = slabs.reshape((4, 2) + slabs.shape[1:])
        own.append(lax.dynamic_index_in_dim(slabs, core, axis=1, keepdims=False))
        send.append(lax.dynamic_index_in_dim(slabs, 1 - core, axis=1, keepdims=False).astype(BF16))
    got = _pair_exchange("grads_to_sibling", send + [small_mat])
    chip_sums = []
    for name, mine, theirs in zip(BIG, own, got[:-1]):
        flat = (-1, mine.shape[-1])
        (part,) = _rowwise("grads_pair_sum", lambda a, b: (a + b.astype(F32),), [mine.reshape(flat), theirs.reshape(flat)], [],
                           [((int(np.prod(mine.shape[:-1])), mine.shape[-1]), BF16)])
        chip_sums.append(part.reshape(mine.shape))
    small_pair = jnp.where(core == 0, jnp.stack([small_mat, got[-1]]), jnp.stack([got[-1], small_mat]))
    recv = _chip_exchange("grads_to_chips", [(p, True) for p in chip_sums] + [(small_pair, False)])
    small_sum = _sum_slabs("sum_small_grads", recv[-1].reshape((N_DEV,) + small_mat.shape)).reshape(-1)

    grads, deltas, new_m, new_v = {}, {}, {}, {}
    for name, slabs in zip(BIG, recv[:-1]):
        grads[name], deltas[name], new_m[name], new_v[name] = _adamw("adamw_" + name, weights[name], slabs, m_in[name],
                                                                     v_in[name], 4)
    off = 0
    for name in SMALL:
        size = int(np.prod(weights[name].shape))
        g = small_sum[off:off + size].reshape(weights[name].shape)
        off += size
        grads[name], deltas[name], new_m[name], new_v[name] = _adamw("adamw_" + name, weights[name], g, m_in[name],
                                                                     v_in[name], False)
    return (loss, dx[None], *[grads[n] for n in WEIGHTS], *[deltas[n] for n in WEIGHTS], *[new_m[n] for n in WEIGHTS],
            *[new_v[n] for n in WEIGHTS])
```

```python
import functools
import math

import numpy as np
import jax
import jax.numpy as jnp
from jax import lax
from jax.experimental import pallas as pl
from jax.experimental.pallas import tpu as pltpu

F32, BF16 = jnp.float32, jnp.bfloat16
EPS = 1e-6
NEG_INF = -1e30
ROPE_THETA = 10000.0
D_MODEL = 1024
N_DEV = 8
BLOCK = 128
SWA_HEADS, SWA_KV = 8, 2
RET_HEADS, RET_DK, RET_DV = 4, 64, 128
MLA_HEADS = 8
S5_GROUPS, S5_STATE, S5_GROUP = 32, 64, 16
S5_CH = S5_GROUPS * S5_STATE
SCAN_LANES = 256
SCAN_SEG = 8
ADAM_LR, ADAM_B1, ADAM_B2, ADAM_EPS, ADAM_WD, ADAM_STEP = 0.001, 0.9, 0.999, 1e-08, 0.01, 10
VMEM_MB = 56

PROJ = dict(sq=(0, 512), su=(512, 512), rv=(1024, 512), rg=(1536, 512), rq=(2048, 256), rk=(2304, 256),
            cq=(2560, 256), sk=(2816, 128), sv=(2944, 128), ckv=(3072, 128), kr=(3200, 128), gates=(3328, 4096))
PROJ_W = 7424
REF_SPLIT = dict(sq=(0, 512), sk=(512, 128), sv=(640, 128), su=(768, 512), rq=(1280, 256), rk=(1536, 256),
                 rv=(1792, 512), rg=(2304, 512), cq=(2816, 256), ckv=(3072, 128), kr=(3200, 32), gates=(3232, 4096))
PROJ_ORDER = ("sq", "su", "rv", "rg", "rq", "rk", "cq", "sk", "sv", "ckv", "kr", "gates")
REF_ORDER = ("sq", "sk", "sv", "su", "rq", "rk", "rv", "rg", "cq", "ckv", "kr", "gates")

NN = (((1,), (0,)), ((), ()))
NT = (((1,), (1,)), ((), ()))
TN = (((0,), (0,)), ((), ()))


def _params(sem):
    return pltpu.CompilerParams(dimension_semantics=sem, vmem_limit_bytes=VMEM_MB * 2 ** 20)


def _raw_dot(a, b, dims):
    return lax.dot_general(a.astype(BF16), b.astype(BF16), dims, preferred_element_type=F32)


@jax.custom_vjp
def _dnn(a, b):
    return _raw_dot(a, b, NN)


@jax.custom_vjp
def _dnt(a, b):
    return _raw_dot(a, b, NT)


@jax.custom_vjp
def _dtn(a, b):
    return _raw_dot(a, b, TN)


_dnn.defvjp(lambda a, b: (_dnn(a, b), (a, b)),
            lambda r, g: (_dnt(g, r[1]).astype(r[0].dtype), _dtn(r[0], g).astype(r[1].dtype)))
_dnt.defvjp(lambda a, b: (_dnt(a, b), (a, b)),
            lambda r, g: (_dnn(g, r[1]).astype(r[0].dtype), _dtn(g, r[0]).astype(r[1].dtype)))
_dtn.defvjp(lambda a, b: (_dtn(a, b), (a, b)),
            lambda r, g: (_dnt(r[1], g).astype(r[0].dtype), _dnn(r[0], g).astype(r[1].dtype)))


class _Cols:
    def __init__(self, arr, width, index):
        self.arr, self.width, self.index = arr, width, index


def _seg(proj, name):
    off, width = PROJ[name]
    return _Cols(proj, width, off // width)


def _row_spec(shape, tile):
    lead = len(shape) - 2
    return pl.BlockSpec(tuple(shape[:-2]) + (tile, shape[-1]), lambda i, lead=lead: (0,) * lead + (i, 0))


def _const_spec(shape):
    nd = len(shape)
    return pl.BlockSpec(tuple(shape), lambda *_, nd=nd: (0,) * nd)


def _rowwise(name, fn, rows, consts, out_rows, out_accs=(), tile=256):
    arrs = [r.arr if isinstance(r, _Cols) else r for r in rows]
    t_len = arrs[0].shape[-2]
    tile = min(tile, t_len)
    n = t_len // tile
    in_specs = []
    for r in rows:
        if isinstance(r, _Cols):
            in_specs.append(pl.BlockSpec((tile, r.width), lambda i, c=r.index: (i, c)))
        else:
            in_specs.append(_row_spec(r.shape, tile))
    in_specs += [_const_spec(c.shape) for c in consts]
    out_specs = [_row_spec(s, tile) for s, _ in out_rows] + [_const_spec(s) for s, _ in out_accs]
    out_shape = [jax.ShapeDtypeStruct(s, d) for s, d in tuple(out_rows) + tuple(out_accs)]
    n_in, n_or = len(rows) + len(consts), len(out_rows)

    def body(*refs):
        res = fn(*[r[...] for r in refs[:n_in]])
        row_out, acc_out = res if out_accs else (res, ())
        for ref, val in zip(refs[n_in:n_in + n_or], row_out):
            if isinstance(val, (list, tuple)):
                for h, v_h in enumerate(val):
                    ref[h] = v_h.astype(ref.dtype)
            else:
                ref[...] = val.astype(ref.dtype)
        first = pl.program_id(0) == 0
        for ref, val in zip(refs[n_in + n_or:], acc_out):
            @pl.when(first)
            def _(ref=ref, val=val):
                ref[...] = val

            @pl.when(jnp.logical_not(first))
            def _(ref=ref, val=val):
                ref[...] += val

    return pl.pallas_call(body, name=name, grid=(n,), in_specs=in_specs, out_specs=out_specs, out_shape=out_shape,
                          compiler_params=_params(("arbitrary",)))(*arrs, *consts)


MM_VMEM_BUDGET = 44 * 2 ** 20


def _pick_tn(n, io_col_bytes, tmp_col_bytes, fixed_bytes):
    best = None
    for cand in range(128, n + 1, 128):
        if n % cand == 0 and 2 * (fixed_bytes + cand * io_col_bytes) + cand * tmp_col_bytes <= MM_VMEM_BUDGET:
            best = cand
    return best if best is not None else min(n, 128)


def _mm(name, a, b, mode, out_dtypes=(F32,), tm=512, tn=None, epi=None, epi_in=()):
    if mode == "nn":
        (m, k), n = a.shape, b.shape[1]
    elif mode == "nt":
        (m, k), n = a.shape, b.shape[0]
    else:
        (k, m), n = a.shape, b.shape[1]
    tm = min(tm, m)
    if tn is None:
        io_col = k * b.dtype.itemsize + tm * (sum(jnp.dtype(d).itemsize for d in out_dtypes) + sum(e.dtype.itemsize for e in epi_in))
        tmp_col = tm * 4 * (2 if epi is not None else 1)
        tn = _pick_tn(n, io_col, tmp_col, tm * k * a.dtype.itemsize)
    tn = min(tn, n)
    assert m % tm == 0 and n % tn == 0, (name, m, n, tm, tn)
    a_spec = pl.BlockSpec((k, tm), lambda i, j: (0, i)) if mode == "tn" else pl.BlockSpec((tm, k), lambda i, j: (i, 0))
    b_spec = pl.BlockSpec((tn, k), lambda i, j: (j, 0)) if mode == "nt" else pl.BlockSpec((k, tn), lambda i, j: (0, j))
    dims = dict(nn=NN, nt=NT, tn=TN)[mode]
    o_spec = pl.BlockSpec((tm, tn), lambda i, j: (i, j))
    n_e = len(epi_in)

    def body(a_ref, b_ref, *refs):
        acc = _raw_dot(a_ref[...], b_ref[...], dims)
        outs = epi(acc, *[r[...] for r in refs[:n_e]]) if epi is not None else (acc,)
        for ref, val in zip(refs[n_e:], outs):
            ref[...] = val.astype(ref.dtype)

    res = pl.pallas_call(body, name=name, grid=(m // tm, n // tn),
                         in_specs=[a_spec, b_spec] + [o_spec] * n_e, out_specs=[o_spec] * len(out_dtypes),
                         out_shape=[jax.ShapeDtypeStruct((m, n), d) for d in out_dtypes],
                         compiler_params=_params(("parallel", "parallel")))(a, b, *epi_in)
    return res[0] if len(out_dtypes) == 1 else res


def _rms(x, g):
    return x * lax.rsqrt(jnp.mean(x * x, axis=-1, keepdims=True) + EPS) * g


def _rms_bwd(x, g, dy):
    _, vjp = jax.vjp(_rms, x, g)
    return vjp(dy)


def _rope_tables(t_len, half):
    dim = 2 * half
    inv = 1.0 / (ROPE_THETA ** (jnp.arange(0, dim, 2, dtype=F32) / dim))
    ang = jnp.arange(t_len, dtype=F32)[:, None] * inv[None, :]
    c, s = jnp.cos(ang), jnp.sin(ang)
    reps = 128 // dim
    return jnp.tile(jnp.concatenate([c, c], 1), (1, reps)), jnp.tile(jnp.concatenate([-s, s], 1), (1, reps))


def _rope(x, cos, sin, half):
    w = x.shape[-1]
    reps = w // 128
    if reps > 1:
        cos, sin = jnp.tile(cos, (1, reps)), jnp.tile(sin, (1, reps))
    lane = lax.broadcasted_iota(jnp.int32, x.shape, 1)
    partner = jnp.where((lane % (2 * half)) < half, pltpu.roll(x, w - half, 1), pltpu.roll(x, half, 1))
    return x * cos + partner * sin


SWA_GROUP = SWA_HEADS // SWA_KV
SWA_QB = 4
SWA_TB_FWD, SWA_TB_BWD = 4, 2


def _swa_mask(tb, first_block):
    rows, keys = SWA_GROUP * tb * BLOCK, (tb + 1) * BLOCK
    qi = lax.broadcasted_iota(jnp.int32, (rows, keys), 0) % (tb * BLOCK) + BLOCK
    kj = lax.broadcasted_iota(jnp.int32, (rows, keys), 1)
    diff = qi - kj
    band = (diff >= 0) & (diff < BLOCK)
    return band if first_block is False else band & (jnp.logical_not(first_block) | (kj >= BLOCK))


def _swa_stack(ref, j, b, tb, dtype=None):
    rows = slice(BLOCK * b, BLOCK * (b + tb))
    parts = [ref[rows, 64 * (SWA_GROUP * j + g):64 * (SWA_GROUP * j + g) + 64] for g in range(SWA_GROUP)]
    out = jnp.concatenate(parts, axis=0)
    return out if dtype is None else out.astype(dtype)


def _swa_sinks(s_ref, j, tb):
    return jnp.concatenate([jnp.broadcast_to(s_ref[SWA_GROUP * j + g:SWA_GROUP * j + g + 1, 0:1], (tb * BLOCK, 1))
                            for g in range(SWA_GROUP)], axis=0)


def _swa_head(q, kw, vw, sink, mask):
    sc = _dnt(q, kw) * (64 ** -0.5)
    sc = jnp.where(mask, sc, NEG_INF)
    m = lax.stop_gradient(jnp.maximum(jnp.max(sc, axis=-1, keepdims=True), sink))
    p = jnp.exp(sc - m)
    denom = jnp.sum(p, axis=-1, keepdims=True) + jnp.exp(sink - m)
    return _dnn(p / denom, vw)


def _swa_prep(proj, cos, sin):
    def fn(sq, sk, sv, c, s):
        return _rope(sq, c, s, 32), _rope(sk, c, s, 32), sv
    t_len = proj.shape[0]
    return _rowwise("swa_prep", fn, [_seg(proj, "sq"), _seg(proj, "sk"), _seg(proj, "sv"), cos, sin], [],
                    [((t_len, 512), BF16), ((t_len, 128), BF16), ((t_len, 128), BF16)])


def _swa_specs(t_len, tb_max=1):
    qb = min(SWA_QB, t_len // BLOCK)
    tb = min(tb_max, qb)
    rows = qb * BLOCK
    cur = lambda w: pl.BlockSpec((rows, w), lambda i: (i, 0))
    prev = pl.BlockSpec((BLOCK, 128), lambda i: (jnp.maximum(qb * i - 1, 0), 0))
    return qb, tb, t_len // rows, cur, prev


def _swa_fwd(q, k, v, sinks):
    t_len = q.shape[0]
    qb, tb, steps, cur, prev = _swa_specs(t_len, SWA_TB_FWD)

    def body(q_ref, kp_ref, kc_ref, vp_ref, vc_ref, s_ref, o_ref):
        first = pl.program_id(0) == 0
        k_all = jnp.concatenate([kp_ref[...], kc_ref[...]], axis=0)
        v_all = jnp.concatenate([vp_ref[...], vc_ref[...]], axis=0)
        for j in range(SWA_KV):
            cols = slice(64 * j, 64 * j + 64)
            sink = _swa_sinks(s_ref, j, tb)
            for b in range(0, qb, tb):
                window = slice(BLOCK * b, BLOCK * (b + tb + 1))
                mask = _swa_mask(tb, first if b == 0 else False)
                o = _swa_head(_swa_stack(q_ref, j, b, tb), k_all[window, cols], v_all[window, cols], sink, mask)
                for g in range(SWA_GROUP):
                    h = SWA_GROUP * j + g
                    o_ref[BLOCK * b:BLOCK * (b + tb), 64 * h:64 * h + 64] = o[tb * BLOCK * g:tb * BLOCK * (g + 1)].astype(o_ref.dtype)

    return pl.pallas_call(
        body, name="swa_fwd", grid=(steps,),
        in_specs=[cur(512), prev, cur(128), prev, cur(128), _const_spec((8, 128))],
        out_specs=cur(512), out_shape=jax.ShapeDtypeStruct((t_len, 512), BF16),
        compiler_params=_params(("parallel",)))(q, k, k, v, v, sinks)


def _swa_bwd(q, k, v, sinks, dy):
    t_len = q.shape[0]
    qb, tb, steps, cur, prev = _swa_specs(t_len, SWA_TB_BWD)

    def body(q_ref, kp_ref, kc_ref, vp_ref, vc_ref, s_ref, dy_ref, dq_ref, dk_ref, dv_ref, dkp_ref, dvp_ref, ds_ref):
        first = pl.program_id(0) == 0

        @pl.when(first)
        def _():
            ds_ref[...] = jnp.zeros_like(ds_ref)

        k_all = jnp.concatenate([kp_ref[...], kc_ref[...]], axis=0)
        v_all = jnp.concatenate([vp_ref[...], vc_ref[...]], axis=0)
        for j in range(SWA_KV):
            cols = slice(64 * j, 64 * j + 64)
            sink = _swa_sinks(s_ref, j, tb)
            dk_parts = [jnp.zeros((BLOCK, 64), F32) for _ in range(qb + 1)]
            dv_parts = [jnp.zeros((BLOCK, 64), F32) for _ in range(qb + 1)]
            dsink = jnp.zeros((SWA_GROUP * tb * BLOCK, 1), F32)
            for b in range(0, qb, tb):
                window = slice(BLOCK * b, BLOCK * (b + tb + 1))
                mask = _swa_mask(tb, first if b == 0 else False)
                _, vjp = jax.vjp(lambda a, b_, c, d, mask=mask: _swa_head(a, b_, c, d, mask), _swa_stack(q_ref, j, b, tb),
                                 k_all[window, cols], v_all[window, cols], sink)
                dq, dkw, dvw, dsink_b = vjp(_swa_stack(dy_ref, j, b, tb, F32))
                dkw, dvw = dkw.astype(F32), dvw.astype(F32)
                for r in range(tb + 1):
                    dk_parts[b + r] = dk_parts[b + r] + dkw[BLOCK * r:BLOCK * (r + 1)]
                    dv_parts[b + r] = dv_parts[b + r] + dvw[BLOCK * r:BLOCK * (r + 1)]
                dsink = dsink + dsink_b
                for g in range(SWA_GROUP):
                    h = SWA_GROUP * j + g
                    dq_ref[BLOCK * b:BLOCK * (b + tb), 64 * h:64 * h + 64] = dq[tb * BLOCK * g:tb * BLOCK * (g + 1)].astype(F32)
            dkp_ref[:, cols], dvp_ref[:, cols] = dk_parts[0], dv_parts[0]
            for b in range(qb):
                dk_ref[BLOCK * b:BLOCK * b + BLOCK, cols] = dk_parts[b + 1]
                dv_ref[BLOCK * b:BLOCK * b + BLOCK, cols] = dv_parts[b + 1]
            for g in range(SWA_GROUP):
                h = SWA_GROUP * j + g
                ds_ref[h:h + 1, :] += jnp.broadcast_to(jnp.sum(dsink[tb * BLOCK * g:tb * BLOCK * (g + 1)], axis=0, keepdims=True), (1, 128))

    part = pl.BlockSpec((BLOCK, 128), lambda i: (i, 0))
    return pl.pallas_call(
        body, name="swa_bwd", grid=(steps,),
        in_specs=[cur(512), prev, cur(128), prev, cur(128), _const_spec((8, 128)), cur(512)],
        out_specs=[cur(512), cur(128), cur(128), part, part, _const_spec((8, 128))],
        out_shape=[jax.ShapeDtypeStruct((t_len, 512), F32), jax.ShapeDtypeStruct((t_len, 128), F32),
                   jax.ShapeDtypeStruct((t_len, 128), F32), jax.ShapeDtypeStruct((steps * BLOCK, 128), F32),
                   jax.ShapeDtypeStruct((steps * BLOCK, 128), F32), jax.ShapeDtypeStruct((8, 128), F32)],
        compiler_params=_params(("arbitrary",)))(q, k, k, v, v, sinks, dy)


def _swa_unprep(dq, dk, dv, dk_prev, dv_prev, cos, sin):
    t_len = dq.shape[0]
    qb, _, steps, cur, _ = _swa_specs(t_len)
    tail = (qb - 1) * BLOCK

    def body(dq_ref, dk_ref, dv_ref, dkn_ref, dvn_ref, c_ref, s_ref, oq_ref, ok_ref, ov_ref):
        more = (pl.program_id(0) < steps - 1).astype(F32)
        cos_t, sin_t = c_ref[...], -s_ref[...]
        oq_ref[...] = _rope(dq_ref[...], cos_t, sin_t, 32).astype(BF16)
        dk_t = jnp.concatenate([dk_ref[:tail], dk_ref[tail:] + more * dkn_ref[...]], axis=0) if tail else dk_ref[...] + more * dkn_ref[...]
        dv_t = jnp.concatenate([dv_ref[:tail], dv_ref[tail:] + more * dvn_ref[...]], axis=0) if tail else dv_ref[...] + more * dvn_ref[...]
        ok_ref[...] = _rope(dk_t, cos_t, sin_t, 32).astype(BF16)
        ov_ref[...] = dv_t.astype(BF16)

    nxt = pl.BlockSpec((BLOCK, 128), lambda i: (jnp.minimum(i + 1, steps - 1), 0))
    return pl.pallas_call(
        body, name="swa_unprep", grid=(steps,),
        in_specs=[cur(512), cur(128), cur(128), nxt, nxt, cur(128), cur(128)],
        out_specs=[cur(512), cur(128), cur(128)],
        out_shape=[jax.ShapeDtypeStruct((t_len, 512), BF16), jax.ShapeDtypeStruct((t_len, 128), BF16),
                   jax.ShapeDtypeStruct((t_len, 128), BF16)],
        compiler_params=_params(("parallel",)))(dq, dk, dv, dk_prev, dv_prev, cos, sin)


def _s5_discretize(lam_re, lam_im, log_dt, b_re, b_im):
    dt = jnp.exp(log_dt)
    mag = jnp.exp(lam_re * dt)
    ab_re, ab_im = mag * jnp.cos(lam_im * dt), mag * jnp.sin(lam_im * dt)
    den = lam_re * lam_re + lam_im * lam_im
    nr, ni = ab_re - 1.0, ab_im
    f_re = (nr * lam_re + ni * lam_im) / den
    f_im = (ni * lam_re - nr * lam_im) / den
    return ab_re, ab_im, f_re * b_re - f_im * b_im, f_re * b_im + f_im * b_re


def _s5_param_fwd(lam_re, lam_im, log_dt, b_re, b_im):
    def body(*refs):
        outs = _s5_discretize(*[r[...] for r in refs[:5]])
        for ref, val in zip(refs[5:], outs):
            ref[...] = val
    n = S5_CH
    return pl.pallas_call(body, name="s5_param_fwd",
                          out_shape=[jax.ShapeDtypeStruct((n, 1), F32)] * 2 + [jax.ShapeDtypeStruct((n, 16), F32)] * 2,
                          )(lam_re, lam_im, log_dt, b_re, b_im)


def _s5_param_bwd(lam_re, lam_im, log_dt, b_re, b_im, group_sum, d_ab_re, d_ab_im, d_bb_re, d_bb_im):
    def body(*refs):
        ins = [r[...] for r in refs[:10]]
        _, vjp = jax.vjp(_s5_discretize, *ins[:5])
        d_lr, d_li, d_ldt, d_br, d_bi = vjp(tuple(ins[6:10]))
        o_lr, o_li, o_ldt, o_br, o_bi = refs[10:]
        o_lr[...], o_li[...], o_br[...], o_bi[...] = d_lr, d_li, d_br, d_bi
        o_ldt[...] = lax.dot_general(ins[5], jnp.broadcast_to(d_ldt, (S5_CH, 128)), NN,
                                     precision=lax.Precision.HIGHEST, preferred_element_type=F32)
    n = S5_CH
    return pl.pallas_call(body, name="s5_param_bwd",
                          out_shape=[jax.ShapeDtypeStruct((n, 1), F32), jax.ShapeDtypeStruct((n, 1), F32),
                                     jax.ShapeDtypeStruct((S5_GROUPS, 128), F32),
                                     jax.ShapeDtypeStruct((n, 16), F32), jax.ShapeDtypeStruct((n, 16), F32)],
                          )(lam_re, lam_im, log_dt, b_re, b_im, group_sum, d_ab_re, d_ab_im, d_bb_re, d_bb_im)


def _s5_expand(name, u, w):
    t_len = u.shape[0]
    tile = min(512, t_len)

    def body(u_ref, w_ref, o_ref):
        o_ref[...] = _raw_dot(u_ref[...], w_ref[...], NN)

    return pl.pallas_call(
        body, name=name, grid=(t_len // tile, 8),
        in_specs=[pl.BlockSpec((tile, 128), lambda i, bb: (i, bb % 4)), pl.BlockSpec((None, 128, 512), lambda i, bb: (bb, 0, 0))],
        out_specs=pl.BlockSpec((None, tile, 512), lambda i, bb: (bb // 4, i, bb % 4)),
        out_shape=jax.ShapeDtypeStruct((2, t_len, S5_CH), F32), compiler_params=_params(("parallel", "parallel")))(u, w)


def _s5_contract(name, x, w):
    t_len = x.shape[1]
    tile = min(512, t_len)

    def body(xr_ref, xi_ref, wr_ref, wi_ref, o_ref):
        o_ref[...] = _raw_dot(xr_ref[...], wr_ref[...], NN) + _raw_dot(xi_ref[...], wi_ref[...], NN)

    return pl.pallas_call(
        body, name=name, grid=(t_len // tile, 4),
        in_specs=[pl.BlockSpec((None, tile, 512), lambda i, b: (0, i, b)), pl.BlockSpec((None, tile, 512), lambda i, b: (1, i, b)),
                  pl.BlockSpec((None, 512, 128), lambda i, b: (b, 0, 0)), pl.BlockSpec((None, 512, 128), lambda i, b: (4 + b, 0, 0))],
        out_specs=pl.BlockSpec((tile, 128), lambda i, b: (i, b)),
        out_shape=jax.ShapeDtypeStruct((t_len, 512), F32), compiler_params=_params(("parallel", "parallel")))(x, x, w, w)


def _s5_band_grad(name, narrow, wide, narrow_first):
    t_len = narrow.shape[0]

    def body(n_ref, w_ref, o_ref):
        if narrow_first:
            o_ref[...] = _raw_dot(n_ref[...], w_ref[...], TN)
        else:
            o_ref[...] = _raw_dot(w_ref[...], n_ref[...], TN)

    shape = (8, 128, 512) if narrow_first else (8, 512, 128)
    return pl.pallas_call(
        body, name=name, grid=(8,),
        in_specs=[pl.BlockSpec((t_len, 128), lambda bb: (0, bb % 4)), pl.BlockSpec((None, t_len, 512), lambda bb: (bb // 4, 0, bb % 4))],
        out_specs=pl.BlockSpec((None,) + shape[1:], lambda bb: (bb, 0, 0)),
        out_shape=jax.ShapeDtypeStruct(shape, F32), compiler_params=_params(("parallel",)))(narrow, wide)


def _s5_scan(name, bu, a_re, a_im, states=None, carry_in=None):
    t_len = bu.shape[1]
    n = t_len // SCAN_SEG
    log_n = int(round(math.log2(n)))
    assert 2 ** log_n == n
    reverse = states is not None
    lanes = SCAN_LANES // 2 if reverse else SCAN_LANES
    nblk = S5_CH // lanes

    def body(*refs):
        if reverse:
            b_ref, ar_ref, ai_ref, x_ref, cin_ref, o_ref, dar_ref, dai_ref, c_scr = refs
            xr_ref, xi_ref, cr_ref, ci_ref = x_ref.at[0], x_ref.at[1], cin_ref.at[0], cin_ref.at[1]
        else:
            b_ref, ar_ref, ai_ref, o_ref, co_ref, c_scr = refs
            cor_ref, coi_ref = co_ref.at[0], co_ref.at[1]
        br_ref, bi_ref, or_ref, oi_ref = b_ref.at[0], b_ref.at[1], o_ref.at[0], o_ref.at[1]
        ar = ar_ref[...]
        ai = -ai_ref[...] if reverse else ai_ref[...]

        def tile_at(i):
            return pl.multiple_of((n - 1 - i if reverse else i) * SCAN_SEG, SCAN_SEG)

        def advance(s, row):
            sr, si = s
            return (ar * sr - ai * si + br_ref[pl.ds(row, SCAN_SEG), :], ar * si + ai * sr + bi_ref[pl.ds(row, SCAN_SEG), :])

        zero = jnp.zeros((SCAN_SEG, lanes), F32)
        fin_r, fin_i = lax.fori_loop(0, n, lambda i, s: advance(s, tile_at(i)), (zero, zero), unroll=8)
        pr, pi = ar[0:1], ai[0:1]
        for _ in range(log_n):
            pr, pi = pr * pr - pi * pi, 2.0 * pr * pi
        er = ei = jnp.zeros((1, lanes), F32)
        for step in range(SCAN_SEG):
            j = SCAN_SEG - 1 - step if reverse else step
            c_scr[0, j:j + 1, :] = er
            c_scr[1, j:j + 1, :] = ei
            er, ei = pr * er - pi * ei + fin_r[j:j + 1], pr * ei + pi * er + fin_i[j:j + 1]
        entry = (c_scr[0], c_scr[1])

        if not reverse:
            cor_ref[...], coi_ref[...] = entry

            def step2(i, s):
                row = tile_at(i)
                s = advance(s, row)
                or_ref[pl.ds(row, SCAN_SEG), :] = s[0]
                oi_ref[pl.ds(row, SCAN_SEG), :] = s[1]
                return s
            lax.fori_loop(0, n, step2, entry, unroll=8)
        else:
            def emit(s, row, xr_prev, xi_prev, acc):
                s = advance(s, row)
                or_ref[pl.ds(row, SCAN_SEG), :] = s[0]
                oi_ref[pl.ds(row, SCAN_SEG), :] = s[1]
                return s, (acc[0] + s[0] * xr_prev + s[1] * xi_prev, acc[1] + s[1] * xr_prev - s[0] * xi_prev)

            def step2(i, carry):
                s, acc = carry
                row = tile_at(i)
                prev = pl.multiple_of(row - SCAN_SEG, SCAN_SEG)
                return emit(s, row, xr_ref[pl.ds(prev, SCAN_SEG), :], xi_ref[pl.ds(prev, SCAN_SEG), :], acc)
            s, acc = lax.fori_loop(0, n - 1, step2, (entry, (zero, zero)), unroll=8)
            s, acc = emit(s, 0, cr_ref[...], ci_ref[...], acc)
            dar_ref[...] = jnp.sum(acc[0], axis=0, keepdims=True)
            dai_ref[...] = jnp.sum(acc[1], axis=0, keepdims=True)

    planes = pl.BlockSpec((2, t_len, lanes), lambda b: (0, 0, b))
    tile8 = pl.BlockSpec((SCAN_SEG, lanes), lambda b: (0, b))
    entry_spec = pl.BlockSpec((2, SCAN_SEG, lanes), lambda b: (0, 0, b))
    row1 = pl.BlockSpec((1, lanes), lambda b: (0, b))
    seq = jax.ShapeDtypeStruct((2, t_len, S5_CH), F32)
    if reverse:
        in_specs = [planes, tile8, tile8, planes, entry_spec]
        args = (bu, a_re, a_im, states, carry_in)
        out_specs = [planes, row1, row1]
        out_shape = [seq, jax.ShapeDtypeStruct((1, S5_CH), F32), jax.ShapeDtypeStruct((1, S5_CH), F32)]
    else:
        in_specs = [planes, tile8, tile8]
        args = (bu, a_re, a_im)
        out_specs = [planes, entry_spec]
        out_shape = [seq, jax.ShapeDtypeStruct((2, SCAN_SEG, S5_CH), F32)]
    return pl.pallas_call(body, name=name, grid=(nblk,), in_specs=in_specs, out_specs=out_specs, out_shape=out_shape,
                          scratch_shapes=[pltpu.VMEM((2, SCAN_SEG, lanes), F32)],
                          compiler_params=_params(("parallel",)))(*args)


def _ret_consts():
    h = np.arange(RET_HEADS, dtype=np.float32)
    log_gamma = np.log1p(-np.exp2(-5.0 - h)).astype(np.float32)
    idx = np.arange(BLOCK, dtype=np.float32)
    diff = idx[:, None] - idx[None, :]
    decay = np.where(diff >= 0, np.exp(log_gamma[:, None, None] * np.maximum(diff, 0.0)), 0.0).astype(np.float32)
    k_w = np.exp(log_gamma[:, None] * (BLOCK - 1 - idx)[None, :]).astype(np.float32)[:, :, None]
    q_w = np.exp(log_gamma[:, None] * (idx + 1.0)[None, :]).astype(np.float32)[:, :, None]
    chunk_decay = [float(v) for v in np.exp(log_gamma * BLOCK).astype(np.float32)]
    return jnp.asarray(decay), jnp.asarray(q_w), jnp.asarray(k_w), chunk_decay


def _ret_head(q, k, v, g, prev, decay, q_w, k_w, chunk_decay):
    inner_s = _dnt(q, k) * decay
    y = _dnn(inner_s, v) + _dnn(q * q_w, prev)
    mu = jnp.mean(y, axis=-1, keepdims=True)
    var = jnp.mean(jnp.square(y - mu), axis=-1, keepdims=True)
    yn = (y - mu) * lax.rsqrt(var + EPS)
    out = g * jax.nn.sigmoid(g) * yn
    return out, prev * chunk_decay + _dtn(k * k_w, v)


def _ret_prep(proj, cos, sin):
    def fn(rq, rk, c, s):
        return _rope(rq, c, s, 32), _rope(rk, c, s, 32) * (RET_DK ** -0.5)
    t_len = proj.shape[0]
    return _rowwise("ret_prep", fn, [_seg(proj, "rq"), _seg(proj, "rk"), cos, sin], [],
                    [((t_len, 256), F32), ((t_len, 256), F32)])


def _ret_unprep(dq, dk, cos, sin):
    def fn(dq_t, dk_t, c, s):
        return _rope(dq_t, c, -s, 32), _rope(dk_t * (RET_DK ** -0.5), c, -s, 32)
    t_len = dq.shape[0]
    return _rowwise("ret_unprep", fn, [dq, dk, cos, sin], [], [((t_len, 256), BF16), ((t_len, 256), BF16)])


def _ret_fwd(q, k, proj):
    t_len = q.shape[0]
    nc = t_len // BLOCK
    decay, q_w, k_w, chunk_decay = _ret_consts()

    def body(q_ref, k_ref, v_ref, g_ref, dec_ref, qw_ref, kw_ref, o_ref, prev_ref, state):
        @pl.when(pl.program_id(0) == 0)
        def _():
            state[...] = jnp.zeros_like(state)
        for h in range(RET_HEADS):
            qk, vv = slice(64 * h, 64 * h + 64), slice(128 * h, 128 * h + 128)
            prev = state[h]
            prev_ref[h] = prev
            out, nxt = _ret_head(q_ref[:, qk], k_ref[:, qk], v_ref[:, vv], g_ref[:, vv], prev,
                                 dec_ref[h], qw_ref[h], kw_ref[h], chunk_decay[h])
            o_ref[:, vv] = out.astype(o_ref.dtype)
            state[h] = nxt

    chunk = lambda w, c=0: pl.BlockSpec((BLOCK, w), lambda i, c=c: (i, c))
    return pl.pallas_call(
        body, name="ret_fwd", grid=(nc,),
        in_specs=[chunk(256), chunk(256), chunk(512, PROJ["rv"][0] // 512), chunk(512, PROJ["rg"][0] // 512),
                  _const_spec(decay.shape), _const_spec(q_w.shape), _const_spec(k_w.shape)],
        out_specs=[chunk(512), pl.BlockSpec((None, RET_HEADS, RET_DK, RET_DV), lambda i: (i, 0, 0, 0))],
        out_shape=[jax.ShapeDtypeStruct((t_len, 512), BF16), jax.ShapeDtypeStruct((nc, RET_HEADS, RET_DK, RET_DV), F32)],
        scratch_shapes=[pltpu.VMEM((RET_HEADS, RET_DK, RET_DV), F32)],
        compiler_params=_params(("arbitrary",)))(q, k, proj, proj, decay, q_w, k_w)


def _ret_bwd(q, k, proj, prevs, dy):
    t_len = q.shape[0]
    nc = t_len // BLOCK
    decay, q_w, k_w, chunk_decay = _ret_consts()

    def body(q_ref, k_ref, v_ref, g_ref, prev_ref, dy_ref, dec_ref, qw_ref, kw_ref, dq_ref, dk_ref, dv_ref, dg_ref, dstate):
        @pl.when(pl.program_id(0) == 0)
        def _():
            dstate[...] = jnp.zeros_like(dstate)
        for h in range(RET_HEADS):
            qk, vv = slice(64 * h, 64 * h + 64), slice(128 * h, 128 * h + 128)
            head = functools.partial(_ret_head, decay=dec_ref[h], q_w=qw_ref[h], k_w=kw_ref[h], chunk_decay=chunk_decay[h])
            _, vjp = jax.vjp(head, q_ref[:, qk], k_ref[:, qk], v_ref[:, vv], g_ref[:, vv], prev_ref[h])
            dq, dk, dv, dg, dprev = vjp((dy_ref[:, vv].astype(F32), dstate[h]))
            dq_ref[:, qk], dk_ref[:, qk] = dq, dk
            dv_ref[:, vv], dg_ref[:, vv] = dv.astype(dv_ref.dtype), dg.astype(dg_ref.dtype)
            dstate[h] = dprev

    chunk = lambda w, c=0: pl.BlockSpec((BLOCK, w), lambda i, c=c: (nc - 1 - i, c))
    return pl.pallas_call(
        body, name="ret_bwd", grid=(nc,),
        in_specs=[chunk(256), chunk(256), chunk(512, PROJ["rv"][0] // 512), chunk(512, PROJ["rg"][0] // 512),
                  pl.BlockSpec((None, RET_HEADS, RET_DK, RET_DV), lambda i: (nc - 1 - i, 0, 0, 0)), chunk(512),
                  _const_spec(decay.shape), _const_spec(q_w.shape), _const_spec(k_w.shape)],
        out_specs=[chunk(256), chunk(256), chunk(512), chunk(512)],
        out_shape=[jax.ShapeDtypeStruct((t_len, 256), F32), jax.ShapeDtypeStruct((t_len, 256), F32),
                   jax.ShapeDtypeStruct((t_len, 512), BF16), jax.ShapeDtypeStruct((t_len, 512), BF16)],
        scratch_shapes=[pltpu.VMEM((RET_HEADS, RET_DK, RET_DV), F32)],
        compiler_params=_params(("arbitrary",)))(q, k, proj, proj, prevs, dy, decay, q_w, k_w)


MLA_SCALE = 96 ** -0.5


def _mla_prep(qf, kvf, proj, cos, sin):
    def fn(q, kv, kr, c, s):
        q_rope = _rope(q[:, 512:768], c, s, 16)
        k_rope = _rope(kr, c, s, 16)[:, :32]
        zero = jnp.zeros_like(k_rope)
        qs = [jnp.concatenate([q[:, 64 * h:64 * h + 64], q_rope[:, 32 * h:32 * h + 32], zero], axis=1) for h in range(MLA_HEADS)]
        ks = [jnp.concatenate([kv[:, 64 * h:64 * h + 64], k_rope, zero], axis=1) for h in range(MLA_HEADS)]
        ones = (lax.broadcasted_iota(jnp.int32, (kv.shape[0], 64), 1) == 0).astype(F32)
        vs = [jnp.concatenate([kv[:, 512 + 64 * h:576 + 64 * h], ones], axis=1) for h in range(MLA_HEADS)]
        return qs, ks, vs
    t_len = qf.shape[0]
    return _rowwise("mla_prep", fn, [qf, kvf, _seg(proj, "kr"), cos, sin], [],
                    [((8, t_len, 128), BF16), ((8, t_len, 128), BF16), ((8, t_len, 128), BF16)])


def _mla_unprep(dq, dk, dv, cos, sin):
    def fn(dq_t, dk_t, dv_t, c, s):
        q_rope = _rope(jnp.concatenate([dq_t[h][:, 64:96] for h in range(MLA_HEADS)], axis=1), c, -s, 16)
        d_qf = jnp.concatenate([dq_t[h][:, :64] for h in range(MLA_HEADS)] + [q_rope], axis=1)
        d_kvf = jnp.concatenate([dk_t[h][:, :64] for h in range(MLA_HEADS)] + [dv_t[h][:, :64] for h in range(MLA_HEADS)], axis=1)
        k_rope = dk_t[0][:, 64:96]
        for h in range(1, MLA_HEADS):
            k_rope = k_rope + dk_t[h][:, 64:96]
        d_kr = _rope(jnp.concatenate([k_rope, jnp.zeros((k_rope.shape[0], 96), F32)], axis=1), c, -s, 16)
        return d_qf, d_kvf, d_kr
    t_len = dq.shape[1]
    return _rowwise("mla_unprep", fn, [dq, dk, dv, cos, sin], [],
                    [((t_len, 768), BF16), ((t_len, 1024), BF16), ((t_len, 128), BF16)])


MLA_EXP2 = MLA_SCALE * math.log2(math.e)
MLA_TILE = 1024


def _diag_mask(tq):
    return lax.broadcasted_iota(jnp.int32, (tq, tq), 1) <= lax.broadcasted_iota(jnp.int32, (tq, tq), 0)


def _tri_steps(nq, key_major):
    if key_major:
        pairs = [(i, j) for j in range(nq) for i in range(j, nq)]
    else:
        pairs = [(i, j) for i in range(nq) for j in range(i + 1)]
    return jnp.asarray([p[0] for p in pairs], jnp.int32), jnp.asarray([p[1] for p in pairs], jnp.int32)


def _ride(body, rider, counts, grid):
    if rider is None:
        return body
    extra = (0, len(rider.inputs), len(rider.out_shape), len(rider.sems))

    def wrapped(*refs):
        host, ridden, pos = [], [], 0
        for mine, theirs in zip(counts, extra):
            host += refs[pos:pos + mine]
            ridden.append(refs[pos + mine:pos + mine + theirs])
            pos += mine + theirs
        ids = [pl.program_id(a) for a in range(len(grid))]
        first, last = ids[0] == 0, ids[0] == grid[0] - 1
        for a in range(1, len(grid)):
            first, last = first & (ids[a] == 0), last & (ids[a] == grid[a] - 1)

        @pl.when(first)
        def _():
            rider.start(*ridden[1:])

        body(*host)

        @pl.when(last)
        def _():
            rider.finish(*ridden[1:])

    return wrapped


def _mla_fwd(q, k, v, rider=None):
    t_len = q.shape[1]
    tq = min(MLA_TILE, t_len)
    nq = t_len // tq
    qi, kj = _tri_steps(nq, False)

    def body(i_ref, j_ref, q_ref, k_ref, v_ref, o_ref, lse_ref, m_s, acc_s):
        step = pl.program_id(1)
        i, j = i_ref[step], j_ref[step]

        @pl.when(j == 0)
        def _():
            m_s[...] = jnp.full_like(m_s, NEG_INF)
            acc_s[...] = jnp.zeros_like(acc_s)

        def update(mask):
            for e in range(2):
                s = _raw_dot(q_ref[e], k_ref[e], NT)
                if mask is not None:
                    s = jnp.where(mask, s, NEG_INF)
                m_old = m_s[e]
                m_new = jnp.maximum(m_old, jnp.max(s, axis=-1, keepdims=True))
                p = jnp.exp2((s - m_new) * MLA_EXP2)
                acc_s[e] = jnp.exp2((m_old - m_new) * MLA_EXP2) * acc_s[e] + _raw_dot(p, v_ref[e], NN)
                m_s[e] = m_new

        @pl.when(j < i)
        def _():
            update(None)

        @pl.when(j == i)
        def _():
            update(_diag_mask(tq))
            outs = []
            for e in range(2):
                acc = acc_s[e]
                denom = acc[:, 64:65]
                outs.append(acc[:, :64] / denom)
                lse_ref[e] = m_s[e] * MLA_EXP2 + jnp.log2(denom)
            o_ref[...] = jnp.concatenate(outs, axis=1)

    grid = (MLA_HEADS // 2, int(qi.shape[0]))
    ride_in, ride_out, ride_sems = ([], [], []) if rider is None else (rider.inputs, rider.out_shape, rider.sems)
    grid_spec = pltpu.PrefetchScalarGridSpec(
        num_scalar_prefetch=2, grid=grid,
        in_specs=[pl.BlockSpec((2, tq, 128), lambda hp, s, qi, kj: (hp, qi[s], 0)),
                  pl.BlockSpec((2, tq, 128), lambda hp, s, qi, kj: (hp, kj[s], 0)),
                  pl.BlockSpec((2, tq, 128), lambda hp, s, qi, kj: (hp, kj[s], 0))] + [ANY_SPEC] * len(ride_in),
        out_specs=[pl.BlockSpec((tq, 128), lambda hp, s, qi, kj: (qi[s], hp)),
                   pl.BlockSpec((2, tq, 1), lambda hp, s, qi, kj: (hp, qi[s], 0))] + [ANY_SPEC] * len(ride_out),
        scratch_shapes=[pltpu.VMEM((2, tq, 1), F32), pltpu.VMEM((2, tq, 128), F32)] + ride_sems)
    res = pl.pallas_call(
        _ride(body, rider, (2, 3, 2, 2), grid), name="mla_fwd" if rider is None else "mla_fwd_gather", grid_spec=grid_spec,
        out_shape=[jax.ShapeDtypeStruct((t_len, 512), F32), jax.ShapeDtypeStruct((MLA_HEADS, t_len, 1), F32)] + ride_out,
        compiler_params=_params(("parallel" if rider is None else "arbitrary", "arbitrary")))(qi, kj, q, k, v, *ride_in)
    return res[0], res[1], res[2:]


def _mla_bwd_prep(o, dy):
    def fn(o_t, dy_t):
        dos, deltas = [], []
        for h in range(MLA_HEADS):
            d_h = dy_t[:, 64 * h:64 * h + 64].astype(F32)
            dos.append(jnp.concatenate([d_h, jnp.zeros_like(d_h)], axis=1))
            deltas.append(jnp.sum(d_h * o_t[:, 64 * h:64 * h + 64], axis=-1, keepdims=True))
        return dos, deltas
    t_len = o.shape[0]
    return _rowwise("mla_bwd_prep", fn, [o, dy], [], [((8, t_len, 128), BF16), ((8, t_len, 1), F32)])


def _mla_bwd(q, k, v, do, lse2, delta, rider=None):
    t_len = q.shape[1]
    tq = min(MLA_TILE, t_len)
    nq = t_len // tq
    qi, kj = _tri_steps(nq, True)

    def body(i_ref, j_ref, q_ref, k_ref, v_ref, do_ref, lse_ref, dl_ref, dq_ref, dk_ref, dv_ref, k_acc, v_acc):
        step = pl.program_id(1)
        i, j = i_ref[step], j_ref[step]

        @pl.when(step == 0)
        def _():
            dq_ref[...] = jnp.zeros_like(dq_ref)

        @pl.when(i == j)
        def _():
            k_acc[...] = jnp.zeros_like(k_acc)
            v_acc[...] = jnp.zeros_like(v_acc)

        def tile(mask):
            p = jnp.exp2(_raw_dot(q_ref[...], k_ref[...], NT) * MLA_EXP2 - lse_ref[...])
            if mask is not None:
                p = jnp.where(mask, p, 0.0)
            ds = p * (_raw_dot(do_ref[...], v_ref[...], NT) - dl_ref[...]) * MLA_SCALE
            v_acc[...] += _raw_dot(p, do_ref[...], TN)
            k_acc[...] += _raw_dot(ds, q_ref[...], TN)
            rows = pl.ds(pl.multiple_of(i * tq, tq), tq)
            dq_ref[rows, :] += _raw_dot(ds, k_ref[...], NN)

        @pl.when(i == j)
        def _():
            tile(_diag_mask(tq))

        @pl.when(i > j)
        def _():
            tile(None)

        @pl.when(i == nq - 1)
        def _():
            dk_ref[...] = k_acc[...]
            dv_ref[...] = v_acc[...]

    q_blk = lambda w: pl.BlockSpec((None, tq, w), lambda h, s, qi, kj: (h, qi[s], 0))
    k_blk = lambda w: pl.BlockSpec((None, tq, w), lambda h, s, qi, kj: (h, kj[s], 0))
    grid = (MLA_HEADS, int(qi.shape[0]))
    ride_in, ride_out, ride_sems = ([], [], []) if rider is None else (rider.inputs, rider.out_shape, rider.sems)
    grid_spec = pltpu.PrefetchScalarGridSpec(
        num_scalar_prefetch=2, grid=grid,
        in_specs=[q_blk(128), k_blk(128), k_blk(128), q_blk(128), q_blk(1), q_blk(1)] + [ANY_SPEC] * len(ride_in),
        out_specs=[pl.BlockSpec((None, t_len, 128), lambda h, s, qi, kj: (h, 0, 0)), k_blk(128), k_blk(128)]
        + [ANY_SPEC] * len(ride_out),
        scratch_shapes=[pltpu.VMEM((tq, 128), F32), pltpu.VMEM((tq, 128), F32)] + ride_sems)
    res = pl.pallas_call(
        _ride(body, rider, (2, 6, 3, 2), grid), name="mla_bwd" if rider is None else "mla_bwd_scatter", grid_spec=grid_spec,
        out_shape=[jax.ShapeDtypeStruct((MLA_HEADS, t_len, 128), F32)] * 3 + ride_out,
        compiler_params=_params(("parallel" if rider is None else "arbitrary", "arbitrary")))(qi, kj, q, k, v, do, lse2, delta, *ride_in)
    return res[0], res[1], res[2], res[3:]


MERGE_TN = 256
GATE_BLOCK0 = PROJ["gates"][0] // MERGE_TN


def _merge_specs(tm):
    y_spec = pl.BlockSpec((tm, 512), lambda i, j: (i, 0))
    w_spec = pl.BlockSpec((512, MERGE_TN), lambda i, j: (0, j))
    gate = lambda b: pl.BlockSpec((tm, MERGE_TN), lambda i, j, b=b: (i, GATE_BLOCK0 + 4 * b + j))
    return [y_spec] * 4 + [w_spec] * 4 + [gate(b) for b in range(4)]


def _merge_fwd(ys, wbs, proj):
    t_len = proj.shape[0]
    tm = min(512, t_len)

    def body(*refs):
        y, w, gl, o_ref = refs[0:4], refs[4:8], refs[8:12], refs[12]
        acc = jnp.zeros((tm, MERGE_TN), F32)
        for b in range(4):
            acc += jax.nn.sigmoid(gl[b][...]) * _raw_dot(y[b][...], w[b][...], NN)
        o_ref[...] = acc.astype(o_ref.dtype)

    return pl.pallas_call(body, name="merge_fwd", grid=(t_len // tm, D_MODEL // MERGE_TN), in_specs=_merge_specs(tm),
                          out_specs=pl.BlockSpec((tm, MERGE_TN), lambda i, j: (i, j)),
                          out_shape=jax.ShapeDtypeStruct((t_len, D_MODEL), BF16),
                          compiler_params=_params(("parallel", "parallel")))(*ys, *wbs, *([proj] * 4))


def _merge_bwd(ys, wbs, proj, dmerged):
    t_len = proj.shape[0]
    tm = min(512, t_len)

    def body(*refs):
        y, w, gl, dm_ref, dgl, dt = refs[0:4], refs[4:8], refs[8:12], refs[12], refs[13:17], refs[17:21]
        dm = dm_ref[...]
        for b in range(4):
            gate = jax.nn.sigmoid(gl[b][...])
            t_b = _raw_dot(y[b][...], w[b][...], NN)
            dgl[b][...] = (dm * t_b * gate * (1.0 - gate)).astype(BF16)
            dt[b][...] = (dm * gate).astype(BF16)

    o_spec = pl.BlockSpec((tm, MERGE_TN), lambda i, j: (i, j))
    return pl.pallas_call(body, name="merge_bwd", grid=(t_len // tm, D_MODEL // MERGE_TN), in_specs=_merge_specs(tm) + [o_spec],
                          out_specs=[o_spec] * 8, out_shape=[jax.ShapeDtypeStruct((t_len, D_MODEL), BF16)] * 8,
                          compiler_params=_params(("parallel", "parallel")))(*ys, *wbs, *([proj] * 4), dmerged)


def _adamw(name, w, g, m, v, slabs):
    shape = w.shape
    size = int(np.prod(shape))
    if shape[-1] >= 96:
        view = (size // shape[-1], shape[-1])
    elif size % 128 == 0:
        view = (size // 128, 128)
    else:
        view = (1, size)
    rows = view[0]
    tile = rows
    for cand in (512, 256, 128, 64):
        if rows > cand and rows % cand == 0 and cand * view[1] * 4 <= 2 ** 20:
            tile = cand
            break
    if rows * view[1] * 4 <= 2 ** 20:
        tile = rows
    c1, c2 = 1.0 - ADAM_B1 ** ADAM_STEP, 1.0 - ADAM_B2 ** ADAM_STEP

    def body(w_ref, g_ref, m_ref, v_ref, og_ref, od_ref, om_ref, ov_ref):
        if slabs:
            grad = g_ref[0].astype(F32)
            for d in range(1, slabs):
                grad = grad + g_ref[d].astype(F32)
        else:
            grad = g_ref[...]
        m_new = ADAM_B1 * m_ref[...] + (1.0 - ADAM_B1) * grad
        v_new = ADAM_B2 * v_ref[...] + (1.0 - ADAM_B2) * jnp.square(grad)
        og_ref[...] = grad
        od_ref[...] = -ADAM_LR * ((m_new / c1) / (jnp.sqrt(v_new / c2) + ADAM_EPS) + ADAM_WD * w_ref[...])
        om_ref[...] = m_new
        ov_ref[...] = v_new

    spec = pl.BlockSpec((tile, view[1]), lambda i: (i, 0))
    g_spec = pl.BlockSpec((slabs, tile, view[1]), lambda i: (0, i, 0)) if slabs else spec
    g_view = g.reshape((slabs,) + view) if slabs else g.reshape(view)
    outs = pl.pallas_call(body, name=name, grid=(rows // tile,), in_specs=[spec, g_spec, spec, spec], out_specs=[spec] * 4,
                          out_shape=[jax.ShapeDtypeStruct(view, F32)] * 4,
                          compiler_params=_params(("parallel",)))(w.reshape(view), g_view, m.reshape(view), v.reshape(view))
    return [o.reshape(shape) for o in outs]


def _sum_slabs(name, x):
    def body(x_ref, o_ref):
        acc = x_ref[0]
        for d in range(1, N_DEV):
            acc = acc + x_ref[d]
        o_ref[...] = acc
    return pl.pallas_call(body, name=name, out_shape=jax.ShapeDtypeStruct(x.shape[1:], x.dtype))(x)


MESH_ID = pl.DeviceIdType.MESH
ANY_SPEC = pl.BlockSpec(memory_space=pl.ANY)


def _remote(src, dst, send_sems, recv_sems, k, to):
    return pltpu.make_async_remote_copy(src_ref=src, dst_ref=dst, send_sem=send_sems.at[k], recv_sem=recv_sems.at[k],
                                        device_id=to, device_id_type=MESH_ID)


class _Rider:
    def __init__(self, inputs, out_shape, sems, start, finish):
        self.inputs, self.out_shape, self.sems, self.start, self.finish = list(inputs), list(out_shape), list(sems), start, finish


def _run_rider(name, rider):
    n_in, n_out = len(rider.inputs), len(rider.out_shape)

    def body(*refs):
        parts = refs[:n_in], refs[n_in:n_in + n_out], refs[n_in + n_out:]
        rider.start(*parts)
        rider.finish(*parts)

    return pl.pallas_call(body, name=name, in_specs=[ANY_SPEC] * n_in, out_specs=[ANY_SPEC] * n_out, out_shape=rider.out_shape,
                          scratch_shapes=rider.sems, compiler_params=pltpu.CompilerParams(has_side_effects=True))(*rider.inputs)


def _gather_rider(blocks):
    n = len(blocks)

    def plan(srcs, dsts, sems, waits):
        send_sems, recv_sems, local_sems = sems
        x, y, c = lax.axis_index("x"), lax.axis_index("y"), lax.axis_index("c")
        sibling = (x, y, 1 - c)
        chips = [(1 - x, y), (x, 1 - y), (1 - x, 1 - y)]
        index = lambda px, py, pc: 4 * px + 2 * py + pc

        def copy(t, k, block, to, src=None):
            slot = dsts[t].at[block]
            return _remote(slot if src is None else src, slot, send_sems, recv_sems, 7 * t + k, to)

        me = index(x, y, c)
        own = [pltpu.make_async_copy(srcs[t], dsts[t].at[me], local_sems.at[t]) for t in range(n)]
        first = []
        for t in range(n):
            first.append(copy(t, 0, me, sibling, src=srcs[t]))
            first += [copy(t, 1 + j, me, (*chip, c), src=srcs[t]) for j, chip in enumerate(chips)]
        if not waits:
            return own, first, [], []
        relays = [(copy(t, 1 + j, index(*chip, c), sibling), copy(t, 4 + j, index(*chip, c), sibling))
                  for j, chip in enumerate(chips) for t in range(n)]
        from_sibling = [copy(t, 0, index(x, y, 1 - c), sibling) for t in range(n)]
        from_sibling += [copy(t, 4 + j, index(*chip, 1 - c), sibling) for t in range(n) for j, chip in enumerate(chips)]
        return own, first, relays, from_sibling

    def start(srcs, dsts, sems):
        own, first, _, _ = plan(srcs, dsts, sems, False)
        for cp in own + first:
            cp.start()

    def finish(srcs, dsts, sems):
        own, first, relays, from_sibling = plan(srcs, dsts, sems, True)
        for arrival, forward in relays:
            arrival.wait_recv()
            forward.start()
        for cp in from_sibling:
            cp.wait_recv()
        for cp in first + [forward for _, forward in relays]:
            cp.wait_send()
        for cp in own:
            cp.wait()

    return _Rider(blocks, [jax.ShapeDtypeStruct((N_DEV,) + b.shape, b.dtype) for b in blocks],
                  [pltpu.SemaphoreType.DMA((7 * n,)), pltpu.SemaphoreType.DMA((7 * n,)), pltpu.SemaphoreType.DMA((n,))],
                  start, finish)


def _pair_exchange(name, blocks):
    n = len(blocks)

    def body(*refs):
        srcs, dsts = refs[:n], refs[n:2 * n]
        send_sems, recv_sems = refs[2 * n:]
        sibling = (lax.axis_index("x"), lax.axis_index("y"), 1 - lax.axis_index("c"))
        copies = [_remote(srcs[t], dsts[t], send_sems, recv_sems, t, sibling) for t in range(n)]
        for cp in copies:
            cp.start()
        for cp in copies:
            cp.wait()

    return pl.pallas_call(
        body, name=name, in_specs=[ANY_SPEC] * n, out_specs=[ANY_SPEC] * n,
        out_shape=[jax.ShapeDtypeStruct(b.shape, b.dtype) for b in blocks],
        scratch_shapes=[pltpu.SemaphoreType.DMA((n,)), pltpu.SemaphoreType.DMA((n,))],
        compiler_params=pltpu.CompilerParams(has_side_effects=True))(*blocks)


def _chip_rider(items):
    n = len(items)
    out_shape = [jax.ShapeDtypeStruct((4,) + tuple(a.shape[1:] if per_chip else a.shape), a.dtype) for a, per_chip in items]

    def plan(srcs, dsts, sems, waits):
        send_sems, recv_sems, local_sems = sems
        x, y, c = lax.axis_index("x"), lax.axis_index("y"), lax.axis_index("c")
        mine = 2 * x + y
        own = [pltpu.make_async_copy(srcs[t].at[mine] if per_chip else srcs[t], dsts[t].at[mine], local_sems.at[t])
               for t, (_, per_chip) in enumerate(items)]
        sends, arrivals = [], []
        for k in range(1, 4):
            px, py = x ^ (k >> 1), y ^ (k & 1)
            peer = 2 * px + py
            for t, (_, per_chip) in enumerate(items):
                src = srcs[t].at[peer] if per_chip else srcs[t]
                sends.append(_remote(src, dsts[t].at[mine], send_sems, recv_sems, 3 * t + k - 1, (px, py, c)))
                if waits:
                    arrivals.append(_remote(src, dsts[t].at[peer], send_sems, recv_sems, 3 * t + k - 1, (px, py, c)))
        return own, sends, arrivals

    def start(srcs, dsts, sems):
        own, sends, _ = plan(srcs, dsts, sems, False)
        for cp in own + sends:
            cp.start()

    def finish(srcs, dsts, sems):
        own, sends, arrivals = plan(srcs, dsts, sems, True)
        for out, arrival in zip(sends, arrivals):
            out.wait_send()
            arrival.wait_recv()
        for cp in own:
            cp.wait()

    return _Rider([a for a, _ in items], out_shape,
                  [pltpu.SemaphoreType.DMA((3 * n,)), pltpu.SemaphoreType.DMA((3 * n,)), pltpu.SemaphoreType.DMA((n,))],
                  start, finish)


def _perm(a):
    t_len, w = a.shape
    return a.reshape(SCAN_SEG, t_len // SCAN_SEG, w).transpose(1, 0, 2).reshape(t_len, w)


def _unperm(a):
    t_len, w = a.shape
    return a.reshape(t_len // SCAN_SEG, SCAN_SEG, w).transpose(1, 0, 2).reshape(t_len, w)


def _band_diag(p):
    a, b = p.shape[1:]
    eye = jnp.eye(8, dtype=p.dtype)
    return (p.reshape(4, 8, a, 1, b) * eye[None, :, None, :, None]).reshape(4, 8 * a, 8 * b)


def _band_diag_extract(m, a, b):
    eye = jnp.eye(8, dtype=m.dtype)
    return (m.reshape(4, 8, a, 8, b) * eye[None, :, None, :, None]).sum(axis=3).reshape(32, a, b)


def _proj_cols(w_ref_layout):
    parts = []
    for name in PROJ_ORDER:
        off, width = REF_SPLIT[name]
        part = w_ref_layout[:, off:off + width]
        if width < PROJ[name][1]:
            part = jnp.pad(part, ((0, 0), (0, PROJ[name][1] - width)))
        parts.append(part)
    return jnp.concatenate(parts, axis=1)


def _unproj_cols(w_proj_layout):
    parts = []
    for name in REF_ORDER:
        parts.append(w_proj_layout[:, PROJ[name][0]:PROJ[name][0] + REF_SPLIT[name][1]])
    return jnp.concatenate(parts, axis=1)


def _uq_cols(w):
    w3 = w.reshape(w.shape[0], MLA_HEADS, 96)
    return jnp.concatenate([w3[:, :, :64].reshape(-1, 512), w3[:, :, 64:].reshape(-1, 256)], axis=1)


def _un_uq_cols(w):
    return jnp.concatenate([w[:, :512].reshape(-1, MLA_HEADS, 64), w[:, 512:].reshape(-1, MLA_HEADS, 32)], axis=2).reshape(-1, 768)


def _ukv_cols(w):
    w3 = w.reshape(w.shape[0], MLA_HEADS, 128)
    return jnp.concatenate([w3[:, :, :64].reshape(-1, 512), w3[:, :, 64:].reshape(-1, 512)], axis=1)


def _un_ukv_cols(w):
    return jnp.concatenate([w[:, :512].reshape(-1, MLA_HEADS, 64), w[:, 512:].reshape(-1, MLA_HEADS, 64)], axis=2).reshape(-1, 1024)


def _cols_gathered(g):
    return g.transpose(1, 0, 2).reshape(g.shape[1], -1)


def _cols_slabs(w):
    r, c = w.shape
    return w.reshape(r, N_DEV, c // N_DEV).transpose(1, 0, 2)


def _row(v):
    return v.reshape(1, -1)


def _s5_weights(p):
    flat = dict(lam_re=p["s5_lam_re"].reshape(S5_CH, 1), lam_im=p["s5_lam_im"].reshape(S5_CH, 1),
                log_dt=jnp.repeat(p["s5_log_dt"], S5_STATE).reshape(S5_CH, 1),
                b_re=p["s5_b_re"].reshape(S5_CH, S5_GROUP), b_im=p["s5_b_im"].reshape(S5_CH, S5_GROUP))
    ab_re, ab_im, bb_re, bb_im = _s5_param_fwd(flat["lam_re"], flat["lam_im"], flat["log_dt"], flat["b_re"], flat["b_im"])
    w_bu = jnp.concatenate([_band_diag(bb.reshape(32, 64, 16).transpose(0, 2, 1)) for bb in (bb_re, bb_im)], axis=0)
    w_y = jnp.concatenate([_band_diag(p["s5_c_re"].transpose(0, 2, 1)), -_band_diag(p["s5_c_im"].transpose(0, 2, 1))], axis=0)
    return dict(flat=flat,
                a_re=jnp.broadcast_to(ab_re.reshape(1, S5_CH), (SCAN_SEG, S5_CH)),
                a_im=jnp.broadcast_to(ab_im.reshape(1, S5_CH), (SCAN_SEG, S5_CH)),
                w_bu=w_bu.astype(BF16), w_bu_t=w_bu.swapaxes(1, 2).astype(BF16),
                w_y=w_y.astype(BF16), w_y_t=w_y.swapaxes(1, 2).astype(BF16))


def _layer_fwd(x, w, p, tabs, rider=None):
    t_len = x.shape[0]
    cos64, sin64, cos32, sin32 = tabs
    s = {}
    (s["h1"],) = _rowwise("pre_mix", lambda xt, g: (_rms(xt, g),), [x], [_row(p["g_pre_mix"])], [((t_len, D_MODEL), BF16)])
    proj = s["proj"] = _mm("proj", s["h1"], w["w_in"], "nn")
    s["qa"], s["ka"], s["va"] = _swa_prep(proj, cos64, sin64)
    s["sinks"] = jnp.broadcast_to(p["swa_sinks"][:, None], (8, 128))
    y_a = _swa_fwd(s["qa"], s["ka"], s["va"], s["sinks"])
    s5 = s["s5"] = _s5_weights(p)
    su = proj[:, PROJ["su"][0]:PROJ["su"][0] + 512]
    s["u_p"] = _perm(su).astype(BF16)
    bu = _s5_expand("s5_bu", s["u_p"], s5["w_bu"])
    s["xs"], s["cin"] = _s5_scan("s5_scan_fwd", bu, s5["a_re"], s5["a_im"])
    s["y_s"] = _unperm(_s5_contract("s5_y", s["xs"], s5["w_y"]))
    (s["z"],) = _rowwise("s5_gelu", lambda y, u, d: (jax.nn.gelu(y + d * u),), [s["y_s"], _seg(proj, "su")],
                         [_row(p["s5_d"])], [((t_len, 512), BF16)])
    s["zz"] = _mm("s5_glu_mm", s["z"], w["w_glu"], "nn")
    (y_b,) = _rowwise("s5_glu", lambda t: (t[:, :512] * jax.nn.sigmoid(t[:, 512:]),), [s["zz"]], [], [((t_len, 512), BF16)])
    s["qc"], s["kc"] = _ret_prep(proj, cos64, sin64)
    y_c, s["prevs"] = _ret_fwd(s["qc"], s["kc"], proj)
    s["cqn"], s["ckvn"] = _rowwise("mla_norm", lambda cq, ckv, gq, gkv: (_rms(cq, gq), _rms(ckv, gkv)),
                                   [_seg(proj, "cq"), _seg(proj, "ckv")], [_row(p["mla_g_q"]), _row(p["mla_g_kv"])],
                                   [((t_len, 256), BF16), ((t_len, 128), BF16)])
    qf = _mm("mla_uq", s["cqn"], w["w_uq"], "nn")
    kvf = _mm("mla_ukv", s["ckvn"], w["w_ukv"], "nn")
    s["Q"], s["K"], s["V"] = _mla_prep(qf, kvf, proj, cos32, sin32)
    y_d, s["lse"], ridden = _mla_fwd(s["Q"], s["K"], s["V"], rider)
    s["ys"] = [y_a, y_b, y_c, y_d]
    s["merged"] = _merge_fwd(s["ys"], w["wb"], proj)
    s["o"] = _mm("out_mm", s["merged"], w["w_out"], "nn")

    def post_mix(xt, ot, g1, g2):
        x1 = xt + _rms(ot, g1)
        return x1, _rms(x1, g2)
    s["x1"], s["h2"] = _rowwise("post_mix", post_mix, [x, s["o"]], [_row(p["g_post_mix"]), _row(p["g_pre_mlp"])],
                                [((t_len, D_MODEL), F32), ((t_len, D_MODEL), BF16)])
    s["a"], s["r"] = _mm("ff1", s["h2"], w["w_ff1"], "nn", out_dtypes=(F32, BF16),
                         epi=lambda acc: (acc, jnp.square(jnp.maximum(acc, 0.0))))
    s["f"] = _mm("ff2", s["r"], w["w_ff2"], "nn")
    (x2,) = _rowwise("post_mlp", lambda x1, f, g: (x1 + _rms(f, g),), [s["x1"], s["f"]], [_row(p["g_post_mlp"])],
                     [((t_len, D_MODEL), F32)])
    return x2, s, ridden


def _layer_bwd(x, dx2, w, p, s, tabs, rider=None):
    t_len = x.shape[0]
    cos64, sin64, cos32, sin32 = tabs
    proj = s["proj"]
    big, small = {}, {}

    def post_mlp_bwd(f, d, g):
        df, dg = _rms_bwd(f, g, d)
        return (df,), (dg,)
    df, small["g_post_mlp"] = _rowwise("post_mlp_bwd", post_mlp_bwd, [s["f"], dx2], [_row(p["g_post_mlp"])],
                                       [((t_len, D_MODEL), BF16)], [((1, D_MODEL), F32)])
    da = _mm("ff2_dx", df, w["w_ff2"], "nt", out_dtypes=(BF16,), epi=lambda acc, a: (acc * 2.0 * jnp.maximum(a, 0.0),),
             epi_in=(s["a"],))
    big["w_ff2"] = _mm("ff2_dw", s["r"], df, "tn")
    dh2 = _mm("ff1_dx", da, w["w_ff1"], "nt")
    big["w_ff1"] = _mm("ff1_dw", s["h2"], da, "tn")

    def post_mix_bwd(x1, o, dh, d2, g1, g2):
        dx1_n, dg2 = _rms_bwd(x1, g2, dh)
        dx1 = d2 + dx1_n
        do, dg1 = _rms_bwd(o, g1, dx1)
        return (dx1, do), (dg1, dg2)
    dx1, do, small["g_post_mix"], small["g_pre_mlp"] = _rowwise(
        "post_mix_bwd", post_mix_bwd, [s["x1"], s["o"], dh2, dx2], [_row(p["g_post_mix"]), _row(p["g_pre_mlp"])],
        [((t_len, D_MODEL), F32), ((t_len, D_MODEL), BF16)], [((1, D_MODEL), F32), ((1, D_MODEL), F32)])
    dmerged = _mm("out_dx", do, w["w_out"], "nt")
    big["w_out"] = _mm("out_dw", s["merged"], do, "tn")

    res = _merge_bwd(s["ys"], w["wb"], proj, dmerged)
    dgl, dts = res[:4], res[4:]
    dys = [_mm("branch_dx", dts[b], w["wb"][b], "nt") for b in range(4)]
    big["wb"] = [_mm("branch_dw", s["ys"][b], dts[b], "tn") for b in range(4)]
    seg = {}

    dqa, dka, dva, dka_prev, dva_prev, dsink = _swa_bwd(s["qa"], s["ka"], s["va"], s["sinks"], dys[0])
    seg["sq"], seg["sk"], seg["sv"] = _swa_unprep(dqa, dka, dva, dka_prev, dva_prev, cos64, sin64)
    small["swa_sinks"] = dsink[:, 0]

    def glu_bwd(t, d):
        za, sg = t[:, :512], jax.nn.sigmoid(t[:, 512:])
        return (jnp.concatenate([d * sg, d * za * sg * (1.0 - sg)], axis=1),)
    (dzz,) = _rowwise("s5_glu_bwd", glu_bwd, [s["zz"], dys[1]], [], [((t_len, 1024), BF16)])
    dz = _mm("s5_glu_dx", dzz, w["w_glu"], "nt")
    big["w_glu"] = _mm("s5_glu_dw", s["z"], dzz, "tn")

    def gelu_bwd(y, u, d, skip):
        _, vjp = jax.vjp(jax.nn.gelu, y + skip * u)
        (dy2,) = vjp(d)
        return (dy2, dy2 * skip), (jnp.sum(dy2 * u, axis=0, keepdims=True),)
    dy_s, du_skip, d_skip = _rowwise("s5_gelu_bwd", gelu_bwd, [s["y_s"], _seg(proj, "su"), dz], [_row(p["s5_d"])],
                                     [((t_len, 512), F32), ((t_len, 512), F32)], [((1, 512), F32)])
    small["s5_d"] = d_skip[0]
    s5 = s["s5"]
    dy_p = _perm(dy_s).astype(BF16)
    dlam_in = _s5_expand("s5_dx_in", dy_p, s5["w_y_t"])
    lam, da_re, da_im = _s5_scan("s5_scan_bwd", dlam_in, s5["a_re"], s5["a_im"], states=s["xs"], carry_in=s["cin"])
    du_s5 = _unperm(_s5_contract("s5_du", lam, s5["w_bu_t"]))
    d_w_bu = _s5_band_grad("s5_dbu", s["u_p"], lam, True)
    d_w_y = _s5_band_grad("s5_dc", dy_p, s["xs"], False)
    (seg["su"],) = _rowwise("s5_du_sum", lambda a, b: (a + b,), [du_s5, du_skip], [], [((t_len, 512), BF16)])
    d_bb_re = _band_diag_extract(d_w_bu[:4], 16, 64).transpose(0, 2, 1).reshape(S5_CH, S5_GROUP)
    d_bb_im = _band_diag_extract(d_w_bu[4:], 16, 64).transpose(0, 2, 1).reshape(S5_CH, S5_GROUP)
    small["s5_c_re"] = _band_diag_extract(d_w_y[:4], 64, 16).transpose(0, 2, 1)
    small["s5_c_im"] = -_band_diag_extract(d_w_y[4:], 64, 16).transpose(0, 2, 1)
    fl = s5["flat"]
    group_sum = jnp.repeat(jnp.eye(S5_GROUPS, dtype=F32), S5_STATE, axis=1)
    d_lr, d_li, d_ldt, d_br, d_bi = _s5_param_bwd(fl["lam_re"], fl["lam_im"], fl["log_dt"], fl["b_re"], fl["b_im"], group_sum,
                                                  da_re.reshape(S5_CH, 1), da_im.reshape(S5_CH, 1), d_bb_re, d_bb_im)
    small["s5_lam_re"], small["s5_lam_im"] = d_lr.reshape(32, 64), d_li.reshape(32, 64)
    small["s5_log_dt"] = d_ldt[:, 0]
    small["s5_b_re"], small["s5_b_im"] = d_br.reshape(32, 64, 16), d_bi.reshape(32, 64, 16)

    dqc, dkc, seg["rv"], seg["rg"] = _ret_bwd(s["qc"], s["kc"], proj, s["prevs"], dys[2])
    seg["rq"], seg["rk"] = _ret_unprep(dqc, dkc, cos64, sin64)

    do_h, delta = _mla_bwd_prep(s["ys"][3], dys[3])
    d_q, d_k, d_v, ridden = _mla_bwd(s["Q"], s["K"], s["V"], do_h, s["lse"], delta, rider)
    dqf, dkvf, seg["kr"] = _mla_unprep(d_q, d_k, d_v, cos32, sin32)
    dcqn = _mm("mla_uq_dx", dqf, w["w_uq"], "nt")
    big["w_uq"] = _mm("mla_uq_dw", s["cqn"], dqf, "tn")
    dckvn = _mm("mla_ukv_dx", dkvf, w["w_ukv"], "nt")
    big["w_ukv"] = _mm("mla_ukv_dw", s["ckvn"], dkvf, "tn")

    def mla_norm_bwd(cq, ckv, d1, d2, gq, gkv):
        dcq, dgq = _rms_bwd(cq, gq, d1)
        dckv, dgkv = _rms_bwd(ckv, gkv, d2)
        return (dcq, dckv), (dgq, dgkv)
    seg["cq"], seg["ckv"], dgq, dgkv = _rowwise(
        "mla_norm_bwd", mla_norm_bwd, [_seg(proj, "cq"), _seg(proj, "ckv"), dcqn, dckvn],
        [_row(p["mla_g_q"]), _row(p["mla_g_kv"])], [((t_len, 256), BF16), ((t_len, 128), BF16)],
        [((1, 256), F32), ((1, 128), F32)])
    small["mla_g_q"], small["mla_g_kv"] = dgq[0], dgkv[0]

    dproj = jnp.concatenate([seg[name] for name in PROJ_ORDER[:-1]] + list(dgl), axis=1)
    dh1 = _mm("proj_dx", dproj, w["w_in"], "nt")
    big["w_in"] = _mm("proj_dw", s["h1"], dproj, "tn")

    def pre_mix_bwd(xt, dh, d1, g):
        dxn, dg = _rms_bwd(xt, g, dh)
        return (d1 + dxn,), (dg,)
    dx, small["g_pre_mix"] = _rowwise("pre_mix_bwd", pre_mix_bwd, [x, dh1, dx1], [_row(p["g_pre_mix"])],
                                      [((t_len, D_MODEL), F32)], [((1, D_MODEL), F32)])
    for name in ("g_pre_mix", "g_post_mix", "g_pre_mlp", "g_post_mlp"):
        small[name] = small[name][0]
    return dx, big, small, ridden


BIG = ("w_in", "s5_w_glu", "mla_w_uq", "mla_w_ukv", "w_branch", "w_out", "w_ff1", "w_ff2")
SMALL = ("g_pre_mix", "g_post_mix", "g_pre_mlp", "g_post_mlp", "swa_sinks", "s5_lam_re", "s5_lam_im", "s5_log_dt",
         "s5_b_re", "s5_b_im", "s5_c_re", "s5_c_im", "s5_d", "mla_g_q", "mla_g_kv")
WEIGHTS = ("g_pre_mix", "g_post_mix", "g_pre_mlp", "g_post_mlp", "w_in", "swa_sinks", "s5_lam_re", "s5_lam_im", "s5_log_dt",
           "s5_b_re", "s5_b_im", "s5_c_re", "s5_c_im", "s5_d", "s5_w_glu", "mla_g_q", "mla_g_kv", "mla_w_uq", "mla_w_ukv",
           "w_branch", "w_out", "w_ff1", "w_ff2")


def _layer_weights(g):
    wb = g["w_branch"].transpose(1, 2, 0, 3).reshape(4, 512, D_MODEL)
    return dict(w_in=_proj_cols(_cols_gathered(g["w_in"])), w_glu=_cols_gathered(g["s5_w_glu"]),
                w_uq=_uq_cols(_cols_gathered(g["mla_w_uq"])), w_ukv=_ukv_cols(_cols_gathered(g["mla_w_ukv"])),
                wb=[wb[b] for b in range(4)], w_out=g["w_out"].reshape(D_MODEL, D_MODEL),
                w_ff1=_cols_gathered(g["w_ff1"]), w_ff2=g["w_ff2"].reshape(4 * D_MODEL, D_MODEL))


def _grad_slabs(big):
    wb = jnp.stack(big["wb"])
    return dict(w_in=_cols_slabs(_unproj_cols(big["w_in"])), s5_w_glu=_cols_slabs(big["w_glu"]),
                mla_w_uq=_cols_slabs(_un_uq_cols(big["w_uq"])), mla_w_ukv=_cols_slabs(_un_ukv_cols(big["w_ukv"])),
                w_branch=wb.reshape(4, 512, N_DEV, D_MODEL // N_DEV).transpose(2, 0, 1, 3),
                w_out=big["w_out"].reshape(N_DEV, D_MODEL // N_DEV, D_MODEL),
                w_ff1=_cols_slabs(big["w_ff1"]), w_ff2=big["w_ff2"].reshape(N_DEV, 4 * D_MODEL // N_DEV, D_MODEL))


def kernel(x, g_pre_mix, g_post_mix, g_pre_mlp, g_post_mlp, w_in, swa_sinks, s5_lam_re, s5_lam_im, s5_log_dt, s5_b_re, s5_b_im, s5_c_re, s5_c_im, s5_d, s5_w_glu, mla_g_q, mla_g_kv, mla_w_uq, mla_w_ukv, w_branch, w_out, w_ff1, w_ff2, loss_target, m_g_pre_mix, m_g_post_mix, m_g_pre_mlp, m_g_post_mlp, m_w_in, m_swa_sinks, m_s5_lam_re, m_s5_lam_im, m_s5_log_dt, m_s5_b_re, m_s5_b_im, m_s5_c_re, m_s5_c_im, m_s5_d, m_s5_w_glu, m_mla_g_q, m_mla_g_kv, m_mla_w_uq, m_mla_w_ukv, m_w_branch, m_w_out, m_w_ff1, m_w_ff2, v_g_pre_mix, v_g_post_mix, v_g_pre_mlp, v_g_post_mlp, v_w_in, v_swa_sinks, v_s5_lam_re, v_s5_lam_im, v_s5_log_dt, v_s5_b_re, v_s5_b_im, v_s5_c_re, v_s5_c_im, v_s5_d, v_s5_w_glu, v_mla_g_q, v_mla_g_kv, v_mla_w_uq, v_mla_w_ukv, v_w_branch, v_w_out, v_w_ff1, v_w_ff2):
    weights = dict(g_pre_mix=g_pre_mix, g_post_mix=g_post_mix, g_pre_mlp=g_pre_mlp, g_post_mlp=g_post_mlp, w_in=w_in,
                   swa_sinks=swa_sinks, s5_lam_re=s5_lam_re, s5_lam_im=s5_lam_im, s5_log_dt=s5_log_dt, s5_b_re=s5_b_re,
                   s5_b_im=s5_b_im, s5_c_re=s5_c_re, s5_c_im=s5_c_im, s5_d=s5_d, s5_w_glu=s5_w_glu, mla_g_q=mla_g_q,
                   mla_g_kv=mla_g_kv, mla_w_uq=mla_w_uq, mla_w_ukv=mla_w_ukv, w_branch=w_branch, w_out=w_out, w_ff1=w_ff1,
                   w_ff2=w_ff2)
    m_in = dict(g_pre_mix=m_g_pre_mix, g_post_mix=m_g_post_mix, g_pre_mlp=m_g_pre_mlp, g_post_mlp=m_g_post_mlp, w_in=m_w_in,
                swa_sinks=m_swa_sinks, s5_lam_re=m_s5_lam_re, s5_lam_im=m_s5_lam_im, s5_log_dt=m_s5_log_dt,
                s5_b_re=m_s5_b_re, s5_b_im=m_s5_b_im, s5_c_re=m_s5_c_re, s5_c_im=m_s5_c_im, s5_d=m_s5_d,
                s5_w_glu=m_s5_w_glu, mla_g_q=m_mla_g_q, mla_g_kv=m_mla_g_kv, mla_w_uq=m_mla_w_uq, mla_w_ukv=m_mla_w_ukv,
                w_branch=m_w_branch, w_out=m_w_out, w_ff1=m_w_ff1, w_ff2=m_w_ff2)
    v_in = dict(g_pre_mix=v_g_pre_mix, g_post_mix=v_g_post_mix, g_pre_mlp=v_g_pre_mlp, g_post_mlp=v_g_post_mlp, w_in=v_w_in,
                swa_sinks=v_swa_sinks, s5_lam_re=v_s5_lam_re, s5_lam_im=v_s5_lam_im, s5_log_dt=v_s5_log_dt,
                s5_b_re=v_s5_b_re, s5_b_im=v_s5_b_im, s5_c_re=v_s5_c_re, s5_c_im=v_s5_c_im, s5_d=v_s5_d,
                s5_w_glu=v_s5_w_glu, mla_g_q=v_mla_g_q, mla_g_kv=v_mla_g_kv, mla_w_uq=v_mla_w_uq, mla_w_ukv=v_mla_w_ukv,
                w_branch=v_w_branch, w_out=v_w_out, w_ff1=v_w_ff1, w_ff2=v_w_ff2)
    depth = g_pre_mix.shape[0]
    t_len = x.shape[1]
    tabs = _rope_tables(t_len, 32) + _rope_tables(t_len, 16)

    def shards(l):
        return [weights[name][l].astype(BF16) for name in BIG]
    gathered = _run_rider("gather_weights", _gather_rider(shards(0)))

    xs, saved, layer_w, layer_p = [x[0]], [], [], []
    for l in range(depth):
        layer_w.append(_layer_weights(dict(zip(BIG, gathered))))
        layer_p.append({name: weights[name][l] for name in SMALL})
        x_next, s, gathered = _layer_fwd(xs[-1], layer_w[l], layer_p[l], tabs, _gather_rider(shards(l + 1)) if l + 1 < depth else None)
        xs.append(x_next)
        saved.append(s)

    def loss_fn(y, tgt):
        err = y - tgt
        part = 0.5 * jnp.sum(jnp.mean(err * err, axis=-1, keepdims=True), axis=0, keepdims=True)
        return (err * (1.0 / D_MODEL),), (jnp.broadcast_to(part, (8, 128)),)
    dx, loss_part = _rowwise("loss", loss_fn, [xs[-1], loss_target[0]], [], [((t_len, D_MODEL), F32)], [((8, 128), F32)])
    loss = lax.psum(loss_part[0, 0], ("x", "y", "c"))

    core = lax.axis_index("c")

    def pair_stage(l, big, extra):
        slabs = _grad_slabs(big)
        own, send = [], []
        for name in BIG:
            by_chip = slabs[name].reshape((4, 2) + slabs[name].shape[1:])
            own.append(lax.dynamic_index_in_dim(by_chip, core, axis=1, keepdims=False))
            send.append(lax.dynamic_index_in_dim(by_chip, 1 - core, axis=1, keepdims=False).astype(BF16))
        got = _pair_exchange("grads_to_sibling_%d" % l, send + extra)
        sums = []
        for mine, theirs in zip(own, got):
            flat = (-1, mine.shape[-1])
            (part,) = _rowwise("grads_pair_sum", lambda a, b: (a + b.astype(F32),), [mine.reshape(flat), theirs.reshape(flat)],
                               [], [((int(np.prod(mine.shape[:-1])), mine.shape[-1]), BF16)])
            sums.append(part.reshape(mine.shape))
        return sums, got[len(BIG):]

    small_grads, chip_recv, rider = [None] * depth, [None] * depth, None
    for l in reversed(range(depth)):
        dx, big, small_grads[l], ridden = _layer_bwd(xs[l], dx, layer_w[l], layer_p[l], saved[l], tabs, rider)
        if rider is not None:
            chip_recv[l + 1] = ridden
        if l > 0:
            sums, _ = pair_stage(l, big, [])
            rider = _chip_rider([(part, True) for part in sums])
    small_vec = jnp.concatenate([jnp.stack([small_grads[l][name] for l in range(depth)]).reshape(-1) for name in SMALL])
    n_small = small_vec.shape[0]
    rows = -(-n_small // 1024) * 8
    small_mat = jnp.pad(small_vec, (0, rows * 128 - n_small)).reshape(rows, 128)
    sums, (sibling_small,) = pair_stage(0, big, [small_mat])
    small_pair = jnp.where(core == 0, jnp.stack([small_mat, sibling_small]), jnp.stack([sibling_small, small_mat]))
    recv = _run_rider("grads_to_chips", _chip_rider([(part, True) for part in sums] + [(small_pair, False)]))
    chip_recv[0] = recv[:-1]
    small_sum = _sum_slabs("sum_small_grads", recv[-1].reshape((N_DEV,) + small_mat.shape)).reshape(-1)

    grads, deltas, new_m, new_v = {}, {}, {}, {}
    for t, name in enumerate(BIG):
        slabs = jnp.stack([chip_recv[l][t] for l in range(depth)], axis=1)
        grads[name], deltas[name], new_m[name], new_v[name] = _adamw("adamw_" + name, weights[name], slabs, m_in[name],
                                                                     v_in[name], 4)
    off = 0
    for name in SMALL:
        size = int(np.prod(weights[name].shape))
        g = small_sum[off:off + size].reshape(weights[name].shape)
        off += size
        grads[name], deltas[name], new_m[name], new_v[name] = _adamw("adamw_" + name, weights[name], g, m_in[name],
                                                                     v_in[name], False)
    return (loss, dx[None], *[grads[n] for n in WEIGHTS], *[deltas[n] for n in WEIGHTS], *[new_m[n] for n in WEIGHTS],
            *[new_v[n] for n in WEIGHTS])
```

```python
import functools
import math

import numpy as np
import jax
import jax.numpy as jnp
from jax import lax
from jax.experimental import pallas as pl
from jax.experimental.pallas import tpu as pltpu

F32, BF16 = jnp.float32, jnp.bfloat16
EPS = 1e-6
NEG_INF = -1e30
ROPE_THETA = 10000.0
D_MODEL = 1024
N_DEV = 8
BLOCK = 128
SWA_HEADS, SWA_KV = 8, 2
RET_HEADS, RET_DK, RET_DV = 4, 64, 128
MLA_HEADS = 8
S5_GROUPS, S5_STATE, S5_GROUP = 32, 64, 16
S5_CH = S5_GROUPS * S5_STATE
SCAN_LANES = 256
SCAN_SEG = 8
ADAM_LR, ADAM_B1, ADAM_B2, ADAM_EPS, ADAM_WD, ADAM_STEP = 0.001, 0.9, 0.999, 1e-08, 0.01, 10
VMEM_MB = 56

PROJ = dict(sq=(0, 512), su=(512, 512), rv=(1024, 512), rg=(1536, 512), rq=(2048, 256), rk=(2304, 256),
            cq=(2560, 256), sk=(2816, 128), sv=(2944, 128), ckv=(3072, 128), kr=(3200, 128), gates=(3328, 4096))
PROJ_W = 7424
REF_SPLIT = dict(sq=(0, 512), sk=(512, 128), sv=(640, 128), su=(768, 512), rq=(1280, 256), rk=(1536, 256),
                 rv=(1792, 512), rg=(2304, 512), cq=(2816, 256), ckv=(3072, 128), kr=(3200, 32), gates=(3232, 4096))
PROJ_ORDER = ("sq", "su", "rv", "rg", "rq", "rk", "cq", "sk", "sv", "ckv", "kr", "gates")
REF_ORDER = ("sq", "sk", "sv", "su", "rq", "rk", "rv", "rg", "cq", "ckv", "kr", "gates")

NN = (((1,), (0,)), ((), ()))
NT = (((1,), (1,)), ((), ()))
TN = (((0,), (0,)), ((), ()))


def _params(sem):
    return pltpu.CompilerParams(dimension_semantics=sem, vmem_limit_bytes=VMEM_MB * 2 ** 20)


def _raw_dot(a, b, dims):
    return lax.dot_general(a.astype(BF16), b.astype(BF16), dims, preferred_element_type=F32)


@jax.custom_vjp
def _dnn(a, b):
    return _raw_dot(a, b, NN)


@jax.custom_vjp
def _dnt(a, b):
    return _raw_dot(a, b, NT)


@jax.custom_vjp
def _dtn(a, b):
    return _raw_dot(a, b, TN)


_dnn.defvjp(lambda a, b: (_dnn(a, b), (a, b)),
            lambda r, g: (_dnt(g, r[1]).astype(r[0].dtype), _dtn(r[0], g).astype(r[1].dtype)))
_dnt.defvjp(lambda a, b: (_dnt(a, b), (a, b)),
            lambda r, g: (_dnn(g, r[1]).astype(r[0].dtype), _dtn(g, r[0]).astype(r[1].dtype)))
_dtn.defvjp(lambda a, b: (_dtn(a, b), (a, b)),
            lambda r, g: (_dnt(r[1], g).astype(r[0].dtype), _dnn(r[0], g).astype(r[1].dtype)))


class _Cols:
    def __init__(self, arr, width, index):
        self.arr, self.width, self.index = arr, width, index


def _seg(proj, name):
    off, width = PROJ[name]
    return _Cols(proj, width, off // width)


def _row_spec(shape, tile):
    lead = len(shape) - 2
    return pl.BlockSpec(tuple(shape[:-2]) + (tile, shape[-1]), lambda i, lead=lead: (0,) * lead + (i, 0))


def _const_spec(shape):
    nd = len(shape)
    return pl.BlockSpec(tuple(shape), lambda *_, nd=nd: (0,) * nd)


def _rowwise(name, fn, rows, consts, out_rows, out_accs=(), tile=256):
    arrs = [r.arr if isinstance(r, _Cols) else r for r in rows]
    t_len = arrs[0].shape[-2]
    tile = min(tile, t_len)
    n = t_len // tile
    in_specs = []
    for r in rows:
        if isinstance(r, _Cols):
            in_specs.append(pl.BlockSpec((tile, r.width), lambda i, c=r.index: (i, c)))
        else:
            in_specs.append(_row_spec(r.shape, tile))
    in_specs += [_const_spec(c.shape) for c in consts]
    out_specs = [_row_spec(s, tile) for s, _ in out_rows] + [_const_spec(s) for s, _ in out_accs]
    out_shape = [jax.ShapeDtypeStruct(s, d) for s, d in tuple(out_rows) + tuple(out_accs)]
    n_in, n_or = len(rows) + len(consts), len(out_rows)

    def body(*refs):
        res = fn(*[r[...] for r in refs[:n_in]])
        row_out, acc_out = res if out_accs else (res, ())
        for ref, val in zip(refs[n_in:n_in + n_or], row_out):
            if isinstance(val, (list, tuple)):
                for h, v_h in enumerate(val):
                    ref[h] = v_h.astype(ref.dtype)
            else:
                ref[...] = val.astype(ref.dtype)
        first = pl.program_id(0) == 0
        for ref, val in zip(refs[n_in + n_or:], acc_out):
            @pl.when(first)
            def _(ref=ref, val=val):
                ref[...] = val

            @pl.when(jnp.logical_not(first))
            def _(ref=ref, val=val):
                ref[...] += val

    return pl.pallas_call(body, name=name, grid=(n,), in_specs=in_specs, out_specs=out_specs, out_shape=out_shape,
                          compiler_params=_params(("arbitrary",)))(*arrs, *consts)


MM_VMEM_BUDGET = 44 * 2 ** 20


def _pick_tn(n, io_col_bytes, tmp_col_bytes, fixed_bytes):
    best = None
    for cand in range(128, n + 1, 128):
        if n % cand == 0 and 2 * (fixed_bytes + cand * io_col_bytes) + cand * tmp_col_bytes <= MM_VMEM_BUDGET:
            best = cand
    return best if best is not None else min(n, 128)


def _mm(name, a, b, mode, out_dtypes=(F32,), tm=512, tn=None, epi=None, epi_in=(), rider=None):
    if mode == "nn":
        (m, k), n = a.shape, b.shape[1]
    elif mode == "nt":
        (m, k), n = a.shape, b.shape[0]
    else:
        (k, m), n = a.shape, b.shape[1]
    tm = min(tm, m)
    if tn is None:
        io_col = k * b.dtype.itemsize + tm * (sum(jnp.dtype(d).itemsize for d in out_dtypes) + sum(e.dtype.itemsize for e in epi_in))
        tmp_col = tm * 4 * (2 if epi is not None else 1)
        tn = _pick_tn(n, io_col, tmp_col, tm * k * a.dtype.itemsize)
    tn = min(tn, n)
    assert m % tm == 0 and n % tn == 0, (name, m, n, tm, tn)
    a_spec = pl.BlockSpec((k, tm), lambda i, j: (0, i)) if mode == "tn" else pl.BlockSpec((tm, k), lambda i, j: (i, 0))
    b_spec = pl.BlockSpec((tn, k), lambda i, j: (j, 0)) if mode == "nt" else pl.BlockSpec((k, tn), lambda i, j: (0, j))
    dims = dict(nn=NN, nt=NT, tn=TN)[mode]
    o_spec = pl.BlockSpec((tm, tn), lambda i, j: (i, j))
    n_e = len(epi_in)

    def body(a_ref, b_ref, *refs):
        acc = _raw_dot(a_ref[...], b_ref[...], dims)
        outs = epi(acc, *[r[...] for r in refs[:n_e]]) if epi is not None else (acc,)
        for ref, val in zip(refs[n_e:], outs):
            ref[...] = val.astype(ref.dtype)

    grid = (m // tm, n // tn)
    ride_in, ride_out, ride_sems = ([], [], []) if rider is None else (rider.inputs, rider.out_shape, rider.sems)
    any_spec = pl.BlockSpec(memory_space=pl.ANY)
    res = pl.pallas_call(_ride(body, rider, (0, 2 + n_e, len(out_dtypes), 0), grid), name=name, grid=grid,
                         in_specs=[a_spec, b_spec] + [o_spec] * n_e + [any_spec] * len(ride_in),
                         out_specs=[o_spec] * len(out_dtypes) + [any_spec] * len(ride_out),
                         out_shape=[jax.ShapeDtypeStruct((m, n), d) for d in out_dtypes] + ride_out, scratch_shapes=ride_sems,
                         compiler_params=_params(("parallel", "parallel") if rider is None else ("arbitrary", "arbitrary")))(
                             a, b, *epi_in, *ride_in)
    own = res[0] if len(out_dtypes) == 1 else res[:len(out_dtypes)]
    return own if rider is None else (own, res[len(out_dtypes):])


def _rms(x, g):
    return x * lax.rsqrt(jnp.mean(x * x, axis=-1, keepdims=True) + EPS) * g


def _rms_bwd(x, g, dy):
    _, vjp = jax.vjp(_rms, x, g)
    return vjp(dy)


def _rope_tables(t_len, half):
    dim = 2 * half
    inv = 1.0 / (ROPE_THETA ** (jnp.arange(0, dim, 2, dtype=F32) / dim))
    ang = jnp.arange(t_len, dtype=F32)[:, None] * inv[None, :]
    c, s = jnp.cos(ang), jnp.sin(ang)
    reps = 128 // dim
    return jnp.tile(jnp.concatenate([c, c], 1), (1, reps)), jnp.tile(jnp.concatenate([-s, s], 1), (1, reps))


def _rope(x, cos, sin, half):
    w = x.shape[-1]
    reps = w // 128
    if reps > 1:
        cos, sin = jnp.tile(cos, (1, reps)), jnp.tile(sin, (1, reps))
    lane = lax.broadcasted_iota(jnp.int32, x.shape, 1)
    partner = jnp.where((lane % (2 * half)) < half, pltpu.roll(x, w - half, 1), pltpu.roll(x, half, 1))
    return x * cos + partner * sin


SWA_GROUP = SWA_HEADS // SWA_KV
SWA_QB = 4
SWA_TB_FWD, SWA_TB_BWD = 4, 2


def _swa_mask(tb, first_block):
    rows, keys = SWA_GROUP * tb * BLOCK, (tb + 1) * BLOCK
    qi = lax.broadcasted_iota(jnp.int32, (rows, keys), 0) % (tb * BLOCK) + BLOCK
    kj = lax.broadcasted_iota(jnp.int32, (rows, keys), 1)
    diff = qi - kj
    band = (diff >= 0) & (diff < BLOCK)
    return band if first_block is False else band & (jnp.logical_not(first_block) | (kj >= BLOCK))


def _swa_stack(ref, j, b, tb, dtype=None):
    rows = slice(BLOCK * b, BLOCK * (b + tb))
    parts = [ref[rows, 64 * (SWA_GROUP * j + g):64 * (SWA_GROUP * j + g) + 64] for g in range(SWA_GROUP)]
    out = jnp.concatenate(parts, axis=0)
    return out if dtype is None else out.astype(dtype)


def _swa_sinks(s_ref, j, tb):
    return jnp.concatenate([jnp.broadcast_to(s_ref[SWA_GROUP * j + g:SWA_GROUP * j + g + 1, 0:1], (tb * BLOCK, 1))
                            for g in range(SWA_GROUP)], axis=0)


def _swa_head(q, kw, vw, sink, mask):
    sc = _dnt(q, kw) * (64 ** -0.5)
    sc = jnp.where(mask, sc, NEG_INF)
    m = lax.stop_gradient(jnp.maximum(jnp.max(sc, axis=-1, keepdims=True), sink))
    p = jnp.exp(sc - m)
    denom = jnp.sum(p, axis=-1, keepdims=True) + jnp.exp(sink - m)
    return _dnn(p / denom, vw)


def _swa_prep(proj, cos, sin):
    def fn(sq, sk, sv, c, s):
        return _rope(sq, c, s, 32), _rope(sk, c, s, 32), sv
    t_len = proj.shape[0]
    return _rowwise("swa_prep", fn, [_seg(proj, "sq"), _seg(proj, "sk"), _seg(proj, "sv"), cos, sin], [],
                    [((t_len, 512), BF16), ((t_len, 128), BF16), ((t_len, 128), BF16)])


def _swa_specs(t_len, tb_max=1):
    qb = min(SWA_QB, t_len // BLOCK)
    tb = min(tb_max, qb)
    rows = qb * BLOCK
    cur = lambda w: pl.BlockSpec((rows, w), lambda i: (i, 0))
    prev = pl.BlockSpec((BLOCK, 128), lambda i: (jnp.maximum(qb * i - 1, 0), 0))
    return qb, tb, t_len // rows, cur, prev


def _swa_fwd(q, k, v, sinks):
    t_len = q.shape[0]
    qb, tb, steps, cur, prev = _swa_specs(t_len, SWA_TB_FWD)

    def body(q_ref, kp_ref, kc_ref, vp_ref, vc_ref, s_ref, o_ref):
        first = pl.program_id(0) == 0
        k_all = jnp.concatenate([kp_ref[...], kc_ref[...]], axis=0)
        v_all = jnp.concatenate([vp_ref[...], vc_ref[...]], axis=0)
        for j in range(SWA_KV):
            cols = slice(64 * j, 64 * j + 64)
            sink = _swa_sinks(s_ref, j, tb)
            for b in range(0, qb, tb):
                window = slice(BLOCK * b, BLOCK * (b + tb + 1))
                mask = _swa_mask(tb, first if b == 0 else False)
                o = _swa_head(_swa_stack(q_ref, j, b, tb), k_all[window, cols], v_all[window, cols], sink, mask)
                for g in range(SWA_GROUP):
                    h = SWA_GROUP * j + g
                    o_ref[BLOCK * b:BLOCK * (b + tb), 64 * h:64 * h + 64] = o[tb * BLOCK * g:tb * BLOCK * (g + 1)].astype(o_ref.dtype)

    return pl.pallas_call(
        body, name="swa_fwd", grid=(steps,),
        in_specs=[cur(512), prev, cur(128), prev, cur(128), _const_spec((8, 128))],
        out_specs=cur(512), out_shape=jax.ShapeDtypeStruct((t_len, 512), BF16),
        compiler_params=_params(("parallel",)))(q, k, k, v, v, sinks)


def _swa_bwd(q, k, v, sinks, dy):
    t_len = q.shape[0]
    qb, tb, steps, cur, prev = _swa_specs(t_len, SWA_TB_BWD)

    def body(q_ref, kp_ref, kc_ref, vp_ref, vc_ref, s_ref, dy_ref, dq_ref, dk_ref, dv_ref, dkp_ref, dvp_ref, ds_ref):
        first = pl.program_id(0) == 0

        @pl.when(first)
        def _():
            ds_ref[...] = jnp.zeros_like(ds_ref)

        k_all = jnp.concatenate([kp_ref[...], kc_ref[...]], axis=0)
        v_all = jnp.concatenate([vp_ref[...], vc_ref[...]], axis=0)
        for j in range(SWA_KV):
            cols = slice(64 * j, 64 * j + 64)
            sink = _swa_sinks(s_ref, j, tb)
            dk_parts = [jnp.zeros((BLOCK, 64), F32) for _ in range(qb + 1)]
            dv_parts = [jnp.zeros((BLOCK, 64), F32) for _ in range(qb + 1)]
            dsink = jnp.zeros((SWA_GROUP * tb * BLOCK, 1), F32)
            for b in range(0, qb, tb):
                window = slice(BLOCK * b, BLOCK * (b + tb + 1))
                mask = _swa_mask(tb, first if b == 0 else False)
                _, vjp = jax.vjp(lambda a, b_, c, d, mask=mask: _swa_head(a, b_, c, d, mask), _swa_stack(q_ref, j, b, tb),
                                 k_all[window, cols], v_all[window, cols], sink)
                dq, dkw, dvw, dsink_b = vjp(_swa_stack(dy_ref, j, b, tb, F32))
                dkw, dvw = dkw.astype(F32), dvw.astype(F32)
                for r in range(tb + 1):
                    dk_parts[b + r] = dk_parts[b + r] + dkw[BLOCK * r:BLOCK * (r + 1)]
                    dv_parts[b + r] = dv_parts[b + r] + dvw[BLOCK * r:BLOCK * (r + 1)]
                dsink = dsink + dsink_b
                for g in range(SWA_GROUP):
                    h = SWA_GROUP * j + g
                    dq_ref[BLOCK * b:BLOCK * (b + tb), 64 * h:64 * h + 64] = dq[tb * BLOCK * g:tb * BLOCK * (g + 1)].astype(F32)
            dkp_ref[:, cols], dvp_ref[:, cols] = dk_parts[0], dv_parts[0]
            for b in range(qb):
                dk_ref[BLOCK * b:BLOCK * b + BLOCK, cols] = dk_parts[b + 1]
                dv_ref[BLOCK * b:BLOCK * b + BLOCK, cols] = dv_parts[b + 1]
            for g in range(SWA_GROUP):
                h = SWA_GROUP * j + g
                ds_ref[h:h + 1, :] += jnp.broadcast_to(jnp.sum(dsink[tb * BLOCK * g:tb * BLOCK * (g + 1)], axis=0, keepdims=True), (1, 128))

    part = pl.BlockSpec((BLOCK, 128), lambda i: (i, 0))
    return pl.pallas_call(
        body, name="swa_bwd", grid=(steps,),
        in_specs=[cur(512), prev, cur(128), prev, cur(128), _const_spec((8, 128)), cur(512)],
        out_specs=[cur(512), cur(128), cur(128), part, part, _const_spec((8, 128))],
        out_shape=[jax.ShapeDtypeStruct((t_len, 512), F32), jax.ShapeDtypeStruct((t_len, 128), F32),
                   jax.ShapeDtypeStruct((t_len, 128), F32), jax.ShapeDtypeStruct((steps * BLOCK, 128), F32),
                   jax.ShapeDtypeStruct((steps * BLOCK, 128), F32), jax.ShapeDtypeStruct((8, 128), F32)],
        compiler_params=_params(("arbitrary",)))(q, k, k, v, v, sinks, dy)


def _swa_unprep(dq, dk, dv, dk_prev, dv_prev, cos, sin):
    t_len = dq.shape[0]
    qb, _, steps, cur, _ = _swa_specs(t_len)
    tail = (qb - 1) * BLOCK

    def body(dq_ref, dk_ref, dv_ref, dkn_ref, dvn_ref, c_ref, s_ref, oq_ref, ok_ref, ov_ref):
        more = (pl.program_id(0) < steps - 1).astype(F32)
        cos_t, sin_t = c_ref[...], -s_ref[...]
        oq_ref[...] = _rope(dq_ref[...], cos_t, sin_t, 32).astype(BF16)
        dk_t = jnp.concatenate([dk_ref[:tail], dk_ref[tail:] + more * dkn_ref[...]], axis=0) if tail else dk_ref[...] + more * dkn_ref[...]
        dv_t = jnp.concatenate([dv_ref[:tail], dv_ref[tail:] + more * dvn_ref[...]], axis=0) if tail else dv_ref[...] + more * dvn_ref[...]
        ok_ref[...] = _rope(dk_t, cos_t, sin_t, 32).astype(BF16)
        ov_ref[...] = dv_t.astype(BF16)

    nxt = pl.BlockSpec((BLOCK, 128), lambda i: (jnp.minimum(i + 1, steps - 1), 0))
    return pl.pallas_call(
        body, name="swa_unprep", grid=(steps,),
        in_specs=[cur(512), cur(128), cur(128), nxt, nxt, cur(128), cur(128)],
        out_specs=[cur(512), cur(128), cur(128)],
        out_shape=[jax.ShapeDtypeStruct((t_len, 512), BF16), jax.ShapeDtypeStruct((t_len, 128), BF16),
                   jax.ShapeDtypeStruct((t_len, 128), BF16)],
        compiler_params=_params(("parallel",)))(dq, dk, dv, dk_prev, dv_prev, cos, sin)


def _s5_discretize(lam_re, lam_im, log_dt, b_re, b_im):
    dt = jnp.exp(log_dt)
    mag = jnp.exp(lam_re * dt)
    ab_re, ab_im = mag * jnp.cos(lam_im * dt), mag * jnp.sin(lam_im * dt)
    den = lam_re * lam_re + lam_im * lam_im
    nr, ni = ab_re - 1.0, ab_im
    f_re = (nr * lam_re + ni * lam_im) / den
    f_im = (ni * lam_re - nr * lam_im) / den
    return ab_re, ab_im, f_re * b_re - f_im * b_im, f_re * b_im + f_im * b_re


def _s5_param_fwd(lam_re, lam_im, log_dt, b_re, b_im):
    def body(*refs):
        outs = _s5_discretize(*[r[...] for r in refs[:5]])
        for ref, val in zip(refs[5:], outs):
            ref[...] = val
    n = S5_CH
    return pl.pallas_call(body, name="s5_param_fwd",
                          out_shape=[jax.ShapeDtypeStruct((n, 1), F32)] * 2 + [jax.ShapeDtypeStruct((n, 16), F32)] * 2,
                          )(lam_re, lam_im, log_dt, b_re, b_im)


def _s5_param_bwd(lam_re, lam_im, log_dt, b_re, b_im, group_sum, d_ab_re, d_ab_im, d_bb_re, d_bb_im):
    def body(*refs):
        ins = [r[...] for r in refs[:10]]
        _, vjp = jax.vjp(_s5_discretize, *ins[:5])
        d_lr, d_li, d_ldt, d_br, d_bi = vjp(tuple(ins[6:10]))
        o_lr, o_li, o_ldt, o_br, o_bi = refs[10:]
        o_lr[...], o_li[...], o_br[...], o_bi[...] = d_lr, d_li, d_br, d_bi
        o_ldt[...] = lax.dot_general(ins[5], jnp.broadcast_to(d_ldt, (S5_CH, 128)), NN,
                                     precision=lax.Precision.HIGHEST, preferred_element_type=F32)
    n = S5_CH
    return pl.pallas_call(body, name="s5_param_bwd",
                          out_shape=[jax.ShapeDtypeStruct((n, 1), F32), jax.ShapeDtypeStruct((n, 1), F32),
                                     jax.ShapeDtypeStruct((S5_GROUPS, 128), F32),
                                     jax.ShapeDtypeStruct((n, 16), F32), jax.ShapeDtypeStruct((n, 16), F32)],
                          )(lam_re, lam_im, log_dt, b_re, b_im, group_sum, d_ab_re, d_ab_im, d_bb_re, d_bb_im)


def _s5_expand(name, u, w):
    t_len = u.shape[0]
    tile = min(512, t_len)

    def body(u_ref, w_ref, o_ref):
        o_ref[...] = _raw_dot(u_ref[...], w_ref[...], NN)

    return pl.pallas_call(
        body, name=name, grid=(t_len // tile, 8),
        in_specs=[pl.BlockSpec((tile, 128), lambda i, bb: (i, bb % 4)), pl.BlockSpec((None, 128, 512), lambda i, bb: (bb, 0, 0))],
        out_specs=pl.BlockSpec((None, tile, 512), lambda i, bb: (bb // 4, i, bb % 4)),
        out_shape=jax.ShapeDtypeStruct((2, t_len, S5_CH), F32), compiler_params=_params(("parallel", "parallel")))(u, w)


def _s5_contract(name, x, w):
    t_len = x.shape[1]
    tile = min(512, t_len)

    def body(xr_ref, xi_ref, wr_ref, wi_ref, o_ref):
        o_ref[...] = _raw_dot(xr_ref[...], wr_ref[...], NN) + _raw_dot(xi_ref[...], wi_ref[...], NN)

    return pl.pallas_call(
        body, name=name, grid=(t_len // tile, 4),
        in_specs=[pl.BlockSpec((None, tile, 512), lambda i, b: (0, i, b)), pl.BlockSpec((None, tile, 512), lambda i, b: (1, i, b)),
                  pl.BlockSpec((None, 512, 128), lambda i, b: (b, 0, 0)), pl.BlockSpec((None, 512, 128), lambda i, b: (4 + b, 0, 0))],
        out_specs=pl.BlockSpec((tile, 128), lambda i, b: (i, b)),
        out_shape=jax.ShapeDtypeStruct((t_len, 512), F32), compiler_params=_params(("parallel", "parallel")))(x, x, w, w)


def _s5_band_grad(name, narrow, wide, narrow_first):
    t_len = narrow.shape[0]

    def body(n_ref, w_ref, o_ref):
        if narrow_first:
            o_ref[...] = _raw_dot(n_ref[...], w_ref[...], TN)
        else:
            o_ref[...] = _raw_dot(w_ref[...], n_ref[...], TN)

    shape = (8, 128, 512) if narrow_first else (8, 512, 128)
    return pl.pallas_call(
        body, name=name, grid=(8,),
        in_specs=[pl.BlockSpec((t_len, 128), lambda bb: (0, bb % 4)), pl.BlockSpec((None, t_len, 512), lambda bb: (bb // 4, 0, bb % 4))],
        out_specs=pl.BlockSpec((None,) + shape[1:], lambda bb: (bb, 0, 0)),
        out_shape=jax.ShapeDtypeStruct(shape, F32), compiler_params=_params(("parallel",)))(narrow, wide)


def _s5_scan(name, bu, a_re, a_im, states=None, carry_in=None):
    t_len = bu.shape[1]
    n = t_len // SCAN_SEG
    log_n = int(round(math.log2(n)))
    assert 2 ** log_n == n
    reverse = states is not None
    lanes = SCAN_LANES // 2 if reverse else SCAN_LANES
    nblk = S5_CH // lanes

    def body(*refs):
        if reverse:
            b_ref, ar_ref, ai_ref, x_ref, cin_ref, o_ref, dar_ref, dai_ref, c_scr = refs
            xr_ref, xi_ref, cr_ref, ci_ref = x_ref.at[0], x_ref.at[1], cin_ref.at[0], cin_ref.at[1]
        else:
            b_ref, ar_ref, ai_ref, o_ref, co_ref, c_scr = refs
            cor_ref, coi_ref = co_ref.at[0], co_ref.at[1]
        br_ref, bi_ref, or_ref, oi_ref = b_ref.at[0], b_ref.at[1], o_ref.at[0], o_ref.at[1]
        ar = ar_ref[...]
        ai = -ai_ref[...] if reverse else ai_ref[...]

        def tile_at(i):
            return pl.multiple_of((n - 1 - i if reverse else i) * SCAN_SEG, SCAN_SEG)

        def advance(s, row):
            sr, si = s
            return (ar * sr - ai * si + br_ref[pl.ds(row, SCAN_SEG), :], ar * si + ai * sr + bi_ref[pl.ds(row, SCAN_SEG), :])

        zero = jnp.zeros((SCAN_SEG, lanes), F32)
        fin_r, fin_i = lax.fori_loop(0, n, lambda i, s: advance(s, tile_at(i)), (zero, zero), unroll=8)
        pr, pi = ar[0:1], ai[0:1]
        for _ in range(log_n):
            pr, pi = pr * pr - pi * pi, 2.0 * pr * pi
        er = ei = jnp.zeros((1, lanes), F32)
        for step in range(SCAN_SEG):
            j = SCAN_SEG - 1 - step if reverse else step
            c_scr[0, j:j + 1, :] = er
            c_scr[1, j:j + 1, :] = ei
            er, ei = pr * er - pi * ei + fin_r[j:j + 1], pr * ei + pi * er + fin_i[j:j + 1]
        entry = (c_scr[0], c_scr[1])

        if not reverse:
            cor_ref[...], coi_ref[...] = entry

            def step2(i, s):
                row = tile_at(i)
                s = advance(s, row)
                or_ref[pl.ds(row, SCAN_SEG), :] = s[0]
                oi_ref[pl.ds(row, SCAN_SEG), :] = s[1]
                return s
            lax.fori_loop(0, n, step2, entry, unroll=8)
        else:
            def emit(s, row, xr_prev, xi_prev, acc):
                s = advance(s, row)
                or_ref[pl.ds(row, SCAN_SEG), :] = s[0]
                oi_ref[pl.ds(row, SCAN_SEG), :] = s[1]
                return s, (acc[0] + s[0] * xr_prev + s[1] * xi_prev, acc[1] + s[1] * xr_prev - s[0] * xi_prev)

            def step2(i, carry):
                s, acc = carry
                row = tile_at(i)
                prev = pl.multiple_of(row - SCAN_SEG, SCAN_SEG)
                return emit(s, row, xr_ref[pl.ds(prev, SCAN_SEG), :], xi_ref[pl.ds(prev, SCAN_SEG), :], acc)
            s, acc = lax.fori_loop(0, n - 1, step2, (entry, (zero, zero)), unroll=8)
            s, acc = emit(s, 0, cr_ref[...], ci_ref[...], acc)
            dar_ref[...] = jnp.sum(acc[0], axis=0, keepdims=True)
            dai_ref[...] = jnp.sum(acc[1], axis=0, keepdims=True)

    planes = pl.BlockSpec((2, t_len, lanes), lambda b: (0, 0, b))
    tile8 = pl.BlockSpec((SCAN_SEG, lanes), lambda b: (0, b))
    entry_spec = pl.BlockSpec((2, SCAN_SEG, lanes), lambda b: (0, 0, b))
    row1 = pl.BlockSpec((1, lanes), lambda b: (0, b))
    seq = jax.ShapeDtypeStruct((2, t_len, S5_CH), F32)
    if reverse:
        in_specs = [planes, tile8, tile8, planes, entry_spec]
        args = (bu, a_re, a_im, states, carry_in)
        out_specs = [planes, row1, row1]
        out_shape = [seq, jax.ShapeDtypeStruct((1, S5_CH), F32), jax.ShapeDtypeStruct((1, S5_CH), F32)]
    else:
        in_specs = [planes, tile8, tile8]
        args = (bu, a_re, a_im)
        out_specs = [planes, entry_spec]
        out_shape = [seq, jax.ShapeDtypeStruct((2, SCAN_SEG, S5_CH), F32)]
    return pl.pallas_call(body, name=name, grid=(nblk,), in_specs=in_specs, out_specs=out_specs, out_shape=out_shape,
                          scratch_shapes=[pltpu.VMEM((2, SCAN_SEG, lanes), F32)],
                          compiler_params=_params(("parallel",)))(*args)


def _ret_consts():
    h = np.arange(RET_HEADS, dtype=np.float32)
    log_gamma = np.log1p(-np.exp2(-5.0 - h)).astype(np.float32)
    idx = np.arange(BLOCK, dtype=np.float32)
    diff = idx[:, None] - idx[None, :]
    decay = np.where(diff >= 0, np.exp(log_gamma[:, None, None] * np.maximum(diff, 0.0)), 0.0).astype(np.float32)
    k_w = np.exp(log_gamma[:, None] * (BLOCK - 1 - idx)[None, :]).astype(np.float32)[:, :, None]
    q_w = np.exp(log_gamma[:, None] * (idx + 1.0)[None, :]).astype(np.float32)[:, :, None]
    chunk_decay = [float(v) for v in np.exp(log_gamma * BLOCK).astype(np.float32)]
    return jnp.asarray(decay), jnp.asarray(q_w), jnp.asarray(k_w), chunk_decay


def _ret_head(q, k, v, g, prev, decay, q_w, k_w, chunk_decay):
    inner_s = _dnt(q, k) * decay
    y = _dnn(inner_s, v) + _dnn(q * q_w, prev)
    mu = jnp.mean(y, axis=-1, keepdims=True)
    var = jnp.mean(jnp.square(y - mu), axis=-1, keepdims=True)
    yn = (y - mu) * lax.rsqrt(var + EPS)
    out = g * jax.nn.sigmoid(g) * yn
    return out, prev * chunk_decay + _dtn(k * k_w, v)


def _ret_prep(proj, cos, sin):
    def fn(rq, rk, c, s):
        return _rope(rq, c, s, 32), _rope(rk, c, s, 32) * (RET_DK ** -0.5)
    t_len = proj.shape[0]
    return _rowwise("ret_prep", fn, [_seg(proj, "rq"), _seg(proj, "rk"), cos, sin], [],
                    [((t_len, 256), F32), ((t_len, 256), F32)])


def _ret_unprep(dq, dk, cos, sin):
    def fn(dq_t, dk_t, c, s):
        return _rope(dq_t, c, -s, 32), _rope(dk_t * (RET_DK ** -0.5), c, -s, 32)
    t_len = dq.shape[0]
    return _rowwise("ret_unprep", fn, [dq, dk, cos, sin], [], [((t_len, 256), BF16), ((t_len, 256), BF16)])


def _ret_fwd(q, k, proj):
    t_len = q.shape[0]
    nc = t_len // BLOCK
    decay, q_w, k_w, chunk_decay = _ret_consts()

    def body(q_ref, k_ref, v_ref, g_ref, dec_ref, qw_ref, kw_ref, o_ref, prev_ref, state):
        @pl.when(pl.program_id(0) == 0)
        def _():
            state[...] = jnp.zeros_like(state)
        for h in range(RET_HEADS):
            qk, vv = slice(64 * h, 64 * h + 64), slice(128 * h, 128 * h + 128)
            prev = state[h]
            prev_ref[h] = prev
            out, nxt = _ret_head(q_ref[:, qk], k_ref[:, qk], v_ref[:, vv], g_ref[:, vv], prev,
                                 dec_ref[h], qw_ref[h], kw_ref[h], chunk_decay[h])
            o_ref[:, vv] = out.astype(o_ref.dtype)
            state[h] = nxt

    chunk = lambda w, c=0: pl.BlockSpec((BLOCK, w), lambda i, c=c: (i, c))
    return pl.pallas_call(
        body, name="ret_fwd", grid=(nc,),
        in_specs=[chunk(256), chunk(256), chunk(512, PROJ["rv"][0] // 512), chunk(512, PROJ["rg"][0] // 512),
                  _const_spec(decay.shape), _const_spec(q_w.shape), _const_spec(k_w.shape)],
        out_specs=[chunk(512), pl.BlockSpec((None, RET_HEADS, RET_DK, RET_DV), lambda i: (i, 0, 0, 0))],
        out_shape=[jax.ShapeDtypeStruct((t_len, 512), BF16), jax.ShapeDtypeStruct((nc, RET_HEADS, RET_DK, RET_DV), F32)],
        scratch_shapes=[pltpu.VMEM((RET_HEADS, RET_DK, RET_DV), F32)],
        compiler_params=_params(("arbitrary",)))(q, k, proj, proj, decay, q_w, k_w)


def _ret_bwd(q, k, proj, prevs, dy):
    t_len = q.shape[0]
    nc = t_len // BLOCK
    decay, q_w, k_w, chunk_decay = _ret_consts()

    def body(q_ref, k_ref, v_ref, g_ref, prev_ref, dy_ref, dec_ref, qw_ref, kw_ref, dq_ref, dk_ref, dv_ref, dg_ref, dstate):
        @pl.when(pl.program_id(0) == 0)
        def _():
            dstate[...] = jnp.zeros_like(dstate)
        for h in range(RET_HEADS):
            qk, vv = slice(64 * h, 64 * h + 64), slice(128 * h, 128 * h + 128)
            head = functools.partial(_ret_head, decay=dec_ref[h], q_w=qw_ref[h], k_w=kw_ref[h], chunk_decay=chunk_decay[h])
            _, vjp = jax.vjp(head, q_ref[:, qk], k_ref[:, qk], v_ref[:, vv], g_ref[:, vv], prev_ref[h])
            dq, dk, dv, dg, dprev = vjp((dy_ref[:, vv].astype(F32), dstate[h]))
            dq_ref[:, qk], dk_ref[:, qk] = dq, dk
            dv_ref[:, vv], dg_ref[:, vv] = dv.astype(dv_ref.dtype), dg.astype(dg_ref.dtype)
            dstate[h] = dprev

    chunk = lambda w, c=0: pl.BlockSpec((BLOCK, w), lambda i, c=c: (nc - 1 - i, c))
    return pl.pallas_call(
        body, name="ret_bwd", grid=(nc,),
        in_specs=[chunk(256), chunk(256), chunk(512, PROJ["rv"][0] // 512), chunk(512, PROJ["rg"][0] // 512),
                  pl.BlockSpec((None, RET_HEADS, RET_DK, RET_DV), lambda i: (nc - 1 - i, 0, 0, 0)), chunk(512),
                  _const_spec(decay.shape), _const_spec(q_w.shape), _const_spec(k_w.shape)],
        out_specs=[chunk(256), chunk(256), chunk(512), chunk(512)],
        out_shape=[jax.ShapeDtypeStruct((t_len, 256), F32), jax.ShapeDtypeStruct((t_len, 256), F32),
                   jax.ShapeDtypeStruct((t_len, 512), BF16), jax.ShapeDtypeStruct((t_len, 512), BF16)],
        scratch_shapes=[pltpu.VMEM((RET_HEADS, RET_DK, RET_DV), F32)],
        compiler_params=_params(("arbitrary",)))(q, k, proj, proj, prevs, dy, decay, q_w, k_w)


MLA_SCALE = 96 ** -0.5


def _mla_prep(qf, kvf, proj, cos, sin):
    def fn(q, kv, kr, c, s):
        q_rope = _rope(q[:, 512:768], c, s, 16)
        k_rope = _rope(kr, c, s, 16)[:, :32]
        zero = jnp.zeros_like(k_rope)
        qs = [jnp.concatenate([q[:, 64 * h:64 * h + 64], q_rope[:, 32 * h:32 * h + 32], zero], axis=1) for h in range(MLA_HEADS)]
        ks = [jnp.concatenate([kv[:, 64 * h:64 * h + 64], k_rope, zero], axis=1) for h in range(MLA_HEADS)]
        ones = (lax.broadcasted_iota(jnp.int32, (kv.shape[0], 64), 1) == 0).astype(F32)
        vs = [jnp.concatenate([kv[:, 512 + 64 * h:576 + 64 * h], ones], axis=1) for h in range(MLA_HEADS)]
        return qs, ks, vs
    t_len = qf.shape[0]
    return _rowwise("mla_prep", fn, [qf, kvf, _seg(proj, "kr"), cos, sin], [],
                    [((8, t_len, 128), BF16), ((8, t_len, 128), BF16), ((8, t_len, 128), BF16)])


def _mla_unprep(dq, dk, dv, cos, sin):
    def fn(dq_t, dk_t, dv_t, c, s):
        q_rope = _rope(jnp.concatenate([dq_t[h][:, 64:96] for h in range(MLA_HEADS)], axis=1), c, -s, 16)
        d_qf = jnp.concatenate([dq_t[h][:, :64] for h in range(MLA_HEADS)] + [q_rope], axis=1)
        d_kvf = jnp.concatenate([dk_t[h][:, :64] for h in range(MLA_HEADS)] + [dv_t[h][:, :64] for h in range(MLA_HEADS)], axis=1)
        k_rope = dk_t[0][:, 64:96]
        for h in range(1, MLA_HEADS):
            k_rope = k_rope + dk_t[h][:, 64:96]
        d_kr = _rope(jnp.concatenate([k_rope, jnp.zeros((k_rope.shape[0], 96), F32)], axis=1), c, -s, 16)
        return d_qf, d_kvf, d_kr
    t_len = dq.shape[1]
    return _rowwise("mla_unprep", fn, [dq, dk, dv, cos, sin], [],
                    [((t_len, 768), BF16), ((t_len, 1024), BF16), ((t_len, 128), BF16)])


MLA_EXP2 = MLA_SCALE * math.log2(math.e)
MLA_TILE = 1024


def _diag_mask(tq):
    return lax.broadcasted_iota(jnp.int32, (tq, tq), 1) <= lax.broadcasted_iota(jnp.int32, (tq, tq), 0)


def _tri_steps(nq, key_major):
    if key_major:
        pairs = [(i, j) for j in range(nq) for i in range(j, nq)]
    else:
        pairs = [(i, j) for i in range(nq) for j in range(i + 1)]
    return jnp.asarray([p[0] for p in pairs], jnp.int32), jnp.asarray([p[1] for p in pairs], jnp.int32)


def _ride(body, rider, counts, grid):
    if rider is None:
        return body
    extra = (0, len(rider.inputs), len(rider.out_shape), len(rider.sems))

    def wrapped(*refs):
        host, ridden, pos = [], [], 0
        for mine, theirs in zip(counts, extra):
            host += refs[pos:pos + mine]
            ridden.append(refs[pos + mine:pos + mine + theirs])
            pos += mine + theirs
        ids = [pl.program_id(a) for a in range(len(grid))]
        first, last = ids[0] == 0, ids[0] == grid[0] - 1
        for a in range(1, len(grid)):
            first, last = first & (ids[a] == 0), last & (ids[a] == grid[a] - 1)

        @pl.when(first)
        def _():
            rider.start(*ridden[1:])

        body(*host)

        @pl.when(last)
        def _():
            rider.finish(*ridden[1:])

    return wrapped


def _mla_fwd(q, k, v, rider=None):
    t_len = q.shape[1]
    tq = min(MLA_TILE, t_len)
    nq = t_len // tq
    qi, kj = _tri_steps(nq, False)

    def body(i_ref, j_ref, q_ref, k_ref, v_ref, o_ref, lse_ref, m_s, acc_s):
        step = pl.program_id(1)
        i, j = i_ref[step], j_ref[step]

        @pl.when(j == 0)
        def _():
            m_s[...] = jnp.full_like(m_s, NEG_INF)
            acc_s[...] = jnp.zeros_like(acc_s)

        def update(mask):
            for e in range(2):
                s = _raw_dot(q_ref[e], k_ref[e], NT)
                if mask is not None:
                    s = jnp.where(mask, s, NEG_INF)
                m_old = m_s[e]
                m_new = jnp.maximum(m_old, jnp.max(s, axis=-1, keepdims=True))
                p = jnp.exp2((s - m_new) * MLA_EXP2)
                acc_s[e] = jnp.exp2((m_old - m_new) * MLA_EXP2) * acc_s[e] + _raw_dot(p, v_ref[e], NN)
                m_s[e] = m_new

        @pl.when(j < i)
        def _():
            update(None)

        @pl.when(j == i)
        def _():
            update(_diag_mask(tq))
            outs = []
            for e in range(2):
                acc = acc_s[e]
                denom = acc[:, 64:65]
                outs.append(acc[:, :64] / denom)
                lse_ref[e] = m_s[e] * MLA_EXP2 + jnp.log2(denom)
            o_ref[...] = jnp.concatenate(outs, axis=1)

    grid = (MLA_HEADS // 2, int(qi.shape[0]))
    ride_in, ride_out, ride_sems = ([], [], []) if rider is None else (rider.inputs, rider.out_shape, rider.sems)
    grid_spec = pltpu.PrefetchScalarGridSpec(
        num_scalar_prefetch=2, grid=grid,
        in_specs=[pl.BlockSpec((2, tq, 128), lambda hp, s, qi, kj: (hp, qi[s], 0)),
                  pl.BlockSpec((2, tq, 128), lambda hp, s, qi, kj: (hp, kj[s], 0)),
                  pl.BlockSpec((2, tq, 128), lambda hp, s, qi, kj: (hp, kj[s], 0))] + [ANY_SPEC] * len(ride_in),
        out_specs=[pl.BlockSpec((tq, 128), lambda hp, s, qi, kj: (qi[s], hp)),
                   pl.BlockSpec((2, tq, 1), lambda hp, s, qi, kj: (hp, qi[s], 0))] + [ANY_SPEC] * len(ride_out),
        scratch_shapes=[pltpu.VMEM((2, tq, 1), F32), pltpu.VMEM((2, tq, 128), F32)] + ride_sems)
    res = pl.pallas_call(
        _ride(body, rider, (2, 3, 2, 2), grid), name="mla_fwd" if rider is None else "mla_fwd_gather", grid_spec=grid_spec,
        out_shape=[jax.ShapeDtypeStruct((t_len, 512), F32), jax.ShapeDtypeStruct((MLA_HEADS, t_len, 1), F32)] + ride_out,
        compiler_params=_params(("parallel" if rider is None else "arbitrary", "arbitrary")))(qi, kj, q, k, v, *ride_in)
    return res[0], res[1], res[2:]


def _mla_bwd_prep(o, dy):
    def fn(o_t, dy_t):
        dos, deltas = [], []
        for h in range(MLA_HEADS):
            d_h = dy_t[:, 64 * h:64 * h + 64].astype(F32)
            dos.append(jnp.concatenate([d_h, jnp.zeros_like(d_h)], axis=1))
            deltas.append(jnp.sum(d_h * o_t[:, 64 * h:64 * h + 64], axis=-1, keepdims=True))
        return dos, deltas
    t_len = o.shape[0]
    return _rowwise("mla_bwd_prep", fn, [o, dy], [], [((8, t_len, 128), BF16), ((8, t_len, 1), F32)])


def _mla_bwd(q, k, v, do, lse2, delta, rider=None):
    t_len = q.shape[1]
    tq = min(MLA_TILE, t_len)
    nq = t_len // tq
    qi, kj = _tri_steps(nq, True)

    def body(i_ref, j_ref, q_ref, k_ref, v_ref, do_ref, lse_ref, dl_ref, dq_ref, dk_ref, dv_ref, k_acc, v_acc):
        step = pl.program_id(1)
        i, j = i_ref[step], j_ref[step]

        @pl.when(step == 0)
        def _():
            dq_ref[...] = jnp.zeros_like(dq_ref)

        @pl.when(i == j)
        def _():
            k_acc[...] = jnp.zeros_like(k_acc)
            v_acc[...] = jnp.zeros_like(v_acc)

        def tile(mask):
            p = jnp.exp2(_raw_dot(q_ref[...], k_ref[...], NT) * MLA_EXP2 - lse_ref[...])
            if mask is not None:
                p = jnp.where(mask, p, 0.0)
            ds = p * (_raw_dot(do_ref[...], v_ref[...], NT) - dl_ref[...]) * MLA_SCALE
            v_acc[...] += _raw_dot(p, do_ref[...], TN)
            k_acc[...] += _raw_dot(ds, q_ref[...], TN)
            rows = pl.ds(pl.multiple_of(i * tq, tq), tq)
            dq_ref[rows, :] += _raw_dot(ds, k_ref[...], NN)

        @pl.when(i == j)
        def _():
            tile(_diag_mask(tq))

        @pl.when(i > j)
        def _():
            tile(None)

        @pl.when(i == nq - 1)
        def _():
            dk_ref[...] = k_acc[...]
            dv_ref[...] = v_acc[...]

    q_blk = lambda w: pl.BlockSpec((None, tq, w), lambda h, s, qi, kj: (h, qi[s], 0))
    k_blk = lambda w: pl.BlockSpec((None, tq, w), lambda h, s, qi, kj: (h, kj[s], 0))
    grid = (MLA_HEADS, int(qi.shape[0]))
    ride_in, ride_out, ride_sems = ([], [], []) if rider is None else (rider.inputs, rider.out_shape, rider.sems)
    grid_spec = pltpu.PrefetchScalarGridSpec(
        num_scalar_prefetch=2, grid=grid,
        in_specs=[q_blk(128), k_blk(128), k_blk(128), q_blk(128), q_blk(1), q_blk(1)] + [ANY_SPEC] * len(ride_in),
        out_specs=[pl.BlockSpec((None, t_len, 128), lambda h, s, qi, kj: (h, 0, 0)), k_blk(128), k_blk(128)]
        + [ANY_SPEC] * len(ride_out),
        scratch_shapes=[pltpu.VMEM((tq, 128), F32), pltpu.VMEM((tq, 128), F32)] + ride_sems)
    res = pl.pallas_call(
        _ride(body, rider, (2, 6, 3, 2), grid), name="mla_bwd" if rider is None else "mla_bwd_scatter", grid_spec=grid_spec,
        out_shape=[jax.ShapeDtypeStruct((MLA_HEADS, t_len, 128), F32)] * 3 + ride_out,
        compiler_params=_params(("parallel" if rider is None else "arbitrary", "arbitrary")))(qi, kj, q, k, v, do, lse2, delta, *ride_in)
    return res[0], res[1], res[2], res[3:]


MERGE_TN = 256
GATE_BLOCK0 = PROJ["gates"][0] // MERGE_TN


def _merge_specs(tm):
    y_spec = pl.BlockSpec((tm, 512), lambda i, j: (i, 0))
    w_spec = pl.BlockSpec((512, MERGE_TN), lambda i, j: (0, j))
    gate = lambda b: pl.BlockSpec((tm, MERGE_TN), lambda i, j, b=b: (i, GATE_BLOCK0 + 4 * b + j))
    return [y_spec] * 4 + [w_spec] * 4 + [gate(b) for b in range(4)]


def _merge_fwd(ys, wbs, proj):
    t_len = proj.shape[0]
    tm = min(512, t_len)

    def body(*refs):
        y, w, gl, o_ref = refs[0:4], refs[4:8], refs[8:12], refs[12]
        acc = jnp.zeros((tm, MERGE_TN), F32)
        for b in range(4):
            acc += jax.nn.sigmoid(gl[b][...]) * _raw_dot(y[b][...], w[b][...], NN)
        o_ref[...] = acc.astype(o_ref.dtype)

    return pl.pallas_call(body, name="merge_fwd", grid=(t_len // tm, D_MODEL // MERGE_TN), in_specs=_merge_specs(tm),
                          out_specs=pl.BlockSpec((tm, MERGE_TN), lambda i, j: (i, j)),
                          out_shape=jax.ShapeDtypeStruct((t_len, D_MODEL), BF16),
                          compiler_params=_params(("parallel", "parallel")))(*ys, *wbs, *([proj] * 4))


def _merge_bwd(ys, wbs, proj, dmerged):
    t_len = proj.shape[0]
    tm = min(512, t_len)

    def body(*refs):
        y, w, gl, dm_ref, dgl, dt = refs[0:4], refs[4:8], refs[8:12], refs[12], refs[13:17], refs[17:21]
        dm = dm_ref[...]
        for b in range(4):
            gate = jax.nn.sigmoid(gl[b][...])
            t_b = _raw_dot(y[b][...], w[b][...], NN)
            dgl[b][...] = (dm * t_b * gate * (1.0 - gate)).astype(BF16)
            dt[b][...] = (dm * gate).astype(BF16)

    o_spec = pl.BlockSpec((tm, MERGE_TN), lambda i, j: (i, j))
    return pl.pallas_call(body, name="merge_bwd", grid=(t_len // tm, D_MODEL // MERGE_TN), in_specs=_merge_specs(tm) + [o_spec],
                          out_specs=[o_spec] * 8, out_shape=[jax.ShapeDtypeStruct((t_len, D_MODEL), BF16)] * 8,
                          compiler_params=_params(("parallel", "parallel")))(*ys, *wbs, *([proj] * 4), dmerged)


def _adamw(name, w, g, m, v, slabs):
    shape = w.shape
    size = int(np.prod(shape))
    if shape[-1] >= 96:
        view = (size // shape[-1], shape[-1])
    elif size % 128 == 0:
        view = (size // 128, 128)
    else:
        view = (1, size)
    rows = view[0]
    tile = rows
    for cand in (512, 256, 128, 64):
        if rows > cand and rows % cand == 0 and cand * view[1] * 4 <= 2 ** 20:
            tile = cand
            break
    if rows * view[1] * 4 <= 2 ** 20:
        tile = rows
    c1, c2 = 1.0 - ADAM_B1 ** ADAM_STEP, 1.0 - ADAM_B2 ** ADAM_STEP

    def body(w_ref, g_ref, m_ref, v_ref, og_ref, od_ref, om_ref, ov_ref):
        if slabs:
            grad = g_ref[0].astype(F32)
            for d in range(1, slabs):
                grad = grad + g_ref[d].astype(F32)
        else:
            grad = g_ref[...]
        m_new = ADAM_B1 * m_ref[...] + (1.0 - ADAM_B1) * grad
        v_new = ADAM_B2 * v_ref[...] + (1.0 - ADAM_B2) * jnp.square(grad)
        og_ref[...] = grad
        od_ref[...] = -ADAM_LR * ((m_new / c1) / (jnp.sqrt(v_new / c2) + ADAM_EPS) + ADAM_WD * w_ref[...])
        om_ref[...] = m_new
        ov_ref[...] = v_new

    spec = pl.BlockSpec((tile, view[1]), lambda i: (i, 0))
    g_spec = pl.BlockSpec((slabs, tile, view[1]), lambda i: (0, i, 0)) if slabs else spec
    g_view = g.reshape((slabs,) + view) if slabs else g.reshape(view)
    outs = pl.pallas_call(body, name=name, grid=(rows // tile,), in_specs=[spec, g_spec, spec, spec], out_specs=[spec] * 4,
                          out_shape=[jax.ShapeDtypeStruct(view, F32)] * 4,
                          compiler_params=_params(("parallel",)))(w.reshape(view), g_view, m.reshape(view), v.reshape(view))
    return [o.reshape(shape) for o in outs]


def _sum_slabs(name, x):
    def body(x_ref, o_ref):
        acc = x_ref[0]
        for d in range(1, N_DEV):
            acc = acc + x_ref[d]
        o_ref[...] = acc
    return pl.pallas_call(body, name=name, out_shape=jax.ShapeDtypeStruct(x.shape[1:], x.dtype))(x)


MESH_ID = pl.DeviceIdType.MESH
ANY_SPEC = pl.BlockSpec(memory_space=pl.ANY)


def _remote(src, dst, send_sems, recv_sems, k, to):
    return pltpu.make_async_remote_copy(src_ref=src, dst_ref=dst, send_sem=send_sems.at[k], recv_sem=recv_sems.at[k],
                                        device_id=to, device_id_type=MESH_ID)


class _Rider:
    def __init__(self, inputs, out_shape, sems, start, finish):
        self.inputs, self.out_shape, self.sems, self.start, self.finish = list(inputs), list(out_shape), list(sems), start, finish


def _run_rider(name, rider):
    n_in, n_out = len(rider.inputs), len(rider.out_shape)

    def body(*refs):
        parts = refs[:n_in], refs[n_in:n_in + n_out], refs[n_in + n_out:]
        rider.start(*parts)
        rider.finish(*parts)

    return pl.pallas_call(body, name=name, in_specs=[ANY_SPEC] * n_in, out_specs=[ANY_SPEC] * n_out, out_shape=rider.out_shape,
                          scratch_shapes=rider.sems, compiler_params=pltpu.CompilerParams(has_side_effects=True))(*rider.inputs)


def _gather_rider(blocks):
    n = len(blocks)

    def plan(srcs, dsts, sems, waits):
        send_sems, recv_sems, local_sems = sems
        x, y, c = lax.axis_index("x"), lax.axis_index("y"), lax.axis_index("c")
        sibling = (x, y, 1 - c)
        chips = [(1 - x, y), (x, 1 - y), (1 - x, 1 - y)]
        index = lambda px, py, pc: 4 * px + 2 * py + pc

        def copy(t, k, block, to, src=None):
            slot = dsts[t].at[block]
            return _remote(slot if src is None else src, slot, send_sems, recv_sems, 7 * t + k, to)

        me = index(x, y, c)
        own = [pltpu.make_async_copy(srcs[t], dsts[t].at[me], local_sems.at[t]) for t in range(n)]
        first = []
        for t in range(n):
            first.append(copy(t, 0, me, sibling, src=srcs[t]))
            first += [copy(t, 1 + j, me, (*chip, c), src=srcs[t]) for j, chip in enumerate(chips)]
        if not waits:
            return own, first, [], []
        relays = [(copy(t, 1 + j, index(*chip, c), sibling), copy(t, 4 + j, index(*chip, c), sibling))
                  for j, chip in enumerate(chips) for t in range(n)]
        from_sibling = [copy(t, 0, index(x, y, 1 - c), sibling) for t in range(n)]
        from_sibling += [copy(t, 4 + j, index(*chip, 1 - c), sibling) for t in range(n) for j, chip in enumerate(chips)]
        return own, first, relays, from_sibling

    def start(srcs, dsts, sems):
        own, first, _, _ = plan(srcs, dsts, sems, False)
        for cp in own + first:
            cp.start()

    def finish(srcs, dsts, sems):
        own, first, relays, from_sibling = plan(srcs, dsts, sems, True)
        for arrival, forward in relays:
            arrival.wait_recv()
            forward.start()
        for cp in from_sibling:
            cp.wait_recv()
        for cp in first + [forward for _, forward in relays]:
            cp.wait_send()
        for cp in own:
            cp.wait()

    return _Rider(blocks, [jax.ShapeDtypeStruct((N_DEV,) + b.shape, b.dtype) for b in blocks],
                  [pltpu.SemaphoreType.DMA((7 * n,)), pltpu.SemaphoreType.DMA((7 * n,)), pltpu.SemaphoreType.DMA((n,))],
                  start, finish)


def _pair_exchange(name, blocks):
    n = len(blocks)

    def body(*refs):
        srcs, dsts = refs[:n], refs[n:2 * n]
        send_sems, recv_sems = refs[2 * n:]
        sibling = (lax.axis_index("x"), lax.axis_index("y"), 1 - lax.axis_index("c"))
        copies = [_remote(srcs[t], dsts[t], send_sems, recv_sems, t, sibling) for t in range(n)]
        for cp in copies:
            cp.start()
        for cp in copies:
            cp.wait()

    return pl.pallas_call(
        body, name=name, in_specs=[ANY_SPEC] * n, out_specs=[ANY_SPEC] * n,
        out_shape=[jax.ShapeDtypeStruct(b.shape, b.dtype) for b in blocks],
        scratch_shapes=[pltpu.SemaphoreType.DMA((n,)), pltpu.SemaphoreType.DMA((n,))],
        compiler_params=pltpu.CompilerParams(has_side_effects=True))(*blocks)


def _chip_rider(items):
    n = len(items)
    out_shape = [jax.ShapeDtypeStruct((4,) + tuple(a.shape[1:] if per_chip else a.shape), a.dtype) for a, per_chip in items]

    def plan(srcs, dsts, sems, waits):
        send_sems, recv_sems, local_sems = sems
        x, y, c = lax.axis_index("x"), lax.axis_index("y"), lax.axis_index("c")
        mine = 2 * x + y
        own = [pltpu.make_async_copy(srcs[t].at[mine] if per_chip else srcs[t], dsts[t].at[mine], local_sems.at[t])
               for t, (_, per_chip) in enumerate(items)]
        sends, arrivals = [], []
        for k in range(1, 4):
            px, py = x ^ (k >> 1), y ^ (k & 1)
            peer = 2 * px + py
            for t, (_, per_chip) in enumerate(items):
                src = srcs[t].at[peer] if per_chip else srcs[t]
                sends.append(_remote(src, dsts[t].at[mine], send_sems, recv_sems, 3 * t + k - 1, (px, py, c)))
                if waits:
                    arrivals.append(_remote(src, dsts[t].at[peer], send_sems, recv_sems, 3 * t + k - 1, (px, py, c)))
        return own, sends, arrivals

    def start(srcs, dsts, sems):
        own, sends, _ = plan(srcs, dsts, sems, False)
        for cp in own + sends:
            cp.start()

    def finish(srcs, dsts, sems):
        own, sends, arrivals = plan(srcs, dsts, sems, True)
        for out, arrival in zip(sends, arrivals):
            out.wait_send()
            arrival.wait_recv()
        for cp in own:
            cp.wait()

    return _Rider([a for a, _ in items], out_shape,
                  [pltpu.SemaphoreType.DMA((3 * n,)), pltpu.SemaphoreType.DMA((3 * n,)), pltpu.SemaphoreType.DMA((n,))],
                  start, finish)


def _perm(a):
    t_len, w = a.shape
    return a.reshape(SCAN_SEG, t_len // SCAN_SEG, w).transpose(1, 0, 2).reshape(t_len, w)


def _unperm(a):
    t_len, w = a.shape
    return a.reshape(t_len // SCAN_SEG, SCAN_SEG, w).transpose(1, 0, 2).reshape(t_len, w)


def _band_diag(p):
    a, b = p.shape[1:]
    eye = jnp.eye(8, dtype=p.dtype)
    return (p.reshape(4, 8, a, 1, b) * eye[None, :, None, :, None]).reshape(4, 8 * a, 8 * b)


def _band_diag_extract(m, a, b):
    eye = jnp.eye(8, dtype=m.dtype)
    return (m.reshape(4, 8, a, 8, b) * eye[None, :, None, :, None]).sum(axis=3).reshape(32, a, b)


def _proj_cols(w_ref_layout):
    parts = []
    for name in PROJ_ORDER:
        off, width = REF_SPLIT[name]
        part = w_ref_layout[:, off:off + width]
        if width < PROJ[name][1]:
            part = jnp.pad(part, ((0, 0), (0, PROJ[name][1] - width)))
        parts.append(part)
    return jnp.concatenate(parts, axis=1)


def _unproj_cols(w_proj_layout):
    parts = []
    for name in REF_ORDER:
        parts.append(w_proj_layout[:, PROJ[name][0]:PROJ[name][0] + REF_SPLIT[name][1]])
    return jnp.concatenate(parts, axis=1)


def _uq_cols(w):
    w3 = w.reshape(w.shape[0], MLA_HEADS, 96)
    return jnp.concatenate([w3[:, :, :64].reshape(-1, 512), w3[:, :, 64:].reshape(-1, 256)], axis=1)


def _un_uq_cols(w):
    return jnp.concatenate([w[:, :512].reshape(-1, MLA_HEADS, 64), w[:, 512:].reshape(-1, MLA_HEADS, 32)], axis=2).reshape(-1, 768)


def _ukv_cols(w):
    w3 = w.reshape(w.shape[0], MLA_HEADS, 128)
    return jnp.concatenate([w3[:, :, :64].reshape(-1, 512), w3[:, :, 64:].reshape(-1, 512)], axis=1)


def _un_ukv_cols(w):
    return jnp.concatenate([w[:, :512].reshape(-1, MLA_HEADS, 64), w[:, 512:].reshape(-1, MLA_HEADS, 64)], axis=2).reshape(-1, 1024)


def _cols_gathered(g):
    return g.transpose(1, 0, 2).reshape(g.shape[1], -1)


def _cols_slabs(w):
    r, c = w.shape
    return w.reshape(r, N_DEV, c // N_DEV).transpose(1, 0, 2)


def _row(v):
    return v.reshape(1, -1)


def _s5_weights(p):
    flat = dict(lam_re=p["s5_lam_re"].reshape(S5_CH, 1), lam_im=p["s5_lam_im"].reshape(S5_CH, 1),
                log_dt=jnp.repeat(p["s5_log_dt"], S5_STATE).reshape(S5_CH, 1),
                b_re=p["s5_b_re"].reshape(S5_CH, S5_GROUP), b_im=p["s5_b_im"].reshape(S5_CH, S5_GROUP))
    ab_re, ab_im, bb_re, bb_im = _s5_param_fwd(flat["lam_re"], flat["lam_im"], flat["log_dt"], flat["b_re"], flat["b_im"])
    w_bu = jnp.concatenate([_band_diag(bb.reshape(32, 64, 16).transpose(0, 2, 1)) for bb in (bb_re, bb_im)], axis=0)
    w_y = jnp.concatenate([_band_diag(p["s5_c_re"].transpose(0, 2, 1)), -_band_diag(p["s5_c_im"].transpose(0, 2, 1))], axis=0)
    return dict(flat=flat,
                a_re=jnp.broadcast_to(ab_re.reshape(1, S5_CH), (SCAN_SEG, S5_CH)),
                a_im=jnp.broadcast_to(ab_im.reshape(1, S5_CH), (SCAN_SEG, S5_CH)),
                w_bu=w_bu.astype(BF16), w_bu_t=w_bu.swapaxes(1, 2).astype(BF16),
                w_y=w_y.astype(BF16), w_y_t=w_y.swapaxes(1, 2).astype(BF16))


def _layer_fwd(x, w, p, tabs, late_shards=None, rider=None):
    t_len = x.shape[0]
    cos64, sin64, cos32, sin32 = tabs
    s = {}
    (s["h1"],) = _rowwise("pre_mix", lambda xt, g: (_rms(xt, g),), [x], [_row(p["g_pre_mix"])], [((t_len, D_MODEL), BF16)])
    if late_shards is None:
        proj = _mm("proj", s["h1"], w["w_in"], "nn")
    else:
        proj, late = _mm("proj_gather", s["h1"], w["w_in"], "nn", rider=_gather_rider(late_shards))
        w.update(_late_weights(late))
    s["proj"] = proj
    s["qa"], s["ka"], s["va"] = _swa_prep(proj, cos64, sin64)
    s["sinks"] = jnp.broadcast_to(p["swa_sinks"][:, None], (8, 128))
    y_a = _swa_fwd(s["qa"], s["ka"], s["va"], s["sinks"])
    s5 = s["s5"] = _s5_weights(p)
    su = proj[:, PROJ["su"][0]:PROJ["su"][0] + 512]
    s["u_p"] = _perm(su).astype(BF16)
    bu = _s5_expand("s5_bu", s["u_p"], s5["w_bu"])
    s["xs"], s["cin"] = _s5_scan("s5_scan_fwd", bu, s5["a_re"], s5["a_im"])
    s["y_s"] = _unperm(_s5_contract("s5_y", s["xs"], s5["w_y"]))
    (s["z"],) = _rowwise("s5_gelu", lambda y, u, d: (jax.nn.gelu(y + d * u),), [s["y_s"], _seg(proj, "su")],
                         [_row(p["s5_d"])], [((t_len, 512), BF16)])
    s["zz"] = _mm("s5_glu_mm", s["z"], w["w_glu"], "nn")
    (y_b,) = _rowwise("s5_glu", lambda t: (t[:, :512] * jax.nn.sigmoid(t[:, 512:]),), [s["zz"]], [], [((t_len, 512), BF16)])
    s["qc"], s["kc"] = _ret_prep(proj, cos64, sin64)
    y_c, s["prevs"] = _ret_fwd(s["qc"], s["kc"], proj)
    s["cqn"], s["ckvn"] = _rowwise("mla_norm", lambda cq, ckv, gq, gkv: (_rms(cq, gq), _rms(ckv, gkv)),
                                   [_seg(proj, "cq"), _seg(proj, "ckv")], [_row(p["mla_g_q"]), _row(p["mla_g_kv"])],
                                   [((t_len, 256), BF16), ((t_len, 128), BF16)])
    qf = _mm("mla_uq", s["cqn"], w["w_uq"], "nn")
    kvf = _mm("mla_ukv", s["ckvn"], w["w_ukv"], "nn")
    s["Q"], s["K"], s["V"] = _mla_prep(qf, kvf, proj, cos32, sin32)
    y_d, s["lse"], ridden = _mla_fwd(s["Q"], s["K"], s["V"], rider)
    s["ys"] = [y_a, y_b, y_c, y_d]
    s["merged"] = _merge_fwd(s["ys"], w["wb"], proj)
    s["o"] = _mm("out_mm", s["merged"], w["w_out"], "nn")

    def post_mix(xt, ot, g1, g2):
        x1 = xt + _rms(ot, g1)
        return x1, _rms(x1, g2)
    s["x1"], s["h2"] = _rowwise("post_mix", post_mix, [x, s["o"]], [_row(p["g_post_mix"]), _row(p["g_pre_mlp"])],
                                [((t_len, D_MODEL), F32), ((t_len, D_MODEL), BF16)])
    s["a"], s["r"] = _mm("ff1", s["h2"], w["w_ff1"], "nn", out_dtypes=(F32, BF16),
                         epi=lambda acc: (acc, jnp.square(jnp.maximum(acc, 0.0))))
    s["f"] = _mm("ff2", s["r"], w["w_ff2"], "nn")
    (x2,) = _rowwise("post_mlp", lambda x1, f, g: (x1 + _rms(f, g),), [s["x1"], s["f"]], [_row(p["g_post_mlp"])],
                     [((t_len, D_MODEL), F32)])
    return x2, s, ridden


def _layer_bwd(x, dx2, w, p, s, tabs, pending=None, exchange_name=None):
    t_len = x.shape[0]
    cos64, sin64, cos32, sin32 = tabs
    proj = s["proj"]
    big, small = {}, {}

    def post_mlp_bwd(f, d, g):
        df, dg = _rms_bwd(f, g, d)
        return (df,), (dg,)
    df, small["g_post_mlp"] = _rowwise("post_mlp_bwd", post_mlp_bwd, [s["f"], dx2], [_row(p["g_post_mlp"])],
                                       [((t_len, D_MODEL), BF16)], [((1, D_MODEL), F32)])
    da = _mm("ff2_dx", df, w["w_ff2"], "nt", out_dtypes=(BF16,), epi=lambda acc, a: (acc * 2.0 * jnp.maximum(a, 0.0),),
             epi_in=(s["a"],))
    big["w_ff2"] = _mm("ff2_dw", s["r"], df, "tn")
    dh2 = _mm("ff1_dx", da, w["w_ff1"], "nt")
    big["w_ff1"] = _mm("ff1_dw", s["h2"], da, "tn")

    def post_mix_bwd(x1, o, dh, d2, g1, g2):
        dx1_n, dg2 = _rms_bwd(x1, g2, dh)
        dx1 = d2 + dx1_n
        do, dg1 = _rms_bwd(o, g1, dx1)
        return (dx1, do), (dg1, dg2)
    dx1, do, small["g_post_mix"], small["g_pre_mlp"] = _rowwise(
        "post_mix_bwd", post_mix_bwd, [s["x1"], s["o"], dh2, dx2], [_row(p["g_post_mix"]), _row(p["g_pre_mlp"])],
        [((t_len, D_MODEL), F32), ((t_len, D_MODEL), BF16)], [((1, D_MODEL), F32), ((1, D_MODEL), F32)])
    dmerged = _mm("out_dx", do, w["w_out"], "nt")
    big["w_out"] = _mm("out_dw", s["merged"], do, "tn")
    rider, n_pending = None, len(pending or ())
    if exchange_name is not None:
        late_sums, _ = _pair_stage(exchange_name, _grad_slabs(big, LATE))
        rider = _chip_rider([(part, True) for part in list(pending or ()) + late_sums])

    res = _merge_bwd(s["ys"], w["wb"], proj, dmerged)
    dgl, dts = res[:4], res[4:]
    dys = [_mm("branch_dx", dts[b], w["wb"][b], "nt") for b in range(4)]
    big["wb"] = [_mm("branch_dw", s["ys"][b], dts[b], "tn") for b in range(4)]
    seg = {}

    dqa, dka, dva, dka_prev, dva_prev, dsink = _swa_bwd(s["qa"], s["ka"], s["va"], s["sinks"], dys[0])
    seg["sq"], seg["sk"], seg["sv"] = _swa_unprep(dqa, dka, dva, dka_prev, dva_prev, cos64, sin64)
    small["swa_sinks"] = dsink[:, 0]

    def glu_bwd(t, d):
        za, sg = t[:, :512], jax.nn.sigmoid(t[:, 512:])
        return (jnp.concatenate([d * sg, d * za * sg * (1.0 - sg)], axis=1),)
    (dzz,) = _rowwise("s5_glu_bwd", glu_bwd, [s["zz"], dys[1]], [], [((t_len, 1024), BF16)])
    dz = _mm("s5_glu_dx", dzz, w["w_glu"], "nt")
    big["w_glu"] = _mm("s5_glu_dw", s["z"], dzz, "tn")

    def gelu_bwd(y, u, d, skip):
        _, vjp = jax.vjp(jax.nn.gelu, y + skip * u)
        (dy2,) = vjp(d)
        return (dy2, dy2 * skip), (jnp.sum(dy2 * u, axis=0, keepdims=True),)
    dy_s, du_skip, d_skip = _rowwise("s5_gelu_bwd", gelu_bwd, [s["y_s"], _seg(proj, "su"), dz], [_row(p["s5_d"])],
                                     [((t_len, 512), F32), ((t_len, 512), F32)], [((1, 512), F32)])
    small["s5_d"] = d_skip[0]
    s5 = s["s5"]
    dy_p = _perm(dy_s).astype(BF16)
    dlam_in = _s5_expand("s5_dx_in", dy_p, s5["w_y_t"])
    lam, da_re, da_im = _s5_scan("s5_scan_bwd", dlam_in, s5["a_re"], s5["a_im"], states=s["xs"], carry_in=s["cin"])
    du_s5 = _unperm(_s5_contract("s5_du", lam, s5["w_bu_t"]))
    d_w_bu = _s5_band_grad("s5_dbu", s["u_p"], lam, True)
    d_w_y = _s5_band_grad("s5_dc", dy_p, s["xs"], False)
    (seg["su"],) = _rowwise("s5_du_sum", lambda a, b: (a + b,), [du_s5, du_skip], [], [((t_len, 512), BF16)])
    d_bb_re = _band_diag_extract(d_w_bu[:4], 16, 64).transpose(0, 2, 1).reshape(S5_CH, S5_GROUP)
    d_bb_im = _band_diag_extract(d_w_bu[4:], 16, 64).transpose(0, 2, 1).reshape(S5_CH, S5_GROUP)
    small["s5_c_re"] = _band_diag_extract(d_w_y[:4], 64, 16).transpose(0, 2, 1)
    small["s5_c_im"] = -_band_diag_extract(d_w_y[4:], 64, 16).transpose(0, 2, 1)
    fl = s5["flat"]
    group_sum = jnp.repeat(jnp.eye(S5_GROUPS, dtype=F32), S5_STATE, axis=1)
    d_lr, d_li, d_ldt, d_br, d_bi = _s5_param_bwd(fl["lam_re"], fl["lam_im"], fl["log_dt"], fl["b_re"], fl["b_im"], group_sum,
                                                  da_re.reshape(S5_CH, 1), da_im.reshape(S5_CH, 1), d_bb_re, d_bb_im)
    small["s5_lam_re"], small["s5_lam_im"] = d_lr.reshape(32, 64), d_li.reshape(32, 64)
    small["s5_log_dt"] = d_ldt[:, 0]
    small["s5_b_re"], small["s5_b_im"] = d_br.reshape(32, 64, 16), d_bi.reshape(32, 64, 16)

    dqc, dkc, seg["rv"], seg["rg"] = _ret_bwd(s["qc"], s["kc"], proj, s["prevs"], dys[2])
    seg["rq"], seg["rk"] = _ret_unprep(dqc, dkc, cos64, sin64)

    do_h, delta = _mla_bwd_prep(s["ys"][3], dys[3])
    d_q, d_k, d_v, ridden = _mla_bwd(s["Q"], s["K"], s["V"], do_h, s["lse"], delta, rider)
    dqf, dkvf, seg["kr"] = _mla_unprep(d_q, d_k, d_v, cos32, sin32)
    dcqn = _mm("mla_uq_dx", dqf, w["w_uq"], "nt")
    big["w_uq"] = _mm("mla_uq_dw", s["cqn"], dqf, "tn")
    dckvn = _mm("mla_ukv_dx", dkvf, w["w_ukv"], "nt")
    big["w_ukv"] = _mm("mla_ukv_dw", s["ckvn"], dkvf, "tn")

    def mla_norm_bwd(cq, ckv, d1, d2, gq, gkv):
        dcq, dgq = _rms_bwd(cq, gq, d1)
        dckv, dgkv = _rms_bwd(ckv, gkv, d2)
        return (dcq, dckv), (dgq, dgkv)
    seg["cq"], seg["ckv"], dgq, dgkv = _rowwise(
        "mla_norm_bwd", mla_norm_bwd, [_seg(proj, "cq"), _seg(proj, "ckv"), dcqn, dckvn],
        [_row(p["mla_g_q"]), _row(p["mla_g_kv"])], [((t_len, 256), BF16), ((t_len, 128), BF16)],
        [((1, 256), F32), ((1, 128), F32)])
    small["mla_g_q"], small["mla_g_kv"] = dgq[0], dgkv[0]

    dproj = jnp.concatenate([seg[name] for name in PROJ_ORDER[:-1]] + list(dgl), axis=1)
    dh1 = _mm("proj_dx", dproj, w["w_in"], "nt")
    big["w_in"] = _mm("proj_dw", s["h1"], dproj, "tn")

    def pre_mix_bwd(xt, dh, d1, g):
        dxn, dg = _rms_bwd(xt, g, dh)
        return (d1 + dxn,), (dg,)
    dx, small["g_pre_mix"] = _rowwise("pre_mix_bwd", pre_mix_bwd, [x, dh1, dx1], [_row(p["g_pre_mix"])],
                                      [((t_len, D_MODEL), F32)], [((1, D_MODEL), F32)])
    for name in ("g_pre_mix", "g_post_mix", "g_pre_mlp", "g_post_mlp"):
        small[name] = small[name][0]
    return dx, big, small, ridden[:n_pending], ridden[n_pending:]


BIG = ("w_in", "s5_w_glu", "mla_w_uq", "mla_w_ukv", "w_branch", "w_out", "w_ff1", "w_ff2")
SMALL = ("g_pre_mix", "g_post_mix", "g_pre_mlp", "g_post_mlp", "swa_sinks", "s5_lam_re", "s5_lam_im", "s5_log_dt",
         "s5_b_re", "s5_b_im", "s5_c_re", "s5_c_im", "s5_d", "mla_g_q", "mla_g_kv")
WEIGHTS = ("g_pre_mix", "g_post_mix", "g_pre_mlp", "g_post_mlp", "w_in", "swa_sinks", "s5_lam_re", "s5_lam_im", "s5_log_dt",
           "s5_b_re", "s5_b_im", "s5_c_re", "s5_c_im", "s5_d", "s5_w_glu", "mla_g_q", "mla_g_kv", "mla_w_uq", "mla_w_ukv",
           "w_branch", "w_out", "w_ff1", "w_ff2")


EARLY = ("w_in", "s5_w_glu", "mla_w_uq", "mla_w_ukv", "w_branch")
LATE = ("w_out", "w_ff1", "w_ff2")


def _early_weights(shards):
    g = dict(zip(EARLY, shards))
    wb = g["w_branch"].transpose(1, 2, 0, 3).reshape(4, 512, D_MODEL)
    return dict(w_in=_proj_cols(_cols_gathered(g["w_in"])), w_glu=_cols_gathered(g["s5_w_glu"]),
                w_uq=_uq_cols(_cols_gathered(g["mla_w_uq"])), w_ukv=_ukv_cols(_cols_gathered(g["mla_w_ukv"])),
                wb=[wb[b] for b in range(4)])


def _late_weights(shards):
    g = dict(zip(LATE, shards))
    return dict(w_out=g["w_out"].reshape(D_MODEL, D_MODEL), w_ff1=_cols_gathered(g["w_ff1"]),
                w_ff2=g["w_ff2"].reshape(4 * D_MODEL, D_MODEL))


def _grad_slabs(big, names):
    make = dict(w_in=lambda: _cols_slabs(_unproj_cols(big["w_in"])), s5_w_glu=lambda: _cols_slabs(big["w_glu"]),
                mla_w_uq=lambda: _cols_slabs(_un_uq_cols(big["w_uq"])), mla_w_ukv=lambda: _cols_slabs(_un_ukv_cols(big["w_ukv"])),
                w_branch=lambda: jnp.stack(big["wb"]).reshape(4, 512, N_DEV, D_MODEL // N_DEV).transpose(2, 0, 1, 3),
                w_out=lambda: big["w_out"].reshape(N_DEV, D_MODEL // N_DEV, D_MODEL),
                w_ff1=lambda: _cols_slabs(big["w_ff1"]), w_ff2=lambda: big["w_ff2"].reshape(N_DEV, 4 * D_MODEL // N_DEV, D_MODEL))
    return [make[name]() for name in names]


def _pair_stage(name, slabs, extra=()):
    core = lax.axis_index("c")
    own, send = [], []
    for slab in slabs:
        by_chip = slab.reshape((4, 2) + slab.shape[1:])
        own.append(lax.dynamic_index_in_dim(by_chip, core, axis=1, keepdims=False))
        send.append(lax.dynamic_index_in_dim(by_chip, 1 - core, axis=1, keepdims=False).astype(BF16))
    got = _pair_exchange(name, send + list(extra))
    sums = []
    for mine, theirs in zip(own, got):
        flat = (-1, mine.shape[-1])
        (part,) = _rowwise("grads_pair_sum", lambda a, b: (a + b.astype(F32),), [mine.reshape(flat), theirs.reshape(flat)],
                           [], [((int(np.prod(mine.shape[:-1])), mine.shape[-1]), BF16)])
        sums.append(part.reshape(mine.shape))
    return sums, got[len(slabs):]


def kernel(x, g_pre_mix, g_post_mix, g_pre_mlp, g_post_mlp, w_in, swa_sinks, s5_lam_re, s5_lam_im, s5_log_dt, s5_b_re, s5_b_im, s5_c_re, s5_c_im, s5_d, s5_w_glu, mla_g_q, mla_g_kv, mla_w_uq, mla_w_ukv, w_branch, w_out, w_ff1, w_ff2, loss_target, m_g_pre_mix, m_g_post_mix, m_g_pre_mlp, m_g_post_mlp, m_w_in, m_swa_sinks, m_s5_lam_re, m_s5_lam_im, m_s5_log_dt, m_s5_b_re, m_s5_b_im, m_s5_c_re, m_s5_c_im, m_s5_d, m_s5_w_glu, m_mla_g_q, m_mla_g_kv, m_mla_w_uq, m_mla_w_ukv, m_w_branch, m_w_out, m_w_ff1, m_w_ff2, v_g_pre_mix, v_g_post_mix, v_g_pre_mlp, v_g_post_mlp, v_w_in, v_swa_sinks, v_s5_lam_re, v_s5_lam_im, v_s5_log_dt, v_s5_b_re, v_s5_b_im, v_s5_c_re, v_s5_c_im, v_s5_d, v_s5_w_glu, v_mla_g_q, v_mla_g_kv, v_mla_w_uq, v_mla_w_ukv, v_w_branch, v_w_out, v_w_ff1, v_w_ff2):
    weights = dict(g_pre_mix=g_pre_mix, g_post_mix=g_post_mix, g_pre_mlp=g_pre_mlp, g_post_mlp=g_post_mlp, w_in=w_in,
                   swa_sinks=swa_sinks, s5_lam_re=s5_lam_re, s5_lam_im=s5_lam_im, s5_log_dt=s5_log_dt, s5_b_re=s5_b_re,
                   s5_b_im=s5_b_im, s5_c_re=s5_c_re, s5_c_im=s5_c_im, s5_d=s5_d, s5_w_glu=s5_w_glu, mla_g_q=mla_g_q,
                   mla_g_kv=mla_g_kv, mla_w_uq=mla_w_uq, mla_w_ukv=mla_w_ukv, w_branch=w_branch, w_out=w_out, w_ff1=w_ff1,
                   w_ff2=w_ff2)
    m_in = dict(g_pre_mix=m_g_pre_mix, g_post_mix=m_g_post_mix, g_pre_mlp=m_g_pre_mlp, g_post_mlp=m_g_post_mlp, w_in=m_w_in,
                swa_sinks=m_swa_sinks, s5_lam_re=m_s5_lam_re, s5_lam_im=m_s5_lam_im, s5_log_dt=m_s5_log_dt,
                s5_b_re=m_s5_b_re, s5_b_im=m_s5_b_im, s5_c_re=m_s5_c_re, s5_c_im=m_s5_c_im, s5_d=m_s5_d,
                s5_w_glu=m_s5_w_glu, mla_g_q=m_mla_g_q, mla_g_kv=m_mla_g_kv, mla_w_uq=m_mla_w_uq, mla_w_ukv=m_mla_w_ukv,
                w_branch=m_w_branch, w_out=m_w_out, w_ff1=m_w_ff1, w_ff2=m_w_ff2)
    v_in = dict(g_pre_mix=v_g_pre_mix, g_post_mix=v_g_post_mix, g_pre_mlp=v_g_pre_mlp, g_post_mlp=v_g_post_mlp, w_in=v_w_in,
                swa_sinks=v_swa_sinks, s5_lam_re=v_s5_lam_re, s5_lam_im=v_s5_lam_im, s5_log_dt=v_s5_log_dt,
                s5_b_re=v_s5_b_re, s5_b_im=v_s5_b_im, s5_c_re=v_s5_c_re, s5_c_im=v_s5_c_im, s5_d=v_s5_d,
                s5_w_glu=v_s5_w_glu, mla_g_q=v_mla_g_q, mla_g_kv=v_mla_g_kv, mla_w_uq=v_mla_w_uq, mla_w_ukv=v_mla_w_ukv,
                w_branch=v_w_branch, w_out=v_w_out, w_ff1=v_w_ff1, w_ff2=v_w_ff2)
    depth = g_pre_mix.shape[0]
    t_len = x.shape[1]
    tabs = _rope_tables(t_len, 32) + _rope_tables(t_len, 16)

    def shards(l, names):
        return [weights[name][l].astype(BF16) for name in names]
    early = _run_rider("gather_weights", _gather_rider(shards(0, EARLY)))

    xs, saved, layer_w, layer_p = [x[0]], [], [], []
    for l in range(depth):
        layer_w.append(_early_weights(early))
        layer_p.append({name: weights[name][l] for name in SMALL})
        x_next, s, early = _layer_fwd(xs[-1], layer_w[l], layer_p[l], tabs, shards(l, LATE),
                                      _gather_rider(shards(l + 1, EARLY)) if l + 1 < depth else None)
        xs.append(x_next)
        saved.append(s)

    def loss_fn(y, tgt):
        err = y - tgt
        part = 0.5 * jnp.sum(jnp.mean(err * err, axis=-1, keepdims=True), axis=0, keepdims=True)
        return (err * (1.0 / D_MODEL),), (jnp.broadcast_to(part, (8, 128)),)
    dx, loss_part = _rowwise("loss", loss_fn, [xs[-1], loss_target[0]], [], [((t_len, D_MODEL), F32)], [((8, 128), F32)])
    loss = lax.psum(loss_part[0, 0], ("x", "y", "c"))

    small_grads, arrived, pending = [None] * depth, [{} for _ in range(depth)], None
    for l in reversed(range(depth)):
        dx, big, small_grads[l], got_pending, got_late = _layer_bwd(xs[l], dx, layer_w[l], layer_p[l], saved[l], tabs, pending,
                                                                    "late_grads_to_sibling")
        if pending is not None:
            arrived[l + 1].update(zip(EARLY, got_pending))
        arrived[l].update(zip(LATE, got_late))
        if l > 0:
            pending, _ = _pair_stage("early_grads_to_sibling", _grad_slabs(big, EARLY))
    small_vec = jnp.concatenate([jnp.stack([small_grads[l][name] for l in range(depth)]).reshape(-1) for name in SMALL])
    n_small = small_vec.shape[0]
    rows = -(-n_small // 1024) * 8
    small_mat = jnp.pad(small_vec, (0, rows * 128 - n_small)).reshape(rows, 128)
    sums, (sibling_small,) = _pair_stage("last_grads_to_sibling", _grad_slabs(big, EARLY), [small_mat])
    small_pair = jnp.where(lax.axis_index("c") == 0, jnp.stack([small_mat, sibling_small]), jnp.stack([sibling_small, small_mat]))
    recv = _run_rider("grads_to_chips", _chip_rider([(part, True) for part in sums] + [(small_pair, False)]))
    arrived[0].update(zip(EARLY, recv[:-1]))
    small_sum = _sum_slabs("sum_small_grads", recv[-1].reshape((N_DEV,) + small_mat.shape)).reshape(-1)

    grads, deltas, new_m, new_v = {}, {}, {}, {}
    for name in BIG:
        slabs = jnp.stack([arrived[l][name] for l in range(depth)], axis=1)
        grads[name], deltas[name], new_m[name], new_v[name] = _adamw("adamw_" + name, weights[name], slabs, m_in[name],
                                                                     v_in[name], 4)
    off = 0
    for name in SMALL:
        size = int(np.prod(weights[name].shape))
        g = small_sum[off:off + size].reshape(weights[name].shape)
        off += size
        grads[name], deltas[name], new_m[name], new_v[name] = _adamw("adamw_" + name, weights[name], g, m_in[name],
                                                                     v_in[name], False)
    return (loss, dx[None], *[grads[n] for n in WEIGHTS], *[deltas[n] for n in WEIGHTS], *[new_m[n] for n in WEIGHTS],
            *[new_v[n] for n in WEIGHTS])
```

```python
import functools
import math

import numpy as np
import jax
import jax.numpy as jnp
from jax import lax
from jax.experimental import pallas as pl
from jax.experimental.pallas import tpu as pltpu

F32, BF16 = jnp.float32, jnp.bfloat16
EPS = 1e-6
NEG_INF = -1e30
ROPE_THETA = 10000.0
D_MODEL = 1024
N_DEV = 8
BLOCK = 128
SWA_HEADS, SWA_KV = 8, 2
RET_HEADS, RET_DK, RET_DV = 4, 64, 128
MLA_HEADS = 8
S5_GROUPS, S5_STATE, S5_GROUP = 32, 64, 16
S5_CH = S5_GROUPS * S5_STATE
SCAN_LANES = 256
SCAN_SEG = 8
ADAM_LR, ADAM_B1, ADAM_B2, ADAM_EPS, ADAM_WD, ADAM_STEP = 0.001, 0.9, 0.999, 1e-08, 0.01, 10
VMEM_MB = 56

PROJ = dict(sq=(0, 512), su=(512, 512), rv=(1024, 512), rg=(1536, 512), rq=(2048, 256), rk=(2304, 256),
            cq=(2560, 256), sk=(2816, 128), sv=(2944, 128), ckv=(3072, 128), kr=(3200, 128), gates=(3328, 4096))
PROJ_W = 7424
REF_SPLIT = dict(sq=(0, 512), sk=(512, 128), sv=(640, 128), su=(768, 512), rq=(1280, 256), rk=(1536, 256),
                 rv=(1792, 512), rg=(2304, 512), cq=(2816, 256), ckv=(3072, 128), kr=(3200, 32), gates=(3232, 4096))
PROJ_ORDER = ("sq", "su", "rv", "rg", "rq", "rk", "cq", "sk", "sv", "ckv", "kr", "gates")
REF_ORDER = ("sq", "sk", "sv", "su", "rq", "rk", "rv", "rg", "cq", "ckv", "kr", "gates")

NN = (((1,), (0,)), ((), ()))
NT = (((1,), (1,)), ((), ()))
TN = (((0,), (0,)), ((), ()))


def _params(sem):
    return pltpu.CompilerParams(dimension_semantics=sem, vmem_limit_bytes=VMEM_MB * 2 ** 20)


def _raw_dot(a, b, dims):
    return lax.dot_general(a.astype(BF16), b.astype(BF16), dims, preferred_element_type=F32)


@jax.custom_vjp
def _dnn(a, b):
    return _raw_dot(a, b, NN)


@jax.custom_vjp
def _dnt(a, b):
    return _raw_dot(a, b, NT)


@jax.custom_vjp
def _dtn(a, b):
    return _raw_dot(a, b, TN)


_dnn.defvjp(lambda a, b: (_dnn(a, b), (a, b)),
            lambda r, g: (_dnt(g, r[1]).astype(r[0].dtype), _dtn(r[0], g).astype(r[1].dtype)))
_dnt.defvjp(lambda a, b: (_dnt(a, b), (a, b)),
            lambda r, g: (_dnn(g, r[1]).astype(r[0].dtype), _dtn(g, r[0]).astype(r[1].dtype)))
_dtn.defvjp(lambda a, b: (_dtn(a, b), (a, b)),
            lambda r, g: (_dnt(r[1], g).astype(r[0].dtype), _dnn(r[0], g).astype(r[1].dtype)))


class _Cols:
    def __init__(self, arr, width, index):
        self.arr, self.width, self.index = arr, width, index


def _seg(proj, name):
    off, width = PROJ[name]
    return _Cols(proj, width, off // width)


def _row_spec(shape, tile):
    lead = len(shape) - 2
    return pl.BlockSpec(tuple(shape[:-2]) + (tile, shape[-1]), lambda i, lead=lead: (0,) * lead + (i, 0))


def _const_spec(shape):
    nd = len(shape)
    return pl.BlockSpec(tuple(shape), lambda *_, nd=nd: (0,) * nd)


def _rowwise(name, fn, rows, consts, out_rows, out_accs=(), tile=256):
    arrs = [r.arr if isinstance(r, _Cols) else r for r in rows]
    t_len = arrs[0].shape[-2]
    tile = min(tile, t_len)
    n = t_len // tile
    in_specs = []
    for r in rows:
        if isinstance(r, _Cols):
            in_specs.append(pl.BlockSpec((tile, r.width), lambda i, c=r.index: (i, c)))
        else:
            in_specs.append(_row_spec(r.shape, tile))
    in_specs += [_const_spec(c.shape) for c in consts]
    out_specs = [_row_spec(s, tile) for s, _ in out_rows] + [_const_spec(s) for s, _ in out_accs]
    out_shape = [jax.ShapeDtypeStruct(s, d) for s, d in tuple(out_rows) + tuple(out_accs)]
    n_in, n_or = len(rows) + len(consts), len(out_rows)

    def body(*refs):
        res = fn(*[r[...] for r in refs[:n_in]])
        row_out, acc_out = res if out_accs else (res, ())
        for ref, val in zip(refs[n_in:n_in + n_or], row_out):
            if isinstance(val, (list, tuple)):
                for h, v_h in enumerate(val):
                    ref[h] = v_h.astype(ref.dtype)
            else:
                ref[...] = val.astype(ref.dtype)
        first = pl.program_id(0) == 0
        for ref, val in zip(refs[n_in + n_or:], acc_out):
            @pl.when(first)
            def _(ref=ref, val=val):
                ref[...] = val

            @pl.when(jnp.logical_not(first))
            def _(ref=ref, val=val):
                ref[...] += val

    return pl.pallas_call(body, name=name, grid=(n,), in_specs=in_specs, out_specs=out_specs, out_shape=out_shape,
                          compiler_params=_params(("arbitrary",)))(*arrs, *consts)


MM_VMEM_BUDGET = 44 * 2 ** 20


def _pick_tn(n, io_col_bytes, tmp_col_bytes, fixed_bytes):
    best = None
    for cand in range(128, n + 1, 128):
        if n % cand == 0 and 2 * (fixed_bytes + cand * io_col_bytes) + cand * tmp_col_bytes <= MM_VMEM_BUDGET:
            best = cand
    return best if best is not None else min(n, 128)


def _mm(name, a, b, mode, out_dtypes=(F32,), tm=512, tn=None, epi=None, epi_in=(), rider=None):
    if mode == "nn":
        (m, k), n = a.shape, b.shape[1]
    elif mode == "nt":
        (m, k), n = a.shape, b.shape[0]
    else:
        (k, m), n = a.shape, b.shape[1]
    tm = min(tm, m)
    if tn is None:
        io_col = k * b.dtype.itemsize + tm * (sum(jnp.dtype(d).itemsize for d in out_dtypes) + sum(e.dtype.itemsize for e in epi_in))
        tmp_col = tm * 4 * (2 if epi is not None else 1)
        tn = _pick_tn(n, io_col, tmp_col, tm * k * a.dtype.itemsize)
    tn = min(tn, n)
    assert m % tm == 0 and n % tn == 0, (name, m, n, tm, tn)
    a_spec = pl.BlockSpec((k, tm), lambda i, j: (0, i)) if mode == "tn" else pl.BlockSpec((tm, k), lambda i, j: (i, 0))
    b_spec = pl.BlockSpec((tn, k), lambda i, j: (j, 0)) if mode == "nt" else pl.BlockSpec((k, tn), lambda i, j: (0, j))
    dims = dict(nn=NN, nt=NT, tn=TN)[mode]
    o_spec = pl.BlockSpec((tm, tn), lambda i, j: (i, j))
    n_e = len(epi_in)

    def body(a_ref, b_ref, *refs):
        acc = _raw_dot(a_ref[...], b_ref[...], dims)
        outs = epi(acc, *[r[...] for r in refs[:n_e]]) if epi is not None else (acc,)
        for ref, val in zip(refs[n_e:], outs):
            ref[...] = val.astype(ref.dtype)

    grid = (m // tm, n // tn)
    ride_in, ride_out, ride_sems = ([], [], []) if rider is None else (rider.inputs, rider.out_shape, rider.sems)
    any_spec = pl.BlockSpec(memory_space=pl.ANY)
    res = pl.pallas_call(_ride(body, rider, (0, 2 + n_e, len(out_dtypes), 0), grid), name=name, grid=grid,
                         in_specs=[a_spec, b_spec] + [o_spec] * n_e + [any_spec] * len(ride_in),
                         out_specs=[o_spec] * len(out_dtypes) + [any_spec] * len(ride_out),
                         out_shape=[jax.ShapeDtypeStruct((m, n), d) for d in out_dtypes] + ride_out, scratch_shapes=ride_sems,
                         compiler_params=_params(("parallel", "parallel") if rider is None else ("arbitrary", "arbitrary")))(
                             a, b, *epi_in, *ride_in)
    own = res[0] if len(out_dtypes) == 1 else res[:len(out_dtypes)]
    return own if rider is None else (own, res[len(out_dtypes):])


def _rms(x, g):
    return x * lax.rsqrt(jnp.mean(x * x, axis=-1, keepdims=True) + EPS) * g


def _rms_bwd(x, g, dy):
    _, vjp = jax.vjp(_rms, x, g)
    return vjp(dy)


def _rope_tables(t_len, half):
    dim = 2 * half
    inv = 1.0 / (ROPE_THETA ** (jnp.arange(0, dim, 2, dtype=F32) / dim))
    ang = jnp.arange(t_len, dtype=F32)[:, None] * inv[None, :]
    c, s = jnp.cos(ang), jnp.sin(ang)
    reps = 128 // dim
    return jnp.tile(jnp.concatenate([c, c], 1), (1, reps)), jnp.tile(jnp.concatenate([-s, s], 1), (1, reps))


def _rope(x, cos, sin, half):
    w = x.shape[-1]
    reps = w // 128
    if reps > 1:
        cos, sin = jnp.tile(cos, (1, reps)), jnp.tile(sin, (1, reps))
    lane = lax.broadcasted_iota(jnp.int32, x.shape, 1)
    partner = jnp.where((lane % (2 * half)) < half, pltpu.roll(x, w - half, 1), pltpu.roll(x, half, 1))
    return x * cos + partner * sin


SWA_GROUP = SWA_HEADS // SWA_KV
SWA_QB = 4
SWA_TB_FWD, SWA_TB_BWD = 4, 2


def _swa_mask(tb, first_block):
    rows, keys = SWA_GROUP * tb * BLOCK, (tb + 1) * BLOCK
    qi = lax.broadcasted_iota(jnp.int32, (rows, keys), 0) % (tb * BLOCK) + BLOCK
    kj = lax.broadcasted_iota(jnp.int32, (rows, keys), 1)
    diff = qi - kj
    band = (diff >= 0) & (diff < BLOCK)
    return band if first_block is False else band & (jnp.logical_not(first_block) | (kj >= BLOCK))


def _swa_stack(ref, j, b, tb, dtype=None):
    rows = slice(BLOCK * b, BLOCK * (b + tb))
    parts = [ref[rows, 64 * (SWA_GROUP * j + g):64 * (SWA_GROUP * j + g) + 64] for g in range(SWA_GROUP)]
    out = jnp.concatenate(parts, axis=0)
    return out if dtype is None else out.astype(dtype)


def _swa_sinks(s_ref, j, tb):
    return jnp.concatenate([jnp.broadcast_to(s_ref[SWA_GROUP * j + g:SWA_GROUP * j + g + 1, 0:1], (tb * BLOCK, 1))
                            for g in range(SWA_GROUP)], axis=0)


def _swa_head(q, kw, vw, sink, mask):
    sc = _dnt(q, kw) * (64 ** -0.5)
    sc = jnp.where(mask, sc, NEG_INF)
    m = lax.stop_gradient(jnp.maximum(jnp.max(sc, axis=-1, keepdims=True), sink))
    p = jnp.exp(sc - m)
    denom = jnp.sum(p, axis=-1, keepdims=True) + jnp.exp(sink - m)
    return _dnn(p / denom, vw)


def _swa_prep(proj, cos, sin):
    def fn(sq, sk, sv, c, s):
        return _rope(sq, c, s, 32), _rope(sk, c, s, 32), sv
    t_len = proj.shape[0]
    return _rowwise("swa_prep", fn, [_seg(proj, "sq"), _seg(proj, "sk"), _seg(proj, "sv"), cos, sin], [],
                    [((t_len, 512), BF16), ((t_len, 128), BF16), ((t_len, 128), BF16)])


def _swa_specs(t_len, tb_max=1):
    qb = min(SWA_QB, t_len // BLOCK)
    tb = min(tb_max, qb)
    rows = qb * BLOCK
    cur = lambda w: pl.BlockSpec((rows, w), lambda i: (i, 0))
    prev = pl.BlockSpec((BLOCK, 128), lambda i: (jnp.maximum(qb * i - 1, 0), 0))
    return qb, tb, t_len // rows, cur, prev


def _swa_fwd(q, k, v, sinks):
    t_len = q.shape[0]
    qb, tb, steps, cur, prev = _swa_specs(t_len, SWA_TB_FWD)

    def body(q_ref, kp_ref, kc_ref, vp_ref, vc_ref, s_ref, o_ref):
        first = pl.program_id(0) == 0
        k_all = jnp.concatenate([kp_ref[...], kc_ref[...]], axis=0)
        v_all = jnp.concatenate([vp_ref[...], vc_ref[...]], axis=0)
        for j in range(SWA_KV):
            cols = slice(64 * j, 64 * j + 64)
            sink = _swa_sinks(s_ref, j, tb)
            for b in range(0, qb, tb):
                window = slice(BLOCK * b, BLOCK * (b + tb + 1))
                mask = _swa_mask(tb, first if b == 0 else False)
                o = _swa_head(_swa_stack(q_ref, j, b, tb), k_all[window, cols], v_all[window, cols], sink, mask)
                for g in range(SWA_GROUP):
                    h = SWA_GROUP * j + g
                    o_ref[BLOCK * b:BLOCK * (b + tb), 64 * h:64 * h + 64] = o[tb * BLOCK * g:tb * BLOCK * (g + 1)].astype(o_ref.dtype)

    return pl.pallas_call(
        body, name="swa_fwd", grid=(steps,),
        in_specs=[cur(512), prev, cur(128), prev, cur(128), _const_spec((8, 128))],
        out_specs=cur(512), out_shape=jax.ShapeDtypeStruct((t_len, 512), BF16),
        compiler_params=_params(("parallel",)))(q, k, k, v, v, sinks)


def _swa_bwd(q, k, v, sinks, dy):
    t_len = q.shape[0]
    qb, tb, steps, cur, prev = _swa_specs(t_len, SWA_TB_BWD)

    def body(q_ref, kp_ref, kc_ref, vp_ref, vc_ref, s_ref, dy_ref, dq_ref, dk_ref, dv_ref, dkp_ref, dvp_ref, ds_ref):
        first = pl.program_id(0) == 0

        @pl.when(first)
        def _():
            ds_ref[...] = jnp.zeros_like(ds_ref)

        k_all = jnp.concatenate([kp_ref[...], kc_ref[...]], axis=0)
        v_all = jnp.concatenate([vp_ref[...], vc_ref[...]], axis=0)
        for j in range(SWA_KV):
            cols = slice(64 * j, 64 * j + 64)
            sink = _swa_sinks(s_ref, j, tb)
            dk_parts = [jnp.zeros((BLOCK, 64), F32) for _ in range(qb + 1)]
            dv_parts = [jnp.zeros((BLOCK, 64), F32) for _ in range(qb + 1)]
            dsink = jnp.zeros((SWA_GROUP * tb * BLOCK, 1), F32)
            for b in range(0, qb, tb):
                window = slice(BLOCK * b, BLOCK * (b + tb + 1))
                mask = _swa_mask(tb, first if b == 0 else False)
                _, vjp = jax.vjp(lambda a, b_, c, d, mask=mask: _swa_head(a, b_, c, d, mask), _swa_stack(q_ref, j, b, tb),
                                 k_all[window, cols], v_all[window, cols], sink)
                dq, dkw, dvw, dsink_b = vjp(_swa_stack(dy_ref, j, b, tb, F32))
                dkw, dvw = dkw.astype(F32), dvw.astype(F32)
                for r in range(tb + 1):
                    dk_parts[b + r] = dk_parts[b + r] + dkw[BLOCK * r:BLOCK * (r + 1)]
                    dv_parts[b + r] = dv_parts[b + r] + dvw[BLOCK * r:BLOCK * (r + 1)]
                dsink = dsink + dsink_b
                for g in range(SWA_GROUP):
                    h = SWA_GROUP * j + g
                    dq_ref[BLOCK * b:BLOCK * (b + tb), 64 * h:64 * h + 64] = dq[tb * BLOCK * g:tb * BLOCK * (g + 1)].astype(F32)
            dkp_ref[:, cols], dvp_ref[:, cols] = dk_parts[0], dv_parts[0]
            for b in range(qb):
                dk_ref[BLOCK * b:BLOCK * b + BLOCK, cols] = dk_parts[b + 1]
                dv_ref[BLOCK * b:BLOCK * b + BLOCK, cols] = dv_parts[b + 1]
            for g in range(SWA_GROUP):
                h = SWA_GROUP * j + g
                ds_ref[h:h + 1, :] += jnp.broadcast_to(jnp.sum(dsink[tb * BLOCK * g:tb * BLOCK * (g + 1)], axis=0, keepdims=True), (1, 128))

    part = pl.BlockSpec((BLOCK, 128), lambda i: (i, 0))
    return pl.pallas_call(
        body, name="swa_bwd", grid=(steps,),
        in_specs=[cur(512), prev, cur(128), prev, cur(128), _const_spec((8, 128)), cur(512)],
        out_specs=[cur(512), cur(128), cur(128), part, part, _const_spec((8, 128))],
        out_shape=[jax.ShapeDtypeStruct((t_len, 512), F32), jax.ShapeDtypeStruct((t_len, 128), F32),
                   jax.ShapeDtypeStruct((t_len, 128), F32), jax.ShapeDtypeStruct((steps * BLOCK, 128), F32),
                   jax.ShapeDtypeStruct((steps * BLOCK, 128), F32), jax.ShapeDtypeStruct((8, 128), F32)],
        compiler_params=_params(("arbitrary",)))(q, k, k, v, v, sinks, dy)


def _swa_unprep(dq, dk, dv, dk_prev, dv_prev, cos, sin):
    t_len = dq.shape[0]
    qb, _, steps, cur, _ = _swa_specs(t_len)
    tail = (qb - 1) * BLOCK

    def body(dq_ref, dk_ref, dv_ref, dkn_ref, dvn_ref, c_ref, s_ref, oq_ref, ok_ref, ov_ref):
        more = (pl.program_id(0) < steps - 1).astype(F32)
        cos_t, sin_t = c_ref[...], -s_ref[...]
        oq_ref[...] = _rope(dq_ref[...], cos_t, sin_t, 32).astype(BF16)
        dk_t = jnp.concatenate([dk_ref[:tail], dk_ref[tail:] + more * dkn_ref[...]], axis=0) if tail else dk_ref[...] + more * dkn_ref[...]
        dv_t = jnp.concatenate([dv_ref[:tail], dv_ref[tail:] + more * dvn_ref[...]], axis=0) if tail else dv_ref[...] + more * dvn_ref[...]
        ok_ref[...] = _rope(dk_t, cos_t, sin_t, 32).astype(BF16)
        ov_ref[...] = dv_t.astype(BF16)

    nxt = pl.BlockSpec((BLOCK, 128), lambda i: (jnp.minimum(i + 1, steps - 1), 0))
    return pl.pallas_call(
        body, name="swa_unprep", grid=(steps,),
        in_specs=[cur(512), cur(128), cur(128), nxt, nxt, cur(128), cur(128)],
        out_specs=[cur(512), cur(128), cur(128)],
        out_shape=[jax.ShapeDtypeStruct((t_len, 512), BF16), jax.ShapeDtypeStruct((t_len, 128), BF16),
                   jax.ShapeDtypeStruct((t_len, 128), BF16)],
        compiler_params=_params(("parallel",)))(dq, dk, dv, dk_prev, dv_prev, cos, sin)


def _s5_discretize(lam_re, lam_im, log_dt, b_re, b_im):
    dt = jnp.exp(log_dt)
    mag = jnp.exp(lam_re * dt)
    ab_re, ab_im = mag * jnp.cos(lam_im * dt), mag * jnp.sin(lam_im * dt)
    den = lam_re * lam_re + lam_im * lam_im
    nr, ni = ab_re - 1.0, ab_im
    f_re = (nr * lam_re + ni * lam_im) / den
    f_im = (ni * lam_re - nr * lam_im) / den
    return ab_re, ab_im, f_re * b_re - f_im * b_im, f_re * b_im + f_im * b_re


def _s5_param_fwd(lam_re, lam_im, log_dt, b_re, b_im):
    def body(*refs):
        outs = _s5_discretize(*[r[...] for r in refs[:5]])
        for ref, val in zip(refs[5:], outs):
            ref[...] = val
    n = S5_CH
    return pl.pallas_call(body, name="s5_param_fwd",
                          out_shape=[jax.ShapeDtypeStruct((n, 1), F32)] * 2 + [jax.ShapeDtypeStruct((n, 16), F32)] * 2,
                          )(lam_re, lam_im, log_dt, b_re, b_im)


def _s5_param_bwd(lam_re, lam_im, log_dt, b_re, b_im, group_sum, d_ab_re, d_ab_im, d_bb_re, d_bb_im):
    def body(*refs):
        ins = [r[...] for r in refs[:10]]
        _, vjp = jax.vjp(_s5_discretize, *ins[:5])
        d_lr, d_li, d_ldt, d_br, d_bi = vjp(tuple(ins[6:10]))
        o_lr, o_li, o_ldt, o_br, o_bi = refs[10:]
        o_lr[...], o_li[...], o_br[...], o_bi[...] = d_lr, d_li, d_br, d_bi
        o_ldt[...] = lax.dot_general(ins[5], jnp.broadcast_to(d_ldt, (S5_CH, 128)), NN,
                                     precision=lax.Precision.HIGHEST, preferred_element_type=F32)
    n = S5_CH
    return pl.pallas_call(body, name="s5_param_bwd",
                          out_shape=[jax.ShapeDtypeStruct((n, 1), F32), jax.ShapeDtypeStruct((n, 1), F32),
                                     jax.ShapeDtypeStruct((S5_GROUPS, 128), F32),
                                     jax.ShapeDtypeStruct((n, 16), F32), jax.ShapeDtypeStruct((n, 16), F32)],
                          )(lam_re, lam_im, log_dt, b_re, b_im, group_sum, d_ab_re, d_ab_im, d_bb_re, d_bb_im)


def _s5_expand(name, u, first_band, w):
    t_len = u.shape[0]
    tile = min(512, t_len)
    seg = t_len // SCAN_SEG

    def body(u_ref, wr_ref, wi_ref, o_ref, up_ref, rows):
        base = pl.program_id(1) * (tile // SCAN_SEG)
        for r in range(tile // SCAN_SEG):
            rows[SCAN_SEG * r:SCAN_SEG * (r + 1), :] = u_ref[pl.ds(base + r, SCAN_SEG, stride=seg), :]
        u_t = rows[...].astype(BF16)
        up_ref[...] = u_t
        o_ref[0] = _raw_dot(u_t, wr_ref[...], NN)
        o_ref[1] = _raw_dot(u_t, wi_ref[...], NN)

    return pl.pallas_call(
        body, name=name, grid=(4, t_len // tile),
        in_specs=[pl.BlockSpec((t_len, 128), lambda b, i: (0, first_band + b)), pl.BlockSpec((None, 128, 512), lambda b, i: (b, 0, 0)),
                  pl.BlockSpec((None, 128, 512), lambda b, i: (4 + b, 0, 0))],
        out_specs=[pl.BlockSpec((2, tile, 512), lambda b, i: (0, i, b)), pl.BlockSpec((tile, 128), lambda b, i: (i, b))],
        out_shape=[jax.ShapeDtypeStruct((2, t_len, S5_CH), F32), jax.ShapeDtypeStruct((t_len, 512), BF16)],
        scratch_shapes=[pltpu.VMEM((tile, 128), F32)], compiler_params=_params(("parallel", "arbitrary")))(u, w, w)


def _s5_contract(name, x, w):
    t_len = x.shape[1]
    tile = min(512, t_len)
    seg = t_len // SCAN_SEG

    def body(x_ref, wr_ref, wi_ref, o_ref):
        y = _raw_dot(x_ref[0], wr_ref[...], NN) + _raw_dot(x_ref[1], wi_ref[...], NN)
        base = pl.program_id(1) * (tile // SCAN_SEG)
        for r in range(tile // SCAN_SEG):
            o_ref[pl.ds(base + r, SCAN_SEG, stride=seg), :] = y[SCAN_SEG * r:SCAN_SEG * (r + 1)]

    return pl.pallas_call(
        body, name=name, grid=(4, t_len // tile),
        in_specs=[pl.BlockSpec((2, tile, 512), lambda b, i: (0, i, b)), pl.BlockSpec((None, 512, 128), lambda b, i: (b, 0, 0)),
                  pl.BlockSpec((None, 512, 128), lambda b, i: (4 + b, 0, 0))],
        out_specs=pl.BlockSpec((t_len, 128), lambda b, i: (0, b)),
        out_shape=jax.ShapeDtypeStruct((t_len, 512), F32), compiler_params=_params(("parallel", "arbitrary")))(x, w, w)


def _s5_band_grad(name, narrow, wide, narrow_first):
    t_len = narrow.shape[0]

    def body(n_ref, w_ref, o_ref):
        if narrow_first:
            o_ref[...] = _raw_dot(n_ref[...], w_ref[...], TN)
        else:
            o_ref[...] = _raw_dot(w_ref[...], n_ref[...], TN)

    shape = (8, 128, 512) if narrow_first else (8, 512, 128)
    return pl.pallas_call(
        body, name=name, grid=(8,),
        in_specs=[pl.BlockSpec((t_len, 128), lambda bb: (0, bb % 4)), pl.BlockSpec((None, t_len, 512), lambda bb: (bb // 4, 0, bb % 4))],
        out_specs=pl.BlockSpec((None,) + shape[1:], lambda bb: (bb, 0, 0)),
        out_shape=jax.ShapeDtypeStruct(shape, F32), compiler_params=_params(("parallel",)))(narrow, wide)


def _s5_scan(name, bu, a_re, a_im, states=None, carry_in=None):
    t_len = bu.shape[1]
    n = t_len // SCAN_SEG
    log_n = int(round(math.log2(n)))
    assert 2 ** log_n == n
    reverse = states is not None
    lanes = SCAN_LANES
    nblk = S5_CH // lanes

    def body(*refs):
        if reverse:
            b_ref, ar_ref, ai_ref, x_ref, cin_ref, o_ref, dar_ref, dai_ref, c_scr = refs
            xr_ref, xi_ref, cr_ref, ci_ref = x_ref.at[0], x_ref.at[1], cin_ref.at[0], cin_ref.at[1]
        else:
            b_ref, ar_ref, ai_ref, o_ref, co_ref, c_scr = refs
            cor_ref, coi_ref = co_ref.at[0], co_ref.at[1]
        br_ref, bi_ref, or_ref, oi_ref = b_ref.at[0], b_ref.at[1], o_ref.at[0], o_ref.at[1]
        ar = ar_ref[...]
        ai = -ai_ref[...] if reverse else ai_ref[...]

        def tile_at(i):
            return pl.multiple_of((n - 1 - i if reverse else i) * SCAN_SEG, SCAN_SEG)

        def advance(s, row):
            sr, si = s
            return (ar * sr - ai * si + br_ref[pl.ds(row, SCAN_SEG), :], ar * si + ai * sr + bi_ref[pl.ds(row, SCAN_SEG), :])

        zero = jnp.zeros((SCAN_SEG, lanes), F32)
        fin_r, fin_i = lax.fori_loop(0, n, lambda i, s: advance(s, tile_at(i)), (zero, zero), unroll=8)
        pr, pi = ar[0:1], ai[0:1]
        for _ in range(log_n):
            pr, pi = pr * pr - pi * pi, 2.0 * pr * pi
        er = ei = jnp.zeros((1, lanes), F32)
        for step in range(SCAN_SEG):
            j = SCAN_SEG - 1 - step if reverse else step
            c_scr[0, j:j + 1, :] = er
            c_scr[1, j:j + 1, :] = ei
            er, ei = pr * er - pi * ei + fin_r[j:j + 1], pr * ei + pi * er + fin_i[j:j + 1]
        entry = (c_scr[0], c_scr[1])

        if not reverse:
            cor_ref[...], coi_ref[...] = entry

            def step2(i, s):
                row = tile_at(i)
                s = advance(s, row)
                or_ref[pl.ds(row, SCAN_SEG), :] = s[0]
                oi_ref[pl.ds(row, SCAN_SEG), :] = s[1]
                return s
            lax.fori_loop(0, n, step2, entry, unroll=8)
        else:
            def emit(s, row, xr_prev, xi_prev, acc):
                s = advance(s, row)
                or_ref[pl.ds(row, SCAN_SEG), :] = s[0]
                oi_ref[pl.ds(row, SCAN_SEG), :] = s[1]
                return s, (acc[0] + s[0] * xr_prev + s[1] * xi_prev, acc[1] + s[1] * xr_prev - s[0] * xi_prev)

            def step2(i, carry):
                s, acc = carry
                row = tile_at(i)
                prev = pl.multiple_of(row - SCAN_SEG, SCAN_SEG)
                return emit(s, row, xr_ref[pl.ds(prev, SCAN_SEG), :], xi_ref[pl.ds(prev, SCAN_SEG), :], acc)
            s, acc = lax.fori_loop(0, n - 1, step2, (entry, (zero, zero)), unroll=8)
            s, acc = emit(s, 0, cr_ref[...], ci_ref[...], acc)
            dar_ref[...] = jnp.sum(acc[0], axis=0, keepdims=True)
            dai_ref[...] = jnp.sum(acc[1], axis=0, keepdims=True)

    planes = pl.BlockSpec((2, t_len, lanes), lambda b: (0, 0, b))
    tile8 = pl.BlockSpec((SCAN_SEG, lanes), lambda b: (0, b))
    entry_spec = pl.BlockSpec((2, SCAN_SEG, lanes), lambda b: (0, 0, b))
    row1 = pl.BlockSpec((1, lanes), lambda b: (0, b))
    seq = jax.ShapeDtypeStruct((2, t_len, S5_CH), F32)
    if reverse:
        in_specs = [planes, tile8, tile8, planes, entry_spec]
        args = (bu, a_re, a_im, states, carry_in)
        out_specs = [planes, row1, row1]
        out_shape = [seq, jax.ShapeDtypeStruct((1, S5_CH), F32), jax.ShapeDtypeStruct((1, S5_CH), F32)]
    else:
        in_specs = [planes, tile8, tile8]
        args = (bu, a_re, a_im)
        out_specs = [planes, entry_spec]
        out_shape = [seq, jax.ShapeDtypeStruct((2, SCAN_SEG, S5_CH), F32)]
    return pl.pallas_call(body, name=name, grid=(nblk,), in_specs=in_specs, out_specs=out_specs, out_shape=out_shape,
                          scratch_shapes=[pltpu.VMEM((2, SCAN_SEG, lanes), F32)],
                          compiler_params=_params(("parallel",)))(*args)


def _ret_consts():
    h = np.arange(RET_HEADS, dtype=np.float32)
    log_gamma = np.log1p(-np.exp2(-5.0 - h)).astype(np.float32)
    idx = np.arange(BLOCK, dtype=np.float32)
    diff = idx[:, None] - idx[None, :]
    decay = np.where(diff >= 0, np.exp(log_gamma[:, None, None] * np.maximum(diff, 0.0)), 0.0).astype(np.float32)
    k_w = np.exp(log_gamma[:, None] * (BLOCK - 1 - idx)[None, :]).astype(np.float32)[:, :, None]
    q_w = np.exp(log_gamma[:, None] * (idx + 1.0)[None, :]).astype(np.float32)[:, :, None]
    chunk_decay = [float(v) for v in np.exp(log_gamma * BLOCK).astype(np.float32)]
    return jnp.asarray(decay), jnp.asarray(q_w), jnp.asarray(k_w), chunk_decay


def _ret_head(q, k, v, g, prev, decay, q_w, k_w, chunk_decay):
    inner_s = _dnt(q, k) * decay
    y = _dnn(inner_s, v) + _dnn(q * q_w, prev)
    mu = jnp.mean(y, axis=-1, keepdims=True)
    var = jnp.mean(jnp.square(y - mu), axis=-1, keepdims=True)
    yn = (y - mu) * lax.rsqrt(var + EPS)
    out = g * jax.nn.sigmoid(g) * yn
    return out, prev * chunk_decay + _dtn(k * k_w, v)


def _ret_prep(proj, cos, sin):
    def fn(rq, rk, c, s):
        return _rope(rq, c, s, 32), _rope(rk, c, s, 32) * (RET_DK ** -0.5)
    t_len = proj.shape[0]
    return _rowwise("ret_prep", fn, [_seg(proj, "rq"), _seg(proj, "rk"), cos, sin], [],
                    [((t_len, 256), F32), ((t_len, 256), F32)])


def _ret_unprep(dq, dk, cos, sin):
    def fn(dq_t, dk_t, c, s):
        return _rope(dq_t, c, -s, 32), _rope(dk_t * (RET_DK ** -0.5), c, -s, 32)
    t_len = dq.shape[0]
    return _rowwise("ret_unprep", fn, [dq, dk, cos, sin], [], [((t_len, 256), BF16), ((t_len, 256), BF16)])


def _ret_fwd(q, k, proj):
    t_len = q.shape[0]
    nc = t_len // BLOCK
    decay, q_w, k_w, chunk_decay = _ret_consts()

    def body(q_ref, k_ref, v_ref, g_ref, dec_ref, qw_ref, kw_ref, o_ref, prev_ref, state):
        @pl.when(pl.program_id(0) == 0)
        def _():
            state[...] = jnp.zeros_like(state)
        for h in range(RET_HEADS):
            qk, vv = slice(64 * h, 64 * h + 64), slice(128 * h, 128 * h + 128)
            prev = state[h]
            prev_ref[h] = prev
            out, nxt = _ret_head(q_ref[:, qk], k_ref[:, qk], v_ref[:, vv], g_ref[:, vv], prev,
                                 dec_ref[h], qw_ref[h], kw_ref[h], chunk_decay[h])
            o_ref[:, vv] = out.astype(o_ref.dtype)
            state[h] = nxt

    chunk = lambda w, c=0: pl.BlockSpec((BLOCK, w), lambda i, c=c: (i, c))
    return pl.pallas_call(
        body, name="ret_fwd", grid=(nc,),
        in_specs=[chunk(256), chunk(256), chunk(512, PROJ["rv"][0] // 512), chunk(512, PROJ["rg"][0] // 512),
                  _const_spec(decay.shape), _const_spec(q_w.shape), _const_spec(k_w.shape)],
        out_specs=[chunk(512), pl.BlockSpec((None, RET_HEADS, RET_DK, RET_DV), lambda i: (i, 0, 0, 0))],
        out_shape=[jax.ShapeDtypeStruct((t_len, 512), BF16), jax.ShapeDtypeStruct((nc, RET_HEADS, RET_DK, RET_DV), F32)],
        scratch_shapes=[pltpu.VMEM((RET_HEADS, RET_DK, RET_DV), F32)],
        compiler_params=_params(("arbitrary",)))(q, k, proj, proj, decay, q_w, k_w)


def _ret_bwd(q, k, proj, prevs, dy):
    t_len = q.shape[0]
    nc = t_len // BLOCK
    decay, q_w, k_w, chunk_decay = _ret_consts()

    def body(q_ref, k_ref, v_ref, g_ref, prev_ref, dy_ref, dec_ref, qw_ref, kw_ref, dq_ref, dk_ref, dv_ref, dg_ref, dstate):
        @pl.when(pl.program_id(0) == 0)
        def _():
            dstate[...] = jnp.zeros_like(dstate)
        for h in range(RET_HEADS):
            qk, vv = slice(64 * h, 64 * h + 64), slice(128 * h, 128 * h + 128)
            head = functools.partial(_ret_head, decay=dec_ref[h], q_w=qw_ref[h], k_w=kw_ref[h], chunk_decay=chunk_decay[h])
            _, vjp = jax.vjp(head, q_ref[:, qk], k_ref[:, qk], v_ref[:, vv], g_ref[:, vv], prev_ref[h])
            dq, dk, dv, dg, dprev = vjp((dy_ref[:, vv].astype(F32), dstate[h]))
            dq_ref[:, qk], dk_ref[:, qk] = dq, dk
            dv_ref[:, vv], dg_ref[:, vv] = dv.astype(dv_ref.dtype), dg.astype(dg_ref.dtype)
            dstate[h] = dprev

    chunk = lambda w, c=0: pl.BlockSpec((BLOCK, w), lambda i, c=c: (nc - 1 - i, c))
    return pl.pallas_call(
        body, name="ret_bwd", grid=(nc,),
        in_specs=[chunk(256), chunk(256), chunk(512, PROJ["rv"][0] // 512), chunk(512, PROJ["rg"][0] // 512),
                  pl.BlockSpec((None, RET_HEADS, RET_DK, RET_DV), lambda i: (nc - 1 - i, 0, 0, 0)), chunk(512),
                  _const_spec(decay.shape), _const_spec(q_w.shape), _const_spec(k_w.shape)],
        out_specs=[chunk(256), chunk(256), chunk(512), chunk(512)],
        out_shape=[jax.ShapeDtypeStruct((t_len, 256), F32), jax.ShapeDtypeStruct((t_len, 256), F32),
                   jax.ShapeDtypeStruct((t_len, 512), BF16), jax.ShapeDtypeStruct((t_len, 512), BF16)],
        scratch_shapes=[pltpu.VMEM((RET_HEADS, RET_DK, RET_DV), F32)],
        compiler_params=_params(("arbitrary",)))(q, k, proj, proj, prevs, dy, decay, q_w, k_w)


MLA_SCALE = 96 ** -0.5


def _mla_prep(qf, kvf, proj, cos, sin):
    def fn(q, kv, kr, c, s):
        q_rope = _rope(q[:, 512:768], c, s, 16)
        k_rope = _rope(kr, c, s, 16)[:, :32]
        zero = jnp.zeros_like(k_rope)
        qs = [jnp.concatenate([q[:, 64 * h:64 * h + 64], q_rope[:, 32 * h:32 * h + 32], zero], axis=1) for h in range(MLA_HEADS)]
        ks = [jnp.concatenate([kv[:, 64 * h:64 * h + 64], k_rope, zero], axis=1) for h in range(MLA_HEADS)]
        ones = (lax.broadcasted_iota(jnp.int32, (kv.shape[0], 64), 1) == 0).astype(F32)
        vs = [jnp.concatenate([kv[:, 512 + 64 * h:576 + 64 * h], ones], axis=1) for h in range(MLA_HEADS)]
        return qs, ks, vs
    t_len = qf.shape[0]
    return _rowwise("mla_prep", fn, [qf, kvf, _seg(proj, "kr"), cos, sin], [],
                    [((8, t_len, 128), BF16), ((8, t_len, 128), BF16), ((8, t_len, 128), BF16)])


def _mla_unprep(dq, dk, dv, cos, sin):
    def fn(dq_t, dk_t, dv_t, c, s):
        q_rope = _rope(jnp.concatenate([dq_t[h][:, 64:96] for h in range(MLA_HEADS)], axis=1), c, -s, 16)
        d_qf = jnp.concatenate([dq_t[h][:, :64] for h in range(MLA_HEADS)] + [q_rope], axis=1)
        d_kvf = jnp.concatenate([dk_t[h][:, :64] for h in range(MLA_HEADS)] + [dv_t[h][:, :64] for h in range(MLA_HEADS)], axis=1)
        k_rope = dk_t[0][:, 64:96]
        for h in range(1, MLA_HEADS):
            k_rope = k_rope + dk_t[h][:, 64:96]
        d_kr = _rope(jnp.concatenate([k_rope, jnp.zeros((k_rope.shape[0], 96), F32)], axis=1), c, -s, 16)
        return d_qf, d_kvf, d_kr
    t_len = dq.shape[1]
    return _rowwise("mla_unprep", fn, [dq, dk, dv, cos, sin], [],
                    [((t_len, 768), BF16), ((t_len, 1024), BF16), ((t_len, 128), BF16)])


MLA_EXP2 = MLA_SCALE * math.log2(math.e)
MLA_TILE = 1024


def _diag_mask(tq):
    return lax.broadcasted_iota(jnp.int32, (tq, tq), 1) <= lax.broadcasted_iota(jnp.int32, (tq, tq), 0)


def _tri_steps(nq, key_major):
    if key_major:
        pairs = [(i, j) for j in range(nq) for i in range(j, nq)]
    else:
        pairs = [(i, j) for i in range(nq) for j in range(i + 1)]
    return jnp.asarray([p[0] for p in pairs], jnp.int32), jnp.asarray([p[1] for p in pairs], jnp.int32)


def _ride(body, rider, counts, grid):
    if rider is None:
        return body
    extra = (0, len(rider.inputs), len(rider.out_shape), len(rider.sems))

    def wrapped(*refs):
        host, ridden, pos = [], [], 0
        for mine, theirs in zip(counts, extra):
            host += refs[pos:pos + mine]
            ridden.append(refs[pos + mine:pos + mine + theirs])
            pos += mine + theirs
        ids = [pl.program_id(a) for a in range(len(grid))]
        first, last = ids[0] == 0, ids[0] == grid[0] - 1
        for a in range(1, len(grid)):
            first, last = first & (ids[a] == 0), last & (ids[a] == grid[a] - 1)

        @pl.when(first)
        def _():
            rider.start(*ridden[1:])

        body(*host)

        @pl.when(last)
        def _():
            rider.finish(*ridden[1:])

    return wrapped


def _mla_fwd(q, k, v, rider=None):
    t_len = q.shape[1]
    tq = min(MLA_TILE, t_len)
    nq = t_len // tq
    qi, kj = _tri_steps(nq, False)

    def body(i_ref, j_ref, q_ref, k_ref, v_ref, o_ref, lse_ref, m_s, acc_s):
        step = pl.program_id(1)
        i, j = i_ref[step], j_ref[step]

        @pl.when(j == 0)
        def _():
            m_s[...] = jnp.full_like(m_s, NEG_INF)
            acc_s[...] = jnp.zeros_like(acc_s)

        def update(mask):
            for e in range(2):
                s = _raw_dot(q_ref[e], k_ref[e], NT)
                if mask is not None:
                    s = jnp.where(mask, s, NEG_INF)
                m_old = m_s[e]
                m_new = jnp.maximum(m_old, jnp.max(s, axis=-1, keepdims=True))
                p = jnp.exp2((s - m_new) * MLA_EXP2)
                acc_s[e] = jnp.exp2((m_old - m_new) * MLA_EXP2) * acc_s[e] + _raw_dot(p, v_ref[e], NN)
                m_s[e] = m_new

        @pl.when(j < i)
        def _():
            update(None)

        @pl.when(j == i)
        def _():
            update(_diag_mask(tq))
            outs = []
            for e in range(2):
                acc = acc_s[e]
                denom = acc[:, 64:65]
                outs.append(acc[:, :64] / denom)
                lse_ref[e] = m_s[e] * MLA_EXP2 + jnp.log2(denom)
            o_ref[...] = jnp.concatenate(outs, axis=1)

    grid = (MLA_HEADS // 2, int(qi.shape[0]))
    ride_in, ride_out, ride_sems = ([], [], []) if rider is None else (rider.inputs, rider.out_shape, rider.sems)
    grid_spec = pltpu.PrefetchScalarGridSpec(
        num_scalar_prefetch=2, grid=grid,
        in_specs=[pl.BlockSpec((2, tq, 128), lambda hp, s, qi, kj: (hp, qi[s], 0)),
                  pl.BlockSpec((2, tq, 128), lambda hp, s, qi, kj: (hp, kj[s], 0)),
                  pl.BlockSpec((2, tq, 128), lambda hp, s, qi, kj: (hp, kj[s], 0))] + [ANY_SPEC] * len(ride_in),
        out_specs=[pl.BlockSpec((tq, 128), lambda hp, s, qi, kj: (qi[s], hp)),
                   pl.BlockSpec((2, tq, 1), lambda hp, s, qi, kj: (hp, qi[s], 0))] + [ANY_SPEC] * len(ride_out),
        scratch_shapes=[pltpu.VMEM((2, tq, 1), F32), pltpu.VMEM((2, tq, 128), F32)] + ride_sems)
    res = pl.pallas_call(
        _ride(body, rider, (2, 3, 2, 2), grid), name="mla_fwd" if rider is None else "mla_fwd_gather", grid_spec=grid_spec,
        out_shape=[jax.ShapeDtypeStruct((t_len, 512), F32), jax.ShapeDtypeStruct((MLA_HEADS, t_len, 1), F32)] + ride_out,
        compiler_params=_params(("parallel" if rider is None else "arbitrary", "arbitrary")))(qi, kj, q, k, v, *ride_in)
    return res[0], res[1], res[2:]


def _mla_bwd_prep(o, dy):
    def fn(o_t, dy_t):
        dos, deltas = [], []
        for h in range(MLA_HEADS):
            d_h = dy_t[:, 64 * h:64 * h + 64].astype(F32)
            dos.append(jnp.concatenate([d_h, jnp.zeros_like(d_h)], axis=1))
            deltas.append(jnp.sum(d_h * o_t[:, 64 * h:64 * h + 64], axis=-1, keepdims=True))
        return dos, deltas
    t_len = o.shape[0]
    return _rowwise("mla_bwd_prep", fn, [o, dy], [], [((8, t_len, 128), BF16), ((8, t_len, 1), F32)])


def _mla_bwd(q, k, v, do, lse2, delta, rider=None):
    t_len = q.shape[1]
    tq = min(MLA_TILE, t_len)
    nq = t_len // tq
    qi, kj = _tri_steps(nq, True)

    def body(i_ref, j_ref, q_ref, k_ref, v_ref, do_ref, lse_ref, dl_ref, dq_ref, dk_ref, dv_ref, k_acc, v_acc):
        step = pl.program_id(1)
        i, j = i_ref[step], j_ref[step]

        @pl.when(step == 0)
        def _():
            dq_ref[...] = jnp.zeros_like(dq_ref)

        @pl.when(i == j)
        def _():
            k_acc[...] = jnp.zeros_like(k_acc)
            v_acc[...] = jnp.zeros_like(v_acc)

        def tile(mask):
            p = jnp.exp2(_raw_dot(q_ref[...], k_ref[...], NT) * MLA_EXP2 - lse_ref[...])
            if mask is not None:
                p = jnp.where(mask, p, 0.0)
            ds = p * (_raw_dot(do_ref[...], v_ref[...], NT) - dl_ref[...]) * MLA_SCALE
            v_acc[...] += _raw_dot(p, do_ref[...], TN)
            k_acc[...] += _raw_dot(ds, q_ref[...], TN)
            rows = pl.ds(pl.multiple_of(i * tq, tq), tq)
            dq_ref[rows, :] += _raw_dot(ds, k_ref[...], NN)

        @pl.when(i == j)
        def _():
            tile(_diag_mask(tq))

        @pl.when(i > j)
        def _():
            tile(None)

        @pl.when(i == nq - 1)
        def _():
            dk_ref[...] = k_acc[...]
            dv_ref[...] = v_acc[...]

    q_blk = lambda w: pl.BlockSpec((None, tq, w), lambda h, s, qi, kj: (h, qi[s], 0))
    k_blk = lambda w: pl.BlockSpec((None, tq, w), lambda h, s, qi, kj: (h, kj[s], 0))
    grid = (MLA_HEADS, int(qi.shape[0]))
    ride_in, ride_out, ride_sems = ([], [], []) if rider is None else (rider.inputs, rider.out_shape, rider.sems)
    grid_spec = pltpu.PrefetchScalarGridSpec(
        num_scalar_prefetch=2, grid=grid,
        in_specs=[q_blk(128), k_blk(128), k_blk(128), q_blk(128), q_blk(1), q_blk(1)] + [ANY_SPEC] * len(ride_in),
        out_specs=[pl.BlockSpec((None, t_len, 128), lambda h, s, qi, kj: (h, 0, 0)), k_blk(128), k_blk(128)]
        + [ANY_SPEC] * len(ride_out),
        scratch_shapes=[pltpu.VMEM((tq, 128), F32), pltpu.VMEM((tq, 128), F32)] + ride_sems)
    res = pl.pallas_call(
        _ride(body, rider, (2, 6, 3, 2), grid), name="mla_bwd" if rider is None else "mla_bwd_scatter", grid_spec=grid_spec,
        out_shape=[jax.ShapeDtypeStruct((MLA_HEADS, t_len, 128), F32)] * 3 + ride_out,
        compiler_params=_params(("parallel" if rider is None else "arbitrary", "arbitrary")))(qi, kj, q, k, v, do, lse2, delta, *ride_in)
    return res[0], res[1], res[2], res[3:]


MERGE_TN = 256
GATE_BLOCK0 = PROJ["gates"][0] // MERGE_TN


def _merge_specs(tm):
    y_spec = pl.BlockSpec((tm, 512), lambda i, j: (i, 0))
    w_spec = pl.BlockSpec((512, MERGE_TN), lambda i, j: (0, j))
    gate = lambda b: pl.BlockSpec((tm, MERGE_TN), lambda i, j, b=b: (i, GATE_BLOCK0 + 4 * b + j))
    return [y_spec] * 4 + [w_spec] * 4 + [gate(b) for b in range(4)]


def _merge_fwd(ys, wbs, proj):
    t_len = proj.shape[0]
    tm = min(512, t_len)

    def body(*refs):
        y, w, gl, o_ref = refs[0:4], refs[4:8], refs[8:12], refs[12]
        acc = jnp.zeros((tm, MERGE_TN), F32)
        for b in range(4):
            acc += jax.nn.sigmoid(gl[b][...]) * _raw_dot(y[b][...], w[b][...], NN)
        o_ref[...] = acc.astype(o_ref.dtype)

    return pl.pallas_call(body, name="merge_fwd", grid=(t_len // tm, D_MODEL // MERGE_TN), in_specs=_merge_specs(tm),
                          out_specs=pl.BlockSpec((tm, MERGE_TN), lambda i, j: (i, j)),
                          out_shape=jax.ShapeDtypeStruct((t_len, D_MODEL), BF16),
                          compiler_params=_params(("parallel", "parallel")))(*ys, *wbs, *([proj] * 4))


def _merge_bwd(ys, wbs, proj, dmerged):
    t_len = proj.shape[0]
    tm = min(512, t_len)

    def body(*refs):
        y, w, gl, dm_ref, dgl, dt = refs[0:4], refs[4:8], refs[8:12], refs[12], refs[13:17], refs[17:21]
        dm = dm_ref[...]
        for b in range(4):
            gate = jax.nn.sigmoid(gl[b][...])
            t_b = _raw_dot(y[b][...], w[b][...], NN)
            dgl[b][...] = (dm * t_b * gate * (1.0 - gate)).astype(BF16)
            dt[b][...] = (dm * gate).astype(BF16)

    o_spec = pl.BlockSpec((tm, MERGE_TN), lambda i, j: (i, j))
    return pl.pallas_call(body, name="merge_bwd", grid=(t_len // tm, D_MODEL // MERGE_TN), in_specs=_merge_specs(tm) + [o_spec],
                          out_specs=[o_spec] * 8, out_shape=[jax.ShapeDtypeStruct((t_len, D_MODEL), BF16)] * 8,
                          compiler_params=_params(("parallel", "parallel")))(*ys, *wbs, *([proj] * 4), dmerged)


def _adamw(name, w, g, m, v, slabs):
    shape = w.shape
    size = int(np.prod(shape))
    if shape[-1] >= 96:
        view = (size // shape[-1], shape[-1])
    elif size % 128 == 0:
        view = (size // 128, 128)
    else:
        view = (1, size)
    rows = view[0]
    tile = rows
    for cand in (512, 256, 128, 64):
        if rows > cand and rows % cand == 0 and cand * view[1] * 4 <= 2 ** 20:
            tile = cand
            break
    if rows * view[1] * 4 <= 2 ** 20:
        tile = rows
    c1, c2 = 1.0 - ADAM_B1 ** ADAM_STEP, 1.0 - ADAM_B2 ** ADAM_STEP

    def body(w_ref, g_ref, m_ref, v_ref, og_ref, od_ref, om_ref, ov_ref):
        if slabs:
            grad = g_ref[0].astype(F32)
            for d in range(1, slabs):
                grad = grad + g_ref[d].astype(F32)
        else:
            grad = g_ref[...]
        m_new = ADAM_B1 * m_ref[...] + (1.0 - ADAM_B1) * grad
        v_new = ADAM_B2 * v_ref[...] + (1.0 - ADAM_B2) * jnp.square(grad)
        og_ref[...] = grad
        od_ref[...] = -ADAM_LR * ((m_new / c1) / (jnp.sqrt(v_new / c2) + ADAM_EPS) + ADAM_WD * w_ref[...])
        om_ref[...] = m_new
        ov_ref[...] = v_new

    spec = pl.BlockSpec((tile, view[1]), lambda i: (i, 0))
    g_spec = pl.BlockSpec((slabs, tile, view[1]), lambda i: (0, i, 0)) if slabs else spec
    g_view = g.reshape((slabs,) + view) if slabs else g.reshape(view)
    outs = pl.pallas_call(body, name=name, grid=(rows // tile,), in_specs=[spec, g_spec, spec, spec], out_specs=[spec] * 4,
                          out_shape=[jax.ShapeDtypeStruct(view, F32)] * 4,
                          compiler_params=_params(("parallel",)))(w.reshape(view), g_view, m.reshape(view), v.reshape(view))
    return [o.reshape(shape) for o in outs]


def _sum_slabs(name, x):
    def body(x_ref, o_ref):
        acc = x_ref[0]
        for d in range(1, N_DEV):
            acc = acc + x_ref[d]
        o_ref[...] = acc
    return pl.pallas_call(body, name=name, out_shape=jax.ShapeDtypeStruct(x.shape[1:], x.dtype))(x)


MESH_ID = pl.DeviceIdType.MESH
ANY_SPEC = pl.BlockSpec(memory_space=pl.ANY)


def _remote(src, dst, send_sems, recv_sems, k, to):
    return pltpu.make_async_remote_copy(src_ref=src, dst_ref=dst, send_sem=send_sems.at[k], recv_sem=recv_sems.at[k],
                                        device_id=to, device_id_type=MESH_ID)


class _Rider:
    def __init__(self, inputs, out_shape, sems, start, finish):
        self.inputs, self.out_shape, self.sems, self.start, self.finish = list(inputs), list(out_shape), list(sems), start, finish


def _run_rider(name, rider):
    n_in, n_out = len(rider.inputs), len(rider.out_shape)

    def body(*refs):
        parts = refs[:n_in], refs[n_in:n_in + n_out], refs[n_in + n_out:]
        rider.start(*parts)
        rider.finish(*parts)

    return pl.pallas_call(body, name=name, in_specs=[ANY_SPEC] * n_in, out_specs=[ANY_SPEC] * n_out, out_shape=rider.out_shape,
                          scratch_shapes=rider.sems, compiler_params=pltpu.CompilerParams(has_side_effects=True))(*rider.inputs)


def _gather_rider(blocks):
    n = len(blocks)

    def plan(srcs, dsts, sems, waits):
        send_sems, recv_sems, local_sems = sems
        x, y, c = lax.axis_index("x"), lax.axis_index("y"), lax.axis_index("c")
        sibling = (x, y, 1 - c)
        chips = [(1 - x, y), (x, 1 - y), (1 - x, 1 - y)]
        index = lambda px, py, pc: 4 * px + 2 * py + pc

        def copy(t, k, block, to, src=None):
            slot = dsts[t].at[block]
            return _remote(slot if src is None else src, slot, send_sems, recv_sems, 7 * t + k, to)

        me = index(x, y, c)
        own = [pltpu.make_async_copy(srcs[t], dsts[t].at[me], local_sems.at[t]) for t in range(n)]
        first = []
        for t in range(n):
            first.append(copy(t, 0, me, sibling, src=srcs[t]))
            first += [copy(t, 1 + j, me, (*chip, c), src=srcs[t]) for j, chip in enumerate(chips)]
        if not waits:
            return own, first, [], []
        relays = [(copy(t, 1 + j, index(*chip, c), sibling), copy(t, 4 + j, index(*chip, c), sibling))
                  for j, chip in enumerate(chips) for t in range(n)]
        from_sibling = [copy(t, 0, index(x, y, 1 - c), sibling) for t in range(n)]
        from_sibling += [copy(t, 4 + j, index(*chip, 1 - c), sibling) for t in range(n) for j, chip in enumerate(chips)]
        return own, first, relays, from_sibling

    def start(srcs, dsts, sems):
        own, first, _, _ = plan(srcs, dsts, sems, False)
        for cp in own + first:
            cp.start()

    def finish(srcs, dsts, sems):
        own, first, relays, from_sibling = plan(srcs, dsts, sems, True)
        for arrival, forward in relays:
            arrival.wait_recv()
            forward.start()
        for cp in from_sibling:
            cp.wait_recv()
        for cp in first + [forward for _, forward in relays]:
            cp.wait_send()
        for cp in own:
            cp.wait()

    return _Rider(blocks, [jax.ShapeDtypeStruct((N_DEV,) + b.shape, b.dtype) for b in blocks],
                  [pltpu.SemaphoreType.DMA((7 * n,)), pltpu.SemaphoreType.DMA((7 * n,)), pltpu.SemaphoreType.DMA((n,))],
                  start, finish)


def _pair_exchange(name, blocks):
    n = len(blocks)

    def body(*refs):
        srcs, dsts = refs[:n], refs[n:2 * n]
        send_sems, recv_sems = refs[2 * n:]
        sibling = (lax.axis_index("x"), lax.axis_index("y"), 1 - lax.axis_index("c"))
        copies = [_remote(srcs[t], dsts[t], send_sems, recv_sems, t, sibling) for t in range(n)]
        for cp in copies:
            cp.start()
        for cp in copies:
            cp.wait()

    return pl.pallas_call(
        body, name=name, in_specs=[ANY_SPEC] * n, out_specs=[ANY_SPEC] * n,
        out_shape=[jax.ShapeDtypeStruct(b.shape, b.dtype) for b in blocks],
        scratch_shapes=[pltpu.SemaphoreType.DMA((n,)), pltpu.SemaphoreType.DMA((n,))],
        compiler_params=pltpu.CompilerParams(has_side_effects=True))(*blocks)


def _chip_rider(items):
    n = len(items)
    out_shape = [jax.ShapeDtypeStruct((4,) + tuple(a.shape[1:] if per_chip else a.shape), a.dtype) for a, per_chip in items]

    def plan(srcs, dsts, sems, waits):
        send_sems, recv_sems, local_sems = sems
        x, y, c = lax.axis_index("x"), lax.axis_index("y"), lax.axis_index("c")
        mine = 2 * x + y
        own = [pltpu.make_async_copy(srcs[t].at[mine] if per_chip else srcs[t], dsts[t].at[mine], local_sems.at[t])
               for t, (_, per_chip) in enumerate(items)]
        sends, arrivals = [], []
        for k in range(1, 4):
            px, py = x ^ (k >> 1), y ^ (k & 1)
            peer = 2 * px + py
            for t, (_, per_chip) in enumerate(items):
                src = srcs[t].at[peer] if per_chip else srcs[t]
                sends.append(_remote(src, dsts[t].at[mine], send_sems, recv_sems, 3 * t + k - 1, (px, py, c)))
                if waits:
                    arrivals.append(_remote(src, dsts[t].at[peer], send_sems, recv_sems, 3 * t + k - 1, (px, py, c)))
        return own, sends, arrivals

    def start(srcs, dsts, sems):
        own, sends, _ = plan(srcs, dsts, sems, False)
        for cp in own + sends:
            cp.start()

    def finish(srcs, dsts, sems):
        own, sends, arrivals = plan(srcs, dsts, sems, True)
        for out, arrival in zip(sends, arrivals):
            out.wait_send()
            arrival.wait_recv()
        for cp in own:
            cp.wait()

    return _Rider([a for a, _ in items], out_shape,
                  [pltpu.SemaphoreType.DMA((3 * n,)), pltpu.SemaphoreType.DMA((3 * n,)), pltpu.SemaphoreType.DMA((n,))],
                  start, finish)


def _band_diag(p):
    a, b = p.shape[1:]
    eye = jnp.eye(8, dtype=p.dtype)
    return (p.reshape(4, 8, a, 1, b) * eye[None, :, None, :, None]).reshape(4, 8 * a, 8 * b)


def _band_diag_extract(m, a, b):
    eye = jnp.eye(8, dtype=m.dtype)
    return (m.reshape(4, 8, a, 8, b) * eye[None, :, None, :, None]).sum(axis=3).reshape(32, a, b)


def _proj_cols(w_ref_layout):
    parts = []
    for name in PROJ_ORDER:
        off, width = REF_SPLIT[name]
        part = w_ref_layout[:, off:off + width]
        if width < PROJ[name][1]:
            part = jnp.pad(part, ((0, 0), (0, PROJ[name][1] - width)))
        parts.append(part)
    return jnp.concatenate(parts, axis=1)


def _unproj_cols(w_proj_layout):
    parts = []
    for name in REF_ORDER:
        parts.append(w_proj_layout[:, PROJ[name][0]:PROJ[name][0] + REF_SPLIT[name][1]])
    return jnp.concatenate(parts, axis=1)


def _uq_cols(w):
    w3 = w.reshape(w.shape[0], MLA_HEADS, 96)
    return jnp.concatenate([w3[:, :, :64].reshape(-1, 512), w3[:, :, 64:].reshape(-1, 256)], axis=1)


def _un_uq_cols(w):
    return jnp.concatenate([w[:, :512].reshape(-1, MLA_HEADS, 64), w[:, 512:].reshape(-1, MLA_HEADS, 32)], axis=2).reshape(-1, 768)


def _ukv_cols(w):
    w3 = w.reshape(w.shape[0], MLA_HEADS, 128)
    return jnp.concatenate([w3[:, :, :64].reshape(-1, 512), w3[:, :, 64:].reshape(-1, 512)], axis=1)


def _un_ukv_cols(w):
    return jnp.concatenate([w[:, :512].reshape(-1, MLA_HEADS, 64), w[:, 512:].reshape(-1, MLA_HEADS, 64)], axis=2).reshape(-1, 1024)


def _cols_gathered(g):
    return g.transpose(1, 0, 2).reshape(g.shape[1], -1)


def _cols_slabs(w):
    r, c = w.shape
    return w.reshape(r, N_DEV, c // N_DEV).transpose(1, 0, 2)


def _row(v):
    return v.reshape(1, -1)


def _s5_weights(p):
    flat = dict(lam_re=p["s5_lam_re"].reshape(S5_CH, 1), lam_im=p["s5_lam_im"].reshape(S5_CH, 1),
                log_dt=jnp.repeat(p["s5_log_dt"], S5_STATE).reshape(S5_CH, 1),
                b_re=p["s5_b_re"].reshape(S5_CH, S5_GROUP), b_im=p["s5_b_im"].reshape(S5_CH, S5_GROUP))
    ab_re, ab_im, bb_re, bb_im = _s5_param_fwd(flat["lam_re"], flat["lam_im"], flat["log_dt"], flat["b_re"], flat["b_im"])
    w_bu = jnp.concatenate([_band_diag(bb.reshape(32, 64, 16).transpose(0, 2, 1)) for bb in (bb_re, bb_im)], axis=0)
    w_y = jnp.concatenate([_band_diag(p["s5_c_re"].transpose(0, 2, 1)), -_band_diag(p["s5_c_im"].transpose(0, 2, 1))], axis=0)
    return dict(flat=flat,
                a_re=jnp.broadcast_to(ab_re.reshape(1, S5_CH), (SCAN_SEG, S5_CH)),
                a_im=jnp.broadcast_to(ab_im.reshape(1, S5_CH), (SCAN_SEG, S5_CH)),
                w_bu=w_bu.astype(BF16), w_bu_t=w_bu.swapaxes(1, 2).astype(BF16),
                w_y=w_y.astype(BF16), w_y_t=w_y.swapaxes(1, 2).astype(BF16))


def _layer_fwd(x, w, p, tabs, late_shards=None, rider=None):
    t_len = x.shape[0]
    cos64, sin64, cos32, sin32 = tabs
    s = {}
    (s["h1"],) = _rowwise("pre_mix", lambda xt, g: (_rms(xt, g),), [x], [_row(p["g_pre_mix"])], [((t_len, D_MODEL), BF16)])
    if late_shards is None:
        proj = _mm("proj", s["h1"], w["w_in"], "nn")
    else:
        proj, late = _mm("proj_gather", s["h1"], w["w_in"], "nn", rider=_gather_rider(late_shards))
        w.update(_late_weights(late))
    s["proj"] = proj
    s["qa"], s["ka"], s["va"] = _swa_prep(proj, cos64, sin64)
    s["sinks"] = jnp.broadcast_to(p["swa_sinks"][:, None], (8, 128))
    y_a = _swa_fwd(s["qa"], s["ka"], s["va"], s["sinks"])
    s5 = s["s5"] = _s5_weights(p)
    bu, s["u_p"] = _s5_expand("s5_bu", proj, PROJ["su"][0] // 128, s5["w_bu"])
    s["xs"], s["cin"] = _s5_scan("s5_scan_fwd", bu, s5["a_re"], s5["a_im"])
    s["y_s"] = _s5_contract("s5_y", s["xs"], s5["w_y"])
    (s["z"],) = _rowwise("s5_gelu", lambda y, u, d: (jax.nn.gelu(y + d * u),), [s["y_s"], _seg(proj, "su")],
                         [_row(p["s5_d"])], [((t_len, 512), BF16)])
    s["zz"] = _mm("s5_glu_mm", s["z"], w["w_glu"], "nn")
    (y_b,) = _rowwise("s5_glu", lambda t: (t[:, :512] * jax.nn.sigmoid(t[:, 512:]),), [s["zz"]], [], [((t_len, 512), BF16)])
    s["qc"], s["kc"] = _ret_prep(proj, cos64, sin64)
    y_c, s["prevs"] = _ret_fwd(s["qc"], s["kc"], proj)
    s["cqn"], s["ckvn"] = _rowwise("mla_norm", lambda cq, ckv, gq, gkv: (_rms(cq, gq), _rms(ckv, gkv)),
                                   [_seg(proj, "cq"), _seg(proj, "ckv")], [_row(p["mla_g_q"]), _row(p["mla_g_kv"])],
                                   [((t_len, 256), BF16), ((t_len, 128), BF16)])
    qf = _mm("mla_uq", s["cqn"], w["w_uq"], "nn")
    kvf = _mm("mla_ukv", s["ckvn"], w["w_ukv"], "nn")
    s["Q"], s["K"], s["V"] = _mla_prep(qf, kvf, proj, cos32, sin32)
    y_d, s["lse"], ridden = _mla_fwd(s["Q"], s["K"], s["V"], rider)
    s["ys"] = [y_a, y_b, y_c, y_d]
    s["merged"] = _merge_fwd(s["ys"], w["wb"], proj)
    s["o"] = _mm("out_mm", s["merged"], w["w_out"], "nn")

    def post_mix(xt, ot, g1, g2):
        x1 = xt + _rms(ot, g1)
        return x1, _rms(x1, g2)
    s["x1"], s["h2"] = _rowwise("post_mix", post_mix, [x, s["o"]], [_row(p["g_post_mix"]), _row(p["g_pre_mlp"])],
                                [((t_len, D_MODEL), F32), ((t_len, D_MODEL), BF16)])
    s["a"], s["r"] = _mm("ff1", s["h2"], w["w_ff1"], "nn", out_dtypes=(F32, BF16),
                         epi=lambda acc: (acc, jnp.square(jnp.maximum(acc, 0.0))))
    s["f"] = _mm("ff2", s["r"], w["w_ff2"], "nn")
    (x2,) = _rowwise("post_mlp", lambda x1, f, g: (x1 + _rms(f, g),), [s["x1"], s["f"]], [_row(p["g_post_mlp"])],
                     [((t_len, D_MODEL), F32)])
    return x2, s, ridden


def _layer_bwd(x, dx2, w, p, s, tabs, pending=None, exchange_name=None):
    t_len = x.shape[0]
    cos64, sin64, cos32, sin32 = tabs
    proj = s["proj"]
    big, small = {}, {}

    def post_mlp_bwd(f, d, g):
        df, dg = _rms_bwd(f, g, d)
        return (df,), (dg,)
    df, small["g_post_mlp"] = _rowwise("post_mlp_bwd", post_mlp_bwd, [s["f"], dx2], [_row(p["g_post_mlp"])],
                                       [((t_len, D_MODEL), BF16)], [((1, D_MODEL), F32)])
    da = _mm("ff2_dx", df, w["w_ff2"], "nt", out_dtypes=(BF16,), epi=lambda acc, a: (acc * 2.0 * jnp.maximum(a, 0.0),),
             epi_in=(s["a"],))
    big["w_ff2"] = _mm("ff2_dw", s["r"], df, "tn")
    dh2 = _mm("ff1_dx", da, w["w_ff1"], "nt")
    big["w_ff1"] = _mm("ff1_dw", s["h2"], da, "tn")

    def post_mix_bwd(x1, o, dh, d2, g1, g2):
        dx1_n, dg2 = _rms_bwd(x1, g2, dh)
        dx1 = d2 + dx1_n
        do, dg1 = _rms_bwd(o, g1, dx1)
        return (dx1, do), (dg1, dg2)
    dx1, do, small["g_post_mix"], small["g_pre_mlp"] = _rowwise(
        "post_mix_bwd", post_mix_bwd, [s["x1"], s["o"], dh2, dx2], [_row(p["g_post_mix"]), _row(p["g_pre_mlp"])],
        [((t_len, D_MODEL), F32), ((t_len, D_MODEL), BF16)], [((1, D_MODEL), F32), ((1, D_MODEL), F32)])
    dmerged = _mm("out_dx", do, w["w_out"], "nt")
    big["w_out"] = _mm("out_dw", s["merged"], do, "tn")
    rider, n_pending = None, len(pending or ())
    if exchange_name is not None:
        late_sums, _ = _pair_stage(exchange_name, _grad_slabs(big, LATE))
        rider = _chip_rider([(part, True) for part in list(pending or ()) + late_sums])

    res = _merge_bwd(s["ys"], w["wb"], proj, dmerged)
    dgl, dts = res[:4], res[4:]
    dys = [_mm("branch_dx", dts[b], w["wb"][b], "nt") for b in range(4)]
    big["wb"] = [_mm("branch_dw", s["ys"][b], dts[b], "tn") for b in range(4)]
    seg = {}

    dqa, dka, dva, dka_prev, dva_prev, dsink = _swa_bwd(s["qa"], s["ka"], s["va"], s["sinks"], dys[0])
    seg["sq"], seg["sk"], seg["sv"] = _swa_unprep(dqa, dka, dva, dka_prev, dva_prev, cos64, sin64)
    small["swa_sinks"] = dsink[:, 0]

    def glu_bwd(t, d):
        za, sg = t[:, :512], jax.nn.sigmoid(t[:, 512:])
        return (jnp.concatenate([d * sg, d * za * sg * (1.0 - sg)], axis=1),)
    (dzz,) = _rowwise("s5_glu_bwd", glu_bwd, [s["zz"], dys[1]], [], [((t_len, 1024), BF16)])
    dz = _mm("s5_glu_dx", dzz, w["w_glu"], "nt")
    big["w_glu"] = _mm("s5_glu_dw", s["z"], dzz, "tn")

    def gelu_bwd(y, u, d, skip):
        _, vjp = jax.vjp(jax.nn.gelu, y + skip * u)
        (dy2,) = vjp(d)
        return (dy2, dy2 * skip), (jnp.sum(dy2 * u, axis=0, keepdims=True),)
    dy_s, du_skip, d_skip = _rowwise("s5_gelu_bwd", gelu_bwd, [s["y_s"], _seg(proj, "su"), dz], [_row(p["s5_d"])],
                                     [((t_len, 512), F32), ((t_len, 512), F32)], [((1, 512), F32)])
    small["s5_d"] = d_skip[0]
    s5 = s["s5"]
    dlam_in, dy_p = _s5_expand("s5_dx_in", dy_s, 0, s5["w_y_t"])
    lam, da_re, da_im = _s5_scan("s5_scan_bwd", dlam_in, s5["a_re"], s5["a_im"], states=s["xs"], carry_in=s["cin"])
    du_s5 = _s5_contract("s5_du", lam, s5["w_bu_t"])
    d_w_bu = _s5_band_grad("s5_dbu", s["u_p"], lam, True)
    d_w_y = _s5_band_grad("s5_dc", dy_p, s["xs"], False)
    (seg["su"],) = _rowwise("s5_du_sum", lambda a, b: (a + b,), [du_s5, du_skip], [], [((t_len, 512), BF16)])
    d_bb_re = _band_diag_extract(d_w_bu[:4], 16, 64).transpose(0, 2, 1).reshape(S5_CH, S5_GROUP)
    d_bb_im = _band_diag_extract(d_w_bu[4:], 16, 64).transpose(0, 2, 1).reshape(S5_CH, S5_GROUP)
    small["s5_c_re"] = _band_diag_extract(d_w_y[:4], 64, 16).transpose(0, 2, 1)
    small["s5_c_im"] = -_band_diag_extract(d_w_y[4:], 64, 16).transpose(0, 2, 1)
    fl = s5["flat"]
    group_sum = jnp.repeat(jnp.eye(S5_GROUPS, dtype=F32), S5_STATE, axis=1)
    d_lr, d_li, d_ldt, d_br, d_bi = _s5_param_bwd(fl["lam_re"], fl["lam_im"], fl["log_dt"], fl["b_re"], fl["b_im"], group_sum,
                                                  da_re.reshape(S5_CH, 1), da_im.reshape(S5_CH, 1), d_bb_re, d_bb_im)
    small["s5_lam_re"], small["s5_lam_im"] = d_lr.reshape(32, 64), d_li.reshape(32, 64)
    small["s5_log_dt"] = d_ldt[:, 0]
    small["s5_b_re"], small["s5_b_im"] = d_br.reshape(32, 64, 16), d_bi.reshape(32, 64, 16)

    dqc, dkc, seg["rv"], seg["rg"] = _ret_bwd(s["qc"], s["kc"], proj, s["prevs"], dys[2])
    seg["rq"], seg["rk"] = _ret_unprep(dqc, dkc, cos64, sin64)

    do_h, delta = _mla_bwd_prep(s["ys"][3], dys[3])
    d_q, d_k, d_v, ridden = _mla_bwd(s["Q"], s["K"], s["V"], do_h, s["lse"], delta, rider)
    dqf, dkvf, seg["kr"] = _mla_unprep(d_q, d_k, d_v, cos32, sin32)
    dcqn = _mm("mla_uq_dx", dqf, w["w_uq"], "nt")
    big["w_uq"] = _mm("mla_uq_dw", s["cqn"], dqf, "tn")
    dckvn = _mm("mla_ukv_dx", dkvf, w["w_ukv"], "nt")
    big["w_ukv"] = _mm("mla_ukv_dw", s["ckvn"], dkvf, "tn")

    def mla_norm_bwd(cq, ckv, d1, d2, gq, gkv):
        dcq, dgq = _rms_bwd(cq, gq, d1)
        dckv, dgkv = _rms_bwd(ckv, gkv, d2)
        return (dcq, dckv), (dgq, dgkv)
    seg["cq"], seg["ckv"], dgq, dgkv = _rowwise(
        "mla_norm_bwd", mla_norm_bwd, [_seg(proj, "cq"), _seg(proj, "ckv"), dcqn, dckvn],
        [_row(p["mla_g_q"]), _row(p["mla_g_kv"])], [((t_len, 256), BF16), ((t_len, 128), BF16)],
        [((1, 256), F32), ((1, 128), F32)])
    small["mla_g_q"], small["mla_g_kv"] = dgq[0], dgkv[0]

    dproj = jnp.concatenate([seg[name] for name in PROJ_ORDER[:-1]] + list(dgl), axis=1)
    dh1 = _mm("proj_dx", dproj, w["w_in"], "nt")
    big["w_in"] = _mm("proj_dw", s["h1"], dproj, "tn")

    def pre_mix_bwd(xt, dh, d1, g):
        dxn, dg = _rms_bwd(xt, g, dh)
        return (d1 + dxn,), (dg,)
    dx, small["g_pre_mix"] = _rowwise("pre_mix_bwd", pre_mix_bwd, [x, dh1, dx1], [_row(p["g_pre_mix"])],
                                      [((t_len, D_MODEL), F32)], [((1, D_MODEL), F32)])
    for name in ("g_pre_mix", "g_post_mix", "g_pre_mlp", "g_post_mlp"):
        small[name] = small[name][0]
    return dx, big, small, ridden[:n_pending], ridden[n_pending:]


BIG = ("w_in", "s5_w_glu", "mla_w_uq", "mla_w_ukv", "w_branch", "w_out", "w_ff1", "w_ff2")
SMALL = ("g_pre_mix", "g_post_mix", "g_pre_mlp", "g_post_mlp", "swa_sinks", "s5_lam_re", "s5_lam_im", "s5_log_dt",
         "s5_b_re", "s5_b_im", "s5_c_re", "s5_c_im", "s5_d", "mla_g_q", "mla_g_kv")
WEIGHTS = ("g_pre_mix", "g_post_mix", "g_pre_mlp", "g_post_mlp", "w_in", "swa_sinks", "s5_lam_re", "s5_lam_im", "s5_log_dt",
           "s5_b_re", "s5_b_im", "s5_c_re", "s5_c_im", "s5_d", "s5_w_glu", "mla_g_q", "mla_g_kv", "mla_w_uq", "mla_w_ukv",
           "w_branch", "w_out", "w_ff1", "w_ff2")


EARLY = ("w_in", "s5_w_glu", "mla_w_uq", "mla_w_ukv", "w_branch")
LATE = ("w_out", "w_ff1", "w_ff2")


def _early_weights(shards):
    g = dict(zip(EARLY, shards))
    wb = g["w_branch"].transpose(1, 2, 0, 3).reshape(4, 512, D_MODEL)
    return dict(w_in=_proj_cols(_cols_gathered(g["w_in"])), w_glu=_cols_gathered(g["s5_w_glu"]),
                w_uq=_uq_cols(_cols_gathered(g["mla_w_uq"])), w_ukv=_ukv_cols(_cols_gathered(g["mla_w_ukv"])),
                wb=[wb[b] for b in range(4)])


def _late_weights(shards):
    g = dict(zip(LATE, shards))
    return dict(w_out=g["w_out"].reshape(D_MODEL, D_MODEL), w_ff1=_cols_gathered(g["w_ff1"]),
                w_ff2=g["w_ff2"].reshape(4 * D_MODEL, D_MODEL))


def _grad_slabs(big, names):
    make = dict(w_in=lambda: _cols_slabs(_unproj_cols(big["w_in"])), s5_w_glu=lambda: _cols_slabs(big["w_glu"]),
                mla_w_uq=lambda: _cols_slabs(_un_uq_cols(big["w_uq"])), mla_w_ukv=lambda: _cols_slabs(_un_ukv_cols(big["w_ukv"])),
                w_branch=lambda: jnp.stack(big["wb"]).reshape(4, 512, N_DEV, D_MODEL // N_DEV).transpose(2, 0, 1, 3),
                w_out=lambda: big["w_out"].reshape(N_DEV, D_MODEL // N_DEV, D_MODEL),
                w_ff1=lambda: _cols_slabs(big["w_ff1"]), w_ff2=lambda: big["w_ff2"].reshape(N_DEV, 4 * D_MODEL // N_DEV, D_MODEL))
    return [make[name]() for name in names]


def _pair_stage(name, slabs, extra=()):
    core = lax.axis_index("c")
    own, send = [], []
    for slab in slabs:
        by_chip = slab.reshape((4, 2) + slab.shape[1:])
        own.append(lax.dynamic_index_in_dim(by_chip, core, axis=1, keepdims=False))
        send.append(lax.dynamic_index_in_dim(by_chip, 1 - core, axis=1, keepdims=False).astype(BF16))
    got = _pair_exchange(name, send + list(extra))
    sums = []
    for mine, theirs in zip(own, got):
        flat = (-1, mine.shape[-1])
        (part,) = _rowwise("grads_pair_sum", lambda a, b: (a + b.astype(F32),), [mine.reshape(flat), theirs.reshape(flat)],
                           [], [((int(np.prod(mine.shape[:-1])), mine.shape[-1]), BF16)])
        sums.append(part.reshape(mine.shape))
    return sums, got[len(slabs):]


def kernel(x, g_pre_mix, g_post_mix, g_pre_mlp, g_post_mlp, w_in, swa_sinks, s5_lam_re, s5_lam_im, s5_log_dt, s5_b_re, s5_b_im, s5_c_re, s5_c_im, s5_d, s5_w_glu, mla_g_q, mla_g_kv, mla_w_uq, mla_w_ukv, w_branch, w_out, w_ff1, w_ff2, loss_target, m_g_pre_mix, m_g_post_mix, m_g_pre_mlp, m_g_post_mlp, m_w_in, m_swa_sinks, m_s5_lam_re, m_s5_lam_im, m_s5_log_dt, m_s5_b_re, m_s5_b_im, m_s5_c_re, m_s5_c_im, m_s5_d, m_s5_w_glu, m_mla_g_q, m_mla_g_kv, m_mla_w_uq, m_mla_w_ukv, m_w_branch, m_w_out, m_w_ff1, m_w_ff2, v_g_pre_mix, v_g_post_mix, v_g_pre_mlp, v_g_post_mlp, v_w_in, v_swa_sinks, v_s5_lam_re, v_s5_lam_im, v_s5_log_dt, v_s5_b_re, v_s5_b_im, v_s5_c_re, v_s5_c_im, v_s5_d, v_s5_w_glu, v_mla_g_q, v_mla_g_kv, v_mla_w_uq, v_mla_w_ukv, v_w_branch, v_w_out, v_w_ff1, v_w_ff2):
    weights = dict(g_pre_mix=g_pre_mix, g_post_mix=g_post_mix, g_pre_mlp=g_pre_mlp, g_post_mlp=g_post_mlp, w_in=w_in,
                   swa_sinks=swa_sinks, s5_lam_re=s5_lam_re, s5_lam_im=s5_lam_im, s5_log_dt=s5_log_dt, s5_b_re=s5_b_re,
                   s5_b_im=s5_b_im, s5_c_re=s5_c_re, s5_c_im=s5_c_im, s5_d=s5_d, s5_w_glu=s5_w_glu, mla_g_q=mla_g_q,
                   mla_g_kv=mla_g_kv, mla_w_uq=mla_w_uq, mla_w_ukv=mla_w_ukv, w_branch=w_branch, w_out=w_out, w_ff1=w_ff1,
                   w_ff2=w_ff2)
    m_in = dict(g_pre_mix=m_g_pre_mix, g_post_mix=m_g_post_mix, g_pre_mlp=m_g_pre_mlp, g_post_mlp=m_g_post_mlp, w_in=m_w_in,
                swa_sinks=m_swa_sinks, s5_lam_re=m_s5_lam_re, s5_lam_im=m_s5_lam_im, s5_log_dt=m_s5_log_dt,
                s5_b_re=m_s5_b_re, s5_b_im=m_s5_b_im, s5_c_re=m_s5_c_re, s5_c_im=m_s5_c_im, s5_d=m_s5_d,
                s5_w_glu=m_s5_w_glu, mla_g_q=m_mla_g_q, mla_g_kv=m_mla_g_kv, mla_w_uq=m_mla_w_uq, mla_w_ukv=m_mla_w_ukv,
                w_branch=m_w_branch, w_out=m_w_out, w_ff1=m_w_ff1, w_ff2=m_w_ff2)
    v_in = dict(g_pre_mix=v_g_pre_mix, g_post_mix=v_g_post_mix, g_pre_mlp=v_g_pre_mlp, g_post_mlp=v_g_post_mlp, w_in=v_w_in,
                swa_sinks=v_swa_sinks, s5_lam_re=v_s5_lam_re, s5_lam_im=v_s5_lam_im, s5_log_dt=v_s5_log_dt,
                s5_b_re=v_s5_b_re, s5_b_im=v_s5_b_im, s5_c_re=v_s5_c_re, s5_c_im=v_s5_c_im, s5_d=v_s5_d,
                s5_w_glu=v_s5_w_glu, mla_g_q=v_mla_g_q, mla_g_kv=v_mla_g_kv, mla_w_uq=v_mla_w_uq, mla_w_ukv=v_mla_w_ukv,
                w_branch=v_w_branch, w_out=v_w_out, w_ff1=v_w_ff1, w_ff2=v_w_ff2)
    depth = g_pre_mix.shape[0]
    t_len = x.shape[1]
    tabs = _rope_tables(t_len, 32) + _rope_tables(t_len, 16)

    def shards(l, names):
        return [weights[name][l].astype(BF16) for name in names]
    early = _run_rider("gather_weights", _gather_rider(shards(0, EARLY)))

    xs, saved, layer_w, layer_p = [x[0]], [], [], []
    for l in range(depth):
        layer_w.append(_early_weights(early))
        layer_p.append({name: weights[name][l] for name in SMALL})
        x_next, s, early = _layer_fwd(xs[-1], layer_w[l], layer_p[l], tabs, shards(l, LATE),
                                      _gather_rider(shards(l + 1, EARLY)) if l + 1 < depth else None)
        xs.append(x_next)
        saved.append(s)

    def loss_fn(y, tgt):
        err = y - tgt
        part = 0.5 * jnp.sum(jnp.mean(err * err, axis=-1, keepdims=True), axis=0, keepdims=True)
        return (err * (1.0 / D_MODEL),), (jnp.broadcast_to(part, (8, 128)),)
    dx, loss_part = _rowwise("loss", loss_fn, [xs[-1], loss_target[0]], [], [((t_len, D_MODEL), F32)], [((8, 128), F32)])
    loss = lax.psum(loss_part[0, 0], ("x", "y", "c"))

    small_grads, arrived, pending = [None] * depth, [{} for _ in range(depth)], None
    for l in reversed(range(depth)):
        dx, big, small_grads[l], got_pending, got_late = _layer_bwd(xs[l], dx, layer_w[l], layer_p[l], saved[l], tabs, pending,
                                                                    "late_grads_to_sibling")
        if pending is not None:
            arrived[l + 1].update(zip(EARLY, got_pending))
        arrived[l].update(zip(LATE, got_late))
        if l > 0:
            pending, _ = _pair_stage("early_grads_to_sibling", _grad_slabs(big, EARLY))
    small_vec = jnp.concatenate([jnp.stack([small_grads[l][name] for l in range(depth)]).reshape(-1) for name in SMALL])
    n_small = small_vec.shape[0]
    rows = -(-n_small // 1024) * 8
    small_mat = jnp.pad(small_vec, (0, rows * 128 - n_small)).reshape(rows, 128)
    sums, (sibling_small,) = _pair_stage("last_grads_to_sibling", _grad_slabs(big, EARLY), [small_mat])
    small_pair = jnp.where(lax.axis_index("c") == 0, jnp.stack([small_mat, sibling_small]), jnp.stack([sibling_small, small_mat]))
    recv = _run_rider("grads_to_chips", _chip_rider([(part, True) for part in sums] + [(small_pair, False)]))
    arrived[0].update(zip(EARLY, recv[:-1]))
    small_sum = _sum_slabs("sum_small_grads", recv[-1].reshape((N_DEV,) + small_mat.shape)).reshape(-1)

    grads, deltas, new_m, new_v = {}, {}, {}, {}
    for name in BIG:
        slabs = jnp.stack([arrived[l][name] for l in range(depth)], axis=1)
        grads[name], deltas[name], new_m[name], new_v[name] = _adamw("adamw_" + name, weights[name], slabs, m_in[name],
                                                                     v_in[name], 4)
    off = 0
    for name in SMALL:
        size = int(np.prod(weights[name].shape))
        g = small_sum[off:off + size].reshape(weights[name].shape)
        off += size
        grads[name], deltas[name], new_m[name], new_v[name] = _adamw("adamw_" + name, weights[name], g, m_in[name],
                                                                     v_in[name], False)
    return (loss, dx[None], *[grads[n] for n in WEIGHTS], *[deltas[n] for n in WEIGHTS], *[new_m[n] for n in WEIGHTS],
            *[new_v[n] for n in WEIGHTS])
```

```python
import functools
import math

import numpy as np
import jax
import jax.numpy as jnp
from jax import lax
from jax.experimental import pallas as pl
from jax.experimental.pallas import tpu as pltpu

F32, BF16 = jnp.float32, jnp.bfloat16
EPS = 1e-6
NEG_INF = -1e30
ROPE_THETA = 10000.0
D_MODEL = 1024
N_DEV = 8
BLOCK = 128
SWA_HEADS, SWA_KV = 8, 2
RET_HEADS, RET_DK, RET_DV = 4, 64, 128
MLA_HEADS = 8
S5_GROUPS, S5_STATE, S5_GROUP = 32, 64, 16
S5_CH = S5_GROUPS * S5_STATE
SCAN_LANES = 256
SCAN_SEG = 8
ADAM_LR, ADAM_B1, ADAM_B2, ADAM_EPS, ADAM_WD, ADAM_STEP = 0.001, 0.9, 0.999, 1e-08, 0.01, 10
VMEM_MB = 56

PROJ = dict(sq=(0, 512), su=(512, 512), rv=(1024, 512), rg=(1536, 512), rq=(2048, 256), rk=(2304, 256),
            cq=(2560, 256), sk=(2816, 128), sv=(2944, 128), ckv=(3072, 128), kr=(3200, 128), gates=(3328, 4096))
PROJ_W = 7424
REF_SPLIT = dict(sq=(0, 512), sk=(512, 128), sv=(640, 128), su=(768, 512), rq=(1280, 256), rk=(1536, 256),
                 rv=(1792, 512), rg=(2304, 512), cq=(2816, 256), ckv=(3072, 128), kr=(3200, 32), gates=(3232, 4096))
PROJ_ORDER = ("sq", "su", "rv", "rg", "rq", "rk", "cq", "sk", "sv", "ckv", "kr", "gates")
REF_ORDER = ("sq", "sk", "sv", "su", "rq", "rk", "rv", "rg", "cq", "ckv", "kr", "gates")

NN = (((1,), (0,)), ((), ()))
NT = (((1,), (1,)), ((), ()))
TN = (((0,), (0,)), ((), ()))


def _params(sem):
    return pltpu.CompilerParams(dimension_semantics=sem, vmem_limit_bytes=VMEM_MB * 2 ** 20)


def _raw_dot(a, b, dims):
    return lax.dot_general(a.astype(BF16), b.astype(BF16), dims, preferred_element_type=F32)


@jax.custom_vjp
def _dnn(a, b):
    return _raw_dot(a, b, NN)


@jax.custom_vjp
def _dnt(a, b):
    return _raw_dot(a, b, NT)


@jax.custom_vjp
def _dtn(a, b):
    return _raw_dot(a, b, TN)


_dnn.defvjp(lambda a, b: (_dnn(a, b), (a, b)),
            lambda r, g: (_dnt(g, r[1]).astype(r[0].dtype), _dtn(r[0], g).astype(r[1].dtype)))
_dnt.defvjp(lambda a, b: (_dnt(a, b), (a, b)),
            lambda r, g: (_dnn(g, r[1]).astype(r[0].dtype), _dtn(g, r[0]).astype(r[1].dtype)))
_dtn.defvjp(lambda a, b: (_dtn(a, b), (a, b)),
            lambda r, g: (_dnt(r[1], g).astype(r[0].dtype), _dnn(r[0], g).astype(r[1].dtype)))


class _Cols:
    def __init__(self, arr, width, index):
        self.arr, self.width, self.index = arr, width, index


def _seg(proj, name):
    off, width = PROJ[name]
    return _Cols(proj, width, off // width)


def _row_spec(shape, tile):
    lead = len(shape) - 2
    return pl.BlockSpec(tuple(shape[:-2]) + (tile, shape[-1]), lambda i, lead=lead: (0,) * lead + (i, 0))


def _const_spec(shape):
    nd = len(shape)
    return pl.BlockSpec(tuple(shape), lambda *_, nd=nd: (0,) * nd)


def _rowwise(name, fn, rows, consts, out_rows, out_accs=(), tile=512):
    arrs = [r.arr if isinstance(r, _Cols) else r for r in rows]
    t_len = arrs[0].shape[-2]
    tile = min(tile, t_len)
    n = t_len // tile
    in_specs = []
    for r in rows:
        if isinstance(r, _Cols):
            in_specs.append(pl.BlockSpec((tile, r.width), lambda i, c=r.index: (i, c)))
        else:
            in_specs.append(_row_spec(r.shape, tile))
    in_specs += [_const_spec(c.shape) for c in consts]
    out_specs = [_row_spec(s, tile) for s, _ in out_rows] + [_const_spec(s) for s, _ in out_accs]
    out_shape = [jax.ShapeDtypeStruct(s, d) for s, d in tuple(out_rows) + tuple(out_accs)]
    n_in, n_or = len(rows) + len(consts), len(out_rows)

    def body(*refs):
        res = fn(*[r[...] for r in refs[:n_in]])
        row_out, acc_out = res if out_accs else (res, ())
        for ref, val in zip(refs[n_in:n_in + n_or], row_out):
            if isinstance(val, (list, tuple)):
                for h, v_h in enumerate(val):
                    ref[h] = v_h.astype(ref.dtype)
            else:
                ref[...] = val.astype(ref.dtype)
        first = pl.program_id(0) == 0
        for ref, val in zip(refs[n_in + n_or:], acc_out):
            @pl.when(first)
            def _(ref=ref, val=val):
                ref[...] = val

            @pl.when(jnp.logical_not(first))
            def _(ref=ref, val=val):
                ref[...] += val

    return pl.pallas_call(body, name=name, grid=(n,), in_specs=in_specs, out_specs=out_specs, out_shape=out_shape,
                          compiler_params=_params(("arbitrary",)))(*arrs, *consts)


MM_VMEM_BUDGET = 44 * 2 ** 20


def _pick_tn(n, io_col_bytes, tmp_col_bytes, fixed_bytes):
    best = None
    for cand in range(128, n + 1, 128):
        if n % cand == 0 and 2 * (fixed_bytes + cand * io_col_bytes) + cand * tmp_col_bytes <= MM_VMEM_BUDGET:
            best = cand
    return best if best is not None else min(n, 128)


def _mm(name, a, b, mode, out_dtypes=(F32,), tm=512, tn=None, epi=None, epi_in=(), rider=None):
    if mode == "nn":
        (m, k), n = a.shape, b.shape[1]
    elif mode == "nt":
        (m, k), n = a.shape, b.shape[0]
    else:
        (k, m), n = a.shape, b.shape[1]
    tm = min(tm, m)
    if tn is None:
        io_col = k * b.dtype.itemsize + tm * (sum(jnp.dtype(d).itemsize for d in out_dtypes) + sum(e.dtype.itemsize for e in epi_in))
        tmp_col = tm * 4 * (2 if epi is not None else 1)
        tn = _pick_tn(n, io_col, tmp_col, tm * k * a.dtype.itemsize)
    tn = min(tn, n)
    assert m % tm == 0 and n % tn == 0, (name, m, n, tm, tn)
    a_spec = pl.BlockSpec((k, tm), lambda i, j: (0, i)) if mode == "tn" else pl.BlockSpec((tm, k), lambda i, j: (i, 0))
    b_spec = pl.BlockSpec((tn, k), lambda i, j: (j, 0)) if mode == "nt" else pl.BlockSpec((k, tn), lambda i, j: (0, j))
    dims = dict(nn=NN, nt=NT, tn=TN)[mode]
    o_spec = pl.BlockSpec((tm, tn), lambda i, j: (i, j))
    n_e = len(epi_in)

    def body(a_ref, b_ref, *refs):
        acc = _raw_dot(a_ref[...], b_ref[...], dims)
        outs = epi(acc, *[r[...] for r in refs[:n_e]]) if epi is not None else (acc,)
        for ref, val in zip(refs[n_e:], outs):
            ref[...] = val.astype(ref.dtype)

    grid = (m // tm, n // tn)
    ride_in, ride_out, ride_sems = ([], [], []) if rider is None else (rider.inputs, rider.out_shape, rider.sems)
    any_spec = pl.BlockSpec(memory_space=pl.ANY)
    res = pl.pallas_call(_ride(body, rider, (0, 2 + n_e, len(out_dtypes), 0), grid), name=name, grid=grid,
                         in_specs=[a_spec, b_spec] + [o_spec] * n_e + [any_spec] * len(ride_in),
                         out_specs=[o_spec] * len(out_dtypes) + [any_spec] * len(ride_out),
                         out_shape=[jax.ShapeDtypeStruct((m, n), d) for d in out_dtypes] + ride_out, scratch_shapes=ride_sems,
                         compiler_params=_params(("parallel", "parallel") if rider is None else ("arbitrary", "arbitrary")))(
                             a, b, *epi_in, *ride_in)
    own = res[0] if len(out_dtypes) == 1 else res[:len(out_dtypes)]
    return own if rider is None else (own, res[len(out_dtypes):])


def _rms(x, g):
    return x * lax.rsqrt(jnp.mean(x * x, axis=-1, keepdims=True) + EPS) * g


def _rms_bwd(x, g, dy):
    _, vjp = jax.vjp(_rms, x, g)
    return vjp(dy)


def _rope_tables(t_len, half):
    dim = 2 * half
    inv = 1.0 / (ROPE_THETA ** (jnp.arange(0, dim, 2, dtype=F32) / dim))
    ang = jnp.arange(t_len, dtype=F32)[:, None] * inv[None, :]
    c, s = jnp.cos(ang), jnp.sin(ang)
    reps = 128 // dim
    return jnp.tile(jnp.concatenate([c, c], 1), (1, reps)), jnp.tile(jnp.concatenate([-s, s], 1), (1, reps))


def _rope(x, cos, sin, half):
    w = x.shape[-1]
    reps = w // 128
    if reps > 1:
        cos, sin = jnp.tile(cos, (1, reps)), jnp.tile(sin, (1, reps))
    lane = lax.broadcasted_iota(jnp.int32, x.shape, 1)
    partner = jnp.where((lane % (2 * half)) < half, pltpu.roll(x, w - half, 1), pltpu.roll(x, half, 1))
    return x * cos + partner * sin


SWA_GROUP = SWA_HEADS // SWA_KV
SWA_QB = 4
SWA_TB_FWD, SWA_TB_BWD = 4, 2


def _swa_mask(tb, first_block):
    rows, keys = SWA_GROUP * tb * BLOCK, (tb + 1) * BLOCK
    qi = lax.broadcasted_iota(jnp.int32, (rows, keys), 0) % (tb * BLOCK) + BLOCK
    kj = lax.broadcasted_iota(jnp.int32, (rows, keys), 1)
    diff = qi - kj
    band = (diff >= 0) & (diff < BLOCK)
    return band if first_block is False else band & (jnp.logical_not(first_block) | (kj >= BLOCK))


def _swa_stack(ref, j, b, tb, dtype=None):
    rows = slice(BLOCK * b, BLOCK * (b + tb))
    parts = [ref[rows, 64 * (SWA_GROUP * j + g):64 * (SWA_GROUP * j + g) + 64] for g in range(SWA_GROUP)]
    out = jnp.concatenate(parts, axis=0)
    return out if dtype is None else out.astype(dtype)


def _swa_sinks(s_ref, j, tb):
    return jnp.concatenate([jnp.broadcast_to(s_ref[SWA_GROUP * j + g:SWA_GROUP * j + g + 1, 0:1], (tb * BLOCK, 1))
                            for g in range(SWA_GROUP)], axis=0)


def _swa_head(q, kw, vw, sink, mask):
    sc = _dnt(q, kw) * (64 ** -0.5)
    sc = jnp.where(mask, sc, NEG_INF)
    m = lax.stop_gradient(jnp.maximum(jnp.max(sc, axis=-1, keepdims=True), sink))
    p = jnp.exp(sc - m)
    denom = jnp.sum(p, axis=-1, keepdims=True) + jnp.exp(sink - m)
    return _dnn(p / denom, vw)


def _swa_prep(proj, cos, sin):
    def fn(sq, sk, sv, c, s):
        return _rope(sq, c, s, 32), _rope(sk, c, s, 32), sv
    t_len = proj.shape[0]
    return _rowwise("swa_prep", fn, [_seg(proj, "sq"), _seg(proj, "sk"), _seg(proj, "sv"), cos, sin], [],
                    [((t_len, 512), BF16), ((t_len, 128), BF16), ((t_len, 128), BF16)])


def _swa_specs(t_len, tb_max=1):
    qb = min(SWA_QB, t_len // BLOCK)
    tb = min(tb_max, qb)
    rows = qb * BLOCK
    cur = lambda w: pl.BlockSpec((rows, w), lambda i: (i, 0))
    prev = pl.BlockSpec((BLOCK, 128), lambda i: (jnp.maximum(qb * i - 1, 0), 0))
    return qb, tb, t_len // rows, cur, prev


def _swa_fwd(q, k, v, sinks):
    t_len = q.shape[0]
    qb, tb, steps, cur, prev = _swa_specs(t_len, SWA_TB_FWD)

    def body(q_ref, kp_ref, kc_ref, vp_ref, vc_ref, s_ref, o_ref):
        first = pl.program_id(0) == 0
        k_all = jnp.concatenate([kp_ref[...], kc_ref[...]], axis=0)
        v_all = jnp.concatenate([vp_ref[...], vc_ref[...]], axis=0)
        for j in range(SWA_KV):
            cols = slice(64 * j, 64 * j + 64)
            sink = _swa_sinks(s_ref, j, tb)
            for b in range(0, qb, tb):
                window = slice(BLOCK * b, BLOCK * (b + tb + 1))
                mask = _swa_mask(tb, first if b == 0 else False)
                o = _swa_head(_swa_stack(q_ref, j, b, tb), k_all[window, cols], v_all[window, cols], sink, mask)
                for g in range(SWA_GROUP):
                    h = SWA_GROUP * j + g
                    o_ref[BLOCK * b:BLOCK * (b + tb), 64 * h:64 * h + 64] = o[tb * BLOCK * g:tb * BLOCK * (g + 1)].astype(o_ref.dtype)

    return pl.pallas_call(
        body, name="swa_fwd", grid=(steps,),
        in_specs=[cur(512), prev, cur(128), prev, cur(128), _const_spec((8, 128))],
        out_specs=cur(512), out_shape=jax.ShapeDtypeStruct((t_len, 512), BF16),
        compiler_params=_params(("parallel",)))(q, k, k, v, v, sinks)


def _swa_bwd(q, k, v, sinks, dy):
    t_len = q.shape[0]
    qb, tb, steps, cur, prev = _swa_specs(t_len, SWA_TB_BWD)

    def body(q_ref, kp_ref, kc_ref, vp_ref, vc_ref, s_ref, dy_ref, dq_ref, dk_ref, dv_ref, dkp_ref, dvp_ref, ds_ref):
        first = pl.program_id(0) == 0

        @pl.when(first)
        def _():
            ds_ref[...] = jnp.zeros_like(ds_ref)

        k_all = jnp.concatenate([kp_ref[...], kc_ref[...]], axis=0)
        v_all = jnp.concatenate([vp_ref[...], vc_ref[...]], axis=0)
        for j in range(SWA_KV):
            cols = slice(64 * j, 64 * j + 64)
            sink = _swa_sinks(s_ref, j, tb)
            dk_parts = [jnp.zeros((BLOCK, 64), F32) for _ in range(qb + 1)]
            dv_parts = [jnp.zeros((BLOCK, 64), F32) for _ in range(qb + 1)]
            dsink = jnp.zeros((SWA_GROUP * tb * BLOCK, 1), F32)
            for b in range(0, qb, tb):
                window = slice(BLOCK * b, BLOCK * (b + tb + 1))
                mask = _swa_mask(tb, first if b == 0 else False)
                _, vjp = jax.vjp(lambda a, b_, c, d, mask=mask: _swa_head(a, b_, c, d, mask), _swa_stack(q_ref, j, b, tb),
                                 k_all[window, cols], v_all[window, cols], sink)
                dq, dkw, dvw, dsink_b = vjp(_swa_stack(dy_ref, j, b, tb, F32))
                dkw, dvw = dkw.astype(F32), dvw.astype(F32)
                for r in range(tb + 1):
                    dk_parts[b + r] = dk_parts[b + r] + dkw[BLOCK * r:BLOCK * (r + 1)]
                    dv_parts[b + r] = dv_parts[b + r] + dvw[BLOCK * r:BLOCK * (r + 1)]
                dsink = dsink + dsink_b
                for g in range(SWA_GROUP):
                    h = SWA_GROUP * j + g
                    dq_ref[BLOCK * b:BLOCK * (b + tb), 64 * h:64 * h + 64] = dq[tb * BLOCK * g:tb * BLOCK * (g + 1)].astype(F32)
            dkp_ref[:, cols], dvp_ref[:, cols] = dk_parts[0], dv_parts[0]
            for b in range(qb):
                dk_ref[BLOCK * b:BLOCK * b + BLOCK, cols] = dk_parts[b + 1]
                dv_ref[BLOCK * b:BLOCK * b + BLOCK, cols] = dv_parts[b + 1]
            for g in range(SWA_GROUP):
                h = SWA_GROUP * j + g
                ds_ref[h:h + 1, :] += jnp.broadcast_to(jnp.sum(dsink[tb * BLOCK * g:tb * BLOCK * (g + 1)], axis=0, keepdims=True), (1, 128))

    part = pl.BlockSpec((BLOCK, 128), lambda i: (i, 0))
    return pl.pallas_call(
        body, name="swa_bwd", grid=(steps,),
        in_specs=[cur(512), prev, cur(128), prev, cur(128), _const_spec((8, 128)), cur(512)],
        out_specs=[cur(512), cur(128), cur(128), part, part, _const_spec((8, 128))],
        out_shape=[jax.ShapeDtypeStruct((t_len, 512), F32), jax.ShapeDtypeStruct((t_len, 128), F32),
                   jax.ShapeDtypeStruct((t_len, 128), F32), jax.ShapeDtypeStruct((steps * BLOCK, 128), F32),
                   jax.ShapeDtypeStruct((steps * BLOCK, 128), F32), jax.ShapeDtypeStruct((8, 128), F32)],
        compiler_params=_params(("arbitrary",)))(q, k, k, v, v, sinks, dy)


def _swa_unprep(dq, dk, dv, dk_prev, dv_prev, cos, sin):
    t_len = dq.shape[0]
    qb, _, steps, cur, _ = _swa_specs(t_len)
    tail = (qb - 1) * BLOCK

    def body(dq_ref, dk_ref, dv_ref, dkn_ref, dvn_ref, c_ref, s_ref, oq_ref, ok_ref, ov_ref):
        more = (pl.program_id(0) < steps - 1).astype(F32)
        cos_t, sin_t = c_ref[...], -s_ref[...]
        oq_ref[...] = _rope(dq_ref[...], cos_t, sin_t, 32).astype(BF16)
        dk_t = jnp.concatenate([dk_ref[:tail], dk_ref[tail:] + more * dkn_ref[...]], axis=0) if tail else dk_ref[...] + more * dkn_ref[...]
        dv_t = jnp.concatenate([dv_ref[:tail], dv_ref[tail:] + more * dvn_ref[...]], axis=0) if tail else dv_ref[...] + more * dvn_ref[...]
        ok_ref[...] = _rope(dk_t, cos_t, sin_t, 32).astype(BF16)
        ov_ref[...] = dv_t.astype(BF16)

    nxt = pl.BlockSpec((BLOCK, 128), lambda i: (jnp.minimum(i + 1, steps - 1), 0))
    return pl.pallas_call(
        body, name="swa_unprep", grid=(steps,),
        in_specs=[cur(512), cur(128), cur(128), nxt, nxt, cur(128), cur(128)],
        out_specs=[cur(512), cur(128), cur(128)],
        out_shape=[jax.ShapeDtypeStruct((t_len, 512), BF16), jax.ShapeDtypeStruct((t_len, 128), BF16),
                   jax.ShapeDtypeStruct((t_len, 128), BF16)],
        compiler_params=_params(("parallel",)))(dq, dk, dv, dk_prev, dv_prev, cos, sin)


def _s5_discretize(lam_re, lam_im, log_dt, b_re, b_im):
    dt = jnp.exp(log_dt)
    mag = jnp.exp(lam_re * dt)
    ab_re, ab_im = mag * jnp.cos(lam_im * dt), mag * jnp.sin(lam_im * dt)
    den = lam_re * lam_re + lam_im * lam_im
    nr, ni = ab_re - 1.0, ab_im
    f_re = (nr * lam_re + ni * lam_im) / den
    f_im = (ni * lam_re - nr * lam_im) / den
    return ab_re, ab_im, f_re * b_re - f_im * b_im, f_re * b_im + f_im * b_re


def _s5_param_fwd(lam_re, lam_im, log_dt, b_re, b_im):
    def body(*refs):
        outs = _s5_discretize(*[r[...] for r in refs[:5]])
        for ref, val in zip(refs[5:], outs):
            ref[...] = val
    n = S5_CH
    return pl.pallas_call(body, name="s5_param_fwd",
                          out_shape=[jax.ShapeDtypeStruct((n, 1), F32)] * 2 + [jax.ShapeDtypeStruct((n, 16), F32)] * 2,
                          )(lam_re, lam_im, log_dt, b_re, b_im)


def _s5_param_bwd(lam_re, lam_im, log_dt, b_re, b_im, group_sum, d_ab_re, d_ab_im, d_bb_re, d_bb_im):
    def body(*refs):
        ins = [r[...] for r in refs[:10]]
        _, vjp = jax.vjp(_s5_discretize, *ins[:5])
        d_lr, d_li, d_ldt, d_br, d_bi = vjp(tuple(ins[6:10]))
        o_lr, o_li, o_ldt, o_br, o_bi = refs[10:]
        o_lr[...], o_li[...], o_br[...], o_bi[...] = d_lr, d_li, d_br, d_bi
        o_ldt[...] = lax.dot_general(ins[5], jnp.broadcast_to(d_ldt, (S5_CH, 128)), NN,
                                     precision=lax.Precision.HIGHEST, preferred_element_type=F32)
    n = S5_CH
    return pl.pallas_call(body, name="s5_param_bwd",
                          out_shape=[jax.ShapeDtypeStruct((n, 1), F32), jax.ShapeDtypeStruct((n, 1), F32),
                                     jax.ShapeDtypeStruct((S5_GROUPS, 128), F32),
                                     jax.ShapeDtypeStruct((n, 16), F32), jax.ShapeDtypeStruct((n, 16), F32)],
                          )(lam_re, lam_im, log_dt, b_re, b_im, group_sum, d_ab_re, d_ab_im, d_bb_re, d_bb_im)


def _s5_expand(name, u, first_band, w):
    t_len = u.shape[0]
    tile = min(512, t_len)
    seg = t_len // SCAN_SEG

    def body(u_ref, wr_ref, wi_ref, o_ref, up_ref, rows):
        base = pl.program_id(1) * (tile // SCAN_SEG)
        for r in range(tile // SCAN_SEG):
            rows[SCAN_SEG * r:SCAN_SEG * (r + 1), :] = u_ref[pl.ds(base + r, SCAN_SEG, stride=seg), :]
        u_t = rows[...].astype(BF16)
        up_ref[...] = u_t
        o_ref[0] = _raw_dot(u_t, wr_ref[...], NN)
        o_ref[1] = _raw_dot(u_t, wi_ref[...], NN)

    return pl.pallas_call(
        body, name=name, grid=(4, t_len // tile),
        in_specs=[pl.BlockSpec((t_len, 128), lambda b, i: (0, first_band + b)), pl.BlockSpec((None, 128, 512), lambda b, i: (b, 0, 0)),
                  pl.BlockSpec((None, 128, 512), lambda b, i: (4 + b, 0, 0))],
        out_specs=[pl.BlockSpec((2, tile, 512), lambda b, i: (0, i, b)), pl.BlockSpec((tile, 128), lambda b, i: (i, b))],
        out_shape=[jax.ShapeDtypeStruct((2, t_len, S5_CH), F32), jax.ShapeDtypeStruct((t_len, 512), BF16)],
        scratch_shapes=[pltpu.VMEM((tile, 128), F32)], compiler_params=_params(("parallel", "arbitrary")))(u, w, w)


def _s5_contract(name, x, w):
    t_len = x.shape[1]
    tile = min(512, t_len)
    seg = t_len // SCAN_SEG

    def body(x_ref, wr_ref, wi_ref, o_ref):
        y = _raw_dot(x_ref[0], wr_ref[...], NN) + _raw_dot(x_ref[1], wi_ref[...], NN)
        base = pl.program_id(1) * (tile // SCAN_SEG)
        for r in range(tile // SCAN_SEG):
            o_ref[pl.ds(base + r, SCAN_SEG, stride=seg), :] = y[SCAN_SEG * r:SCAN_SEG * (r + 1)]

    return pl.pallas_call(
        body, name=name, grid=(4, t_len // tile),
        in_specs=[pl.BlockSpec((2, tile, 512), lambda b, i: (0, i, b)), pl.BlockSpec((None, 512, 128), lambda b, i: (b, 0, 0)),
                  pl.BlockSpec((None, 512, 128), lambda b, i: (4 + b, 0, 0))],
        out_specs=pl.BlockSpec((t_len, 128), lambda b, i: (0, b)),
        out_shape=jax.ShapeDtypeStruct((t_len, 512), F32), compiler_params=_params(("parallel", "arbitrary")))(x, w, w)


def _s5_band_grad(name, narrow, wide, narrow_first):
    t_len = narrow.shape[0]

    def body(n_ref, w_ref, o_ref):
        if narrow_first:
            o_ref[...] = _raw_dot(n_ref[...], w_ref[...], TN)
        else:
            o_ref[...] = _raw_dot(w_ref[...], n_ref[...], TN)

    shape = (8, 128, 512) if narrow_first else (8, 512, 128)
    return pl.pallas_call(
        body, name=name, grid=(8,),
        in_specs=[pl.BlockSpec((t_len, 128), lambda bb: (0, bb % 4)), pl.BlockSpec((None, t_len, 512), lambda bb: (bb // 4, 0, bb % 4))],
        out_specs=pl.BlockSpec((None,) + shape[1:], lambda bb: (bb, 0, 0)),
        out_shape=jax.ShapeDtypeStruct(shape, F32), compiler_params=_params(("parallel",)))(narrow, wide)


def _s5_scan(name, bu, a_re, a_im, states=None, carry_in=None):
    t_len = bu.shape[1]
    n = t_len // SCAN_SEG
    log_n = int(round(math.log2(n)))
    assert 2 ** log_n == n
    reverse = states is not None
    lanes = SCAN_LANES
    nblk = S5_CH // lanes

    def body(*refs):
        if reverse:
            b_ref, ar_ref, ai_ref, x_ref, cin_ref, o_ref, dar_ref, dai_ref, c_scr = refs
            xr_ref, xi_ref, cr_ref, ci_ref = x_ref.at[0], x_ref.at[1], cin_ref.at[0], cin_ref.at[1]
        else:
            b_ref, ar_ref, ai_ref, o_ref, co_ref, c_scr = refs
            cor_ref, coi_ref = co_ref.at[0], co_ref.at[1]
        br_ref, bi_ref, or_ref, oi_ref = b_ref.at[0], b_ref.at[1], o_ref.at[0], o_ref.at[1]
        ar = ar_ref[...]
        ai = -ai_ref[...] if reverse else ai_ref[...]

        def tile_at(i):
            return pl.multiple_of((n - 1 - i if reverse else i) * SCAN_SEG, SCAN_SEG)

        def advance(s, row):
            sr, si = s
            return (ar * sr - ai * si + br_ref[pl.ds(row, SCAN_SEG), :], ar * si + ai * sr + bi_ref[pl.ds(row, SCAN_SEG), :])

        zero = jnp.zeros((SCAN_SEG, lanes), F32)
        fin_r, fin_i = lax.fori_loop(0, n, lambda i, s: advance(s, tile_at(i)), (zero, zero), unroll=8)
        pr, pi = ar[0:1], ai[0:1]
        for _ in range(log_n):
            pr, pi = pr * pr - pi * pi, 2.0 * pr * pi
        er = ei = jnp.zeros((1, lanes), F32)
        for step in range(SCAN_SEG):
            j = SCAN_SEG - 1 - step if reverse else step
            c_scr[0, j:j + 1, :] = er
            c_scr[1, j:j + 1, :] = ei
            er, ei = pr * er - pi * ei + fin_r[j:j + 1], pr * ei + pi * er + fin_i[j:j + 1]
        entry = (c_scr[0], c_scr[1])

        if not reverse:
            cor_ref[...], coi_ref[...] = entry

            def step2(i, s):
                row = tile_at(i)
                s = advance(s, row)
                or_ref[pl.ds(row, SCAN_SEG), :] = s[0]
                oi_ref[pl.ds(row, SCAN_SEG), :] = s[1]
                return s
            lax.fori_loop(0, n, step2, entry, unroll=8)
        else:
            def emit(s, row, xr_prev, xi_prev, acc):
                s = advance(s, row)
                or_ref[pl.ds(row, SCAN_SEG), :] = s[0]
                oi_ref[pl.ds(row, SCAN_SEG), :] = s[1]
                return s, (acc[0] + s[0] * xr_prev + s[1] * xi_prev, acc[1] + s[1] * xr_prev - s[0] * xi_prev)

            def step2(i, carry):
                s, acc = carry
                row = tile_at(i)
                prev = pl.multiple_of(row - SCAN_SEG, SCAN_SEG)
                return emit(s, row, xr_ref[pl.ds(prev, SCAN_SEG), :], xi_ref[pl.ds(prev, SCAN_SEG), :], acc)
            s, acc = lax.fori_loop(0, n - 1, step2, (entry, (zero, zero)), unroll=8)
            s, acc = emit(s, 0, cr_ref[...], ci_ref[...], acc)
            dar_ref[...] = jnp.sum(acc[0], axis=0, keepdims=True)
            dai_ref[...] = jnp.sum(acc[1], axis=0, keepdims=True)

    planes = pl.BlockSpec((2, t_len, lanes), lambda b: (0, 0, b))
    tile8 = pl.BlockSpec((SCAN_SEG, lanes), lambda b: (0, b))
    entry_spec = pl.BlockSpec((2, SCAN_SEG, lanes), lambda b: (0, 0, b))
    row1 = pl.BlockSpec((1, lanes), lambda b: (0, b))
    seq = jax.ShapeDtypeStruct((2, t_len, S5_CH), F32)
    if reverse:
        in_specs = [planes, tile8, tile8, planes, entry_spec]
        args = (bu, a_re, a_im, states, carry_in)
        out_specs = [planes, row1, row1]
        out_shape = [seq, jax.ShapeDtypeStruct((1, S5_CH), F32), jax.ShapeDtypeStruct((1, S5_CH), F32)]
    else:
        in_specs = [planes, tile8, tile8]
        args = (bu, a_re, a_im)
        out_specs = [planes, entry_spec]
        out_shape = [seq, jax.ShapeDtypeStruct((2, SCAN_SEG, S5_CH), F32)]
    return pl.pallas_call(body, name=name, grid=(nblk,), in_specs=in_specs, out_specs=out_specs, out_shape=out_shape,
                          scratch_shapes=[pltpu.VMEM((2, SCAN_SEG, lanes), F32)],
                          compiler_params=_params(("parallel",)))(*args)


def _ret_consts():
    h = np.arange(RET_HEADS, dtype=np.float32)
    log_gamma = np.log1p(-np.exp2(-5.0 - h)).astype(np.float32)
    idx = np.arange(BLOCK, dtype=np.float32)
    diff = idx[:, None] - idx[None, :]
    decay = np.where(diff >= 0, np.exp(log_gamma[:, None, None] * np.maximum(diff, 0.0)), 0.0).astype(np.float32)
    k_w = np.exp(log_gamma[:, None] * (BLOCK - 1 - idx)[None, :]).astype(np.float32)[:, :, None]
    q_w = np.exp(log_gamma[:, None] * (idx + 1.0)[None, :]).astype(np.float32)[:, :, None]
    chunk_decay = [float(v) for v in np.exp(log_gamma * BLOCK).astype(np.float32)]
    return jnp.asarray(decay), jnp.asarray(q_w), jnp.asarray(k_w), chunk_decay


def _ret_head(q, k, v, g, prev, decay, q_w, k_w, chunk_decay):
    inner_s = _dnt(q, k) * decay
    y = _dnn(inner_s, v) + _dnn(q * q_w, prev)
    mu = jnp.mean(y, axis=-1, keepdims=True)
    var = jnp.mean(jnp.square(y - mu), axis=-1, keepdims=True)
    yn = (y - mu) * lax.rsqrt(var + EPS)
    out = g * jax.nn.sigmoid(g) * yn
    return out, prev * chunk_decay + _dtn(k * k_w, v)


def _ret_prep(proj, cos, sin):
    def fn(rq, rk, c, s):
        return _rope(rq, c, s, 32), _rope(rk, c, s, 32) * (RET_DK ** -0.5)
    t_len = proj.shape[0]
    return _rowwise("ret_prep", fn, [_seg(proj, "rq"), _seg(proj, "rk"), cos, sin], [],
                    [((t_len, 256), F32), ((t_len, 256), F32)])


def _ret_unprep(dq, dk, cos, sin):
    def fn(dq_t, dk_t, c, s):
        return _rope(dq_t, c, -s, 32), _rope(dk_t * (RET_DK ** -0.5), c, -s, 32)
    t_len = dq.shape[0]
    return _rowwise("ret_unprep", fn, [dq, dk, cos, sin], [], [((t_len, 256), BF16), ((t_len, 256), BF16)])


def _ret_fwd(q, k, proj):
    t_len = q.shape[0]
    nc = t_len // BLOCK
    decay, q_w, k_w, chunk_decay = _ret_consts()

    def body(q_ref, k_ref, v_ref, g_ref, dec_ref, qw_ref, kw_ref, o_ref, prev_ref, state):
        @pl.when(pl.program_id(0) == 0)
        def _():
            state[...] = jnp.zeros_like(state)
        for h in range(RET_HEADS):
            qk, vv = slice(64 * h, 64 * h + 64), slice(128 * h, 128 * h + 128)
            prev = state[h]
            prev_ref[h] = prev
            out, nxt = _ret_head(q_ref[:, qk], k_ref[:, qk], v_ref[:, vv], g_ref[:, vv], prev,
                                 dec_ref[h], qw_ref[h], kw_ref[h], chunk_decay[h])
            o_ref[:, vv] = out.astype(o_ref.dtype)
            state[h] = nxt

    chunk = lambda w, c=0: pl.BlockSpec((BLOCK, w), lambda i, c=c: (i, c))
    return pl.pallas_call(
        body, name="ret_fwd", grid=(nc,),
        in_specs=[chunk(256), chunk(256), chunk(512, PROJ["rv"][0] // 512), chunk(512, PROJ["rg"][0] // 512),
                  _const_spec(decay.shape), _const_spec(q_w.shape), _const_spec(k_w.shape)],
        out_specs=[chunk(512), pl.BlockSpec((None, RET_HEADS, RET_DK, RET_DV), lambda i: (i, 0, 0, 0))],
        out_shape=[jax.ShapeDtypeStruct((t_len, 512), BF16), jax.ShapeDtypeStruct((nc, RET_HEADS, RET_DK, RET_DV), F32)],
        scratch_shapes=[pltpu.VMEM((RET_HEADS, RET_DK, RET_DV), F32)],
        compiler_params=_params(("arbitrary",)))(q, k, proj, proj, decay, q_w, k_w)


def _ret_bwd(q, k, proj, prevs, dy):
    t_len = q.shape[0]
    nc = t_len // BLOCK
    decay, q_w, k_w, chunk_decay = _ret_consts()

    def body(q_ref, k_ref, v_ref, g_ref, prev_ref, dy_ref, dec_ref, qw_ref, kw_ref, dq_ref, dk_ref, dv_ref, dg_ref, dstate):
        @pl.when(pl.program_id(0) == 0)
        def _():
            dstate[...] = jnp.zeros_like(dstate)
        for h in range(RET_HEADS):
            qk, vv = slice(64 * h, 64 * h + 64), slice(128 * h, 128 * h + 128)
            head = functools.partial(_ret_head, decay=dec_ref[h], q_w=qw_ref[h], k_w=kw_ref[h], chunk_decay=chunk_decay[h])
            _, vjp = jax.vjp(head, q_ref[:, qk], k_ref[:, qk], v_ref[:, vv], g_ref[:, vv], prev_ref[h])
            dq, dk, dv, dg, dprev = vjp((dy_ref[:, vv].astype(F32), dstate[h]))
            dq_ref[:, qk], dk_ref[:, qk] = dq, dk
            dv_ref[:, vv], dg_ref[:, vv] = dv.astype(dv_ref.dtype), dg.astype(dg_ref.dtype)
            dstate[h] = dprev

    chunk = lambda w, c=0: pl.BlockSpec((BLOCK, w), lambda i, c=c: (nc - 1 - i, c))
    return pl.pallas_call(
        body, name="ret_bwd", grid=(nc,),
        in_specs=[chunk(256), chunk(256), chunk(512, PROJ["rv"][0] // 512), chunk(512, PROJ["rg"][0] // 512),
                  pl.BlockSpec((None, RET_HEADS, RET_DK, RET_DV), lambda i: (nc - 1 - i, 0, 0, 0)), chunk(512),
                  _const_spec(decay.shape), _const_spec(q_w.shape), _const_spec(k_w.shape)],
        out_specs=[chunk(256), chunk(256), chunk(512), chunk(512)],
        out_shape=[jax.ShapeDtypeStruct((t_len, 256), F32), jax.ShapeDtypeStruct((t_len, 256), F32),
                   jax.ShapeDtypeStruct((t_len, 512), BF16), jax.ShapeDtypeStruct((t_len, 512), BF16)],
        scratch_shapes=[pltpu.VMEM((RET_HEADS, RET_DK, RET_DV), F32)],
        compiler_params=_params(("arbitrary",)))(q, k, proj, proj, prevs, dy, decay, q_w, k_w)


MLA_SCALE = 96 ** -0.5


def _mla_prep(qf, kvf, proj, cos, sin):
    def fn(q, kv, kr, c, s):
        q_rope = _rope(q[:, 512:768], c, s, 16)
        k_rope = _rope(kr, c, s, 16)[:, :32]
        zero = jnp.zeros_like(k_rope)
        qs = [jnp.concatenate([q[:, 64 * h:64 * h + 64], q_rope[:, 32 * h:32 * h + 32], zero], axis=1) for h in range(MLA_HEADS)]
        ks = [jnp.concatenate([kv[:, 64 * h:64 * h + 64], k_rope, zero], axis=1) for h in range(MLA_HEADS)]
        ones = (lax.broadcasted_iota(jnp.int32, (kv.shape[0], 64), 1) == 0).astype(F32)
        vs = [jnp.concatenate([kv[:, 512 + 64 * h:576 + 64 * h], ones], axis=1) for h in range(MLA_HEADS)]
        return qs, ks, vs
    t_len = qf.shape[0]
    return _rowwise("mla_prep", fn, [qf, kvf, _seg(proj, "kr"), cos, sin], [],
                    [((8, t_len, 128), BF16), ((8, t_len, 128), BF16), ((8, t_len, 128), BF16)])


def _mla_unprep(dq, dk, dv, cos, sin):
    def fn(dq_t, dk_t, dv_t, c, s):
        q_rope = _rope(jnp.concatenate([dq_t[h][:, 64:96] for h in range(MLA_HEADS)], axis=1), c, -s, 16)
        d_qf = jnp.concatenate([dq_t[h][:, :64] for h in range(MLA_HEADS)] + [q_rope], axis=1)
        d_kvf = jnp.concatenate([dk_t[h][:, :64] for h in range(MLA_HEADS)] + [dv_t[h][:, :64] for h in range(MLA_HEADS)], axis=1)
        k_rope = dk_t[0][:, 64:96]
        for h in range(1, MLA_HEADS):
            k_rope = k_rope + dk_t[h][:, 64:96]
        d_kr = _rope(jnp.concatenate([k_rope, jnp.zeros((k_rope.shape[0], 96), F32)], axis=1), c, -s, 16)
        return d_qf, d_kvf, d_kr
    t_len = dq.shape[1]
    return _rowwise("mla_unprep", fn, [dq, dk, dv, cos, sin], [],
                    [((t_len, 768), BF16), ((t_len, 1024), BF16), ((t_len, 128), BF16)])


MLA_EXP2 = MLA_SCALE * math.log2(math.e)
MLA_TILE = 1024


def _diag_mask(tq):
    return lax.broadcasted_iota(jnp.int32, (tq, tq), 1) <= lax.broadcasted_iota(jnp.int32, (tq, tq), 0)


def _tri_steps(nq, key_major):
    if key_major:
        pairs = [(i, j) for j in range(nq) for i in range(j, nq)]
    else:
        pairs = [(i, j) for i in range(nq) for j in range(i + 1)]
    return jnp.asarray([p[0] for p in pairs], jnp.int32), jnp.asarray([p[1] for p in pairs], jnp.int32)


def _ride(body, rider, counts, grid):
    if rider is None:
        return body
    extra = (0, len(rider.inputs), len(rider.out_shape), len(rider.sems))

    def wrapped(*refs):
        host, ridden, pos = [], [], 0
        for mine, theirs in zip(counts, extra):
            host += refs[pos:pos + mine]
            ridden.append(refs[pos + mine:pos + mine + theirs])
            pos += mine + theirs
        ids = [pl.program_id(a) for a in range(len(grid))]
        first, last = ids[0] == 0, ids[0] == grid[0] - 1
        for a in range(1, len(grid)):
            first, last = first & (ids[a] == 0), last & (ids[a] == grid[a] - 1)

        @pl.when(first)
        def _():
            rider.start(*ridden[1:])

        body(*host)

        @pl.when(last)
        def _():
            rider.finish(*ridden[1:])

    return wrapped


def _mla_fwd(q, k, v, rider=None):
    t_len = q.shape[1]
    tq = min(MLA_TILE, t_len)
    nq = t_len // tq
    qi, kj = _tri_steps(nq, False)

    def body(i_ref, j_ref, q_ref, k_ref, v_ref, o_ref, lse_ref, m_s, acc_s):
        step = pl.program_id(1)
        i, j = i_ref[step], j_ref[step]

        @pl.when(j == 0)
        def _():
            m_s[...] = jnp.full_like(m_s, NEG_INF)
            acc_s[...] = jnp.zeros_like(acc_s)

        def update(mask):
            for e in range(2):
                s = _raw_dot(q_ref[e], k_ref[e], NT)
                if mask is not None:
                    s = jnp.where(mask, s, NEG_INF)
                m_old = m_s[e]
                m_new = jnp.maximum(m_old, jnp.max(s, axis=-1, keepdims=True))
                p = jnp.exp2((s - m_new) * MLA_EXP2)
                acc_s[e] = jnp.exp2((m_old - m_new) * MLA_EXP2) * acc_s[e] + _raw_dot(p, v_ref[e], NN)
                m_s[e] = m_new

        @pl.when(j < i)
        def _():
            update(None)

        @pl.when(j == i)
        def _():
            update(_diag_mask(tq))
            outs = []
            for e in range(2):
                acc = acc_s[e]
                denom = acc[:, 64:65]
                outs.append(acc[:, :64] / denom)
                lse_ref[e] = m_s[e] * MLA_EXP2 + jnp.log2(denom)
            o_ref[...] = jnp.concatenate(outs, axis=1)

    grid = (MLA_HEADS // 2, int(qi.shape[0]))
    ride_in, ride_out, ride_sems = ([], [], []) if rider is None else (rider.inputs, rider.out_shape, rider.sems)
    grid_spec = pltpu.PrefetchScalarGridSpec(
        num_scalar_prefetch=2, grid=grid,
        in_specs=[pl.BlockSpec((2, tq, 128), lambda hp, s, qi, kj: (hp, qi[s], 0)),
                  pl.BlockSpec((2, tq, 128), lambda hp, s, qi, kj: (hp, kj[s], 0)),
                  pl.BlockSpec((2, tq, 128), lambda hp, s, qi, kj: (hp, kj[s], 0))] + [ANY_SPEC] * len(ride_in),
        out_specs=[pl.BlockSpec((tq, 128), lambda hp, s, qi, kj: (qi[s], hp)),
                   pl.BlockSpec((2, tq, 1), lambda hp, s, qi, kj: (hp, qi[s], 0))] + [ANY_SPEC] * len(ride_out),
        scratch_shapes=[pltpu.VMEM((2, tq, 1), F32), pltpu.VMEM((2, tq, 128), F32)] + ride_sems)
    res = pl.pallas_call(
        _ride(body, rider, (2, 3, 2, 2), grid), name="mla_fwd" if rider is None else "mla_fwd_gather", grid_spec=grid_spec,
        out_shape=[jax.ShapeDtypeStruct((t_len, 512), F32), jax.ShapeDtypeStruct((MLA_HEADS, t_len, 1), F32)] + ride_out,
        compiler_params=_params(("parallel" if rider is None else "arbitrary", "arbitrary")))(qi, kj, q, k, v, *ride_in)
    return res[0], res[1], res[2:]


def _mla_bwd_prep(o, dy):
    def fn(o_t, dy_t):
        dos, deltas = [], []
        for h in range(MLA_HEADS):
            d_h = dy_t[:, 64 * h:64 * h + 64].astype(F32)
            dos.append(jnp.concatenate([d_h, jnp.zeros_like(d_h)], axis=1))
            deltas.append(jnp.sum(d_h * o_t[:, 64 * h:64 * h + 64], axis=-1, keepdims=True))
        return dos, deltas
    t_len = o.shape[0]
    return _rowwise("mla_bwd_prep", fn, [o, dy], [], [((8, t_len, 128), BF16), ((8, t_len, 1), F32)])


def _mla_bwd(q, k, v, do, lse2, delta, rider=None):
    t_len = q.shape[1]
    tq = min(MLA_TILE, t_len)
    nq = t_len // tq
    qi, kj = _tri_steps(nq, True)

    def body(i_ref, j_ref, q_ref, k_ref, v_ref, do_ref, lse_ref, dl_ref, dq_ref, dk_ref, dv_ref, k_acc, v_acc):
        step = pl.program_id(1)
        i, j = i_ref[step], j_ref[step]

        @pl.when(step == 0)
        def _():
            dq_ref[...] = jnp.zeros_like(dq_ref)

        @pl.when(i == j)
        def _():
            k_acc[...] = jnp.zeros_like(k_acc)
            v_acc[...] = jnp.zeros_like(v_acc)

        def tile(mask):
            p = jnp.exp2(_raw_dot(q_ref[...], k_ref[...], NT) * MLA_EXP2 - lse_ref[...])
            if mask is not None:
                p = jnp.where(mask, p, 0.0)
            ds = p * (_raw_dot(do_ref[...], v_ref[...], NT) - dl_ref[...]) * MLA_SCALE
            v_acc[...] += _raw_dot(p, do_ref[...], TN)
            k_acc[...] += _raw_dot(ds, q_ref[...], TN)
            rows = pl.ds(pl.multiple_of(i * tq, tq), tq)
            dq_ref[rows, :] += _raw_dot(ds, k_ref[...], NN)

        @pl.when(i == j)
        def _():
            tile(_diag_mask(tq))

        @pl.when(i > j)
        def _():
            tile(None)

        @pl.when(i == nq - 1)
        def _():
            dk_ref[...] = k_acc[...]
            dv_ref[...] = v_acc[...]

    q_blk = lambda w: pl.BlockSpec((None, tq, w), lambda h, s, qi, kj: (h, qi[s], 0))
    k_blk = lambda w: pl.BlockSpec((None, tq, w), lambda h, s, qi, kj: (h, kj[s], 0))
    grid = (MLA_HEADS, int(qi.shape[0]))
    ride_in, ride_out, ride_sems = ([], [], []) if rider is None else (rider.inputs, rider.out_shape, rider.sems)
    grid_spec = pltpu.PrefetchScalarGridSpec(
        num_scalar_prefetch=2, grid=grid,
        in_specs=[q_blk(128), k_blk(128), k_blk(128), q_blk(128), q_blk(1), q_blk(1)] + [ANY_SPEC] * len(ride_in),
        out_specs=[pl.BlockSpec((None, t_len, 128), lambda h, s, qi, kj: (h, 0, 0)), k_blk(128), k_blk(128)]
        + [ANY_SPEC] * len(ride_out),
        scratch_shapes=[pltpu.VMEM((tq, 128), F32), pltpu.VMEM((tq, 128), F32)] + ride_sems)
    res = pl.pallas_call(
        _ride(body, rider, (2, 6, 3, 2), grid), name="mla_bwd" if rider is None else "mla_bwd_scatter", grid_spec=grid_spec,
        out_shape=[jax.ShapeDtypeStruct((MLA_HEADS, t_len, 128), F32)] * 3 + ride_out,
        compiler_params=_params(("parallel" if rider is None else "arbitrary", "arbitrary")))(qi, kj, q, k, v, do, lse2, delta, *ride_in)
    return res[0], res[1], res[2], res[3:]


MERGE_TN = 256
GATE_BLOCK0 = PROJ["gates"][0] // MERGE_TN


def _merge_specs(tm):
    y_spec = pl.BlockSpec((tm, 512), lambda i, j: (i, 0))
    w_spec = pl.BlockSpec((512, MERGE_TN), lambda i, j: (0, j))
    gate = lambda b: pl.BlockSpec((tm, MERGE_TN), lambda i, j, b=b: (i, GATE_BLOCK0 + 4 * b + j))
    return [y_spec] * 4 + [w_spec] * 4 + [gate(b) for b in range(4)]


def _merge_fwd(ys, wbs, proj):
    t_len = proj.shape[0]
    tm = min(512, t_len)

    def body(*refs):
        y, w, gl, o_ref = refs[0:4], refs[4:8], refs[8:12], refs[12]
        acc = jnp.zeros((tm, MERGE_TN), F32)
        for b in range(4):
            acc += jax.nn.sigmoid(gl[b][...]) * _raw_dot(y[b][...], w[b][...], NN)
        o_ref[...] = acc.astype(o_ref.dtype)

    return pl.pallas_call(body, name="merge_fwd", grid=(t_len // tm, D_MODEL // MERGE_TN), in_specs=_merge_specs(tm),
                          out_specs=pl.BlockSpec((tm, MERGE_TN), lambda i, j: (i, j)),
                          out_shape=jax.ShapeDtypeStruct((t_len, D_MODEL), BF16),
                          compiler_params=_params(("parallel", "parallel")))(*ys, *wbs, *([proj] * 4))


def _merge_bwd(ys, wbs, proj, dmerged):
    t_len = proj.shape[0]
    tm = min(512, t_len)

    def body(*refs):
        y, w, gl, dm_ref, dgl, dt = refs[0:4], refs[4:8], refs[8:12], refs[12], refs[13:17], refs[17:21]
        dm = dm_ref[...]
        for b in range(4):
            gate = jax.nn.sigmoid(gl[b][...])
            t_b = _raw_dot(y[b][...], w[b][...], NN)
            dgl[b][...] = (dm * t_b * gate * (1.0 - gate)).astype(BF16)
            dt[b][...] = (dm * gate).astype(BF16)

    o_spec = pl.BlockSpec((tm, MERGE_TN), lambda i, j: (i, j))
    return pl.pallas_call(body, name="merge_bwd", grid=(t_len // tm, D_MODEL // MERGE_TN), in_specs=_merge_specs(tm) + [o_spec],
                          out_specs=[o_spec] * 8, out_shape=[jax.ShapeDtypeStruct((t_len, D_MODEL), BF16)] * 8,
                          compiler_params=_params(("parallel", "parallel")))(*ys, *wbs, *([proj] * 4), dmerged)


def _adamw(name, w, g, m, v, slabs):
    shape = w.shape
    size = int(np.prod(shape))
    if shape[-1] >= 96:
        view = (size // shape[-1], shape[-1])
    elif size % 128 == 0:
        view = (size // 128, 128)
    else:
        view = (1, size)
    rows = view[0]
    tile = rows
    for cand in (512, 256, 128, 64):
        if rows > cand and rows % cand == 0 and cand * view[1] * 4 <= 2 ** 20:
            tile = cand
            break
    if rows * view[1] * 4 <= 2 ** 20:
        tile = rows
    c1, c2 = 1.0 - ADAM_B1 ** ADAM_STEP, 1.0 - ADAM_B2 ** ADAM_STEP

    def body(w_ref, g_ref, m_ref, v_ref, og_ref, od_ref, om_ref, ov_ref):
        if slabs:
            grad = g_ref[0].astype(F32)
            for d in range(1, slabs):
                grad = grad + g_ref[d].astype(F32)
        else:
            grad = g_ref[...]
        m_new = ADAM_B1 * m_ref[...] + (1.0 - ADAM_B1) * grad
        v_new = ADAM_B2 * v_ref[...] + (1.0 - ADAM_B2) * jnp.square(grad)
        og_ref[...] = grad
        od_ref[...] = -ADAM_LR * ((m_new / c1) / (jnp.sqrt(v_new / c2) + ADAM_EPS) + ADAM_WD * w_ref[...])
        om_ref[...] = m_new
        ov_ref[...] = v_new

    spec = pl.BlockSpec((tile, view[1]), lambda i: (i, 0))
    g_spec = pl.BlockSpec((slabs, tile, view[1]), lambda i: (0, i, 0)) if slabs else spec
    g_view = g.reshape((slabs,) + view) if slabs else g.reshape(view)
    outs = pl.pallas_call(body, name=name, grid=(rows // tile,), in_specs=[spec, g_spec, spec, spec], out_specs=[spec] * 4,
                          out_shape=[jax.ShapeDtypeStruct(view, F32)] * 4,
                          compiler_params=_params(("parallel",)))(w.reshape(view), g_view, m.reshape(view), v.reshape(view))
    return [o.reshape(shape) for o in outs]


def _sum_slabs(name, x):
    def body(x_ref, o_ref):
        acc = x_ref[0]
        for d in range(1, N_DEV):
            acc = acc + x_ref[d]
        o_ref[...] = acc
    return pl.pallas_call(body, name=name, out_shape=jax.ShapeDtypeStruct(x.shape[1:], x.dtype))(x)


MESH_ID = pl.DeviceIdType.MESH
ANY_SPEC = pl.BlockSpec(memory_space=pl.ANY)


def _remote(src, dst, send_sems, recv_sems, k, to):
    return pltpu.make_async_remote_copy(src_ref=src, dst_ref=dst, send_sem=send_sems.at[k], recv_sem=recv_sems.at[k],
                                        device_id=to, device_id_type=MESH_ID)


class _Rider:
    def __init__(self, inputs, out_shape, sems, start, finish):
        self.inputs, self.out_shape, self.sems, self.start, self.finish = list(inputs), list(out_shape), list(sems), start, finish


def _run_rider(name, rider):
    n_in, n_out = len(rider.inputs), len(rider.out_shape)

    def body(*refs):
        parts = refs[:n_in], refs[n_in:n_in + n_out], refs[n_in + n_out:]
        rider.start(*parts)
        rider.finish(*parts)

    return pl.pallas_call(body, name=name, in_specs=[ANY_SPEC] * n_in, out_specs=[ANY_SPEC] * n_out, out_shape=rider.out_shape,
                          scratch_shapes=rider.sems, compiler_params=pltpu.CompilerParams(has_side_effects=True))(*rider.inputs)


def _gather_rider(blocks):
    n = len(blocks)

    def plan(srcs, dsts, sems, waits):
        send_sems, recv_sems, local_sems = sems
        x, y, c = lax.axis_index("x"), lax.axis_index("y"), lax.axis_index("c")
        sibling = (x, y, 1 - c)
        chips = [(1 - x, y), (x, 1 - y), (1 - x, 1 - y)]
        index = lambda px, py, pc: 4 * px + 2 * py + pc

        def copy(t, k, block, to, src=None):
            slot = dsts[t].at[block]
            return _remote(slot if src is None else src, slot, send_sems, recv_sems, 7 * t + k, to)

        me = index(x, y, c)
        own = [pltpu.make_async_copy(srcs[t], dsts[t].at[me], local_sems.at[t]) for t in range(n)]
        first = []
        for t in range(n):
            first.append(copy(t, 0, me, sibling, src=srcs[t]))
            first += [copy(t, 1 + j, me, (*chip, c), src=srcs[t]) for j, chip in enumerate(chips)]
        if not waits:
            return own, first, [], []
        relays = [(copy(t, 1 + j, index(*chip, c), sibling), copy(t, 4 + j, index(*chip, c), sibling))
                  for j, chip in enumerate(chips) for t in range(n)]
        from_sibling = [copy(t, 0, index(x, y, 1 - c), sibling) for t in range(n)]
        from_sibling += [copy(t, 4 + j, index(*chip, 1 - c), sibling) for t in range(n) for j, chip in enumerate(chips)]
        return own, first, relays, from_sibling

    def start(srcs, dsts, sems):
        own, first, _, _ = plan(srcs, dsts, sems, False)
        for cp in own + first:
            cp.start()

    def finish(srcs, dsts, sems):
        own, first, relays, from_sibling = plan(srcs, dsts, sems, True)
        for arrival, forward in relays:
            arrival.wait_recv()
            forward.start()
        for cp in from_sibling:
            cp.wait_recv()
        for cp in first + [forward for _, forward in relays]:
            cp.wait_send()
        for cp in own:
            cp.wait()

    return _Rider(blocks, [jax.ShapeDtypeStruct((N_DEV,) + b.shape, b.dtype) for b in blocks],
                  [pltpu.SemaphoreType.DMA((7 * n,)), pltpu.SemaphoreType.DMA((7 * n,)), pltpu.SemaphoreType.DMA((n,))],
                  start, finish)


def _pair_exchange(name, blocks):
    n = len(blocks)

    def body(*refs):
        srcs, dsts = refs[:n], refs[n:2 * n]
        send_sems, recv_sems = refs[2 * n:]
        sibling = (lax.axis_index("x"), lax.axis_index("y"), 1 - lax.axis_index("c"))
        copies = [_remote(srcs[t], dsts[t], send_sems, recv_sems, t, sibling) for t in range(n)]
        for cp in copies:
            cp.start()
        for cp in copies:
            cp.wait()

    return pl.pallas_call(
        body, name=name, in_specs=[ANY_SPEC] * n, out_specs=[ANY_SPEC] * n,
        out_shape=[jax.ShapeDtypeStruct(b.shape, b.dtype) for b in blocks],
        scratch_shapes=[pltpu.SemaphoreType.DMA((n,)), pltpu.SemaphoreType.DMA((n,))],
        compiler_params=pltpu.CompilerParams(has_side_effects=True))(*blocks)


def _chip_rider(items):
    n = len(items)
    out_shape = [jax.ShapeDtypeStruct((4,) + tuple(a.shape[1:] if per_chip else a.shape), a.dtype) for a, per_chip in items]

    def plan(srcs, dsts, sems, waits):
        send_sems, recv_sems, local_sems = sems
        x, y, c = lax.axis_index("x"), lax.axis_index("y"), lax.axis_index("c")
        mine = 2 * x + y
        own = [pltpu.make_async_copy(srcs[t].at[mine] if per_chip else srcs[t], dsts[t].at[mine], local_sems.at[t])
               for t, (_, per_chip) in enumerate(items)]
        sends, arrivals = [], []
        for k in range(1, 4):
            px, py = x ^ (k >> 1), y ^ (k & 1)
            peer = 2 * px + py
            for t, (_, per_chip) in enumerate(items):
                src = srcs[t].at[peer] if per_chip else srcs[t]
                sends.append(_remote(src, dsts[t].at[mine], send_sems, recv_sems, 3 * t + k - 1, (px, py, c)))
                if waits:
                    arrivals.append(_remote(src, dsts[t].at[peer], send_sems, recv_sems, 3 * t + k - 1, (px, py, c)))
        return own, sends, arrivals

    def start(srcs, dsts, sems):
        own, sends, _ = plan(srcs, dsts, sems, False)
        for cp in own + sends:
            cp.start()

    def finish(srcs, dsts, sems):
        own, sends, arrivals = plan(srcs, dsts, sems, True)
        for out, arrival in zip(sends, arrivals):
            out.wait_send()
            arrival.wait_recv()
        for cp in own:
            cp.wait()

    return _Rider([a for a, _ in items], out_shape,
                  [pltpu.SemaphoreType.DMA((3 * n,)), pltpu.SemaphoreType.DMA((3 * n,)), pltpu.SemaphoreType.DMA((n,))],
                  start, finish)


def _band_diag(p):
    a, b = p.shape[1:]
    eye = jnp.eye(8, dtype=p.dtype)
    return (p.reshape(4, 8, a, 1, b) * eye[None, :, None, :, None]).reshape(4, 8 * a, 8 * b)


def _band_diag_extract(m, a, b):
    eye = jnp.eye(8, dtype=m.dtype)
    return (m.reshape(4, 8, a, 8, b) * eye[None, :, None, :, None]).sum(axis=3).reshape(32, a, b)


def _proj_cols(w_ref_layout):
    parts = []
    for name in PROJ_ORDER:
        off, width = REF_SPLIT[name]
        part = w_ref_layout[:, off:off + width]
        if width < PROJ[name][1]:
            part = jnp.pad(part, ((0, 0), (0, PROJ[name][1] - width)))
        parts.append(part)
    return jnp.concatenate(parts, axis=1)


def _unproj_cols(w_proj_layout):
    parts = []
    for name in REF_ORDER:
        parts.append(w_proj_layout[:, PROJ[name][0]:PROJ[name][0] + REF_SPLIT[name][1]])
    return jnp.concatenate(parts, axis=1)


def _uq_cols(w):
    w3 = w.reshape(w.shape[0], MLA_HEADS, 96)
    return jnp.concatenate([w3[:, :, :64].reshape(-1, 512), w3[:, :, 64:].reshape(-1, 256)], axis=1)


def _un_uq_cols(w):
    return jnp.concatenate([w[:, :512].reshape(-1, MLA_HEADS, 64), w[:, 512:].reshape(-1, MLA_HEADS, 32)], axis=2).reshape(-1, 768)


def _ukv_cols(w):
    w3 = w.reshape(w.shape[0], MLA_HEADS, 128)
    return jnp.concatenate([w3[:, :, :64].reshape(-1, 512), w3[:, :, 64:].reshape(-1, 512)], axis=1)


def _un_ukv_cols(w):
    return jnp.concatenate([w[:, :512].reshape(-1, MLA_HEADS, 64), w[:, 512:].reshape(-1, MLA_HEADS, 64)], axis=2).reshape(-1, 1024)


def _cols_gathered(g):
    return g.transpose(1, 0, 2).reshape(g.shape[1], -1)


def _cols_slabs(w):
    r, c = w.shape
    return w.reshape(r, N_DEV, c // N_DEV).transpose(1, 0, 2)


def _row(v):
    return v.reshape(1, -1)


def _s5_weights(p):
    flat = dict(lam_re=p["s5_lam_re"].reshape(S5_CH, 1), lam_im=p["s5_lam_im"].reshape(S5_CH, 1),
                log_dt=jnp.repeat(p["s5_log_dt"], S5_STATE).reshape(S5_CH, 1),
                b_re=p["s5_b_re"].reshape(S5_CH, S5_GROUP), b_im=p["s5_b_im"].reshape(S5_CH, S5_GROUP))
    ab_re, ab_im, bb_re, bb_im = _s5_param_fwd(flat["lam_re"], flat["lam_im"], flat["log_dt"], flat["b_re"], flat["b_im"])
    w_bu = jnp.concatenate([_band_diag(bb.reshape(32, 64, 16).transpose(0, 2, 1)) for bb in (bb_re, bb_im)], axis=0)
    w_y = jnp.concatenate([_band_diag(p["s5_c_re"].transpose(0, 2, 1)), -_band_diag(p["s5_c_im"].transpose(0, 2, 1))], axis=0)
    return dict(flat=flat,
                a_re=jnp.broadcast_to(ab_re.reshape(1, S5_CH), (SCAN_SEG, S5_CH)),
                a_im=jnp.broadcast_to(ab_im.reshape(1, S5_CH), (SCAN_SEG, S5_CH)),
                w_bu=w_bu.astype(BF16), w_bu_t=w_bu.swapaxes(1, 2).astype(BF16),
                w_y=w_y.astype(BF16), w_y_t=w_y.swapaxes(1, 2).astype(BF16))


def _layer_fwd(x, w, p, tabs, late_shards=None, rider=None):
    t_len = x.shape[0]
    cos64, sin64, cos32, sin32 = tabs
    s = {}
    (s["h1"],) = _rowwise("pre_mix", lambda xt, g: (_rms(xt, g),), [x], [_row(p["g_pre_mix"])], [((t_len, D_MODEL), BF16)])
    if late_shards is None:
        proj = _mm("proj", s["h1"], w["w_in"], "nn")
    else:
        proj, late = _mm("proj_gather", s["h1"], w["w_in"], "nn", rider=_gather_rider(late_shards))
        w.update(_late_weights(late))
    s["proj"] = proj
    s["qa"], s["ka"], s["va"] = _swa_prep(proj, cos64, sin64)
    s["sinks"] = jnp.broadcast_to(p["swa_sinks"][:, None], (8, 128))
    y_a = _swa_fwd(s["qa"], s["ka"], s["va"], s["sinks"])
    s5 = s["s5"] = _s5_weights(p)
    bu, s["u_p"] = _s5_expand("s5_bu", proj, PROJ["su"][0] // 128, s5["w_bu"])
    s["xs"], s["cin"] = _s5_scan("s5_scan_fwd", bu, s5["a_re"], s5["a_im"])
    s["y_s"] = _s5_contract("s5_y", s["xs"], s5["w_y"])
    (s["z"],) = _rowwise("s5_gelu", lambda y, u, d: (jax.nn.gelu(y + d * u),), [s["y_s"], _seg(proj, "su")],
                         [_row(p["s5_d"])], [((t_len, 512), BF16)])
    s["zz"] = _mm("s5_glu_mm", s["z"], w["w_glu"], "nn")
    (y_b,) = _rowwise("s5_glu", lambda t: (t[:, :512] * jax.nn.sigmoid(t[:, 512:]),), [s["zz"]], [], [((t_len, 512), BF16)])
    s["qc"], s["kc"] = _ret_prep(proj, cos64, sin64)
    y_c, s["prevs"] = _ret_fwd(s["qc"], s["kc"], proj)
    s["cqn"], s["ckvn"] = _rowwise("mla_norm", lambda cq, ckv, gq, gkv: (_rms(cq, gq), _rms(ckv, gkv)),
                                   [_seg(proj, "cq"), _seg(proj, "ckv")], [_row(p["mla_g_q"]), _row(p["mla_g_kv"])],
                                   [((t_len, 256), BF16), ((t_len, 128), BF16)])
    qf = _mm("mla_uq", s["cqn"], w["w_uq"], "nn")
    kvf = _mm("mla_ukv", s["ckvn"], w["w_ukv"], "nn")
    s["Q"], s["K"], s["V"] = _mla_prep(qf, kvf, proj, cos32, sin32)
    y_d, s["lse"], ridden = _mla_fwd(s["Q"], s["K"], s["V"], rider)
    s["ys"] = [y_a, y_b, y_c, y_d]
    s["merged"] = _merge_fwd(s["ys"], w["wb"], proj)
    s["o"] = _mm("out_mm", s["merged"], w["w_out"], "nn")

    def post_mix(xt, ot, g1, g2):
        x1 = xt + _rms(ot, g1)
        return x1, _rms(x1, g2)
    s["x1"], s["h2"] = _rowwise("post_mix", post_mix, [x, s["o"]], [_row(p["g_post_mix"]), _row(p["g_pre_mlp"])],
                                [((t_len, D_MODEL), F32), ((t_len, D_MODEL), BF16)])
    s["a"], s["r"] = _mm("ff1", s["h2"], w["w_ff1"], "nn", out_dtypes=(F32, BF16),
                         epi=lambda acc: (acc, jnp.square(jnp.maximum(acc, 0.0))))
    s["f"] = _mm("ff2", s["r"], w["w_ff2"], "nn")
    (x2,) = _rowwise("post_mlp", lambda x1, f, g: (x1 + _rms(f, g),), [s["x1"], s["f"]], [_row(p["g_post_mlp"])],
                     [((t_len, D_MODEL), F32)])
    return x2, s, ridden


def _layer_bwd(x, dx2, w, p, s, tabs, pending=None, exchange_name=None):
    t_len = x.shape[0]
    cos64, sin64, cos32, sin32 = tabs
    proj = s["proj"]
    big, small = {}, {}

    def post_mlp_bwd(f, d, g):
        df, dg = _rms_bwd(f, g, d)
        return (df,), (dg,)
    df, small["g_post_mlp"] = _rowwise("post_mlp_bwd", post_mlp_bwd, [s["f"], dx2], [_row(p["g_post_mlp"])],
                                       [((t_len, D_MODEL), BF16)], [((1, D_MODEL), F32)])
    da = _mm("ff2_dx", df, w["w_ff2"], "nt", out_dtypes=(BF16,), epi=lambda acc, a: (acc * 2.0 * jnp.maximum(a, 0.0),),
             epi_in=(s["a"],))
    big["w_ff2"] = _mm("ff2_dw", s["r"], df, "tn")
    dh2 = _mm("ff1_dx", da, w["w_ff1"], "nt")
    big["w_ff1"] = _mm("ff1_dw", s["h2"], da, "tn")

    def post_mix_bwd(x1, o, dh, d2, g1, g2):
        dx1_n, dg2 = _rms_bwd(x1, g2, dh)
        dx1 = d2 + dx1_n
        do, dg1 = _rms_bwd(o, g1, dx1)
        return (dx1, do), (dg1, dg2)
    dx1, do, small["g_post_mix"], small["g_pre_mlp"] = _rowwise(
        "post_mix_bwd", post_mix_bwd, [s["x1"], s["o"], dh2, dx2], [_row(p["g_post_mix"]), _row(p["g_pre_mlp"])],
        [((t_len, D_MODEL), F32), ((t_len, D_MODEL), BF16)], [((1, D_MODEL), F32), ((1, D_MODEL), F32)])
    dmerged = _mm("out_dx", do, w["w_out"], "nt")
    big["w_out"] = _mm("out_dw", s["merged"], do, "tn")
    rider, n_pending = None, len(pending or ())
    if exchange_name is not None:
        late_sums, _ = _pair_stage(exchange_name, _grad_slabs(big, LATE))
        rider = _chip_rider([(part, True) for part in list(pending or ()) + late_sums])

    res = _merge_bwd(s["ys"], w["wb"], proj, dmerged)
    dgl, dts = res[:4], res[4:]
    dys = [_mm("branch_dx", dts[b], w["wb"][b], "nt") for b in range(4)]
    big["wb"] = [_mm("branch_dw", s["ys"][b], dts[b], "tn") for b in range(4)]
    seg = {}

    dqa, dka, dva, dka_prev, dva_prev, dsink = _swa_bwd(s["qa"], s["ka"], s["va"], s["sinks"], dys[0])
    seg["sq"], seg["sk"], seg["sv"] = _swa_unprep(dqa, dka, dva, dka_prev, dva_prev, cos64, sin64)
    small["swa_sinks"] = dsink[:, 0]

    def glu_bwd(t, d):
        za, sg = t[:, :512], jax.nn.sigmoid(t[:, 512:])
        return (jnp.concatenate([d * sg, d * za * sg * (1.0 - sg)], axis=1),)
    (dzz,) = _rowwise("s5_glu_bwd", glu_bwd, [s["zz"], dys[1]], [], [((t_len, 1024), BF16)])
    dz = _mm("s5_glu_dx", dzz, w["w_glu"], "nt")
    big["w_glu"] = _mm("s5_glu_dw", s["z"], dzz, "tn")

    def gelu_bwd(y, u, d, skip):
        _, vjp = jax.vjp(jax.nn.gelu, y + skip * u)
        (dy2,) = vjp(d)
        return (dy2, dy2 * skip), (jnp.sum(dy2 * u, axis=0, keepdims=True),)
    dy_s, du_skip, d_skip = _rowwise("s5_gelu_bwd", gelu_bwd, [s["y_s"], _seg(proj, "su"), dz], [_row(p["s5_d"])],
                                     [((t_len, 512), F32), ((t_len, 512), F32)], [((1, 512), F32)])
    small["s5_d"] = d_skip[0]
    s5 = s["s5"]
    dlam_in, dy_p = _s5_expand("s5_dx_in", dy_s, 0, s5["w_y_t"])
    lam, da_re, da_im = _s5_scan("s5_scan_bwd", dlam_in, s5["a_re"], s5["a_im"], states=s["xs"], carry_in=s["cin"])
    du_s5 = _s5_contract("s5_du", lam, s5["w_bu_t"])
    d_w_bu = _s5_band_grad("s5_dbu", s["u_p"], lam, True)
    d_w_y = _s5_band_grad("s5_dc", dy_p, s["xs"], False)
    (seg["su"],) = _rowwise("s5_du_sum", lambda a, b: (a + b,), [du_s5, du_skip], [], [((t_len, 512), BF16)])
    d_bb_re = _band_diag_extract(d_w_bu[:4], 16, 64).transpose(0, 2, 1).reshape(S5_CH, S5_GROUP)
    d_bb_im = _band_diag_extract(d_w_bu[4:], 16, 64).transpose(0, 2, 1).reshape(S5_CH, S5_GROUP)
    small["s5_c_re"] = _band_diag_extract(d_w_y[:4], 64, 16).transpose(0, 2, 1)
    small["s5_c_im"] = -_band_diag_extract(d_w_y[4:], 64, 16).transpose(0, 2, 1)
    fl = s5["flat"]
    group_sum = jnp.repeat(jnp.eye(S5_GROUPS, dtype=F32), S5_STATE, axis=1)
    d_lr, d_li, d_ldt, d_br, d_bi = _s5_param_bwd(fl["lam_re"], fl["lam_im"], fl["log_dt"], fl["b_re"], fl["b_im"], group_sum,
                                                  da_re.reshape(S5_CH, 1), da_im.reshape(S5_CH, 1), d_bb_re, d_bb_im)
    small["s5_lam_re"], small["s5_lam_im"] = d_lr.reshape(32, 64), d_li.reshape(32, 64)
    small["s5_log_dt"] = d_ldt[:, 0]
    small["s5_b_re"], small["s5_b_im"] = d_br.reshape(32, 64, 16), d_bi.reshape(32, 64, 16)

    dqc, dkc, seg["rv"], seg["rg"] = _ret_bwd(s["qc"], s["kc"], proj, s["prevs"], dys[2])
    seg["rq"], seg["rk"] = _ret_unprep(dqc, dkc, cos64, sin64)

    do_h, delta = _mla_bwd_prep(s["ys"][3], dys[3])
    d_q, d_k, d_v, ridden = _mla_bwd(s["Q"], s["K"], s["V"], do_h, s["lse"], delta, rider)
    dqf, dkvf, seg["kr"] = _mla_unprep(d_q, d_k, d_v, cos32, sin32)
    dcqn = _mm("mla_uq_dx", dqf, w["w_uq"], "nt")
    big["w_uq"] = _mm("mla_uq_dw", s["cqn"], dqf, "tn")
    dckvn = _mm("mla_ukv_dx", dkvf, w["w_ukv"], "nt")
    big["w_ukv"] = _mm("mla_ukv_dw", s["ckvn"], dkvf, "tn")

    def mla_norm_bwd(cq, ckv, d1, d2, gq, gkv):
        dcq, dgq = _rms_bwd(cq, gq, d1)
        dckv, dgkv = _rms_bwd(ckv, gkv, d2)
        return (dcq, dckv), (dgq, dgkv)
    seg["cq"], seg["ckv"], dgq, dgkv = _rowwise(
        "mla_norm_bwd", mla_norm_bwd, [_seg(proj, "cq"), _seg(proj, "ckv"), dcqn, dckvn],
        [_row(p["mla_g_q"]), _row(p["mla_g_kv"])], [((t_len, 256), BF16), ((t_len, 128), BF16)],
        [((1, 256), F32), ((1, 128), F32)])
    small["mla_g_q"], small["mla_g_kv"] = dgq[0], dgkv[0]

    dproj = jnp.concatenate([seg[name] for name in PROJ_ORDER[:-1]] + list(dgl), axis=1)
    dh1 = _mm("proj_dx", dproj, w["w_in"], "nt")
    big["w_in"] = _mm("proj_dw", s["h1"], dproj, "tn")

    def pre_mix_bwd(xt, dh, d1, g):
        dxn, dg = _rms_bwd(xt, g, dh)
        return (d1 + dxn,), (dg,)
    dx, small["g_pre_mix"] = _rowwise("pre_mix_bwd", pre_mix_bwd, [x, dh1, dx1], [_row(p["g_pre_mix"])],
                                      [((t_len, D_MODEL), F32)], [((1, D_MODEL), F32)])
    for name in ("g_pre_mix", "g_post_mix", "g_pre_mlp", "g_post_mlp"):
        small[name] = small[name][0]
    return dx, big, small, ridden[:n_pending], ridden[n_pending:]


BIG = ("w_in", "s5_w_glu", "mla_w_uq", "mla_w_ukv", "w_branch", "w_out", "w_ff1", "w_ff2")
SMALL = ("g_pre_mix", "g_post_mix", "g_pre_mlp", "g_post_mlp", "swa_sinks", "s5_lam_re", "s5_lam_im", "s5_log_dt",
         "s5_b_re", "s5_b_im", "s5_c_re", "s5_c_im", "s5_d", "mla_g_q", "mla_g_kv")
WEIGHTS = ("g_pre_mix", "g_post_mix", "g_pre_mlp", "g_post_mlp", "w_in", "swa_sinks", "s5_lam_re", "s5_lam_im", "s5_log_dt",
           "s5_b_re", "s5_b_im", "s5_c_re", "s5_c_im", "s5_d", "s5_w_glu", "mla_g_q", "mla_g_kv", "mla_w_uq", "mla_w_ukv",
           "w_branch", "w_out", "w_ff1", "w_ff2")


EARLY = ("w_in", "s5_w_glu", "mla_w_uq", "mla_w_ukv", "w_branch")
LATE = ("w_out", "w_ff1", "w_ff2")


def _early_weights(shards):
    g = dict(zip(EARLY, shards))
    wb = g["w_branch"].transpose(1, 2, 0, 3).reshape(4, 512, D_MODEL)
    return dict(w_in=_proj_cols(_cols_gathered(g["w_in"])), w_glu=_cols_gathered(g["s5_w_glu"]),
                w_uq=_uq_cols(_cols_gathered(g["mla_w_uq"])), w_ukv=_ukv_cols(_cols_gathered(g["mla_w_ukv"])),
                wb=[wb[b] for b in range(4)])


def _late_weights(shards):
    g = dict(zip(LATE, shards))
    return dict(w_out=g["w_out"].reshape(D_MODEL, D_MODEL), w_ff1=_cols_gathered(g["w_ff1"]),
                w_ff2=g["w_ff2"].reshape(4 * D_MODEL, D_MODEL))


def _grad_slabs(big, names):
    make = dict(w_in=lambda: _cols_slabs(_unproj_cols(big["w_in"])), s5_w_glu=lambda: _cols_slabs(big["w_glu"]),
                mla_w_uq=lambda: _cols_slabs(_un_uq_cols(big["w_uq"])), mla_w_ukv=lambda: _cols_slabs(_un_ukv_cols(big["w_ukv"])),
                w_branch=lambda: jnp.stack(big["wb"]).reshape(4, 512, N_DEV, D_MODEL // N_DEV).transpose(2, 0, 1, 3),
                w_out=lambda: big["w_out"].reshape(N_DEV, D_MODEL // N_DEV, D_MODEL),
                w_ff1=lambda: _cols_slabs(big["w_ff1"]), w_ff2=lambda: big["w_ff2"].reshape(N_DEV, 4 * D_MODEL // N_DEV, D_MODEL))
    return [make[name]() for name in names]


def _pair_stage(name, slabs, extra=()):
    core = lax.axis_index("c")
    own, send = [], []
    for slab in slabs:
        by_chip = slab.reshape((4, 2) + slab.shape[1:])
        own.append(lax.dynamic_index_in_dim(by_chip, core, axis=1, keepdims=False))
        send.append(lax.dynamic_index_in_dim(by_chip, 1 - core, axis=1, keepdims=False).astype(BF16))
    got = _pair_exchange(name, send + list(extra))
    sums = []
    for mine, theirs in zip(own, got):
        flat = (-1, mine.shape[-1])
        (part,) = _rowwise("grads_pair_sum", lambda a, b: (a + b.astype(F32),), [mine.reshape(flat), theirs.reshape(flat)],
                           [], [((int(np.prod(mine.shape[:-1])), mine.shape[-1]), BF16)])
        sums.append(part.reshape(mine.shape))
    return sums, got[len(slabs):]


def kernel(x, g_pre_mix, g_post_mix, g_pre_mlp, g_post_mlp, w_in, swa_sinks, s5_lam_re, s5_lam_im, s5_log_dt, s5_b_re, s5_b_im, s5_c_re, s5_c_im, s5_d, s5_w_glu, mla_g_q, mla_g_kv, mla_w_uq, mla_w_ukv, w_branch, w_out, w_ff1, w_ff2, loss_target, m_g_pre_mix, m_g_post_mix, m_g_pre_mlp, m_g_post_mlp, m_w_in, m_swa_sinks, m_s5_lam_re, m_s5_lam_im, m_s5_log_dt, m_s5_b_re, m_s5_b_im, m_s5_c_re, m_s5_c_im, m_s5_d, m_s5_w_glu, m_mla_g_q, m_mla_g_kv, m_mla_w_uq, m_mla_w_ukv, m_w_branch, m_w_out, m_w_ff1, m_w_ff2, v_g_pre_mix, v_g_post_mix, v_g_pre_mlp, v_g_post_mlp, v_w_in, v_swa_sinks, v_s5_lam_re, v_s5_lam_im, v_s5_log_dt, v_s5_b_re, v_s5_b_im, v_s5_c_re, v_s5_c_im, v_s5_d, v_s5_w_glu, v_mla_g_q, v_mla_g_kv, v_mla_w_uq, v_mla_w_ukv, v_w_branch, v_w_out, v_w_ff1, v_w_ff2):
    weights = dict(g_pre_mix=g_pre_mix, g_post_mix=g_post_mix, g_pre_mlp=g_pre_mlp, g_post_mlp=g_post_mlp, w_in=w_in,
                   swa_sinks=swa_sinks, s5_lam_re=s5_lam_re, s5_lam_im=s5_lam_im, s5_log_dt=s5_log_dt, s5_b_re=s5_b_re,
                   s5_b_im=s5_b_im, s5_c_re=s5_c_re, s5_c_im=s5_c_im, s5_d=s5_d, s5_w_glu=s5_w_glu, mla_g_q=mla_g_q,
                   mla_g_kv=mla_g_kv, mla_w_uq=mla_w_uq, mla_w_ukv=mla_w_ukv, w_branch=w_branch, w_out=w_out, w_ff1=w_ff1,
                   w_ff2=w_ff2)
    m_in = dict(g_pre_mix=m_g_pre_mix, g_post_mix=m_g_post_mix, g_pre_mlp=m_g_pre_mlp, g_post_mlp=m_g_post_mlp, w_in=m_w_in,
                swa_sinks=m_swa_sinks, s5_lam_re=m_s5_lam_re, s5_lam_im=m_s5_lam_im, s5_log_dt=m_s5_log_dt,
                s5_b_re=m_s5_b_re, s5_b_im=m_s5_b_im, s5_c_re=m_s5_c_re, s5_c_im=m_s5_c_im, s5_d=m_s5_d,
                s5_w_glu=m_s5_w_glu, mla_g_q=m_mla_g_q, mla_g_kv=m_mla_g_kv, mla_w_uq=m_mla_w_uq, mla_w_ukv=m_mla_w_ukv,
                w_branch=m_w_branch, w_out=m_w_out, w_ff1=m_w_ff1, w_ff2=m_w_ff2)
    v_in = dict(g_pre_mix=v_g_pre_mix, g_post_mix=v_g_post_mix, g_pre_mlp=v_g_pre_mlp, g_post_mlp=v_g_post_mlp, w_in=v_w_in,
                swa_sinks=v_swa_sinks, s5_lam_re=v_s5_lam_re, s5_lam_im=v_s5_lam_im, s5_log_dt=v_s5_log_dt,
                s5_b_re=v_s5_b_re, s5_b_im=v_s5_b_im, s5_c_re=v_s5_c_re, s5_c_im=v_s5_c_im, s5_d=v_s5_d,
                s5_w_glu=v_s5_w_glu, mla_g_q=v_mla_g_q, mla_g_kv=v_mla_g_kv, mla_w_uq=v_mla_w_uq, mla_w_ukv=v_mla_w_ukv,
                w_branch=v_w_branch, w_out=v_w_out, w_ff1=v_w_ff1, w_ff2=v_w_ff2)
    depth = g_pre_mix.shape[0]
    t_len = x.shape[1]
    tabs = _rope_tables(t_len, 32) + _rope_tables(t_len, 16)

    def shards(l, names):
        return [weights[name][l].astype(BF16) for name in names]
    early = _run_rider("gather_weights", _gather_rider(shards(0, EARLY)))

    xs, saved, layer_w, layer_p = [x[0]], [], [], []
    for l in range(depth):
        layer_w.append(_early_weights(early))
        layer_p.append({name: weights[name][l] for name in SMALL})
        x_next, s, early = _layer_fwd(xs[-1], layer_w[l], layer_p[l], tabs, shards(l, LATE),
                                      _gather_rider(shards(l + 1, EARLY)) if l + 1 < depth else None)
        xs.append(x_next)
        saved.append(s)

    def loss_fn(y, tgt):
        err = y - tgt
        part = 0.5 * jnp.sum(jnp.mean(err * err, axis=-1, keepdims=True), axis=0, keepdims=True)
        return (err * (1.0 / D_MODEL),), (jnp.broadcast_to(part, (8, 128)),)
    dx, loss_part = _rowwise("loss", loss_fn, [xs[-1], loss_target[0]], [], [((t_len, D_MODEL), F32)], [((8, 128), F32)])
    loss = lax.psum(loss_part[0, 0], ("x", "y", "c"))

    small_grads, arrived, pending = [None] * depth, [{} for _ in range(depth)], None
    for l in reversed(range(depth)):
        dx, big, small_grads[l], got_pending, got_late = _layer_bwd(xs[l], dx, layer_w[l], layer_p[l], saved[l], tabs, pending,
                                                                    "late_grads_to_sibling")
        if pending is not None:
            arrived[l + 1].update(zip(EARLY, got_pending))
        arrived[l].update(zip(LATE, got_late))
        if l > 0:
            pending, _ = _pair_stage("early_grads_to_sibling", _grad_slabs(big, EARLY))
    small_vec = jnp.concatenate([jnp.stack([small_grads[l][name] for l in range(depth)]).reshape(-1) for name in SMALL])
    n_small = small_vec.shape[0]
    rows = -(-n_small // 1024) * 8
    small_mat = jnp.pad(small_vec, (0, rows * 128 - n_small)).reshape(rows, 128)
    sums, (sibling_small,) = _pair_stage("last_grads_to_sibling", _grad_slabs(big, EARLY), [small_mat])
    small_pair = jnp.where(lax.axis_index("c") == 0, jnp.stack([small_mat, sibling_small]), jnp.stack([sibling_small, small_mat]))
    recv = _run_rider("grads_to_chips", _chip_rider([(part, True) for part in sums] + [(small_pair, False)]))
    arrived[0].update(zip(EARLY, recv[:-1]))
    small_sum = _sum_slabs("sum_small_grads", recv[-1].reshape((N_DEV,) + small_mat.shape)).reshape(-1)

    grads, deltas, new_m, new_v = {}, {}, {}, {}
    for name in BIG:
        slabs = jnp.stack([arrived[l][name] for l in range(depth)], axis=1)
        grads[name], deltas[name], new_m[name], new_v[name] = _adamw("adamw_" + name, weights[name], slabs, m_in[name],
                                                                     v_in[name], 4)
    off = 0
    for name in SMALL:
        size = int(np.prod(weights[name].shape))
        g = small_sum[off:off + size].reshape(weights[name].shape)
        off += size
        grads[name], deltas[name], new_m[name], new_v[name] = _adamw("adamw_" + name, weights[name], g, m_in[name],
                                                                     v_in[name], False)
    return (loss, dx[None], *[grads[n] for n in WEIGHTS], *[deltas[n] for n in WEIGHTS], *[new_m[n] for n in WEIGHTS],
            *[new_v[n] for n in WEIGHTS])
```

```python
import functools
import math

import numpy as np
import jax
import jax.numpy as jnp
from jax import lax
from jax.experimental import pallas as pl
from jax.experimental.pallas import tpu as pltpu

F32, BF16 = jnp.float32, jnp.bfloat16
EPS = 1e-6
NEG_INF = -1e30
ROPE_THETA = 10000.0
D_MODEL = 1024
N_DEV = 8
BLOCK = 128
SWA_HEADS, SWA_KV = 8, 2
RET_HEADS, RET_DK, RET_DV = 4, 64, 128
MLA_HEADS = 8
S5_GROUPS, S5_STATE, S5_GROUP = 32, 64, 16
S5_CH = S5_GROUPS * S5_STATE
SCAN_LANES = 256
SCAN_SEG = 8
ADAM_LR, ADAM_B1, ADAM_B2, ADAM_EPS, ADAM_WD, ADAM_STEP = 0.001, 0.9, 0.999, 1e-08, 0.01, 10
VMEM_MB = 56

PROJ = dict(sq=(0, 512), su=(512, 512), rv=(1024, 512), rg=(1536, 512), rq=(2048, 256), rk=(2304, 256),
            cq=(2560, 256), sk=(2816, 128), sv=(2944, 128), ckv=(3072, 128), kr=(3200, 128), gates=(3328, 4096))
PROJ_W = 7424
REF_SPLIT = dict(sq=(0, 512), sk=(512, 128), sv=(640, 128), su=(768, 512), rq=(1280, 256), rk=(1536, 256),
                 rv=(1792, 512), rg=(2304, 512), cq=(2816, 256), ckv=(3072, 128), kr=(3200, 32), gates=(3232, 4096))
PROJ_ORDER = ("sq", "su", "rv", "rg", "rq", "rk", "cq", "sk", "sv", "ckv", "kr", "gates")
REF_ORDER = ("sq", "sk", "sv", "su", "rq", "rk", "rv", "rg", "cq", "ckv", "kr", "gates")

NN = (((1,), (0,)), ((), ()))
NT = (((1,), (1,)), ((), ()))
TN = (((0,), (0,)), ((), ()))


def _params(sem):
    return pltpu.CompilerParams(dimension_semantics=sem, vmem_limit_bytes=VMEM_MB * 2 ** 20)


def _raw_dot(a, b, dims):
    return lax.dot_general(a.astype(BF16), b.astype(BF16), dims, preferred_element_type=F32)


@jax.custom_vjp
def _dnn(a, b):
    return _raw_dot(a, b, NN)


@jax.custom_vjp
def _dnt(a, b):
    return _raw_dot(a, b, NT)


@jax.custom_vjp
def _dtn(a, b):
    return _raw_dot(a, b, TN)


_dnn.defvjp(lambda a, b: (_dnn(a, b), (a, b)),
            lambda r, g: (_dnt(g, r[1]).astype(r[0].dtype), _dtn(r[0], g).astype(r[1].dtype)))
_dnt.defvjp(lambda a, b: (_dnt(a, b), (a, b)),
            lambda r, g: (_dnn(g, r[1]).astype(r[0].dtype), _dtn(g, r[0]).astype(r[1].dtype)))
_dtn.defvjp(lambda a, b: (_dtn(a, b), (a, b)),
            lambda r, g: (_dnt(r[1], g).astype(r[0].dtype), _dnn(r[0], g).astype(r[1].dtype)))


class _Cols:
    def __init__(self, arr, width, index):
        self.arr, self.width, self.index = arr, width, index


def _seg(proj, name):
    off, width = PROJ[name]
    return _Cols(proj, width, off // width)


def _row_spec(shape, tile):
    lead = len(shape) - 2
    return pl.BlockSpec(tuple(shape[:-2]) + (tile, shape[-1]), lambda i, lead=lead: (0,) * lead + (i, 0))


def _const_spec(shape):
    nd = len(shape)
    return pl.BlockSpec(tuple(shape), lambda *_, nd=nd: (0,) * nd)


ROW_VMEM_BUDGET = 24 * 2 ** 20


def _rowwise(name, fn, rows, consts, out_rows, out_accs=()):
    arrs = [r.arr if isinstance(r, _Cols) else r for r in rows]
    t_len = arrs[0].shape[-2]
    row_bytes = sum((r.width if isinstance(r, _Cols) else int(np.prod(r.shape)) // r.shape[-2]) * a.dtype.itemsize
                    for r, a in zip(rows, arrs))
    row_bytes += sum(int(np.prod(s)) // s[-2] * jnp.dtype(d).itemsize for s, d in out_rows)
    tile = next((t for t in (1024, 512, 256, 128) if t_len % t == 0 and 2 * t * row_bytes <= ROW_VMEM_BUDGET), min(128, t_len))
    n = t_len // tile
    in_specs = []
    for r in rows:
        if isinstance(r, _Cols):
            in_specs.append(pl.BlockSpec((tile, r.width), lambda i, c=r.index: (i, c)))
        else:
            in_specs.append(_row_spec(r.shape, tile))
    in_specs += [_const_spec(c.shape) for c in consts]
    out_specs = [_row_spec(s, tile) for s, _ in out_rows] + [_const_spec(s) for s, _ in out_accs]
    out_shape = [jax.ShapeDtypeStruct(s, d) for s, d in tuple(out_rows) + tuple(out_accs)]
    n_in, n_or = len(rows) + len(consts), len(out_rows)

    def body(*refs):
        res = fn(*[r[...] for r in refs[:n_in]])
        row_out, acc_out = res if out_accs else (res, ())
        for ref, val in zip(refs[n_in:n_in + n_or], row_out):
            if isinstance(val, (list, tuple)):
                for h, v_h in enumerate(val):
                    ref[h] = v_h.astype(ref.dtype)
            else:
                ref[...] = val.astype(ref.dtype)
        first = pl.program_id(0) == 0
        for ref, val in zip(refs[n_in + n_or:], acc_out):
            @pl.when(first)
            def _(ref=ref, val=val):
                ref[...] = val

            @pl.when(jnp.logical_not(first))
            def _(ref=ref, val=val):
                ref[...] += val

    return pl.pallas_call(body, name=name, grid=(n,), in_specs=in_specs, out_specs=out_specs, out_shape=out_shape,
                          compiler_params=_params(("arbitrary",)))(*arrs, *consts)


MM_VMEM_BUDGET = 44 * 2 ** 20


def _pick_tn(n, io_col_bytes, tmp_col_bytes, fixed_bytes):
    best = None
    for cand in range(128, n + 1, 128):
        if n % cand == 0 and 2 * (fixed_bytes + cand * io_col_bytes) + cand * tmp_col_bytes <= MM_VMEM_BUDGET:
            best = cand
    return best if best is not None else min(n, 128)


def _mm(name, a, b, mode, out_dtypes=(F32,), tm=512, tn=None, epi=None, epi_in=(), rider=None):
    if mode == "nn":
        (m, k), n = a.shape, b.shape[1]
    elif mode == "nt":
        (m, k), n = a.shape, b.shape[0]
    else:
        (k, m), n = a.shape, b.shape[1]
    tm = min(tm, m)
    if tn is None:
        io_col = k * b.dtype.itemsize + tm * (sum(jnp.dtype(d).itemsize for d in out_dtypes) + sum(e.dtype.itemsize for e in epi_in))
        tmp_col = tm * 4 * (2 if epi is not None else 1)
        tn = _pick_tn(n, io_col, tmp_col, tm * k * a.dtype.itemsize)
    tn = min(tn, n)
    assert m % tm == 0 and n % tn == 0, (name, m, n, tm, tn)
    a_spec = pl.BlockSpec((k, tm), lambda i, j: (0, i)) if mode == "tn" else pl.BlockSpec((tm, k), lambda i, j: (i, 0))
    b_spec = pl.BlockSpec((tn, k), lambda i, j: (j, 0)) if mode == "nt" else pl.BlockSpec((k, tn), lambda i, j: (0, j))
    dims = dict(nn=NN, nt=NT, tn=TN)[mode]
    o_spec = pl.BlockSpec((tm, tn), lambda i, j: (i, j))
    n_e = len(epi_in)

    def body(a_ref, b_ref, *refs):
        acc = _raw_dot(a_ref[...], b_ref[...], dims)
        outs = epi(acc, *[r[...] for r in refs[:n_e]]) if epi is not None else (acc,)
        for ref, val in zip(refs[n_e:], outs):
            ref[...] = val.astype(ref.dtype)

    grid = (m // tm, n // tn)
    ride_in, ride_out, ride_sems = ([], [], []) if rider is None else (rider.inputs, rider.out_shape, rider.sems)
    any_spec = pl.BlockSpec(memory_space=pl.ANY)
    res = pl.pallas_call(_ride(body, rider, (0, 2 + n_e, len(out_dtypes), 0), grid), name=name, grid=grid,
                         in_specs=[a_spec, b_spec] + [o_spec] * n_e + [any_spec] * len(ride_in),
                         out_specs=[o_spec] * len(out_dtypes) + [any_spec] * len(ride_out),
                         out_shape=[jax.ShapeDtypeStruct((m, n), d) for d in out_dtypes] + ride_out, scratch_shapes=ride_sems,
                         compiler_params=_params(("parallel", "parallel") if rider is None else ("arbitrary", "arbitrary")))(
                             a, b, *epi_in, *ride_in)
    own = res[0] if len(out_dtypes) == 1 else res[:len(out_dtypes)]
    return own if rider is None else (own, res[len(out_dtypes):])


def _rms(x, g):
    return x * lax.rsqrt(jnp.mean(x * x, axis=-1, keepdims=True) + EPS) * g


def _rms_bwd(x, g, dy):
    _, vjp = jax.vjp(_rms, x, g)
    return vjp(dy)


def _rope_tables(t_len, half):
    dim = 2 * half
    inv = 1.0 / (ROPE_THETA ** (jnp.arange(0, dim, 2, dtype=F32) / dim))
    ang = jnp.arange(t_len, dtype=F32)[:, None] * inv[None, :]
    c, s = jnp.cos(ang), jnp.sin(ang)
    reps = 128 // dim
    return jnp.tile(jnp.concatenate([c, c], 1), (1, reps)), jnp.tile(jnp.concatenate([-s, s], 1), (1, reps))


def _rope(x, cos, sin, half):
    w = x.shape[-1]
    reps = w // 128
    if reps > 1:
        cos, sin = jnp.tile(cos, (1, reps)), jnp.tile(sin, (1, reps))
    lane = lax.broadcasted_iota(jnp.int32, x.shape, 1)
    partner = jnp.where((lane % (2 * half)) < half, pltpu.roll(x, w - half, 1), pltpu.roll(x, half, 1))
    return x * cos + partner * sin


SWA_GROUP = SWA_HEADS // SWA_KV
SWA_QB = 4
SWA_TB_FWD, SWA_TB_BWD = 4, 2


def _swa_mask(tb, first_block):
    rows, keys = SWA_GROUP * tb * BLOCK, (tb + 1) * BLOCK
    qi = lax.broadcasted_iota(jnp.int32, (rows, keys), 0) % (tb * BLOCK) + BLOCK
    kj = lax.broadcasted_iota(jnp.int32, (rows, keys), 1)
    diff = qi - kj
    band = (diff >= 0) & (diff < BLOCK)
    return band if first_block is False else band & (jnp.logical_not(first_block) | (kj >= BLOCK))


def _swa_stack(ref, j, b, tb, dtype=None):
    rows = slice(BLOCK * b, BLOCK * (b + tb))
    parts = [ref[rows, 64 * (SWA_GROUP * j + g):64 * (SWA_GROUP * j + g) + 64] for g in range(SWA_GROUP)]
    out = jnp.concatenate(parts, axis=0)
    return out if dtype is None else out.astype(dtype)


def _swa_sinks(s_ref, j, tb):
    return jnp.concatenate([jnp.broadcast_to(s_ref[SWA_GROUP * j + g:SWA_GROUP * j + g + 1, 0:1], (tb * BLOCK, 1))
                            for g in range(SWA_GROUP)], axis=0)


def _swa_head(q, kw, vw, sink, mask):
    sc = _dnt(q, kw) * (64 ** -0.5)
    sc = jnp.where(mask, sc, NEG_INF)
    m = lax.stop_gradient(jnp.maximum(jnp.max(sc, axis=-1, keepdims=True), sink))
    p = jnp.exp(sc - m)
    denom = jnp.sum(p, axis=-1, keepdims=True) + jnp.exp(sink - m)
    return _dnn(p / denom, vw)


def _swa_prep(proj, cos, sin):
    def fn(sq, sk, sv, c, s):
        return _rope(sq, c, s, 32), _rope(sk, c, s, 32), sv
    t_len = proj.shape[0]
    return _rowwise("swa_prep", fn, [_seg(proj, "sq"), _seg(proj, "sk"), _seg(proj, "sv"), cos, sin], [],
                    [((t_len, 512), BF16), ((t_len, 128), BF16), ((t_len, 128), BF16)])


def _swa_specs(t_len, tb_max=1):
    qb = min(SWA_QB, t_len // BLOCK)
    tb = min(tb_max, qb)
    rows = qb * BLOCK
    cur = lambda w: pl.BlockSpec((rows, w), lambda i: (i, 0))
    prev = pl.BlockSpec((BLOCK, 128), lambda i: (jnp.maximum(qb * i - 1, 0), 0))
    return qb, tb, t_len // rows, cur, prev


def _swa_fwd(q, k, v, sinks):
    t_len = q.shape[0]
    qb, tb, steps, cur, prev = _swa_specs(t_len, SWA_TB_FWD)

    def body(q_ref, kp_ref, kc_ref, vp_ref, vc_ref, s_ref, o_ref):
        first = pl.program_id(0) == 0
        k_all = jnp.concatenate([kp_ref[...], kc_ref[...]], axis=0)
        v_all = jnp.concatenate([vp_ref[...], vc_ref[...]], axis=0)
        for j in range(SWA_KV):
            cols = slice(64 * j, 64 * j + 64)
            sink = _swa_sinks(s_ref, j, tb)
            for b in range(0, qb, tb):
                window = slice(BLOCK * b, BLOCK * (b + tb + 1))
                mask = _swa_mask(tb, first if b == 0 else False)
                o = _swa_head(_swa_stack(q_ref, j, b, tb), k_all[window, cols], v_all[window, cols], sink, mask)
                for g in range(SWA_GROUP):
                    h = SWA_GROUP * j + g
                    o_ref[BLOCK * b:BLOCK * (b + tb), 64 * h:64 * h + 64] = o[tb * BLOCK * g:tb * BLOCK * (g + 1)].astype(o_ref.dtype)

    return pl.pallas_call(
        body, name="swa_fwd", grid=(steps,),
        in_specs=[cur(512), prev, cur(128), prev, cur(128), _const_spec((8, 128))],
        out_specs=cur(512), out_shape=jax.ShapeDtypeStruct((t_len, 512), BF16),
        compiler_params=_params(("parallel",)))(q, k, k, v, v, sinks)


def _swa_bwd(q, k, v, sinks, dy):
    t_len = q.shape[0]
    qb, tb, steps, cur, prev = _swa_specs(t_len, SWA_TB_BWD)

    def body(q_ref, kp_ref, kc_ref, vp_ref, vc_ref, s_ref, dy_ref, dq_ref, dk_ref, dv_ref, dkp_ref, dvp_ref, ds_ref):
        first = pl.program_id(0) == 0

        @pl.when(first)
        def _():
            ds_ref[...] = jnp.zeros_like(ds_ref)

        k_all = jnp.concatenate([kp_ref[...], kc_ref[...]], axis=0)
        v_all = jnp.concatenate([vp_ref[...], vc_ref[...]], axis=0)
        for j in range(SWA_KV):
            cols = slice(64 * j, 64 * j + 64)
            sink = _swa_sinks(s_ref, j, tb)
            dk_parts = [jnp.zeros((BLOCK, 64), F32) for _ in range(qb + 1)]
            dv_parts = [jnp.zeros((BLOCK, 64), F32) for _ in range(qb + 1)]
            dsink = jnp.zeros((SWA_GROUP * tb * BLOCK, 1), F32)
            for b in range(0, qb, tb):
                window = slice(BLOCK * b, BLOCK * (b + tb + 1))
                mask = _swa_mask(tb, first if b == 0 else False)
                _, vjp = jax.vjp(lambda a, b_, c, d, mask=mask: _swa_head(a, b_, c, d, mask), _swa_stack(q_ref, j, b, tb),
                                 k_all[window, cols], v_all[window, cols], sink)
                dq, dkw, dvw, dsink_b = vjp(_swa_stack(dy_ref, j, b, tb, F32))
                dkw, dvw = dkw.astype(F32), dvw.astype(F32)
                for r in range(tb + 1):
                    dk_parts[b + r] = dk_parts[b + r] + dkw[BLOCK * r:BLOCK * (r + 1)]
                    dv_parts[b + r] = dv_parts[b + r] + dvw[BLOCK * r:BLOCK * (r + 1)]
                dsink = dsink + dsink_b
                for g in range(SWA_GROUP):
                    h = SWA_GROUP * j + g
                    dq_ref[BLOCK * b:BLOCK * (b + tb), 64 * h:64 * h + 64] = dq[tb * BLOCK * g:tb * BLOCK * (g + 1)].astype(F32)
            dkp_ref[:, cols], dvp_ref[:, cols] = dk_parts[0], dv_parts[0]
            for b in range(qb):
                dk_ref[BLOCK * b:BLOCK * b + BLOCK, cols] = dk_parts[b + 1]
                dv_ref[BLOCK * b:BLOCK * b + BLOCK, cols] = dv_parts[b + 1]
            for g in range(SWA_GROUP):
                h = SWA_GROUP * j + g
                ds_ref[h:h + 1, :] += jnp.broadcast_to(jnp.sum(dsink[tb * BLOCK * g:tb * BLOCK * (g + 1)], axis=0, keepdims=True), (1, 128))

    part = pl.BlockSpec((BLOCK, 128), lambda i: (i, 0))
    return pl.pallas_call(
        body, name="swa_bwd", grid=(steps,),
        in_specs=[cur(512), prev, cur(128), prev, cur(128), _const_spec((8, 128)), cur(512)],
        out_specs=[cur(512), cur(128), cur(128), part, part, _const_spec((8, 128))],
        out_shape=[jax.ShapeDtypeStruct((t_len, 512), F32), jax.ShapeDtypeStruct((t_len, 128), F32),
                   jax.ShapeDtypeStruct((t_len, 128), F32), jax.ShapeDtypeStruct((steps * BLOCK, 128), F32),
                   jax.ShapeDtypeStruct((steps * BLOCK, 128), F32), jax.ShapeDtypeStruct((8, 128), F32)],
        compiler_params=_params(("arbitrary",)))(q, k, k, v, v, sinks, dy)


def _swa_unprep(dq, dk, dv, dk_prev, dv_prev, cos, sin):
    t_len = dq.shape[0]
    qb, _, steps, cur, _ = _swa_specs(t_len)
    tail = (qb - 1) * BLOCK

    def body(dq_ref, dk_ref, dv_ref, dkn_ref, dvn_ref, c_ref, s_ref, oq_ref, ok_ref, ov_ref):
        more = (pl.program_id(0) < steps - 1).astype(F32)
        cos_t, sin_t = c_ref[...], -s_ref[...]
        oq_ref[...] = _rope(dq_ref[...], cos_t, sin_t, 32).astype(BF16)
        dk_t = jnp.concatenate([dk_ref[:tail], dk_ref[tail:] + more * dkn_ref[...]], axis=0) if tail else dk_ref[...] + more * dkn_ref[...]
        dv_t = jnp.concatenate([dv_ref[:tail], dv_ref[tail:] + more * dvn_ref[...]], axis=0) if tail else dv_ref[...] + more * dvn_ref[...]
        ok_ref[...] = _rope(dk_t, cos_t, sin_t, 32).astype(BF16)
        ov_ref[...] = dv_t.astype(BF16)

    nxt = pl.BlockSpec((BLOCK, 128), lambda i: (jnp.minimum(i + 1, steps - 1), 0))
    return pl.pallas_call(
        body, name="swa_unprep", grid=(steps,),
        in_specs=[cur(512), cur(128), cur(128), nxt, nxt, cur(128), cur(128)],
        out_specs=[cur(512), cur(128), cur(128)],
        out_shape=[jax.ShapeDtypeStruct((t_len, 512), BF16), jax.ShapeDtypeStruct((t_len, 128), BF16),
                   jax.ShapeDtypeStruct((t_len, 128), BF16)],
        compiler_params=_params(("parallel",)))(dq, dk, dv, dk_prev, dv_prev, cos, sin)


def _s5_discretize(lam_re, lam_im, log_dt, b_re, b_im):
    dt = jnp.exp(log_dt)
    mag = jnp.exp(lam_re * dt)
    ab_re, ab_im = mag * jnp.cos(lam_im * dt), mag * jnp.sin(lam_im * dt)
    den = lam_re * lam_re + lam_im * lam_im
    nr, ni = ab_re - 1.0, ab_im
    f_re = (nr * lam_re + ni * lam_im) / den
    f_im = (ni * lam_re - nr * lam_im) / den
    return ab_re, ab_im, f_re * b_re - f_im * b_im, f_re * b_im + f_im * b_re


def _s5_param_fwd(lam_re, lam_im, log_dt, b_re, b_im):
    def body(*refs):
        outs = _s5_discretize(*[r[...] for r in refs[:5]])
        for ref, val in zip(refs[5:], outs):
            ref[...] = val
    n = S5_CH
    return pl.pallas_call(body, name="s5_param_fwd",
                          out_shape=[jax.ShapeDtypeStruct((n, 1), F32)] * 2 + [jax.ShapeDtypeStruct((n, 16), F32)] * 2,
                          )(lam_re, lam_im, log_dt, b_re, b_im)


def _s5_param_bwd(lam_re, lam_im, log_dt, b_re, b_im, group_sum, d_ab_re, d_ab_im, d_bb_re, d_bb_im):
    def body(*refs):
        ins = [r[...] for r in refs[:10]]
        _, vjp = jax.vjp(_s5_discretize, *ins[:5])
        d_lr, d_li, d_ldt, d_br, d_bi = vjp(tuple(ins[6:10]))
        o_lr, o_li, o_ldt, o_br, o_bi = refs[10:]
        o_lr[...], o_li[...], o_br[...], o_bi[...] = d_lr, d_li, d_br, d_bi
        o_ldt[...] = lax.dot_general(ins[5], jnp.broadcast_to(d_ldt, (S5_CH, 128)), NN,
                                     precision=lax.Precision.HIGHEST, preferred_element_type=F32)
    n = S5_CH
    return pl.pallas_call(body, name="s5_param_bwd",
                          out_shape=[jax.ShapeDtypeStruct((n, 1), F32), jax.ShapeDtypeStruct((n, 1), F32),
                                     jax.ShapeDtypeStruct((S5_GROUPS, 128), F32),
                                     jax.ShapeDtypeStruct((n, 16), F32), jax.ShapeDtypeStruct((n, 16), F32)],
                          )(lam_re, lam_im, log_dt, b_re, b_im, group_sum, d_ab_re, d_ab_im, d_bb_re, d_bb_im)


def _s5_expand(name, u, first_band, w):
    t_len = u.shape[0]
    tile = min(512, t_len)
    seg = t_len // SCAN_SEG

    def body(u_ref, wr_ref, wi_ref, o_ref, up_ref, rows):
        base = pl.program_id(1) * (tile // SCAN_SEG)
        for r in range(tile // SCAN_SEG):
            rows[SCAN_SEG * r:SCAN_SEG * (r + 1), :] = u_ref[pl.ds(base + r, SCAN_SEG, stride=seg), :]
        u_t = rows[...].astype(BF16)
        up_ref[...] = u_t
        o_ref[0] = _raw_dot(u_t, wr_ref[...], NN)
        o_ref[1] = _raw_dot(u_t, wi_ref[...], NN)

    return pl.pallas_call(
        body, name=name, grid=(4, t_len // tile),
        in_specs=[pl.BlockSpec((t_len, 128), lambda b, i: (0, first_band + b)), pl.BlockSpec((None, 128, 512), lambda b, i: (b, 0, 0)),
                  pl.BlockSpec((None, 128, 512), lambda b, i: (4 + b, 0, 0))],
        out_specs=[pl.BlockSpec((2, tile, 512), lambda b, i: (0, i, b)), pl.BlockSpec((tile, 128), lambda b, i: (i, b))],
        out_shape=[jax.ShapeDtypeStruct((2, t_len, S5_CH), F32), jax.ShapeDtypeStruct((t_len, 512), BF16)],
        scratch_shapes=[pltpu.VMEM((tile, 128), F32)], compiler_params=_params(("parallel", "arbitrary")))(u, w, w)


def _s5_contract(name, x, w):
    t_len = x.shape[1]
    tile = min(512, t_len)
    seg = t_len // SCAN_SEG

    def body(x_ref, wr_ref, wi_ref, o_ref):
        y = _raw_dot(x_ref[0], wr_ref[...], NN) + _raw_dot(x_ref[1], wi_ref[...], NN)
        base = pl.program_id(1) * (tile // SCAN_SEG)
        for r in range(tile // SCAN_SEG):
            o_ref[pl.ds(base + r, SCAN_SEG, stride=seg), :] = y[SCAN_SEG * r:SCAN_SEG * (r + 1)]

    return pl.pallas_call(
        body, name=name, grid=(4, t_len // tile),
        in_specs=[pl.BlockSpec((2, tile, 512), lambda b, i: (0, i, b)), pl.BlockSpec((None, 512, 128), lambda b, i: (b, 0, 0)),
                  pl.BlockSpec((None, 512, 128), lambda b, i: (4 + b, 0, 0))],
        out_specs=pl.BlockSpec((t_len, 128), lambda b, i: (0, b)),
        out_shape=jax.ShapeDtypeStruct((t_len, 512), F32), compiler_params=_params(("parallel", "arbitrary")))(x, w, w)


def _s5_band_grad(name, narrow, wide, narrow_first):
    t_len = narrow.shape[0]

    def body(n_ref, w_ref, o_ref):
        if narrow_first:
            o_ref[...] = _raw_dot(n_ref[...], w_ref[...], TN)
        else:
            o_ref[...] = _raw_dot(w_ref[...], n_ref[...], TN)

    shape = (8, 128, 512) if narrow_first else (8, 512, 128)
    return pl.pallas_call(
        body, name=name, grid=(8,),
        in_specs=[pl.BlockSpec((t_len, 128), lambda bb: (0, bb % 4)), pl.BlockSpec((None, t_len, 512), lambda bb: (bb // 4, 0, bb % 4))],
        out_specs=pl.BlockSpec((None,) + shape[1:], lambda bb: (bb, 0, 0)),
        out_shape=jax.ShapeDtypeStruct(shape, F32), compiler_params=_params(("parallel",)))(narrow, wide)


def _s5_scan(name, bu, a_re, a_im, states=None, carry_in=None):
    t_len = bu.shape[1]
    n = t_len // SCAN_SEG
    log_n = int(round(math.log2(n)))
    assert 2 ** log_n == n
    reverse = states is not None
    lanes = SCAN_LANES
    nblk = S5_CH // lanes

    def body(*refs):
        if reverse:
            b_ref, ar_ref, ai_ref, x_ref, cin_ref, o_ref, dar_ref, dai_ref, c_scr = refs
            xr_ref, xi_ref, cr_ref, ci_ref = x_ref.at[0], x_ref.at[1], cin_ref.at[0], cin_ref.at[1]
        else:
            b_ref, ar_ref, ai_ref, o_ref, co_ref, c_scr = refs
            cor_ref, coi_ref = co_ref.at[0], co_ref.at[1]
        br_ref, bi_ref, or_ref, oi_ref = b_ref.at[0], b_ref.at[1], o_ref.at[0], o_ref.at[1]
        ar = ar_ref[...]
        ai = -ai_ref[...] if reverse else ai_ref[...]

        def tile_at(i):
            return pl.multiple_of((n - 1 - i if reverse else i) * SCAN_SEG, SCAN_SEG)

        def advance(s, row):
            sr, si = s
            return (ar * sr - ai * si + br_ref[pl.ds(row, SCAN_SEG), :], ar * si + ai * sr + bi_ref[pl.ds(row, SCAN_SEG), :])

        zero = jnp.zeros((SCAN_SEG, lanes), F32)
        fin_r, fin_i = lax.fori_loop(0, n, lambda i, s: advance(s, tile_at(i)), (zero, zero), unroll=8)
        pr, pi = ar[0:1], ai[0:1]
        for _ in range(log_n):
            pr, pi = pr * pr - pi * pi, 2.0 * pr * pi
        er = ei = jnp.zeros((1, lanes), F32)
        for step in range(SCAN_SEG):
            j = SCAN_SEG - 1 - step if reverse else step
            c_scr[0, j:j + 1, :] = er
            c_scr[1, j:j + 1, :] = ei
            er, ei = pr * er - pi * ei + fin_r[j:j + 1], pr * ei + pi * er + fin_i[j:j + 1]
        entry = (c_scr[0], c_scr[1])

        if not reverse:
            cor_ref[...], coi_ref[...] = entry

            def step2(i, s):
                row = tile_at(i)
                s = advance(s, row)
                or_ref[pl.ds(row, SCAN_SEG), :] = s[0]
                oi_ref[pl.ds(row, SCAN_SEG), :] = s[1]
                return s
            lax.fori_loop(0, n, step2, entry, unroll=8)
        else:
            def emit(s, row, xr_prev, xi_prev, acc):
                s = advance(s, row)
                or_ref[pl.ds(row, SCAN_SEG), :] = s[0]
                oi_ref[pl.ds(row, SCAN_SEG), :] = s[1]
                return s, (acc[0] + s[0] * xr_prev + s[1] * xi_prev, acc[1] + s[1] * xr_prev - s[0] * xi_prev)

            def step2(i, carry):
                s, acc = carry
                row = tile_at(i)
                prev = pl.multiple_of(row - SCAN_SEG, SCAN_SEG)
                return emit(s, row, xr_ref[pl.ds(prev, SCAN_SEG), :], xi_ref[pl.ds(prev, SCAN_SEG), :], acc)
            s, acc = lax.fori_loop(0, n - 1, step2, (entry, (zero, zero)), unroll=8)
            s, acc = emit(s, 0, cr_ref[...], ci_ref[...], acc)
            dar_ref[...] = jnp.sum(acc[0], axis=0, keepdims=True)
            dai_ref[...] = jnp.sum(acc[1], axis=0, keepdims=True)

    planes = pl.BlockSpec((2, t_len, lanes), lambda b: (0, 0, b))
    tile8 = pl.BlockSpec((SCAN_SEG, lanes), lambda b: (0, b))
    entry_spec = pl.BlockSpec((2, SCAN_SEG, lanes), lambda b: (0, 0, b))
    row1 = pl.BlockSpec((1, lanes), lambda b: (0, b))
    seq = jax.ShapeDtypeStruct((2, t_len, S5_CH), F32)
    if reverse:
        in_specs = [planes, tile8, tile8, planes, entry_spec]
        args = (bu, a_re, a_im, states, carry_in)
        out_specs = [planes, row1, row1]
        out_shape = [seq, jax.ShapeDtypeStruct((1, S5_CH), F32), jax.ShapeDtypeStruct((1, S5_CH), F32)]
    else:
        in_specs = [planes, tile8, tile8]
        args = (bu, a_re, a_im)
        out_specs = [planes, entry_spec]
        out_shape = [seq, jax.ShapeDtypeStruct((2, SCAN_SEG, S5_CH), F32)]
    return pl.pallas_call(body, name=name, grid=(nblk,), in_specs=in_specs, out_specs=out_specs, out_shape=out_shape,
                          scratch_shapes=[pltpu.VMEM((2, SCAN_SEG, lanes), F32)],
                          compiler_params=_params(("parallel",)))(*args)


def _ret_consts():
    h = np.arange(RET_HEADS, dtype=np.float32)
    log_gamma = np.log1p(-np.exp2(-5.0 - h)).astype(np.float32)
    idx = np.arange(BLOCK, dtype=np.float32)
    diff = idx[:, None] - idx[None, :]
    decay = np.where(diff >= 0, np.exp(log_gamma[:, None, None] * np.maximum(diff, 0.0)), 0.0).astype(np.float32)
    k_w = np.exp(log_gamma[:, None] * (BLOCK - 1 - idx)[None, :]).astype(np.float32)[:, :, None]
    q_w = np.exp(log_gamma[:, None] * (idx + 1.0)[None, :]).astype(np.float32)[:, :, None]
    chunk_decay = [float(v) for v in np.exp(log_gamma * BLOCK).astype(np.float32)]
    return jnp.asarray(decay), jnp.asarray(q_w), jnp.asarray(k_w), chunk_decay


def _ret_head(q, k, v, g, prev, decay, q_w, k_w, chunk_decay):
    inner_s = _dnt(q, k) * decay
    y = _dnn(inner_s, v) + _dnn(q * q_w, prev)
    mu = jnp.mean(y, axis=-1, keepdims=True)
    var = jnp.mean(jnp.square(y - mu), axis=-1, keepdims=True)
    yn = (y - mu) * lax.rsqrt(var + EPS)
    out = g * jax.nn.sigmoid(g) * yn
    return out, prev * chunk_decay + _dtn(k * k_w, v)


def _ret_prep(proj, cos, sin):
    def fn(rq, rk, c, s):
        return _rope(rq, c, s, 32), _rope(rk, c, s, 32) * (RET_DK ** -0.5)
    t_len = proj.shape[0]
    return _rowwise("ret_prep", fn, [_seg(proj, "rq"), _seg(proj, "rk"), cos, sin], [],
                    [((t_len, 256), F32), ((t_len, 256), F32)])


def _ret_unprep(dq, dk, cos, sin):
    def fn(dq_t, dk_t, c, s):
        return _rope(dq_t, c, -s, 32), _rope(dk_t * (RET_DK ** -0.5), c, -s, 32)
    t_len = dq.shape[0]
    return _rowwise("ret_unprep", fn, [dq, dk, cos, sin], [], [((t_len, 256), BF16), ((t_len, 256), BF16)])


def _ret_fwd(q, k, proj):
    t_len = q.shape[0]
    nc = t_len // BLOCK
    decay, q_w, k_w, chunk_decay = _ret_consts()

    def body(q_ref, k_ref, v_ref, g_ref, dec_ref, qw_ref, kw_ref, o_ref, prev_ref, state):
        @pl.when(pl.program_id(0) == 0)
        def _():
            state[...] = jnp.zeros_like(state)
        for h in range(RET_HEADS):
            qk, vv = slice(64 * h, 64 * h + 64), slice(128 * h, 128 * h + 128)
            prev = state[h]
            prev_ref[h] = prev
            out, nxt = _ret_head(q_ref[:, qk], k_ref[:, qk], v_ref[:, vv], g_ref[:, vv], prev,
                                 dec_ref[h], qw_ref[h], kw_ref[h], chunk_decay[h])
            o_ref[:, vv] = out.astype(o_ref.dtype)
            state[h] = nxt

    chunk = lambda w, c=0: pl.BlockSpec((BLOCK, w), lambda i, c=c: (i, c))
    return pl.pallas_call(
        body, name="ret_fwd", grid=(nc,),
        in_specs=[chunk(256), chunk(256), chunk(512, PROJ["rv"][0] // 512), chunk(512, PROJ["rg"][0] // 512),
                  _const_spec(decay.shape), _const_spec(q_w.shape), _const_spec(k_w.shape)],
        out_specs=[chunk(512), pl.BlockSpec((None, RET_HEADS, RET_DK, RET_DV), lambda i: (i, 0, 0, 0))],
        out_shape=[jax.ShapeDtypeStruct((t_len, 512), BF16), jax.ShapeDtypeStruct((nc, RET_HEADS, RET_DK, RET_DV), F32)],
        scratch_shapes=[pltpu.VMEM((RET_HEADS, RET_DK, RET_DV), F32)],
        compiler_params=_params(("arbitrary",)))(q, k, proj, proj, decay, q_w, k_w)


def _ret_bwd(q, k, proj, prevs, dy):
    t_len = q.shape[0]
    nc = t_len // BLOCK
    decay, q_w, k_w, chunk_decay = _ret_consts()

    def body(q_ref, k_ref, v_ref, g_ref, prev_ref, dy_ref, dec_ref, qw_ref, kw_ref, dq_ref, dk_ref, dv_ref, dg_ref, dstate):
        @pl.when(pl.program_id(0) == 0)
        def _():
            dstate[...] = jnp.zeros_like(dstate)
        for h in range(RET_HEADS):
            qk, vv = slice(64 * h, 64 * h + 64), slice(128 * h, 128 * h + 128)
            head = functools.partial(_ret_head, decay=dec_ref[h], q_w=qw_ref[h], k_w=kw_ref[h], chunk_decay=chunk_decay[h])
            _, vjp = jax.vjp(head, q_ref[:, qk], k_ref[:, qk], v_ref[:, vv], g_ref[:, vv], prev_ref[h])
            dq, dk, dv, dg, dprev = vjp((dy_ref[:, vv].astype(F32), dstate[h]))
            dq_ref[:, qk], dk_ref[:, qk] = dq, dk
            dv_ref[:, vv], dg_ref[:, vv] = dv.astype(dv_ref.dtype), dg.astype(dg_ref.dtype)
            dstate[h] = dprev

    chunk = lambda w, c=0: pl.BlockSpec((BLOCK, w), lambda i, c=c: (nc - 1 - i, c))
    return pl.pallas_call(
        body, name="ret_bwd", grid=(nc,),
        in_specs=[chunk(256), chunk(256), chunk(512, PROJ["rv"][0] // 512), chunk(512, PROJ["rg"][0] // 512),
                  pl.BlockSpec((None, RET_HEADS, RET_DK, RET_DV), lambda i: (nc - 1 - i, 0, 0, 0)), chunk(512),
                  _const_spec(decay.shape), _const_spec(q_w.shape), _const_spec(k_w.shape)],
        out_specs=[chunk(256), chunk(256), chunk(512), chunk(512)],
        out_shape=[jax.ShapeDtypeStruct((t_len, 256), F32), jax.ShapeDtypeStruct((t_len, 256), F32),
                   jax.ShapeDtypeStruct((t_len, 512), BF16), jax.ShapeDtypeStruct((t_len, 512), BF16)],
        scratch_shapes=[pltpu.VMEM((RET_HEADS, RET_DK, RET_DV), F32)],
        compiler_params=_params(("arbitrary",)))(q, k, proj, proj, prevs, dy, decay, q_w, k_w)


MLA_SCALE = 96 ** -0.5


def _mla_prep(qf, kvf, proj, cos, sin):
    def fn(q, kv, kr, c, s):
        q_rope = _rope(q[:, 512:768], c, s, 16)
        k_rope = _rope(kr, c, s, 16)[:, :32]
        zero = jnp.zeros_like(k_rope)
        qs = [jnp.concatenate([q[:, 64 * h:64 * h + 64], q_rope[:, 32 * h:32 * h + 32], zero], axis=1) for h in range(MLA_HEADS)]
        ks = [jnp.concatenate([kv[:, 64 * h:64 * h + 64], k_rope, zero], axis=1) for h in range(MLA_HEADS)]
        ones = (lax.broadcasted_iota(jnp.int32, (kv.shape[0], 64), 1) == 0).astype(F32)
        vs = [jnp.concatenate([kv[:, 512 + 64 * h:576 + 64 * h], ones], axis=1) for h in range(MLA_HEADS)]
        return qs, ks, vs
    t_len = qf.shape[0]
    return _rowwise("mla_prep", fn, [qf, kvf, _seg(proj, "kr"), cos, sin], [],
                    [((8, t_len, 128), BF16), ((8, t_len, 128), BF16), ((8, t_len, 128), BF16)])


def _mla_unprep(dq, dk, dv, cos, sin):
    def fn(dq_t, dk_t, dv_t, c, s):
        q_rope = _rope(jnp.concatenate([dq_t[h][:, 64:96] for h in range(MLA_HEADS)], axis=1), c, -s, 16)
        d_qf = jnp.concatenate([dq_t[h][:, :64] for h in range(MLA_HEADS)] + [q_rope], axis=1)
        d_kvf = jnp.concatenate([dk_t[h][:, :64] for h in range(MLA_HEADS)] + [dv_t[h][:, :64] for h in range(MLA_HEADS)], axis=1)
        k_rope = dk_t[0][:, 64:96]
        for h in range(1, MLA_HEADS):
            k_rope = k_rope + dk_t[h][:, 64:96]
        d_kr = _rope(jnp.concatenate([k_rope, jnp.zeros((k_rope.shape[0], 96), F32)], axis=1), c, -s, 16)
        return d_qf, d_kvf, d_kr
    t_len = dq.shape[1]
    return _rowwise("mla_unprep", fn, [dq, dk, dv, cos, sin], [],
                    [((t_len, 768), BF16), ((t_len, 1024), BF16), ((t_len, 128), BF16)])


MLA_EXP2 = MLA_SCALE * math.log2(math.e)
MLA_TILE = 1024


def _diag_mask(tq):
    return lax.broadcasted_iota(jnp.int32, (tq, tq), 1) <= lax.broadcasted_iota(jnp.int32, (tq, tq), 0)


def _tri_steps(nq, key_major):
    if key_major:
        pairs = [(i, j) for j in range(nq) for i in range(j, nq)]
    else:
        pairs = [(i, j) for i in range(nq) for j in range(i + 1)]
    return jnp.asarray([p[0] for p in pairs], jnp.int32), jnp.asarray([p[1] for p in pairs], jnp.int32)


def _ride(body, rider, counts, grid):
    if rider is None:
        return body
    extra = (0, len(rider.inputs), len(rider.out_shape), len(rider.sems))

    def wrapped(*refs):
        host, ridden, pos = [], [], 0
        for mine, theirs in zip(counts, extra):
            host += refs[pos:pos + mine]
            ridden.append(refs[pos + mine:pos + mine + theirs])
            pos += mine + theirs
        ids = [pl.program_id(a) for a in range(len(grid))]
        first, last = ids[0] == 0, ids[0] == grid[0] - 1
        for a in range(1, len(grid)):
            first, last = first & (ids[a] == 0), last & (ids[a] == grid[a] - 1)

        @pl.when(first)
        def _():
            rider.start(*ridden[1:])

        body(*host)

        @pl.when(last)
        def _():
            rider.finish(*ridden[1:])

    return wrapped


def _mla_fwd(q, k, v, rider=None):
    t_len = q.shape[1]
    tq = min(MLA_TILE, t_len)
    nq = t_len // tq
    qi, kj = _tri_steps(nq, False)

    def body(i_ref, j_ref, q_ref, k_ref, v_ref, o_ref, lse_ref, m_s, acc_s):
        step = pl.program_id(1)
        i, j = i_ref[step], j_ref[step]

        @pl.when(j == 0)
        def _():
            m_s[...] = jnp.full_like(m_s, NEG_INF)
            acc_s[...] = jnp.zeros_like(acc_s)

        def update(mask):
            for e in range(2):
                s = _raw_dot(q_ref[e], k_ref[e], NT)
                if mask is not None:
                    s = jnp.where(mask, s, NEG_INF)
                m_old = m_s[e]
                m_new = jnp.maximum(m_old, jnp.max(s, axis=-1, keepdims=True))
                p = jnp.exp2((s - m_new) * MLA_EXP2)
                acc_s[e] = jnp.exp2((m_old - m_new) * MLA_EXP2) * acc_s[e] + _raw_dot(p, v_ref[e], NN)
                m_s[e] = m_new

        @pl.when(j < i)
        def _():
            update(None)

        @pl.when(j == i)
        def _():
            update(_diag_mask(tq))
            outs = []
            for e in range(2):
                acc = acc_s[e]
                denom = acc[:, 64:65]
                outs.append(acc[:, :64] / denom)
                lse_ref[e] = m_s[e] * MLA_EXP2 + jnp.log2(denom)
            o_ref[...] = jnp.concatenate(outs, axis=1)

    grid = (MLA_HEADS // 2, int(qi.shape[0]))
    ride_in, ride_out, ride_sems = ([], [], []) if rider is None else (rider.inputs, rider.out_shape, rider.sems)
    grid_spec = pltpu.PrefetchScalarGridSpec(
        num_scalar_prefetch=2, grid=grid,
        in_specs=[pl.BlockSpec((2, tq, 128), lambda hp, s, qi, kj: (hp, qi[s], 0)),
                  pl.BlockSpec((2, tq, 128), lambda hp, s, qi, kj: (hp, kj[s], 0)),
                  pl.BlockSpec((2, tq, 128), lambda hp, s, qi, kj: (hp, kj[s], 0))] + [ANY_SPEC] * len(ride_in),
        out_specs=[pl.BlockSpec((tq, 128), lambda hp, s, qi, kj: (qi[s], hp)),
                   pl.BlockSpec((2, tq, 1), lambda hp, s, qi, kj: (hp, qi[s], 0))] + [ANY_SPEC] * len(ride_out),
        scratch_shapes=[pltpu.VMEM((2, tq, 1), F32), pltpu.VMEM((2, tq, 128), F32)] + ride_sems)
    res = pl.pallas_call(
        _ride(body, rider, (2, 3, 2, 2), grid), name="mla_fwd" if rider is None else "mla_fwd_gather", grid_spec=grid_spec,
        out_shape=[jax.ShapeDtypeStruct((t_len, 512), F32), jax.ShapeDtypeStruct((MLA_HEADS, t_len, 1), F32)] + ride_out,
        compiler_params=_params(("parallel" if rider is None else "arbitrary", "arbitrary")))(qi, kj, q, k, v, *ride_in)
    return res[0], res[1], res[2:]


def _mla_bwd_prep(o, dy):
    def fn(o_t, dy_t):
        dos, deltas = [], []
        for h in range(MLA_HEADS):
            d_h = dy_t[:, 64 * h:64 * h + 64].astype(F32)
            dos.append(jnp.concatenate([d_h, jnp.zeros_like(d_h)], axis=1))
            deltas.append(jnp.sum(d_h * o_t[:, 64 * h:64 * h + 64], axis=-1, keepdims=True))
        return dos, deltas
    t_len = o.shape[0]
    return _rowwise("mla_bwd_prep", fn, [o, dy], [], [((8, t_len, 128), BF16), ((8, t_len, 1), F32)])


def _mla_bwd(q, k, v, do, lse2, delta, rider=None):
    t_len = q.shape[1]
    tq = min(MLA_TILE, t_len)
    nq = t_len // tq
    qi, kj = _tri_steps(nq, True)

    def body(i_ref, j_ref, q_ref, k_ref, v_ref, do_ref, lse_ref, dl_ref, dq_ref, dk_ref, dv_ref, k_acc, v_acc):
        step = pl.program_id(1)
        i, j = i_ref[step], j_ref[step]

        @pl.when(step == 0)
        def _():
            dq_ref[...] = jnp.zeros_like(dq_ref)

        @pl.when(i == j)
        def _():
            k_acc[...] = jnp.zeros_like(k_acc)
            v_acc[...] = jnp.zeros_like(v_acc)

        def tile(mask):
            p = jnp.exp2(_raw_dot(q_ref[...], k_ref[...], NT) * MLA_EXP2 - lse_ref[...])
            if mask is not None:
                p = jnp.where(mask, p, 0.0)
            ds = p * (_raw_dot(do_ref[...], v_ref[...], NT) - dl_ref[...]) * MLA_SCALE
            v_acc[...] += _raw_dot(p, do_ref[...], TN)
            k_acc[...] += _raw_dot(ds, q_ref[...], TN)
            rows = pl.ds(pl.multiple_of(i * tq, tq), tq)
            dq_ref[rows, :] += _raw_dot(ds, k_ref[...], NN)

        @pl.when(i == j)
        def _():
            tile(_diag_mask(tq))

        @pl.when(i > j)
        def _():
            tile(None)

        @pl.when(i == nq - 1)
        def _():
            dk_ref[...] = k_acc[...]
            dv_ref[...] = v_acc[...]

    q_blk = lambda w: pl.BlockSpec((None, tq, w), lambda h, s, qi, kj: (h, qi[s], 0))
    k_blk = lambda w: pl.BlockSpec((None, tq, w), lambda h, s, qi, kj: (h, kj[s], 0))
    grid = (MLA_HEADS, int(qi.shape[0]))
    ride_in, ride_out, ride_sems = ([], [], []) if rider is None else (rider.inputs, rider.out_shape, rider.sems)
    grid_spec = pltpu.PrefetchScalarGridSpec(
        num_scalar_prefetch=2, grid=grid,
        in_specs=[q_blk(128), k_blk(128), k_blk(128), q_blk(128), q_blk(1), q_blk(1)] + [ANY_SPEC] * len(ride_in),
        out_specs=[pl.BlockSpec((None, t_len, 128), lambda h, s, qi, kj: (h, 0, 0)), k_blk(128), k_blk(128)]
        + [ANY_SPEC] * len(ride_out),
        scratch_shapes=[pltpu.VMEM((tq, 128), F32), pltpu.VMEM((tq, 128), F32)] + ride_sems)
    res = pl.pallas_call(
        _ride(body, rider, (2, 6, 3, 2), grid), name="mla_bwd" if rider is None else "mla_bwd_scatter", grid_spec=grid_spec,
        out_shape=[jax.ShapeDtypeStruct((MLA_HEADS, t_len, 128), F32)] * 3 + ride_out,
        compiler_params=_params(("parallel" if rider is None else "arbitrary", "arbitrary")))(qi, kj, q, k, v, do, lse2, delta, *ride_in)
    return res[0], res[1], res[2], res[3:]


MERGE_TN = 256
GATE_BLOCK0 = PROJ["gates"][0] // MERGE_TN


def _merge_specs(tm):
    y_spec = pl.BlockSpec((tm, 512), lambda i, j: (i, 0))
    w_spec = pl.BlockSpec((512, MERGE_TN), lambda i, j: (0, j))
    gate = lambda b: pl.BlockSpec((tm, MERGE_TN), lambda i, j, b=b: (i, GATE_BLOCK0 + 4 * b + j))
    return [y_spec] * 4 + [w_spec] * 4 + [gate(b) for b in range(4)]


def _merge_fwd(ys, wbs, proj):
    t_len = proj.shape[0]
    tm = min(512, t_len)

    def body(*refs):
        y, w, gl, o_ref = refs[0:4], refs[4:8], refs[8:12], refs[12]
        acc = jnp.zeros((tm, MERGE_TN), F32)
        for b in range(4):
            acc += jax.nn.sigmoid(gl[b][...]) * _raw_dot(y[b][...], w[b][...], NN)
        o_ref[...] = acc.astype(o_ref.dtype)

    return pl.pallas_call(body, name="merge_fwd", grid=(t_len // tm, D_MODEL // MERGE_TN), in_specs=_merge_specs(tm),
                          out_specs=pl.BlockSpec((tm, MERGE_TN), lambda i, j: (i, j)),
                          out_shape=jax.ShapeDtypeStruct((t_len, D_MODEL), BF16),
                          compiler_params=_params(("parallel", "parallel")))(*ys, *wbs, *([proj] * 4))


def _merge_bwd(ys, wbs, proj, dmerged):
    t_len = proj.shape[0]
    tm = min(512, t_len)

    def body(*refs):
        y, w, gl, dm_ref, dgl, dt = refs[0:4], refs[4:8], refs[8:12], refs[12], refs[13:17], refs[17:21]
        dm = dm_ref[...]
        for b in range(4):
            gate = jax.nn.sigmoid(gl[b][...])
            t_b = _raw_dot(y[b][...], w[b][...], NN)
            dgl[b][...] = (dm * t_b * gate * (1.0 - gate)).astype(BF16)
            dt[b][...] = (dm * gate).astype(BF16)

    o_spec = pl.BlockSpec((tm, MERGE_TN), lambda i, j: (i, j))
    return pl.pallas_call(body, name="merge_bwd", grid=(t_len // tm, D_MODEL // MERGE_TN), in_specs=_merge_specs(tm) + [o_spec],
                          out_specs=[o_spec] * 8, out_shape=[jax.ShapeDtypeStruct((t_len, D_MODEL), BF16)] * 8,
                          compiler_params=_params(("parallel", "parallel")))(*ys, *wbs, *([proj] * 4), dmerged)


def _adamw(name, w, g, m, v, slabs):
    shape = w.shape
    size = int(np.prod(shape))
    if shape[-1] >= 96:
        view = (size // shape[-1], shape[-1])
    elif size % 128 == 0:
        view = (size // 128, 128)
    else:
        view = (1, size)
    rows = view[0]
    tile = rows
    for cand in (512, 256, 128, 64):
        if rows > cand and rows % cand == 0 and cand * view[1] * 4 <= 2 ** 20:
            tile = cand
            break
    if rows * view[1] * 4 <= 2 ** 20:
        tile = rows
    c1, c2 = 1.0 - ADAM_B1 ** ADAM_STEP, 1.0 - ADAM_B2 ** ADAM_STEP

    def body(w_ref, g_ref, m_ref, v_ref, og_ref, od_ref, om_ref, ov_ref):
        if slabs:
            grad = g_ref[0].astype(F32)
            for d in range(1, slabs):
                grad = grad + g_ref[d].astype(F32)
        else:
            grad = g_ref[...]
        m_new = ADAM_B1 * m_ref[...] + (1.0 - ADAM_B1) * grad
        v_new = ADAM_B2 * v_ref[...] + (1.0 - ADAM_B2) * jnp.square(grad)
        og_ref[...] = grad
        od_ref[...] = -ADAM_LR * ((m_new / c1) / (jnp.sqrt(v_new / c2) + ADAM_EPS) + ADAM_WD * w_ref[...])
        om_ref[...] = m_new
        ov_ref[...] = v_new

    spec = pl.BlockSpec((tile, view[1]), lambda i: (i, 0))
    g_spec = pl.BlockSpec((slabs, tile, view[1]), lambda i: (0, i, 0)) if slabs else spec
    g_view = g.reshape((slabs,) + view) if slabs else g.reshape(view)
    outs = pl.pallas_call(body, name=name, grid=(rows // tile,), in_specs=[spec, g_spec, spec, spec], out_specs=[spec] * 4,
                          out_shape=[jax.ShapeDtypeStruct(view, F32)] * 4,
                          compiler_params=_params(("parallel",)))(w.reshape(view), g_view, m.reshape(view), v.reshape(view))
    return [o.reshape(shape) for o in outs]


def _sum_slabs(name, x):
    def body(x_ref, o_ref):
        acc = x_ref[0]
        for d in range(1, N_DEV):
            acc = acc + x_ref[d]
        o_ref[...] = acc
    return pl.pallas_call(body, name=name, out_shape=jax.ShapeDtypeStruct(x.shape[1:], x.dtype))(x)


MESH_ID = pl.DeviceIdType.MESH
ANY_SPEC = pl.BlockSpec(memory_space=pl.ANY)


def _remote(src, dst, send_sems, recv_sems, k, to):
    return pltpu.make_async_remote_copy(src_ref=src, dst_ref=dst, send_sem=send_sems.at[k], recv_sem=recv_sems.at[k],
                                        device_id=to, device_id_type=MESH_ID)


class _Rider:
    def __init__(self, inputs, out_shape, sems, start, finish):
        self.inputs, self.out_shape, self.sems, self.start, self.finish = list(inputs), list(out_shape), list(sems), start, finish


def _run_rider(name, rider):
    n_in, n_out = len(rider.inputs), len(rider.out_shape)

    def body(*refs):
        parts = refs[:n_in], refs[n_in:n_in + n_out], refs[n_in + n_out:]
        rider.start(*parts)
        rider.finish(*parts)

    return pl.pallas_call(body, name=name, in_specs=[ANY_SPEC] * n_in, out_specs=[ANY_SPEC] * n_out, out_shape=rider.out_shape,
                          scratch_shapes=rider.sems, compiler_params=pltpu.CompilerParams(has_side_effects=True))(*rider.inputs)


def _gather_rider(blocks):
    n = len(blocks)

    def plan(srcs, dsts, sems, waits):
        send_sems, recv_sems, local_sems = sems
        x, y, c = lax.axis_index("x"), lax.axis_index("y"), lax.axis_index("c")
        sibling = (x, y, 1 - c)
        chips = [(1 - x, y), (x, 1 - y), (1 - x, 1 - y)]
        index = lambda px, py, pc: 4 * px + 2 * py + pc

        def copy(t, k, block, to, src=None):
            slot = dsts[t].at[block]
            return _remote(slot if src is None else src, slot, send_sems, recv_sems, 7 * t + k, to)

        me = index(x, y, c)
        own = [pltpu.make_async_copy(srcs[t], dsts[t].at[me], local_sems.at[t]) for t in range(n)]
        first = []
        for t in range(n):
            first.append(copy(t, 0, me, sibling, src=srcs[t]))
            first += [copy(t, 1 + j, me, (*chip, c), src=srcs[t]) for j, chip in enumerate(chips)]
        if not waits:
            return own, first, [], []
        relays = [(copy(t, 1 + j, index(*chip, c), sibling), copy(t, 4 + j, index(*chip, c), sibling))
                  for j, chip in enumerate(chips) for t in range(n)]
        from_sibling = [copy(t, 0, index(x, y, 1 - c), sibling) for t in range(n)]
        from_sibling += [copy(t, 4 + j, index(*chip, 1 - c), sibling) for t in range(n) for j, chip in enumerate(chips)]
        return own, first, relays, from_sibling

    def start(srcs, dsts, sems):
        own, first, _, _ = plan(srcs, dsts, sems, False)
        for cp in own + first:
            cp.start()

    def finish(srcs, dsts, sems):
        own, first, relays, from_sibling = plan(srcs, dsts, sems, True)
        for arrival, forward in relays:
            arrival.wait_recv()
            forward.start()
        for cp in from_sibling:
            cp.wait_recv()
        for cp in first + [forward for _, forward in relays]:
            cp.wait_send()
        for cp in own:
            cp.wait()

    return _Rider(blocks, [jax.ShapeDtypeStruct((N_DEV,) + b.shape, b.dtype) for b in blocks],
                  [pltpu.SemaphoreType.DMA((7 * n,)), pltpu.SemaphoreType.DMA((7 * n,)), pltpu.SemaphoreType.DMA((n,))],
                  start, finish)


def _pair_exchange(name, blocks):
    n = len(blocks)

    def body(*refs):
        srcs, dsts = refs[:n], refs[n:2 * n]
        send_sems, recv_sems = refs[2 * n:]
        sibling = (lax.axis_index("x"), lax.axis_index("y"), 1 - lax.axis_index("c"))
        copies = [_remote(srcs[t], dsts[t], send_sems, recv_sems, t, sibling) for t in range(n)]
        for cp in copies:
            cp.start()
        for cp in copies:
            cp.wait()

    return pl.pallas_call(
        body, name=name, in_specs=[ANY_SPEC] * n, out_specs=[ANY_SPEC] * n,
        out_shape=[jax.ShapeDtypeStruct(b.shape, b.dtype) for b in blocks],
        scratch_shapes=[pltpu.SemaphoreType.DMA((n,)), pltpu.SemaphoreType.DMA((n,))],
        compiler_params=pltpu.CompilerParams(has_side_effects=True))(*blocks)


def _chip_rider(items):
    n = len(items)
    out_shape = [jax.ShapeDtypeStruct((4,) + tuple(a.shape[1:] if per_chip else a.shape), a.dtype) for a, per_chip in items]

    def plan(srcs, dsts, sems, waits):
        send_sems, recv_sems, local_sems = sems
        x, y, c = lax.axis_index("x"), lax.axis_index("y"), lax.axis_index("c")
        mine = 2 * x + y
        own = [pltpu.make_async_copy(srcs[t].at[mine] if per_chip else srcs[t], dsts[t].at[mine], local_sems.at[t])
               for t, (_, per_chip) in enumerate(items)]
        sends, arrivals = [], []
        for k in range(1, 4):
            px, py = x ^ (k >> 1), y ^ (k & 1)
            peer = 2 * px + py
            for t, (_, per_chip) in enumerate(items):
                src = srcs[t].at[peer] if per_chip else srcs[t]
                sends.append(_remote(src, dsts[t].at[mine], send_sems, recv_sems, 3 * t + k - 1, (px, py, c)))
                if waits:
                    arrivals.append(_remote(src, dsts[t].at[peer], send_sems, recv_sems, 3 * t + k - 1, (px, py, c)))
        return own, sends, arrivals

    def start(srcs, dsts, sems):
        own, sends, _ = plan(srcs, dsts, sems, False)
        for cp in own + sends:
            cp.start()

    def finish(srcs, dsts, sems):
        own, sends, arrivals = plan(srcs, dsts, sems, True)
        for out, arrival in zip(sends, arrivals):
            out.wait_send()
            arrival.wait_recv()
        for cp in own:
            cp.wait()

    return _Rider([a for a, _ in items], out_shape,
                  [pltpu.SemaphoreType.DMA((3 * n,)), pltpu.SemaphoreType.DMA((3 * n,)), pltpu.SemaphoreType.DMA((n,))],
                  start, finish)


def _band_diag(p):
    a, b = p.shape[1:]
    eye = jnp.eye(8, dtype=p.dtype)
    return (p.reshape(4, 8, a, 1, b) * eye[None, :, None, :, None]).reshape(4, 8 * a, 8 * b)


def _band_diag_extract(m, a, b):
    eye = jnp.eye(8, dtype=m.dtype)
    return (m.reshape(4, 8, a, 8, b) * eye[None, :, None, :, None]).sum(axis=3).reshape(32, a, b)


def _proj_cols(w_ref_layout):
    parts = []
    for name in PROJ_ORDER:
        off, width = REF_SPLIT[name]
        part = w_ref_layout[:, off:off + width]
        if width < PROJ[name][1]:
            part = jnp.pad(part, ((0, 0), (0, PROJ[name][1] - width)))
        parts.append(part)
    return jnp.concatenate(parts, axis=1)


def _unproj_cols(w_proj_layout):
    parts = []
    for name in REF_ORDER:
        parts.append(w_proj_layout[:, PROJ[name][0]:PROJ[name][0] + REF_SPLIT[name][1]])
    return jnp.concatenate(parts, axis=1)


def _uq_cols(w):
    w3 = w.reshape(w.shape[0], MLA_HEADS, 96)
    return jnp.concatenate([w3[:, :, :64].reshape(-1, 512), w3[:, :, 64:].reshape(-1, 256)], axis=1)


def _un_uq_cols(w):
    return jnp.concatenate([w[:, :512].reshape(-1, MLA_HEADS, 64), w[:, 512:].reshape(-1, MLA_HEADS, 32)], axis=2).reshape(-1, 768)


def _ukv_cols(w):
    w3 = w.reshape(w.shape[0], MLA_HEADS, 128)
    return jnp.concatenate([w3[:, :, :64].reshape(-1, 512), w3[:, :, 64:].reshape(-1, 512)], axis=1)


def _un_ukv_cols(w):
    return jnp.concatenate([w[:, :512].reshape(-1, MLA_HEADS, 64), w[:, 512:].reshape(-1, MLA_HEADS, 64)], axis=2).reshape(-1, 1024)


def _cols_gathered(g):
    return g.transpose(1, 0, 2).reshape(g.shape[1], -1)


def _cols_slabs(w):
    r, c = w.shape
    return w.reshape(r, N_DEV, c // N_DEV).transpose(1, 0, 2)


def _row(v):
    return v.reshape(1, -1)


def _s5_weights(p):
    flat = dict(lam_re=p["s5_lam_re"].reshape(S5_CH, 1), lam_im=p["s5_lam_im"].reshape(S5_CH, 1),
                log_dt=jnp.repeat(p["s5_log_dt"], S5_STATE).reshape(S5_CH, 1),
                b_re=p["s5_b_re"].reshape(S5_CH, S5_GROUP), b_im=p["s5_b_im"].reshape(S5_CH, S5_GROUP))
    ab_re, ab_im, bb_re, bb_im = _s5_param_fwd(flat["lam_re"], flat["lam_im"], flat["log_dt"], flat["b_re"], flat["b_im"])
    w_bu = jnp.concatenate([_band_diag(bb.reshape(32, 64, 16).transpose(0, 2, 1)) for bb in (bb_re, bb_im)], axis=0)
    w_y = jnp.concatenate([_band_diag(p["s5_c_re"].transpose(0, 2, 1)), -_band_diag(p["s5_c_im"].transpose(0, 2, 1))], axis=0)
    return dict(flat=flat,
                a_re=jnp.broadcast_to(ab_re.reshape(1, S5_CH), (SCAN_SEG, S5_CH)),
                a_im=jnp.broadcast_to(ab_im.reshape(1, S5_CH), (SCAN_SEG, S5_CH)),
                w_bu=w_bu.astype(BF16), w_bu_t=w_bu.swapaxes(1, 2).astype(BF16),
                w_y=w_y.astype(BF16), w_y_t=w_y.swapaxes(1, 2).astype(BF16))


def _layer_fwd(x, w, p, tabs, late_shards=None, rider=None):
    t_len = x.shape[0]
    cos64, sin64, cos32, sin32 = tabs
    s = {}
    (s["h1"],) = _rowwise("pre_mix", lambda xt, g: (_rms(xt, g),), [x], [_row(p["g_pre_mix"])], [((t_len, D_MODEL), BF16)])
    if late_shards is None:
        proj = _mm("proj", s["h1"], w["w_in"], "nn")
    else:
        proj, late = _mm("proj_gather", s["h1"], w["w_in"], "nn", rider=_gather_rider(late_shards))
        w.update(_late_weights(late))
    s["proj"] = proj
    s["qa"], s["ka"], s["va"] = _swa_prep(proj, cos64, sin64)
    s["sinks"] = jnp.broadcast_to(p["swa_sinks"][:, None], (8, 128))
    y_a = _swa_fwd(s["qa"], s["ka"], s["va"], s["sinks"])
    s5 = s["s5"] = _s5_weights(p)
    bu, s["u_p"] = _s5_expand("s5_bu", proj, PROJ["su"][0] // 128, s5["w_bu"])
    s["xs"], s["cin"] = _s5_scan("s5_scan_fwd", bu, s5["a_re"], s5["a_im"])
    s["y_s"] = _s5_contract("s5_y", s["xs"], s5["w_y"])
    (s["z"],) = _rowwise("s5_gelu", lambda y, u, d: (jax.nn.gelu(y + d * u),), [s["y_s"], _seg(proj, "su")],
                         [_row(p["s5_d"])], [((t_len, 512), BF16)])
    s["zz"] = _mm("s5_glu_mm", s["z"], w["w_glu"], "nn")
    (y_b,) = _rowwise("s5_glu", lambda t: (t[:, :512] * jax.nn.sigmoid(t[:, 512:]),), [s["zz"]], [], [((t_len, 512), BF16)])
    s["qc"], s["kc"] = _ret_prep(proj, cos64, sin64)
    y_c, s["prevs"] = _ret_fwd(s["qc"], s["kc"], proj)
    s["cqn"], s["ckvn"] = _rowwise("mla_norm", lambda cq, ckv, gq, gkv: (_rms(cq, gq), _rms(ckv, gkv)),
                                   [_seg(proj, "cq"), _seg(proj, "ckv")], [_row(p["mla_g_q"]), _row(p["mla_g_kv"])],
                                   [((t_len, 256), BF16), ((t_len, 128), BF16)])
    qf = _mm("mla_uq", s["cqn"], w["w_uq"], "nn")
    kvf = _mm("mla_ukv", s["ckvn"], w["w_ukv"], "nn")
    s["Q"], s["K"], s["V"] = _mla_prep(qf, kvf, proj, cos32, sin32)
    y_d, s["lse"], ridden = _mla_fwd(s["Q"], s["K"], s["V"], rider)
    s["ys"] = [y_a, y_b, y_c, y_d]
    s["merged"] = _merge_fwd(s["ys"], w["wb"], proj)
    s["o"] = _mm("out_mm", s["merged"], w["w_out"], "nn")

    def post_mix(xt, ot, g1, g2):
        x1 = xt + _rms(ot, g1)
        return x1, _rms(x1, g2)
    s["x1"], s["h2"] = _rowwise("post_mix", post_mix, [x, s["o"]], [_row(p["g_post_mix"]), _row(p["g_pre_mlp"])],
                                [((t_len, D_MODEL), F32), ((t_len, D_MODEL), BF16)])
    s["a"], s["r"] = _mm("ff1", s["h2"], w["w_ff1"], "nn", out_dtypes=(F32, BF16),
                         epi=lambda acc: (acc, jnp.square(jnp.maximum(acc, 0.0))))
    s["f"] = _mm("ff2", s["r"], w["w_ff2"], "nn")
    (x2,) = _rowwise("post_mlp", lambda x1, f, g: (x1 + _rms(f, g),), [s["x1"], s["f"]], [_row(p["g_post_mlp"])],
                     [((t_len, D_MODEL), F32)])
    return x2, s, ridden


def _layer_bwd(x, dx2, w, p, s, tabs, pending=None, exchange_name=None):
    t_len = x.shape[0]
    cos64, sin64, cos32, sin32 = tabs
    proj = s["proj"]
    big, small = {}, {}

    def post_mlp_bwd(f, d, g):
        df, dg = _rms_bwd(f, g, d)
        return (df,), (dg,)
    df, small["g_post_mlp"] = _rowwise("post_mlp_bwd", post_mlp_bwd, [s["f"], dx2], [_row(p["g_post_mlp"])],
                                       [((t_len, D_MODEL), BF16)], [((1, D_MODEL), F32)])
    da = _mm("ff2_dx", df, w["w_ff2"], "nt", out_dtypes=(BF16,), epi=lambda acc, a: (acc * 2.0 * jnp.maximum(a, 0.0),),
             epi_in=(s["a"],))
    big["w_ff2"] = _mm("ff2_dw", s["r"], df, "tn")
    dh2 = _mm("ff1_dx", da, w["w_ff1"], "nt")
    big["w_ff1"] = _mm("ff1_dw", s["h2"], da, "tn")

    def post_mix_bwd(x1, o, dh, d2, g1, g2):
        dx1_n, dg2 = _rms_bwd(x1, g2, dh)
        dx1 = d2 + dx1_n
        do, dg1 = _rms_bwd(o, g1, dx1)
        return (dx1, do), (dg1, dg2)
    dx1, do, small["g_post_mix"], small["g_pre_mlp"] = _rowwise(
        "post_mix_bwd", post_mix_bwd, [s["x1"], s["o"], dh2, dx2], [_row(p["g_post_mix"]), _row(p["g_pre_mlp"])],
        [((t_len, D_MODEL), F32), ((t_len, D_MODEL), BF16)], [((1, D_MODEL), F32), ((1, D_MODEL), F32)])
    dmerged = _mm("out_dx", do, w["w_out"], "nt")
    big["w_out"] = _mm("out_dw", s["merged"], do, "tn")
    rider, n_pending = None, len(pending or ())
    if exchange_name is not None:
        late_sums, _ = _pair_stage(exchange_name, _grad_slabs(big, LATE))
        rider = _chip_rider([(part, True) for part in list(pending or ()) + late_sums])

    res = _merge_bwd(s["ys"], w["wb"], proj, dmerged)
    dgl, dts = res[:4], res[4:]
    dys = [_mm("branch_dx", dts[b], w["wb"][b], "nt") for b in range(4)]
    big["wb"] = [_mm("branch_dw", s["ys"][b], dts[b], "tn") for b in range(4)]
    seg = {}

    dqa, dka, dva, dka_prev, dva_prev, dsink = _swa_bwd(s["qa"], s["ka"], s["va"], s["sinks"], dys[0])
    seg["sq"], seg["sk"], seg["sv"] = _swa_unprep(dqa, dka, dva, dka_prev, dva_prev, cos64, sin64)
    small["swa_sinks"] = dsink[:, 0]

    def glu_bwd(t, d):
        za, sg = t[:, :512], jax.nn.sigmoid(t[:, 512:])
        return (jnp.concatenate([d * sg, d * za * sg * (1.0 - sg)], axis=1),)
    (dzz,) = _rowwise("s5_glu_bwd", glu_bwd, [s["zz"], dys[1]], [], [((t_len, 1024), BF16)])
    dz = _mm("s5_glu_dx", dzz, w["w_glu"], "nt")
    big["w_glu"] = _mm("s5_glu_dw", s["z"], dzz, "tn")

    def gelu_bwd(y, u, d, skip):
        _, vjp = jax.vjp(jax.nn.gelu, y + skip * u)
        (dy2,) = vjp(d)
        return (dy2, dy2 * skip), (jnp.sum(dy2 * u, axis=0, keepdims=True),)
    dy_s, du_skip, d_skip = _rowwise("s5_gelu_bwd", gelu_bwd, [s["y_s"], _seg(proj, "su"), dz], [_row(p["s5_d"])],
                                     [((t_len, 512), F32), ((t_len, 512), F32)], [((1, 512), F32)])
    small["s5_d"] = d_skip[0]
    s5 = s["s5"]
    dlam_in, dy_p = _s5_expand("s5_dx_in", dy_s, 0, s5["w_y_t"])
    lam, da_re, da_im = _s5_scan("s5_scan_bwd", dlam_in, s5["a_re"], s5["a_im"], states=s["xs"], carry_in=s["cin"])
    du_s5 = _s5_contract("s5_du", lam, s5["w_bu_t"])
    d_w_bu = _s5_band_grad("s5_dbu", s["u_p"], lam, True)
    d_w_y = _s5_band_grad("s5_dc", dy_p, s["xs"], False)
    (seg["su"],) = _rowwise("s5_du_sum", lambda a, b: (a + b,), [du_s5, du_skip], [], [((t_len, 512), BF16)])
    d_bb_re = _band_diag_extract(d_w_bu[:4], 16, 64).transpose(0, 2, 1).reshape(S5_CH, S5_GROUP)
    d_bb_im = _band_diag_extract(d_w_bu[4:], 16, 64).transpose(0, 2, 1).reshape(S5_CH, S5_GROUP)
    small["s5_c_re"] = _band_diag_extract(d_w_y[:4], 64, 16).transpose(0, 2, 1)
    small["s5_c_im"] = -_band_diag_extract(d_w_y[4:], 64, 16).transpose(0, 2, 1)
    fl = s5["flat"]
    group_sum = jnp.repeat(jnp.eye(S5_GROUPS, dtype=F32), S5_STATE, axis=1)
    d_lr, d_li, d_ldt, d_br, d_bi = _s5_param_bwd(fl["lam_re"], fl["lam_im"], fl["log_dt"], fl["b_re"], fl["b_im"], group_sum,
                                                  da_re.reshape(S5_CH, 1), da_im.reshape(S5_CH, 1), d_bb_re, d_bb_im)
    small["s5_lam_re"], small["s5_lam_im"] = d_lr.reshape(32, 64), d_li.reshape(32, 64)
    small["s5_log_dt"] = d_ldt[:, 0]
    small["s5_b_re"], small["s5_b_im"] = d_br.reshape(32, 64, 16), d_bi.reshape(32, 64, 16)

    dqc, dkc, seg["rv"], seg["rg"] = _ret_bwd(s["qc"], s["kc"], proj, s["prevs"], dys[2])
    seg["rq"], seg["rk"] = _ret_unprep(dqc, dkc, cos64, sin64)

    do_h, delta = _mla_bwd_prep(s["ys"][3], dys[3])
    d_q, d_k, d_v, ridden = _mla_bwd(s["Q"], s["K"], s["V"], do_h, s["lse"], delta, rider)
    dqf, dkvf, seg["kr"] = _mla_unprep(d_q, d_k, d_v, cos32, sin32)
    dcqn = _mm("mla_uq_dx", dqf, w["w_uq"], "nt")
    big["w_uq"] = _mm("mla_uq_dw", s["cqn"], dqf, "tn")
    dckvn = _mm("mla_ukv_dx", dkvf, w["w_ukv"], "nt")
    big["w_ukv"] = _mm("mla_ukv_dw", s["ckvn"], dkvf, "tn")

    def mla_norm_bwd(cq, ckv, d1, d2, gq, gkv):
        dcq, dgq = _rms_bwd(cq, gq, d1)
        dckv, dgkv = _rms_bwd(ckv, gkv, d2)
        return (dcq, dckv), (dgq, dgkv)
    seg["cq"], seg["ckv"], dgq, dgkv = _rowwise(
        "mla_norm_bwd", mla_norm_bwd, [_seg(proj, "cq"), _seg(proj, "ckv"), dcqn, dckvn],
        [_row(p["mla_g_q"]), _row(p["mla_g_kv"])], [((t_len, 256), BF16), ((t_len, 128), BF16)],
        [((1, 256), F32), ((1, 128), F32)])
    small["mla_g_q"], small["mla_g_kv"] = dgq[0], dgkv[0]

    dproj = jnp.concatenate([seg[name] for name in PROJ_ORDER[:-1]] + list(dgl), axis=1)
    dh1 = _mm("proj_dx", dproj, w["w_in"], "nt")
    big["w_in"] = _mm("proj_dw", s["h1"], dproj, "tn")

    def pre_mix_bwd(xt, dh, d1, g):
        dxn, dg = _rms_bwd(xt, g, dh)
        return (d1 + dxn,), (dg,)
    dx, small["g_pre_mix"] = _rowwise("pre_mix_bwd", pre_mix_bwd, [x, dh1, dx1], [_row(p["g_pre_mix"])],
                                      [((t_len, D_MODEL), F32)], [((1, D_MODEL), F32)])
    for name in ("g_pre_mix", "g_post_mix", "g_pre_mlp", "g_post_mlp"):
        small[name] = small[name][0]
    return dx, big, small, ridden[:n_pending], ridden[n_pending:]


BIG = ("w_in", "s5_w_glu", "mla_w_uq", "mla_w_ukv", "w_branch", "w_out", "w_ff1", "w_ff2")
SMALL = ("g_pre_mix", "g_post_mix", "g_pre_mlp", "g_post_mlp", "swa_sinks", "s5_lam_re", "s5_lam_im", "s5_log_dt",
         "s5_b_re", "s5_b_im", "s5_c_re", "s5_c_im", "s5_d", "mla_g_q", "mla_g_kv")
WEIGHTS = ("g_pre_mix", "g_post_mix", "g_pre_mlp", "g_post_mlp", "w_in", "swa_sinks", "s5_lam_re", "s5_lam_im", "s5_log_dt",
           "s5_b_re", "s5_b_im", "s5_c_re", "s5_c_im", "s5_d", "s5_w_glu", "mla_g_q", "mla_g_kv", "mla_w_uq", "mla_w_ukv",
           "w_branch", "w_out", "w_ff1", "w_ff2")


EARLY = ("w_in", "s5_w_glu", "mla_w_uq", "mla_w_ukv", "w_branch")
LATE = ("w_out", "w_ff1", "w_ff2")


def _early_weights(shards):
    g = dict(zip(EARLY, shards))
    wb = g["w_branch"].transpose(1, 2, 0, 3).reshape(4, 512, D_MODEL)
    return dict(w_in=_proj_cols(_cols_gathered(g["w_in"])), w_glu=_cols_gathered(g["s5_w_glu"]),
                w_uq=_uq_cols(_cols_gathered(g["mla_w_uq"])), w_ukv=_ukv_cols(_cols_gathered(g["mla_w_ukv"])),
                wb=[wb[b] for b in range(4)])


def _late_weights(shards):
    g = dict(zip(LATE, shards))
    return dict(w_out=g["w_out"].reshape(D_MODEL, D_MODEL), w_ff1=_cols_gathered(g["w_ff1"]),
                w_ff2=g["w_ff2"].reshape(4 * D_MODEL, D_MODEL))


def _grad_slabs(big, names):
    make = dict(w_in=lambda: _cols_slabs(_unproj_cols(big["w_in"])), s5_w_glu=lambda: _cols_slabs(big["w_glu"]),
                mla_w_uq=lambda: _cols_slabs(_un_uq_cols(big["w_uq"])), mla_w_ukv=lambda: _cols_slabs(_un_ukv_cols(big["w_ukv"])),
                w_branch=lambda: jnp.stack(big["wb"]).reshape(4, 512, N_DEV, D_MODEL // N_DEV).transpose(2, 0, 1, 3),
                w_out=lambda: big["w_out"].reshape(N_DEV, D_MODEL // N_DEV, D_MODEL),
                w_ff1=lambda: _cols_slabs(big["w_ff1"]), w_ff2=lambda: big["w_ff2"].reshape(N_DEV, 4 * D_MODEL // N_DEV, D_MODEL))
    return [make[name]() for name in names]


def _pair_stage(name, slabs, extra=()):
    core = lax.axis_index("c")
    own, send = [], []
    for slab in slabs:
        by_chip = slab.reshape((4, 2) + slab.shape[1:])
        own.append(lax.dynamic_index_in_dim(by_chip, core, axis=1, keepdims=False))
        send.append(lax.dynamic_index_in_dim(by_chip, 1 - core, axis=1, keepdims=False).astype(BF16))
    got = _pair_exchange(name, send + list(extra))
    sums = []
    for mine, theirs in zip(own, got):
        flat = (-1, mine.shape[-1])
        (part,) = _rowwise("grads_pair_sum", lambda a, b: (a + b.astype(F32),), [mine.reshape(flat), theirs.reshape(flat)],
                           [], [((int(np.prod(mine.shape[:-1])), mine.shape[-1]), BF16)])
        sums.append(part.reshape(mine.shape))
    return sums, got[len(slabs):]


def kernel(x, g_pre_mix, g_post_mix, g_pre_mlp, g_post_mlp, w_in, swa_sinks, s5_lam_re, s5_lam_im, s5_log_dt, s5_b_re, s5_b_im, s5_c_re, s5_c_im, s5_d, s5_w_glu, mla_g_q, mla_g_kv, mla_w_uq, mla_w_ukv, w_branch, w_out, w_ff1, w_ff2, loss_target, m_g_pre_mix, m_g_post_mix, m_g_pre_mlp, m_g_post_mlp, m_w_in, m_swa_sinks, m_s5_lam_re, m_s5_lam_im, m_s5_log_dt, m_s5_b_re, m_s5_b_im, m_s5_c_re, m_s5_c_im, m_s5_d, m_s5_w_glu, m_mla_g_q, m_mla_g_kv, m_mla_w_uq, m_mla_w_ukv, m_w_branch, m_w_out, m_w_ff1, m_w_ff2, v_g_pre_mix, v_g_post_mix, v_g_pre_mlp, v_g_post_mlp, v_w_in, v_swa_sinks, v_s5_lam_re, v_s5_lam_im, v_s5_log_dt, v_s5_b_re, v_s5_b_im, v_s5_c_re, v_s5_c_im, v_s5_d, v_s5_w_glu, v_mla_g_q, v_mla_g_kv, v_mla_w_uq, v_mla_w_ukv, v_w_branch, v_w_out, v_w_ff1, v_w_ff2):
    weights = dict(g_pre_mix=g_pre_mix, g_post_mix=g_post_mix, g_pre_mlp=g_pre_mlp, g_post_mlp=g_post_mlp, w_in=w_in,
                   swa_sinks=swa_sinks, s5_lam_re=s5_lam_re, s5_lam_im=s5_lam_im, s5_log_dt=s5_log_dt, s5_b_re=s5_b_re,
                   s5_b_im=s5_b_im, s5_c_re=s5_c_re, s5_c_im=s5_c_im, s5_d=s5_d, s5_w_glu=s5_w_glu, mla_g_q=mla_g_q,
                   mla_g_kv=mla_g_kv, mla_w_uq=mla_w_uq, mla_w_ukv=mla_w_ukv, w_branch=w_branch, w_out=w_out, w_ff1=w_ff1,
                   w_ff2=w_ff2)
    m_in = dict(g_pre_mix=m_g_pre_mix, g_post_mix=m_g_post_mix, g_pre_mlp=m_g_pre_mlp, g_post_mlp=m_g_post_mlp, w_in=m_w_in,
                swa_sinks=m_swa_sinks, s5_lam_re=m_s5_lam_re, s5_lam_im=m_s5_lam_im, s5_log_dt=m_s5_log_dt,
                s5_b_re=m_s5_b_re, s5_b_im=m_s5_b_im, s5_c_re=m_s5_c_re, s5_c_im=m_s5_c_im, s5_d=m_s5_d,
                s5_w_glu=m_s5_w_glu, mla_g_q=m_mla_g_q, mla_g_kv=m_mla_g_kv, mla_w_uq=m_mla_w_uq, mla_w_ukv=m_mla_w_ukv,
                w_branch=m_w_branch, w_out=m_w_out, w_ff1=m_w_ff1, w_ff2=m_w_ff2)
    v_in = dict(g_pre_mix=v_g_pre_mix, g_post_mix=v_g_post_mix, g_pre_mlp=v_g_pre_mlp, g_post_mlp=v_g_post_mlp, w_in=v_w_in,
                swa_sinks=v_swa_sinks, s5_lam_re=v_s5_lam_re, s5_lam_im=v_s5_lam_im, s5_log_dt=v_s5_log_dt,
                s5_b_re=v_s5_b_re, s5_b_im=v_s5_b_im, s5_c_re=v_s5_c_re, s5_c_im=v_s5_c_im, s5_d=v_s5_d,
                s5_w_glu=v_s5_w_glu, mla_g_q=v_mla_g_q, mla_g_kv=v_mla_g_kv, mla_w_uq=v_mla_w_uq, mla_w_ukv=v_mla_w_ukv,
                w_branch=v_w_branch, w_out=v_w_out, w_ff1=v_w_ff1, w_ff2=v_w_ff2)
    depth = g_pre_mix.shape[0]
    t_len = x.shape[1]
    tabs = _rope_tables(t_len, 32) + _rope_tables(t_len, 16)

    def shards(l, names):
        return [weights[name][l].astype(BF16) for name in names]
    early = _run_rider("gather_weights", _gather_rider(shards(0, EARLY)))

    xs, saved, layer_w, layer_p = [x[0]], [], [], []
    for l in range(depth):
        layer_w.append(_early_weights(early))
        layer_p.append({name: weights[name][l] for name in SMALL})
        x_next, s, early = _layer_fwd(xs[-1], layer_w[l], layer_p[l], tabs, shards(l, LATE),
                                      _gather_rider(shards(l + 1, EARLY)) if l + 1 < depth else None)
        xs.append(x_next)
        saved.append(s)

    def loss_fn(y, tgt):
        err = y - tgt
        part = 0.5 * jnp.sum(jnp.mean(err * err, axis=-1, keepdims=True), axis=0, keepdims=True)
        return (err * (1.0 / D_MODEL),), (jnp.broadcast_to(part, (8, 128)),)
    dx, loss_part = _rowwise("loss", loss_fn, [xs[-1], loss_target[0]], [], [((t_len, D_MODEL), F32)], [((8, 128), F32)])
    loss = lax.psum(loss_part[0, 0], ("x", "y", "c"))

    small_grads, arrived, pending = [None] * depth, [{} for _ in range(depth)], None
    for l in reversed(range(depth)):
        dx, big, small_grads[l], got_pending, got_late = _layer_bwd(xs[l], dx, layer_w[l], layer_p[l], saved[l], tabs, pending,
                                                                    "late_grads_to_sibling")
        if pending is not None:
            arrived[l + 1].update(zip(EARLY, got_pending))
        arrived[l].update(zip(LATE, got_late))
        if l > 0:
            pending, _ = _pair_stage("early_grads_to_sibling", _grad_slabs(big, EARLY))
    small_vec = jnp.concatenate([jnp.stack([small_grads[l][name] for l in range(depth)]).reshape(-1) for name in SMALL])
    n_small = small_vec.shape[0]
    rows = -(-n_small // 1024) * 8
    small_mat = jnp.pad(small_vec, (0, rows * 128 - n_small)).reshape(rows, 128)
    sums, (sibling_small,) = _pair_stage("last_grads_to_sibling", _grad_slabs(big, EARLY), [small_mat])
    small_pair = jnp.where(lax.axis_index("c") == 0, jnp.stack([small_mat, sibling_small]), jnp.stack([sibling_small, small_mat]))
    recv = _run_rider("grads_to_chips", _chip_rider([(part, True) for part in sums] + [(small_pair, False)]))
    arrived[0].update(zip(EARLY, recv[:-1]))
    small_sum = _sum_slabs("sum_small_grads", recv[-1].reshape((N_DEV,) + small_mat.shape)).reshape(-1)

    grads, deltas, new_m, new_v = {}, {}, {}, {}
    for name in BIG:
        slabs = jnp.stack([arrived[l][name] for l in range(depth)], axis=1)
        grads[name], deltas[name], new_m[name], new_v[name] = _adamw("adamw_" + name, weights[name], slabs, m_in[name],
                                                                     v_in[name], 4)
    off = 0
    for name in SMALL:
        size = int(np.prod(weights[name].shape))
        g = small_sum[off:off + size].reshape(weights[name].shape)
        off += size
        grads[name], deltas[name], new_m[name], new_v[name] = _adamw("adamw_" + name, weights[name], g, m_in[name],
                                                                     v_in[name], False)
    return (loss, dx[None], *[grads[n] for n in WEIGHTS], *[deltas[n] for n in WEIGHTS], *[new_m[n] for n in WEIGHTS],
            *[new_v[n] for n in WEIGHTS])
```

```python
import functools
import math

import numpy as np
import jax
import jax.numpy as jnp
from jax import lax
from jax.experimental import pallas as pl
from jax.experimental.pallas import tpu as pltpu

F32, BF16 = jnp.float32, jnp.bfloat16
EPS = 1e-6
NEG_INF = -1e30
ROPE_THETA = 10000.0
D_MODEL = 1024
N_DEV = 8
BLOCK = 128
SWA_HEADS, SWA_KV = 8, 2
RET_HEADS, RET_DK, RET_DV = 4, 64, 128
MLA_HEADS = 8
S5_GROUPS, S5_STATE, S5_GROUP = 32, 64, 16
S5_CH = S5_GROUPS * S5_STATE
SCAN_LANES = 256
SCAN_SEG = 8
ADAM_LR, ADAM_B1, ADAM_B2, ADAM_EPS, ADAM_WD, ADAM_STEP = 0.001, 0.9, 0.999, 1e-08, 0.01, 10
VMEM_MB = 56

PROJ = dict(sq=(0, 512), su=(512, 512), rv=(1024, 512), rg=(1536, 512), rq=(2048, 256), rk=(2304, 256),
            cq=(2560, 256), sk=(2816, 128), sv=(2944, 128), ckv=(3072, 128), kr=(3200, 128), gates=(3328, 4096))
PROJ_W = 7424
REF_SPLIT = dict(sq=(0, 512), sk=(512, 128), sv=(640, 128), su=(768, 512), rq=(1280, 256), rk=(1536, 256),
                 rv=(1792, 512), rg=(2304, 512), cq=(2816, 256), ckv=(3072, 128), kr=(3200, 32), gates=(3232, 4096))
PROJ_ORDER = ("sq", "su", "rv", "rg", "rq", "rk", "cq", "sk", "sv", "ckv", "kr", "gates")
REF_ORDER = ("sq", "sk", "sv", "su", "rq", "rk", "rv", "rg", "cq", "ckv", "kr", "gates")

NN = (((1,), (0,)), ((), ()))
NT = (((1,), (1,)), ((), ()))
TN = (((0,), (0,)), ((), ()))


def _params(sem):
    return pltpu.CompilerParams(dimension_semantics=sem, vmem_limit_bytes=VMEM_MB * 2 ** 20)


def _raw_dot(a, b, dims):
    return lax.dot_general(a.astype(BF16), b.astype(BF16), dims, preferred_element_type=F32)


@jax.custom_vjp
def _dnn(a, b):
    return _raw_dot(a, b, NN)


@jax.custom_vjp
def _dnt(a, b):
    return _raw_dot(a, b, NT)


@jax.custom_vjp
def _dtn(a, b):
    return _raw_dot(a, b, TN)


_dnn.defvjp(lambda a, b: (_dnn(a, b), (a, b)),
            lambda r, g: (_dnt(g, r[1]).astype(r[0].dtype), _dtn(r[0], g).astype(r[1].dtype)))
_dnt.defvjp(lambda a, b: (_dnt(a, b), (a, b)),
            lambda r, g: (_dnn(g, r[1]).astype(r[0].dtype), _dtn(g, r[0]).astype(r[1].dtype)))
_dtn.defvjp(lambda a, b: (_dtn(a, b), (a, b)),
            lambda r, g: (_dnt(r[1], g).astype(r[0].dtype), _dnn(r[0], g).astype(r[1].dtype)))


class _Cols:
    def __init__(self, arr, width, index):
        self.arr, self.width, self.index = arr, width, index


def _seg(proj, name):
    off, width = PROJ[name]
    return _Cols(proj, width, off // width)


def _row_spec(shape, tile):
    lead = len(shape) - 2
    return pl.BlockSpec(tuple(shape[:-2]) + (tile, shape[-1]), lambda i, lead=lead: (0,) * lead + (i, 0))


def _const_spec(shape):
    nd = len(shape)
    return pl.BlockSpec(tuple(shape), lambda *_, nd=nd: (0,) * nd)


ROW_VMEM_BUDGET = 24 * 2 ** 20


def _rowwise(name, fn, rows, consts, out_rows, out_accs=()):
    arrs = [r.arr if isinstance(r, _Cols) else r for r in rows]
    t_len = arrs[0].shape[-2]
    row_bytes = sum((r.width if isinstance(r, _Cols) else int(np.prod(r.shape)) // r.shape[-2]) * a.dtype.itemsize
                    for r, a in zip(rows, arrs))
    row_bytes += sum(int(np.prod(s)) // s[-2] * jnp.dtype(d).itemsize for s, d in out_rows)
    tile = next((t for t in (1024, 512, 256, 128) if t_len % t == 0 and 2 * t * row_bytes <= ROW_VMEM_BUDGET), min(128, t_len))
    n = t_len // tile
    in_specs = []
    for r in rows:
        if isinstance(r, _Cols):
            in_specs.append(pl.BlockSpec((tile, r.width), lambda i, c=r.index: (i, c)))
        else:
            in_specs.append(_row_spec(r.shape, tile))
    in_specs += [_const_spec(c.shape) for c in consts]
    out_specs = [_row_spec(s, tile) for s, _ in out_rows] + [_const_spec(s) for s, _ in out_accs]
    out_shape = [jax.ShapeDtypeStruct(s, d) for s, d in tuple(out_rows) + tuple(out_accs)]
    n_in, n_or = len(rows) + len(consts), len(out_rows)

    def body(*refs):
        res = fn(*[r[...] for r in refs[:n_in]])
        row_out, acc_out = res if out_accs else (res, ())
        for ref, val in zip(refs[n_in:n_in + n_or], row_out):
            if isinstance(val, (list, tuple)):
                for h, v_h in enumerate(val):
                    ref[h] = v_h.astype(ref.dtype)
            else:
                ref[...] = val.astype(ref.dtype)
        first = pl.program_id(0) == 0
        for ref, val in zip(refs[n_in + n_or:], acc_out):
            @pl.when(first)
            def _(ref=ref, val=val):
                ref[...] = val

            @pl.when(jnp.logical_not(first))
            def _(ref=ref, val=val):
                ref[...] += val

    return pl.pallas_call(body, name=name, grid=(n,), in_specs=in_specs, out_specs=out_specs, out_shape=out_shape,
                          compiler_params=_params(("arbitrary",)))(*arrs, *consts)


MM_VMEM_BUDGET = 44 * 2 ** 20


def _pick_tn(n, io_col_bytes, tmp_col_bytes, fixed_bytes):
    best = None
    for cand in range(128, n + 1, 128):
        if n % cand == 0 and 2 * (fixed_bytes + cand * io_col_bytes) + cand * tmp_col_bytes <= MM_VMEM_BUDGET:
            best = cand
    return best if best is not None else min(n, 128)


def _mm(name, a, b, mode, out_dtypes=(F32,), tm=512, tn=None, epi=None, epi_in=(), rider=None):
    if mode == "nn":
        (m, k), n = a.shape, b.shape[1]
    elif mode == "nt":
        (m, k), n = a.shape, b.shape[0]
    else:
        (k, m), n = a.shape, b.shape[1]
    tm = min(tm, m)
    if tn is None:
        io_col = k * b.dtype.itemsize + tm * (sum(jnp.dtype(d).itemsize for d in out_dtypes) + sum(e.dtype.itemsize for e in epi_in))
        tmp_col = tm * 4 * (2 if epi is not None else 1)
        tn = _pick_tn(n, io_col, tmp_col, tm * k * a.dtype.itemsize)
    tn = min(tn, n)
    assert m % tm == 0 and n % tn == 0, (name, m, n, tm, tn)
    a_spec = pl.BlockSpec((k, tm), lambda i, j: (0, i)) if mode == "tn" else pl.BlockSpec((tm, k), lambda i, j: (i, 0))
    b_spec = pl.BlockSpec((tn, k), lambda i, j: (j, 0)) if mode == "nt" else pl.BlockSpec((k, tn), lambda i, j: (0, j))
    dims = dict(nn=NN, nt=NT, tn=TN)[mode]
    o_spec = pl.BlockSpec((tm, tn), lambda i, j: (i, j))
    n_e = len(epi_in)

    def body(a_ref, b_ref, *refs):
        acc = _raw_dot(a_ref[...], b_ref[...], dims)
        outs = epi(acc, *[r[...] for r in refs[:n_e]]) if epi is not None else (acc,)
        for ref, val in zip(refs[n_e:], outs):
            ref[...] = val.astype(ref.dtype)

    grid = (m // tm, n // tn)
    ride_in, ride_out, ride_sems = ([], [], []) if rider is None else (rider.inputs, rider.out_shape, rider.sems)
    any_spec = pl.BlockSpec(memory_space=pl.ANY)
    res = pl.pallas_call(_ride(body, rider, (0, 2 + n_e, len(out_dtypes), 0), grid), name=name, grid=grid,
                         in_specs=[a_spec, b_spec] + [o_spec] * n_e + [any_spec] * len(ride_in),
                         out_specs=[o_spec] * len(out_dtypes) + [any_spec] * len(ride_out),
                         out_shape=[jax.ShapeDtypeStruct((m, n), d) for d in out_dtypes] + ride_out, scratch_shapes=ride_sems,
                         compiler_params=_params(("parallel", "parallel") if rider is None else ("arbitrary", "arbitrary")))(
                             a, b, *epi_in, *ride_in)
    own = res[0] if len(out_dtypes) == 1 else res[:len(out_dtypes)]
    return own if rider is None else (own, res[len(out_dtypes):])


def _rms(x, g):
    return x * lax.rsqrt(jnp.mean(x * x, axis=-1, keepdims=True) + EPS) * g


def _rms_bwd(x, g, dy):
    _, vjp = jax.vjp(_rms, x, g)
    return vjp(dy)


def _rope_tables(t_len, half):
    dim = 2 * half
    inv = 1.0 / (ROPE_THETA ** (jnp.arange(0, dim, 2, dtype=F32) / dim))
    ang = jnp.arange(t_len, dtype=F32)[:, None] * inv[None, :]
    c, s = jnp.cos(ang), jnp.sin(ang)
    reps = 128 // dim
    return jnp.tile(jnp.concatenate([c, c], 1), (1, reps)), jnp.tile(jnp.concatenate([-s, s], 1), (1, reps))


def _rope(x, cos, sin, half):
    w = x.shape[-1]
    reps = w // 128
    if reps > 1:
        cos, sin = jnp.tile(cos, (1, reps)), jnp.tile(sin, (1, reps))
    lane = lax.broadcasted_iota(jnp.int32, x.shape, 1)
    partner = jnp.where((lane % (2 * half)) < half, pltpu.roll(x, w - half, 1), pltpu.roll(x, half, 1))
    return x * cos + partner * sin


SWA_GROUP = SWA_HEADS // SWA_KV
SWA_QB = 4
SWA_TB_FWD, SWA_TB_BWD = 4, 2


def _swa_mask(tb, first_block):
    rows, keys = SWA_GROUP * tb * BLOCK, (tb + 1) * BLOCK
    qi = lax.broadcasted_iota(jnp.int32, (rows, keys), 0) % (tb * BLOCK) + BLOCK
    kj = lax.broadcasted_iota(jnp.int32, (rows, keys), 1)
    diff = qi - kj
    band = (diff >= 0) & (diff < BLOCK)
    return band if first_block is False else band & (jnp.logical_not(first_block) | (kj >= BLOCK))


def _swa_stack(ref, j, b, tb, dtype=None):
    rows = slice(BLOCK * b, BLOCK * (b + tb))
    parts = [ref[rows, 64 * (SWA_GROUP * j + g):64 * (SWA_GROUP * j + g) + 64] for g in range(SWA_GROUP)]
    out = jnp.concatenate(parts, axis=0)
    return out if dtype is None else out.astype(dtype)


def _swa_sinks(s_ref, j, tb):
    return jnp.concatenate([jnp.broadcast_to(s_ref[SWA_GROUP * j + g:SWA_GROUP * j + g + 1, 0:1], (tb * BLOCK, 1))
                            for g in range(SWA_GROUP)], axis=0)


def _swa_head(q, kw, vw, sink, mask):
    sc = _dnt(q, kw) * (64 ** -0.5)
    sc = jnp.where(mask, sc, NEG_INF)
    m = lax.stop_gradient(jnp.maximum(jnp.max(sc, axis=-1, keepdims=True), sink))
    p = jnp.exp(sc - m)
    denom = jnp.sum(p, axis=-1, keepdims=True) + jnp.exp(sink - m)
    return _dnn(p / denom, vw)


def _swa_prep(proj, cos, sin):
    def fn(sq, sk, sv, c, s):
        return _rope(sq, c, s, 32), _rope(sk, c, s, 32), sv
    t_len = proj.shape[0]
    return _rowwise("swa_prep", fn, [_seg(proj, "sq"), _seg(proj, "sk"), _seg(proj, "sv"), cos, sin], [],
                    [((t_len, 512), BF16), ((t_len, 128), BF16), ((t_len, 128), BF16)])


def _swa_specs(t_len, tb_max=1):
    qb = min(SWA_QB, t_len // BLOCK)
    tb = min(tb_max, qb)
    rows = qb * BLOCK
    cur = lambda w: pl.BlockSpec((rows, w), lambda i: (i, 0))
    prev = pl.BlockSpec((BLOCK, 128), lambda i: (jnp.maximum(qb * i - 1, 0), 0))
    return qb, tb, t_len // rows, cur, prev


def _swa_fwd(q, k, v, sinks):
    t_len = q.shape[0]
    qb, tb, steps, cur, prev = _swa_specs(t_len, SWA_TB_FWD)

    def body(q_ref, kp_ref, kc_ref, vp_ref, vc_ref, s_ref, o_ref):
        first = pl.program_id(0) == 0
        k_all = jnp.concatenate([kp_ref[...], kc_ref[...]], axis=0)
        v_all = jnp.concatenate([vp_ref[...], vc_ref[...]], axis=0)
        for j in range(SWA_KV):
            cols = slice(64 * j, 64 * j + 64)
            sink = _swa_sinks(s_ref, j, tb)
            for b in range(0, qb, tb):
                window = slice(BLOCK * b, BLOCK * (b + tb + 1))
                mask = _swa_mask(tb, first if b == 0 else False)
                o = _swa_head(_swa_stack(q_ref, j, b, tb), k_all[window, cols], v_all[window, cols], sink, mask)
                for g in range(SWA_GROUP):
                    h = SWA_GROUP * j + g
                    o_ref[BLOCK * b:BLOCK * (b + tb), 64 * h:64 * h + 64] = o[tb * BLOCK * g:tb * BLOCK * (g + 1)].astype(o_ref.dtype)

    return pl.pallas_call(
        body, name="swa_fwd", grid=(steps,),
        in_specs=[cur(512), prev, cur(128), prev, cur(128), _const_spec((8, 128))],
        out_specs=cur(512), out_shape=jax.ShapeDtypeStruct((t_len, 512), BF16),
        compiler_params=_params(("parallel",)))(q, k, k, v, v, sinks)


def _swa_bwd(q, k, v, sinks, dy):
    t_len = q.shape[0]
    qb, tb, steps, cur, prev = _swa_specs(t_len, SWA_TB_BWD)

    def body(q_ref, kp_ref, kc_ref, vp_ref, vc_ref, s_ref, dy_ref, dq_ref, dk_ref, dv_ref, dkp_ref, dvp_ref, ds_ref):
        first = pl.program_id(0) == 0

        @pl.when(first)
        def _():
            ds_ref[...] = jnp.zeros_like(ds_ref)

        k_all = jnp.concatenate([kp_ref[...], kc_ref[...]], axis=0)
        v_all = jnp.concatenate([vp_ref[...], vc_ref[...]], axis=0)
        for j in range(SWA_KV):
            cols = slice(64 * j, 64 * j + 64)
            sink = _swa_sinks(s_ref, j, tb)
            dk_parts = [jnp.zeros((BLOCK, 64), F32) for _ in range(qb + 1)]
            dv_parts = [jnp.zeros((BLOCK, 64), F32) for _ in range(qb + 1)]
            dsink = jnp.zeros((SWA_GROUP * tb * BLOCK, 1), F32)
            for b in range(0, qb, tb):
                window = slice(BLOCK * b, BLOCK * (b + tb + 1))
                mask = _swa_mask(tb, first if b == 0 else False)
                _, vjp = jax.vjp(lambda a, b_, c, d, mask=mask: _swa_head(a, b_, c, d, mask), _swa_stack(q_ref, j, b, tb),
                                 k_all[window, cols], v_all[window, cols], sink)
                dq, dkw, dvw, dsink_b = vjp(_swa_stack(dy_ref, j, b, tb, F32))
                dkw, dvw = dkw.astype(F32), dvw.astype(F32)
                for r in range(tb + 1):
                    dk_parts[b + r] = dk_parts[b + r] + dkw[BLOCK * r:BLOCK * (r + 1)]
                    dv_parts[b + r] = dv_parts[b + r] + dvw[BLOCK * r:BLOCK * (r + 1)]
                dsink = dsink + dsink_b
                for g in range(SWA_GROUP):
                    h = SWA_GROUP * j + g
                    dq_ref[BLOCK * b:BLOCK * (b + tb), 64 * h:64 * h + 64] = dq[tb * BLOCK * g:tb * BLOCK * (g + 1)].astype(F32)
            dkp_ref[:, cols], dvp_ref[:, cols] = dk_parts[0], dv_parts[0]
            for b in range(qb):
                dk_ref[BLOCK * b:BLOCK * b + BLOCK, cols] = dk_parts[b + 1]
                dv_ref[BLOCK * b:BLOCK * b + BLOCK, cols] = dv_parts[b + 1]
            for g in range(SWA_GROUP):
                h = SWA_GROUP * j + g
                ds_ref[h:h + 1, :] += jnp.broadcast_to(jnp.sum(dsink[tb * BLOCK * g:tb * BLOCK * (g + 1)], axis=0, keepdims=True), (1, 128))

    part = pl.BlockSpec((BLOCK, 128), lambda i: (i, 0))
    return pl.pallas_call(
        body, name="swa_bwd", grid=(steps,),
        in_specs=[cur(512), prev, cur(128), prev, cur(128), _const_spec((8, 128)), cur(512)],
        out_specs=[cur(512), cur(128), cur(128), part, part, _const_spec((8, 128))],
        out_shape=[jax.ShapeDtypeStruct((t_len, 512), F32), jax.ShapeDtypeStruct((t_len, 128), F32),
                   jax.ShapeDtypeStruct((t_len, 128), F32), jax.ShapeDtypeStruct((steps * BLOCK, 128), F32),
                   jax.ShapeDtypeStruct((steps * BLOCK, 128), F32), jax.ShapeDtypeStruct((8, 128), F32)],
        compiler_params=_params(("arbitrary",)))(q, k, k, v, v, sinks, dy)


def _swa_unprep(dq, dk, dv, dk_prev, dv_prev, cos, sin):
    t_len = dq.shape[0]
    qb, _, steps, cur, _ = _swa_specs(t_len)
    tail = (qb - 1) * BLOCK

    def body(dq_ref, dk_ref, dv_ref, dkn_ref, dvn_ref, c_ref, s_ref, oq_ref, ok_ref, ov_ref):
        more = (pl.program_id(0) < steps - 1).astype(F32)
        cos_t, sin_t = c_ref[...], -s_ref[...]
        oq_ref[...] = _rope(dq_ref[...], cos_t, sin_t, 32).astype(BF16)
        dk_t = jnp.concatenate([dk_ref[:tail], dk_ref[tail:] + more * dkn_ref[...]], axis=0) if tail else dk_ref[...] + more * dkn_ref[...]
        dv_t = jnp.concatenate([dv_ref[:tail], dv_ref[tail:] + more * dvn_ref[...]], axis=0) if tail else dv_ref[...] + more * dvn_ref[...]
        ok_ref[...] = _rope(dk_t, cos_t, sin_t, 32).astype(BF16)
        ov_ref[...] = dv_t.astype(BF16)

    nxt = pl.BlockSpec((BLOCK, 128), lambda i: (jnp.minimum(i + 1, steps - 1), 0))
    return pl.pallas_call(
        body, name="swa_unprep", grid=(steps,),
        in_specs=[cur(512), cur(128), cur(128), nxt, nxt, cur(128), cur(128)],
        out_specs=[cur(512), cur(128), cur(128)],
        out_shape=[jax.ShapeDtypeStruct((t_len, 512), BF16), jax.ShapeDtypeStruct((t_len, 128), BF16),
                   jax.ShapeDtypeStruct((t_len, 128), BF16)],
        compiler_params=_params(("parallel",)))(dq, dk, dv, dk_prev, dv_prev, cos, sin)


def _s5_discretize(lam_re, lam_im, log_dt, b_re, b_im):
    dt = jnp.exp(log_dt)
    mag = jnp.exp(lam_re * dt)
    ab_re, ab_im = mag * jnp.cos(lam_im * dt), mag * jnp.sin(lam_im * dt)
    den = lam_re * lam_re + lam_im * lam_im
    nr, ni = ab_re - 1.0, ab_im
    f_re = (nr * lam_re + ni * lam_im) / den
    f_im = (ni * lam_re - nr * lam_im) / den
    return ab_re, ab_im, f_re * b_re - f_im * b_im, f_re * b_im + f_im * b_re


def _s5_param_fwd(lam_re, lam_im, log_dt, b_re, b_im):
    def body(*refs):
        outs = _s5_discretize(*[r[...] for r in refs[:5]])
        for ref, val in zip(refs[5:], outs):
            ref[...] = val
    n = S5_CH
    return pl.pallas_call(body, name="s5_param_fwd",
                          out_shape=[jax.ShapeDtypeStruct((n, 1), F32)] * 2 + [jax.ShapeDtypeStruct((n, 16), F32)] * 2,
                          )(lam_re, lam_im, log_dt, b_re, b_im)


def _s5_param_bwd(lam_re, lam_im, log_dt, b_re, b_im, group_sum, d_ab_re, d_ab_im, d_bb_re, d_bb_im):
    def body(*refs):
        ins = [r[...] for r in refs[:10]]
        _, vjp = jax.vjp(_s5_discretize, *ins[:5])
        d_lr, d_li, d_ldt, d_br, d_bi = vjp(tuple(ins[6:10]))
        o_lr, o_li, o_ldt, o_br, o_bi = refs[10:]
        o_lr[...], o_li[...], o_br[...], o_bi[...] = d_lr, d_li, d_br, d_bi
        o_ldt[...] = lax.dot_general(ins[5], jnp.broadcast_to(d_ldt, (S5_CH, 128)), NN,
                                     precision=lax.Precision.HIGHEST, preferred_element_type=F32)
    n = S5_CH
    return pl.pallas_call(body, name="s5_param_bwd",
                          out_shape=[jax.ShapeDtypeStruct((n, 1), F32), jax.ShapeDtypeStruct((n, 1), F32),
                                     jax.ShapeDtypeStruct((S5_GROUPS, 128), F32),
                                     jax.ShapeDtypeStruct((n, 16), F32), jax.ShapeDtypeStruct((n, 16), F32)],
                          )(lam_re, lam_im, log_dt, b_re, b_im, group_sum, d_ab_re, d_ab_im, d_bb_re, d_bb_im)


def _s5_expand(name, u, first_band, w):
    t_len = u.shape[0]
    tile = min(512, t_len)
    seg = t_len // SCAN_SEG

    def body(u_ref, wr_ref, wi_ref, o_ref, up_ref, rows):
        base = pl.program_id(1) * (tile // SCAN_SEG)
        for r in range(tile // SCAN_SEG):
            rows[SCAN_SEG * r:SCAN_SEG * (r + 1), :] = u_ref[pl.ds(base + r, SCAN_SEG, stride=seg), :]
        u_t = rows[...].astype(BF16)
        up_ref[...] = u_t
        o_ref[0] = _raw_dot(u_t, wr_ref[...], NN)
        o_ref[1] = _raw_dot(u_t, wi_ref[...], NN)

    return pl.pallas_call(
        body, name=name, grid=(4, t_len // tile),
        in_specs=[pl.BlockSpec((t_len, 128), lambda b, i: (0, first_band + b)), pl.BlockSpec((None, 128, 512), lambda b, i: (b, 0, 0)),
                  pl.BlockSpec((None, 128, 512), lambda b, i: (4 + b, 0, 0))],
        out_specs=[pl.BlockSpec((2, tile, 512), lambda b, i: (0, i, b)), pl.BlockSpec((tile, 128), lambda b, i: (i, b))],
        out_shape=[jax.ShapeDtypeStruct((2, t_len, S5_CH), F32), jax.ShapeDtypeStruct((t_len, 512), BF16)],
        scratch_shapes=[pltpu.VMEM((tile, 128), F32)], compiler_params=_params(("parallel", "arbitrary")))(u, w, w)


def _s5_contract(name, x, w):
    t_len = x.shape[1]
    tile = min(512, t_len)
    seg = t_len // SCAN_SEG

    def body(x_ref, wr_ref, wi_ref, o_ref):
        y = _raw_dot(x_ref[0], wr_ref[...], NN) + _raw_dot(x_ref[1], wi_ref[...], NN)
        base = pl.program_id(1) * (tile // SCAN_SEG)
        for r in range(tile // SCAN_SEG):
            o_ref[pl.ds(base + r, SCAN_SEG, stride=seg), :] = y[SCAN_SEG * r:SCAN_SEG * (r + 1)]

    return pl.pallas_call(
        body, name=name, grid=(4, t_len // tile),
        in_specs=[pl.BlockSpec((2, tile, 512), lambda b, i: (0, i, b)), pl.BlockSpec((None, 512, 128), lambda b, i: (b, 0, 0)),
                  pl.BlockSpec((None, 512, 128), lambda b, i: (4 + b, 0, 0))],
        out_specs=pl.BlockSpec((t_len, 128), lambda b, i: (0, b)),
        out_shape=jax.ShapeDtypeStruct((t_len, 512), F32), compiler_params=_params(("parallel", "arbitrary")))(x, w, w)


def _s5_band_grad(name, narrow, wide, narrow_first):
    t_len = narrow.shape[0]

    def body(n_ref, w_ref, o_ref):
        if narrow_first:
            o_ref[...] = _raw_dot(n_ref[...], w_ref[...], TN)
        else:
            o_ref[...] = _raw_dot(w_ref[...], n_ref[...], TN)

    shape = (8, 128, 512) if narrow_first else (8, 512, 128)
    return pl.pallas_call(
        body, name=name, grid=(8,),
        in_specs=[pl.BlockSpec((t_len, 128), lambda bb: (0, bb % 4)), pl.BlockSpec((None, t_len, 512), lambda bb: (bb // 4, 0, bb % 4))],
        out_specs=pl.BlockSpec((None,) + shape[1:], lambda bb: (bb, 0, 0)),
        out_shape=jax.ShapeDtypeStruct(shape, F32), compiler_params=_params(("parallel",)))(narrow, wide)


def _s5_scan(name, bu, a_re, a_im, states=None, carry_in=None):
    t_len = bu.shape[1]
    n = t_len // SCAN_SEG
    log_n = int(round(math.log2(n)))
    assert 2 ** log_n == n
    reverse = states is not None
    lanes = SCAN_LANES
    nblk = S5_CH // lanes

    def body(*refs):
        if reverse:
            b_ref, ar_ref, ai_ref, x_ref, cin_ref, o_ref, dar_ref, dai_ref, c_scr = refs
            xr_ref, xi_ref, cr_ref, ci_ref = x_ref.at[0], x_ref.at[1], cin_ref.at[0], cin_ref.at[1]
        else:
            b_ref, ar_ref, ai_ref, o_ref, co_ref, c_scr = refs
            cor_ref, coi_ref = co_ref.at[0], co_ref.at[1]
        br_ref, bi_ref, or_ref, oi_ref = b_ref.at[0], b_ref.at[1], o_ref.at[0], o_ref.at[1]
        ar = ar_ref[...]
        ai = -ai_ref[...] if reverse else ai_ref[...]

        def tile_at(i):
            return pl.multiple_of((n - 1 - i if reverse else i) * SCAN_SEG, SCAN_SEG)

        def advance(s, row):
            sr, si = s
            return (ar * sr - ai * si + br_ref[pl.ds(row, SCAN_SEG), :], ar * si + ai * sr + bi_ref[pl.ds(row, SCAN_SEG), :])

        zero = jnp.zeros((SCAN_SEG, lanes), F32)
        fin_r, fin_i = lax.fori_loop(0, n, lambda i, s: advance(s, tile_at(i)), (zero, zero), unroll=8)
        pr, pi = ar[0:1], ai[0:1]
        for _ in range(log_n):
            pr, pi = pr * pr - pi * pi, 2.0 * pr * pi
        er = ei = jnp.zeros((1, lanes), F32)
        for step in range(SCAN_SEG):
            j = SCAN_SEG - 1 - step if reverse else step
            c_scr[0, j:j + 1, :] = er
            c_scr[1, j:j + 1, :] = ei
            er, ei = pr * er - pi * ei + fin_r[j:j + 1], pr * ei + pi * er + fin_i[j:j + 1]
        entry = (c_scr[0], c_scr[1])

        if not reverse:
            cor_ref[...], coi_ref[...] = entry

            def step2(i, s):
                row = tile_at(i)
                s = advance(s, row)
                or_ref[pl.ds(row, SCAN_SEG), :] = s[0]
                oi_ref[pl.ds(row, SCAN_SEG), :] = s[1]
                return s
            lax.fori_loop(0, n, step2, entry, unroll=8)
        else:
            def emit(s, row, xr_prev, xi_prev, acc):
                s = advance(s, row)
                or_ref[pl.ds(row, SCAN_SEG), :] = s[0]
                oi_ref[pl.ds(row, SCAN_SEG), :] = s[1]
                return s, (acc[0] + s[0] * xr_prev + s[1] * xi_prev, acc[1] + s[1] * xr_prev - s[0] * xi_prev)

            def step2(i, carry):
                s, acc = carry
                row = tile_at(i)
                prev = pl.multiple_of(row - SCAN_SEG, SCAN_SEG)
                return emit(s, row, xr_ref[pl.ds(prev, SCAN_SEG), :], xi_ref[pl.ds(prev, SCAN_SEG), :], acc)
            s, acc = lax.fori_loop(0, n - 1, step2, (entry, (zero, zero)), unroll=8)
            s, acc = emit(s, 0, cr_ref[...], ci_ref[...], acc)
            dar_ref[...] = jnp.sum(acc[0], axis=0, keepdims=True)
            dai_ref[...] = jnp.sum(acc[1], axis=0, keepdims=True)

    planes = pl.BlockSpec((2, t_len, lanes), lambda b: (0, 0, b))
    tile8 = pl.BlockSpec((SCAN_SEG, lanes), lambda b: (0, b))
    entry_spec = pl.BlockSpec((2, SCAN_SEG, lanes), lambda b: (0, 0, b))
    row1 = pl.BlockSpec((1, lanes), lambda b: (0, b))
    seq = jax.ShapeDtypeStruct((2, t_len, S5_CH), F32)
    if reverse:
        in_specs = [planes, tile8, tile8, planes, entry_spec]
        args = (bu, a_re, a_im, states, carry_in)
        out_specs = [planes, row1, row1]
        out_shape = [seq, jax.ShapeDtypeStruct((1, S5_CH), F32), jax.ShapeDtypeStruct((1, S5_CH), F32)]
    else:
        in_specs = [planes, tile8, tile8]
        args = (bu, a_re, a_im)
        out_specs = [planes, entry_spec]
        out_shape = [seq, jax.ShapeDtypeStruct((2, SCAN_SEG, S5_CH), F32)]
    return pl.pallas_call(body, name=name, grid=(nblk,), in_specs=in_specs, out_specs=out_specs, out_shape=out_shape,
                          scratch_shapes=[pltpu.VMEM((2, SCAN_SEG, lanes), F32)],
                          compiler_params=_params(("parallel",)))(*args)


def _ret_consts():
    h = np.arange(RET_HEADS, dtype=np.float32)
    log_gamma = np.log1p(-np.exp2(-5.0 - h)).astype(np.float32)
    idx = np.arange(BLOCK, dtype=np.float32)
    diff = idx[:, None] - idx[None, :]
    decay = np.where(diff >= 0, np.exp(log_gamma[:, None, None] * np.maximum(diff, 0.0)), 0.0).astype(np.float32)
    k_w = np.exp(log_gamma[:, None] * (BLOCK - 1 - idx)[None, :]).astype(np.float32)[:, :, None]
    q_w = np.exp(log_gamma[:, None] * (idx + 1.0)[None, :]).astype(np.float32)[:, :, None]
    chunk_decay = [float(v) for v in np.exp(log_gamma * BLOCK).astype(np.float32)]
    return jnp.asarray(decay), jnp.asarray(q_w), jnp.asarray(k_w), chunk_decay


def _ret_head(q, k, v, g, prev, decay, q_w, k_w, chunk_decay):
    inner_s = _dnt(q, k) * decay
    y = _dnn(inner_s, v) + _dnn(q * q_w, prev)
    mu = jnp.mean(y, axis=-1, keepdims=True)
    var = jnp.mean(jnp.square(y - mu), axis=-1, keepdims=True)
    yn = (y - mu) * lax.rsqrt(var + EPS)
    out = g * jax.nn.sigmoid(g) * yn
    return out, prev * chunk_decay + _dtn(k * k_w, v)


def _ret_prep(proj, cos, sin):
    def fn(rq, rk, c, s):
        return _rope(rq, c, s, 32), _rope(rk, c, s, 32) * (RET_DK ** -0.5)
    t_len = proj.shape[0]
    return _rowwise("ret_prep", fn, [_seg(proj, "rq"), _seg(proj, "rk"), cos, sin], [],
                    [((t_len, 256), F32), ((t_len, 256), F32)])


def _ret_unprep(dq, dk, cos, sin):
    def fn(dq_t, dk_t, c, s):
        return _rope(dq_t, c, -s, 32), _rope(dk_t * (RET_DK ** -0.5), c, -s, 32)
    t_len = dq.shape[0]
    return _rowwise("ret_unprep", fn, [dq, dk, cos, sin], [], [((t_len, 256), BF16), ((t_len, 256), BF16)])


def _ret_fwd(q, k, proj):
    t_len = q.shape[0]
    nc = t_len // BLOCK
    decay, q_w, k_w, chunk_decay = _ret_consts()

    def body(q_ref, k_ref, v_ref, g_ref, dec_ref, qw_ref, kw_ref, o_ref, prev_ref, state):
        @pl.when(pl.program_id(0) == 0)
        def _():
            state[...] = jnp.zeros_like(state)
        for h in range(RET_HEADS):
            qk, vv = slice(64 * h, 64 * h + 64), slice(128 * h, 128 * h + 128)
            prev = state[h]
            prev_ref[h] = prev
            out, nxt = _ret_head(q_ref[:, qk], k_ref[:, qk], v_ref[:, vv], g_ref[:, vv], prev,
                                 dec_ref[h], qw_ref[h], kw_ref[h], chunk_decay[h])
            o_ref[:, vv] = out.astype(o_ref.dtype)
            state[h] = nxt

    chunk = lambda w, c=0: pl.BlockSpec((BLOCK, w), lambda i, c=c: (i, c))
    return pl.pallas_call(
        body, name="ret_fwd", grid=(nc,),
        in_specs=[chunk(256), chunk(256), chunk(512, PROJ["rv"][0] // 512), chunk(512, PROJ["rg"][0] // 512),
                  _const_spec(decay.shape), _const_spec(q_w.shape), _const_spec(k_w.shape)],
        out_specs=[chunk(512), pl.BlockSpec((None, RET_HEADS, RET_DK, RET_DV), lambda i: (i, 0, 0, 0))],
        out_shape=[jax.ShapeDtypeStruct((t_len, 512), BF16), jax.ShapeDtypeStruct((nc, RET_HEADS, RET_DK, RET_DV), F32)],
        scratch_shapes=[pltpu.VMEM((RET_HEADS, RET_DK, RET_DV), F32)],
        compiler_params=_params(("arbitrary",)))(q, k, proj, proj, decay, q_w, k_w)


def _ret_bwd(q, k, proj, prevs, dy):
    t_len = q.shape[0]
    nc = t_len // BLOCK
    decay, q_w, k_w, chunk_decay = _ret_consts()

    def body(q_ref, k_ref, v_ref, g_ref, prev_ref, dy_ref, dec_ref, qw_ref, kw_ref, dq_ref, dk_ref, dv_ref, dg_ref, dstate):
        @pl.when(pl.program_id(0) == 0)
        def _():
            dstate[...] = jnp.zeros_like(dstate)
        for h in range(RET_HEADS):
            qk, vv = slice(64 * h, 64 * h + 64), slice(128 * h, 128 * h + 128)
            head = functools.partial(_ret_head, decay=dec_ref[h], q_w=qw_ref[h], k_w=kw_ref[h], chunk_decay=chunk_decay[h])
            _, vjp = jax.vjp(head, q_ref[:, qk], k_ref[:, qk], v_ref[:, vv], g_ref[:, vv], prev_ref[h])
            dq, dk, dv, dg, dprev = vjp((dy_ref[:, vv].astype(F32), dstate[h]))
            dq_ref[:, qk], dk_ref[:, qk] = dq, dk
            dv_ref[:, vv], dg_ref[:, vv] = dv.astype(dv_ref.dtype), dg.astype(dg_ref.dtype)
            dstate[h] = dprev

    chunk = lambda w, c=0: pl.BlockSpec((BLOCK, w), lambda i, c=c: (nc - 1 - i, c))
    return pl.pallas_call(
        body, name="ret_bwd", grid=(nc,),
        in_specs=[chunk(256), chunk(256), chunk(512, PROJ["rv"][0] // 512), chunk(512, PROJ["rg"][0] // 512),
                  pl.BlockSpec((None, RET_HEADS, RET_DK, RET_DV), lambda i: (nc - 1 - i, 0, 0, 0)), chunk(512),
                  _const_spec(decay.shape), _const_spec(q_w.shape), _const_spec(k_w.shape)],
        out_specs=[chunk(256), chunk(256), chunk(512), chunk(512)],
        out_shape=[jax.ShapeDtypeStruct((t_len, 256), F32), jax.ShapeDtypeStruct((t_len, 256), F32),
                   jax.ShapeDtypeStruct((t_len, 512), BF16), jax.ShapeDtypeStruct((t_len, 512), BF16)],
        scratch_shapes=[pltpu.VMEM((RET_HEADS, RET_DK, RET_DV), F32)],
        compiler_params=_params(("arbitrary",)))(q, k, proj, proj, prevs, dy, decay, q_w, k_w)


MLA_SCALE = 96 ** -0.5


def _mla_prep(qf, kvf, proj, cos, sin):
    def fn(q, kv, kr, c, s):
        q_rope = _rope(q[:, 512:768], c, s, 16)
        k_rope = _rope(kr, c, s, 16)[:, :32]
        zero = jnp.zeros_like(k_rope)
        qs = [jnp.concatenate([q[:, 64 * h:64 * h + 64], q_rope[:, 32 * h:32 * h + 32], zero], axis=1) for h in range(MLA_HEADS)]
        ks = [jnp.concatenate([kv[:, 64 * h:64 * h + 64], k_rope, zero], axis=1) for h in range(MLA_HEADS)]
        ones = (lax.broadcasted_iota(jnp.int32, (kv.shape[0], 64), 1) == 0).astype(F32)
        vs = [jnp.concatenate([kv[:, 512 + 64 * h:576 + 64 * h], ones], axis=1) for h in range(MLA_HEADS)]
        return qs, ks, vs
    t_len = qf.shape[0]
    return _rowwise("mla_prep", fn, [qf, kvf, _seg(proj, "kr"), cos, sin], [],
                    [((8, t_len, 128), BF16), ((8, t_len, 128), BF16), ((8, t_len, 128), BF16)])


def _mla_unprep(dq, dk, dv, cos, sin):
    def fn(dq_t, dk_t, dv_t, c, s):
        q_rope = _rope(jnp.concatenate([dq_t[h][:, 64:96] for h in range(MLA_HEADS)], axis=1), c, -s, 16)
        d_qf = jnp.concatenate([dq_t[h][:, :64] for h in range(MLA_HEADS)] + [q_rope], axis=1)
        d_kvf = jnp.concatenate([dk_t[h][:, :64] for h in range(MLA_HEADS)] + [dv_t[h][:, :64] for h in range(MLA_HEADS)], axis=1)
        k_rope = dk_t[0][:, 64:96]
        for h in range(1, MLA_HEADS):
            k_rope = k_rope + dk_t[h][:, 64:96]
        d_kr = _rope(jnp.concatenate([k_rope, jnp.zeros((k_rope.shape[0], 96), F32)], axis=1), c, -s, 16)
        return d_qf, d_kvf, d_kr
    t_len = dq.shape[1]
    return _rowwise("mla_unprep", fn, [dq, dk, dv, cos, sin], [],
                    [((t_len, 768), BF16), ((t_len, 1024), BF16), ((t_len, 128), BF16)])


MLA_EXP2 = MLA_SCALE * math.log2(math.e)
MLA_TILE = 1024


def _diag_mask(tq):
    return lax.broadcasted_iota(jnp.int32, (tq, tq), 1) <= lax.broadcasted_iota(jnp.int32, (tq, tq), 0)


def _tri_steps(nq, key_major):
    if key_major:
        pairs = [(i, j) for j in range(nq) for i in range(j, nq)]
    else:
        pairs = [(i, j) for i in range(nq) for j in range(i + 1)]
    return jnp.asarray([p[0] for p in pairs], jnp.int32), jnp.asarray([p[1] for p in pairs], jnp.int32)


def _ride(body, rider, counts, grid):
    if rider is None:
        return body
    extra = (0, len(rider.inputs), len(rider.out_shape), len(rider.sems))

    def wrapped(*refs):
        host, ridden, pos = [], [], 0
        for mine, theirs in zip(counts, extra):
            host += refs[pos:pos + mine]
            ridden.append(refs[pos + mine:pos + mine + theirs])
            pos += mine + theirs
        ids = [pl.program_id(a) for a in range(len(grid))]
        first, last = ids[0] == 0, ids[0] == grid[0] - 1
        for a in range(1, len(grid)):
            first, last = first & (ids[a] == 0), last & (ids[a] == grid[a] - 1)

        @pl.when(first)
        def _():
            rider.start(*ridden[1:])

        body(*host)

        @pl.when(last)
        def _():
            rider.finish(*ridden[1:])

    return wrapped


def _mla_fwd(q, k, v, rider=None):
    t_len = q.shape[1]
    tq = min(MLA_TILE, t_len)
    nq = t_len // tq
    qi, kj = _tri_steps(nq, False)

    def body(i_ref, j_ref, q_ref, k_ref, v_ref, o_ref, lse_ref, m_s, acc_s):
        step = pl.program_id(1)
        i, j = i_ref[step], j_ref[step]

        @pl.when(j == 0)
        def _():
            m_s[...] = jnp.full_like(m_s, NEG_INF)
            acc_s[...] = jnp.zeros_like(acc_s)

        def update(mask):
            for e in range(2):
                s = _raw_dot(q_ref[e], k_ref[e], NT)
                if mask is not None:
                    s = jnp.where(mask, s, NEG_INF)
                m_old = m_s[e]
                m_new = jnp.maximum(m_old, jnp.max(s, axis=-1, keepdims=True))
                p = jnp.exp2((s - m_new) * MLA_EXP2)
                acc_s[e] = jnp.exp2((m_old - m_new) * MLA_EXP2) * acc_s[e] + _raw_dot(p, v_ref[e], NN)
                m_s[e] = m_new

        @pl.when(j < i)
        def _():
            update(None)

        @pl.when(j == i)
        def _():
            update(_diag_mask(tq))
            outs = []
            for e in range(2):
                acc = acc_s[e]
                denom = acc[:, 64:65]
                outs.append(acc[:, :64] / denom)
                lse_ref[e] = m_s[e] * MLA_EXP2 + jnp.log2(denom)
            o_ref[...] = jnp.concatenate(outs, axis=1)

    grid = (MLA_HEADS // 2, int(qi.shape[0]))
    ride_in, ride_out, ride_sems = ([], [], []) if rider is None else (rider.inputs, rider.out_shape, rider.sems)
    grid_spec = pltpu.PrefetchScalarGridSpec(
        num_scalar_prefetch=2, grid=grid,
        in_specs=[pl.BlockSpec((2, tq, 128), lambda hp, s, qi, kj: (hp, qi[s], 0)),
                  pl.BlockSpec((2, tq, 128), lambda hp, s, qi, kj: (hp, kj[s], 0)),
                  pl.BlockSpec((2, tq, 128), lambda hp, s, qi, kj: (hp, kj[s], 0))] + [ANY_SPEC] * len(ride_in),
        out_specs=[pl.BlockSpec((tq, 128), lambda hp, s, qi, kj: (qi[s], hp)),
                   pl.BlockSpec((2, tq, 1), lambda hp, s, qi, kj: (hp, qi[s], 0))] + [ANY_SPEC] * len(ride_out),
        scratch_shapes=[pltpu.VMEM((2, tq, 1), F32), pltpu.VMEM((2, tq, 128), F32)] + ride_sems)
    res = pl.pallas_call(
        _ride(body, rider, (2, 3, 2, 2), grid), name="mla_fwd" if rider is None else "mla_fwd_gather", grid_spec=grid_spec,
        out_shape=[jax.ShapeDtypeStruct((t_len, 512), F32), jax.ShapeDtypeStruct((MLA_HEADS, t_len, 1), F32)] + ride_out,
        compiler_params=_params(("parallel" if rider is None else "arbitrary", "arbitrary")))(qi, kj, q, k, v, *ride_in)
    return res[0], res[1], res[2:]


def _mla_bwd_prep(o, dy):
    def fn(o_t, dy_t):
        dos, deltas = [], []
        for h in range(MLA_HEADS):
            d_h = dy_t[:, 64 * h:64 * h + 64].astype(F32)
            dos.append(jnp.concatenate([d_h, jnp.zeros_like(d_h)], axis=1))
            deltas.append(jnp.sum(d_h * o_t[:, 64 * h:64 * h + 64], axis=-1, keepdims=True))
        return dos, deltas
    t_len = o.shape[0]
    return _rowwise("mla_bwd_prep", fn, [o, dy], [], [((8, t_len, 128), BF16), ((8, t_len, 1), F32)])


def _mla_bwd(q, k, v, do, lse2, delta, rider=None):
    t_len = q.shape[1]
    tq = min(MLA_TILE, t_len)
    nq = t_len // tq
    qi, kj = _tri_steps(nq, True)

    def body(i_ref, j_ref, q_ref, k_ref, v_ref, do_ref, lse_ref, dl_ref, dq_ref, dk_ref, dv_ref, k_acc, v_acc):
        step = pl.program_id(1)
        i, j = i_ref[step], j_ref[step]

        @pl.when(step == 0)
        def _():
            dq_ref[...] = jnp.zeros_like(dq_ref)

        @pl.when(i == j)
        def _():
            k_acc[...] = jnp.zeros_like(k_acc)
            v_acc[...] = jnp.zeros_like(v_acc)

        def tile(mask):
            p = jnp.exp2(_raw_dot(q_ref[...], k_ref[...], NT) * MLA_EXP2 - lse_ref[...])
            if mask is not None:
                p = jnp.where(mask, p, 0.0)
            ds = p * (_raw_dot(do_ref[...], v_ref[...], NT) - dl_ref[...]) * MLA_SCALE
            v_acc[...] += _raw_dot(p, do_ref[...], TN)
            k_acc[...] += _raw_dot(ds, q_ref[...], TN)
            rows = pl.ds(pl.multiple_of(i * tq, tq), tq)
            dq_ref[rows, :] += _raw_dot(ds, k_ref[...], NN)

        @pl.when(i == j)
        def _():
            tile(_diag_mask(tq))

        @pl.when(i > j)
        def _():
            tile(None)

        @pl.when(i == nq - 1)
        def _():
            dk_ref[...] = k_acc[...]
            dv_ref[...] = v_acc[...]

    q_blk = lambda w: pl.BlockSpec((None, tq, w), lambda h, s, qi, kj: (h, qi[s], 0))
    k_blk = lambda w: pl.BlockSpec((None, tq, w), lambda h, s, qi, kj: (h, kj[s], 0))
    grid = (MLA_HEADS, int(qi.shape[0]))
    ride_in, ride_out, ride_sems = ([], [], []) if rider is None else (rider.inputs, rider.out_shape, rider.sems)
    grid_spec = pltpu.PrefetchScalarGridSpec(
        num_scalar_prefetch=2, grid=grid,
        in_specs=[q_blk(128), k_blk(128), k_blk(128), q_blk(128), q_blk(1), q_blk(1)] + [ANY_SPEC] * len(ride_in),
        out_specs=[pl.BlockSpec((None, t_len, 128), lambda h, s, qi, kj: (h, 0, 0)), k_blk(128), k_blk(128)]
        + [ANY_SPEC] * len(ride_out),
        scratch_shapes=[pltpu.VMEM((tq, 128), F32), pltpu.VMEM((tq, 128), F32)] + ride_sems)
    res = pl.pallas_call(
        _ride(body, rider, (2, 6, 3, 2), grid), name="mla_bwd" if rider is None else "mla_bwd_scatter", grid_spec=grid_spec,
        out_shape=[jax.ShapeDtypeStruct((MLA_HEADS, t_len, 128), F32)] * 3 + ride_out,
        compiler_params=_params(("parallel" if rider is None else "arbitrary", "arbitrary")))(qi, kj, q, k, v, do, lse2, delta, *ride_in)
    return res[0], res[1], res[2], res[3:]


MERGE_TM = 1024
MERGE_TN = 256
GATE_BLOCK0 = PROJ["gates"][0] // MERGE_TN


def _merge_specs(tm):
    y_spec = pl.BlockSpec((tm, 512), lambda i, j: (i, 0))
    w_spec = pl.BlockSpec((512, MERGE_TN), lambda i, j: (0, j))
    gate = lambda b: pl.BlockSpec((tm, MERGE_TN), lambda i, j, b=b: (i, GATE_BLOCK0 + 4 * b + j))
    return [y_spec] * 4 + [w_spec] * 4 + [gate(b) for b in range(4)]


def _merge_fwd(ys, wbs, proj):
    t_len = proj.shape[0]
    tm = min(MERGE_TM, t_len)

    def body(*refs):
        y, w, gl, o_ref = refs[0:4], refs[4:8], refs[8:12], refs[12]
        acc = jnp.zeros((tm, MERGE_TN), F32)
        for b in range(4):
            acc += jax.nn.sigmoid(gl[b][...]) * _raw_dot(y[b][...], w[b][...], NN)
        o_ref[...] = acc.astype(o_ref.dtype)

    return pl.pallas_call(body, name="merge_fwd", grid=(t_len // tm, D_MODEL // MERGE_TN), in_specs=_merge_specs(tm),
                          out_specs=pl.BlockSpec((tm, MERGE_TN), lambda i, j: (i, j)),
                          out_shape=jax.ShapeDtypeStruct((t_len, D_MODEL), BF16),
                          compiler_params=_params(("parallel", "parallel")))(*ys, *wbs, *([proj] * 4))


def _merge_bwd(ys, wbs, proj, dmerged):
    t_len = proj.shape[0]
    tm = min(MERGE_TM, t_len)

    def body(*refs):
        y, w, gl, dm_ref, dgl, dt = refs[0:4], refs[4:8], refs[8:12], refs[12], refs[13:17], refs[17:21]
        dm = dm_ref[...]
        for b in range(4):
            gate = jax.nn.sigmoid(gl[b][...])
            t_b = _raw_dot(y[b][...], w[b][...], NN)
            dgl[b][...] = (dm * t_b * gate * (1.0 - gate)).astype(BF16)
            dt[b][...] = (dm * gate).astype(BF16)

    o_spec = pl.BlockSpec((tm, MERGE_TN), lambda i, j: (i, j))
    return pl.pallas_call(body, name="merge_bwd", grid=(t_len // tm, D_MODEL // MERGE_TN), in_specs=_merge_specs(tm) + [o_spec],
                          out_specs=[o_spec] * 8, out_shape=[jax.ShapeDtypeStruct((t_len, D_MODEL), BF16)] * 8,
                          compiler_params=_params(("parallel", "parallel")))(*ys, *wbs, *([proj] * 4), dmerged)


def _adamw(name, w, g, m, v, slabs):
    shape = w.shape
    size = int(np.prod(shape))
    if shape[-1] >= 96:
        view = (size // shape[-1], shape[-1])
    elif size % 128 == 0:
        view = (size // 128, 128)
    else:
        view = (1, size)
    rows = view[0]
    tile = rows
    for cand in (512, 256, 128, 64):
        if rows > cand and rows % cand == 0 and cand * view[1] * 4 <= 2 ** 20:
            tile = cand
            break
    if rows * view[1] * 4 <= 2 ** 20:
        tile = rows
    c1, c2 = 1.0 - ADAM_B1 ** ADAM_STEP, 1.0 - ADAM_B2 ** ADAM_STEP

    def body(w_ref, g_ref, m_ref, v_ref, og_ref, od_ref, om_ref, ov_ref):
        if slabs:
            grad = g_ref[0].astype(F32)
            for d in range(1, slabs):
                grad = grad + g_ref[d].astype(F32)
        else:
            grad = g_ref[...]
        m_new = ADAM_B1 * m_ref[...] + (1.0 - ADAM_B1) * grad
        v_new = ADAM_B2 * v_ref[...] + (1.0 - ADAM_B2) * jnp.square(grad)
        og_ref[...] = grad
        od_ref[...] = -ADAM_LR * ((m_new / c1) / (jnp.sqrt(v_new / c2) + ADAM_EPS) + ADAM_WD * w_ref[...])
        om_ref[...] = m_new
        ov_ref[...] = v_new

    spec = pl.BlockSpec((tile, view[1]), lambda i: (i, 0))
    g_spec = pl.BlockSpec((slabs, tile, view[1]), lambda i: (0, i, 0)) if slabs else spec
    g_view = g.reshape((slabs,) + view) if slabs else g.reshape(view)
    outs = pl.pallas_call(body, name=name, grid=(rows // tile,), in_specs=[spec, g_spec, spec, spec], out_specs=[spec] * 4,
                          out_shape=[jax.ShapeDtypeStruct(view, F32)] * 4,
                          compiler_params=_params(("parallel",)))(w.reshape(view), g_view, m.reshape(view), v.reshape(view))
    return [o.reshape(shape) for o in outs]


def _sum_slabs(name, x):
    def body(x_ref, o_ref):
        acc = x_ref[0]
        for d in range(1, N_DEV):
            acc = acc + x_ref[d]
        o_ref[...] = acc
    return pl.pallas_call(body, name=name, out_shape=jax.ShapeDtypeStruct(x.shape[1:], x.dtype))(x)


MESH_ID = pl.DeviceIdType.MESH
ANY_SPEC = pl.BlockSpec(memory_space=pl.ANY)


def _remote(src, dst, send_sems, recv_sems, k, to):
    return pltpu.make_async_remote_copy(src_ref=src, dst_ref=dst, send_sem=send_sems.at[k], recv_sem=recv_sems.at[k],
                                        device_id=to, device_id_type=MESH_ID)


class _Rider:
    def __init__(self, inputs, out_shape, sems, start, finish):
        self.inputs, self.out_shape, self.sems, self.start, self.finish = list(inputs), list(out_shape), list(sems), start, finish


def _run_rider(name, rider):
    n_in, n_out = len(rider.inputs), len(rider.out_shape)

    def body(*refs):
        parts = refs[:n_in], refs[n_in:n_in + n_out], refs[n_in + n_out:]
        rider.start(*parts)
        rider.finish(*parts)

    return pl.pallas_call(body, name=name, in_specs=[ANY_SPEC] * n_in, out_specs=[ANY_SPEC] * n_out, out_shape=rider.out_shape,
                          scratch_shapes=rider.sems, compiler_params=pltpu.CompilerParams(has_side_effects=True))(*rider.inputs)


def _gather_rider(blocks):
    n = len(blocks)

    def plan(srcs, dsts, sems, waits):
        send_sems, recv_sems, local_sems = sems
        x, y, c = lax.axis_index("x"), lax.axis_index("y"), lax.axis_index("c")
        sibling = (x, y, 1 - c)
        chips = [(1 - x, y), (x, 1 - y), (1 - x, 1 - y)]
        index = lambda px, py, pc: 4 * px + 2 * py + pc

        def copy(t, k, block, to, src=None):
            slot = dsts[t].at[block]
            return _remote(slot if src is None else src, slot, send_sems, recv_sems, 7 * t + k, to)

        me = index(x, y, c)
        own = [pltpu.make_async_copy(srcs[t], dsts[t].at[me], local_sems.at[t]) for t in range(n)]
        first = []
        for t in range(n):
            first.append(copy(t, 0, me, sibling, src=srcs[t]))
            first += [copy(t, 1 + j, me, (*chip, c), src=srcs[t]) for j, chip in enumerate(chips)]
        if not waits:
            return own, first, [], []
        relays = [(copy(t, 1 + j, index(*chip, c), sibling), copy(t, 4 + j, index(*chip, c), sibling))
                  for j, chip in enumerate(chips) for t in range(n)]
        from_sibling = [copy(t, 0, index(x, y, 1 - c), sibling) for t in range(n)]
        from_sibling += [copy(t, 4 + j, index(*chip, 1 - c), sibling) for t in range(n) for j, chip in enumerate(chips)]
        return own, first, relays, from_sibling

    def start(srcs, dsts, sems):
        own, first, _, _ = plan(srcs, dsts, sems, False)
        for cp in own + first:
            cp.start()

    def finish(srcs, dsts, sems):
        own, first, relays, from_sibling = plan(srcs, dsts, sems, True)
        for arrival, forward in relays:
            arrival.wait_recv()
            forward.start()
        for cp in from_sibling:
            cp.wait_recv()
        for cp in first + [forward for _, forward in relays]:
            cp.wait_send()
        for cp in own:
            cp.wait()

    return _Rider(blocks, [jax.ShapeDtypeStruct((N_DEV,) + b.shape, b.dtype) for b in blocks],
                  [pltpu.SemaphoreType.DMA((7 * n,)), pltpu.SemaphoreType.DMA((7 * n,)), pltpu.SemaphoreType.DMA((n,))],
                  start, finish)


def _pair_exchange(name, blocks):
    n = len(blocks)

    def body(*refs):
        srcs, dsts = refs[:n], refs[n:2 * n]
        send_sems, recv_sems = refs[2 * n:]
        sibling = (lax.axis_index("x"), lax.axis_index("y"), 1 - lax.axis_index("c"))
        copies = [_remote(srcs[t], dsts[t], send_sems, recv_sems, t, sibling) for t in range(n)]
        for cp in copies:
            cp.start()
        for cp in copies:
            cp.wait()

    return pl.pallas_call(
        body, name=name, in_specs=[ANY_SPEC] * n, out_specs=[ANY_SPEC] * n,
        out_shape=[jax.ShapeDtypeStruct(b.shape, b.dtype) for b in blocks],
        scratch_shapes=[pltpu.SemaphoreType.DMA((n,)), pltpu.SemaphoreType.DMA((n,))],
        compiler_params=pltpu.CompilerParams(has_side_effects=True))(*blocks)


def _chip_rider(items):
    n = len(items)
    out_shape = [jax.ShapeDtypeStruct((4,) + tuple(a.shape[1:] if per_chip else a.shape), a.dtype) for a, per_chip in items]

    def plan(srcs, dsts, sems, waits):
        send_sems, recv_sems, local_sems = sems
        x, y, c = lax.axis_index("x"), lax.axis_index("y"), lax.axis_index("c")
        mine = 2 * x + y
        own = [pltpu.make_async_copy(srcs[t].at[mine] if per_chip else srcs[t], dsts[t].at[mine], local_sems.at[t])
               for t, (_, per_chip) in enumerate(items)]
        sends, arrivals = [], []
        for k in range(1, 4):
            px, py = x ^ (k >> 1), y ^ (k & 1)
            peer = 2 * px + py
            for t, (_, per_chip) in enumerate(items):
                src = srcs[t].at[peer] if per_chip else srcs[t]
                sends.append(_remote(src, dsts[t].at[mine], send_sems, recv_sems, 3 * t + k - 1, (px, py, c)))
                if waits:
                    arrivals.append(_remote(src, dsts[t].at[peer], send_sems, recv_sems, 3 * t + k - 1, (px, py, c)))
        return own, sends, arrivals

    def start(srcs, dsts, sems):
        own, sends, _ = plan(srcs, dsts, sems, False)
        for cp in own + sends:
            cp.start()

    def finish(srcs, dsts, sems):
        own, sends, arrivals = plan(srcs, dsts, sems, True)
        for out, arrival in zip(sends, arrivals):
            out.wait_send()
            arrival.wait_recv()
        for cp in own:
            cp.wait()

    return _Rider([a for a, _ in items], out_shape,
                  [pltpu.SemaphoreType.DMA((3 * n,)), pltpu.SemaphoreType.DMA((3 * n,)), pltpu.SemaphoreType.DMA((n,))],
                  start, finish)


def _band_diag(p):
    a, b = p.shape[1:]
    eye = jnp.eye(8, dtype=p.dtype)
    return (p.reshape(4, 8, a, 1, b) * eye[None, :, None, :, None]).reshape(4, 8 * a, 8 * b)


def _band_diag_extract(m, a, b):
    eye = jnp.eye(8, dtype=m.dtype)
    return (m.reshape(4, 8, a, 8, b) * eye[None, :, None, :, None]).sum(axis=3).reshape(32, a, b)


def _proj_cols(w_ref_layout):
    parts = []
    for name in PROJ_ORDER:
        off, width = REF_SPLIT[name]
        part = w_ref_layout[:, off:off + width]
        if width < PROJ[name][1]:
            part = jnp.pad(part, ((0, 0), (0, PROJ[name][1] - width)))
        parts.append(part)
    return jnp.concatenate(parts, axis=1)


def _unproj_cols(w_proj_layout):
    parts = []
    for name in REF_ORDER:
        parts.append(w_proj_layout[:, PROJ[name][0]:PROJ[name][0] + REF_SPLIT[name][1]])
    return jnp.concatenate(parts, axis=1)


def _uq_cols(w):
    w3 = w.reshape(w.shape[0], MLA_HEADS, 96)
    return jnp.concatenate([w3[:, :, :64].reshape(-1, 512), w3[:, :, 64:].reshape(-1, 256)], axis=1)


def _un_uq_cols(w):
    return jnp.concatenate([w[:, :512].reshape(-1, MLA_HEADS, 64), w[:, 512:].reshape(-1, MLA_HEADS, 32)], axis=2).reshape(-1, 768)


def _ukv_cols(w):
    w3 = w.reshape(w.shape[0], MLA_HEADS, 128)
    return jnp.concatenate([w3[:, :, :64].reshape(-1, 512), w3[:, :, 64:].reshape(-1, 512)], axis=1)


def _un_ukv_cols(w):
    return jnp.concatenate([w[:, :512].reshape(-1, MLA_HEADS, 64), w[:, 512:].reshape(-1, MLA_HEADS, 64)], axis=2).reshape(-1, 1024)


def _cols_gathered(g):
    return g.transpose(1, 0, 2).reshape(g.shape[1], -1)


def _cols_slabs(w):
    r, c = w.shape
    return w.reshape(r, N_DEV, c // N_DEV).transpose(1, 0, 2)


def _row(v):
    return v.reshape(1, -1)


def _s5_weights(p):
    flat = dict(lam_re=p["s5_lam_re"].reshape(S5_CH, 1), lam_im=p["s5_lam_im"].reshape(S5_CH, 1),
                log_dt=jnp.repeat(p["s5_log_dt"], S5_STATE).reshape(S5_CH, 1),
                b_re=p["s5_b_re"].reshape(S5_CH, S5_GROUP), b_im=p["s5_b_im"].reshape(S5_CH, S5_GROUP))
    ab_re, ab_im, bb_re, bb_im = _s5_param_fwd(flat["lam_re"], flat["lam_im"], flat["log_dt"], flat["b_re"], flat["b_im"])
    w_bu = jnp.concatenate([_band_diag(bb.reshape(32, 64, 16).transpose(0, 2, 1)) for bb in (bb_re, bb_im)], axis=0)
    w_y = jnp.concatenate([_band_diag(p["s5_c_re"].transpose(0, 2, 1)), -_band_diag(p["s5_c_im"].transpose(0, 2, 1))], axis=0)
    return dict(flat=flat,
                a_re=jnp.broadcast_to(ab_re.reshape(1, S5_CH), (SCAN_SEG, S5_CH)),
                a_im=jnp.broadcast_to(ab_im.reshape(1, S5_CH), (SCAN_SEG, S5_CH)),
                w_bu=w_bu.astype(BF16), w_bu_t=w_bu.swapaxes(1, 2).astype(BF16),
                w_y=w_y.astype(BF16), w_y_t=w_y.swapaxes(1, 2).astype(BF16))


def _layer_fwd(x, w, p, tabs, late_shards=None, rider=None):
    t_len = x.shape[0]
    cos64, sin64, cos32, sin32 = tabs
    s = {}
    (s["h1"],) = _rowwise("pre_mix", lambda xt, g: (_rms(xt, g),), [x], [_row(p["g_pre_mix"])], [((t_len, D_MODEL), BF16)])
    if late_shards is None:
        proj = _mm("proj", s["h1"], w["w_in"], "nn")
    else:
        proj, late = _mm("proj_gather", s["h1"], w["w_in"], "nn", rider=_gather_rider(late_shards))
        w.update(_late_weights(late))
    s["proj"] = proj
    s["qa"], s["ka"], s["va"] = _swa_prep(proj, cos64, sin64)
    s["sinks"] = jnp.broadcast_to(p["swa_sinks"][:, None], (8, 128))
    y_a = _swa_fwd(s["qa"], s["ka"], s["va"], s["sinks"])
    s5 = s["s5"] = _s5_weights(p)
    bu, s["u_p"] = _s5_expand("s5_bu", proj, PROJ["su"][0] // 128, s5["w_bu"])
    s["xs"], s["cin"] = _s5_scan("s5_scan_fwd", bu, s5["a_re"], s5["a_im"])
    s["y_s"] = _s5_contract("s5_y", s["xs"], s5["w_y"])
    (s["z"],) = _rowwise("s5_gelu", lambda y, u, d: (jax.nn.gelu(y + d * u),), [s["y_s"], _seg(proj, "su")],
                         [_row(p["s5_d"])], [((t_len, 512), BF16)])
    s["zz"] = _mm("s5_glu_mm", s["z"], w["w_glu"], "nn")
    (y_b,) = _rowwise("s5_glu", lambda t: (t[:, :512] * jax.nn.sigmoid(t[:, 512:]),), [s["zz"]], [], [((t_len, 512), BF16)])
    s["qc"], s["kc"] = _ret_prep(proj, cos64, sin64)
    y_c, s["prevs"] = _ret_fwd(s["qc"], s["kc"], proj)
    s["cqn"], s["ckvn"] = _rowwise("mla_norm", lambda cq, ckv, gq, gkv: (_rms(cq, gq), _rms(ckv, gkv)),
                                   [_seg(proj, "cq"), _seg(proj, "ckv")], [_row(p["mla_g_q"]), _row(p["mla_g_kv"])],
                                   [((t_len, 256), BF16), ((t_len, 128), BF16)])
    qf = _mm("mla_uq", s["cqn"], w["w_uq"], "nn")
    kvf = _mm("mla_ukv", s["ckvn"], w["w_ukv"], "nn")
    s["Q"], s["K"], s["V"] = _mla_prep(qf, kvf, proj, cos32, sin32)
    y_d, s["lse"], ridden = _mla_fwd(s["Q"], s["K"], s["V"], rider)
    s["ys"] = [y_a, y_b, y_c, y_d]
    s["merged"] = _merge_fwd(s["ys"], w["wb"], proj)
    s["o"] = _mm("out_mm", s["merged"], w["w_out"], "nn")

    def post_mix(xt, ot, g1, g2):
        x1 = xt + _rms(ot, g1)
        return x1, _rms(x1, g2)
    s["x1"], s["h2"] = _rowwise("post_mix", post_mix, [x, s["o"]], [_row(p["g_post_mix"]), _row(p["g_pre_mlp"])],
                                [((t_len, D_MODEL), F32), ((t_len, D_MODEL), BF16)])
    s["a"], s["r"] = _mm("ff1", s["h2"], w["w_ff1"], "nn", out_dtypes=(F32, BF16),
                         epi=lambda acc: (acc, jnp.square(jnp.maximum(acc, 0.0))))
    s["f"] = _mm("ff2", s["r"], w["w_ff2"], "nn")
    (x2,) = _rowwise("post_mlp", lambda x1, f, g: (x1 + _rms(f, g),), [s["x1"], s["f"]], [_row(p["g_post_mlp"])],
                     [((t_len, D_MODEL), F32)])
    return x2, s, ridden


def _layer_bwd(x, dx2, w, p, s, tabs, pending=None, exchange_name=None):
    t_len = x.shape[0]
    cos64, sin64, cos32, sin32 = tabs
    proj = s["proj"]
    big, small = {}, {}

    def post_mlp_bwd(f, d, g):
        df, dg = _rms_bwd(f, g, d)
        return (df,), (dg,)
    df, small["g_post_mlp"] = _rowwise("post_mlp_bwd", post_mlp_bwd, [s["f"], dx2], [_row(p["g_post_mlp"])],
                                       [((t_len, D_MODEL), BF16)], [((1, D_MODEL), F32)])
    da = _mm("ff2_dx", df, w["w_ff2"], "nt", out_dtypes=(BF16,), epi=lambda acc, a: (acc * 2.0 * jnp.maximum(a, 0.0),),
             epi_in=(s["a"],))
    big["w_ff2"] = _mm("ff2_dw", s["r"], df, "tn")
    dh2 = _mm("ff1_dx", da, w["w_ff1"], "nt")
    big["w_ff1"] = _mm("ff1_dw", s["h2"], da, "tn")

    def post_mix_bwd(x1, o, dh, d2, g1, g2):
        dx1_n, dg2 = _rms_bwd(x1, g2, dh)
        dx1 = d2 + dx1_n
        do, dg1 = _rms_bwd(o, g1, dx1)
        return (dx1, do), (dg1, dg2)
    dx1, do, small["g_post_mix"], small["g_pre_mlp"] = _rowwise(
        "post_mix_bwd", post_mix_bwd, [s["x1"], s["o"], dh2, dx2], [_row(p["g_post_mix"]), _row(p["g_pre_mlp"])],
        [((t_len, D_MODEL), F32), ((t_len, D_MODEL), BF16)], [((1, D_MODEL), F32), ((1, D_MODEL), F32)])
    dmerged = _mm("out_dx", do, w["w_out"], "nt")
    big["w_out"] = _mm("out_dw", s["merged"], do, "tn")
    rider, n_pending = None, len(pending or ())
    if exchange_name is not None:
        late_sums, _ = _pair_stage(exchange_name, _grad_slabs(big, LATE))
        rider = _chip_rider([(part, True) for part in list(pending or ()) + late_sums])

    res = _merge_bwd(s["ys"], w["wb"], proj, dmerged)
    dgl, dts = res[:4], res[4:]
    dys = [_mm("branch_dx", dts[b], w["wb"][b], "nt") for b in range(4)]
    big["wb"] = [_mm("branch_dw", s["ys"][b], dts[b], "tn") for b in range(4)]
    seg = {}

    dqa, dka, dva, dka_prev, dva_prev, dsink = _swa_bwd(s["qa"], s["ka"], s["va"], s["sinks"], dys[0])
    seg["sq"], seg["sk"], seg["sv"] = _swa_unprep(dqa, dka, dva, dka_prev, dva_prev, cos64, sin64)
    small["swa_sinks"] = dsink[:, 0]

    def glu_bwd(t, d):
        za, sg = t[:, :512], jax.nn.sigmoid(t[:, 512:])
        return (jnp.concatenate([d * sg, d * za * sg * (1.0 - sg)], axis=1),)
    (dzz,) = _rowwise("s5_glu_bwd", glu_bwd, [s["zz"], dys[1]], [], [((t_len, 1024), BF16)])
    dz = _mm("s5_glu_dx", dzz, w["w_glu"], "nt")
    big["w_glu"] = _mm("s5_glu_dw", s["z"], dzz, "tn")

    def gelu_bwd(y, u, d, skip):
        _, vjp = jax.vjp(jax.nn.gelu, y + skip * u)
        (dy2,) = vjp(d)
        return (dy2, dy2 * skip), (jnp.sum(dy2 * u, axis=0, keepdims=True),)
    dy_s, du_skip, d_skip = _rowwise("s5_gelu_bwd", gelu_bwd, [s["y_s"], _seg(proj, "su"), dz], [_row(p["s5_d"])],
                                     [((t_len, 512), F32), ((t_len, 512), F32)], [((1, 512), F32)])
    small["s5_d"] = d_skip[0]
    s5 = s["s5"]
    dlam_in, dy_p = _s5_expand("s5_dx_in", dy_s, 0, s5["w_y_t"])
    lam, da_re, da_im = _s5_scan("s5_scan_bwd", dlam_in, s5["a_re"], s5["a_im"], states=s["xs"], carry_in=s["cin"])
    du_s5 = _s5_contract("s5_du", lam, s5["w_bu_t"])
    d_w_bu = _s5_band_grad("s5_dbu", s["u_p"], lam, True)
    d_w_y = _s5_band_grad("s5_dc", dy_p, s["xs"], False)
    (seg["su"],) = _rowwise("s5_du_sum", lambda a, b: (a + b,), [du_s5, du_skip], [], [((t_len, 512), BF16)])
    d_bb_re = _band_diag_extract(d_w_bu[:4], 16, 64).transpose(0, 2, 1).reshape(S5_CH, S5_GROUP)
    d_bb_im = _band_diag_extract(d_w_bu[4:], 16, 64).transpose(0, 2, 1).reshape(S5_CH, S5_GROUP)
    small["s5_c_re"] = _band_diag_extract(d_w_y[:4], 64, 16).transpose(0, 2, 1)
    small["s5_c_im"] = -_band_diag_extract(d_w_y[4:], 64, 16).transpose(0, 2, 1)
    fl = s5["flat"]
    group_sum = jnp.repeat(jnp.eye(S5_GROUPS, dtype=F32), S5_STATE, axis=1)
    d_lr, d_li, d_ldt, d_br, d_bi = _s5_param_bwd(fl["lam_re"], fl["lam_im"], fl["log_dt"], fl["b_re"], fl["b_im"], group_sum,
                                                  da_re.reshape(S5_CH, 1), da_im.reshape(S5_CH, 1), d_bb_re, d_bb_im)
    small["s5_lam_re"], small["s5_lam_im"] = d_lr.reshape(32, 64), d_li.reshape(32, 64)
    small["s5_log_dt"] = d_ldt[:, 0]
    small["s5_b_re"], small["s5_b_im"] = d_br.reshape(32, 64, 16), d_bi.reshape(32, 64, 16)

    dqc, dkc, seg["rv"], seg["rg"] = _ret_bwd(s["qc"], s["kc"], proj, s["prevs"], dys[2])
    seg["rq"], seg["rk"] = _ret_unprep(dqc, dkc, cos64, sin64)

    do_h, delta = _mla_bwd_prep(s["ys"][3], dys[3])
    d_q, d_k, d_v, ridden = _mla_bwd(s["Q"], s["K"], s["V"], do_h, s["lse"], delta, rider)
    dqf, dkvf, seg["kr"] = _mla_unprep(d_q, d_k, d_v, cos32, sin32)
    dcqn = _mm("mla_uq_dx", dqf, w["w_uq"], "nt")
    big["w_uq"] = _mm("mla_uq_dw", s["cqn"], dqf, "tn")
    dckvn = _mm("mla_ukv_dx", dkvf, w["w_ukv"], "nt")
    big["w_ukv"] = _mm("mla_ukv_dw", s["ckvn"], dkvf, "tn")

    def mla_norm_bwd(cq, ckv, d1, d2, gq, gkv):
        dcq, dgq = _rms_bwd(cq, gq, d1)
        dckv, dgkv = _rms_bwd(ckv, gkv, d2)
        return (dcq, dckv), (dgq, dgkv)
    seg["cq"], seg["ckv"], dgq, dgkv = _rowwise(
        "mla_norm_bwd", mla_norm_bwd, [_seg(proj, "cq"), _seg(proj, "ckv"), dcqn, dckvn],
        [_row(p["mla_g_q"]), _row(p["mla_g_kv"])], [((t_len, 256), BF16), ((t_len, 128), BF16)],
        [((1, 256), F32), ((1, 128), F32)])
    small["mla_g_q"], small["mla_g_kv"] = dgq[0], dgkv[0]

    dproj = jnp.concatenate([seg[name] for name in PROJ_ORDER[:-1]] + list(dgl), axis=1)
    dh1 = _mm("proj_dx", dproj, w["w_in"], "nt")
    big["w_in"] = _mm("proj_dw", s["h1"], dproj, "tn")

    def pre_mix_bwd(xt, dh, d1, g):
        dxn, dg = _rms_bwd(xt, g, dh)
        return (d1 + dxn,), (dg,)
    dx, small["g_pre_mix"] = _rowwise("pre_mix_bwd", pre_mix_bwd, [x, dh1, dx1], [_row(p["g_pre_mix"])],
                                      [((t_len, D_MODEL), F32)], [((1, D_MODEL), F32)])
    for name in ("g_pre_mix", "g_post_mix", "g_pre_mlp", "g_post_mlp"):
        small[name] = small[name][0]
    return dx, big, small, ridden[:n_pending], ridden[n_pending:]


BIG = ("w_in", "s5_w_glu", "mla_w_uq", "mla_w_ukv", "w_branch", "w_out", "w_ff1", "w_ff2")
SMALL = ("g_pre_mix", "g_post_mix", "g_pre_mlp", "g_post_mlp", "swa_sinks", "s5_lam_re", "s5_lam_im", "s5_log_dt",
         "s5_b_re", "s5_b_im", "s5_c_re", "s5_c_im", "s5_d", "mla_g_q", "mla_g_kv")
WEIGHTS = ("g_pre_mix", "g_post_mix", "g_pre_mlp", "g_post_mlp", "w_in", "swa_sinks", "s5_lam_re", "s5_lam_im", "s5_log_dt",
           "s5_b_re", "s5_b_im", "s5_c_re", "s5_c_im", "s5_d", "s5_w_glu", "mla_g_q", "mla_g_kv", "mla_w_uq", "mla_w_ukv",
           "w_branch", "w_out", "w_ff1", "w_ff2")


EARLY = ("w_in", "s5_w_glu", "mla_w_uq", "mla_w_ukv", "w_branch")
LATE = ("w_out", "w_ff1", "w_ff2")


def _early_weights(shards):
    g = dict(zip(EARLY, shards))
    wb = g["w_branch"].transpose(1, 2, 0, 3).reshape(4, 512, D_MODEL)
    return dict(w_in=_proj_cols(_cols_gathered(g["w_in"])), w_glu=_cols_gathered(g["s5_w_glu"]),
                w_uq=_uq_cols(_cols_gathered(g["mla_w_uq"])), w_ukv=_ukv_cols(_cols_gathered(g["mla_w_ukv"])),
                wb=[wb[b] for b in range(4)])


def _late_weights(shards):
    g = dict(zip(LATE, shards))
    return dict(w_out=g["w_out"].reshape(D_MODEL, D_MODEL), w_ff1=_cols_gathered(g["w_ff1"]),
                w_ff2=g["w_ff2"].reshape(4 * D_MODEL, D_MODEL))


def _grad_slabs(big, names):
    make = dict(w_in=lambda: _cols_slabs(_unproj_cols(big["w_in"])), s5_w_glu=lambda: _cols_slabs(big["w_glu"]),
                mla_w_uq=lambda: _cols_slabs(_un_uq_cols(big["w_uq"])), mla_w_ukv=lambda: _cols_slabs(_un_ukv_cols(big["w_ukv"])),
                w_branch=lambda: jnp.stack(big["wb"]).reshape(4, 512, N_DEV, D_MODEL // N_DEV).transpose(2, 0, 1, 3),
                w_out=lambda: big["w_out"].reshape(N_DEV, D_MODEL // N_DEV, D_MODEL),
                w_ff1=lambda: _cols_slabs(big["w_ff1"]), w_ff2=lambda: big["w_ff2"].reshape(N_DEV, 4 * D_MODEL // N_DEV, D_MODEL))
    return [make[name]() for name in names]


def _pair_stage(name, slabs, extra=()):
    core = lax.axis_index("c")
    own, send = [], []
    for slab in slabs:
        by_chip = slab.reshape((4, 2) + slab.shape[1:])
        own.append(lax.dynamic_index_in_dim(by_chip, core, axis=1, keepdims=False))
        send.append(lax.dynamic_index_in_dim(by_chip, 1 - core, axis=1, keepdims=False).astype(BF16))
    got = _pair_exchange(name, send + list(extra))
    sums = []
    for mine, theirs in zip(own, got):
        flat = (-1, mine.shape[-1])
        (part,) = _rowwise("grads_pair_sum", lambda a, b: (a + b.astype(F32),), [mine.reshape(flat), theirs.reshape(flat)],
                           [], [((int(np.prod(mine.shape[:-1])), mine.shape[-1]), BF16)])
        sums.append(part.reshape(mine.shape))
    return sums, got[len(slabs):]


def kernel(x, g_pre_mix, g_post_mix, g_pre_mlp, g_post_mlp, w_in, swa_sinks, s5_lam_re, s5_lam_im, s5_log_dt, s5_b_re, s5_b_im, s5_c_re, s5_c_im, s5_d, s5_w_glu, mla_g_q, mla_g_kv, mla_w_uq, mla_w_ukv, w_branch, w_out, w_ff1, w_ff2, loss_target, m_g_pre_mix, m_g_post_mix, m_g_pre_mlp, m_g_post_mlp, m_w_in, m_swa_sinks, m_s5_lam_re, m_s5_lam_im, m_s5_log_dt, m_s5_b_re, m_s5_b_im, m_s5_c_re, m_s5_c_im, m_s5_d, m_s5_w_glu, m_mla_g_q, m_mla_g_kv, m_mla_w_uq, m_mla_w_ukv, m_w_branch, m_w_out, m_w_ff1, m_w_ff2, v_g_pre_mix, v_g_post_mix, v_g_pre_mlp, v_g_post_mlp, v_w_in, v_swa_sinks, v_s5_lam_re, v_s5_lam_im, v_s5_log_dt, v_s5_b_re, v_s5_b_im, v_s5_c_re, v_s5_c_im, v_s5_d, v_s5_w_glu, v_mla_g_q, v_mla_g_kv, v_mla_w_uq, v_mla_w_ukv, v_w_branch, v_w_out, v_w_ff1, v_w_ff2):
    weights = dict(g_pre_mix=g_pre_mix, g_post_mix=g_post_mix, g_pre_mlp=g_pre_mlp, g_post_mlp=g_post_mlp, w_in=w_in,
                   swa_sinks=swa_sinks, s5_lam_re=s5_lam_re, s5_lam_im=s5_lam_im, s5_log_dt=s5_log_dt, s5_b_re=s5_b_re,
                   s5_b_im=s5_b_im, s5_c_re=s5_c_re, s5_c_im=s5_c_im, s5_d=s5_d, s5_w_glu=s5_w_glu, mla_g_q=mla_g_q,
                   mla_g_kv=mla_g_kv, mla_w_uq=mla_w_uq, mla_w_ukv=mla_w_ukv, w_branch=w_branch, w_out=w_out, w_ff1=w_ff1,
                   w_ff2=w_ff2)
    m_in = dict(g_pre_mix=m_g_pre_mix, g_post_mix=m_g_post_mix, g_pre_mlp=m_g_pre_mlp, g_post_mlp=m_g_post_mlp, w_in=m_w_in,
                swa_sinks=m_swa_sinks, s5_lam_re=m_s5_lam_re, s5_lam_im=m_s5_lam_im, s5_log_dt=m_s5_log_dt,
                s5_b_re=m_s5_b_re, s5_b_im=m_s5_b_im, s5_c_re=m_s5_c_re, s5_c_im=m_s5_c_im, s5_d=m_s5_d,
                s5_w_glu=m_s5_w_glu, mla_g_q=m_mla_g_q, mla_g_kv=m_mla_g_kv, mla_w_uq=m_mla_w_uq, mla_w_ukv=m_mla_w_ukv,
                w_branch=m_w_branch, w_out=m_w_out, w_ff1=m_w_ff1, w_ff2=m_w_ff2)
    v_in = dict(g_pre_mix=v_g_pre_mix, g_post_mix=v_g_post_mix, g_pre_mlp=v_g_pre_mlp, g_post_mlp=v_g_post_mlp, w_in=v_w_in,
                swa_sinks=v_swa_sinks, s5_lam_re=v_s5_lam_re, s5_lam_im=v_s5_lam_im, s5_log_dt=v_s5_log_dt,
                s5_b_re=v_s5_b_re, s5_b_im=v_s5_b_im, s5_c_re=v_s5_c_re, s5_c_im=v_s5_c_im, s5_d=v_s5_d,
                s5_w_glu=v_s5_w_glu, mla_g_q=v_mla_g_q, mla_g_kv=v_mla_g_kv, mla_w_uq=v_mla_w_uq, mla_w_ukv=v_mla_w_ukv,
                w_branch=v_w_branch, w_out=v_w_out, w_ff1=v_w_ff1, w_ff2=v_w_ff2)
    depth = g_pre_mix.shape[0]
    t_len = x.shape[1]
    tabs = _rope_tables(t_len, 32) + _rope_tables(t_len, 16)

    def shards(l, names):
        return [weights[name][l].astype(BF16) for name in names]
    early = _run_rider("gather_weights", _gather_rider(shards(0, EARLY)))

    xs, saved, layer_w, layer_p = [x[0]], [], [], []
    for l in range(depth):
        layer_w.append(_early_weights(early))
        layer_p.append({name: weights[name][l] for name in SMALL})
        x_next, s, early = _layer_fwd(xs[-1], layer_w[l], layer_p[l], tabs, shards(l, LATE),
                                      _gather_rider(shards(l + 1, EARLY)) if l + 1 < depth else None)
        xs.append(x_next)
        saved.append(s)

    def loss_fn(y, tgt):
        err = y - tgt
        part = 0.5 * jnp.sum(jnp.mean(err * err, axis=-1, keepdims=True), axis=0, keepdims=True)
        return (err * (1.0 / D_MODEL),), (jnp.broadcast_to(part, (8, 128)),)
    dx, loss_part = _rowwise("loss", loss_fn, [xs[-1], loss_target[0]], [], [((t_len, D_MODEL), F32)], [((8, 128), F32)])
    loss = lax.psum(loss_part[0, 0], ("x", "y", "c"))

    small_grads, arrived, pending = [None] * depth, [{} for _ in range(depth)], None
    for l in reversed(range(depth)):
        dx, big, small_grads[l], got_pending, got_late = _layer_bwd(xs[l], dx, layer_w[l], layer_p[l], saved[l], tabs, pending,
                                                                    "late_grads_to_sibling")
        if pending is not None:
            arrived[l + 1].update(zip(EARLY, got_pending))
        arrived[l].update(zip(LATE, got_late))
        if l > 0:
            pending, _ = _pair_stage("early_grads_to_sibling", _grad_slabs(big, EARLY))
    small_vec = jnp.concatenate([jnp.stack([small_grads[l][name] for l in range(depth)]).reshape(-1) for name in SMALL])
    n_small = small_vec.shape[0]
    rows = -(-n_small // 1024) * 8
    small_mat = jnp.pad(small_vec, (0, rows * 128 - n_small)).reshape(rows, 128)
    sums, (sibling_small,) = _pair_stage("last_grads_to_sibling", _grad_slabs(big, EARLY), [small_mat])
    small_pair = jnp.where(lax.axis_index("c") == 0, jnp.stack([small_mat, sibling_small]), jnp.stack([sibling_small, small_mat]))
    recv = _run_rider("grads_to_chips", _chip_rider([(part, True) for part in sums] + [(small_pair, False)]))
    arrived[0].update(zip(EARLY, recv[:-1]))
    small_sum = _sum_slabs("sum_small_grads", recv[-1].reshape((N_DEV,) + small_mat.shape)).reshape(-1)

    grads, deltas, new_m, new_v = {}, {}, {}, {}
    for name in BIG:
        slabs = jnp.stack([arrived[l][name] for l in range(depth)], axis=1)
        grads[name], deltas[name], new_m[name], new_v[name] = _adamw("adamw_" + name, weights[name], slabs, m_in[name],
                                                                     v_in[name], 4)
    off = 0
    for name in SMALL:
        size = int(np.prod(weights[name].shape))
        g = small_sum[off:off + size].reshape(weights[name].shape)
        off += size
        grads[name], deltas[name], new_m[name], new_v[name] = _adamw("adamw_" + name, weights[name], g, m_in[name],
                                                                     v_in[name], False)
    return (loss, dx[None], *[grads[n] for n in WEIGHTS], *[deltas[n] for n in WEIGHTS], *[new_m[n] for n in WEIGHTS],
            *[new_v[n] for n in WEIGHTS])
```

```python
import functools
import math

import numpy as np
import jax
import jax.numpy as jnp
from jax import lax
from jax.experimental import pallas as pl
from jax.experimental.pallas import tpu as pltpu

F32, BF16 = jnp.float32, jnp.bfloat16
EPS = 1e-6
NEG_INF = -1e30
ROPE_THETA = 10000.0
D_MODEL = 1024
N_DEV = 8
BLOCK = 128
SWA_HEADS, SWA_KV = 8, 2
RET_HEADS, RET_DK, RET_DV = 4, 64, 128
MLA_HEADS = 8
S5_GROUPS, S5_STATE, S5_GROUP = 32, 64, 16
S5_CH = S5_GROUPS * S5_STATE
SCAN_LANES = 256
SCAN_SEG = 8
ADAM_LR, ADAM_B1, ADAM_B2, ADAM_EPS, ADAM_WD, ADAM_STEP = 0.001, 0.9, 0.999, 1e-08, 0.01, 10
VMEM_MB = 56

PROJ = dict(sq=(0, 512), su=(512, 512), rv=(1024, 512), rg=(1536, 512), rq=(2048, 256), rk=(2304, 256),
            cq=(2560, 256), sk=(2816, 128), sv=(2944, 128), ckv=(3072, 128), kr=(3200, 128), gates=(3328, 4096))
PROJ_W = 7424
REF_SPLIT = dict(sq=(0, 512), sk=(512, 128), sv=(640, 128), su=(768, 512), rq=(1280, 256), rk=(1536, 256),
                 rv=(1792, 512), rg=(2304, 512), cq=(2816, 256), ckv=(3072, 128), kr=(3200, 32), gates=(3232, 4096))
PROJ_ORDER = ("sq", "su", "rv", "rg", "rq", "rk", "cq", "sk", "sv", "ckv", "kr", "gates")
REF_ORDER = ("sq", "sk", "sv", "su", "rq", "rk", "rv", "rg", "cq", "ckv", "kr", "gates")

NN = (((1,), (0,)), ((), ()))
NT = (((1,), (1,)), ((), ()))
TN = (((0,), (0,)), ((), ()))


def _params(sem):
    return pltpu.CompilerParams(dimension_semantics=sem, vmem_limit_bytes=VMEM_MB * 2 ** 20)


def _raw_dot(a, b, dims):
    return lax.dot_general(a.astype(BF16), b.astype(BF16), dims, preferred_element_type=F32)


@jax.custom_vjp
def _dnn(a, b):
    return _raw_dot(a, b, NN)


@jax.custom_vjp
def _dnt(a, b):
    return _raw_dot(a, b, NT)


@jax.custom_vjp
def _dtn(a, b):
    return _raw_dot(a, b, TN)


_dnn.defvjp(lambda a, b: (_dnn(a, b), (a, b)),
            lambda r, g: (_dnt(g, r[1]).astype(r[0].dtype), _dtn(r[0], g).astype(r[1].dtype)))
_dnt.defvjp(lambda a, b: (_dnt(a, b), (a, b)),
            lambda r, g: (_dnn(g, r[1]).astype(r[0].dtype), _dtn(g, r[0]).astype(r[1].dtype)))
_dtn.defvjp(lambda a, b: (_dtn(a, b), (a, b)),
            lambda r, g: (_dnt(r[1], g).astype(r[0].dtype), _dnn(r[0], g).astype(r[1].dtype)))


class _Cols:
    def __init__(self, arr, width, index):
        self.arr, self.width, self.index = arr, width, index


def _seg(proj, name):
    off, width = PROJ[name]
    return _Cols(proj, width, off // width)


def _row_spec(shape, tile):
    lead = len(shape) - 2
    return pl.BlockSpec(tuple(shape[:-2]) + (tile, shape[-1]), lambda i, lead=lead: (0,) * lead + (i, 0))


def _const_spec(shape):
    nd = len(shape)
    return pl.BlockSpec(tuple(shape), lambda *_, nd=nd: (0,) * nd)


ROW_VMEM_BUDGET = 24 * 2 ** 20


def _rowwise(name, fn, rows, consts, out_rows, out_accs=()):
    arrs = [r.arr if isinstance(r, _Cols) else r for r in rows]
    t_len = arrs[0].shape[-2]
    row_bytes = sum((r.width if isinstance(r, _Cols) else int(np.prod(r.shape)) // r.shape[-2]) * a.dtype.itemsize
                    for r, a in zip(rows, arrs))
    row_bytes += sum(int(np.prod(s)) // s[-2] * jnp.dtype(d).itemsize for s, d in out_rows)
    tile = next((t for t in (1024, 512, 256, 128) if t_len % t == 0 and 2 * t * row_bytes <= ROW_VMEM_BUDGET), min(128, t_len))
    n = t_len // tile
    in_specs = []
    for r in rows:
        if isinstance(r, _Cols):
            in_specs.append(pl.BlockSpec((tile, r.width), lambda i, c=r.index: (i, c)))
        else:
            in_specs.append(_row_spec(r.shape, tile))
    in_specs += [_const_spec(c.shape) for c in consts]
    out_specs = [_row_spec(s, tile) for s, _ in out_rows] + [_const_spec(s) for s, _ in out_accs]
    out_shape = [jax.ShapeDtypeStruct(s, d) for s, d in tuple(out_rows) + tuple(out_accs)]
    n_in, n_or = len(rows) + len(consts), len(out_rows)

    def body(*refs):
        res = fn(*[r[...] for r in refs[:n_in]])
        row_out, acc_out = res if out_accs else (res, ())
        for ref, val in zip(refs[n_in:n_in + n_or], row_out):
            if isinstance(val, (list, tuple)):
                for h, v_h in enumerate(val):
                    ref[h] = v_h.astype(ref.dtype)
            else:
                ref[...] = val.astype(ref.dtype)
        first = pl.program_id(0) == 0
        for ref, val in zip(refs[n_in + n_or:], acc_out):
            @pl.when(first)
            def _(ref=ref, val=val):
                ref[...] = val

            @pl.when(jnp.logical_not(first))
            def _(ref=ref, val=val):
                ref[...] += val

    return pl.pallas_call(body, name=name, grid=(n,), in_specs=in_specs, out_specs=out_specs, out_shape=out_shape,
                          compiler_params=_params(("arbitrary",)))(*arrs, *consts)


MM_VMEM_BUDGET = 44 * 2 ** 20


def _pick_tn(n, io_col_bytes, tmp_col_bytes, fixed_bytes):
    best = None
    for cand in range(128, n + 1, 128):
        if n % cand == 0 and 2 * (fixed_bytes + cand * io_col_bytes) + cand * tmp_col_bytes <= MM_VMEM_BUDGET:
            best = cand
    return best if best is not None else min(n, 128)


def _mm(name, a, b, mode, out_dtypes=(F32,), tm=512, tn=None, epi=None, epi_in=(), rider=None):
    if mode == "nn":
        (m, k), n = a.shape, b.shape[1]
    elif mode == "nt":
        (m, k), n = a.shape, b.shape[0]
    else:
        (k, m), n = a.shape, b.shape[1]
    tm = min(tm, m)
    if tn is None:
        io_col = k * b.dtype.itemsize + tm * (sum(jnp.dtype(d).itemsize for d in out_dtypes) + sum(e.dtype.itemsize for e in epi_in))
        tmp_col = tm * 4 * (2 if epi is not None else 1)
        tn = _pick_tn(n, io_col, tmp_col, tm * k * a.dtype.itemsize)
    tn = min(tn, n)
    assert m % tm == 0 and n % tn == 0, (name, m, n, tm, tn)
    a_spec = pl.BlockSpec((k, tm), lambda i, j: (0, i)) if mode == "tn" else pl.BlockSpec((tm, k), lambda i, j: (i, 0))
    b_spec = pl.BlockSpec((tn, k), lambda i, j: (j, 0)) if mode == "nt" else pl.BlockSpec((k, tn), lambda i, j: (0, j))
    dims = dict(nn=NN, nt=NT, tn=TN)[mode]
    o_spec = pl.BlockSpec((tm, tn), lambda i, j: (i, j))
    n_e = len(epi_in)

    def body(a_ref, b_ref, *refs):
        acc = _raw_dot(a_ref[...], b_ref[...], dims)
        outs = epi(acc, *[r[...] for r in refs[:n_e]]) if epi is not None else (acc,)
        for ref, val in zip(refs[n_e:], outs):
            ref[...] = val.astype(ref.dtype)

    grid = (m // tm, n // tn)
    ride_in, ride_out, ride_sems = ([], [], []) if rider is None else (rider.inputs, rider.out_shape, rider.sems)
    any_spec = pl.BlockSpec(memory_space=pl.ANY)
    res = pl.pallas_call(_ride(body, rider, (0, 2 + n_e, len(out_dtypes), 0), grid), name=name, grid=grid,
                         in_specs=[a_spec, b_spec] + [o_spec] * n_e + [any_spec] * len(ride_in),
                         out_specs=[o_spec] * len(out_dtypes) + [any_spec] * len(ride_out),
                         out_shape=[jax.ShapeDtypeStruct((m, n), d) for d in out_dtypes] + ride_out, scratch_shapes=ride_sems,
                         compiler_params=_params(("parallel", "parallel") if rider is None else ("arbitrary", "arbitrary")))(
                             a, b, *epi_in, *ride_in)
    own = res[0] if len(out_dtypes) == 1 else res[:len(out_dtypes)]
    return own if rider is None else (own, res[len(out_dtypes):])


def _rms(x, g):
    return x * lax.rsqrt(jnp.mean(x * x, axis=-1, keepdims=True) + EPS) * g


def _rms_bwd(x, g, dy):
    _, vjp = jax.vjp(_rms, x, g)
    return vjp(dy)


def _rope_tables(t_len, half):
    dim = 2 * half
    inv = 1.0 / (ROPE_THETA ** (jnp.arange(0, dim, 2, dtype=F32) / dim))
    ang = jnp.arange(t_len, dtype=F32)[:, None] * inv[None, :]
    c, s = jnp.cos(ang), jnp.sin(ang)
    reps = 128 // dim
    return jnp.tile(jnp.concatenate([c, c], 1), (1, reps)), jnp.tile(jnp.concatenate([-s, s], 1), (1, reps))


def _rope(x, cos, sin, half):
    w = x.shape[-1]
    reps = w // 128
    if reps > 1:
        cos, sin = jnp.tile(cos, (1, reps)), jnp.tile(sin, (1, reps))
    lane = lax.broadcasted_iota(jnp.int32, x.shape, 1)
    partner = jnp.where((lane % (2 * half)) < half, pltpu.roll(x, w - half, 1), pltpu.roll(x, half, 1))
    return x * cos + partner * sin


SWA_GROUP = SWA_HEADS // SWA_KV
SWA_QB = 4
SWA_TB_FWD, SWA_TB_BWD = 4, 2


def _swa_mask(tb, first_block):
    rows, keys = SWA_GROUP * tb * BLOCK, (tb + 1) * BLOCK
    qi = lax.broadcasted_iota(jnp.int32, (rows, keys), 0) % (tb * BLOCK) + BLOCK
    kj = lax.broadcasted_iota(jnp.int32, (rows, keys), 1)
    diff = qi - kj
    band = (diff >= 0) & (diff < BLOCK)
    return band if first_block is False else band & (jnp.logical_not(first_block) | (kj >= BLOCK))


def _swa_stack(ref, j, b, tb, dtype=None):
    rows = slice(BLOCK * b, BLOCK * (b + tb))
    parts = [ref[rows, 64 * (SWA_GROUP * j + g):64 * (SWA_GROUP * j + g) + 64] for g in range(SWA_GROUP)]
    out = jnp.concatenate(parts, axis=0)
    return out if dtype is None else out.astype(dtype)


def _swa_sinks(s_ref, j, tb):
    return jnp.concatenate([jnp.broadcast_to(s_ref[SWA_GROUP * j + g:SWA_GROUP * j + g + 1, 0:1], (tb * BLOCK, 1))
                            for g in range(SWA_GROUP)], axis=0)


def _swa_head(q, kw, vw, sink, mask):
    sc = _dnt(q, kw) * (64 ** -0.5)
    sc = jnp.where(mask, sc, NEG_INF)
    m = lax.stop_gradient(jnp.maximum(jnp.max(sc, axis=-1, keepdims=True), sink))
    p = jnp.exp(sc - m)
    denom = jnp.sum(p, axis=-1, keepdims=True) + jnp.exp(sink - m)
    return _dnn(p / denom, vw)


def _swa_prep(proj, cos, sin):
    def fn(sq, sk, sv, c, s):
        return _rope(sq, c, s, 32), _rope(sk, c, s, 32), sv
    t_len = proj.shape[0]
    return _rowwise("swa_prep", fn, [_seg(proj, "sq"), _seg(proj, "sk"), _seg(proj, "sv"), cos, sin], [],
                    [((t_len, 512), BF16), ((t_len, 128), BF16), ((t_len, 128), BF16)])


def _swa_specs(t_len, tb_max=1):
    qb = min(SWA_QB, t_len // BLOCK)
    tb = min(tb_max, qb)
    rows = qb * BLOCK
    cur = lambda w: pl.BlockSpec((rows, w), lambda i: (i, 0))
    prev = pl.BlockSpec((BLOCK, 128), lambda i: (jnp.maximum(qb * i - 1, 0), 0))
    return qb, tb, t_len // rows, cur, prev


def _swa_fwd(q, k, v, sinks):
    t_len = q.shape[0]
    qb, tb, steps, cur, prev = _swa_specs(t_len, SWA_TB_FWD)

    def body(q_ref, kp_ref, kc_ref, vp_ref, vc_ref, s_ref, o_ref):
        first = pl.program_id(0) == 0
        k_all = jnp.concatenate([kp_ref[...], kc_ref[...]], axis=0)
        v_all = jnp.concatenate([vp_ref[...], vc_ref[...]], axis=0)
        for j in range(SWA_KV):
            cols = slice(64 * j, 64 * j + 64)
            sink = _swa_sinks(s_ref, j, tb)
            for b in range(0, qb, tb):
                window = slice(BLOCK * b, BLOCK * (b + tb + 1))
                mask = _swa_mask(tb, first if b == 0 else False)
                o = _swa_head(_swa_stack(q_ref, j, b, tb), k_all[window, cols], v_all[window, cols], sink, mask)
                for g in range(SWA_GROUP):
                    h = SWA_GROUP * j + g
                    o_ref[BLOCK * b:BLOCK * (b + tb), 64 * h:64 * h + 64] = o[tb * BLOCK * g:tb * BLOCK * (g + 1)].astype(o_ref.dtype)

    return pl.pallas_call(
        body, name="swa_fwd", grid=(steps,),
        in_specs=[cur(512), prev, cur(128), prev, cur(128), _const_spec((8, 128))],
        out_specs=cur(512), out_shape=jax.ShapeDtypeStruct((t_len, 512), BF16),
        compiler_params=_params(("parallel",)))(q, k, k, v, v, sinks)


def _swa_bwd(q, k, v, sinks, dy):
    t_len = q.shape[0]
    qb, tb, steps, cur, prev = _swa_specs(t_len, SWA_TB_BWD)

    def body(q_ref, kp_ref, kc_ref, vp_ref, vc_ref, s_ref, dy_ref, dq_ref, dk_ref, dv_ref, dkp_ref, dvp_ref, ds_ref):
        first = pl.program_id(0) == 0

        @pl.when(first)
        def _():
            ds_ref[...] = jnp.zeros_like(ds_ref)

        k_all = jnp.concatenate([kp_ref[...], kc_ref[...]], axis=0)
        v_all = jnp.concatenate([vp_ref[...], vc_ref[...]], axis=0)
        for j in range(SWA_KV):
            cols = slice(64 * j, 64 * j + 64)
            sink = _swa_sinks(s_ref, j, tb)
            dk_parts = [jnp.zeros((BLOCK, 64), F32) for _ in range(qb + 1)]
            dv_parts = [jnp.zeros((BLOCK, 64), F32) for _ in range(qb + 1)]
            dsink = jnp.zeros((SWA_GROUP * tb * BLOCK, 1), F32)
            for b in range(0, qb, tb):
                window = slice(BLOCK * b, BLOCK * (b + tb + 1))
                mask = _swa_mask(tb, first if b == 0 else False)
                _, vjp = jax.vjp(lambda a, b_, c, d, mask=mask: _swa_head(a, b_, c, d, mask), _swa_stack(q_ref, j, b, tb),
                                 k_all[window, cols], v_all[window, cols], sink)
                dq, dkw, dvw, dsink_b = vjp(_swa_stack(dy_ref, j, b, tb, F32))
                dkw, dvw = dkw.astype(F32), dvw.astype(F32)
                for r in range(tb + 1):
                    dk_parts[b + r] = dk_parts[b + r] + dkw[BLOCK * r:BLOCK * (r + 1)]
                    dv_parts[b + r] = dv_parts[b + r] + dvw[BLOCK * r:BLOCK * (r + 1)]
                dsink = dsink + dsink_b
                for g in range(SWA_GROUP):
                    h = SWA_GROUP * j + g
                    dq_ref[BLOCK * b:BLOCK * (b + tb), 64 * h:64 * h + 64] = dq[tb * BLOCK * g:tb * BLOCK * (g + 1)].astype(F32)
            dkp_ref[:, cols], dvp_ref[:, cols] = dk_parts[0], dv_parts[0]
            for b in range(qb):
                dk_ref[BLOCK * b:BLOCK * b + BLOCK, cols] = dk_parts[b + 1]
                dv_ref[BLOCK * b:BLOCK * b + BLOCK, cols] = dv_parts[b + 1]
            for g in range(SWA_GROUP):
                h = SWA_GROUP * j + g
                ds_ref[h:h + 1, :] += jnp.broadcast_to(jnp.sum(dsink[tb * BLOCK * g:tb * BLOCK * (g + 1)], axis=0, keepdims=True), (1, 128))

    part = pl.BlockSpec((BLOCK, 128), lambda i: (i, 0))
    return pl.pallas_call(
        body, name="swa_bwd", grid=(steps,),
        in_specs=[cur(512), prev, cur(128), prev, cur(128), _const_spec((8, 128)), cur(512)],
        out_specs=[cur(512), cur(128), cur(128), part, part, _const_spec((8, 128))],
        out_shape=[jax.ShapeDtypeStruct((t_len, 512), F32), jax.ShapeDtypeStruct((t_len, 128), F32),
                   jax.ShapeDtypeStruct((t_len, 128), F32), jax.ShapeDtypeStruct((steps * BLOCK, 128), F32),
                   jax.ShapeDtypeStruct((steps * BLOCK, 128), F32), jax.ShapeDtypeStruct((8, 128), F32)],
        compiler_params=_params(("arbitrary",)))(q, k, k, v, v, sinks, dy)


def _swa_unprep(dq, dk, dv, dk_prev, dv_prev, cos, sin):
    t_len = dq.shape[0]
    qb, _, steps, cur, _ = _swa_specs(t_len)
    tail = (qb - 1) * BLOCK

    def body(dq_ref, dk_ref, dv_ref, dkn_ref, dvn_ref, c_ref, s_ref, oq_ref, ok_ref, ov_ref):
        more = (pl.program_id(0) < steps - 1).astype(F32)
        cos_t, sin_t = c_ref[...], -s_ref[...]
        oq_ref[...] = _rope(dq_ref[...], cos_t, sin_t, 32).astype(BF16)
        dk_t = jnp.concatenate([dk_ref[:tail], dk_ref[tail:] + more * dkn_ref[...]], axis=0) if tail else dk_ref[...] + more * dkn_ref[...]
        dv_t = jnp.concatenate([dv_ref[:tail], dv_ref[tail:] + more * dvn_ref[...]], axis=0) if tail else dv_ref[...] + more * dvn_ref[...]
        ok_ref[...] = _rope(dk_t, cos_t, sin_t, 32).astype(BF16)
        ov_ref[...] = dv_t.astype(BF16)

    nxt = pl.BlockSpec((BLOCK, 128), lambda i: (jnp.minimum(i + 1, steps - 1), 0))
    return pl.pallas_call(
        body, name="swa_unprep", grid=(steps,),
        in_specs=[cur(512), cur(128), cur(128), nxt, nxt, cur(128), cur(128)],
        out_specs=[cur(512), cur(128), cur(128)],
        out_shape=[jax.ShapeDtypeStruct((t_len, 512), BF16), jax.ShapeDtypeStruct((t_len, 128), BF16),
                   jax.ShapeDtypeStruct((t_len, 128), BF16)],
        compiler_params=_params(("parallel",)))(dq, dk, dv, dk_prev, dv_prev, cos, sin)


def _s5_discretize(lam_re, lam_im, log_dt, b_re, b_im):
    dt = jnp.exp(log_dt)
    mag = jnp.exp(lam_re * dt)
    ab_re, ab_im = mag * jnp.cos(lam_im * dt), mag * jnp.sin(lam_im * dt)
    den = lam_re * lam_re + lam_im * lam_im
    nr, ni = ab_re - 1.0, ab_im
    f_re = (nr * lam_re + ni * lam_im) / den
    f_im = (ni * lam_re - nr * lam_im) / den
    return ab_re, ab_im, f_re * b_re - f_im * b_im, f_re * b_im + f_im * b_re


def _s5_param_fwd(lam_re, lam_im, log_dt, b_re, b_im):
    def body(*refs):
        outs = _s5_discretize(*[r[...] for r in refs[:5]])
        for ref, val in zip(refs[5:], outs):
            ref[...] = val
    n = S5_CH
    return pl.pallas_call(body, name="s5_param_fwd",
                          out_shape=[jax.ShapeDtypeStruct((n, 1), F32)] * 2 + [jax.ShapeDtypeStruct((n, 16), F32)] * 2,
                          )(lam_re, lam_im, log_dt, b_re, b_im)


def _s5_param_bwd(lam_re, lam_im, log_dt, b_re, b_im, group_sum, d_ab_re, d_ab_im, d_bb_re, d_bb_im):
    def body(*refs):
        ins = [r[...] for r in refs[:10]]
        _, vjp = jax.vjp(_s5_discretize, *ins[:5])
        d_lr, d_li, d_ldt, d_br, d_bi = vjp(tuple(ins[6:10]))
        o_lr, o_li, o_ldt, o_br, o_bi = refs[10:]
        o_lr[...], o_li[...], o_br[...], o_bi[...] = d_lr, d_li, d_br, d_bi
        o_ldt[...] = lax.dot_general(ins[5], jnp.broadcast_to(d_ldt, (S5_CH, 128)), NN,
                                     precision=lax.Precision.HIGHEST, preferred_element_type=F32)
    n = S5_CH
    return pl.pallas_call(body, name="s5_param_bwd",
                          out_shape=[jax.ShapeDtypeStruct((n, 1), F32), jax.ShapeDtypeStruct((n, 1), F32),
                                     jax.ShapeDtypeStruct((S5_GROUPS, 128), F32),
                                     jax.ShapeDtypeStruct((n, 16), F32), jax.ShapeDtypeStruct((n, 16), F32)],
                          )(lam_re, lam_im, log_dt, b_re, b_im, group_sum, d_ab_re, d_ab_im, d_bb_re, d_bb_im)


def _s5_expand(name, u, first_band, w):
    t_len = u.shape[0]
    tile = min(512, t_len)
    seg = t_len // SCAN_SEG

    def body(u_ref, wr_ref, wi_ref, o_ref, up_ref, rows):
        base = pl.program_id(1) * (tile // SCAN_SEG)
        for r in range(tile // SCAN_SEG):
            rows[SCAN_SEG * r:SCAN_SEG * (r + 1), :] = u_ref[pl.ds(base + r, SCAN_SEG, stride=seg), :]
        u_t = rows[...].astype(BF16)
        up_ref[...] = u_t
        o_ref[0] = _raw_dot(u_t, wr_ref[...], NN)
        o_ref[1] = _raw_dot(u_t, wi_ref[...], NN)

    return pl.pallas_call(
        body, name=name, grid=(4, t_len // tile),
        in_specs=[pl.BlockSpec((t_len, 128), lambda b, i: (0, first_band + b)), pl.BlockSpec((None, 128, 512), lambda b, i: (b, 0, 0)),
                  pl.BlockSpec((None, 128, 512), lambda b, i: (4 + b, 0, 0))],
        out_specs=[pl.BlockSpec((2, tile, 512), lambda b, i: (0, i, b)), pl.BlockSpec((tile, 128), lambda b, i: (i, b))],
        out_shape=[jax.ShapeDtypeStruct((2, t_len, S5_CH), F32), jax.ShapeDtypeStruct((t_len, 512), BF16)],
        scratch_shapes=[pltpu.VMEM((tile, 128), F32)], compiler_params=_params(("parallel", "arbitrary")))(u, w, w)


def _s5_contract(name, x, w):
    t_len = x.shape[1]
    tile = min(512, t_len)
    seg = t_len // SCAN_SEG

    def body(x_ref, wr_ref, wi_ref, o_ref):
        y = _raw_dot(x_ref[0], wr_ref[...], NN) + _raw_dot(x_ref[1], wi_ref[...], NN)
        base = pl.program_id(1) * (tile // SCAN_SEG)
        for r in range(tile // SCAN_SEG):
            o_ref[pl.ds(base + r, SCAN_SEG, stride=seg), :] = y[SCAN_SEG * r:SCAN_SEG * (r + 1)]

    return pl.pallas_call(
        body, name=name, grid=(4, t_len // tile),
        in_specs=[pl.BlockSpec((2, tile, 512), lambda b, i: (0, i, b)), pl.BlockSpec((None, 512, 128), lambda b, i: (b, 0, 0)),
                  pl.BlockSpec((None, 512, 128), lambda b, i: (4 + b, 0, 0))],
        out_specs=pl.BlockSpec((t_len, 128), lambda b, i: (0, b)),
        out_shape=jax.ShapeDtypeStruct((t_len, 512), F32), compiler_params=_params(("parallel", "arbitrary")))(x, w, w)


def _s5_band_grad(name, narrow, wide, narrow_first):
    t_len = narrow.shape[0]

    def body(n_ref, w_ref, o_ref):
        if narrow_first:
            o_ref[...] = _raw_dot(n_ref[...], w_ref[...], TN)
        else:
            o_ref[...] = _raw_dot(w_ref[...], n_ref[...], TN)

    shape = (8, 128, 512) if narrow_first else (8, 512, 128)
    return pl.pallas_call(
        body, name=name, grid=(8,),
        in_specs=[pl.BlockSpec((t_len, 128), lambda bb: (0, bb % 4)), pl.BlockSpec((None, t_len, 512), lambda bb: (bb // 4, 0, bb % 4))],
        out_specs=pl.BlockSpec((None,) + shape[1:], lambda bb: (bb, 0, 0)),
        out_shape=jax.ShapeDtypeStruct(shape, F32), compiler_params=_params(("parallel",)))(narrow, wide)


def _s5_scan(name, bu, a_re, a_im, states=None, carry_in=None):
    t_len = bu.shape[1]
    n = t_len // SCAN_SEG
    log_n = int(round(math.log2(n)))
    assert 2 ** log_n == n
    reverse = states is not None
    lanes = SCAN_LANES
    nblk = S5_CH // lanes

    def body(*refs):
        if reverse:
            b_ref, ar_ref, ai_ref, x_ref, cin_ref, o_ref, dar_ref, dai_ref, c_scr = refs
            xr_ref, xi_ref, cr_ref, ci_ref = x_ref.at[0], x_ref.at[1], cin_ref.at[0], cin_ref.at[1]
        else:
            b_ref, ar_ref, ai_ref, o_ref, co_ref, c_scr = refs
            cor_ref, coi_ref = co_ref.at[0], co_ref.at[1]
        br_ref, bi_ref, or_ref, oi_ref = b_ref.at[0], b_ref.at[1], o_ref.at[0], o_ref.at[1]
        ar = ar_ref[...]
        ai = -ai_ref[...] if reverse else ai_ref[...]

        def tile_at(i):
            return pl.multiple_of((n - 1 - i if reverse else i) * SCAN_SEG, SCAN_SEG)

        def advance(s, row):
            sr, si = s
            return (ar * sr - ai * si + br_ref[pl.ds(row, SCAN_SEG), :], ar * si + ai * sr + bi_ref[pl.ds(row, SCAN_SEG), :])

        zero = jnp.zeros((SCAN_SEG, lanes), F32)
        fin_r, fin_i = lax.fori_loop(0, n, lambda i, s: advance(s, tile_at(i)), (zero, zero), unroll=8)
        pr, pi = ar[0:1], ai[0:1]
        for _ in range(log_n):
            pr, pi = pr * pr - pi * pi, 2.0 * pr * pi
        er = ei = jnp.zeros((1, lanes), F32)
        for step in range(SCAN_SEG):
            j = SCAN_SEG - 1 - step if reverse else step
            c_scr[0, j:j + 1, :] = er
            c_scr[1, j:j + 1, :] = ei
            er, ei = pr * er - pi * ei + fin_r[j:j + 1], pr * ei + pi * er + fin_i[j:j + 1]
        entry = (c_scr[0], c_scr[1])

        if not reverse:
            cor_ref[...], coi_ref[...] = entry

            def step2(i, s):
                row = tile_at(i)
                s = advance(s, row)
                or_ref[pl.ds(row, SCAN_SEG), :] = s[0]
                oi_ref[pl.ds(row, SCAN_SEG), :] = s[1]
                return s
            lax.fori_loop(0, n, step2, entry, unroll=8)
        else:
            def emit(s, row, xr_prev, xi_prev, acc):
                s = advance(s, row)
                or_ref[pl.ds(row, SCAN_SEG), :] = s[0]
                oi_ref[pl.ds(row, SCAN_SEG), :] = s[1]
                return s, (acc[0] + s[0] * xr_prev + s[1] * xi_prev, acc[1] + s[1] * xr_prev - s[0] * xi_prev)

            def step2(i, carry):
                s, acc = carry
                row = tile_at(i)
                prev = pl.multiple_of(row - SCAN_SEG, SCAN_SEG)
                return emit(s, row, xr_ref[pl.ds(prev, SCAN_SEG), :], xi_ref[pl.ds(prev, SCAN_SEG), :], acc)
            s, acc = lax.fori_loop(0, n - 1, step2, (entry, (zero, zero)), unroll=8)
            s, acc = emit(s, 0, cr_ref[...], ci_ref[...], acc)
            dar_ref[...] = jnp.sum(acc[0], axis=0, keepdims=True)
            dai_ref[...] = jnp.sum(acc[1], axis=0, keepdims=True)

    planes = pl.BlockSpec((2, t_len, lanes), lambda b: (0, 0, b))
    tile8 = pl.BlockSpec((SCAN_SEG, lanes), lambda b: (0, b))
    entry_spec = pl.BlockSpec((2, SCAN_SEG, lanes), lambda b: (0, 0, b))
    row1 = pl.BlockSpec((1, lanes), lambda b: (0, b))
    seq = jax.ShapeDtypeStruct((2, t_len, S5_CH), F32)
    if reverse:
        in_specs = [planes, tile8, tile8, planes, entry_spec]
        args = (bu, a_re, a_im, states, carry_in)
        out_specs = [planes, row1, row1]
        out_shape = [seq, jax.ShapeDtypeStruct((1, S5_CH), F32), jax.ShapeDtypeStruct((1, S5_CH), F32)]
    else:
        in_specs = [planes, tile8, tile8]
        args = (bu, a_re, a_im)
        out_specs = [planes, entry_spec]
        out_shape = [seq, jax.ShapeDtypeStruct((2, SCAN_SEG, S5_CH), F32)]
    return pl.pallas_call(body, name=name, grid=(nblk,), in_specs=in_specs, out_specs=out_specs, out_shape=out_shape,
                          scratch_shapes=[pltpu.VMEM((2, SCAN_SEG, lanes), F32)],
                          compiler_params=_params(("parallel",)))(*args)


def _ret_consts():
    h = np.arange(RET_HEADS, dtype=np.float32)
    log_gamma = np.log1p(-np.exp2(-5.0 - h)).astype(np.float32)
    idx = np.arange(BLOCK, dtype=np.float32)
    diff = idx[:, None] - idx[None, :]
    decay = np.where(diff >= 0, np.exp(log_gamma[:, None, None] * np.maximum(diff, 0.0)), 0.0).astype(np.float32)
    k_w = np.exp(log_gamma[:, None] * (BLOCK - 1 - idx)[None, :]).astype(np.float32)[:, :, None]
    q_w = np.exp(log_gamma[:, None] * (idx + 1.0)[None, :]).astype(np.float32)[:, :, None]
    chunk_decay = [float(v) for v in np.exp(log_gamma * BLOCK).astype(np.float32)]
    return jnp.asarray(decay), jnp.asarray(q_w), jnp.asarray(k_w), chunk_decay


def _ret_head(q, k, v, g, prev, decay, q_w, k_w, chunk_decay):
    inner_s = _dnt(q, k) * decay
    y = _dnn(inner_s, v) + _dnn(q * q_w, prev)
    mu = jnp.mean(y, axis=-1, keepdims=True)
    var = jnp.mean(jnp.square(y - mu), axis=-1, keepdims=True)
    yn = (y - mu) * lax.rsqrt(var + EPS)
    out = g * jax.nn.sigmoid(g) * yn
    return out, prev * chunk_decay + _dtn(k * k_w, v)


def _ret_prep(proj, cos, sin):
    def fn(rq, rk, c, s):
        return _rope(rq, c, s, 32), _rope(rk, c, s, 32) * (RET_DK ** -0.5)
    t_len = proj.shape[0]
    return _rowwise("ret_prep", fn, [_seg(proj, "rq"), _seg(proj, "rk"), cos, sin], [],
                    [((t_len, 256), F32), ((t_len, 256), F32)])


def _ret_unprep(dq, dk, cos, sin):
    def fn(dq_t, dk_t, c, s):
        return _rope(dq_t, c, -s, 32), _rope(dk_t * (RET_DK ** -0.5), c, -s, 32)
    t_len = dq.shape[0]
    return _rowwise("ret_unprep", fn, [dq, dk, cos, sin], [], [((t_len, 256), BF16), ((t_len, 256), BF16)])


def _ret_fwd(q, k, proj):
    t_len = q.shape[0]
    nc = t_len // BLOCK
    decay, q_w, k_w, chunk_decay = _ret_consts()

    def body(q_ref, k_ref, v_ref, g_ref, dec_ref, qw_ref, kw_ref, o_ref, prev_ref, state):
        @pl.when(pl.program_id(0) == 0)
        def _():
            state[...] = jnp.zeros_like(state)
        for h in range(RET_HEADS):
            qk, vv = slice(64 * h, 64 * h + 64), slice(128 * h, 128 * h + 128)
            prev = state[h]
            prev_ref[h] = prev
            out, nxt = _ret_head(q_ref[:, qk], k_ref[:, qk], v_ref[:, vv], g_ref[:, vv], prev,
                                 dec_ref[h], qw_ref[h], kw_ref[h], chunk_decay[h])
            o_ref[:, vv] = out.astype(o_ref.dtype)
            state[h] = nxt

    chunk = lambda w, c=0: pl.BlockSpec((BLOCK, w), lambda i, c=c: (i, c))
    return pl.pallas_call(
        body, name="ret_fwd", grid=(nc,),
        in_specs=[chunk(256), chunk(256), chunk(512, PROJ["rv"][0] // 512), chunk(512, PROJ["rg"][0] // 512),
                  _const_spec(decay.shape), _const_spec(q_w.shape), _const_spec(k_w.shape)],
        out_specs=[chunk(512), pl.BlockSpec((None, RET_HEADS, RET_DK, RET_DV), lambda i: (i, 0, 0, 0))],
        out_shape=[jax.ShapeDtypeStruct((t_len, 512), BF16), jax.ShapeDtypeStruct((nc, RET_HEADS, RET_DK, RET_DV), F32)],
        scratch_shapes=[pltpu.VMEM((RET_HEADS, RET_DK, RET_DV), F32)],
        compiler_params=_params(("arbitrary",)))(q, k, proj, proj, decay, q_w, k_w)


def _ret_bwd(q, k, proj, prevs, dy):
    t_len = q.shape[0]
    nc = t_len // BLOCK
    decay, q_w, k_w, chunk_decay = _ret_consts()

    def body(q_ref, k_ref, v_ref, g_ref, prev_ref, dy_ref, dec_ref, qw_ref, kw_ref, dq_ref, dk_ref, dv_ref, dg_ref, dstate):
        @pl.when(pl.program_id(0) == 0)
        def _():
            dstate[...] = jnp.zeros_like(dstate)
        for h in range(RET_HEADS):
            qk, vv = slice(64 * h, 64 * h + 64), slice(128 * h, 128 * h + 128)
            head = functools.partial(_ret_head, decay=dec_ref[h], q_w=qw_ref[h], k_w=kw_ref[h], chunk_decay=chunk_decay[h])
            _, vjp = jax.vjp(head, q_ref[:, qk], k_ref[:, qk], v_ref[:, vv], g_ref[:, vv], prev_ref[h])
            dq, dk, dv, dg, dprev = vjp((dy_ref[:, vv].astype(F32), dstate[h]))
            dq_ref[:, qk], dk_ref[:, qk] = dq, dk
            dv_ref[:, vv], dg_ref[:, vv] = dv.astype(dv_ref.dtype), dg.astype(dg_ref.dtype)
            dstate[h] = dprev

    chunk = lambda w, c=0: pl.BlockSpec((BLOCK, w), lambda i, c=c: (nc - 1 - i, c))
    return pl.pallas_call(
        body, name="ret_bwd", grid=(nc,),
        in_specs=[chunk(256), chunk(256), chunk(512, PROJ["rv"][0] // 512), chunk(512, PROJ["rg"][0] // 512),
                  pl.BlockSpec((None, RET_HEADS, RET_DK, RET_DV), lambda i: (nc - 1 - i, 0, 0, 0)), chunk(512),
                  _const_spec(decay.shape), _const_spec(q_w.shape), _const_spec(k_w.shape)],
        out_specs=[chunk(256), chunk(256), chunk(512), chunk(512)],
        out_shape=[jax.ShapeDtypeStruct((t_len, 256), F32), jax.ShapeDtypeStruct((t_len, 256), F32),
                   jax.ShapeDtypeStruct((t_len, 512), BF16), jax.ShapeDtypeStruct((t_len, 512), BF16)],
        scratch_shapes=[pltpu.VMEM((RET_HEADS, RET_DK, RET_DV), F32)],
        compiler_params=_params(("arbitrary",)))(q, k, proj, proj, prevs, dy, decay, q_w, k_w)


MLA_SCALE = 96 ** -0.5


def _mla_prep(qf, kvf, proj, cos, sin):
    def fn(q, kv, kr, c, s):
        q_rope = _rope(q[:, 512:768], c, s, 16)
        k_rope = _rope(kr, c, s, 16)[:, :32]
        zero = jnp.zeros_like(k_rope)
        qs = [jnp.concatenate([q[:, 64 * h:64 * h + 64], q_rope[:, 32 * h:32 * h + 32], zero], axis=1) for h in range(MLA_HEADS)]
        ks = [jnp.concatenate([kv[:, 64 * h:64 * h + 64], k_rope, zero], axis=1) for h in range(MLA_HEADS)]
        ones = (lax.broadcasted_iota(jnp.int32, (kv.shape[0], 64), 1) == 0).astype(F32)
        vs = [jnp.concatenate([kv[:, 512 + 64 * h:576 + 64 * h], ones], axis=1) for h in range(MLA_HEADS)]
        return qs, ks, vs
    t_len = qf.shape[0]
    return _rowwise("mla_prep", fn, [qf, kvf, _seg(proj, "kr"), cos, sin], [],
                    [((8, t_len, 128), BF16), ((8, t_len, 128), BF16), ((8, t_len, 128), BF16)])


def _mla_unprep(dq, dk, dv, cos, sin):
    def fn(dq_t, dk_t, dv_t, c, s):
        q_rope = _rope(jnp.concatenate([dq_t[h][:, 64:96] for h in range(MLA_HEADS)], axis=1), c, -s, 16)
        d_qf = jnp.concatenate([dq_t[h][:, :64] for h in range(MLA_HEADS)] + [q_rope], axis=1)
        d_kvf = jnp.concatenate([dk_t[h][:, :64] for h in range(MLA_HEADS)] + [dv_t[h][:, :64] for h in range(MLA_HEADS)], axis=1)
        k_rope = dk_t[0][:, 64:96]
        for h in range(1, MLA_HEADS):
            k_rope = k_rope + dk_t[h][:, 64:96]
        d_kr = _rope(jnp.concatenate([k_rope, jnp.zeros((k_rope.shape[0], 96), F32)], axis=1), c, -s, 16)
        return d_qf, d_kvf, d_kr
    t_len = dq.shape[1]
    return _rowwise("mla_unprep", fn, [dq, dk, dv, cos, sin], [],
                    [((t_len, 768), BF16), ((t_len, 1024), BF16), ((t_len, 128), BF16)])


MLA_EXP2 = MLA_SCALE * math.log2(math.e)
MLA_TILE = 1024


def _diag_mask(tq):
    return lax.broadcasted_iota(jnp.int32, (tq, tq), 1) <= lax.broadcasted_iota(jnp.int32, (tq, tq), 0)


def _tri_steps(nq, key_major):
    if key_major:
        pairs = [(i, j) for j in range(nq) for i in range(j, nq)]
    else:
        pairs = [(i, j) for i in range(nq) for j in range(i + 1)]
    return jnp.asarray([p[0] for p in pairs], jnp.int32), jnp.asarray([p[1] for p in pairs], jnp.int32)


def _ride(body, rider, counts, grid):
    if rider is None:
        return body
    extra = (0, len(rider.inputs), len(rider.out_shape), len(rider.sems))

    def wrapped(*refs):
        host, ridden, pos = [], [], 0
        for mine, theirs in zip(counts, extra):
            host += refs[pos:pos + mine]
            ridden.append(refs[pos + mine:pos + mine + theirs])
            pos += mine + theirs
        ids = [pl.program_id(a) for a in range(len(grid))]
        first, last = ids[0] == 0, ids[0] == grid[0] - 1
        for a in range(1, len(grid)):
            first, last = first & (ids[a] == 0), last & (ids[a] == grid[a] - 1)

        @pl.when(first)
        def _():
            rider.start(*ridden[1:])

        body(*host)

        @pl.when(last)
        def _():
            rider.finish(*ridden[1:])

    return wrapped


def _mla_fwd(q, k, v, rider=None):
    t_len = q.shape[1]
    tq = min(MLA_TILE, t_len)
    nq = t_len // tq
    qi, kj = _tri_steps(nq, False)

    def body(i_ref, j_ref, q_ref, k_ref, v_ref, o_ref, lse_ref, m_s, acc_s):
        step = pl.program_id(1)
        i, j = i_ref[step], j_ref[step]

        @pl.when(j == 0)
        def _():
            m_s[...] = jnp.full_like(m_s, NEG_INF)
            acc_s[...] = jnp.zeros_like(acc_s)

        def update(mask):
            for e in range(2):
                s = _raw_dot(q_ref[e], k_ref[e], NT)
                if mask is not None:
                    s = jnp.where(mask, s, NEG_INF)
                m_old = m_s[e]
                m_new = jnp.maximum(m_old, jnp.max(s, axis=-1, keepdims=True))
                p = jnp.exp2((s - m_new) * MLA_EXP2)
                acc_s[e] = jnp.exp2((m_old - m_new) * MLA_EXP2) * acc_s[e] + _raw_dot(p, v_ref[e], NN)
                m_s[e] = m_new

        @pl.when(j < i)
        def _():
            update(None)

        @pl.when(j == i)
        def _():
            update(_diag_mask(tq))
            outs = []
            for e in range(2):
                acc = acc_s[e]
                denom = acc[:, 64:65]
                outs.append(acc[:, :64] / denom)
                lse_ref[e] = m_s[e] * MLA_EXP2 + jnp.log2(denom)
            o_ref[...] = jnp.concatenate(outs, axis=1)

    grid = (MLA_HEADS // 2, int(qi.shape[0]))
    ride_in, ride_out, ride_sems = ([], [], []) if rider is None else (rider.inputs, rider.out_shape, rider.sems)
    grid_spec = pltpu.PrefetchScalarGridSpec(
        num_scalar_prefetch=2, grid=grid,
        in_specs=[pl.BlockSpec((2, tq, 128), lambda hp, s, qi, kj: (hp, qi[s], 0)),
                  pl.BlockSpec((2, tq, 128), lambda hp, s, qi, kj: (hp, kj[s], 0)),
                  pl.BlockSpec((2, tq, 128), lambda hp, s, qi, kj: (hp, kj[s], 0))] + [ANY_SPEC] * len(ride_in),
        out_specs=[pl.BlockSpec((tq, 128), lambda hp, s, qi, kj: (qi[s], hp)),
                   pl.BlockSpec((2, tq, 1), lambda hp, s, qi, kj: (hp, qi[s], 0))] + [ANY_SPEC] * len(ride_out),
        scratch_shapes=[pltpu.VMEM((2, tq, 1), F32), pltpu.VMEM((2, tq, 128), F32)] + ride_sems)
    res = pl.pallas_call(
        _ride(body, rider, (2, 3, 2, 2), grid), name="mla_fwd" if rider is None else "mla_fwd_gather", grid_spec=grid_spec,
        out_shape=[jax.ShapeDtypeStruct((t_len, 512), F32), jax.ShapeDtypeStruct((MLA_HEADS, t_len, 1), F32)] + ride_out,
        compiler_params=_params(("parallel" if rider is None else "arbitrary", "arbitrary")))(qi, kj, q, k, v, *ride_in)
    return res[0], res[1], res[2:]


def _mla_bwd_prep(o, dy):
    def fn(o_t, dy_t):
        dos, deltas = [], []
        for h in range(MLA_HEADS):
            d_h = dy_t[:, 64 * h:64 * h + 64].astype(F32)
            dos.append(jnp.concatenate([d_h, jnp.zeros_like(d_h)], axis=1))
            deltas.append(jnp.sum(d_h * o_t[:, 64 * h:64 * h + 64], axis=-1, keepdims=True))
        return dos, deltas
    t_len = o.shape[0]
    return _rowwise("mla_bwd_prep", fn, [o, dy], [], [((8, t_len, 128), BF16), ((8, t_len, 1), F32)])


def _mla_bwd(q, k, v, do, lse2, delta, rider=None):
    t_len = q.shape[1]
    tq = min(MLA_TILE, t_len)
    nq = t_len // tq
    qi, kj = _tri_steps(nq, True)

    def body(i_ref, j_ref, q_ref, k_ref, v_ref, do_ref, lse_ref, dl_ref, dq_ref, dk_ref, dv_ref, k_acc, v_acc):
        step = pl.program_id(1)
        i, j = i_ref[step], j_ref[step]

        @pl.when(step == 0)
        def _():
            dq_ref[...] = jnp.zeros_like(dq_ref)

        @pl.when(i == j)
        def _():
            k_acc[...] = jnp.zeros_like(k_acc)
            v_acc[...] = jnp.zeros_like(v_acc)

        def tile(mask):
            p = jnp.exp2(_raw_dot(q_ref[...], k_ref[...], NT) * MLA_EXP2 - lse_ref[...])
            if mask is not None:
                p = jnp.where(mask, p, 0.0)
            ds = p * (_raw_dot(do_ref[...], v_ref[...], NT) - dl_ref[...]) * MLA_SCALE
            v_acc[...] += _raw_dot(p, do_ref[...], TN)
            k_acc[...] += _raw_dot(ds, q_ref[...], TN)
            rows = pl.ds(pl.multiple_of(i * tq, tq), tq)
            dq_ref[rows, :] += _raw_dot(ds, k_ref[...], NN)

        @pl.when(i == j)
        def _():
            tile(_diag_mask(tq))

        @pl.when(i > j)
        def _():
            tile(None)

        @pl.when(i == nq - 1)
        def _():
            dk_ref[...] = k_acc[...]
            dv_ref[...] = v_acc[...]

    q_blk = lambda w: pl.BlockSpec((None, tq, w), lambda h, s, qi, kj: (h, qi[s], 0))
    k_blk = lambda w: pl.BlockSpec((None, tq, w), lambda h, s, qi, kj: (h, kj[s], 0))
    grid = (MLA_HEADS, int(qi.shape[0]))
    ride_in, ride_out, ride_sems = ([], [], []) if rider is None else (rider.inputs, rider.out_shape, rider.sems)
    grid_spec = pltpu.PrefetchScalarGridSpec(
        num_scalar_prefetch=2, grid=grid,
        in_specs=[q_blk(128), k_blk(128), k_blk(128), q_blk(128), q_blk(1), q_blk(1)] + [ANY_SPEC] * len(ride_in),
        out_specs=[pl.BlockSpec((None, t_len, 128), lambda h, s, qi, kj: (h, 0, 0)), k_blk(128), k_blk(128)]
        + [ANY_SPEC] * len(ride_out),
        scratch_shapes=[pltpu.VMEM((tq, 128), F32), pltpu.VMEM((tq, 128), F32)] + ride_sems)
    res = pl.pallas_call(
        _ride(body, rider, (2, 6, 3, 2), grid), name="mla_bwd" if rider is None else "mla_bwd_scatter", grid_spec=grid_spec,
        out_shape=[jax.ShapeDtypeStruct((MLA_HEADS, t_len, 128), F32)] * 3 + ride_out,
        compiler_params=_params(("parallel" if rider is None else "arbitrary", "arbitrary")))(qi, kj, q, k, v, do, lse2, delta, *ride_in)
    return res[0], res[1], res[2], res[3:]


MERGE_TM = 1024
MERGE_TN = 256
GATE_BLOCK0 = 0


def _merge_specs(tm):
    y_spec = pl.BlockSpec((tm, 512), lambda i, j: (i, 0))
    w_spec = pl.BlockSpec((512, MERGE_TN), lambda i, j: (0, j))
    gate = lambda b: pl.BlockSpec((tm, MERGE_TN), lambda i, j, b=b: (i, GATE_BLOCK0 + 4 * b + j))
    return [y_spec] * 4 + [w_spec] * 4 + [gate(b) for b in range(4)]


def _merge_fwd(ys, wbs, proj):
    t_len = proj.shape[0]
    tm = min(MERGE_TM, t_len)

    def body(*refs):
        y, w, gl, o_ref = refs[0:4], refs[4:8], refs[8:12], refs[12]
        acc = jnp.zeros((tm, MERGE_TN), F32)
        for b in range(4):
            acc += jax.nn.sigmoid(gl[b][...].astype(F32)) * _raw_dot(y[b][...], w[b][...], NN)
        o_ref[...] = acc.astype(o_ref.dtype)

    return pl.pallas_call(body, name="merge_fwd", grid=(t_len // tm, D_MODEL // MERGE_TN), in_specs=_merge_specs(tm),
                          out_specs=pl.BlockSpec((tm, MERGE_TN), lambda i, j: (i, j)),
                          out_shape=jax.ShapeDtypeStruct((t_len, D_MODEL), BF16),
                          compiler_params=_params(("parallel", "parallel")))(*ys, *wbs, *([proj] * 4))


def _merge_bwd(ys, wbs, proj, dmerged):
    t_len = proj.shape[0]
    tm = min(MERGE_TM, t_len)

    def body(*refs):
        y, w, gl, dm_ref, dgl, dt = refs[0:4], refs[4:8], refs[8:12], refs[12], refs[13:17], refs[17:21]
        dm = dm_ref[...]
        for b in range(4):
            gate = jax.nn.sigmoid(gl[b][...].astype(F32))
            t_b = _raw_dot(y[b][...], w[b][...], NN)
            dgl[b][...] = (dm * t_b * gate * (1.0 - gate)).astype(BF16)
            dt[b][...] = (dm * gate).astype(BF16)

    o_spec = pl.BlockSpec((tm, MERGE_TN), lambda i, j: (i, j))
    return pl.pallas_call(body, name="merge_bwd", grid=(t_len // tm, D_MODEL // MERGE_TN), in_specs=_merge_specs(tm) + [o_spec],
                          out_specs=[o_spec] * 8, out_shape=[jax.ShapeDtypeStruct((t_len, D_MODEL), BF16)] * 8,
                          compiler_params=_params(("parallel", "parallel")))(*ys, *wbs, *([proj] * 4), dmerged)


def _adamw(name, w, g, m, v, slabs):
    shape = w.shape
    size = int(np.prod(shape))
    if shape[-1] >= 96:
        view = (size // shape[-1], shape[-1])
    elif size % 128 == 0:
        view = (size // 128, 128)
    else:
        view = (1, size)
    rows = view[0]
    tile = rows
    for cand in (512, 256, 128, 64):
        if rows > cand and rows % cand == 0 and cand * view[1] * 4 <= 2 ** 20:
            tile = cand
            break
    if rows * view[1] * 4 <= 2 ** 20:
        tile = rows
    c1, c2 = 1.0 - ADAM_B1 ** ADAM_STEP, 1.0 - ADAM_B2 ** ADAM_STEP

    def body(w_ref, g_ref, m_ref, v_ref, og_ref, od_ref, om_ref, ov_ref):
        if slabs:
            grad = g_ref[0].astype(F32)
            for d in range(1, slabs):
                grad = grad + g_ref[d].astype(F32)
        else:
            grad = g_ref[...]
        m_new = ADAM_B1 * m_ref[...] + (1.0 - ADAM_B1) * grad
        v_new = ADAM_B2 * v_ref[...] + (1.0 - ADAM_B2) * jnp.square(grad)
        og_ref[...] = grad
        od_ref[...] = -ADAM_LR * ((m_new / c1) / (jnp.sqrt(v_new / c2) + ADAM_EPS) + ADAM_WD * w_ref[...])
        om_ref[...] = m_new
        ov_ref[...] = v_new

    spec = pl.BlockSpec((tile, view[1]), lambda i: (i, 0))
    g_spec = pl.BlockSpec((slabs, tile, view[1]), lambda i: (0, i, 0)) if slabs else spec
    g_view = g.reshape((slabs,) + view) if slabs else g.reshape(view)
    outs = pl.pallas_call(body, name=name, grid=(rows // tile,), in_specs=[spec, g_spec, spec, spec], out_specs=[spec] * 4,
                          out_shape=[jax.ShapeDtypeStruct(view, F32)] * 4,
                          compiler_params=_params(("parallel",)))(w.reshape(view), g_view, m.reshape(view), v.reshape(view))
    return [o.reshape(shape) for o in outs]


def _sum_slabs(name, x):
    def body(x_ref, o_ref):
        acc = x_ref[0]
        for d in range(1, N_DEV):
            acc = acc + x_ref[d]
        o_ref[...] = acc
    return pl.pallas_call(body, name=name, out_shape=jax.ShapeDtypeStruct(x.shape[1:], x.dtype))(x)


MESH_ID = pl.DeviceIdType.MESH
ANY_SPEC = pl.BlockSpec(memory_space=pl.ANY)


def _remote(src, dst, send_sems, recv_sems, k, to):
    return pltpu.make_async_remote_copy(src_ref=src, dst_ref=dst, send_sem=send_sems.at[k], recv_sem=recv_sems.at[k],
                                        device_id=to, device_id_type=MESH_ID)


class _Rider:
    def __init__(self, inputs, out_shape, sems, start, finish):
        self.inputs, self.out_shape, self.sems, self.start, self.finish = list(inputs), list(out_shape), list(sems), start, finish


def _run_rider(name, rider):
    n_in, n_out = len(rider.inputs), len(rider.out_shape)

    def body(*refs):
        parts = refs[:n_in], refs[n_in:n_in + n_out], refs[n_in + n_out:]
        rider.start(*parts)
        rider.finish(*parts)

    return pl.pallas_call(body, name=name, in_specs=[ANY_SPEC] * n_in, out_specs=[ANY_SPEC] * n_out, out_shape=rider.out_shape,
                          scratch_shapes=rider.sems, compiler_params=pltpu.CompilerParams(has_side_effects=True))(*rider.inputs)


def _gather_rider(blocks):
    n = len(blocks)

    def plan(srcs, dsts, sems, waits):
        send_sems, recv_sems, local_sems = sems
        x, y, c = lax.axis_index("x"), lax.axis_index("y"), lax.axis_index("c")
        sibling = (x, y, 1 - c)
        chips = [(1 - x, y), (x, 1 - y), (1 - x, 1 - y)]
        index = lambda px, py, pc: 4 * px + 2 * py + pc

        def copy(t, k, block, to, src=None):
            slot = dsts[t].at[block]
            return _remote(slot if src is None else src, slot, send_sems, recv_sems, 7 * t + k, to)

        me = index(x, y, c)
        own = [pltpu.make_async_copy(srcs[t], dsts[t].at[me], local_sems.at[t]) for t in range(n)]
        first = []
        for t in range(n):
            first.append(copy(t, 0, me, sibling, src=srcs[t]))
            first += [copy(t, 1 + j, me, (*chip, c), src=srcs[t]) for j, chip in enumerate(chips)]
        if not waits:
            return own, first, [], []
        relays = [(copy(t, 1 + j, index(*chip, c), sibling), copy(t, 4 + j, index(*chip, c), sibling))
                  for j, chip in enumerate(chips) for t in range(n)]
        from_sibling = [copy(t, 0, index(x, y, 1 - c), sibling) for t in range(n)]
        from_sibling += [copy(t, 4 + j, index(*chip, 1 - c), sibling) for t in range(n) for j, chip in enumerate(chips)]
        return own, first, relays, from_sibling

    def start(srcs, dsts, sems):
        own, first, _, _ = plan(srcs, dsts, sems, False)
        for cp in own + first:
            cp.start()

    def finish(srcs, dsts, sems):
        own, first, relays, from_sibling = plan(srcs, dsts, sems, True)
        for arrival, forward in relays:
            arrival.wait_recv()
            forward.start()
        for cp in from_sibling:
            cp.wait_recv()
        for cp in first + [forward for _, forward in relays]:
            cp.wait_send()
        for cp in own:
            cp.wait()

    return _Rider(blocks, [jax.ShapeDtypeStruct((N_DEV,) + b.shape, b.dtype) for b in blocks],
                  [pltpu.SemaphoreType.DMA((7 * n,)), pltpu.SemaphoreType.DMA((7 * n,)), pltpu.SemaphoreType.DMA((n,))],
                  start, finish)


def _pair_exchange(name, blocks):
    n = len(blocks)

    def body(*refs):
        srcs, dsts = refs[:n], refs[n:2 * n]
        send_sems, recv_sems = refs[2 * n:]
        sibling = (lax.axis_index("x"), lax.axis_index("y"), 1 - lax.axis_index("c"))
        copies = [_remote(srcs[t], dsts[t], send_sems, recv_sems, t, sibling) for t in range(n)]
        for cp in copies:
            cp.start()
        for cp in copies:
            cp.wait()

    return pl.pallas_call(
        body, name=name, in_specs=[ANY_SPEC] * n, out_specs=[ANY_SPEC] * n,
        out_shape=[jax.ShapeDtypeStruct(b.shape, b.dtype) for b in blocks],
        scratch_shapes=[pltpu.SemaphoreType.DMA((n,)), pltpu.SemaphoreType.DMA((n,))],
        compiler_params=pltpu.CompilerParams(has_side_effects=True))(*blocks)


def _chip_rider(items):
    n = len(items)
    out_shape = [jax.ShapeDtypeStruct((4,) + tuple(a.shape[1:] if per_chip else a.shape), a.dtype) for a, per_chip in items]

    def plan(srcs, dsts, sems, waits):
        send_sems, recv_sems, local_sems = sems
        x, y, c = lax.axis_index("x"), lax.axis_index("y"), lax.axis_index("c")
        mine = 2 * x + y
        own = [pltpu.make_async_copy(srcs[t].at[mine] if per_chip else srcs[t], dsts[t].at[mine], local_sems.at[t])
               for t, (_, per_chip) in enumerate(items)]
        sends, arrivals = [], []
        for k in range(1, 4):
            px, py = x ^ (k >> 1), y ^ (k & 1)
            peer = 2 * px + py
            for t, (_, per_chip) in enumerate(items):
                src = srcs[t].at[peer] if per_chip else srcs[t]
                sends.append(_remote(src, dsts[t].at[mine], send_sems, recv_sems, 3 * t + k - 1, (px, py, c)))
                if waits:
                    arrivals.append(_remote(src, dsts[t].at[peer], send_sems, recv_sems, 3 * t + k - 1, (px, py, c)))
        return own, sends, arrivals

    def start(srcs, dsts, sems):
        own, sends, _ = plan(srcs, dsts, sems, False)
        for cp in own + sends:
            cp.start()

    def finish(srcs, dsts, sems):
        own, sends, arrivals = plan(srcs, dsts, sems, True)
        for out, arrival in zip(sends, arrivals):
            out.wait_send()
            arrival.wait_recv()
        for cp in own:
            cp.wait()

    return _Rider([a for a, _ in items], out_shape,
                  [pltpu.SemaphoreType.DMA((3 * n,)), pltpu.SemaphoreType.DMA((3 * n,)), pltpu.SemaphoreType.DMA((n,))],
                  start, finish)


def _band_diag(p):
    a, b = p.shape[1:]
    eye = jnp.eye(8, dtype=p.dtype)
    return (p.reshape(4, 8, a, 1, b) * eye[None, :, None, :, None]).reshape(4, 8 * a, 8 * b)


def _band_diag_extract(m, a, b):
    eye = jnp.eye(8, dtype=m.dtype)
    return (m.reshape(4, 8, a, 8, b) * eye[None, :, None, :, None]).sum(axis=3).reshape(32, a, b)


def _proj_cols(w_ref_layout):
    parts = []
    for name in PROJ_ORDER:
        off, width = REF_SPLIT[name]
        part = w_ref_layout[:, off:off + width]
        if width < PROJ[name][1]:
            part = jnp.pad(part, ((0, 0), (0, PROJ[name][1] - width)))
        parts.append(part)
    return jnp.concatenate(parts, axis=1)


def _unproj_cols(w_proj_layout):
    parts = []
    for name in REF_ORDER:
        parts.append(w_proj_layout[:, PROJ[name][0]:PROJ[name][0] + REF_SPLIT[name][1]])
    return jnp.concatenate(parts, axis=1)


def _uq_cols(w):
    w3 = w.reshape(w.shape[0], MLA_HEADS, 96)
    return jnp.concatenate([w3[:, :, :64].reshape(-1, 512), w3[:, :, 64:].reshape(-1, 256)], axis=1)


def _un_uq_cols(w):
    return jnp.concatenate([w[:, :512].reshape(-1, MLA_HEADS, 64), w[:, 512:].reshape(-1, MLA_HEADS, 32)], axis=2).reshape(-1, 768)


def _ukv_cols(w):
    w3 = w.reshape(w.shape[0], MLA_HEADS, 128)
    return jnp.concatenate([w3[:, :, :64].reshape(-1, 512), w3[:, :, 64:].reshape(-1, 512)], axis=1)


def _un_ukv_cols(w):
    return jnp.concatenate([w[:, :512].reshape(-1, MLA_HEADS, 64), w[:, 512:].reshape(-1, MLA_HEADS, 64)], axis=2).reshape(-1, 1024)


def _cols_gathered(g):
    return g.transpose(1, 0, 2).reshape(g.shape[1], -1)


def _cols_slabs(w):
    r, c = w.shape
    return w.reshape(r, N_DEV, c // N_DEV).transpose(1, 0, 2)


def _row(v):
    return v.reshape(1, -1)


def _s5_weights(p):
    flat = dict(lam_re=p["s5_lam_re"].reshape(S5_CH, 1), lam_im=p["s5_lam_im"].reshape(S5_CH, 1),
                log_dt=jnp.repeat(p["s5_log_dt"], S5_STATE).reshape(S5_CH, 1),
                b_re=p["s5_b_re"].reshape(S5_CH, S5_GROUP), b_im=p["s5_b_im"].reshape(S5_CH, S5_GROUP))
    ab_re, ab_im, bb_re, bb_im = _s5_param_fwd(flat["lam_re"], flat["lam_im"], flat["log_dt"], flat["b_re"], flat["b_im"])
    w_bu = jnp.concatenate([_band_diag(bb.reshape(32, 64, 16).transpose(0, 2, 1)) for bb in (bb_re, bb_im)], axis=0)
    w_y = jnp.concatenate([_band_diag(p["s5_c_re"].transpose(0, 2, 1)), -_band_diag(p["s5_c_im"].transpose(0, 2, 1))], axis=0)
    return dict(flat=flat,
                a_re=jnp.broadcast_to(ab_re.reshape(1, S5_CH), (SCAN_SEG, S5_CH)),
                a_im=jnp.broadcast_to(ab_im.reshape(1, S5_CH), (SCAN_SEG, S5_CH)),
                w_bu=w_bu.astype(BF16), w_bu_t=w_bu.swapaxes(1, 2).astype(BF16),
                w_y=w_y.astype(BF16), w_y_t=w_y.swapaxes(1, 2).astype(BF16))


def _layer_fwd(x, w, p, tabs, late_shards=None, rider=None):
    t_len = x.shape[0]
    cos64, sin64, cos32, sin32 = tabs
    s = {}
    (s["h1"],) = _rowwise("pre_mix", lambda xt, g: (_rms(xt, g),), [x], [_row(p["g_pre_mix"])], [((t_len, D_MODEL), BF16)])
    n_mix = PROJ["gates"][0]
    if late_shards is None:
        proj = _mm("proj", s["h1"], w["w_in"][:, :n_mix], "nn")
    else:
        proj, late = _mm("proj_gather", s["h1"], w["w_in"][:, :n_mix], "nn", rider=_gather_rider(late_shards))
        w.update(_late_weights(late))
    s["proj"] = proj
    s["gates"] = _mm("proj_gates", s["h1"], w["w_in"][:, n_mix:], "nn", out_dtypes=(BF16,))
    s["qa"], s["ka"], s["va"] = _swa_prep(proj, cos64, sin64)
    s["sinks"] = jnp.broadcast_to(p["swa_sinks"][:, None], (8, 128))
    y_a = _swa_fwd(s["qa"], s["ka"], s["va"], s["sinks"])
    s5 = s["s5"] = _s5_weights(p)
    bu, s["u_p"] = _s5_expand("s5_bu", proj, PROJ["su"][0] // 128, s5["w_bu"])
    s["xs"], s["cin"] = _s5_scan("s5_scan_fwd", bu, s5["a_re"], s5["a_im"])
    s["y_s"] = _s5_contract("s5_y", s["xs"], s5["w_y"])
    (s["z"],) = _rowwise("s5_gelu", lambda y, u, d: (jax.nn.gelu(y + d * u),), [s["y_s"], _seg(proj, "su")],
                         [_row(p["s5_d"])], [((t_len, 512), BF16)])
    s["zz"] = _mm("s5_glu_mm", s["z"], w["w_glu"], "nn")
    (y_b,) = _rowwise("s5_glu", lambda t: (t[:, :512] * jax.nn.sigmoid(t[:, 512:]),), [s["zz"]], [], [((t_len, 512), BF16)])
    s["qc"], s["kc"] = _ret_prep(proj, cos64, sin64)
    y_c, s["prevs"] = _ret_fwd(s["qc"], s["kc"], proj)
    s["cqn"], s["ckvn"] = _rowwise("mla_norm", lambda cq, ckv, gq, gkv: (_rms(cq, gq), _rms(ckv, gkv)),
                                   [_seg(proj, "cq"), _seg(proj, "ckv")], [_row(p["mla_g_q"]), _row(p["mla_g_kv"])],
                                   [((t_len, 256), BF16), ((t_len, 128), BF16)])
    qf = _mm("mla_uq", s["cqn"], w["w_uq"], "nn")
    kvf = _mm("mla_ukv", s["ckvn"], w["w_ukv"], "nn")
    s["Q"], s["K"], s["V"] = _mla_prep(qf, kvf, proj, cos32, sin32)
    y_d, s["lse"], ridden = _mla_fwd(s["Q"], s["K"], s["V"], rider)
    s["ys"] = [y_a, y_b, y_c, y_d]
    s["merged"] = _merge_fwd(s["ys"], w["wb"], s["gates"])
    s["o"] = _mm("out_mm", s["merged"], w["w_out"], "nn")

    def post_mix(xt, ot, g1, g2):
        x1 = xt + _rms(ot, g1)
        return x1, _rms(x1, g2)
    s["x1"], s["h2"] = _rowwise("post_mix", post_mix, [x, s["o"]], [_row(p["g_post_mix"]), _row(p["g_pre_mlp"])],
                                [((t_len, D_MODEL), F32), ((t_len, D_MODEL), BF16)])
    s["a"], s["r"] = _mm("ff1", s["h2"], w["w_ff1"], "nn", out_dtypes=(F32, BF16),
                         epi=lambda acc: (acc, jnp.square(jnp.maximum(acc, 0.0))))
    s["f"] = _mm("ff2", s["r"], w["w_ff2"], "nn")
    (x2,) = _rowwise("post_mlp", lambda x1, f, g: (x1 + _rms(f, g),), [s["x1"], s["f"]], [_row(p["g_post_mlp"])],
                     [((t_len, D_MODEL), F32)])
    return x2, s, ridden


def _layer_bwd(x, dx2, w, p, s, tabs, pending=None, exchange_name=None):
    t_len = x.shape[0]
    cos64, sin64, cos32, sin32 = tabs
    proj = s["proj"]
    big, small = {}, {}

    def post_mlp_bwd(f, d, g):
        df, dg = _rms_bwd(f, g, d)
        return (df,), (dg,)
    df, small["g_post_mlp"] = _rowwise("post_mlp_bwd", post_mlp_bwd, [s["f"], dx2], [_row(p["g_post_mlp"])],
                                       [((t_len, D_MODEL), BF16)], [((1, D_MODEL), F32)])
    da = _mm("ff2_dx", df, w["w_ff2"], "nt", out_dtypes=(BF16,), epi=lambda acc, a: (acc * 2.0 * jnp.maximum(a, 0.0),),
             epi_in=(s["a"],))
    big["w_ff2"] = _mm("ff2_dw", s["r"], df, "tn")
    dh2 = _mm("ff1_dx", da, w["w_ff1"], "nt")
    big["w_ff1"] = _mm("ff1_dw", s["h2"], da, "tn")

    def post_mix_bwd(x1, o, dh, d2, g1, g2):
        dx1_n, dg2 = _rms_bwd(x1, g2, dh)
        dx1 = d2 + dx1_n
        do, dg1 = _rms_bwd(o, g1, dx1)
        return (dx1, do), (dg1, dg2)
    dx1, do, small["g_post_mix"], small["g_pre_mlp"] = _rowwise(
        "post_mix_bwd", post_mix_bwd, [s["x1"], s["o"], dh2, dx2], [_row(p["g_post_mix"]), _row(p["g_pre_mlp"])],
        [((t_len, D_MODEL), F32), ((t_len, D_MODEL), BF16)], [((1, D_MODEL), F32), ((1, D_MODEL), F32)])
    dmerged = _mm("out_dx", do, w["w_out"], "nt")
    big["w_out"] = _mm("out_dw", s["merged"], do, "tn")
    rider, n_pending = None, len(pending or ())
    if exchange_name is not None:
        late_sums, _ = _pair_stage(exchange_name, _grad_slabs(big, LATE))
        rider = _chip_rider([(part, True) for part in list(pending or ()) + late_sums])

    res = _merge_bwd(s["ys"], w["wb"], s["gates"], dmerged)
    dgl, dts = res[:4], res[4:]
    dys = [_mm("branch_dx", dts[b], w["wb"][b], "nt") for b in range(4)]
    big["wb"] = [_mm("branch_dw", s["ys"][b], dts[b], "tn") for b in range(4)]
    seg = {}

    dqa, dka, dva, dka_prev, dva_prev, dsink = _swa_bwd(s["qa"], s["ka"], s["va"], s["sinks"], dys[0])
    seg["sq"], seg["sk"], seg["sv"] = _swa_unprep(dqa, dka, dva, dka_prev, dva_prev, cos64, sin64)
    small["swa_sinks"] = dsink[:, 0]

    def glu_bwd(t, d):
        za, sg = t[:, :512], jax.nn.sigmoid(t[:, 512:])
        return (jnp.concatenate([d * sg, d * za * sg * (1.0 - sg)], axis=1),)
    (dzz,) = _rowwise("s5_glu_bwd", glu_bwd, [s["zz"], dys[1]], [], [((t_len, 1024), BF16)])
    dz = _mm("s5_glu_dx", dzz, w["w_glu"], "nt")
    big["w_glu"] = _mm("s5_glu_dw", s["z"], dzz, "tn")

    def gelu_bwd(y, u, d, skip):
        _, vjp = jax.vjp(jax.nn.gelu, y + skip * u)
        (dy2,) = vjp(d)
        return (dy2, dy2 * skip), (jnp.sum(dy2 * u, axis=0, keepdims=True),)
    dy_s, du_skip, d_skip = _rowwise("s5_gelu_bwd", gelu_bwd, [s["y_s"], _seg(proj, "su"), dz], [_row(p["s5_d"])],
                                     [((t_len, 512), F32), ((t_len, 512), F32)], [((1, 512), F32)])
    small["s5_d"] = d_skip[0]
    s5 = s["s5"]
    dlam_in, dy_p = _s5_expand("s5_dx_in", dy_s, 0, s5["w_y_t"])
    lam, da_re, da_im = _s5_scan("s5_scan_bwd", dlam_in, s5["a_re"], s5["a_im"], states=s["xs"], carry_in=s["cin"])
    du_s5 = _s5_contract("s5_du", lam, s5["w_bu_t"])
    d_w_bu = _s5_band_grad("s5_dbu", s["u_p"], lam, True)
    d_w_y = _s5_band_grad("s5_dc", dy_p, s["xs"], False)
    (seg["su"],) = _rowwise("s5_du_sum", lambda a, b: (a + b,), [du_s5, du_skip], [], [((t_len, 512), BF16)])
    d_bb_re = _band_diag_extract(d_w_bu[:4], 16, 64).transpose(0, 2, 1).reshape(S5_CH, S5_GROUP)
    d_bb_im = _band_diag_extract(d_w_bu[4:], 16, 64).transpose(0, 2, 1).reshape(S5_CH, S5_GROUP)
    small["s5_c_re"] = _band_diag_extract(d_w_y[:4], 64, 16).transpose(0, 2, 1)
    small["s5_c_im"] = -_band_diag_extract(d_w_y[4:], 64, 16).transpose(0, 2, 1)
    fl = s5["flat"]
    group_sum = jnp.repeat(jnp.eye(S5_GROUPS, dtype=F32), S5_STATE, axis=1)
    d_lr, d_li, d_ldt, d_br, d_bi = _s5_param_bwd(fl["lam_re"], fl["lam_im"], fl["log_dt"], fl["b_re"], fl["b_im"], group_sum,
                                                  da_re.reshape(S5_CH, 1), da_im.reshape(S5_CH, 1), d_bb_re, d_bb_im)
    small["s5_lam_re"], small["s5_lam_im"] = d_lr.reshape(32, 64), d_li.reshape(32, 64)
    small["s5_log_dt"] = d_ldt[:, 0]
    small["s5_b_re"], small["s5_b_im"] = d_br.reshape(32, 64, 16), d_bi.reshape(32, 64, 16)

    dqc, dkc, seg["rv"], seg["rg"] = _ret_bwd(s["qc"], s["kc"], proj, s["prevs"], dys[2])
    seg["rq"], seg["rk"] = _ret_unprep(dqc, dkc, cos64, sin64)

    do_h, delta = _mla_bwd_prep(s["ys"][3], dys[3])
    d_q, d_k, d_v, ridden = _mla_bwd(s["Q"], s["K"], s["V"], do_h, s["lse"], delta, rider)
    dqf, dkvf, seg["kr"] = _mla_unprep(d_q, d_k, d_v, cos32, sin32)
    dcqn = _mm("mla_uq_dx", dqf, w["w_uq"], "nt")
    big["w_uq"] = _mm("mla_uq_dw", s["cqn"], dqf, "tn")
    dckvn = _mm("mla_ukv_dx", dkvf, w["w_ukv"], "nt")
    big["w_ukv"] = _mm("mla_ukv_dw", s["ckvn"], dkvf, "tn")

    def mla_norm_bwd(cq, ckv, d1, d2, gq, gkv):
        dcq, dgq = _rms_bwd(cq, gq, d1)
        dckv, dgkv = _rms_bwd(ckv, gkv, d2)
        return (dcq, dckv), (dgq, dgkv)
    seg["cq"], seg["ckv"], dgq, dgkv = _rowwise(
        "mla_norm_bwd", mla_norm_bwd, [_seg(proj, "cq"), _seg(proj, "ckv"), dcqn, dckvn],
        [_row(p["mla_g_q"]), _row(p["mla_g_kv"])], [((t_len, 256), BF16), ((t_len, 128), BF16)],
        [((1, 256), F32), ((1, 128), F32)])
    small["mla_g_q"], small["mla_g_kv"] = dgq[0], dgkv[0]

    dproj = jnp.concatenate([seg[name] for name in PROJ_ORDER[:-1]] + list(dgl), axis=1)
    dh1 = _mm("proj_dx", dproj, w["w_in"], "nt")
    big["w_in"] = _mm("proj_dw", s["h1"], dproj, "tn")

    def pre_mix_bwd(xt, dh, d1, g):
        dxn, dg = _rms_bwd(xt, g, dh)
        return (d1 + dxn,), (dg,)
    dx, small["g_pre_mix"] = _rowwise("pre_mix_bwd", pre_mix_bwd, [x, dh1, dx1], [_row(p["g_pre_mix"])],
                                      [((t_len, D_MODEL), F32)], [((1, D_MODEL), F32)])
    for name in ("g_pre_mix", "g_post_mix", "g_pre_mlp", "g_post_mlp"):
        small[name] = small[name][0]
    return dx, big, small, ridden[:n_pending], ridden[n_pending:]


BIG = ("w_in", "s5_w_glu", "mla_w_uq", "mla_w_ukv", "w_branch", "w_out", "w_ff1", "w_ff2")
SMALL = ("g_pre_mix", "g_post_mix", "g_pre_mlp", "g_post_mlp", "swa_sinks", "s5_lam_re", "s5_lam_im", "s5_log_dt",
         "s5_b_re", "s5_b_im", "s5_c_re", "s5_c_im", "s5_d", "mla_g_q", "mla_g_kv")
WEIGHTS = ("g_pre_mix", "g_post_mix", "g_pre_mlp", "g_post_mlp", "w_in", "swa_sinks", "s5_lam_re", "s5_lam_im", "s5_log_dt",
           "s5_b_re", "s5_b_im", "s5_c_re", "s5_c_im", "s5_d", "s5_w_glu", "mla_g_q", "mla_g_kv", "mla_w_uq", "mla_w_ukv",
           "w_branch", "w_out", "w_ff1", "w_ff2")


EARLY = ("w_in", "s5_w_glu", "mla_w_uq", "mla_w_ukv", "w_branch")
LATE = ("w_out", "w_ff1", "w_ff2")


def _early_weights(shards):
    g = dict(zip(EARLY, shards))
    wb = g["w_branch"].transpose(1, 2, 0, 3).reshape(4, 512, D_MODEL)
    return dict(w_in=_proj_cols(_cols_gathered(g["w_in"])), w_glu=_cols_gathered(g["s5_w_glu"]),
                w_uq=_uq_cols(_cols_gathered(g["mla_w_uq"])), w_ukv=_ukv_cols(_cols_gathered(g["mla_w_ukv"])),
                wb=[wb[b] for b in range(4)])


def _late_weights(shards):
    g = dict(zip(LATE, shards))
    return dict(w_out=g["w_out"].reshape(D_MODEL, D_MODEL), w_ff1=_cols_gathered(g["w_ff1"]),
                w_ff2=g["w_ff2"].reshape(4 * D_MODEL, D_MODEL))


def _grad_slabs(big, names):
    make = dict(w_in=lambda: _cols_slabs(_unproj_cols(big["w_in"])), s5_w_glu=lambda: _cols_slabs(big["w_glu"]),
                mla_w_uq=lambda: _cols_slabs(_un_uq_cols(big["w_uq"])), mla_w_ukv=lambda: _cols_slabs(_un_ukv_cols(big["w_ukv"])),
                w_branch=lambda: jnp.stack(big["wb"]).reshape(4, 512, N_DEV, D_MODEL // N_DEV).transpose(2, 0, 1, 3),
                w_out=lambda: big["w_out"].reshape(N_DEV, D_MODEL // N_DEV, D_MODEL),
                w_ff1=lambda: _cols_slabs(big["w_ff1"]), w_ff2=lambda: big["w_ff2"].reshape(N_DEV, 4 * D_MODEL // N_DEV, D_MODEL))
    return [make[name]() for name in names]


def _pair_stage(name, slabs, extra=()):
    core = lax.axis_index("c")
    own, send = [], []
    for slab in slabs:
        by_chip = slab.reshape((4, 2) + slab.shape[1:])
        own.append(lax.dynamic_index_in_dim(by_chip, core, axis=1, keepdims=False))
        send.append(lax.dynamic_index_in_dim(by_chip, 1 - core, axis=1, keepdims=False).astype(BF16))
    got = _pair_exchange(name, send + list(extra))
    sums = []
    for mine, theirs in zip(own, got):
        flat = (-1, mine.shape[-1])
        (part,) = _rowwise("grads_pair_sum", lambda a, b: (a + b.astype(F32),), [mine.reshape(flat), theirs.reshape(flat)],
                           [], [((int(np.prod(mine.shape[:-1])), mine.shape[-1]), BF16)])
        sums.append(part.reshape(mine.shape))
    return sums, got[len(slabs):]


def kernel(x, g_pre_mix, g_post_mix, g_pre_mlp, g_post_mlp, w_in, swa_sinks, s5_lam_re, s5_lam_im, s5_log_dt, s5_b_re, s5_b_im, s5_c_re, s5_c_im, s5_d, s5_w_glu, mla_g_q, mla_g_kv, mla_w_uq, mla_w_ukv, w_branch, w_out, w_ff1, w_ff2, loss_target, m_g_pre_mix, m_g_post_mix, m_g_pre_mlp, m_g_post_mlp, m_w_in, m_swa_sinks, m_s5_lam_re, m_s5_lam_im, m_s5_log_dt, m_s5_b_re, m_s5_b_im, m_s5_c_re, m_s5_c_im, m_s5_d, m_s5_w_glu, m_mla_g_q, m_mla_g_kv, m_mla_w_uq, m_mla_w_ukv, m_w_branch, m_w_out, m_w_ff1, m_w_ff2, v_g_pre_mix, v_g_post_mix, v_g_pre_mlp, v_g_post_mlp, v_w_in, v_swa_sinks, v_s5_lam_re, v_s5_lam_im, v_s5_log_dt, v_s5_b_re, v_s5_b_im, v_s5_c_re, v_s5_c_im, v_s5_d, v_s5_w_glu, v_mla_g_q, v_mla_g_kv, v_mla_w_uq, v_mla_w_ukv, v_w_branch, v_w_out, v_w_ff1, v_w_ff2):
    weights = dict(g_pre_mix=g_pre_mix, g_post_mix=g_post_mix, g_pre_mlp=g_pre_mlp, g_post_mlp=g_post_mlp, w_in=w_in,
                   swa_sinks=swa_sinks, s5_lam_re=s5_lam_re, s5_lam_im=s5_lam_im, s5_log_dt=s5_log_dt, s5_b_re=s5_b_re,
                   s5_b_im=s5_b_im, s5_c_re=s5_c_re, s5_c_im=s5_c_im, s5_d=s5_d, s5_w_glu=s5_w_glu, mla_g_q=mla_g_q,
                   mla_g_kv=mla_g_kv, mla_w_uq=mla_w_uq, mla_w_ukv=mla_w_ukv, w_branch=w_branch, w_out=w_out, w_ff1=w_ff1,
                   w_ff2=w_ff2)
    m_in = dict(g_pre_mix=m_g_pre_mix, g_post_mix=m_g_post_mix, g_pre_mlp=m_g_pre_mlp, g_post_mlp=m_g_post_mlp, w_in=m_w_in,
                swa_sinks=m_swa_sinks, s5_lam_re=m_s5_lam_re, s5_lam_im=m_s5_lam_im, s5_log_dt=m_s5_log_dt,
                s5_b_re=m_s5_b_re, s5_b_im=m_s5_b_im, s5_c_re=m_s5_c_re, s5_c_im=m_s5_c_im, s5_d=m_s5_d,
                s5_w_glu=m_s5_w_glu, mla_g_q=m_mla_g_q, mla_g_kv=m_mla_g_kv, mla_w_uq=m_mla_w_uq, mla_w_ukv=m_mla_w_ukv,
                w_branch=m_w_branch, w_out=m_w_out, w_ff1=m_w_ff1, w_ff2=m_w_ff2)
    v_in = dict(g_pre_mix=v_g_pre_mix, g_post_mix=v_g_post_mix, g_pre_mlp=v_g_pre_mlp, g_post_mlp=v_g_post_mlp, w_in=v_w_in,
                swa_sinks=v_swa_sinks, s5_lam_re=v_s5_lam_re, s5_lam_im=v_s5_lam_im, s5_log_dt=v_s5_log_dt,
                s5_b_re=v_s5_b_re, s5_b_im=v_s5_b_im, s5_c_re=v_s5_c_re, s5_c_im=v_s5_c_im, s5_d=v_s5_d,
                s5_w_glu=v_s5_w_glu, mla_g_q=v_mla_g_q, mla_g_kv=v_mla_g_kv, mla_w_uq=v_mla_w_uq, mla_w_ukv=v_mla_w_ukv,
                w_branch=v_w_branch, w_out=v_w_out, w_ff1=v_w_ff1, w_ff2=v_w_ff2)
    depth = g_pre_mix.shape[0]
    t_len = x.shape[1]
    tabs = _rope_tables(t_len, 32) + _rope_tables(t_len, 16)

    def shards(l, names):
        return [weights[name][l].astype(BF16) for name in names]
    early = _run_rider("gather_weights", _gather_rider(shards(0, EARLY)))

    xs, saved, layer_w, layer_p = [x[0]], [], [], []
    for l in range(depth):
        layer_w.append(_early_weights(early))
        layer_p.append({name: weights[name][l] for name in SMALL})
        x_next, s, early = _layer_fwd(xs[-1], layer_w[l], layer_p[l], tabs, shards(l, LATE),
                                      _gather_rider(shards(l + 1, EARLY)) if l + 1 < depth else None)
        xs.append(x_next)
        saved.append(s)

    def loss_fn(y, tgt):
        err = y - tgt
        part = 0.5 * jnp.sum(jnp.mean(err * err, axis=-1, keepdims=True), axis=0, keepdims=True)
        return (err * (1.0 / D_MODEL),), (jnp.broadcast_to(part, (8, 128)),)
    dx, loss_part = _rowwise("loss", loss_fn, [xs[-1], loss_target[0]], [], [((t_len, D_MODEL), F32)], [((8, 128), F32)])
    loss = lax.psum(loss_part[0, 0], ("x", "y", "c"))

    small_grads, arrived, pending = [None] * depth, [{} for _ in range(depth)], None
    for l in reversed(range(depth)):
        dx, big, small_grads[l], got_pending, got_late = _layer_bwd(xs[l], dx, layer_w[l], layer_p[l], saved[l], tabs, pending,
                                                                    "late_grads_to_sibling")
        if pending is not None:
            arrived[l + 1].update(zip(EARLY, got_pending))
        arrived[l].update(zip(LATE, got_late))
        if l > 0:
            pending, _ = _pair_stage("early_grads_to_sibling", _grad_slabs(big, EARLY))
    small_vec = jnp.concatenate([jnp.stack([small_grads[l][name] for l in range(depth)]).reshape(-1) for name in SMALL])
    n_small = small_vec.shape[0]
    rows = -(-n_small // 1024) * 8
    small_mat = jnp.pad(small_vec, (0, rows * 128 - n_small)).reshape(rows, 128)
    sums, (sibling_small,) = _pair_stage("last_grads_to_sibling", _grad_slabs(big, EARLY), [small_mat])
    small_pair = jnp.where(lax.axis_index("c") == 0, jnp.stack([small_mat, sibling_small]), jnp.stack([sibling_small, small_mat]))
    recv = _run_rider("grads_to_chips", _chip_rider([(part, True) for part in sums] + [(small_pair, False)]))
    arrived[0].update(zip(EARLY, recv[:-1]))
    small_sum = _sum_slabs("sum_small_grads", recv[-1].reshape((N_DEV,) + small_mat.shape)).reshape(-1)

    grads, deltas, new_m, new_v = {}, {}, {}, {}
    for name in BIG:
        slabs = jnp.stack([arrived[l][name] for l in range(depth)], axis=1)
        grads[name], deltas[name], new_m[name], new_v[name] = _adamw("adamw_" + name, weights[name], slabs, m_in[name],
                                                                     v_in[name], 4)
    off = 0
    for name in SMALL:
        size = int(np.prod(weights[name].shape))
        g = small_sum[off:off + size].reshape(weights[name].shape)
        off += size
        grads[name], deltas[name], new_m[name], new_v[name] = _adamw("adamw_" + name, weights[name], g, m_in[name],
                                                                     v_in[name], False)
    return (loss, dx[None], *[grads[n] for n in WEIGHTS], *[deltas[n] for n in WEIGHTS], *[new_m[n] for n in WEIGHTS],
            *[new_v[n] for n in WEIGHTS])
```
